```python
import jax, jax.numpy as jnp
from jax import lax
import numpy as np

D_MODEL = 1024
BATCH = 8
SEQ = 8192
DEPTH = 4

A_HEADS = 4
A_HEAD_DIM = 64
A_WIDTH = A_HEADS * A_HEAD_DIM
B_HEADS = 4
B_HEAD_DIM = 128
B_WIDTH = B_HEADS * B_HEAD_DIM
C_BLOCKS = 4
C_WIDTH = 256
C_BLOCK_DIM = C_WIDTH // C_BLOCKS
D_MIX = A_WIDTH + B_WIDTH + C_WIDTH
IN_SPLITS = (A_WIDTH, A_WIDTH, A_WIDTH, A_WIDTH,
             B_WIDTH, B_WIDTH, B_WIDTH, B_WIDTH, B_HEADS, B_HEADS,
             C_WIDTH, C_WIDTH)
D_IN = 4 * A_WIDTH + 4 * B_WIDTH + 2 * B_HEADS + 2 * C_WIDTH
CONV_K = 4
CHUNK = 64
RG_C = 8.0
D_FF = 4 * D_MODEL
EPS = 1e-6
TINY = 1e-30

kernel_name = "hymba_style_hgrn2_gdn_rglru_hybrid"


def rmsnorm(x, g):
    xf = x.astype(jnp.float32)
    y = xf * lax.rsqrt(jnp.mean(xf * xf, axis=-1, keepdims=True) + EPS)
    return (y * g.astype(jnp.float32)).astype(x.dtype)


def l2norm(x):
    return x * lax.rsqrt(jnp.sum(x * x, axis=-1, keepdims=True) + EPS)


def split_cols(t, sizes):
    offsets = np.cumsum(np.array(sizes))[:-1].tolist()
    return jnp.split(t, offsets, axis=-1)


def causal_depthwise_conv(x, w):
    k = w.shape[0]
    s = x.shape[1]
    xp = jnp.pad(x, ((0, 0), (k - 1, 0), (0, 0)))
    y = xp[:, 0:s] * w[0]
    for j in range(1, k):
        y = y + xp[:, j:j + s] * w[j]
    return y


def masked_exp(mask, t):
    return jnp.where(mask, jnp.exp(jnp.where(mask, t, 0.0)), 0.0)


def to_chunks(t):
    b, s = t.shape[:2]
    t = t.reshape(b, s // CHUNK, CHUNK, *t.shape[2:])
    t = jnp.moveaxis(t, 2, 3)
    return jnp.moveaxis(t, 1, 0)


def from_chunks(t):
    n, b, h, c, d = t.shape
    return jnp.transpose(t, (1, 0, 3, 2, 4)).reshape(b, n * c, h, d)


def hgrn2_chunk(q, k, v, log_f):
    bsz, _, h, dk = q.shape
    dv = v.shape[-1]
    qc, kc, vc = to_chunks(q), to_chunks(k), to_chunks(v)
    cum = jnp.cumsum(to_chunks(log_f), axis=-2)
    cum_last = cum[..., -1:, :]
    q_dec = qc * jnp.exp(cum)
    k_dec = kc * jnp.exp(cum_last - cum)
    chunk_dec = jnp.exp(cum_last[..., 0, :])
    causal = jnp.tril(jnp.ones((CHUNK, CHUNK), dtype=bool))[:, :, None]

    def step(state, xs):
        q_, k_, v_, c_, qd, kd, cd = xs
        diff = c_[..., :, None, :] - c_[..., None, :, :]
        dec = masked_exp(causal, diff)
        attn = jnp.einsum('bhtd,bhsd,bhtsd->bhts', q_, k_, dec)
        o = jnp.einsum('bhtd,bhde->bhte', qd, state) + jnp.einsum('bhts,bhse->bhte', attn, v_)
        state = state * cd[..., :, None] + jnp.einsum('bhsd,bhse->bhde', kd, v_)
        return state, o

    s0 = jnp.zeros((bsz, h, dk, dv), jnp.float32)
    _, o = lax.scan(step, s0, (qc, kc, vc, cum, q_dec, k_dec, chunk_dec))
    return from_chunks(o)


def hgrn2_mixer(q_in, f_in, i_in, g_in, lb, norm_g):
    bsz, s, _ = q_in.shape
    shp = (bsz, s, A_HEADS, A_HEAD_DIM)
    q = jax.nn.silu(q_in.astype(jnp.float32)).reshape(shp)
    fp = f_in.astype(jnp.float32).reshape(shp)
    lbh = lb.astype(jnp.float32).reshape(A_HEADS, A_HEAD_DIM)
    f = lbh + (1.0 - lbh) * jax.nn.sigmoid(fp)
    log_f = jnp.log(jnp.maximum(f, TINY))
    k = (1.0 - lbh) * jax.nn.sigmoid(-fp)
    v = i_in.astype(jnp.float32).reshape(shp)
    o = hgrn2_chunk(q, k, v, log_f)
    o = rmsnorm(o, norm_g) * jax.nn.silu(g_in.astype(jnp.float32).reshape(shp))
    return o.reshape(bsz, s, A_WIDTH)


def gated_delta_chunk(q, k, v, log_alpha, beta):
    bsz, _, h, dk = q.shape
    dv = v.shape[-1]
    qc = to_chunks(q) * (dk ** -0.5)
    kc, vc = to_chunks(k), to_chunks(v)
    gc = jnp.cumsum(to_chunks(log_alpha), axis=-1)
    bc = to_chunks(beta)
    causal = jnp.tril(jnp.ones((CHUNK, CHUNK), dtype=bool))
    strict = jnp.tril(jnp.ones((CHUNK, CHUNK), dtype=bool), -1)
    decay = masked_exp(causal, gc[..., :, None] - gc[..., None, :])
    kb = kc * bc[..., None]
    a_mat = jnp.where(strict, jnp.einsum('nbhtd,nbhsd->nbhts', kb, kc) * decay, 0.0)
    u = lax.linalg.triangular_solve(a_mat, vc * bc[..., None], left_side=True, lower=True,
                                    unit_diagonal=True)
    w = lax.linalg.triangular_solve(a_mat, kb * jnp.exp(gc)[..., None], left_side=True,
                                    lower=True, unit_diagonal=True)
    qk = jnp.einsum('nbhtd,nbhsd->nbhts', qc, kc) * decay
    q_dec = qc * jnp.exp(gc)[..., None]
    k_dec = kc * jnp.exp(gc[..., -1:] - gc)[..., None]
    chunk_dec = jnp.exp(gc[..., -1])

    def step(state, xs):
        qd, kd, qk_, u_, w_, cd = xs
        v_new = u_ - jnp.einsum('bhtd,bhde->bhte', w_, state)
        o = jnp.einsum('bhtd,bhde->bhte', qd, state) + jnp.einsum('bhts,bhse->bhte', qk_, v_new)
        state = state * cd[..., None, None] + jnp.einsum('bhsd,bhse->bhde', kd, v_new)
        return state, o

    s0 = jnp.zeros((bsz, h, dk, dv), jnp.float32)
    _, o = lax.scan(step, s0, (q_dec, k_dec, qk, u, w, chunk_dec))
    return from_chunks(o)


def gdn_mixer(q_in, k_in, v_in, z_in, b_in, a_in, conv_w, a_log, dt_bias, norm_g):
    bsz, s, _ = q_in.shape
    shp = (bsz, s, B_HEADS, B_HEAD_DIM)
    qkv = jnp.concatenate([q_in, k_in, v_in], axis=-1).astype(jnp.float32)
    qkv = jax.nn.silu(causal_depthwise_conv(qkv, conv_w.astype(jnp.float32)))
    q, k, v = split_cols(qkv, (B_WIDTH, B_WIDTH, B_WIDTH))
    q = l2norm(q.reshape(shp))
    k = l2norm(k.reshape(shp))
    v = v.reshape(shp)
    beta = jax.nn.sigmoid(b_in.astype(jnp.float32))
    log_alpha = -jnp.exp(a_log.astype(jnp.float32)) * jax.nn.softplus(
        a_in.astype(jnp.float32) + dt_bias.astype(jnp.float32))
    o = gated_delta_chunk(q, k, v, log_alpha, beta)
    o = rmsnorm(o, norm_g) * jax.nn.silu(z_in.astype(jnp.float32).reshape(shp))
    return o.reshape(bsz, s, B_WIDTH)


def rglru(x, w_a, b_a, w_x, b_x, lam):
    bsz, s, _ = x.shape
    xb = x.reshape(bsz, s, C_BLOCKS, C_BLOCK_DIM)
    r = jax.nn.sigmoid(jnp.einsum('bsnd,nde->bsne', xb, w_a.astype(jnp.float32)).reshape(bsz, s, C_WIDTH)
                       + b_a.astype(jnp.float32))
    i = jax.nn.sigmoid(jnp.einsum('bsnd,nde->bsne', xb, w_x.astype(jnp.float32)).reshape(bsz, s, C_WIDTH)
                       + b_x.astype(jnp.float32))
    log_a = -RG_C * r * jax.nn.softplus(-lam.astype(jnp.float32))
    a = jnp.exp(log_a)
    pos = jnp.arange(s)[None, :, None]
    mult = jnp.where(pos == 0, 1.0, jnp.sqrt(jnp.maximum(-jnp.expm1(2.0 * log_a), EPS)))
    bx = mult * i * x

    def combine(c1, c2):
        a1, b1 = c1
        a2, b2 = c2
        return a1 * a2, a2 * b1 + b2

    _, hs = lax.associative_scan(combine, (a, bx), axis=1)
    return hs


def rglru_mixer(x_in, y_in, conv_w, conv_b, w_a, b_a, w_x, b_x, lam):
    xc = causal_depthwise_conv(x_in.astype(jnp.float32), conv_w.astype(jnp.float32)) + conv_b.astype(jnp.float32)
    h = rglru(xc, w_a, b_a, w_x, b_x, lam)
    return jax.nn.gelu(y_in.astype(jnp.float32), approximate=True) * h


def _fwd_setup_inputs(seed: int = 0) -> dict:
    key = jax.random.key(seed)
    ks = jax.random.split(key, 24)
    f32 = jnp.float32
    nrm = jax.random.normal
    uni = jax.random.uniform
    x = nrm(ks[0], (BATCH, SEQ, D_MODEL), f32)
    norm1_g = 1.0 + 0.02 * nrm(ks[1], (DEPTH, D_MODEL), f32)
    w_in = nrm(ks[2], (DEPTH, D_MODEL, D_IN), f32) * D_MODEL ** -0.5
    hgrn_lb_logits = 0.1 * nrm(ks[3], (DEPTH, A_WIDTH), f32)
    hgrn_norm_g = 1.0 + 0.02 * nrm(ks[4], (DEPTH, A_HEAD_DIM), f32)
    gdn_conv_w = nrm(ks[5], (DEPTH, CONV_K, 3 * B_WIDTH), f32) * CONV_K ** -0.5
    gdn_a_log = jnp.log(uni(ks[6], (DEPTH, B_HEADS), f32, 1.0, 16.0))
    dt = jnp.exp(uni(ks[7], (DEPTH, B_HEADS), f32, float(np.log(1e-3)), float(np.log(1e-1))))
    gdn_dt_bias = dt + jnp.log(-jnp.expm1(-dt))
    gdn_norm_g = 1.0 + 0.02 * nrm(ks[8], (DEPTH, B_HEAD_DIM), f32)
    lru_conv_w = nrm(ks[9], (DEPTH, CONV_K, C_WIDTH), f32) * CONV_K ** -0.5
    lru_conv_b = 0.01 * nrm(ks[10], (DEPTH, C_WIDTH), f32)
    lru_w_a = nrm(ks[11], (DEPTH, C_BLOCKS, C_BLOCK_DIM, C_BLOCK_DIM), f32) * C_BLOCK_DIM ** -0.5
    lru_b_a = 0.01 * nrm(ks[12], (DEPTH, C_WIDTH), f32)
    lru_w_x = nrm(ks[13], (DEPTH, C_BLOCKS, C_BLOCK_DIM, C_BLOCK_DIM), f32) * C_BLOCK_DIM ** -0.5
    lru_b_x = 0.01 * nrm(ks[14], (DEPTH, C_WIDTH), f32)
    a_c = uni(ks[15], (DEPTH, C_WIDTH), f32, 0.9, 0.999)
    s_a = a_c ** (1.0 / RG_C)
    lru_lambda = jnp.log(s_a) - jnp.log1p(-s_a)
    w_out = nrm(ks[16], (DEPTH, D_MIX, D_MODEL), f32) * D_MIX ** -0.5
    norm2_g = 1.0 + 0.02 * nrm(ks[17], (DEPTH, D_MODEL), f32)
    w_up = nrm(ks[18], (DEPTH, D_MODEL, D_FF), f32) * D_MODEL ** -0.5
    w_down = nrm(ks[19], (DEPTH, D_FF, D_MODEL), f32) * D_FF ** -0.5
    final_norm_g = 1.0 + 0.02 * nrm(ks[20], (D_MODEL,), f32)
    return {"x": x, "norm1_g": norm1_g, "w_in": w_in, "hgrn_lb_logits": hgrn_lb_logits,
            "hgrn_norm_g": hgrn_norm_g, "gdn_conv_w": gdn_conv_w, "gdn_a_log": gdn_a_log,
            "gdn_dt_bias": gdn_dt_bias, "gdn_norm_g": gdn_norm_g, "lru_conv_w": lru_conv_w,
            "lru_conv_b": lru_conv_b, "lru_w_a": lru_w_a, "lru_b_a": lru_b_a, "lru_w_x": lru_w_x,
            "lru_b_x": lru_b_x, "lru_lambda": lru_lambda, "w_out": w_out, "norm2_g": norm2_g,
            "w_up": w_up, "w_down": w_down, "final_norm_g": final_norm_g}


def _fwd_reference(x, norm1_g, w_in, hgrn_lb_logits, hgrn_norm_g, gdn_conv_w, gdn_a_log, gdn_dt_bias,
              gdn_norm_g, lru_conv_w, lru_conv_b, lru_w_a, lru_b_a, lru_w_x, lru_b_x, lru_lambda,
              w_out, norm2_g, w_up, w_down, final_norm_g):
    p = jax.nn.softmax(hgrn_lb_logits.astype(jnp.float32), axis=0)
    lower_bounds = jnp.clip(jnp.cumsum(p, axis=0) - p[0], 0.0, 1.0 - EPS)
    for l in range(DEPTH):
        h = rmsnorm(x, norm1_g[l])
        proj = h @ w_in[l]
        (aq, af, ai, ag, bq, bk, bv, bz, bb, ba, cx, cy) = split_cols(proj, IN_SPLITS)
        o_a = hgrn2_mixer(aq, af, ai, ag, lower_bounds[l], hgrn_norm_g[l])
        o_b = gdn_mixer(bq, bk, bv, bz, bb, ba, gdn_conv_w[l], gdn_a_log[l], gdn_dt_bias[l], gdn_norm_g[l])
        o_c = rglru_mixer(cx, cy, lru_conv_w[l], lru_conv_b[l], lru_w_a[l], lru_b_a[l],
                          lru_w_x[l], lru_b_x[l], lru_lambda[l])
        mixed = jnp.concatenate([o_a, o_b, o_c], axis=-1).astype(x.dtype)
        x = x + mixed @ w_out[l]
        h2 = rmsnorm(x, norm2_g[l])
        x = x + jnp.square(jax.nn.relu(h2 @ w_up[l])) @ w_down[l]
    return rmsnorm(x, final_norm_g)


import jax as _jax
import jax.numpy as _jnp

TWIN_FORMAT = 'train_step'
FWD_PARAMS = ['x', 'norm1_g', 'w_in', 'hgrn_lb_logits', 'hgrn_norm_g', 'gdn_conv_w', 'gdn_a_log', 'gdn_dt_bias', 'gdn_norm_g', 'lru_conv_w', 'lru_conv_b', 'lru_w_a', 'lru_b_a', 'lru_w_x', 'lru_b_x', 'lru_lambda', 'w_out', 'norm2_g', 'w_up', 'w_down', 'final_norm_g']
TWIN_WEIGHTS = ['norm1_g', 'w_in', 'hgrn_lb_logits', 'hgrn_norm_g', 'gdn_conv_w', 'gdn_a_log', 'gdn_dt_bias', 'gdn_norm_g', 'lru_conv_w', 'lru_conv_b', 'lru_w_a', 'lru_b_a', 'lru_w_x', 'lru_b_x', 'lru_lambda', 'w_out', 'norm2_g', 'w_up', 'w_down', 'final_norm_g']
TWIN_DIFF_INPUT = 'x'
TWIN_INPUTS = ['x', 'norm1_g', 'w_in', 'hgrn_lb_logits', 'hgrn_norm_g', 'gdn_conv_w', 'gdn_a_log', 'gdn_dt_bias', 'gdn_norm_g', 'lru_conv_w', 'lru_conv_b', 'lru_w_a', 'lru_b_a', 'lru_w_x', 'lru_b_x', 'lru_lambda', 'w_out', 'norm2_g', 'w_up', 'w_down', 'final_norm_g', 'loss_target', 'm_norm1_g', 'm_w_in', 'm_hgrn_lb_logits', 'm_hgrn_norm_g', 'm_gdn_conv_w', 'm_gdn_a_log', 'm_gdn_dt_bias', 'm_gdn_norm_g', 'm_lru_conv_w', 'm_lru_conv_b', 'm_lru_w_a', 'm_lru_b_a', 'm_lru_w_x', 'm_lru_b_x', 'm_lru_lambda', 'm_w_out', 'm_norm2_g', 'm_w_up', 'm_w_down', 'm_final_norm_g', 'v_norm1_g', 'v_w_in', 'v_hgrn_lb_logits', 'v_hgrn_norm_g', 'v_gdn_conv_w', 'v_gdn_a_log', 'v_gdn_dt_bias', 'v_gdn_norm_g', 'v_lru_conv_w', 'v_lru_conv_b', 'v_lru_w_a', 'v_lru_b_a', 'v_lru_w_x', 'v_lru_b_x', 'v_lru_lambda', 'v_w_out', 'v_norm2_g', 'v_w_up', 'v_w_down', 'v_final_norm_g']
TWIN_OUTPUTS = ['loss', 'grad_x', 'grad_norm1_g', 'grad_w_in', 'grad_hgrn_lb_logits', 'grad_hgrn_norm_g', 'grad_gdn_conv_w', 'grad_gdn_a_log', 'grad_gdn_dt_bias', 'grad_gdn_norm_g', 'grad_lru_conv_w', 'grad_lru_conv_b', 'grad_lru_w_a', 'grad_lru_b_a', 'grad_lru_w_x', 'grad_lru_b_x', 'grad_lru_lambda', 'grad_w_out', 'grad_norm2_g', 'grad_w_up', 'grad_w_down', 'grad_final_norm_g', 'delta_norm1_g', 'delta_w_in', 'delta_hgrn_lb_logits', 'delta_hgrn_norm_g', 'delta_gdn_conv_w', 'delta_gdn_a_log', 'delta_gdn_dt_bias', 'delta_gdn_norm_g', 'delta_lru_conv_w', 'delta_lru_conv_b', 'delta_lru_w_a', 'delta_lru_b_a', 'delta_lru_w_x', 'delta_lru_b_x', 'delta_lru_lambda', 'delta_w_out', 'delta_norm2_g', 'delta_w_up', 'delta_w_down', 'delta_final_norm_g', 'new_m_norm1_g', 'new_m_w_in', 'new_m_hgrn_lb_logits', 'new_m_hgrn_norm_g', 'new_m_gdn_conv_w', 'new_m_gdn_a_log', 'new_m_gdn_dt_bias', 'new_m_gdn_norm_g', 'new_m_lru_conv_w', 'new_m_lru_conv_b', 'new_m_lru_w_a', 'new_m_lru_b_a', 'new_m_lru_w_x', 'new_m_lru_b_x', 'new_m_lru_lambda', 'new_m_w_out', 'new_m_norm2_g', 'new_m_w_up', 'new_m_w_down', 'new_m_final_norm_g', 'new_v_norm1_g', 'new_v_w_in', 'new_v_hgrn_lb_logits', 'new_v_hgrn_norm_g', 'new_v_gdn_conv_w', 'new_v_gdn_a_log', 'new_v_gdn_dt_bias', 'new_v_gdn_norm_g', 'new_v_lru_conv_w', 'new_v_lru_conv_b', 'new_v_lru_w_a', 'new_v_lru_b_a', 'new_v_lru_w_x', 'new_v_lru_b_x', 'new_v_lru_lambda', 'new_v_w_out', 'new_v_norm2_g', 'new_v_w_up', 'new_v_w_down', 'new_v_final_norm_g']
TWIN_LEAF_KINDS = {'loss': 'loss', 'grad_x': 'grad_x', 'grad_norm1_g': 'grad_w', 'grad_w_in': 'grad_w', 'grad_hgrn_lb_logits': 'grad_w', 'grad_hgrn_norm_g': 'grad_w', 'grad_gdn_conv_w': 'grad_w', 'grad_gdn_a_log': 'grad_w', 'grad_gdn_dt_bias': 'grad_w', 'grad_gdn_norm_g': 'grad_w', 'grad_lru_conv_w': 'grad_w', 'grad_lru_conv_b': 'grad_w', 'grad_lru_w_a': 'grad_w', 'grad_lru_b_a': 'grad_w', 'grad_lru_w_x': 'grad_w', 'grad_lru_b_x': 'grad_w', 'grad_lru_lambda': 'grad_w', 'grad_w_out': 'grad_w', 'grad_norm2_g': 'grad_w', 'grad_w_up': 'grad_w', 'grad_w_down': 'grad_w', 'grad_final_norm_g': 'grad_w', 'delta_norm1_g': 'delta_w', 'delta_w_in': 'delta_w', 'delta_hgrn_lb_logits': 'delta_w', 'delta_hgrn_norm_g': 'delta_w', 'delta_gdn_conv_w': 'delta_w', 'delta_gdn_a_log': 'delta_w', 'delta_gdn_dt_bias': 'delta_w', 'delta_gdn_norm_g': 'delta_w', 'delta_lru_conv_w': 'delta_w', 'delta_lru_conv_b': 'delta_w', 'delta_lru_w_a': 'delta_w', 'delta_lru_b_a': 'delta_w', 'delta_lru_w_x': 'delta_w', 'delta_lru_b_x': 'delta_w', 'delta_lru_lambda': 'delta_w', 'delta_w_out': 'delta_w', 'delta_norm2_g': 'delta_w', 'delta_w_up': 'delta_w', 'delta_w_down': 'delta_w', 'delta_final_norm_g': 'delta_w', 'new_m_norm1_g': 'new_m', 'new_m_w_in': 'new_m', 'new_m_hgrn_lb_logits': 'new_m', 'new_m_hgrn_norm_g': 'new_m', 'new_m_gdn_conv_w': 'new_m', 'new_m_gdn_a_log': 'new_m', 'new_m_gdn_dt_bias': 'new_m', 'new_m_gdn_norm_g': 'new_m', 'new_m_lru_conv_w': 'new_m', 'new_m_lru_conv_b': 'new_m', 'new_m_lru_w_a': 'new_m', 'new_m_lru_b_a': 'new_m', 'new_m_lru_w_x': 'new_m', 'new_m_lru_b_x': 'new_m', 'new_m_lru_lambda': 'new_m', 'new_m_w_out': 'new_m', 'new_m_norm2_g': 'new_m', 'new_m_w_up': 'new_m', 'new_m_w_down': 'new_m', 'new_m_final_norm_g': 'new_m', 'new_v_norm1_g': 'new_v', 'new_v_w_in': 'new_v', 'new_v_hgrn_lb_logits': 'new_v', 'new_v_hgrn_norm_g': 'new_v', 'new_v_gdn_conv_w': 'new_v', 'new_v_gdn_a_log': 'new_v', 'new_v_gdn_dt_bias': 'new_v', 'new_v_gdn_norm_g': 'new_v', 'new_v_lru_conv_w': 'new_v', 'new_v_lru_conv_b': 'new_v', 'new_v_lru_w_a': 'new_v', 'new_v_lru_b_a': 'new_v', 'new_v_lru_w_x': 'new_v', 'new_v_lru_b_x': 'new_v', 'new_v_lru_lambda': 'new_v', 'new_v_w_out': 'new_v', 'new_v_norm2_g': 'new_v', 'new_v_w_up': 'new_v', 'new_v_w_down': 'new_v', 'new_v_final_norm_g': 'new_v'}


def _forward(args):
    return _fwd_reference(*[args[k] for k in FWD_PARAMS])


def _output_shape():
    def fwd():
        inp = _fwd_setup_inputs(0)
        return _fwd_reference(*[inp[k] for k in FWD_PARAMS])
    out = _jax.eval_shape(fwd)
    return out.shape, out.dtype

N_MICROBATCH = 1
ADAM_LR = 0.001
ADAM_B1 = 0.9
ADAM_B2 = 0.999
ADAM_EPS = 1e-08
ADAM_WD = 0.01
ADAM_STEP = 10
PER_EXAMPLE_BATCH_AXIS = {'x': 0, 'loss_target': 0}
SHARED_INPUTS = []
_WEIGHT_DTYPES = {'norm1_g': _jnp.float32, 'w_in': _jnp.float32, 'hgrn_lb_logits': _jnp.float32, 'hgrn_norm_g': _jnp.float32, 'gdn_conv_w': _jnp.float32, 'gdn_a_log': _jnp.float32, 'gdn_dt_bias': _jnp.float32, 'gdn_norm_g': _jnp.float32, 'lru_conv_w': _jnp.float32, 'lru_conv_b': _jnp.float32, 'lru_w_a': _jnp.float32, 'lru_b_a': _jnp.float32, 'lru_w_x': _jnp.float32, 'lru_b_x': _jnp.float32, 'lru_lambda': _jnp.float32, 'w_out': _jnp.float32, 'norm2_g': _jnp.float32, 'w_up': _jnp.float32, 'w_down': _jnp.float32, 'final_norm_g': _jnp.float32}
MOMENT_SCALE = {'norm1_g': 2.312057e-01, 'w_in': 1.212652e-01, 'hgrn_lb_logits': 7.900745e-03, 'hgrn_norm_g': 2.644436e-01, 'gdn_conv_w': 9.147659e-02, 'gdn_a_log': 4.406318e-01, 'gdn_dt_bias': 4.357132e-01, 'gdn_norm_g': 3.044701e-01, 'lru_conv_w': 3.699059e-01, 'lru_conv_b': 1.343311e+00, 'lru_w_a': 5.191493e-02, 'lru_b_a': 7.388321e-02, 'lru_w_x': 1.027129e-01, 'lru_b_x': 1.327631e-01, 'lru_lambda': 1.741606e-01, 'w_out': 1.672654e-01, 'norm2_g': 2.002122e-01, 'w_up': 9.828637e-02, 'w_down': 2.427401e-01, 'final_norm_g': 6.622887e+01}


def _to_microbatches(a, axis):
    t = _jnp.moveaxis(a, axis, 0)
    t = t.reshape((N_MICROBATCH, t.shape[0] // N_MICROBATCH) + t.shape[1:])
    return _jnp.moveaxis(t, 1, axis + 1)


def setup_inputs(seed: int = 0) -> dict:
    inp = _fwd_setup_inputs(seed)
    key = _jax.random.fold_in(_jax.random.key(seed), 7919)
    shape, _ = _output_shape()
    out = dict(inp)
    out["loss_target"] = _jax.random.normal(_jax.random.fold_in(key, 0), shape, _jnp.float32)
    for i, name in enumerate(TWIN_WEIGHTS):
        w = inp[name].astype(_jnp.float32)
        if MOMENT_SCALE is None:
            s = _jnp.sqrt(_jnp.mean(_jnp.square(w)) + 1e-30)
        else:
            s = MOMENT_SCALE[name]
        km, kv = _jax.random.split(_jax.random.fold_in(key, i + 1))
        out[name] = w
        out["m_" + name] = s * _jax.random.normal(km, w.shape, _jnp.float32)
        out["v_" + name] = (s * s) * _jax.random.uniform(kv, w.shape, _jnp.float32, 0.5, 1.5)
    if N_MICROBATCH > 1:
        for name, axis in PER_EXAMPLE_BATCH_AXIS.items():
            out[name] = _to_microbatches(out[name], axis)
    return {'x': out['x'], 'norm1_g': out['norm1_g'], 'w_in': out['w_in'], 'hgrn_lb_logits': out['hgrn_lb_logits'], 'hgrn_norm_g': out['hgrn_norm_g'], 'gdn_conv_w': out['gdn_conv_w'], 'gdn_a_log': out['gdn_a_log'], 'gdn_dt_bias': out['gdn_dt_bias'], 'gdn_norm_g': out['gdn_norm_g'], 'lru_conv_w': out['lru_conv_w'], 'lru_conv_b': out['lru_conv_b'], 'lru_w_a': out['lru_w_a'], 'lru_b_a': out['lru_b_a'], 'lru_w_x': out['lru_w_x'], 'lru_b_x': out['lru_b_x'], 'lru_lambda': out['lru_lambda'], 'w_out': out['w_out'], 'norm2_g': out['norm2_g'], 'w_up': out['w_up'], 'w_down': out['w_down'], 'final_norm_g': out['final_norm_g'], 'loss_target': out['loss_target'], 'm_norm1_g': out['m_norm1_g'], 'm_w_in': out['m_w_in'], 'm_hgrn_lb_logits': out['m_hgrn_lb_logits'], 'm_hgrn_norm_g': out['m_hgrn_norm_g'], 'm_gdn_conv_w': out['m_gdn_conv_w'], 'm_gdn_a_log': out['m_gdn_a_log'], 'm_gdn_dt_bias': out['m_gdn_dt_bias'], 'm_gdn_norm_g': out['m_gdn_norm_g'], 'm_lru_conv_w': out['m_lru_conv_w'], 'm_lru_conv_b': out['m_lru_conv_b'], 'm_lru_w_a': out['m_lru_w_a'], 'm_lru_b_a': out['m_lru_b_a'], 'm_lru_w_x': out['m_lru_w_x'], 'm_lru_b_x': out['m_lru_b_x'], 'm_lru_lambda': out['m_lru_lambda'], 'm_w_out': out['m_w_out'], 'm_norm2_g': out['m_norm2_g'], 'm_w_up': out['m_w_up'], 'm_w_down': out['m_w_down'], 'm_final_norm_g': out['m_final_norm_g'], 'v_norm1_g': out['v_norm1_g'], 'v_w_in': out['v_w_in'], 'v_hgrn_lb_logits': out['v_hgrn_lb_logits'], 'v_hgrn_norm_g': out['v_hgrn_norm_g'], 'v_gdn_conv_w': out['v_gdn_conv_w'], 'v_gdn_a_log': out['v_gdn_a_log'], 'v_gdn_dt_bias': out['v_gdn_dt_bias'], 'v_gdn_norm_g': out['v_gdn_norm_g'], 'v_lru_conv_w': out['v_lru_conv_w'], 'v_lru_conv_b': out['v_lru_conv_b'], 'v_lru_w_a': out['v_lru_w_a'], 'v_lru_b_a': out['v_lru_b_a'], 'v_lru_w_x': out['v_lru_w_x'], 'v_lru_b_x': out['v_lru_b_x'], 'v_lru_lambda': out['v_lru_lambda'], 'v_w_out': out['v_w_out'], 'v_norm2_g': out['v_norm2_g'], 'v_w_up': out['v_w_up'], 'v_w_down': out['v_w_down'], 'v_final_norm_g': out['v_final_norm_g']}


def _loss(weights, diff, rest, loss_target):
    with _jax.named_scope("forward"):
        args = {**rest, TWIN_DIFF_INPUT: diff, **{k: w.astype(_WEIGHT_DTYPES[k]) for k, w in weights.items()}}
        y = _forward(args)
    with _jax.named_scope("loss_head"):
        err = _jnp.square(y.astype(_jnp.float32) - loss_target)
        return 0.5 * _jnp.sum(_jnp.mean(err, axis=-1)) if err.ndim else 0.5 * err


def _adamw(w, g, m, v):
    m = ADAM_B1 * m + (1.0 - ADAM_B1) * g
    v = ADAM_B2 * v + (1.0 - ADAM_B2) * _jnp.square(g)
    m_hat = m / (1.0 - ADAM_B1 ** ADAM_STEP)
    v_hat = v / (1.0 - ADAM_B2 ** ADAM_STEP)
    delta = -ADAM_LR * (m_hat / (_jnp.sqrt(v_hat) + ADAM_EPS) + ADAM_WD * w)
    return delta, m, v


def reference(x, norm1_g, w_in, hgrn_lb_logits, hgrn_norm_g, gdn_conv_w, gdn_a_log, gdn_dt_bias, gdn_norm_g, lru_conv_w, lru_conv_b, lru_w_a, lru_b_a, lru_w_x, lru_b_x, lru_lambda, w_out, norm2_g, w_up, w_down, final_norm_g, loss_target, m_norm1_g, m_w_in, m_hgrn_lb_logits, m_hgrn_norm_g, m_gdn_conv_w, m_gdn_a_log, m_gdn_dt_bias, m_gdn_norm_g, m_lru_conv_w, m_lru_conv_b, m_lru_w_a, m_lru_b_a, m_lru_w_x, m_lru_b_x, m_lru_lambda, m_w_out, m_norm2_g, m_w_up, m_w_down, m_final_norm_g, v_norm1_g, v_w_in, v_hgrn_lb_logits, v_hgrn_norm_g, v_gdn_conv_w, v_gdn_a_log, v_gdn_dt_bias, v_gdn_norm_g, v_lru_conv_w, v_lru_conv_b, v_lru_w_a, v_lru_b_a, v_lru_w_x, v_lru_b_x, v_lru_lambda, v_w_out, v_norm2_g, v_w_up, v_w_down, v_final_norm_g):
    given = dict(x=x, norm1_g=norm1_g, w_in=w_in, hgrn_lb_logits=hgrn_lb_logits, hgrn_norm_g=hgrn_norm_g, gdn_conv_w=gdn_conv_w, gdn_a_log=gdn_a_log, gdn_dt_bias=gdn_dt_bias, gdn_norm_g=gdn_norm_g, lru_conv_w=lru_conv_w, lru_conv_b=lru_conv_b, lru_w_a=lru_w_a, lru_b_a=lru_b_a, lru_w_x=lru_w_x, lru_b_x=lru_b_x, lru_lambda=lru_lambda, w_out=w_out, norm2_g=norm2_g, w_up=w_up, w_down=w_down, final_norm_g=final_norm_g, loss_target=loss_target, m_norm1_g=m_norm1_g, m_w_in=m_w_in, m_hgrn_lb_logits=m_hgrn_lb_logits, m_hgrn_norm_g=m_hgrn_norm_g, m_gdn_conv_w=m_gdn_conv_w, m_gdn_a_log=m_gdn_a_log, m_gdn_dt_bias=m_gdn_dt_bias, m_gdn_norm_g=m_gdn_norm_g, m_lru_conv_w=m_lru_conv_w, m_lru_conv_b=m_lru_conv_b, m_lru_w_a=m_lru_w_a, m_lru_b_a=m_lru_b_a, m_lru_w_x=m_lru_w_x, m_lru_b_x=m_lru_b_x, m_lru_lambda=m_lru_lambda, m_w_out=m_w_out, m_norm2_g=m_norm2_g, m_w_up=m_w_up, m_w_down=m_w_down, m_final_norm_g=m_final_norm_g, v_norm1_g=v_norm1_g, v_w_in=v_w_in, v_hgrn_lb_logits=v_hgrn_lb_logits, v_hgrn_norm_g=v_hgrn_norm_g, v_gdn_conv_w=v_gdn_conv_w, v_gdn_a_log=v_gdn_a_log, v_gdn_dt_bias=v_gdn_dt_bias, v_gdn_norm_g=v_gdn_norm_g, v_lru_conv_w=v_lru_conv_w, v_lru_conv_b=v_lru_conv_b, v_lru_w_a=v_lru_w_a, v_lru_b_a=v_lru_b_a, v_lru_w_x=v_lru_w_x, v_lru_b_x=v_lru_b_x, v_lru_lambda=v_lru_lambda, v_w_out=v_w_out, v_norm2_g=v_norm2_g, v_w_up=v_w_up, v_w_down=v_w_down, v_final_norm_g=v_final_norm_g)
    weights = {n: given[n] for n in TWIN_WEIGHTS}
    shared = {n: given[n] for n in SHARED_INPUTS}
    per_example = {n: given[n] for n in ['x']}
    grad_fn = _jax.value_and_grad(_loss, argnums=(0, 1))

    def one_microbatch(ex, loss_target):
        ex = dict(ex)
        diff = ex.pop(TWIN_DIFF_INPUT)
        return grad_fn(weights, diff, {**shared, **ex}, loss_target)

    if N_MICROBATCH == 1:
        loss, (grad_w, grad_x) = one_microbatch(per_example, given["loss_target"])
    else:
        def body(carry, xs):
            loss_sum, grad_sum = carry
            l_k, (gw_k, gx_k) = one_microbatch(xs[0], xs[1])
            with _jax.named_scope("update"):
                return (loss_sum + l_k, _jax.tree.map(_jnp.add, grad_sum, gw_k)), gx_k

        init = (_jnp.zeros((), _jnp.float32), _jax.tree.map(_jnp.zeros_like, weights))
        (loss, grad_w), grad_x = _jax.lax.scan(body, init, (per_example, given["loss_target"]))
    with _jax.named_scope("update"):
        delta_w, new_m, new_v = {}, {}, {}
        for n in TWIN_WEIGHTS:
            delta_w[n], new_m[n], new_v[n] = _adamw(weights[n], grad_w[n], given["m_" + n], given["v_" + n])
    return (loss, grad_x, *[grad_w[n] for n in TWIN_WEIGHTS], *[delta_w[n] for n in TWIN_WEIGHTS],
            *[new_m[n] for n in TWIN_WEIGHTS], *[new_v[n] for n in TWIN_WEIGHTS])
```

```python
import functools

import jax
import jax.numpy as jnp
from jax import lax
from jax.experimental import pallas as pl
from jax.experimental.pallas import tpu as pltpu

F32 = jnp.float32
BF16 = jnp.bfloat16
HI = lax.Precision.HIGHEST
MESH = pl.DeviceIdType.MESH

DEPTH = 4
D_MODEL = 1024
A_W, B_W, C_W = 256, 512, 256
A_HD, B_HD = 64, 128
D_IN = 3592
D_IN_PAD = 3840
COL_BZ, COL_AQ, COL_AF, COL_AI, COL_AG, COL_CX, COL_CY = 1536, 2048, 2304, 2560, 2816, 3072, 3328
GATE0 = 3584
D_FF = 4096
CHUNK = 64
SUB = 16
EPS = 1e-6
TINY = 1e-30
RG_C = 8.0
LR, B1, B2, AEPS, WD, STEP = 0.001, 0.9, 0.999, 1e-08, 0.01, 10
VMEM_LIMIT = 56 * 1024 * 1024
PACK_ROWS = 13 * 1024

NN = ((1,), (0,))
NT = ((1,), (1,))
TN = ((0,), (0,))


def _dot(a, b, dims=NN, prec=HI):
    return lax.dot_general(a, b, (dims, ((), ())), precision=prec, preferred_element_type=F32)


def _iota(shape, d):
    return lax.broadcasted_iota(jnp.int32, shape, d)


def _block_ones(width, group):
    return (_iota((width, width), 0) // group == _iota((width, width), 1) // group).astype(F32)


def _params(sem):
    return pltpu.CompilerParams(dimension_semantics=sem, vmem_limit_bytes=VMEM_LIMIT)


def _tile(dim, prefs):
    for p in prefs:
        if dim % p == 0:
            return p
    return dim


def _whole(shape):
    return pl.BlockSpec(shape, lambda i: (0,) * len(shape))


def _rowspec(t, w, col0=0):
    cb = col0 // w
    return pl.BlockSpec((t, w), lambda i: (i, cb))


def _cols(arr, col0, width):
    assert col0 % width == 0
    return (arr, col0, width)


def _row_operands(rows, t):
    arrs, specs = [], []
    for r in rows:
        arr, col0, w = r if isinstance(r, tuple) else (r, 0, r.shape[1])
        arrs.append(arr)
        specs.append(_rowspec(t, w, col0))
    return arrs, specs


def _rowwise_fwd(name, fn, params, rows, out_widths, out_dtypes, t):
    rows, row_specs = _row_operands(rows, t)
    s = rows[0].shape[0]
    n_p, n_r = len(params), len(rows)

    def body(*refs):
        p = [r[...] for r in refs[:n_p]]
        xs = [r[...].astype(F32) for r in refs[n_p:n_p + n_r]]
        outs = fn(pl.program_id(0) * t, *p, *xs)
        for o_ref, o in zip(refs[n_p + n_r:], outs):
            o_ref[...] = o.astype(o_ref.dtype)

    return pl.pallas_call(
        body, grid=(s // t,),
        in_specs=[_whole(p.shape) for p in params] + row_specs,
        out_specs=[_rowspec(t, w) for w in out_widths],
        out_shape=[jax.ShapeDtypeStruct((s, w), d) for w, d in zip(out_widths, out_dtypes)],
        name=name, compiler_params=_params(("arbitrary",)))(*params, *rows)


def _rowwise_bwd(name, fn, params, rows, douts, t, need=None, res=None):
    rows, row_specs = _row_operands(rows, t)
    s = rows[0].shape[0]
    n_p, n_r, n_o = len(params), len(rows), len(douts)
    need = [True] * n_r if need is None else need
    n_res = 0 if res is None else 1

    def body(*refs):
        p = [r[...] for r in refs[:n_p]]
        xs = [r[...].astype(F32) for r in refs[n_p:n_p + n_r]]
        dys = [r[...].astype(F32) for r in refs[n_p + n_r:n_p + n_r + n_o]]
        k = n_p + n_r + n_o
        res_ref = refs[k] if n_res else None
        dp_refs = refs[k + n_res:k + n_res + n_p]
        dx_refs = refs[k + n_res + n_p:]
        row0 = pl.program_id(0) * t
        _, vjp = jax.vjp(lambda *a: tuple(fn(row0, *a)), *p, *xs)
        g = vjp(tuple(dys))

        @pl.when(pl.program_id(0) == 0)
        def _():
            for r in dp_refs:
                r[...] = jnp.zeros_like(r)

        for r, gp in zip(dp_refs, g[:n_p]):
            r[...] += gp
        gx = [gi for gi, nd in zip(g[n_p:], need) if nd]
        for j, (r, gi) in enumerate(zip(dx_refs, gx)):
            if j == 0 and n_res:
                gi = gi + res_ref[...]
            r[...] = gi

    widths = [sp.block_shape[1] for sp, nd in zip(row_specs, need) if nd]
    outs = pl.pallas_call(
        body, grid=(s // t,),
        in_specs=([_whole(p.shape) for p in params] + row_specs
                  + [_rowspec(t, d.shape[1]) for d in douts] + ([_rowspec(t, res.shape[1])] if n_res else [])),
        out_specs=[_whole(p.shape) for p in params] + [_rowspec(t, w) for w in widths],
        out_shape=[jax.ShapeDtypeStruct(p.shape, F32) for p in params] + [jax.ShapeDtypeStruct((s, w), F32) for w in widths],
        name=name, compiler_params=_params(("arbitrary",)))(*params, *rows, *douts, *([res] if n_res else []))
    return list(outs[:n_p]), list(outs[n_p:])


def _rmsnorm_fn(row0, g, x):
    y = x * lax.rsqrt(jnp.mean(x * x, axis=-1, keepdims=True) + EPS)
    return (y * g,)


def _hgrn_pre_fn(layer, row0, logits, aq, af):
    m = jnp.max(logits, axis=0, keepdims=True)
    e = jnp.exp(logits - m)
    p = e / jnp.sum(e, axis=0, keepdims=True)
    if layer == 0:
        lb = jnp.zeros((1, A_W), F32)
    else:
        acc = p[1:2]
        for j in range(2, layer + 1):
            acc = acc + p[j:j + 1]
        lb = jnp.minimum(jnp.maximum(acc, 0.0), 1.0 - EPS)
    sig = jax.nn.sigmoid(af)
    f = lb + (1.0 - lb) * sig
    log_f = jnp.log(jnp.maximum(f, TINY))
    k = (1.0 - lb) * jax.nn.sigmoid(-af)
    return jax.nn.silu(aq), k, log_f


def _hgrn_post_fn(row0, g, o, ag):
    ms = _dot(o * o, _block_ones(A_W, A_HD)) * (1.0 / A_HD)
    return (o * lax.rsqrt(ms + EPS) * g * jax.nn.silu(ag),)


def _gdn_pre_fn(row0, a_log, dt_bias, qkvc, gate):
    act = jax.nn.silu(qkvc)
    q, k, v = act[:, :B_W], act[:, B_W:2 * B_W], act[:, 2 * B_W:]
    bo = _block_ones(B_W, B_HD)
    q = q * lax.rsqrt(_dot(q * q, bo) + EPS) * (B_HD ** -0.5)
    k = k * lax.rsqrt(_dot(k * k, bo) + EPS)
    beta = jax.nn.sigmoid(gate)
    log_alpha = -jnp.exp(a_log) * jax.nn.softplus(gate + dt_bias)
    head = _iota((128, B_W), 1) // B_HD
    lane = _iota((128, B_W), 0)
    beta_b = _dot(beta, (lane == head).astype(F32))
    la_b = _dot(log_alpha, (lane == head + 4).astype(F32))
    return q, k, v, beta_b, la_b


def _gdn_post_fn(row0, g, o, bz):
    ms = _dot(o * o, _block_ones(B_W, B_HD)) * (1.0 / B_HD)
    return (o * lax.rsqrt(ms + EPS) * g * jax.nn.silu(bz),)


def _lru_pre_fn(row0, w_a, b_a, w_x, b_x, lam, xc):
    r = jax.nn.sigmoid(_dot(xc, w_a) + b_a)
    i = jax.nn.sigmoid(_dot(xc, w_x) + b_x)
    log_a = -RG_C * r * jax.nn.softplus(-lam)
    a = jnp.exp(log_a)
    one_minus = -jnp.tanh(log_a) * (a * a + 1.0)
    mult = jnp.sqrt(jnp.maximum(one_minus, EPS))
    pos = row0 + _iota(xc.shape, 0)
    mult = jnp.where(pos == 0, 1.0, mult)
    return a, mult * i * xc


def _lru_post_fn(row0, h, cy):
    return (jax.nn.gelu(cy, approximate=True) * h,)


def _lru_da_fn(row0, g, h_prev):
    return (g * h_prev,)


def _relu2(u):
    r = jnp.maximum(u, 0.0)
    return r * r


def _drelu2(acc, u):
    return acc * (2.0 * jnp.maximum(u, 0.0))


def _matmul(name, a, b, mode, a_fn=None, res=None, epi=None, epi_in=None, out_dtype=F32):
    if mode == "nn":
        (m, k), n = a.shape, b.shape[1]
    elif mode == "nt":
        (m, k), n = a.shape, b.shape[0]
    else:
        (k, m), n = a.shape, b.shape[1]
    tm = _tile(m, (512, 256, 128))
    tn = _tile(n, (1024, 768, 512, 256, 128))
    tk = _tile(k, (1024, 768, 512, 256, 128)) if mode != "tn" else _tile(k, (512,))
    nk = k // tk
    a_spec = {"nn": pl.BlockSpec((tm, tk), lambda i, j, kk: (i, kk)),
              "nt": pl.BlockSpec((tm, tk), lambda i, j, kk: (i, kk)),
              "tn": pl.BlockSpec((tk, tm), lambda i, j, kk: (kk, i))}[mode]
    b_spec = {"nn": pl.BlockSpec((tk, tn), lambda i, j, kk: (kk, j)),
              "nt": pl.BlockSpec((tn, tk), lambda i, j, kk: (j, kk)),
              "tn": pl.BlockSpec((tk, tn), lambda i, j, kk: (kk, j))}[mode]
    o_spec = pl.BlockSpec((tm, tn), lambda i, j, kk: (i, j))
    dims = {"nn": NN, "nt": NT, "tn": TN}[mode]
    extra = [x for x in (res, epi_in) if x is not None]

    def body(*refs):
        a_ref, b_ref = refs[0], refs[1]
        rest = list(refs[2:])
        res_ref = rest.pop(0) if res is not None else None
        epi_ref = rest.pop(0) if epi_in is not None else None
        o_ref, acc = rest
        kk = pl.program_id(2)

        @pl.when(kk == 0)
        def _():
            acc[...] = jnp.zeros_like(acc)

        at = a_ref[...]
        if a_fn is not None:
            at = a_fn(at.astype(F32))
        acc[...] += lax.dot_general(at.astype(BF16), b_ref[...].astype(BF16), (dims, ((), ())),
                                    preferred_element_type=F32)

        @pl.when(kk == nk - 1)
        def _():
            out = acc[...]
            if res_ref is not None:
                out = out + res_ref[...]
            if epi is not None:
                out = epi(out, epi_ref[...])
            o_ref[...] = out.astype(o_ref.dtype)

    return pl.pallas_call(
        body, grid=(m // tm, n // tn, nk),
        in_specs=[a_spec, b_spec] + [o_spec] * len(extra), out_specs=o_spec,
        out_shape=jax.ShapeDtypeStruct((m, n), out_dtype),
        scratch_shapes=[pltpu.VMEM((tm, tn), F32)],
        name=name, compiler_params=_params(("arbitrary", "arbitrary", "arbitrary")))(a, b, *extra)


HALO = 8


def _conv_fwd(name, x, col0, width, w, bias, t=512):
    s = x.shape[0]
    cb = col0 // width
    hb = t // HALO

    def body(x_ref, halo_ref, w_ref, b_ref, y_ref):
        i = pl.program_id(0)
        halo = jnp.where(i == 0, 0.0, halo_ref[...])
        xp = jnp.concatenate([halo, x_ref[...]], axis=0)
        wv = w_ref[...]
        y = b_ref[...] + wv[0:1] * xp[HALO - 3:HALO - 3 + t]
        for j in range(1, 4):
            y = y + wv[j:j + 1] * xp[HALO - 3 + j:HALO - 3 + j + t]
        y_ref[...] = y

    return pl.pallas_call(
        body, grid=(s // t,),
        in_specs=[pl.BlockSpec((t, width), lambda i: (i, cb)),
                  pl.BlockSpec((HALO, width), lambda i: (jnp.maximum(i * hb - 1, 0), cb)),
                  _whole(w.shape), _whole(bias.shape)],
        out_specs=_rowspec(t, width), out_shape=jax.ShapeDtypeStruct((s, width), F32),
        name=name, compiler_params=_params(("arbitrary",)))(x, x, w, bias)


def _conv_bwd(name, dy, x, col0, width, w, t=512):
    s = x.shape[0]
    cb = col0 // width
    hb = t // HALO
    nblk = s // t

    def body(dy_ref, dyn_ref, x_ref, halo_ref, w_ref, dx_ref, dw_ref, db_ref):
        i = pl.program_id(0)
        dyv = dy_ref[...]
        nxt = jnp.where(i == nblk - 1, 0.0, dyn_ref[...])
        dyp = jnp.concatenate([dyv, nxt], axis=0)
        wv = w_ref[...]
        dx = wv[3:4] * dyv
        for j in range(3):
            dx = dx + wv[j:j + 1] * dyp[3 - j:3 - j + t]
        dx_ref[...] = dx
        halo = jnp.where(i == 0, 0.0, halo_ref[...])
        xp = jnp.concatenate([halo, x_ref[...]], axis=0)
        dw = jnp.concatenate(
            [jnp.sum(dyv * xp[HALO - 3 + j:HALO - 3 + j + t], axis=0, keepdims=True) for j in range(4)], axis=0)

        @pl.when(i == 0)
        def _():
            dw_ref[...] = jnp.zeros_like(dw_ref)
            db_ref[...] = jnp.zeros_like(db_ref)

        dw_ref[...] += dw
        db_ref[...] += jnp.sum(dyv, axis=0, keepdims=True)

    return pl.pallas_call(
        body, grid=(nblk,),
        in_specs=[_rowspec(t, width),
                  pl.BlockSpec((HALO, width), lambda i: (jnp.minimum((i + 1) * hb, s // HALO - 1), 0)),
                  pl.BlockSpec((t, width), lambda i: (i, cb)),
                  pl.BlockSpec((HALO, width), lambda i: (jnp.maximum(i * hb - 1, 0), cb)),
                  _whole(w.shape)],
        out_specs=[_rowspec(t, width), _whole((4, width)), _whole((1, width))],
        out_shape=[jax.ShapeDtypeStruct((s, width), F32), jax.ShapeDtypeStruct((4, width), F32),
                   jax.ShapeDtypeStruct((1, width), F32)],
        name=name, compiler_params=_params(("arbitrary",)))(dy, dy, x, x, w)


def _hgrn_chunk(st, q, k, v, lf):
    r = _iota((CHUNK, CHUNK), 0)
    c = _iota((CHUNK, CHUNK), 1)
    cum = _dot((c <= r).astype(F32), lf)
    tot = _dot(jnp.ones((CHUNK, CHUNK), F32), lf)
    cb = _dot((c == (r // SUB) * SUB - 1).astype(F32), cum)
    bm = _iota((A_W, A_W), 0) // A_HD == _iota((A_W, A_W), 1) // A_HD
    bo = bm.astype(F32)
    o_inter = _dot(q * jnp.exp(cum), st, NT)
    kd = k * jnp.exp(tot - cum)
    dec = jnp.concatenate([jnp.exp(tot)] * (A_W // CHUNK), axis=0)
    st_new = st * dec + jnp.where(bm, _dot(v, kd, TN), 0.0)
    qt = q * jnp.exp(cum - cb)
    lane_h = _iota((SUB, A_W), 1) // A_HD
    n_h = A_W // A_HD
    t3 = _iota((SUB, SUB, A_W), 0)
    s3 = _iota((SUB, SUB, A_W), 1)
    outs = []
    for i in range(CHUNK // SUB):
        sl = slice(i * SUB, (i + 1) * SUB)
        qi, ki, vi, ci = q[sl], k[sl], v[sl], cum[sl]
        e = jnp.exp(jnp.minimum(ci[:, None, :] - ci[None, :, :], 0.0))
        d3 = jnp.where(s3 <= t3, qi[:, None, :] * ki[None, :, :] * e, 0.0)
        ds = _dot(d3.reshape(SUB * SUB, A_W), bo).reshape(SUB, SUB, A_W)
        oi = jnp.sum(ds * vi[None, :, :], axis=1)
        if i > 0:
            n = i * SUB
            kt = k[:n] * jnp.exp(jnp.minimum(cb[sl][0:1] - cum[:n], 0.0))
            qs = jnp.concatenate([jnp.where(lane_h == h, qt[sl], 0.0) for h in range(n_h)], axis=0)
            p = _dot(_dot(qs, kt, NT), v[:n])
            for h in range(n_h):
                oi = oi + jnp.where(lane_h == h, p[h * SUB:(h + 1) * SUB], 0.0)
        outs.append(oi)
    return st_new, o_inter + jnp.concatenate(outs, axis=0)


def _chunkspec(w, rev, n):
    if rev:
        return pl.BlockSpec((CHUNK, w), lambda i: (n - 1 - i, 0))
    return pl.BlockSpec((CHUNK, w), lambda i: (i, 0))


def _hgrn_fwd(name, q, k, v, lf):
    s = q.shape[0]
    n = s // CHUNK

    def body(q_ref, k_ref, v_ref, lf_ref, o_ref, sts_ref, st):
        @pl.when(pl.program_id(0) == 0)
        def _():
            st[...] = jnp.zeros_like(st)

        sts_ref[0] = st[...]
        st_new, o = _hgrn_chunk(st[...], q_ref[...], k_ref[...], v_ref[...], lf_ref[...])
        st[...] = st_new
        o_ref[...] = o

    return pl.pallas_call(
        body, grid=(n,), in_specs=[_chunkspec(A_W, False, n)] * 4,
        out_specs=[_chunkspec(A_W, False, n), pl.BlockSpec((1, A_W, A_W), lambda i: (i, 0, 0))],
        out_shape=[jax.ShapeDtypeStruct((s, A_W), F32), jax.ShapeDtypeStruct((n, A_W, A_W), F32)],
        scratch_shapes=[pltpu.VMEM((A_W, A_W), F32)],
        name=name, compiler_params=_params(("arbitrary",)))(q, k, v, lf)


def _hgrn_bwd(name, q, k, v, lf, sts, do):
    s = q.shape[0]
    n = s // CHUNK

    def body(q_ref, k_ref, v_ref, lf_ref, sts_ref, do_ref, dq_ref, dk_ref, dv_ref, dlf_ref, dst):
        @pl.when(pl.program_id(0) == 0)
        def _():
            dst[...] = jnp.zeros_like(dst)

        _, vjp = jax.vjp(_hgrn_chunk, sts_ref[0], q_ref[...], k_ref[...], v_ref[...], lf_ref[...])
        g = vjp((dst[...], do_ref[...]))
        dst[...] = g[0]
        dq_ref[...] = g[1]
        dk_ref[...] = g[2]
        dv_ref[...] = g[3]
        dlf_ref[...] = g[4]

    rs = _chunkspec(A_W, True, n)
    return pl.pallas_call(
        body, grid=(n,),
        in_specs=[rs] * 4 + [pl.BlockSpec((1, A_W, A_W), lambda i: (n - 1 - i, 0, 0)), rs],
        out_specs=[rs] * 4, out_shape=[jax.ShapeDtypeStruct((s, A_W), F32)] * 4,
        scratch_shapes=[pltpu.VMEM((A_W, A_W), F32)],
        name=name, compiler_params=_params(("arbitrary",)))(q, k, v, lf, sts, do)


B_NH = B_W // B_HD


def _gdn_chunk_head(st, q, k, v, beta, la):
    r = _iota((CHUNK, CHUNK), 0)
    c = _iota((CHUNK, CHUNK), 1)
    causal = c <= r
    strict = c < r
    g = _dot(causal.astype(F32), la)
    tot = _dot(jnp.ones((CHUNK, CHUNK), F32), la)
    gcol = g[:, :CHUNK]
    decay = jnp.where(causal, jnp.exp(jnp.where(causal, gcol - gcol.T, 0.0)), 0.0)
    kb = k * beta
    a_mat = jnp.where(strict, _dot(kb, k, NT) * decay, 0.0)
    nil = -a_mat
    inv = (r == c).astype(F32) + nil
    for _ in range(5):
        nil = _dot(nil, nil)
        inv = inv + _dot(inv, nil)
    eg = jnp.exp(g)
    u = _dot(inv, v * beta)
    w = _dot(inv, kb * eg)
    qk = jnp.where(causal, _dot(q, k, NT) * decay, 0.0)
    v_new = u - _dot(w, st)
    o = _dot(q * eg, st) + _dot(qk, v_new)
    dec = jnp.concatenate([jnp.exp(tot)] * (B_HD // CHUNK), axis=0)
    st_new = st * dec + _dot(k * jnp.exp(tot - g), v_new, TN)
    return st_new, o


def _gdn_chunk(sts, q, k, v, beta, la):
    new, outs = [], []
    for h in range(B_NH):
        sl = slice(h * B_HD, (h + 1) * B_HD)
        st_new, o = _gdn_chunk_head(sts[h], q[:, sl], k[:, sl], v[:, sl], beta[:, sl], la[:, sl])
        new.append(st_new)
        outs.append(o)
    return new, jnp.concatenate(outs, axis=1)


def _gdn_fwd(name, q, k, v, beta, la):
    s = q.shape[0]
    n = s // CHUNK

    def body(q_ref, k_ref, v_ref, b_ref, la_ref, o_ref, sts_ref, st):
        @pl.when(pl.program_id(0) == 0)
        def _():
            st[...] = jnp.zeros_like(st)

        sts_ref[0] = st[...]
        new, o = _gdn_chunk([st[h] for h in range(B_NH)], q_ref[...], k_ref[...], v_ref[...], b_ref[...], la_ref[...])
        for h in range(B_NH):
            st[h] = new[h]
        o_ref[...] = o

    st_spec = pl.BlockSpec((1, B_NH, B_HD, B_HD), lambda i: (i, 0, 0, 0))
    return pl.pallas_call(
        body, grid=(n,), in_specs=[_chunkspec(B_W, False, n)] * 5,
        out_specs=[_chunkspec(B_W, False, n), st_spec],
        out_shape=[jax.ShapeDtypeStruct((s, B_W), F32), jax.ShapeDtypeStruct((n, B_NH, B_HD, B_HD), F32)],
        scratch_shapes=[pltpu.VMEM((B_NH, B_HD, B_HD), F32)],
        name=name, compiler_params=_params(("arbitrary",)))(q, k, v, beta, la)


def _gdn_bwd(name, q, k, v, beta, la, sts, do):
    s = q.shape[0]
    n = s // CHUNK

    def body(q_ref, k_ref, v_ref, b_ref, la_ref, sts_ref, do_ref, dq_ref, dk_ref, dv_ref, db_ref, dla_ref, dst):
        @pl.when(pl.program_id(0) == 0)
        def _():
            dst[...] = jnp.zeros_like(dst)

        _, vjp = jax.vjp(_gdn_chunk, [sts_ref[0, h] for h in range(B_NH)], q_ref[...], k_ref[...], v_ref[...],
                         b_ref[...], la_ref[...])
        g = vjp(([dst[h] for h in range(B_NH)], do_ref[...]))
        for h in range(B_NH):
            dst[h] = g[0][h]
        dq_ref[...] = g[1]
        dk_ref[...] = g[2]
        dv_ref[...] = g[3]
        db_ref[...] = g[4]
        dla_ref[...] = g[5]

    rs = _chunkspec(B_W, True, n)
    return pl.pallas_call(
        body, grid=(n,),
        in_specs=[rs] * 5 + [pl.BlockSpec((1, B_NH, B_HD, B_HD), lambda i: (n - 1 - i, 0, 0, 0)), rs],
        out_specs=[rs] * 5, out_shape=[jax.ShapeDtypeStruct((s, B_W), F32)] * 5,
        scratch_shapes=[pltpu.VMEM((B_NH, B_HD, B_HD), F32)],
        name=name, compiler_params=_params(("arbitrary",)))(q, k, v, beta, la, sts, do)


SCAN_T = 512


def _block_scan(a, b, reverse):
    t = a.shape[0]
    rows = _iota(a.shape, 0)
    sft = 1
    while sft < t:
        if reverse:
            a_s, b_s = pltpu.roll(a, t - sft, 0), pltpu.roll(b, t - sft, 0)
            edge = rows >= t - sft
        else:
            a_s, b_s = pltpu.roll(a, sft, 0), pltpu.roll(b, sft, 0)
            edge = rows < sft
        a_s = jnp.where(edge, 1.0, a_s)
        b_s = jnp.where(edge, 0.0, b_s)
        b = a * b_s + b
        a = a * a_s
        sft *= 2
    return a, b


def _lru_scan_fwd(name, a, b):
    s, w = a.shape
    t = SCAN_T

    def body(a_ref, b_ref, h_ref, hp_ref, carry):
        @pl.when(pl.program_id(0) == 0)
        def _():
            carry[...] = jnp.zeros_like(carry)

        h_in = carry[0:1]
        ca, cb = _block_scan(a_ref[...], b_ref[...], False)
        h = ca * h_in + cb
        h_ref[...] = h
        hp_ref[...] = jnp.where(_iota(h.shape, 0) == 0, h_in, pltpu.roll(h, 1, 0))
        carry[...] = jnp.broadcast_to(h[t - 1:t], carry.shape)

    return pl.pallas_call(
        body, grid=(s // t,), in_specs=[_rowspec(t, w)] * 2, out_specs=[_rowspec(t, w)] * 2,
        out_shape=[jax.ShapeDtypeStruct((s, w), F32)] * 2, scratch_shapes=[pltpu.VMEM((8, w), F32)],
        name=name, compiler_params=_params(("arbitrary",)))(a, b)


def _lru_scan_bwd(name, a, dh):
    s, w = a.shape
    t = SCAN_T
    n = s // t

    def body(a_ref, dh_ref, g_ref, a_next, g_next):
        @pl.when(pl.program_id(0) == 0)
        def _():
            a_next[...] = jnp.zeros_like(a_next)
            g_next[...] = jnp.zeros_like(g_next)

        av = a_ref[...]
        a_up = jnp.where(_iota(av.shape, 0) == t - 1, a_next[0:1], pltpu.roll(av, t - 1, 0))
        ca, cb = _block_scan(a_up, dh_ref[...], True)
        g = ca * g_next[0:1] + cb
        g_ref[...] = g
        a_next[...] = jnp.broadcast_to(av[0:1], a_next.shape)
        g_next[...] = jnp.broadcast_to(g[0:1], g_next.shape)

    rs = pl.BlockSpec((t, w), lambda i: (n - 1 - i, 0))
    return pl.pallas_call(
        body, grid=(n,), in_specs=[rs] * 2, out_specs=rs, out_shape=jax.ShapeDtypeStruct((s, w), F32),
        scratch_shapes=[pltpu.VMEM((8, w), F32), pltpu.VMEM((8, w), F32)],
        name=name, compiler_params=_params(("arbitrary",)))(a, dh)


def _loss_fn(g, x, tgt):
    y = x * lax.rsqrt(jnp.mean(x * x, axis=-1, keepdims=True) + EPS) * g
    err = y - tgt
    return 0.5 * jnp.sum(jnp.mean(err * err, axis=-1, keepdims=True), axis=0, keepdims=True)


def _loss_and_grad(name, g, x, tgt, t=256):
    s, d = x.shape

    def body(g_ref, x_ref, t_ref, loss_ref, dx_ref, dg_ref):
        val, vjp = jax.vjp(_loss_fn, g_ref[...], x_ref[...], t_ref[...])
        dg, dx, _ = vjp(jnp.ones((1, 1), F32))

        @pl.when(pl.program_id(0) == 0)
        def _():
            loss_ref[...] = jnp.zeros_like(loss_ref)
            dg_ref[...] = jnp.zeros_like(dg_ref)

        loss_ref[...] += jnp.broadcast_to(val, loss_ref.shape)
        dg_ref[...] += dg
        dx_ref[...] = dx

    return pl.pallas_call(
        body, grid=(s // t,), in_specs=[_whole(g.shape), _rowspec(t, d), _rowspec(t, d)],
        out_specs=[_whole((1, 128)), _rowspec(t, d), _whole(g.shape)],
        out_shape=[jax.ShapeDtypeStruct((1, 128), F32), jax.ShapeDtypeStruct((s, d), F32),
                   jax.ShapeDtypeStruct(g.shape, F32)],
        name=name, compiler_params=_params(("arbitrary",)))(g, x, tgt)


def _ew(name, fn, ins, n_out, t=None):
    r, c = ins[0].shape
    t = _tile(r, (512, 256, 128, 64, 32, 16, 8)) if t is None else t

    def body(*refs):
        outs = fn(*[x[...] for x in refs[:len(ins)]])
        for o_ref, o in zip(refs[len(ins):], outs):
            o_ref[...] = o

    return pl.pallas_call(
        body, grid=(r // t,), in_specs=[_rowspec(t, c)] * len(ins), out_specs=[_rowspec(t, c)] * n_out,
        out_shape=[jax.ShapeDtypeStruct((r, c), F32)] * n_out,
        name=name, compiler_params=_params(("arbitrary",)))(*ins)


def _adam_fn(w, g, m, v):
    m = B1 * m + (1.0 - B1) * g
    v = B2 * v + (1.0 - B2) * (g * g)
    m_hat = m / (1.0 - B1 ** STEP)
    v_hat = v / (1.0 - B2 ** STEP)
    delta = -LR * (m_hat / (jnp.sqrt(v_hat) + AEPS) + WD * w)
    return delta, m, v


def _add2_fn(a, b):
    return (a + b,)


def _add4_fn(a, b, c, d):
    return (((a + b) + c) + d,)


def _adam(name, w, g, m, v):
    shp = w.shape
    two = lambda z: z.reshape(-1, shp[-1])
    outs = _ew(name, _adam_fn, [two(w), two(g), two(m), two(v)], 3)
    return [o.reshape(shp) for o in outs]


def _swap(name, bufs, flips):
    n = len(bufs)
    hbm = pl.BlockSpec(memory_space=pltpu.HBM)

    def body(*refs):
        srcs, dsts = refs[:n], refs[n:2 * n]
        send_sems, recv_sems = refs[2 * n], refs[2 * n + 1]
        me = (lax.axis_index("x"), lax.axis_index("y"), lax.axis_index("c"))
        copies = []
        for i in range(n):
            peer = tuple(1 - m if f else m for m, f in zip(me, flips[i]))
            cp = pltpu.make_async_remote_copy(src_ref=srcs[i], dst_ref=dsts[i], send_sem=send_sems.at[i],
                                              recv_sem=recv_sems.at[i], device_id=peer, device_id_type=MESH)
            cp.start()
            copies.append(cp)
        for cp in copies:
            cp.wait()

    return pl.pallas_call(
        body, in_specs=[hbm] * n, out_specs=[hbm] * n,
        out_shape=[jax.ShapeDtypeStruct(b.shape, b.dtype) for b in bufs],
        scratch_shapes=[pltpu.SemaphoreType.DMA((n,)), pltpu.SemaphoreType.DMA((n,))],
        name=name, compiler_params=pltpu.CompilerParams(has_side_effects=True))(*bufs)


FLIP_C = (False, False, True)
FLIP_Y = (False, True, False)
FLIP_X = (True, False, False)
FLIP_XY = (True, True, False)
CHIP_FLIPS = (FLIP_Y, FLIP_X, FLIP_XY)


def _half(buf, which):
    r = buf.shape[-2] // 2
    return lax.dynamic_slice_in_dim(buf, which * r, r, axis=buf.ndim - 2)


def _join_halves(mine, other, core):
    r = mine.shape[0]
    out = lax.dynamic_update_slice(jnp.zeros((2 * r,) + mine.shape[1:], mine.dtype), mine, (core * r, 0))
    return lax.dynamic_update_slice(out, other, ((1 - core) * r, 0))


def _pack_big(w_in, w_out, w_up, w_down):
    pad = jnp.pad(w_in, ((0, 0), (0, 0), (0, 1024 - w_in.shape[-1])))
    return jnp.concatenate([pad.reshape(-1, 1024), w_up.reshape(-1, 1024), w_down.reshape(-1, 1024),
                            w_out.reshape(-1, 1024)], axis=0)


def _unpack_big(p):
    w_in = p[0:4096].reshape(4, 1024, 1024)[:, :, :D_IN // 4]
    w_up = p[4096:8192].reshape(4, 1024, 1024)
    w_down = p[8192:12288].reshape(4, 1024, 1024)
    w_out = p[12288:13312].reshape(4, 256, 1024)
    return w_in, w_out, w_up, w_down


def _gather_big(packed, chip, core):
    mine = _half(packed, core)
    got = _swap("gather_chips", [mine] * 3, CHIP_FLIPS)
    sib = _swap("gather_sibling", got, [FLIP_C] * 3)
    return [_join_halves(got[j], sib[j], core) for j in range(3)]


def _reduce_scatter_big(per_chip, chip, core):
    r = PACK_ROWS // 2
    give = lax.dynamic_slice_in_dim(per_chip, (1 - core) * r, r, axis=1)
    keep = lax.dynamic_slice_in_dim(per_chip, core * r, r, axis=1)
    (sib,) = _swap("rs_sibling", [give.reshape(4 * r, 1024)], [FLIP_C])
    (pair,) = _ew("rs_add_pair", _add2_fn, [keep.reshape(4 * r, 1024), sib], 1)
    pair = pair.reshape(4, r, 1024)
    sends = [lax.dynamic_index_in_dim(pair, chip ^ (j + 1), axis=0, keepdims=False) for j in range(3)]
    own = lax.dynamic_index_in_dim(pair, chip, axis=0, keepdims=False)
    got = _swap("rs_chips", sends, CHIP_FLIPS)
    (total,) = _ew("rs_add_chips", _add4_fn, [own] + list(got), 1)
    (other,) = _swap("rs_share", [total], [FLIP_C])
    return _join_halves(total, other, core)


def _allreduce_small(tag, buf):
    for name, flip in (("c", FLIP_C), ("y", FLIP_Y), ("x", FLIP_X)):
        (other,) = _swap(tag + "_swap_" + name, [buf], [flip])
        (buf,) = _ew(tag + "_add_" + name, _add2_fn, [buf, other], 1)
    return buf


def _permute_in_cols(w):
    z = jnp.zeros(w.shape[:-1] + (D_IN_PAD - D_IN,), w.dtype)
    return jnp.concatenate([w[..., 1024:3072], w[..., :1024], w[..., 3080:D_IN], w[..., 3072:3080], z], axis=-1)


def _unpermute_in_cols(w):
    return jnp.concatenate([w[..., 2048:3072], w[..., :2048], w[..., GATE0:GATE0 + 8], w[..., 3072:GATE0]], axis=-1)


def _block_diag(w):
    out = jnp.zeros((C_W, C_W), w.dtype)
    for n in range(4):
        out = lax.dynamic_update_slice(out, w[n], (64 * n, 64 * n))
    return out


def _diag_blocks(w):
    return jnp.stack([w[64 * n:64 * (n + 1), 64 * n:64 * (n + 1)] for n in range(4)], axis=0)


def _gate_row(v):
    return jnp.concatenate([jnp.zeros((4,), F32), v, jnp.zeros((120,), F32)]).reshape(1, 128)


SMALL = ("norm1_g", "hgrn_lb_logits", "hgrn_norm_g", "gdn_conv_w", "gdn_a_log", "gdn_dt_bias", "gdn_norm_g",
         "lru_conv_w", "lru_conv_b", "lru_w_a", "lru_b_a", "lru_w_x", "lru_b_x", "lru_lambda", "norm2_g",
         "final_norm_g")


def _flatten_small(tree):
    flat = jnp.concatenate([tree[k].reshape(-1) for k in sorted(tree)])
    rows = -(-flat.shape[0] // 1024) * 8
    return jnp.pad(flat, (0, rows * 128 - flat.shape[0])).reshape(rows, 128)


def _unflatten_small(buf, shapes):
    flat = buf.reshape(-1)
    out, off = {}, 0
    for k in sorted(shapes):
        size = 1
        for d in shapes[k]:
            size *= d
        out[k] = flat[off:off + size].reshape(shapes[k])
        off += size
    return out


def _layer_fwd(l, x, p, w_in, w_out, w_up, w_down):
    sv = {"x0": x}
    (h,) = _rowwise_fwd("norm1", _rmsnorm_fn, [p["norm1_g"]], [x], [D_MODEL], [BF16], 512)
    proj = _matmul("proj_in", h, w_in, "nn")
    sv["h"], sv["proj"] = h, proj
    aq, af, ag = _cols(proj, COL_AQ, A_W), _cols(proj, COL_AF, A_W), _cols(proj, COL_AG, A_W)
    ai = lax.slice_in_dim(proj, COL_AI, COL_AI + A_W, axis=1)
    bz, gate, cy = _cols(proj, COL_BZ, B_W), _cols(proj, GATE0, 128), _cols(proj, COL_CY, C_W)
    q, k, lf = _rowwise_fwd("hgrn_pre", functools.partial(_hgrn_pre_fn, l), [p["lb_logits"]], [aq, af],
                            [A_W] * 3, [F32] * 3, 512)
    o_a, sts_a = _hgrn_fwd("hgrn_chunk", q, k, ai, lf)
    (y_a,) = _rowwise_fwd("hgrn_post", _hgrn_post_fn, [p["hgrn_norm_g"]], [o_a, ag], [A_W], [BF16], 512)
    sv.update(aq=aq, af=af, ai=ai, ag=ag, hq=q, hk=k, hlf=lf, o_a=o_a, sts_a=sts_a)
    qkvc = _conv_fwd("gdn_conv", proj, 0, 3 * B_W, p["gdn_conv_w"], jnp.zeros((1, 3 * B_W), F32), 256)
    gq, gk, gv, gb, gla = _rowwise_fwd("gdn_pre", _gdn_pre_fn, [p["gdn_a_log"], p["gdn_dt_bias"]], [qkvc, gate],
                                       [B_W] * 5, [F32] * 5, 256)
    o_b, sts_b = _gdn_fwd("gdn_chunk", gq, gk, gv, gb, gla)
    (y_b,) = _rowwise_fwd("gdn_post", _gdn_post_fn, [p["gdn_norm_g"]], [o_b, bz], [B_W], [BF16], 512)
    sv.update(qkvc=qkvc, gate=gate, bz=bz, gq=gq, gk=gk, gv=gv, gb=gb, gla=gla, o_b=o_b, sts_b=sts_b)
    xc = _conv_fwd("lru_conv", proj, COL_CX, C_W, p["lru_conv_w"], p["lru_conv_b"], 512)
    a, b = _rowwise_fwd("lru_pre", _lru_pre_fn, [p["lru_w_a"], p["lru_b_a"], p["lru_w_x"], p["lru_b_x"], p["lru_lambda"]],
                        [xc], [C_W] * 2, [F32] * 2, 512)
    hs, h_prev = _lru_scan_fwd("lru_scan", a, b)
    (y_c,) = _rowwise_fwd("lru_post", _lru_post_fn, [], [hs, cy], [C_W], [BF16], 512)
    sv.update(xc=xc, la=a, hs=hs, h_prev=h_prev, cy=cy)
    mixed = jnp.concatenate([y_a, y_b, y_c], axis=1)
    x1 = _matmul("proj_out", mixed, w_out, "nn", res=x)
    (h2,) = _rowwise_fwd("norm2", _rmsnorm_fn, [p["norm2_g"]], [x1], [D_MODEL], [BF16], 512)
    up = _matmul("mlp_up", h2, w_up, "nn")
    x2 = _matmul("mlp_down", up, w_down, "nn", a_fn=_relu2, res=x1)
    sv.update(mixed=mixed, x1=x1, h2=h2, up=up)
    return x2, sv


def _layer_bwd(l, dx, p, sv, w_in, w_out, w_up, w_down):
    gs = {}
    dup = _matmul("mlp_down_dx", dx, w_down, "nt", epi=_drelu2, epi_in=sv["up"])
    d_w_down = _matmul("mlp_down_dw", sv["up"], dx, "tn", a_fn=_relu2)
    dh2 = _matmul("mlp_up_dx", dup, w_up, "nt")
    d_w_up = _matmul("mlp_up_dw", sv["h2"], dup, "tn")
    (gs["norm2_g"],), (dx1,) = _rowwise_bwd("norm2_bwd", _rmsnorm_fn, [p["norm2_g"]], [sv["x1"]], [dh2], 512, res=dx)
    dmixed = _matmul("proj_out_dx", dx1, w_out, "nt")
    d_w_out = _matmul("proj_out_dw", sv["mixed"], dx1, "tn")
    dy_a, dy_b, dy_c = dmixed[:, :A_W], dmixed[:, A_W:A_W + B_W], dmixed[:, A_W + B_W:]
    (gs["hgrn_norm_g"],), (do_a, dag) = _rowwise_bwd("hgrn_post_bwd", _hgrn_post_fn, [p["hgrn_norm_g"]],
                                                     [sv["o_a"], sv["ag"]], [dy_a], 512)
    dq, dk, dai, dlf = _hgrn_bwd("hgrn_chunk_bwd", sv["hq"], sv["hk"], sv["ai"], sv["hlf"], sv["sts_a"], do_a)
    (gs["lb_logits"],), (daq, daf) = _rowwise_bwd("hgrn_pre_bwd", functools.partial(_hgrn_pre_fn, l), [p["lb_logits"]],
                                                  [sv["aq"], sv["af"]], [dq, dk, dlf], 512)
    (gs["gdn_norm_g"],), (do_b, dbz) = _rowwise_bwd("gdn_post_bwd", _gdn_post_fn, [p["gdn_norm_g"]],
                                                    [sv["o_b"], sv["bz"]], [dy_b], 512)
    dgq, dgk, dgv, dgb, dgla = _gdn_bwd("gdn_chunk_bwd", sv["gq"], sv["gk"], sv["gv"], sv["gb"], sv["gla"], sv["sts_b"], do_b)
    (gs["gdn_a_log"], gs["gdn_dt_bias"]), (dqkvc, dgate) = _rowwise_bwd(
        "gdn_pre_bwd", _gdn_pre_fn, [p["gdn_a_log"], p["gdn_dt_bias"]], [sv["qkvc"], sv["gate"]],
        [dgq, dgk, dgv, dgb, dgla], 256)
    dqkv, gs["gdn_conv_w"], _ = _conv_bwd("gdn_conv_bwd", dqkvc, sv["proj"], 0, 3 * B_W, p["gdn_conv_w"], 256)
    _, (dhs, dcy) = _rowwise_bwd("lru_post_bwd", _lru_post_fn, [], [sv["hs"], sv["cy"]], [dy_c], 512)
    g = _lru_scan_bwd("lru_scan_bwd", sv["la"], dhs)
    (da,) = _rowwise_fwd("lru_da", _lru_da_fn, [], [g, sv["h_prev"]], [C_W], [F32], 512)
    lru_params = [p["lru_w_a"], p["lru_b_a"], p["lru_w_x"], p["lru_b_x"], p["lru_lambda"]]
    dps, (dxc,) = _rowwise_bwd("lru_pre_bwd", _lru_pre_fn, lru_params, [sv["xc"]], [da, g], 512)
    gs["lru_w_a"], gs["lru_b_a"], gs["lru_w_x"], gs["lru_b_x"], gs["lru_lambda"] = dps
    dcx, gs["lru_conv_w"], gs["lru_conv_b"] = _conv_bwd("lru_conv_bwd", dxc, sv["proj"], COL_CX, C_W, p["lru_conv_w"], 512)
    s = dx.shape[0]
    dproj = jnp.concatenate([dqkv, dbz, daq, daf, dai, dag, dcx, dcy, dgate,
                             jnp.zeros((s, D_IN_PAD - GATE0 - 128), F32)], axis=1)
    dh = _matmul("proj_in_dx", dproj, w_in, "nt")
    d_w_in = _matmul("proj_in_dw", sv["h"], dproj, "tn")
    (gs["norm1_g"],), (dx0,) = _rowwise_bwd("norm1_bwd", _rmsnorm_fn, [p["norm1_g"]], [sv["x0"]], [dh], 512, res=dx1)
    return dx0, dict(w_in=d_w_in, w_out=d_w_out, w_up=d_w_up, w_down=d_w_down), gs


def _layer_params(l, lb_logits, hgrn_norm_g, gdn_conv_w, gdn_a_log, gdn_dt_bias, gdn_norm_g, lru_conv_w, lru_conv_b,
                  lru_w_a, lru_b_a, lru_w_x, lru_b_x, lru_lambda, norm1_g, norm2_g):
    row = lambda v: v.reshape(1, -1)
    return dict(
        norm1_g=row(norm1_g[l]), norm2_g=row(norm2_g[l]), lb_logits=lb_logits,
        hgrn_norm_g=row(jnp.tile(hgrn_norm_g[l], A_W // A_HD)),
        gdn_conv_w=gdn_conv_w[l], gdn_a_log=_gate_row(gdn_a_log[l]), gdn_dt_bias=_gate_row(gdn_dt_bias[l]),
        gdn_norm_g=row(jnp.tile(gdn_norm_g[l], B_NH)),
        lru_conv_w=lru_conv_w[l], lru_conv_b=row(lru_conv_b[l]), lru_w_a=_block_diag(lru_w_a[l]), lru_b_a=row(lru_b_a[l]),
        lru_w_x=_block_diag(lru_w_x[l]), lru_b_x=row(lru_b_x[l]), lru_lambda=row(lru_lambda[l]))


def _small_grads_to_reference(gs_layers, d_final_g, d_logits):
    st = lambda k, f=lambda z: z: jnp.stack([f(g[k]) for g in gs_layers], axis=0)
    vec = lambda z: z.reshape(-1)
    return dict(
        norm1_g=st("norm1_g", vec), norm2_g=st("norm2_g", vec), hgrn_lb_logits=d_logits,
        hgrn_norm_g=st("hgrn_norm_g", lambda z: z.reshape(A_W // A_HD, A_HD).sum(0)),
        gdn_conv_w=st("gdn_conv_w"), gdn_a_log=st("gdn_a_log", lambda z: z[0, 4:8]),
        gdn_dt_bias=st("gdn_dt_bias", lambda z: z[0, 4:8]),
        gdn_norm_g=st("gdn_norm_g", lambda z: z.reshape(B_NH, B_HD).sum(0)),
        lru_conv_w=st("lru_conv_w"), lru_conv_b=st("lru_conv_b", vec), lru_w_a=st("lru_w_a", _diag_blocks),
        lru_b_a=st("lru_b_a", vec), lru_w_x=st("lru_w_x", _diag_blocks), lru_b_x=st("lru_b_x", vec),
        lru_lambda=st("lru_lambda", vec), final_norm_g=d_final_g.reshape(-1))


def _local_step(x, tgt, small, w_in_full, w_out_full, w_up_full, w_down_full):
    layer_p = [_layer_params(l, small["hgrn_lb_logits"], small["hgrn_norm_g"], small["gdn_conv_w"], small["gdn_a_log"],
                             small["gdn_dt_bias"], small["gdn_norm_g"], small["lru_conv_w"], small["lru_conv_b"],
                             small["lru_w_a"], small["lru_b_a"], small["lru_w_x"], small["lru_b_x"], small["lru_lambda"],
                             small["norm1_g"], small["norm2_g"]) for l in range(DEPTH)]
    saved = []
    for l in range(DEPTH):
        x, sv = _layer_fwd(l, x, layer_p[l], w_in_full[l], w_out_full[l], w_up_full[l], w_down_full[l])
        saved.append(sv)
    loss, dx, d_final_g = _loss_and_grad("final_loss", small["final_norm_g"].reshape(1, -1), x, tgt)
    big = [None] * DEPTH
    gs_layers = [None] * DEPTH
    d_logits = jnp.zeros_like(small["hgrn_lb_logits"])
    for l in reversed(range(DEPTH)):
        dx, big[l], gs_layers[l] = _layer_bwd(l, dx, layer_p[l], saved[l], w_in_full[l], w_out_full[l], w_up_full[l],
                                              w_down_full[l])
        d_logits = d_logits + gs_layers[l]["lb_logits"]
    return loss, dx, big, _small_grads_to_reference(gs_layers, d_final_g, d_logits)


def kernel(x, norm1_g, w_in, hgrn_lb_logits, hgrn_norm_g, gdn_conv_w, gdn_a_log, gdn_dt_bias, gdn_norm_g, lru_conv_w, lru_conv_b, lru_w_a, lru_b_a, lru_w_x, lru_b_x, lru_lambda, w_out, norm2_g, w_up, w_down, final_norm_g, loss_target, m_norm1_g, m_w_in, m_hgrn_lb_logits, m_hgrn_norm_g, m_gdn_conv_w, m_gdn_a_log, m_gdn_dt_bias, m_gdn_norm_g, m_lru_conv_w, m_lru_conv_b, m_lru_w_a, m_lru_b_a, m_lru_w_x, m_lru_b_x, m_lru_lambda, m_w_out, m_norm2_g, m_w_up, m_w_down, m_final_norm_g, v_norm1_g, v_w_in, v_hgrn_lb_logits, v_hgrn_norm_g, v_gdn_conv_w, v_gdn_a_log, v_gdn_dt_bias, v_gdn_norm_g, v_lru_conv_w, v_lru_conv_b, v_lru_w_a, v_lru_b_a, v_lru_w_x, v_lru_b_x, v_lru_lambda, v_w_out, v_norm2_g, v_w_up, v_w_down, v_final_norm_g):
    args = dict(locals())
    ix, iy, core = lax.axis_index("x"), lax.axis_index("y"), lax.axis_index("c")
    chip = 2 * ix + iy

    packed = _pack_big(w_in, w_out, w_up, w_down).astype(BF16)
    others = _gather_big(packed, chip, core)
    shares = jnp.stack([packed] + others, axis=0)
    by_chip = [lax.dynamic_index_in_dim(shares, chip ^ k, axis=0, keepdims=False) for k in range(4)]
    parts = [_unpack_big(b) for b in by_chip]
    w_in_full = _permute_in_cols(jnp.concatenate([pt[0] for pt in parts], axis=-1))
    w_out_full = jnp.concatenate([pt[1] for pt in parts], axis=1)
    w_up_full = jnp.concatenate([pt[2] for pt in parts], axis=-1)
    w_down_full = jnp.concatenate([pt[3] for pt in parts], axis=1)

    n_gc, n_lc = gdn_conv_w.shape[-1], lru_conv_w.shape[-1]
    conv_full = dict(
        gdn_conv_w=lax.dynamic_update_slice(jnp.zeros((DEPTH, 4, 4 * n_gc), F32), gdn_conv_w * 0.5, (0, 0, chip * n_gc)),
        lru_conv_w=lax.dynamic_update_slice(jnp.zeros((DEPTH, 4, 4 * n_lc), F32), lru_conv_w * 0.5, (0, 0, chip * n_lc)))
    conv_shapes = {k: v.shape for k, v in conv_full.items()}
    conv_full = _unflatten_small(_allreduce_small("convw", _flatten_small(conv_full)), conv_shapes)

    small = {k: args[k] for k in SMALL}
    small.update(conv_full)

    loss, grad_x, big, sg = _local_step(x[0], loss_target[0], small, w_in_full, w_out_full, w_up_full, w_down_full)

    sg["loss"] = loss[0, :1]
    shapes = {k: v.shape for k, v in sg.items()}
    sg = _unflatten_small(_allreduce_small("smallgrad", _flatten_small(sg)), shapes)
    loss_out = sg.pop("loss")[0]

    d_w_in = _unpermute_in_cols(jnp.stack([b["w_in"] for b in big], axis=0))
    d_w_out = jnp.stack([b["w_out"] for b in big], axis=0)
    d_w_up = jnp.stack([b["w_up"] for b in big], axis=0)
    d_w_down = jnp.stack([b["w_down"] for b in big], axis=0)
    n_in = D_IN // 4
    per_chip = jnp.stack([
        _pack_big(d_w_in[:, :, k * n_in:(k + 1) * n_in], d_w_out[:, k * 256:(k + 1) * 256],
                  d_w_up[:, :, k * 1024:(k + 1) * 1024], d_w_down[:, k * 1024:(k + 1) * 1024]) for k in range(4)], axis=0)
    g_in, g_out, g_up, g_down = _unpack_big(_reduce_scatter_big(per_chip, chip, core))

    grads = dict(sg)
    grads["gdn_conv_w"] = lax.dynamic_slice_in_dim(sg["gdn_conv_w"], chip * n_gc, n_gc, axis=2)
    grads["lru_conv_w"] = lax.dynamic_slice_in_dim(sg["lru_conv_w"], chip * n_lc, n_lc, axis=2)
    grads.update(w_in=g_in, w_out=g_out, w_up=g_up, w_down=g_down)

    names = ['norm1_g', 'w_in', 'hgrn_lb_logits', 'hgrn_norm_g', 'gdn_conv_w', 'gdn_a_log', 'gdn_dt_bias', 'gdn_norm_g',
             'lru_conv_w', 'lru_conv_b', 'lru_w_a', 'lru_b_a', 'lru_w_x', 'lru_b_x', 'lru_lambda', 'w_out', 'norm2_g',
             'w_up', 'w_down', 'final_norm_g']
    big_names = ("w_in", "w_out", "w_up", "w_down")
    delta, new_m, new_v = {}, {}, {}
    for k in big_names:
        delta[k], new_m[k], new_v[k] = _adam("adam_" + k, args[k], grads[k], args["m_" + k], args["v_" + k])
    small_names = [k for k in names if k not in big_names]
    shapes = {k: args[k].shape for k in small_names}
    flat = [_flatten_small({k: src(k) for k in small_names})
            for src in (lambda k: args[k], lambda k: grads[k], lambda k: args["m_" + k], lambda k: args["v_" + k])]
    outs = _ew("adam_small", _adam_fn, flat, 3)
    for dst, o in zip((delta, new_m, new_v), outs):
        dst.update(_unflatten_small(o, shapes))
    return (loss_out, grad_x[None], *[grads[k] for k in names], *[delta[k] for k in names],
            *[new_m[k] for k in names], *[new_v[k] for k in names])
```

```python
import functools

import jax
import jax.numpy as jnp
from jax import lax
from jax.experimental import pallas as pl
from jax.experimental.pallas import tpu as pltpu

F32 = jnp.float32
BF16 = jnp.bfloat16
MESH = pl.DeviceIdType.MESH

DEPTH = 4
D_MODEL = 1024
A_W, B_W, C_W = 256, 512, 256
A_HD, B_HD = 64, 128
B_NH = B_W // B_HD
D_IN = 3592
D_IN_PAD = 3840
COL_BZ, COL_AQ, COL_AF, COL_AI, COL_AG, COL_CX, COL_CY = 1536, 2048, 2304, 2560, 2816, 3072, 3328
GATE0 = 3584
D_FF = 4096
CHUNK = 64
SUB = 16
EPS = 1e-6
TINY = 1e-30
RG_C = 8.0
LR, B1, B2, AEPS, WD, STEP = 0.001, 0.9, 0.999, 1e-08, 0.01, 10
VMEM_LIMIT = 56 * 1024 * 1024
PACK_ROWS = 13 * 1024

NN = ((1,), (0,))
NT = ((1,), (1,))
TN = ((0,), (0,))


DIMS = {"nn": NN, "nt": NT, "tn": TN}


def _split(a, n):
    parts, r = [], a
    for i in range(n):
        p = r.astype(BF16)
        parts.append(p)
        if i + 1 < n:
            r = r - p.astype(F32)
    return parts


def _split_dot(a, b, mode, na, nb):
    (ca,), (cb,) = DIMS[mode]
    dims = ((DIMS[mode]), ((), ())) if a.ndim == 2 else (((ca + 1,), (cb + 1,)), ((0,), (0,)))
    pa, pb = _split(a, na), _split(b, nb)
    acc = None
    for i in range(na):
        for j in range(nb):
            if i + j < max(na, nb):
                t = lax.dot_general(pa[i], pb[j], dims, preferred_element_type=F32)
                acc = t if acc is None else acc + t
    return acc


@functools.partial(jax.custom_vjp, nondiff_argnums=(2, 3, 4, 5))
def _mdot(a, b, mode, na, nb, need):
    return _split_dot(a, b, mode, na, nb)


def _mdot_fwd(a, b, mode, na, nb, need):
    return _split_dot(a, b, mode, na, nb), (a, b)


def _mdot_bwd(mode, na, nb, need, res, ct):
    a, b = res
    n = max(na, nb)
    da, db = jnp.zeros_like(a), jnp.zeros_like(b)
    if mode == "nn":
        if need[0]:
            da = _split_dot(ct, b, "nt", n, nb)
        if need[1]:
            db = _split_dot(a, ct, "tn", na, n)
    elif mode == "nt":
        if need[0]:
            da = _split_dot(ct, b, "nn", n, nb)
        if need[1]:
            db = _split_dot(ct, a, "tn", n, na)
    else:
        if need[0]:
            da = _split_dot(b, ct, "nt", nb, n)
        if need[1]:
            db = _split_dot(a, ct, "nn", na, n)
    return da, db


_mdot.defvjp(_mdot_fwd, _mdot_bwd)
BOTH = (True, True)


def _bdot(a, b, mode="nn"):
    return _mdot(a, b, mode, 1, 1, BOTH)


def _group_sum(x, ones):
    return _mdot(x, ones, "nn", 2, 1, (True, False))


def _running_sum(x):
    shape = x.shape[:-2] + (CHUNK, CHUNK)
    tri = (_iota(shape, len(shape) - 1) <= _iota(shape, len(shape) - 2)).astype(F32)
    return _mdot(tri, x, "nn", 1, 3, (False, True))


def _iota(shape, d):
    return lax.broadcasted_iota(jnp.int32, shape, d)


def _block_ones(width, group):
    return (_iota((width, width), 0) // group == _iota((width, width), 1) // group).astype(F32)


def _params(sem):
    return pltpu.CompilerParams(dimension_semantics=sem, vmem_limit_bytes=VMEM_LIMIT)


def _tile(dim, prefs):
    for p in prefs:
        if dim % p == 0:
            return p
    return dim


def _whole(shape):
    return pl.BlockSpec(shape, lambda i: (0,) * len(shape))


def _rowspec(t, w, col0=0):
    cb = col0 // w
    return pl.BlockSpec((t, w), lambda i: (i, cb))


def _cols(arr, col0, width):
    assert col0 % width == 0
    return (arr, col0, width)


def _row_operands(rows, t):
    arrs, specs = [], []
    for r in rows:
        arr, col0, w = r if isinstance(r, tuple) else (r, 0, r.shape[1])
        arrs.append(arr)
        specs.append(_rowspec(t, w, col0))
    return arrs, specs


def _rowwise_fwd(name, fn, params, rows, out_widths, out_dtypes, t):
    rows, row_specs = _row_operands(rows, t)
    s = rows[0].shape[0]
    n_p, n_r = len(params), len(rows)

    def body(*refs):
        p = [r[...] for r in refs[:n_p]]
        xs = [r[...].astype(F32) for r in refs[n_p:n_p + n_r]]
        outs = fn(pl.program_id(0) * t, *p, *xs)
        for o_ref, o in zip(refs[n_p + n_r:], outs):
            o_ref[...] = o.astype(o_ref.dtype)

    return pl.pallas_call(
        body, grid=(s // t,),
        in_specs=[_whole(p.shape) for p in params] + row_specs,
        out_specs=[_rowspec(t, w) for w in out_widths],
        out_shape=[jax.ShapeDtypeStruct((s, w), d) for w, d in zip(out_widths, out_dtypes)],
        name=name, compiler_params=_params(("arbitrary",)))(*params, *rows)


def _rowwise_bwd(name, fn, params, rows, douts, t, need=None, res=None):
    rows, row_specs = _row_operands(rows, t)
    s = rows[0].shape[0]
    n_p, n_r, n_o = len(params), len(rows), len(douts)
    need = [True] * n_r if need is None else need
    n_res = 0 if res is None else 1

    def body(*refs):
        p = [r[...] for r in refs[:n_p]]
        xs = [r[...].astype(F32) for r in refs[n_p:n_p + n_r]]
        dys = [r[...].astype(F32) for r in refs[n_p + n_r:n_p + n_r + n_o]]
        k = n_p + n_r + n_o
        res_ref = refs[k] if n_res else None
        dp_refs = refs[k + n_res:k + n_res + n_p]
        dx_refs = refs[k + n_res + n_p:]
        row0 = pl.program_id(0) * t
        _, vjp = jax.vjp(lambda *a: tuple(fn(row0, *a)), *p, *xs)
        g = vjp(tuple(dys))

        @pl.when(pl.program_id(0) == 0)
        def _():
            for r in dp_refs:
                r[...] = jnp.zeros_like(r)

        for r, gp in zip(dp_refs, g[:n_p]):
            r[...] += gp
        gx = [gi for gi, nd in zip(g[n_p:], need) if nd]
        for j, (r, gi) in enumerate(zip(dx_refs, gx)):
            if j == 0 and n_res:
                gi = gi + res_ref[...]
            r[...] = gi

    widths = [sp.block_shape[1] for sp, nd in zip(row_specs, need) if nd]
    outs = pl.pallas_call(
        body, grid=(s // t,),
        in_specs=([_whole(p.shape) for p in params] + row_specs
                  + [_rowspec(t, d.shape[1]) for d in douts] + ([_rowspec(t, res.shape[1])] if n_res else [])),
        out_specs=[_whole(p.shape) for p in params] + [_rowspec(t, w) for w in widths],
        out_shape=[jax.ShapeDtypeStruct(p.shape, F32) for p in params] + [jax.ShapeDtypeStruct((s, w), F32) for w in widths],
        name=name, compiler_params=_params(("arbitrary",)))(*params, *rows, *douts, *([res] if n_res else []))
    return list(outs[:n_p]), list(outs[n_p:])


def _rmsnorm_fn(row0, g, x):
    y = x * lax.rsqrt(jnp.mean(x * x, axis=-1, keepdims=True) + EPS)
    return (y * g,)


def _hgrn_pre_fn(layer, row0, logits, aq, af):
    m = jnp.max(logits, axis=0, keepdims=True)
    e = jnp.exp(logits - m)
    p = e / jnp.sum(e, axis=0, keepdims=True)
    if layer == 0:
        lb = jnp.zeros((1, A_W), F32)
    else:
        acc = p[1:2]
        for j in range(2, layer + 1):
            acc = acc + p[j:j + 1]
        lb = jnp.minimum(jnp.maximum(acc, 0.0), 1.0 - EPS)
    sig = jax.nn.sigmoid(af)
    f = lb + (1.0 - lb) * sig
    log_f = jnp.log(jnp.maximum(f, TINY))
    k = (1.0 - lb) * jax.nn.sigmoid(-af)
    return jax.nn.silu(aq), k, log_f


def _hgrn_post_fn(row0, g, o, ag):
    ms = _group_sum(o * o, _block_ones(A_W, A_HD)) * (1.0 / A_HD)
    return (o * lax.rsqrt(ms + EPS) * g * jax.nn.silu(ag),)


def _gdn_pre_fn(row0, a_log, dt_bias, qkvc, gate):
    act = jax.nn.silu(qkvc)
    q, k, v = act[:, :B_W], act[:, B_W:2 * B_W], act[:, 2 * B_W:]
    bo = _block_ones(B_W, B_HD)
    q = q * lax.rsqrt(_group_sum(q * q, bo) + EPS) * (B_HD ** -0.5)
    k = k * lax.rsqrt(_group_sum(k * k, bo) + EPS)
    beta = jax.nn.sigmoid(gate)
    log_alpha = -jnp.exp(a_log) * jax.nn.softplus(gate + dt_bias)
    t = gate.shape[0]
    spread = lambda z, lane0: jnp.concatenate(
        [jnp.broadcast_to(z[:, lane0 + h:lane0 + h + 1], (t, B_HD)) for h in range(B_NH)], axis=1)
    return q, k, v, spread(beta, 0), spread(log_alpha, B_NH)


def _gdn_post_fn(row0, g, o, bz):
    ms = _group_sum(o * o, _block_ones(B_W, B_HD)) * (1.0 / B_HD)
    return (o * lax.rsqrt(ms + EPS) * g * jax.nn.silu(bz),)


def _lru_pre_fn(row0, w_a, b_a, w_x, b_x, lam, xc):
    r = jax.nn.sigmoid(_mdot(xc, w_a, "nn", 2, 2, BOTH) + b_a)
    i = jax.nn.sigmoid(_mdot(xc, w_x, "nn", 2, 2, BOTH) + b_x)
    log_a = -RG_C * r * jax.nn.softplus(-lam)
    a = jnp.exp(log_a)
    one_minus = -jnp.tanh(log_a) * (a * a + 1.0)
    mult = jnp.sqrt(jnp.maximum(one_minus, EPS))
    pos = row0 + _iota(xc.shape, 0)
    mult = jnp.where(pos == 0, 1.0, mult)
    return a, mult * i * xc


def _lru_post_fn(row0, h, cy):
    return (jax.nn.gelu(cy, approximate=True) * h,)


def _lru_da_fn(row0, g, h_prev):
    return (g * h_prev,)


def _relu2(u):
    r = jnp.maximum(u, 0.0)
    return r * r


def _drelu2(acc, u):
    return acc * (2.0 * jnp.maximum(u, 0.0))


def _matmul(name, a, b, mode, a_fn=None, res=None, epi=None, epi_in=None, out_dtype=F32):
    if mode == "nn":
        (m, k), n = a.shape, b.shape[1]
    elif mode == "nt":
        (m, k), n = a.shape, b.shape[0]
    else:
        (k, m), n = a.shape, b.shape[1]
    tm = _tile(m, (512, 256, 128))
    tn = _tile(n, (1024, 768, 512, 256, 128))
    tk = _tile(k, (1024, 768, 512, 256, 128)) if mode != "tn" else _tile(k, (512,))
    nk = k // tk
    a_spec = {"nn": pl.BlockSpec((tm, tk), lambda i, j, kk: (i, kk)),
              "nt": pl.BlockSpec((tm, tk), lambda i, j, kk: (i, kk)),
              "tn": pl.BlockSpec((tk, tm), lambda i, j, kk: (kk, i))}[mode]
    b_spec = {"nn": pl.BlockSpec((tk, tn), lambda i, j, kk: (kk, j)),
              "nt": pl.BlockSpec((tn, tk), lambda i, j, kk: (j, kk)),
              "tn": pl.BlockSpec((tk, tn), lambda i, j, kk: (kk, j))}[mode]
    o_spec = pl.BlockSpec((tm, tn), lambda i, j, kk: (i, j))
    dims = {"nn": NN, "nt": NT, "tn": TN}[mode]
    extra = [x for x in (res, epi_in) if x is not None]

    def body(*refs):
        a_ref, b_ref = refs[0], refs[1]
        rest = list(refs[2:])
        res_ref = rest.pop(0) if res is not None else None
        epi_ref = rest.pop(0) if epi_in is not None else None
        o_ref, acc = rest
        kk = pl.program_id(2)

        @pl.when(kk == 0)
        def _():
            acc[...] = jnp.zeros_like(acc)

        at = a_ref[...]
        if a_fn is not None:
            at = a_fn(at.astype(F32))
        acc[...] += lax.dot_general(at.astype(BF16), b_ref[...].astype(BF16), (dims, ((), ())),
                                    preferred_element_type=F32)

        @pl.when(kk == nk - 1)
        def _():
            out = acc[...]
            if res_ref is not None:
                out = out + res_ref[...]
            if epi is not None:
                out = epi(out, epi_ref[...])
            o_ref[...] = out.astype(o_ref.dtype)

    return pl.pallas_call(
        body, grid=(m // tm, n // tn, nk),
        in_specs=[a_spec, b_spec] + [o_spec] * len(extra), out_specs=o_spec,
        out_shape=jax.ShapeDtypeStruct((m, n), out_dtype),
        scratch_shapes=[pltpu.VMEM((tm, tn), F32)],
        name=name, compiler_params=_params(("arbitrary", "arbitrary", "arbitrary")))(a, b, *extra)


HALO = 8


def _conv_fwd(name, x, col0, width, w, bias, t=512):
    s = x.shape[0]
    cb = col0 // width
    hb = t // HALO

    def body(x_ref, halo_ref, w_ref, b_ref, y_ref):
        i = pl.program_id(0)
        halo = jnp.where(i == 0, 0.0, halo_ref[...])
        xp = jnp.concatenate([halo, x_ref[...]], axis=0)
        wv = w_ref[...]
        y = b_ref[...] + wv[0:1] * xp[HALO - 3:HALO - 3 + t]
        for j in range(1, 4):
            y = y + wv[j:j + 1] * xp[HALO - 3 + j:HALO - 3 + j + t]
        y_ref[...] = y

    return pl.pallas_call(
        body, grid=(s // t,),
        in_specs=[pl.BlockSpec((t, width), lambda i: (i, cb)),
                  pl.BlockSpec((HALO, width), lambda i: (jnp.maximum(i * hb - 1, 0), cb)),
                  _whole(w.shape), _whole(bias.shape)],
        out_specs=_rowspec(t, width), out_shape=jax.ShapeDtypeStruct((s, width), F32),
        name=name, compiler_params=_params(("arbitrary",)))(x, x, w, bias)


def _conv_bwd(name, dy, x, col0, width, w, t=512):
    s = x.shape[0]
    cb = col0 // width
    hb = t // HALO
    nblk = s // t

    def body(dy_ref, dyn_ref, x_ref, halo_ref, w_ref, dx_ref, dw_ref, db_ref):
        i = pl.program_id(0)
        dyv = dy_ref[...]
        nxt = jnp.where(i == nblk - 1, 0.0, dyn_ref[...])
        dyp = jnp.concatenate([dyv, nxt], axis=0)
        wv = w_ref[...]
        dx = wv[3:4] * dyv
        for j in range(3):
            dx = dx + wv[j:j + 1] * dyp[3 - j:3 - j + t]
        dx_ref[...] = dx
        halo = jnp.where(i == 0, 0.0, halo_ref[...])
        xp = jnp.concatenate([halo, x_ref[...]], axis=0)
        dw = jnp.concatenate(
            [jnp.sum(dyv * xp[HALO - 3 + j:HALO - 3 + j + t], axis=0, keepdims=True) for j in range(4)], axis=0)

        @pl.when(i == 0)
        def _():
            dw_ref[...] = jnp.zeros_like(dw_ref)
            db_ref[...] = jnp.zeros_like(db_ref)

        dw_ref[...] += dw
        db_ref[...] += jnp.sum(dyv, axis=0, keepdims=True)

    return pl.pallas_call(
        body, grid=(nblk,),
        in_specs=[_rowspec(t, width),
                  pl.BlockSpec((HALO, width), lambda i: (jnp.minimum((i + 1) * hb, s // HALO - 1), 0)),
                  pl.BlockSpec((t, width), lambda i: (i, cb)),
                  pl.BlockSpec((HALO, width), lambda i: (jnp.maximum(i * hb - 1, 0), cb)),
                  _whole(w.shape)],
        out_specs=[_rowspec(t, width), _whole((4, width)), _whole((1, width))],
        out_shape=[jax.ShapeDtypeStruct((s, width), F32), jax.ShapeDtypeStruct((4, width), F32),
                   jax.ShapeDtypeStruct((1, width), F32)],
        name=name, compiler_params=_params(("arbitrary",)))(dy, dy, x, x, w)


def _hgrn_chunk(st, q, k, v, lf):
    cum = _running_sum(lf)
    tot = jnp.sum(lf, axis=0, keepdims=True)
    bm = _iota((A_W, A_W), 0) // A_HD == _iota((A_W, A_W), 1) // A_HD
    bo = bm.astype(F32)
    o_inter = _bdot(q * jnp.exp(cum), st, "nt")
    kd = k * jnp.exp(tot - cum)
    st_new = st * jnp.exp(tot) + jnp.where(bm, _bdot(v, kd, "tn"), 0.0)
    lane_h = _iota((SUB, A_W), 1) // A_HD
    n_h = A_W // A_HD
    t3 = _iota((SUB, SUB, A_W), 0)
    s3 = _iota((SUB, SUB, A_W), 1)
    outs = []
    for i in range(CHUNK // SUB):
        sl = slice(i * SUB, (i + 1) * SUB)
        qi, ki, vi, ci = q[sl], k[sl], v[sl], cum[sl]
        e = jnp.exp(jnp.minimum(ci[:, None, :] - ci[None, :, :], 0.0))
        d3 = jnp.where(s3 <= t3, qi[:, None, :] * ki[None, :, :] * e, 0.0)
        ds = _mdot(d3.reshape(SUB * SUB, A_W), bo, "nn", 1, 1, (True, False)).reshape(SUB, SUB, A_W)
        oi = jnp.sum(ds * vi[None, :, :], axis=1)
        if i > 0:
            n = i * SUB
            cb = cum[n - 1:n]
            qt = qi * jnp.exp(ci - cb)
            kt = k[:n] * jnp.exp(jnp.minimum(cb - cum[:n], 0.0))
            qs = jnp.concatenate([jnp.where(lane_h == h, qt, 0.0) for h in range(n_h)], axis=0)
            p = _bdot(_bdot(qs, kt, "nt"), v[:n])
            for h in range(n_h):
                oi = oi + jnp.where(lane_h == h, p[h * SUB:(h + 1) * SUB], 0.0)
        outs.append(oi)
    return st_new, o_inter + jnp.concatenate(outs, axis=0)


def _chunkspec(w, rev, n):
    if rev:
        return pl.BlockSpec((CHUNK, w), lambda i: (n - 1 - i, 0))
    return pl.BlockSpec((CHUNK, w), lambda i: (i, 0))


def _hgrn_fwd(name, q, k, v, lf):
    s = q.shape[0]
    n = s // CHUNK

    def body(q_ref, k_ref, v_ref, lf_ref, o_ref, sts_ref, st):
        @pl.when(pl.program_id(0) == 0)
        def _():
            st[...] = jnp.zeros_like(st)

        sts_ref[0] = st[...]
        st_new, o = _hgrn_chunk(st[...], q_ref[...], k_ref[...], v_ref[...], lf_ref[...])
        st[...] = st_new
        o_ref[...] = o

    return pl.pallas_call(
        body, grid=(n,), in_specs=[_chunkspec(A_W, False, n)] * 4,
        out_specs=[_chunkspec(A_W, False, n), pl.BlockSpec((1, A_W, A_W), lambda i: (i, 0, 0))],
        out_shape=[jax.ShapeDtypeStruct((s, A_W), F32), jax.ShapeDtypeStruct((n, A_W, A_W), F32)],
        scratch_shapes=[pltpu.VMEM((A_W, A_W), F32)],
        name=name, compiler_params=_params(("arbitrary",)))(q, k, v, lf)


def _hgrn_bwd(name, q, k, v, lf, sts, do):
    s = q.shape[0]
    n = s // CHUNK

    def body(q_ref, k_ref, v_ref, lf_ref, sts_ref, do_ref, dq_ref, dk_ref, dv_ref, dlf_ref, dst):
        @pl.when(pl.program_id(0) == 0)
        def _():
            dst[...] = jnp.zeros_like(dst)

        _, vjp = jax.vjp(_hgrn_chunk, sts_ref[0], q_ref[...], k_ref[...], v_ref[...], lf_ref[...])
        g = vjp((dst[...], do_ref[...]))
        dst[...] = g[0]
        dq_ref[...] = g[1]
        dk_ref[...] = g[2]
        dv_ref[...] = g[3]
        dlf_ref[...] = g[4]

    rs = _chunkspec(A_W, True, n)
    return pl.pallas_call(
        body, grid=(n,),
        in_specs=[rs] * 4 + [pl.BlockSpec((1, A_W, A_W), lambda i: (n - 1 - i, 0, 0)), rs],
        out_specs=[rs] * 4, out_shape=[jax.ShapeDtypeStruct((s, A_W), F32)] * 4,
        scratch_shapes=[pltpu.VMEM((A_W, A_W), F32)],
        name=name, compiler_params=_params(("arbitrary",)))(q, k, v, lf, sts, do)


PREP_CHUNKS = 4
DEC_ROWS = 8


def _to_heads(x, rows=CHUNK, width=B_HD):
    n = x.shape[0] // rows
    x3 = x.reshape(n, rows, x.shape[1])
    return jnp.concatenate([x3[:, :, h * B_HD:h * B_HD + width] for h in range(B_NH)], axis=0)


def _from_heads(y):
    n = y.shape[0] // B_NH
    x3 = jnp.concatenate([y[h * n:(h + 1) * n] for h in range(B_NH)], axis=2)
    return x3.reshape(n * y.shape[1], B_NH * B_HD)


def _gdn_prep(q, k, v, beta, la):
    q, k, v, beta, la = [_to_heads(z) for z in (q, k, v, beta, la)]
    nb = q.shape[0]
    r = _iota((nb, CHUNK, CHUNK), 1)
    c = _iota((nb, CHUNK, CHUNK), 2)
    causal = c <= r
    g = _running_sum(la)
    tot = jnp.sum(la, axis=1, keepdims=True)
    gcol = g[:, :, :CHUNK]
    decay = jnp.where(causal, jnp.exp(jnp.where(causal, gcol - jnp.swapaxes(gcol, 1, 2), 0.0)), 0.0)
    kb = k * beta
    a_mat = jnp.where(c < r, _bdot(kb, k, "nt") * decay, 0.0)
    nil = -a_mat
    inv = (r == c).astype(F32) + nil
    for _ in range(5):
        nil = _mdot(nil, nil, "nn", 2, 2, BOTH)
        inv = inv + _mdot(inv, nil, "nn", 2, 2, BOTH)
    eg = jnp.exp(g)
    u = _bdot(inv, v * beta)
    w = _bdot(inv, kb * eg)
    qk = jnp.where(causal, _bdot(q, k, "nt") * decay, 0.0)
    qk = jnp.concatenate([qk, jnp.zeros((nb, CHUNK, B_HD - CHUNK), F32)], axis=2)
    dec = jnp.broadcast_to(jnp.exp(tot), (nb, DEC_ROWS, B_HD))
    return tuple(_from_heads(z) for z in (u, w, qk, q * eg, k * jnp.exp(tot - g), dec))


def _gdn_rec(st, u, w, qk, qd, kd, dec):
    u, w, qd, kd = [_to_heads(z) for z in (u, w, qd, kd)]
    qk = _to_heads(qk, width=CHUNK)
    v_new = u - _bdot(w, st)
    o = _bdot(qd, st) + _bdot(qk, v_new)
    st_new = st * _to_heads(dec, rows=DEC_ROWS)[:, :1, :1] + _bdot(kd, v_new, "tn")
    return st_new, _from_heads(o)


def _gdn_prep_fwd(name, q, k, v, beta, la):
    s = q.shape[0]
    t = PREP_CHUNKS * CHUNK
    td = PREP_CHUNKS * DEC_ROWS

    def body(*refs):
        outs = _gdn_prep(*[r[...] for r in refs[:5]])
        for o_ref, o in zip(refs[5:], outs):
            o_ref[...] = o

    return pl.pallas_call(
        body, grid=(s // t,), in_specs=[_rowspec(t, B_W)] * 5,
        out_specs=[_rowspec(t, B_W)] * 5 + [_rowspec(td, B_W)],
        out_shape=[jax.ShapeDtypeStruct((s, B_W), F32)] * 5 + [jax.ShapeDtypeStruct((s // CHUNK * DEC_ROWS, B_W), F32)],
        name=name, compiler_params=_params(("arbitrary",)))(q, k, v, beta, la)


def _gdn_prep_bwd(name, q, k, v, beta, la, cts):
    s = q.shape[0]
    t = PREP_CHUNKS * CHUNK
    td = PREP_CHUNKS * DEC_ROWS

    def body(*refs):
        _, vjp = jax.vjp(_gdn_prep, *[r[...] for r in refs[:5]])
        g = vjp(tuple(r[...] for r in refs[5:11]))
        for o_ref, o in zip(refs[11:], g):
            o_ref[...] = o

    return pl.pallas_call(
        body, grid=(s // t,), in_specs=[_rowspec(t, B_W)] * 10 + [_rowspec(td, B_W)],
        out_specs=[_rowspec(t, B_W)] * 5, out_shape=[jax.ShapeDtypeStruct((s, B_W), F32)] * 5,
        name=name, compiler_params=_params(("arbitrary",)))(q, k, v, beta, la, *cts)


def _gdn_rec_fwd(name, u, w, qk, qd, kd, dec):
    s = u.shape[0]
    n = s // CHUNK

    def body(u_ref, w_ref, qk_ref, qd_ref, kd_ref, dec_ref, o_ref, sts_ref, st):
        @pl.when(pl.program_id(0) == 0)
        def _():
            st[...] = jnp.zeros_like(st)

        sts_ref[0] = st[...]
        st_new, o = _gdn_rec(st[...], u_ref[...], w_ref[...], qk_ref[...], qd_ref[...], kd_ref[...], dec_ref[...])
        st[...] = st_new
        o_ref[...] = o

    cs = _chunkspec(B_W, False, n)
    return pl.pallas_call(
        body, grid=(n,), in_specs=[cs] * 5 + [pl.BlockSpec((DEC_ROWS, B_W), lambda i: (i, 0))],
        out_specs=[cs, pl.BlockSpec((1, B_NH, B_HD, B_HD), lambda i: (i, 0, 0, 0))],
        out_shape=[jax.ShapeDtypeStruct((s, B_W), F32), jax.ShapeDtypeStruct((n, B_NH, B_HD, B_HD), F32)],
        scratch_shapes=[pltpu.VMEM((B_NH, B_HD, B_HD), F32)],
        name=name, compiler_params=_params(("arbitrary",)))(u, w, qk, qd, kd, dec)


def _gdn_rec_bwd(name, u, w, qk, qd, kd, dec, sts, do):
    s = u.shape[0]
    n = s // CHUNK

    def body(u_ref, w_ref, qk_ref, qd_ref, kd_ref, dec_ref, sts_ref, do_ref,
             du_ref, dw_ref, dqk_ref, dqd_ref, dkd_ref, ddec_ref, dst):
        @pl.when(pl.program_id(0) == 0)
        def _():
            dst[...] = jnp.zeros_like(dst)

        _, vjp = jax.vjp(_gdn_rec, sts_ref[0], u_ref[...], w_ref[...], qk_ref[...], qd_ref[...], kd_ref[...],
                         dec_ref[...])
        g = vjp((dst[...], do_ref[...]))
        dst[...] = g[0]
        for r, gi in zip((du_ref, dw_ref, dqk_ref, dqd_ref, dkd_ref, ddec_ref), g[1:]):
            r[...] = gi

    rs = _chunkspec(B_W, True, n)
    ds = pl.BlockSpec((DEC_ROWS, B_W), lambda i: (n - 1 - i, 0))
    return pl.pallas_call(
        body, grid=(n,),
        in_specs=[rs] * 5 + [ds, pl.BlockSpec((1, B_NH, B_HD, B_HD), lambda i: (n - 1 - i, 0, 0, 0)), rs],
        out_specs=[rs] * 5 + [ds],
        out_shape=[jax.ShapeDtypeStruct((s, B_W), F32)] * 5 + [jax.ShapeDtypeStruct((n * DEC_ROWS, B_W), F32)],
        scratch_shapes=[pltpu.VMEM((B_NH, B_HD, B_HD), F32)],
        name=name, compiler_params=_params(("arbitrary",)))(u, w, qk, qd, kd, dec, sts, do)


SCAN_T = 512


def _block_scan(a, b, reverse):
    t = a.shape[0]
    rows = _iota(a.shape, 0)
    sft = 1
    while sft < t:
        if reverse:
            a_s, b_s = pltpu.roll(a, t - sft, 0), pltpu.roll(b, t - sft, 0)
            edge = rows >= t - sft
        else:
            a_s, b_s = pltpu.roll(a, sft, 0), pltpu.roll(b, sft, 0)
            edge = rows < sft
        a_s = jnp.where(edge, 1.0, a_s)
        b_s = jnp.where(edge, 0.0, b_s)
        b = a * b_s + b
        a = a * a_s
        sft *= 2
    return a, b


def _lru_scan_fwd(name, a, b):
    s, w = a.shape
    t = SCAN_T

    def body(a_ref, b_ref, h_ref, hp_ref, carry):
        @pl.when(pl.program_id(0) == 0)
        def _():
            carry[...] = jnp.zeros_like(carry)

        h_in = carry[0:1]
        ca, cb = _block_scan(a_ref[...], b_ref[...], False)
        h = ca * h_in + cb
        h_ref[...] = h
        hp_ref[...] = jnp.where(_iota(h.shape, 0) == 0, h_in, pltpu.roll(h, 1, 0))
        carry[...] = jnp.broadcast_to(h[t - 1:t], carry.shape)

    return pl.pallas_call(
        body, grid=(s // t,), in_specs=[_rowspec(t, w)] * 2, out_specs=[_rowspec(t, w)] * 2,
        out_shape=[jax.ShapeDtypeStruct((s, w), F32)] * 2, scratch_shapes=[pltpu.VMEM((8, w), F32)],
        name=name, compiler_params=_params(("arbitrary",)))(a, b)


def _lru_scan_bwd(name, a, dh):
    s, w = a.shape
    t = SCAN_T
    n = s // t

    def body(a_ref, dh_ref, g_ref, a_next, g_next):
        @pl.when(pl.program_id(0) == 0)
        def _():
            a_next[...] = jnp.zeros_like(a_next)
            g_next[...] = jnp.zeros_like(g_next)

        av = a_ref[...]
        a_up = jnp.where(_iota(av.shape, 0) == t - 1, a_next[0:1], pltpu.roll(av, t - 1, 0))
        ca, cb = _block_scan(a_up, dh_ref[...], True)
        g = ca * g_next[0:1] + cb
        g_ref[...] = g
        a_next[...] = jnp.broadcast_to(av[0:1], a_next.shape)
        g_next[...] = jnp.broadcast_to(g[0:1], g_next.shape)

    rs = pl.BlockSpec((t, w), lambda i: (n - 1 - i, 0))
    return pl.pallas_call(
        body, grid=(n,), in_specs=[rs] * 2, out_specs=rs, out_shape=jax.ShapeDtypeStruct((s, w), F32),
        scratch_shapes=[pltpu.VMEM((8, w), F32), pltpu.VMEM((8, w), F32)],
        name=name, compiler_params=_params(("arbitrary",)))(a, dh)


def _loss_fn(g, x, tgt):
    y = x * lax.rsqrt(jnp.mean(x * x, axis=-1, keepdims=True) + EPS) * g
    err = y - tgt
    return 0.5 * jnp.sum(jnp.mean(err * err, axis=-1, keepdims=True), axis=0, keepdims=True)


def _loss_and_grad(name, g, x, tgt, t=256):
    s, d = x.shape

    def body(g_ref, x_ref, t_ref, loss_ref, dx_ref, dg_ref):
        val, vjp = jax.vjp(_loss_fn, g_ref[...], x_ref[...], t_ref[...])
        dg, dx, _ = vjp(jnp.ones((1, 1), F32))

        @pl.when(pl.program_id(0) == 0)
        def _():
            loss_ref[...] = jnp.zeros_like(loss_ref)
            dg_ref[...] = jnp.zeros_like(dg_ref)

        loss_ref[...] += jnp.broadcast_to(val, loss_ref.shape)
        dg_ref[...] += dg
        dx_ref[...] = dx

    return pl.pallas_call(
        body, grid=(s // t,), in_specs=[_whole(g.shape), _rowspec(t, d), _rowspec(t, d)],
        out_specs=[_whole((1, 128)), _rowspec(t, d), _whole(g.shape)],
        out_shape=[jax.ShapeDtypeStruct((1, 128), F32), jax.ShapeDtypeStruct((s, d), F32),
                   jax.ShapeDtypeStruct(g.shape, F32)],
        name=name, compiler_params=_params(("arbitrary",)))(g, x, tgt)


def _ew(name, fn, ins, n_out, t=None):
    r, c = ins[0].shape
    t = _tile(r, (512, 256, 128, 64, 32, 16, 8)) if t is None else t

    def body(*refs):
        outs = fn(*[x[...] for x in refs[:len(ins)]])
        for o_ref, o in zip(refs[len(ins):], outs):
            o_ref[...] = o

    return pl.pallas_call(
        body, grid=(r // t,), in_specs=[_rowspec(t, c)] * len(ins), out_specs=[_rowspec(t, c)] * n_out,
        out_shape=[jax.ShapeDtypeStruct((r, c), F32)] * n_out,
        name=name, compiler_params=_params(("arbitrary",)))(*ins)


def _adam_fn(w, g, m, v):
    m = B1 * m + (1.0 - B1) * g
    v = B2 * v + (1.0 - B2) * (g * g)
    m_hat = m / (1.0 - B1 ** STEP)
    v_hat = v / (1.0 - B2 ** STEP)
    delta = -LR * (m_hat / (jnp.sqrt(v_hat) + AEPS) + WD * w)
    return delta, m, v


def _add2_fn(a, b):
    return (a + b,)


def _add4_fn(a, b, c, d):
    return (((a + b) + c) + d,)


def _adam(name, w, g, m, v):
    shp = w.shape
    two = lambda z: z.reshape(-1, shp[-1])
    outs = _ew(name, _adam_fn, [two(w), two(g), two(m), two(v)], 3)
    return [o.reshape(shp) for o in outs]


def _swap(name, bufs, flips):
    n = len(bufs)
    hbm = pl.BlockSpec(memory_space=pltpu.HBM)

    def body(*refs):
        srcs, dsts = refs[:n], refs[n:2 * n]
        send_sems, recv_sems = refs[2 * n], refs[2 * n + 1]
        me = (lax.axis_index("x"), lax.axis_index("y"), lax.axis_index("c"))
        copies = []
        for i in range(n):
            peer = tuple(1 - m if f else m for m, f in zip(me, flips[i]))
            cp = pltpu.make_async_remote_copy(src_ref=srcs[i], dst_ref=dsts[i], send_sem=send_sems.at[i],
                                              recv_sem=recv_sems.at[i], device_id=peer, device_id_type=MESH)
            cp.start()
            copies.append(cp)
        for cp in copies:
            cp.wait()

    return pl.pallas_call(
        body, in_specs=[hbm] * n, out_specs=[hbm] * n,
        out_shape=[jax.ShapeDtypeStruct(b.shape, b.dtype) for b in bufs],
        scratch_shapes=[pltpu.SemaphoreType.DMA((n,)), pltpu.SemaphoreType.DMA((n,))],
        name=name, compiler_params=pltpu.CompilerParams(has_side_effects=True))(*bufs)


FLIP_C = (False, False, True)
FLIP_Y = (False, True, False)
FLIP_X = (True, False, False)
FLIP_XY = (True, True, False)
CHIP_FLIPS = (FLIP_Y, FLIP_X, FLIP_XY)


def _half(buf, which):
    r = buf.shape[-2] // 2
    return lax.dynamic_slice_in_dim(buf, which * r, r, axis=buf.ndim - 2)


def _join_halves(mine, other, core):
    r = mine.shape[0]
    out = lax.dynamic_update_slice(jnp.zeros((2 * r,) + mine.shape[1:], mine.dtype), mine, (core * r, 0))
    return lax.dynamic_update_slice(out, other, ((1 - core) * r, 0))


def _pack_big(w_in, w_out, w_up, w_down):
    pad = jnp.pad(w_in, ((0, 0), (0, 0), (0, 1024 - w_in.shape[-1])))
    return jnp.concatenate([pad.reshape(-1, 1024), w_up.reshape(-1, 1024), w_down.reshape(-1, 1024),
                            w_out.reshape(-1, 1024)], axis=0)


def _unpack_big(p):
    w_in = p[0:4096].reshape(4, 1024, 1024)[:, :, :D_IN // 4]
    w_up = p[4096:8192].reshape(4, 1024, 1024)
    w_down = p[8192:12288].reshape(4, 1024, 1024)
    w_out = p[12288:13312].reshape(4, 256, 1024)
    return w_in, w_out, w_up, w_down


def _gather_big(packed, chip, core):
    mine = _half(packed, core)
    got = _swap("gather_chips", [mine] * 3, CHIP_FLIPS)
    sib = _swap("gather_sibling", got, [FLIP_C] * 3)
    return [_join_halves(got[j], sib[j], core) for j in range(3)]


def _reduce_scatter_big(per_chip, chip, core):
    r = PACK_ROWS // 2
    give = lax.dynamic_slice_in_dim(per_chip, (1 - core) * r, r, axis=1)
    keep = lax.dynamic_slice_in_dim(per_chip, core * r, r, axis=1)
    (sib,) = _swap("rs_sibling", [give.reshape(4 * r, 1024)], [FLIP_C])
    (pair,) = _ew("rs_add_pair", _add2_fn, [keep.reshape(4 * r, 1024), sib], 1)
    pair = pair.reshape(4, r, 1024)
    sends = [lax.dynamic_index_in_dim(pair, chip ^ (j + 1), axis=0, keepdims=False) for j in range(3)]
    own = lax.dynamic_index_in_dim(pair, chip, axis=0, keepdims=False)
    got = _swap("rs_chips", sends, CHIP_FLIPS)
    (total,) = _ew("rs_add_chips", _add4_fn, [own] + list(got), 1)
    (other,) = _swap("rs_share", [total], [FLIP_C])
    return _join_halves(total, other, core)


def _allreduce_small(tag, buf):
    for name, flip in (("c", FLIP_C), ("y", FLIP_Y), ("x", FLIP_X)):
        (other,) = _swap(tag + "_swap_" + name, [buf], [flip])
        (buf,) = _ew(tag + "_add_" + name, _add2_fn, [buf, other], 1)
    return buf


def _permute_in_cols(w):
    z = jnp.zeros(w.shape[:-1] + (D_IN_PAD - D_IN,), w.dtype)
    return jnp.concatenate([w[..., 1024:3072], w[..., :1024], w[..., 3080:D_IN], w[..., 3072:3080], z], axis=-1)


def _unpermute_in_cols(w):
    return jnp.concatenate([w[..., 2048:3072], w[..., :2048], w[..., GATE0:GATE0 + 8], w[..., 3072:GATE0]], axis=-1)


def _block_diag(w):
    out = jnp.zeros((C_W, C_W), w.dtype)
    for n in range(4):
        out = lax.dynamic_update_slice(out, w[n], (64 * n, 64 * n))
    return out


def _diag_blocks(w):
    return jnp.stack([w[64 * n:64 * (n + 1), 64 * n:64 * (n + 1)] for n in range(4)], axis=0)


def _gate_row(v):
    return jnp.concatenate([jnp.zeros((4,), F32), v, jnp.zeros((120,), F32)]).reshape(1, 128)


SMALL = ("norm1_g", "hgrn_lb_logits", "hgrn_norm_g", "gdn_conv_w", "gdn_a_log", "gdn_dt_bias", "gdn_norm_g",
         "lru_conv_w", "lru_conv_b", "lru_w_a", "lru_b_a", "lru_w_x", "lru_b_x", "lru_lambda", "norm2_g",
         "final_norm_g")


def _flatten_small(tree):
    flat = jnp.concatenate([tree[k].reshape(-1) for k in sorted(tree)])
    rows = -(-flat.shape[0] // (512 * 128)) * 512
    return jnp.pad(flat, (0, rows * 128 - flat.shape[0])).reshape(rows, 128)


def _unflatten_small(buf, shapes):
    flat = buf.reshape(-1)
    out, off = {}, 0
    for k in sorted(shapes):
        size = 1
        for d in shapes[k]:
            size *= d
        out[k] = flat[off:off + size].reshape(shapes[k])
        off += size
    return out


def _layer_fwd(l, x, p, w_in, w_out, w_up, w_down):
    sv = {"x0": x}
    (h,) = _rowwise_fwd("norm1", _rmsnorm_fn, [p["norm1_g"]], [x], [D_MODEL], [BF16], 512)
    proj = _matmul("proj_in", h, w_in, "nn")
    sv["h"], sv["proj"] = h, proj
    aq, af, ag = _cols(proj, COL_AQ, A_W), _cols(proj, COL_AF, A_W), _cols(proj, COL_AG, A_W)
    ai = lax.slice_in_dim(proj, COL_AI, COL_AI + A_W, axis=1)
    bz, gate, cy = _cols(proj, COL_BZ, B_W), _cols(proj, GATE0, 128), _cols(proj, COL_CY, C_W)
    q, k, lf = _rowwise_fwd("hgrn_pre", functools.partial(_hgrn_pre_fn, l), [p["lb_logits"]], [aq, af],
                            [A_W] * 3, [F32] * 3, 512)
    o_a, sts_a = _hgrn_fwd("hgrn_chunk", q, k, ai, lf)
    (y_a,) = _rowwise_fwd("hgrn_post", _hgrn_post_fn, [p["hgrn_norm_g"]], [o_a, ag], [A_W], [BF16], 512)
    sv.update(aq=aq, af=af, ai=ai, ag=ag, hq=q, hk=k, hlf=lf, o_a=o_a, sts_a=sts_a)
    qkvc = _conv_fwd("gdn_conv", proj, 0, 3 * B_W, p["gdn_conv_w"], jnp.zeros((1, 3 * B_W), F32), 256)
    gq, gk, gv, gb, gla = _rowwise_fwd("gdn_pre", _gdn_pre_fn, [p["gdn_a_log"], p["gdn_dt_bias"]], [qkvc, gate],
                                       [B_W] * 5, [F32] * 5, 256)
    wy = _gdn_prep_fwd("gdn_prep", gq, gk, gv, gb, gla)
    o_b, sts_b = _gdn_rec_fwd("gdn_rec", *wy)
    sv["wy"] = wy
    (y_b,) = _rowwise_fwd("gdn_post", _gdn_post_fn, [p["gdn_norm_g"]], [o_b, bz], [B_W], [BF16], 512)
    sv.update(qkvc=qkvc, gate=gate, bz=bz, gq=gq, gk=gk, gv=gv, gb=gb, gla=gla, o_b=o_b, sts_b=sts_b)
    xc = _conv_fwd("lru_conv", proj, COL_CX, C_W, p["lru_conv_w"], p["lru_conv_b"], 512)
    a, b = _rowwise_fwd("lru_pre", _lru_pre_fn, [p["lru_w_a"], p["lru_b_a"], p["lru_w_x"], p["lru_b_x"], p["lru_lambda"]],
                        [xc], [C_W] * 2, [F32] * 2, 512)
    hs, h_prev = _lru_scan_fwd("lru_scan", a, b)
    (y_c,) = _rowwise_fwd("lru_post", _lru_post_fn, [], [hs, cy], [C_W], [BF16], 512)
    sv.update(xc=xc, la=a, hs=hs, h_prev=h_prev, cy=cy)
    mixed = jnp.concatenate([y_a, y_b, y_c], axis=1)
    x1 = _matmul("proj_out", mixed, w_out, "nn", res=x)
    (h2,) = _rowwise_fwd("norm2", _rmsnorm_fn, [p["norm2_g"]], [x1], [D_MODEL], [BF16], 512)
    up = _matmul("mlp_up", h2, w_up, "nn")
    x2 = _matmul("mlp_down", up, w_down, "nn", a_fn=_relu2, res=x1)
    sv.update(mixed=mixed, x1=x1, h2=h2, up=up)
    return x2, sv


def _layer_bwd(l, dx, p, sv, w_in, w_out, w_up, w_down):
    gs = {}
    dup = _matmul("mlp_down_dx", dx, w_down, "nt", epi=_drelu2, epi_in=sv["up"])
    d_w_down = _matmul("mlp_down_dw", sv["up"], dx, "tn", a_fn=_relu2)
    dh2 = _matmul("mlp_up_dx", dup, w_up, "nt")
    d_w_up = _matmul("mlp_up_dw", sv["h2"], dup, "tn")
    (gs["norm2_g"],), (dx1,) = _rowwise_bwd("norm2_bwd", _rmsnorm_fn, [p["norm2_g"]], [sv["x1"]], [dh2], 512, res=dx)
    dmixed = _matmul("proj_out_dx", dx1, w_out, "nt")
    d_w_out = _matmul("proj_out_dw", sv["mixed"], dx1, "tn")
    dy_a, dy_b, dy_c = dmixed[:, :A_W], dmixed[:, A_W:A_W + B_W], dmixed[:, A_W + B_W:]
    (gs["hgrn_norm_g"],), (do_a, dag) = _rowwise_bwd("hgrn_post_bwd", _hgrn_post_fn, [p["hgrn_norm_g"]],
                                                     [sv["o_a"], sv["ag"]], [dy_a], 512)
    dq, dk, dai, dlf = _hgrn_bwd("hgrn_chunk_bwd", sv["hq"], sv["hk"], sv["ai"], sv["hlf"], sv["sts_a"], do_a)
    (gs["lb_logits"],), (daq, daf) = _rowwise_bwd("hgrn_pre_bwd", functools.partial(_hgrn_pre_fn, l), [p["lb_logits"]],
                                                  [sv["aq"], sv["af"]], [dq, dk, dlf], 512)
    (gs["gdn_norm_g"],), (do_b, dbz) = _rowwise_bwd("gdn_post_bwd", _gdn_post_fn, [p["gdn_norm_g"]],
                                                    [sv["o_b"], sv["bz"]], [dy_b], 512)
    d_wy = _gdn_rec_bwd("gdn_rec_bwd", *sv["wy"], sv["sts_b"], do_b)
    dgq, dgk, dgv, dgb, dgla = _gdn_prep_bwd("gdn_prep_bwd", sv["gq"], sv["gk"], sv["gv"], sv["gb"], sv["gla"], d_wy)
    (gs["gdn_a_log"], gs["gdn_dt_bias"]), (dqkvc, dgate) = _rowwise_bwd(
        "gdn_pre_bwd", _gdn_pre_fn, [p["gdn_a_log"], p["gdn_dt_bias"]], [sv["qkvc"], sv["gate"]],
        [dgq, dgk, dgv, dgb, dgla], 256)
    dqkv, gs["gdn_conv_w"], _ = _conv_bwd("gdn_conv_bwd", dqkvc, sv["proj"], 0, 3 * B_W, p["gdn_conv_w"], 256)
    _, (dhs, dcy) = _rowwise_bwd("lru_post_bwd", _lru_post_fn, [], [sv["hs"], sv["cy"]], [dy_c], 512)
    g = _lru_scan_bwd("lru_scan_bwd", sv["la"], dhs)
    (da,) = _rowwise_fwd("lru_da", _lru_da_fn, [], [g, sv["h_prev"]], [C_W], [F32], 512)
    lru_params = [p["lru_w_a"], p["lru_b_a"], p["lru_w_x"], p["lru_b_x"], p["lru_lambda"]]
    dps, (dxc,) = _rowwise_bwd("lru_pre_bwd", _lru_pre_fn, lru_params, [sv["xc"]], [da, g], 512)
    gs["lru_w_a"], gs["lru_b_a"], gs["lru_w_x"], gs["lru_b_x"], gs["lru_lambda"] = dps
    dcx, gs["lru_conv_w"], gs["lru_conv_b"] = _conv_bwd("lru_conv_bwd", dxc, sv["proj"], COL_CX, C_W, p["lru_conv_w"], 512)
    s = dx.shape[0]
    dproj = jnp.concatenate([dqkv, dbz, daq, daf, dai, dag, dcx, dcy, dgate,
                             jnp.zeros((s, D_IN_PAD - GATE0 - 128), F32)], axis=1)
    dh = _matmul("proj_in_dx", dproj, w_in, "nt")
    d_w_in = _matmul("proj_in_dw", sv["h"], dproj, "tn")
    (gs["norm1_g"],), (dx0,) = _rowwise_bwd("norm1_bwd", _rmsnorm_fn, [p["norm1_g"]], [sv["x0"]], [dh], 512, res=dx1)
    return dx0, dict(w_in=d_w_in, w_out=d_w_out, w_up=d_w_up, w_down=d_w_down), gs


def _layer_params(l, lb_logits, hgrn_norm_g, gdn_conv_w, gdn_a_log, gdn_dt_bias, gdn_norm_g, lru_conv_w, lru_conv_b,
                  lru_w_a, lru_b_a, lru_w_x, lru_b_x, lru_lambda, norm1_g, norm2_g):
    row = lambda v: v.reshape(1, -1)
    return dict(
        norm1_g=row(norm1_g[l]), norm2_g=row(norm2_g[l]), lb_logits=lb_logits,
        hgrn_norm_g=row(jnp.tile(hgrn_norm_g[l], A_W // A_HD)),
        gdn_conv_w=gdn_conv_w[l], gdn_a_log=_gate_row(gdn_a_log[l]), gdn_dt_bias=_gate_row(gdn_dt_bias[l]),
        gdn_norm_g=row(jnp.tile(gdn_norm_g[l], B_NH)),
        lru_conv_w=lru_conv_w[l], lru_conv_b=row(lru_conv_b[l]), lru_w_a=_block_diag(lru_w_a[l]), lru_b_a=row(lru_b_a[l]),
        lru_w_x=_block_diag(lru_w_x[l]), lru_b_x=row(lru_b_x[l]), lru_lambda=row(lru_lambda[l]))


def _small_grads_to_reference(gs_layers, d_final_g, d_logits):
    st = lambda k, f=lambda z: z: jnp.stack([f(g[k]) for g in gs_layers], axis=0)
    vec = lambda z: z.reshape(-1)
    return dict(
        norm1_g=st("norm1_g", vec), norm2_g=st("norm2_g", vec), hgrn_lb_logits=d_logits,
        hgrn_norm_g=st("hgrn_norm_g", lambda z: z.reshape(A_W // A_HD, A_HD).sum(0)),
        gdn_conv_w=st("gdn_conv_w"), gdn_a_log=st("gdn_a_log", lambda z: z[0, 4:8]),
        gdn_dt_bias=st("gdn_dt_bias", lambda z: z[0, 4:8]),
        gdn_norm_g=st("gdn_norm_g", lambda z: z.reshape(B_NH, B_HD).sum(0)),
        lru_conv_w=st("lru_conv_w"), lru_conv_b=st("lru_conv_b", vec), lru_w_a=st("lru_w_a", _diag_blocks),
        lru_b_a=st("lru_b_a", vec), lru_w_x=st("lru_w_x", _diag_blocks), lru_b_x=st("lru_b_x", vec),
        lru_lambda=st("lru_lambda", vec), final_norm_g=d_final_g.reshape(-1))


def _local_step(x, tgt, small, w_in_full, w_out_full, w_up_full, w_down_full):
    layer_p = [_layer_params(l, small["hgrn_lb_logits"], small["hgrn_norm_g"], small["gdn_conv_w"], small["gdn_a_log"],
                             small["gdn_dt_bias"], small["gdn_norm_g"], small["lru_conv_w"], small["lru_conv_b"],
                             small["lru_w_a"], small["lru_b_a"], small["lru_w_x"], small["lru_b_x"], small["lru_lambda"],
                             small["norm1_g"], small["norm2_g"]) for l in range(DEPTH)]
    saved = []
    for l in range(DEPTH):
        x, sv = _layer_fwd(l, x, layer_p[l], w_in_full[l], w_out_full[l], w_up_full[l], w_down_full[l])
        saved.append(sv)
    loss, dx, d_final_g = _loss_and_grad("final_loss", small["final_norm_g"].reshape(1, -1), x, tgt)
    big = [None] * DEPTH
    gs_layers = [None] * DEPTH
    d_logits = jnp.zeros_like(small["hgrn_lb_logits"])
    for l in reversed(range(DEPTH)):
        dx, big[l], gs_layers[l] = _layer_bwd(l, dx, layer_p[l], saved[l], w_in_full[l], w_out_full[l], w_up_full[l],
                                              w_down_full[l])
        d_logits = d_logits + gs_layers[l]["lb_logits"]
    return loss, dx, big, _small_grads_to_reference(gs_layers, d_final_g, d_logits)


def kernel(x, norm1_g, w_in, hgrn_lb_logits, hgrn_norm_g, gdn_conv_w, gdn_a_log, gdn_dt_bias, gdn_norm_g, lru_conv_w, lru_conv_b, lru_w_a, lru_b_a, lru_w_x, lru_b_x, lru_lambda, w_out, norm2_g, w_up, w_down, final_norm_g, loss_target, m_norm1_g, m_w_in, m_hgrn_lb_logits, m_hgrn_norm_g, m_gdn_conv_w, m_gdn_a_log, m_gdn_dt_bias, m_gdn_norm_g, m_lru_conv_w, m_lru_conv_b, m_lru_w_a, m_lru_b_a, m_lru_w_x, m_lru_b_x, m_lru_lambda, m_w_out, m_norm2_g, m_w_up, m_w_down, m_final_norm_g, v_norm1_g, v_w_in, v_hgrn_lb_logits, v_hgrn_norm_g, v_gdn_conv_w, v_gdn_a_log, v_gdn_dt_bias, v_gdn_norm_g, v_lru_conv_w, v_lru_conv_b, v_lru_w_a, v_lru_b_a, v_lru_w_x, v_lru_b_x, v_lru_lambda, v_w_out, v_norm2_g, v_w_up, v_w_down, v_final_norm_g):
    args = dict(locals())
    ix, iy, core = lax.axis_index("x"), lax.axis_index("y"), lax.axis_index("c")
    chip = 2 * ix + iy

    packed = _pack_big(w_in, w_out, w_up, w_down).astype(BF16)
    others = _gather_big(packed, chip, core)
    shares = jnp.stack([packed] + others, axis=0)
    by_chip = [lax.dynamic_index_in_dim(shares, chip ^ k, axis=0, keepdims=False) for k in range(4)]
    parts = [_unpack_big(b) for b in by_chip]
    w_in_full = _permute_in_cols(jnp.concatenate([pt[0] for pt in parts], axis=-1))
    w_out_full = jnp.concatenate([pt[1] for pt in parts], axis=1)
    w_up_full = jnp.concatenate([pt[2] for pt in parts], axis=-1)
    w_down_full = jnp.concatenate([pt[3] for pt in parts], axis=1)

    n_gc, n_lc = gdn_conv_w.shape[-1], lru_conv_w.shape[-1]
    conv_full = dict(
        gdn_conv_w=lax.dynamic_update_slice(jnp.zeros((DEPTH, 4, 4 * n_gc), F32), gdn_conv_w * 0.5, (0, 0, chip * n_gc)),
        lru_conv_w=lax.dynamic_update_slice(jnp.zeros((DEPTH, 4, 4 * n_lc), F32), lru_conv_w * 0.5, (0, 0, chip * n_lc)))
    conv_shapes = {k: v.shape for k, v in conv_full.items()}
    conv_full = _unflatten_small(_allreduce_small("convw", _flatten_small(conv_full)), conv_shapes)

    small = {k: args[k] for k in SMALL}
    small.update(conv_full)

    loss, grad_x, big, sg = _local_step(x[0], loss_target[0], small, w_in_full, w_out_full, w_up_full, w_down_full)

    sg["loss"] = loss[0, :1]
    shapes = {k: v.shape for k, v in sg.items()}
    sg = _unflatten_small(_allreduce_small("smallgrad", _flatten_small(sg)), shapes)
    loss_out = sg.pop("loss")[0]

    d_w_in = _unpermute_in_cols(jnp.stack([b["w_in"] for b in big], axis=0))
    d_w_out = jnp.stack([b["w_out"] for b in big], axis=0)
    d_w_up = jnp.stack([b["w_up"] for b in big], axis=0)
    d_w_down = jnp.stack([b["w_down"] for b in big], axis=0)
    n_in = D_IN // 4
    per_chip = jnp.stack([
        _pack_big(d_w_in[:, :, k * n_in:(k + 1) * n_in], d_w_out[:, k * 256:(k + 1) * 256],
                  d_w_up[:, :, k * 1024:(k + 1) * 1024], d_w_down[:, k * 1024:(k + 1) * 1024]) for k in range(4)], axis=0)
    g_in, g_out, g_up, g_down = _unpack_big(_reduce_scatter_big(per_chip, chip, core))

    grads = dict(sg)
    grads["gdn_conv_w"] = lax.dynamic_slice_in_dim(sg["gdn_conv_w"], chip * n_gc, n_gc, axis=2)
    grads["lru_conv_w"] = lax.dynamic_slice_in_dim(sg["lru_conv_w"], chip * n_lc, n_lc, axis=2)
    grads.update(w_in=g_in, w_out=g_out, w_up=g_up, w_down=g_down)

    names = ['norm1_g', 'w_in', 'hgrn_lb_logits', 'hgrn_norm_g', 'gdn_conv_w', 'gdn_a_log', 'gdn_dt_bias', 'gdn_norm_g',
             'lru_conv_w', 'lru_conv_b', 'lru_w_a', 'lru_b_a', 'lru_w_x', 'lru_b_x', 'lru_lambda', 'w_out', 'norm2_g',
             'w_up', 'w_down', 'final_norm_g']
    big_names = ("w_in", "w_out", "w_up", "w_down")
    delta, new_m, new_v = {}, {}, {}
    for k in big_names:
        delta[k], new_m[k], new_v[k] = _adam("adam_" + k, args[k], grads[k], args["m_" + k], args["v_" + k])
    small_names = [k for k in names if k not in big_names]
    shapes = {k: args[k].shape for k in small_names}
    flat = [_flatten_small({k: src(k) for k in small_names})
            for src in (lambda k: args[k], lambda k: grads[k], lambda k: args["m_" + k], lambda k: args["v_" + k])]
    outs = _ew("adam_small", _adam_fn, flat, 3)
    for dst, o in zip((delta, new_m, new_v), outs):
        dst.update(_unflatten_small(o, shapes))
    return (loss_out, grad_x[None], *[grads[k] for k in names], *[delta[k] for k in names],
            *[new_m[k] for k in names], *[new_v[k] for k in names])
```

```python
import functools

import jax
import jax.numpy as jnp
from jax import lax
from jax.experimental import pallas as pl
from jax.experimental.pallas import tpu as pltpu

F32 = jnp.float32
BF16 = jnp.bfloat16
MESH = pl.DeviceIdType.MESH

DEPTH = 4
D_MODEL = 1024
A_W, B_W, C_W = 256, 512, 256
A_HD, B_HD = 64, 128
B_NH = B_W // B_HD
D_IN = 3592
D_IN_PAD = 3840
COL_BZ, COL_AQ, COL_AF, COL_AI, COL_AG, COL_CX, COL_CY = 1536, 2048, 2304, 2560, 2816, 3072, 3328
GATE0 = 3584
D_FF = 4096
CHUNK = 64
SUB = 16
EPS = 1e-6
TINY = 1e-30
RG_C = 8.0
LR, B1, B2, AEPS, WD, STEP = 0.001, 0.9, 0.999, 1e-08, 0.01, 10
VMEM_LIMIT = 56 * 1024 * 1024
PACK_ROWS = 13 * 1024

NN = ((1,), (0,))
NT = ((1,), (1,))
TN = ((0,), (0,))


DIMS = {"nn": NN, "nt": NT, "tn": TN}


def _split(a, n):
    parts, r = [], a
    for i in range(n):
        p = r.astype(BF16)
        parts.append(p)
        if i + 1 < n:
            r = r - p.astype(F32)
    return parts


def _split_dot(a, b, mode, na, nb):
    (ca,), (cb,) = DIMS[mode]
    dims = ((DIMS[mode]), ((), ())) if a.ndim == 2 else (((ca + 1,), (cb + 1,)), ((0,), (0,)))
    pa, pb = _split(a, na), _split(b, nb)
    acc = None
    for i in range(na):
        for j in range(nb):
            if i + j < max(na, nb):
                t = lax.dot_general(pa[i], pb[j], dims, preferred_element_type=F32)
                acc = t if acc is None else acc + t
    return acc


@functools.partial(jax.custom_vjp, nondiff_argnums=(2, 3, 4, 5))
def _mdot(a, b, mode, na, nb, need):
    return _split_dot(a, b, mode, na, nb)


def _mdot_fwd(a, b, mode, na, nb, need):
    return _split_dot(a, b, mode, na, nb), (a, b)


def _mdot_bwd(mode, na, nb, need, res, ct):
    a, b = res
    n = max(na, nb)
    da, db = jnp.zeros_like(a), jnp.zeros_like(b)
    if mode == "nn":
        if need[0]:
            da = _split_dot(ct, b, "nt", n, nb)
        if need[1]:
            db = _split_dot(a, ct, "tn", na, n)
    elif mode == "nt":
        if need[0]:
            da = _split_dot(ct, b, "nn", n, nb)
        if need[1]:
            db = _split_dot(ct, a, "tn", n, na)
    else:
        if need[0]:
            da = _split_dot(b, ct, "nt", nb, n)
        if need[1]:
            db = _split_dot(a, ct, "nn", na, n)
    return da, db


_mdot.defvjp(_mdot_fwd, _mdot_bwd)
BOTH = (True, True)


def _bdot(a, b, mode="nn"):
    return _mdot(a, b, mode, 1, 1, BOTH)


def _group_sum(x, ones):
    return _mdot(x, ones, "nn", 2, 1, (True, False))


def _running_sum(x):
    shape = x.shape[:-2] + (CHUNK, CHUNK)
    tri = (_iota(shape, len(shape) - 1) <= _iota(shape, len(shape) - 2)).astype(F32)
    return _mdot(tri, x, "nn", 1, 3, (False, True))


def _iota(shape, d):
    return lax.broadcasted_iota(jnp.int32, shape, d)


def _block_ones(width, group):
    return (_iota((width, width), 0) // group == _iota((width, width), 1) // group).astype(F32)


def _params(sem):
    return pltpu.CompilerParams(dimension_semantics=sem, vmem_limit_bytes=VMEM_LIMIT)


def _tile(dim, prefs):
    for p in prefs:
        if dim % p == 0:
            return p
    return dim


def _whole(shape):
    return pl.BlockSpec(shape, lambda i: (0,) * len(shape))


def _rowspec(t, w, col0=0):
    cb = col0 // w
    return pl.BlockSpec((t, w), lambda i: (i, cb))


def _cols(arr, col0, width):
    assert col0 % width == 0
    return (arr, col0, width)


def _row_operands(rows, t):
    arrs, specs = [], []
    for r in rows:
        arr, col0, w = r if isinstance(r, tuple) else (r, 0, r.shape[1])
        arrs.append(arr)
        specs.append(_rowspec(t, w, col0))
    return arrs, specs


def _rowwise_fwd(name, fn, params, rows, out_widths, out_dtypes, t):
    rows, row_specs = _row_operands(rows, t)
    s = rows[0].shape[0]
    n_p, n_r = len(params), len(rows)

    def body(*refs):
        p = [r[...] for r in refs[:n_p]]
        xs = [r[...].astype(F32) for r in refs[n_p:n_p + n_r]]
        outs = fn(pl.program_id(0) * t, *p, *xs)
        for o_ref, o in zip(refs[n_p + n_r:], outs):
            o_ref[...] = o.astype(o_ref.dtype)

    return pl.pallas_call(
        body, grid=(s // t,),
        in_specs=[_whole(p.shape) for p in params] + row_specs,
        out_specs=[_rowspec(t, w) for w in out_widths],
        out_shape=[jax.ShapeDtypeStruct((s, w), d) for w, d in zip(out_widths, out_dtypes)],
        name=name, compiler_params=_params(("arbitrary",)))(*params, *rows)


def _rowwise_bwd(name, fn, params, rows, douts, t, need=None, res=None):
    rows, row_specs = _row_operands(rows, t)
    s = rows[0].shape[0]
    n_p, n_r, n_o = len(params), len(rows), len(douts)
    need = [True] * n_r if need is None else need
    n_res = 0 if res is None else 1

    def body(*refs):
        p = [r[...] for r in refs[:n_p]]
        xs = [r[...].astype(F32) for r in refs[n_p:n_p + n_r]]
        dys = [r[...].astype(F32) for r in refs[n_p + n_r:n_p + n_r + n_o]]
        k = n_p + n_r + n_o
        res_ref = refs[k] if n_res else None
        dp_refs = refs[k + n_res:k + n_res + n_p]
        dx_refs = refs[k + n_res + n_p:]
        row0 = pl.program_id(0) * t
        _, vjp = jax.vjp(lambda *a: tuple(fn(row0, *a)), *p, *xs)
        g = vjp(tuple(dys))

        @pl.when(pl.program_id(0) == 0)
        def _():
            for r in dp_refs:
                r[...] = jnp.zeros_like(r)

        for r, gp in zip(dp_refs, g[:n_p]):
            r[...] += gp
        gx = [gi for gi, nd in zip(g[n_p:], need) if nd]
        for j, (r, gi) in enumerate(zip(dx_refs, gx)):
            if j == 0 and n_res:
                gi = gi + res_ref[...]
            r[...] = gi

    widths = [sp.block_shape[1] for sp, nd in zip(row_specs, need) if nd]
    outs = pl.pallas_call(
        body, grid=(s // t,),
        in_specs=([_whole(p.shape) for p in params] + row_specs
                  + [_rowspec(t, d.shape[1]) for d in douts] + ([_rowspec(t, res.shape[1])] if n_res else [])),
        out_specs=[_whole(p.shape) for p in params] + [_rowspec(t, w) for w in widths],
        out_shape=[jax.ShapeDtypeStruct(p.shape, F32) for p in params] + [jax.ShapeDtypeStruct((s, w), F32) for w in widths],
        name=name, compiler_params=_params(("arbitrary",)))(*params, *rows, *douts, *([res] if n_res else []))
    return list(outs[:n_p]), list(outs[n_p:])


def _rmsnorm_fn(row0, g, x):
    y = x * lax.rsqrt(jnp.mean(x * x, axis=-1, keepdims=True) + EPS)
    return (y * g,)


def _hgrn_pre_fn(layer, row0, logits, aq, af):
    m = jnp.max(logits, axis=0, keepdims=True)
    e = jnp.exp(logits - m)
    p = e / jnp.sum(e, axis=0, keepdims=True)
    if layer == 0:
        lb = jnp.zeros((1, A_W), F32)
    else:
        acc = p[1:2]
        for j in range(2, layer + 1):
            acc = acc + p[j:j + 1]
        lb = jnp.minimum(jnp.maximum(acc, 0.0), 1.0 - EPS)
    sig = jax.nn.sigmoid(af)
    f = lb + (1.0 - lb) * sig
    log_f = jnp.log(jnp.maximum(f, TINY))
    k = (1.0 - lb) * jax.nn.sigmoid(-af)
    return jax.nn.silu(aq), k, log_f


def _hgrn_post_fn(row0, g, o, ag):
    ms = _group_sum(o * o, _block_ones(A_W, A_HD)) * (1.0 / A_HD)
    return (o * lax.rsqrt(ms + EPS) * g * jax.nn.silu(ag),)


def _gdn_pre_fn(row0, a_log, dt_bias, qkvc, gate):
    act = jax.nn.silu(qkvc)
    q, k, v = act[:, :B_W], act[:, B_W:2 * B_W], act[:, 2 * B_W:]
    bo = _block_ones(B_W, B_HD)
    q = q * lax.rsqrt(_group_sum(q * q, bo) + EPS) * (B_HD ** -0.5)
    k = k * lax.rsqrt(_group_sum(k * k, bo) + EPS)
    beta = jax.nn.sigmoid(gate)
    log_alpha = -jnp.exp(a_log) * jax.nn.softplus(gate + dt_bias)
    t = gate.shape[0]
    spread = lambda z, lane0: jnp.concatenate(
        [jnp.broadcast_to(z[:, lane0 + h:lane0 + h + 1], (t, B_HD)) for h in range(B_NH)], axis=1)
    return q, k, v, spread(beta, 0), spread(log_alpha, B_NH)


def _gdn_post_fn(row0, g, o, bz):
    ms = _group_sum(o * o, _block_ones(B_W, B_HD)) * (1.0 / B_HD)
    return (o * lax.rsqrt(ms + EPS) * g * jax.nn.silu(bz),)


def _lru_pre_fn(row0, w_a, b_a, w_x, b_x, lam, xc):
    r = jax.nn.sigmoid(_mdot(xc, w_a, "nn", 2, 2, BOTH) + b_a)
    i = jax.nn.sigmoid(_mdot(xc, w_x, "nn", 2, 2, BOTH) + b_x)
    log_a = -RG_C * r * jax.nn.softplus(-lam)
    a = jnp.exp(log_a)
    one_minus = -jnp.tanh(log_a) * (a * a + 1.0)
    mult = jnp.sqrt(jnp.maximum(one_minus, EPS))
    pos = row0 + _iota(xc.shape, 0)
    mult = jnp.where(pos == 0, 1.0, mult)
    return a, mult * i * xc


def _lru_post_fn(row0, h, cy):
    return (jax.nn.gelu(cy, approximate=True) * h,)


def _lru_da_fn(row0, g, h_prev):
    return (g * h_prev,)


def _relu2(u):
    r = jnp.maximum(u, 0.0)
    return r * r


def _drelu2(acc, u):
    return acc * (2.0 * jnp.maximum(u, 0.0))


MM_VMEM_BUDGET = 44 * 1024 * 1024


def _mm_tiles(mode, m, n, k, a_bytes, b_bytes, n_extra):
    best = None
    for tk in {k, 4096, 2048, 1024, 512, 256}:
        for tn in {n, 1280, 1024, 768, 512, 256, 128}:
            for tm in {m, 1024, 512, 256, 128}:
                if m % tm or n % tn or k % tk or tm > 1024:
                    continue
                tiles = tm * tk * a_bytes + tk * tn * b_bytes + (1 + n_extra) * tm * tn * 4
                if 2 * tiles + (tm * tn * 4 if tk < k else 0) > MM_VMEM_BUDGET:
                    continue
                a_reads = 1 if tk == k else n // tn
                b_reads = 1 if (tk == k and tn == n) else m // tm
                cost = m * k * a_bytes * a_reads + k * n * b_bytes * b_reads + m * n * 4 * (k // tk - 1)
                if best is None or (cost, -tm) < best[0]:
                    best = ((cost, -tm), (tm, tn, tk))
    if best is None:
        raise ValueError("no matmul tiling fits VMEM")
    return best[1]


def _matmul(name, a, b, mode, a_fn=None, res=None, epi=None, epi_in=None, out_dtype=F32):
    if mode == "nn":
        (m, k), n = a.shape, b.shape[1]
    elif mode == "nt":
        (m, k), n = a.shape, b.shape[0]
    else:
        (k, m), n = a.shape, b.shape[1]
    n_extra = (res is not None) + (epi_in is not None)
    tm, tn, tk = _mm_tiles(mode, m, n, k, a.dtype.itemsize, b.dtype.itemsize, n_extra)
    nk = k // tk
    a_spec = {"nn": pl.BlockSpec((tm, tk), lambda i, j, kk: (i, kk)),
              "nt": pl.BlockSpec((tm, tk), lambda i, j, kk: (i, kk)),
              "tn": pl.BlockSpec((tk, tm), lambda i, j, kk: (kk, i))}[mode]
    b_spec = {"nn": pl.BlockSpec((tk, tn), lambda i, j, kk: (kk, j)),
              "nt": pl.BlockSpec((tn, tk), lambda i, j, kk: (j, kk)),
              "tn": pl.BlockSpec((tk, tn), lambda i, j, kk: (kk, j))}[mode]
    o_spec = pl.BlockSpec((tm, tn), lambda i, j, kk: (i, j))
    dims = {"nn": NN, "nt": NT, "tn": TN}[mode]
    extra = [x for x in (res, epi_in) if x is not None]

    def body(*refs):
        a_ref, b_ref = refs[0], refs[1]
        rest = list(refs[2:])
        res_ref = rest.pop(0) if res is not None else None
        epi_ref = rest.pop(0) if epi_in is not None else None
        o_ref = rest[0]
        acc = rest[1] if nk > 1 else None
        kk = pl.program_id(2)
        at = a_ref[...]
        if a_fn is not None:
            at = a_fn(at.astype(F32))
        part = lax.dot_general(at.astype(BF16), b_ref[...].astype(BF16), (dims, ((), ())),
                               preferred_element_type=F32)

        def finish(out):
            if res_ref is not None:
                out = out + res_ref[...]
            if epi is not None:
                out = epi(out, epi_ref[...])
            o_ref[...] = out.astype(o_ref.dtype)

        if nk == 1:
            finish(part)
        else:
            @pl.when(kk == 0)
            def _():
                acc[...] = part

            @pl.when(jnp.logical_and(kk > 0, kk < nk - 1))
            def _():
                acc[...] += part

            @pl.when(kk == nk - 1)
            def _():
                finish(acc[...] + part)

    return pl.pallas_call(
        body, grid=(m // tm, n // tn, nk),
        in_specs=[a_spec, b_spec] + [o_spec] * len(extra), out_specs=o_spec,
        out_shape=jax.ShapeDtypeStruct((m, n), out_dtype),
        scratch_shapes=[pltpu.VMEM((tm, tn), F32)] if nk > 1 else [],
        name=name, compiler_params=_params(("arbitrary", "arbitrary", "arbitrary")))(a, b, *extra)


HALO = 8


def _conv_fwd(name, x, col0, width, w, bias, t=512):
    s = x.shape[0]
    cb = col0 // width
    hb = t // HALO

    def body(x_ref, halo_ref, w_ref, b_ref, y_ref):
        i = pl.program_id(0)
        halo = jnp.where(i == 0, 0.0, halo_ref[...])
        xp = jnp.concatenate([halo, x_ref[...]], axis=0)
        wv = w_ref[...]
        y = b_ref[...] + wv[0:1] * xp[HALO - 3:HALO - 3 + t]
        for j in range(1, 4):
            y = y + wv[j:j + 1] * xp[HALO - 3 + j:HALO - 3 + j + t]
        y_ref[...] = y

    return pl.pallas_call(
        body, grid=(s // t,),
        in_specs=[pl.BlockSpec((t, width), lambda i: (i, cb)),
                  pl.BlockSpec((HALO, width), lambda i: (jnp.maximum(i * hb - 1, 0), cb)),
                  _whole(w.shape), _whole(bias.shape)],
        out_specs=_rowspec(t, width), out_shape=jax.ShapeDtypeStruct((s, width), F32),
        name=name, compiler_params=_params(("arbitrary",)))(x, x, w, bias)


def _conv_bwd(name, dy, x, col0, width, w, t=512):
    s = x.shape[0]
    cb = col0 // width
    hb = t // HALO
    nblk = s // t

    def body(dy_ref, dyn_ref, x_ref, halo_ref, w_ref, dx_ref, dw_ref, db_ref):
        i = pl.program_id(0)
        dyv = dy_ref[...]
        nxt = jnp.where(i == nblk - 1, 0.0, dyn_ref[...])
        dyp = jnp.concatenate([dyv, nxt], axis=0)
        wv = w_ref[...]
        dx = wv[3:4] * dyv
        for j in range(3):
            dx = dx + wv[j:j + 1] * dyp[3 - j:3 - j + t]
        dx_ref[...] = dx
        halo = jnp.where(i == 0, 0.0, halo_ref[...])
        xp = jnp.concatenate([halo, x_ref[...]], axis=0)
        dw = jnp.concatenate(
            [jnp.sum(dyv * xp[HALO - 3 + j:HALO - 3 + j + t], axis=0, keepdims=True) for j in range(4)], axis=0)

        @pl.when(i == 0)
        def _():
            dw_ref[...] = jnp.zeros_like(dw_ref)
            db_ref[...] = jnp.zeros_like(db_ref)

        dw_ref[...] += dw
        db_ref[...] += jnp.sum(dyv, axis=0, keepdims=True)

    return pl.pallas_call(
        body, grid=(nblk,),
        in_specs=[_rowspec(t, width),
                  pl.BlockSpec((HALO, width), lambda i: (jnp.minimum((i + 1) * hb, s // HALO - 1), 0)),
                  pl.BlockSpec((t, width), lambda i: (i, cb)),
                  pl.BlockSpec((HALO, width), lambda i: (jnp.maximum(i * hb - 1, 0), cb)),
                  _whole(w.shape)],
        out_specs=[_rowspec(t, width), _whole((4, width)), _whole((1, width))],
        out_shape=[jax.ShapeDtypeStruct((s, width), F32), jax.ShapeDtypeStruct((4, width), F32),
                   jax.ShapeDtypeStruct((1, width), F32)],
        name=name, compiler_params=_params(("arbitrary",)))(dy, dy, x, x, w)


def _hgrn_chunk(st, q, k, v, lf):
    cum = _running_sum(lf)
    tot = jnp.sum(lf, axis=0, keepdims=True)
    bm = _iota((A_W, A_W), 0) // A_HD == _iota((A_W, A_W), 1) // A_HD
    bo = bm.astype(F32)
    o_inter = _bdot(q * jnp.exp(cum), st, "nt")
    kd = k * jnp.exp(tot - cum)
    st_new = st * jnp.exp(tot) + jnp.where(bm, _bdot(v, kd, "tn"), 0.0)
    lane_h = _iota((SUB, A_W), 1) // A_HD
    n_h = A_W // A_HD
    t3 = _iota((SUB, SUB, A_W), 0)
    s3 = _iota((SUB, SUB, A_W), 1)
    outs = []
    for i in range(CHUNK // SUB):
        sl = slice(i * SUB, (i + 1) * SUB)
        qi, ki, vi, ci = q[sl], k[sl], v[sl], cum[sl]
        e = jnp.exp(jnp.minimum(ci[:, None, :] - ci[None, :, :], 0.0))
        d3 = jnp.where(s3 <= t3, qi[:, None, :] * ki[None, :, :] * e, 0.0)
        ds = _mdot(d3.reshape(SUB * SUB, A_W), bo, "nn", 1, 1, (True, False)).reshape(SUB, SUB, A_W)
        oi = jnp.sum(ds * vi[None, :, :], axis=1)
        if i > 0:
            n = i * SUB
            cb = cum[n - 1:n]
            qt = qi * jnp.exp(ci - cb)
            kt = k[:n] * jnp.exp(jnp.minimum(cb - cum[:n], 0.0))
            qs = jnp.concatenate([jnp.where(lane_h == h, qt, 0.0) for h in range(n_h)], axis=0)
            p = _bdot(_bdot(qs, kt, "nt"), v[:n])
            for h in range(n_h):
                oi = oi + jnp.where(lane_h == h, p[h * SUB:(h + 1) * SUB], 0.0)
        outs.append(oi)
    return st_new, o_inter + jnp.concatenate(outs, axis=0)


def _chunkspec(w, rev, n):
    if rev:
        return pl.BlockSpec((CHUNK, w), lambda i: (n - 1 - i, 0))
    return pl.BlockSpec((CHUNK, w), lambda i: (i, 0))


def _hgrn_fwd(name, q, k, v, lf):
    s = q.shape[0]
    n = s // CHUNK

    def body(q_ref, k_ref, v_ref, lf_ref, o_ref, sts_ref, st):
        @pl.when(pl.program_id(0) == 0)
        def _():
            st[...] = jnp.zeros_like(st)

        sts_ref[0] = st[...]
        st_new, o = _hgrn_chunk(st[...], q_ref[...], k_ref[...], v_ref[...], lf_ref[...])
        st[...] = st_new
        o_ref[...] = o

    return pl.pallas_call(
        body, grid=(n,), in_specs=[_chunkspec(A_W, False, n)] * 4,
        out_specs=[_chunkspec(A_W, False, n), pl.BlockSpec((1, A_W, A_W), lambda i: (i, 0, 0))],
        out_shape=[jax.ShapeDtypeStruct((s, A_W), F32), jax.ShapeDtypeStruct((n, A_W, A_W), F32)],
        scratch_shapes=[pltpu.VMEM((A_W, A_W), F32)],
        name=name, compiler_params=_params(("arbitrary",)))(q, k, v, lf)


def _hgrn_bwd(name, q, k, v, lf, sts, do):
    s = q.shape[0]
    n = s // CHUNK

    def body(q_ref, k_ref, v_ref, lf_ref, sts_ref, do_ref, dq_ref, dk_ref, dv_ref, dlf_ref, dst):
        @pl.when(pl.program_id(0) == 0)
        def _():
            dst[...] = jnp.zeros_like(dst)

        _, vjp = jax.vjp(_hgrn_chunk, sts_ref[0], q_ref[...], k_ref[...], v_ref[...], lf_ref[...])
        g = vjp((dst[...], do_ref[...]))
        dst[...] = g[0]
        dq_ref[...] = g[1]
        dk_ref[...] = g[2]
        dv_ref[...] = g[3]
        dlf_ref[...] = g[4]

    rs = _chunkspec(A_W, True, n)
    return pl.pallas_call(
        body, grid=(n,),
        in_specs=[rs] * 4 + [pl.BlockSpec((1, A_W, A_W), lambda i: (n - 1 - i, 0, 0)), rs],
        out_specs=[rs] * 4, out_shape=[jax.ShapeDtypeStruct((s, A_W), F32)] * 4,
        scratch_shapes=[pltpu.VMEM((A_W, A_W), F32)],
        name=name, compiler_params=_params(("arbitrary",)))(q, k, v, lf, sts, do)


PREP_CHUNKS = 4
DEC_ROWS = 8


def _to_heads(x, rows=CHUNK, width=B_HD):
    n = x.shape[0] // rows
    x3 = x.reshape(n, rows, x.shape[1])
    return jnp.concatenate([x3[:, :, h * B_HD:h * B_HD + width] for h in range(B_NH)], axis=0)


def _from_heads(y):
    n = y.shape[0] // B_NH
    x3 = jnp.concatenate([y[h * n:(h + 1) * n] for h in range(B_NH)], axis=2)
    return x3.reshape(n * y.shape[1], B_NH * B_HD)


@jax.custom_vjp
def _unit_lower_inverse(a_mat):
    nil = -a_mat
    eye = (_iota(a_mat.shape, 1) == _iota(a_mat.shape, 2)).astype(F32)
    inv = eye + nil
    sq = 2
    while sq < CHUNK:
        nil = _split_dot(nil, nil, "nn", 2, 2)
        inv = inv + _split_dot(inv, nil, "nn", 2, 2)
        sq *= 2
    return inv


def _unit_lower_inverse_fwd(a_mat):
    inv = _unit_lower_inverse(a_mat)
    return inv, inv


def _unit_lower_inverse_bwd(inv, ct):
    return (-_split_dot(_split_dot(inv, ct, "tn", 2, 2), inv, "nt", 2, 2),)


_unit_lower_inverse.defvjp(_unit_lower_inverse_fwd, _unit_lower_inverse_bwd)


def _gdn_prep(q, k, v, beta, la):
    q, k, v, beta, la = [_to_heads(z) for z in (q, k, v, beta, la)]
    nb = q.shape[0]
    r = _iota((nb, CHUNK, CHUNK), 1)
    c = _iota((nb, CHUNK, CHUNK), 2)
    causal = c <= r
    g = _running_sum(la)
    tot = jnp.sum(la, axis=1, keepdims=True)
    gcol = g[:, :, :CHUNK]
    decay = jnp.where(causal, jnp.exp(jnp.where(causal, gcol - jnp.swapaxes(gcol, 1, 2), 0.0)), 0.0)
    kb = k * beta
    a_mat = jnp.where(c < r, _bdot(kb, k, "nt") * decay, 0.0)
    inv = _unit_lower_inverse(a_mat)
    eg = jnp.exp(g)
    u = _bdot(inv, v * beta)
    w = _bdot(inv, kb * eg)
    qk = jnp.where(causal, _bdot(q, k, "nt") * decay, 0.0)
    qk = jnp.concatenate([qk, jnp.zeros((nb, CHUNK, B_HD - CHUNK), F32)], axis=2)
    dec = jnp.broadcast_to(jnp.exp(tot), (nb, DEC_ROWS, B_HD))
    return tuple(_from_heads(z) for z in (u, w, qk, q * eg, k * jnp.exp(tot - g), dec))


def _gdn_rec(st, u, w, qk, qd, kd, dec):
    u, w, qd, kd = [_to_heads(z) for z in (u, w, qd, kd)]
    qk = _to_heads(qk, width=CHUNK)
    v_new = u - _bdot(w, st)
    o = _bdot(qd, st) + _bdot(qk, v_new)
    st_new = st * _to_heads(dec, rows=DEC_ROWS)[:, :1, :1] + _bdot(kd, v_new, "tn")
    return st_new, _from_heads(o)


def _gdn_prep_fwd(name, q, k, v, beta, la):
    s = q.shape[0]
    t = PREP_CHUNKS * CHUNK
    td = PREP_CHUNKS * DEC_ROWS

    def body(*refs):
        outs = _gdn_prep(*[r[...] for r in refs[:5]])
        for o_ref, o in zip(refs[5:], outs):
            o_ref[...] = o

    return pl.pallas_call(
        body, grid=(s // t,), in_specs=[_rowspec(t, B_W)] * 5,
        out_specs=[_rowspec(t, B_W)] * 5 + [_rowspec(td, B_W)],
        out_shape=[jax.ShapeDtypeStruct((s, B_W), F32)] * 5 + [jax.ShapeDtypeStruct((s // CHUNK * DEC_ROWS, B_W), F32)],
        name=name, compiler_params=_params(("arbitrary",)))(q, k, v, beta, la)


def _gdn_prep_bwd(name, q, k, v, beta, la, cts):
    s = q.shape[0]
    t = PREP_CHUNKS * CHUNK
    td = PREP_CHUNKS * DEC_ROWS

    def body(*refs):
        _, vjp = jax.vjp(_gdn_prep, *[r[...] for r in refs[:5]])
        g = vjp(tuple(r[...] for r in refs[5:11]))
        for o_ref, o in zip(refs[11:], g):
            o_ref[...] = o

    return pl.pallas_call(
        body, grid=(s // t,), in_specs=[_rowspec(t, B_W)] * 10 + [_rowspec(td, B_W)],
        out_specs=[_rowspec(t, B_W)] * 5, out_shape=[jax.ShapeDtypeStruct((s, B_W), F32)] * 5,
        name=name, compiler_params=_params(("arbitrary",)))(q, k, v, beta, la, *cts)


def _gdn_rec_fwd(name, u, w, qk, qd, kd, dec):
    s = u.shape[0]
    n = s // CHUNK

    def body(u_ref, w_ref, qk_ref, qd_ref, kd_ref, dec_ref, o_ref, sts_ref, st):
        @pl.when(pl.program_id(0) == 0)
        def _():
            st[...] = jnp.zeros_like(st)

        sts_ref[0] = st[...]
        st_new, o = _gdn_rec(st[...], u_ref[...], w_ref[...], qk_ref[...], qd_ref[...], kd_ref[...], dec_ref[...])
        st[...] = st_new
        o_ref[...] = o

    cs = _chunkspec(B_W, False, n)
    return pl.pallas_call(
        body, grid=(n,), in_specs=[cs] * 5 + [pl.BlockSpec((DEC_ROWS, B_W), lambda i: (i, 0))],
        out_specs=[cs, pl.BlockSpec((1, B_NH, B_HD, B_HD), lambda i: (i, 0, 0, 0))],
        out_shape=[jax.ShapeDtypeStruct((s, B_W), F32), jax.ShapeDtypeStruct((n, B_NH, B_HD, B_HD), F32)],
        scratch_shapes=[pltpu.VMEM((B_NH, B_HD, B_HD), F32)],
        name=name, compiler_params=_params(("arbitrary",)))(u, w, qk, qd, kd, dec)


def _gdn_rec_bwd(name, u, w, qk, qd, kd, dec, sts, do):
    s = u.shape[0]
    n = s // CHUNK

    def body(u_ref, w_ref, qk_ref, qd_ref, kd_ref, dec_ref, sts_ref, do_ref,
             du_ref, dw_ref, dqk_ref, dqd_ref, dkd_ref, ddec_ref, dst):
        @pl.when(pl.program_id(0) == 0)
        def _():
            dst[...] = jnp.zeros_like(dst)

        _, vjp = jax.vjp(_gdn_rec, sts_ref[0], u_ref[...], w_ref[...], qk_ref[...], qd_ref[...], kd_ref[...],
                         dec_ref[...])
        g = vjp((dst[...], do_ref[...]))
        dst[...] = g[0]
        for r, gi in zip((du_ref, dw_ref, dqk_ref, dqd_ref, dkd_ref, ddec_ref), g[1:]):
            r[...] = gi

    rs = _chunkspec(B_W, True, n)
    ds = pl.BlockSpec((DEC_ROWS, B_W), lambda i: (n - 1 - i, 0))
    return pl.pallas_call(
        body, grid=(n,),
        in_specs=[rs] * 5 + [ds, pl.BlockSpec((1, B_NH, B_HD, B_HD), lambda i: (n - 1 - i, 0, 0, 0)), rs],
        out_specs=[rs] * 5 + [ds],
        out_shape=[jax.ShapeDtypeStruct((s, B_W), F32)] * 5 + [jax.ShapeDtypeStruct((n * DEC_ROWS, B_W), F32)],
        scratch_shapes=[pltpu.VMEM((B_NH, B_HD, B_HD), F32)],
        name=name, compiler_params=_params(("arbitrary",)))(u, w, qk, qd, kd, dec, sts, do)


SCAN_T = 512


def _block_scan(a, b, reverse):
    t = a.shape[0]
    rows = _iota(a.shape, 0)
    sft = 1
    while sft < t:
        if reverse:
            a_s, b_s = pltpu.roll(a, t - sft, 0), pltpu.roll(b, t - sft, 0)
            edge = rows >= t - sft
        else:
            a_s, b_s = pltpu.roll(a, sft, 0), pltpu.roll(b, sft, 0)
            edge = rows < sft
        a_s = jnp.where(edge, 1.0, a_s)
        b_s = jnp.where(edge, 0.0, b_s)
        b = a * b_s + b
        a = a * a_s
        sft *= 2
    return a, b


def _lru_scan_fwd(name, a, b):
    s, w = a.shape
    t = SCAN_T

    def body(a_ref, b_ref, h_ref, hp_ref, carry):
        @pl.when(pl.program_id(0) == 0)
        def _():
            carry[...] = jnp.zeros_like(carry)

        h_in = carry[0:1]
        ca, cb = _block_scan(a_ref[...], b_ref[...], False)
        h = ca * h_in + cb
        h_ref[...] = h
        hp_ref[...] = jnp.where(_iota(h.shape, 0) == 0, h_in, pltpu.roll(h, 1, 0))
        carry[...] = jnp.broadcast_to(h[t - 1:t], carry.shape)

    return pl.pallas_call(
        body, grid=(s // t,), in_specs=[_rowspec(t, w)] * 2, out_specs=[_rowspec(t, w)] * 2,
        out_shape=[jax.ShapeDtypeStruct((s, w), F32)] * 2, scratch_shapes=[pltpu.VMEM((8, w), F32)],
        name=name, compiler_params=_params(("arbitrary",)))(a, b)


def _lru_scan_bwd(name, a, dh):
    s, w = a.shape
    t = SCAN_T
    n = s // t

    def body(a_ref, dh_ref, g_ref, a_next, g_next):
        @pl.when(pl.program_id(0) == 0)
        def _():
            a_next[...] = jnp.zeros_like(a_next)
            g_next[...] = jnp.zeros_like(g_next)

        av = a_ref[...]
        a_up = jnp.where(_iota(av.shape, 0) == t - 1, a_next[0:1], pltpu.roll(av, t - 1, 0))
        ca, cb = _block_scan(a_up, dh_ref[...], True)
        g = ca * g_next[0:1] + cb
        g_ref[...] = g
        a_next[...] = jnp.broadcast_to(av[0:1], a_next.shape)
        g_next[...] = jnp.broadcast_to(g[0:1], g_next.shape)

    rs = pl.BlockSpec((t, w), lambda i: (n - 1 - i, 0))
    return pl.pallas_call(
        body, grid=(n,), in_specs=[rs] * 2, out_specs=rs, out_shape=jax.ShapeDtypeStruct((s, w), F32),
        scratch_shapes=[pltpu.VMEM((8, w), F32), pltpu.VMEM((8, w), F32)],
        name=name, compiler_params=_params(("arbitrary",)))(a, dh)


def _loss_fn(g, x, tgt):
    y = x * lax.rsqrt(jnp.mean(x * x, axis=-1, keepdims=True) + EPS) * g
    err = y - tgt
    return 0.5 * jnp.sum(jnp.mean(err * err, axis=-1, keepdims=True), axis=0, keepdims=True)


def _loss_and_grad(name, g, x, tgt, t=256):
    s, d = x.shape

    def body(g_ref, x_ref, t_ref, loss_ref, dx_ref, dg_ref):
        val, vjp = jax.vjp(_loss_fn, g_ref[...], x_ref[...], t_ref[...])
        dg, dx, _ = vjp(jnp.ones((1, 1), F32))

        @pl.when(pl.program_id(0) == 0)
        def _():
            loss_ref[...] = jnp.zeros_like(loss_ref)
            dg_ref[...] = jnp.zeros_like(dg_ref)

        loss_ref[...] += jnp.broadcast_to(val, loss_ref.shape)
        dg_ref[...] += dg
        dx_ref[...] = dx

    return pl.pallas_call(
        body, grid=(s // t,), in_specs=[_whole(g.shape), _rowspec(t, d), _rowspec(t, d)],
        out_specs=[_whole((1, 128)), _rowspec(t, d), _whole(g.shape)],
        out_shape=[jax.ShapeDtypeStruct((1, 128), F32), jax.ShapeDtypeStruct((s, d), F32),
                   jax.ShapeDtypeStruct(g.shape, F32)],
        name=name, compiler_params=_params(("arbitrary",)))(g, x, tgt)


def _ew(name, fn, ins, n_out, t=None):
    r, c = ins[0].shape
    t = _tile(r, (512, 256, 128, 64, 32, 16, 8)) if t is None else t

    def body(*refs):
        outs = fn(*[x[...] for x in refs[:len(ins)]])
        for o_ref, o in zip(refs[len(ins):], outs):
            o_ref[...] = o

    return pl.pallas_call(
        body, grid=(r // t,), in_specs=[_rowspec(t, c)] * len(ins), out_specs=[_rowspec(t, c)] * n_out,
        out_shape=[jax.ShapeDtypeStruct((r, c), F32)] * n_out,
        name=name, compiler_params=_params(("arbitrary",)))(*ins)


def _adam_fn(w, g, m, v):
    m = B1 * m + (1.0 - B1) * g
    v = B2 * v + (1.0 - B2) * (g * g)
    m_hat = m / (1.0 - B1 ** STEP)
    v_hat = v / (1.0 - B2 ** STEP)
    delta = -LR * (m_hat / (jnp.sqrt(v_hat) + AEPS) + WD * w)
    return delta, m, v


def _add2_fn(a, b):
    return (a + b,)


def _add4_fn(a, b, c, d):
    return (((a + b) + c) + d,)


def _adam(name, w, g, m, v):
    shp = w.shape
    two = lambda z: z.reshape(-1, shp[-1])
    outs = _ew(name, _adam_fn, [two(w), two(g), two(m), two(v)], 3)
    return [o.reshape(shp) for o in outs]


def _swap(name, bufs, flips):
    n = len(bufs)
    hbm = pl.BlockSpec(memory_space=pltpu.HBM)

    def body(*refs):
        srcs, dsts = refs[:n], refs[n:2 * n]
        send_sems, recv_sems = refs[2 * n], refs[2 * n + 1]
        me = (lax.axis_index("x"), lax.axis_index("y"), lax.axis_index("c"))
        copies = []
        for i in range(n):
            peer = tuple(1 - m if f else m for m, f in zip(me, flips[i]))
            cp = pltpu.make_async_remote_copy(src_ref=srcs[i], dst_ref=dsts[i], send_sem=send_sems.at[i],
                                              recv_sem=recv_sems.at[i], device_id=peer, device_id_type=MESH)
            cp.start()
            copies.append(cp)
        for cp in copies:
            cp.wait()

    return pl.pallas_call(
        body, in_specs=[hbm] * n, out_specs=[hbm] * n,
        out_shape=[jax.ShapeDtypeStruct(b.shape, b.dtype) for b in bufs],
        scratch_shapes=[pltpu.SemaphoreType.DMA((n,)), pltpu.SemaphoreType.DMA((n,))],
        name=name, compiler_params=pltpu.CompilerParams(has_side_effects=True))(*bufs)


FLIP_C = (False, False, True)
FLIP_Y = (False, True, False)
FLIP_X = (True, False, False)
FLIP_XY = (True, True, False)
CHIP_FLIPS = (FLIP_Y, FLIP_X, FLIP_XY)


HALF_ROWS = PACK_ROWS // 2
SUM_ROWS = 512


def _mesh_pos():
    ix, iy, core = lax.axis_index("x"), lax.axis_index("y"), lax.axis_index("c")
    others = [(ix, 1 - iy), (1 - ix, iy), (1 - ix, 1 - iy)]
    return ix, iy, core, others


def _pack_big(w_in, w_out, w_up, w_down):
    pad = jnp.pad(w_in, ((0, 0), (0, 0), (0, 1024 - w_in.shape[-1])))
    return jnp.concatenate([pad.reshape(-1, 1024), w_up.reshape(-1, 1024), w_down.reshape(-1, 1024),
                            w_out.reshape(-1, 1024)], axis=0)


def _unpack_big(p):
    w_in = p[0:4096].reshape(4, 1024, 1024)[:, :, :D_IN // 4]
    w_up = p[4096:8192].reshape(4, 1024, 1024)
    w_down = p[8192:12288].reshape(4, 1024, 1024)
    w_out = p[12288:13312].reshape(4, 256, 1024)
    return w_in, w_out, w_up, w_down


def _gather_big(packed):
    hbm = pl.BlockSpec(memory_space=pltpu.HBM)

    def body(src, out, send_sems, recv_sems, local_sem):
        ix, iy, core, others = _mesh_pos()
        mine = pl.ds(core * HALF_ROWS, HALF_ROWS)
        theirs = pl.ds((1 - core) * HALF_ROWS, HALF_ROWS)

        def copy(k, src_ref, owner, rows, to):
            return pltpu.make_async_remote_copy(
                src_ref=src_ref, dst_ref=out.at[owner, rows], send_sem=send_sems.at[k], recv_sem=recv_sems.at[k],
                device_id=to, device_id_type=MESH)

        chip = 2 * ix + iy
        own = pltpu.make_async_copy(src, out.at[chip], local_sem)
        own.start()
        first = [copy(j, src.at[mine], chip, mine, (*others[j], core)) for j in range(3)]
        for cp in first:
            cp.start()
        passed = []
        for j, (ox, oy) in enumerate(others):
            owner = 2 * ox + oy
            copy(j, src.at[mine], owner, mine, (ix, iy, core)).wait_recv()
            fwd = copy(3 + j, out.at[owner, mine], owner, mine, (ix, iy, 1 - core))
            fwd.start()
            passed.append(fwd)
        for j, (ox, oy) in enumerate(others):
            copy(3 + j, src.at[theirs], 2 * ox + oy, theirs, (ix, iy, core)).wait_recv()
        for cp in first + passed:
            cp.wait_send()
        own.wait()

    return pl.pallas_call(
        body, in_specs=[hbm], out_specs=hbm, out_shape=jax.ShapeDtypeStruct((4,) + packed.shape, packed.dtype),
        scratch_shapes=[pltpu.SemaphoreType.DMA((6,)), pltpu.SemaphoreType.DMA((6,)), pltpu.SemaphoreType.DMA],
        name="gather_big", compiler_params=pltpu.CompilerParams(has_side_effects=True))(packed)


def _rs_sibling(per_chip):
    hbm = pl.BlockSpec(memory_space=pltpu.HBM)

    def body(src, out, send_sem, recv_sem):
        ix, iy, core, _ = _mesh_pos()
        give = pl.ds((1 - core) * HALF_ROWS, HALF_ROWS)
        cp = pltpu.make_async_remote_copy(src_ref=src.at[pl.ds(0, 4), give], dst_ref=out, send_sem=send_sem,
                                          recv_sem=recv_sem, device_id=(ix, iy, 1 - core), device_id_type=MESH)
        cp.start()
        cp.wait()

    return pl.pallas_call(
        body, in_specs=[hbm], out_specs=hbm, out_shape=jax.ShapeDtypeStruct((4, HALF_ROWS, 1024), F32),
        scratch_shapes=[pltpu.SemaphoreType.DMA, pltpu.SemaphoreType.DMA],
        name="rs_sibling", compiler_params=pltpu.CompilerParams(has_side_effects=True))(per_chip)


def _rs_add_pair(per_chip, sib, core):
    nb = HALF_ROWS // SUM_ROWS
    blk = (1, SUM_ROWS, 1024)

    def body(core_ref, a_ref, b_ref, o_ref):
        o_ref[...] = a_ref[...] + b_ref[...]

    return pl.pallas_call(
        body,
        grid_spec=pltpu.PrefetchScalarGridSpec(
            num_scalar_prefetch=1, grid=(4, nb),
            in_specs=[pl.BlockSpec(blk, lambda k, i, c: (k, c[0] * nb + i, 0)),
                      pl.BlockSpec(blk, lambda k, i, c: (k, i, 0))],
            out_specs=pl.BlockSpec(blk, lambda k, i, c: (k, i, 0))),
        out_shape=jax.ShapeDtypeStruct((4, HALF_ROWS, 1024), F32),
        name="rs_add_pair", compiler_params=_params(("arbitrary", "arbitrary")))(core.reshape(1), per_chip, sib)


def _rs_chips(pair):
    hbm = pl.BlockSpec(memory_space=pltpu.HBM)

    def body(src, out, send_sems, recv_sems):
        ix, iy, core, others = _mesh_pos()
        copies = []
        for j, (ox, oy) in enumerate(others):
            cp = pltpu.make_async_remote_copy(src_ref=src.at[2 * ox + oy], dst_ref=out.at[j], send_sem=send_sems.at[j],
                                              recv_sem=recv_sems.at[j], device_id=(ox, oy, core), device_id_type=MESH)
            cp.start()
            copies.append(cp)
        for cp in copies:
            cp.wait()

    return pl.pallas_call(
        body, in_specs=[hbm], out_specs=hbm, out_shape=jax.ShapeDtypeStruct((3, HALF_ROWS, 1024), F32),
        scratch_shapes=[pltpu.SemaphoreType.DMA((3,)), pltpu.SemaphoreType.DMA((3,))],
        name="rs_chips", compiler_params=pltpu.CompilerParams(has_side_effects=True))(pair)


def _rs_add_chips(pair, got, chip):
    nb = HALF_ROWS // SUM_ROWS
    blk = (1, SUM_ROWS, 1024)

    def body(chip_ref, a_ref, g0_ref, g1_ref, g2_ref, o_ref):
        o_ref[...] = ((a_ref[0] + g0_ref[0]) + g1_ref[0]) + g2_ref[0]

    return pl.pallas_call(
        body,
        grid_spec=pltpu.PrefetchScalarGridSpec(
            num_scalar_prefetch=1, grid=(nb,),
            in_specs=[pl.BlockSpec(blk, lambda i, c: (c[0], i, 0))]
            + [pl.BlockSpec(blk, functools.partial(lambda j, i, c: (j, i, 0), j)) for j in range(3)],
            out_specs=pl.BlockSpec((SUM_ROWS, 1024), lambda i, c: (i, 0))),
        out_shape=jax.ShapeDtypeStruct((HALF_ROWS, 1024), F32),
        name="rs_add_chips", compiler_params=_params(("arbitrary",)))(chip.reshape(1), pair, got, got, got)


def _rs_share(total):
    hbm = pl.BlockSpec(memory_space=pltpu.HBM)

    def body(src, out, send_sem, recv_sem, local_sem):
        ix, iy, core, _ = _mesh_pos()
        mine = pl.ds(core * HALF_ROWS, HALF_ROWS)
        theirs = pl.ds((1 - core) * HALF_ROWS, HALF_ROWS)
        own = pltpu.make_async_copy(src, out.at[mine], local_sem)
        own.start()
        cp = pltpu.make_async_remote_copy(src_ref=src, dst_ref=out.at[mine], send_sem=send_sem, recv_sem=recv_sem,
                                          device_id=(ix, iy, 1 - core), device_id_type=MESH)
        cp.start()
        pltpu.make_async_remote_copy(src_ref=src, dst_ref=out.at[theirs], send_sem=send_sem, recv_sem=recv_sem,
                                     device_id=(ix, iy, core), device_id_type=MESH).wait_recv()
        cp.wait_send()
        own.wait()

    return pl.pallas_call(
        body, in_specs=[hbm], out_specs=hbm, out_shape=jax.ShapeDtypeStruct((PACK_ROWS, 1024), F32),
        scratch_shapes=[pltpu.SemaphoreType.DMA, pltpu.SemaphoreType.DMA, pltpu.SemaphoreType.DMA],
        name="rs_share", compiler_params=pltpu.CompilerParams(has_side_effects=True))(total)


def _reduce_scatter_big(per_chip, chip, core):
    pair = _rs_add_pair(per_chip, _rs_sibling(per_chip), core)
    total = _rs_add_chips(pair, _rs_chips(pair), chip)
    return _rs_share(total)


def _allreduce_small(tag, buf):
    for name, flip in (("c", FLIP_C), ("y", FLIP_Y), ("x", FLIP_X)):
        (other,) = _swap(tag + "_swap_" + name, [buf], [flip])
        (buf,) = _ew(tag + "_add_" + name, _add2_fn, [buf, other], 1)
    return buf


def _permute_in_cols(w):
    z = jnp.zeros(w.shape[:-1] + (D_IN_PAD - D_IN,), w.dtype)
    return jnp.concatenate([w[..., 1024:3072], w[..., :1024], w[..., 3080:D_IN], w[..., 3072:3080], z], axis=-1)


IN_COL_SEGMENTS = ((0, 1024, 2048), (1024, 3072, -1024), (3072, 3080, GATE0 - 3072), (3080, D_IN, COL_CX - 3080))


def _orig_cols(w_pad, lo, hi):
    parts = []
    for first, last, shift in IN_COL_SEGMENTS:
        a, b = max(lo, first), min(hi, last)
        if a < b:
            parts.append(w_pad[..., a + shift:b + shift])
    return parts


def _unpermute_in_cols(w):
    return jnp.concatenate([w[..., 2048:3072], w[..., :2048], w[..., GATE0:GATE0 + 8], w[..., 3072:GATE0]], axis=-1)


def _block_diag(w):
    out = jnp.zeros((C_W, C_W), w.dtype)
    for n in range(4):
        out = lax.dynamic_update_slice(out, w[n], (64 * n, 64 * n))
    return out


def _diag_blocks(w):
    return jnp.stack([w[64 * n:64 * (n + 1), 64 * n:64 * (n + 1)] for n in range(4)], axis=0)


def _gate_row(v):
    return jnp.concatenate([jnp.zeros((4,), F32), v, jnp.zeros((120,), F32)]).reshape(1, 128)


SMALL = ("norm1_g", "hgrn_lb_logits", "hgrn_norm_g", "gdn_conv_w", "gdn_a_log", "gdn_dt_bias", "gdn_norm_g",
         "lru_conv_w", "lru_conv_b", "lru_w_a", "lru_b_a", "lru_w_x", "lru_b_x", "lru_lambda", "norm2_g",
         "final_norm_g")


def _flatten_small(tree):
    flat = jnp.concatenate([tree[k].reshape(-1) for k in sorted(tree)])
    rows = -(-flat.shape[0] // (512 * 128)) * 512
    return jnp.pad(flat, (0, rows * 128 - flat.shape[0])).reshape(rows, 128)


def _unflatten_small(buf, shapes):
    flat = buf.reshape(-1)
    out, off = {}, 0
    for k in sorted(shapes):
        size = 1
        for d in shapes[k]:
            size *= d
        out[k] = flat[off:off + size].reshape(shapes[k])
        off += size
    return out


def _layer_fwd(l, x, p, w_in, w_out, w_up, w_down):
    sv = {"x0": x}
    (h,) = _rowwise_fwd("norm1", _rmsnorm_fn, [p["norm1_g"]], [x], [D_MODEL], [BF16], 512)
    proj = _matmul("proj_in", h, w_in, "nn")
    sv["h"], sv["proj"] = h, proj
    aq, af, ag = _cols(proj, COL_AQ, A_W), _cols(proj, COL_AF, A_W), _cols(proj, COL_AG, A_W)
    ai = lax.slice_in_dim(proj, COL_AI, COL_AI + A_W, axis=1)
    bz, gate, cy = _cols(proj, COL_BZ, B_W), _cols(proj, GATE0, 128), _cols(proj, COL_CY, C_W)
    q, k, lf = _rowwise_fwd("hgrn_pre", functools.partial(_hgrn_pre_fn, l), [p["lb_logits"]], [aq, af],
                            [A_W] * 3, [F32] * 3, 512)
    o_a, sts_a = _hgrn_fwd("hgrn_chunk", q, k, ai, lf)
    (y_a,) = _rowwise_fwd("hgrn_post", _hgrn_post_fn, [p["hgrn_norm_g"]], [o_a, ag], [A_W], [BF16], 512)
    sv.update(aq=aq, af=af, ai=ai, ag=ag, hq=q, hk=k, hlf=lf, o_a=o_a, sts_a=sts_a)
    qkvc = _conv_fwd("gdn_conv", proj, 0, 3 * B_W, p["gdn_conv_w"], jnp.zeros((1, 3 * B_W), F32), 256)
    gq, gk, gv, gb, gla = _rowwise_fwd("gdn_pre", _gdn_pre_fn, [p["gdn_a_log"], p["gdn_dt_bias"]], [qkvc, gate],
                                       [B_W] * 5, [F32] * 5, 256)
    wy = _gdn_prep_fwd("gdn_prep", gq, gk, gv, gb, gla)
    o_b, sts_b = _gdn_rec_fwd("gdn_rec", *wy)
    sv["wy"] = wy
    (y_b,) = _rowwise_fwd("gdn_post", _gdn_post_fn, [p["gdn_norm_g"]], [o_b, bz], [B_W], [BF16], 512)
    sv.update(qkvc=qkvc, gate=gate, bz=bz, gq=gq, gk=gk, gv=gv, gb=gb, gla=gla, o_b=o_b, sts_b=sts_b)
    xc = _conv_fwd("lru_conv", proj, COL_CX, C_W, p["lru_conv_w"], p["lru_conv_b"], 512)
    a, b = _rowwise_fwd("lru_pre", _lru_pre_fn, [p["lru_w_a"], p["lru_b_a"], p["lru_w_x"], p["lru_b_x"], p["lru_lambda"]],
                        [xc], [C_W] * 2, [F32] * 2, 512)
    hs, h_prev = _lru_scan_fwd("lru_scan", a, b)
    (y_c,) = _rowwise_fwd("lru_post", _lru_post_fn, [], [hs, cy], [C_W], [BF16], 512)
    sv.update(xc=xc, la=a, hs=hs, h_prev=h_prev, cy=cy)
    mixed = jnp.concatenate([y_a, y_b, y_c], axis=1)
    x1 = _matmul("proj_out", mixed, w_out, "nn", res=x)
    (h2,) = _rowwise_fwd("norm2", _rmsnorm_fn, [p["norm2_g"]], [x1], [D_MODEL], [BF16], 512)
    up = _matmul("mlp_up", h2, w_up, "nn")
    x2 = _matmul("mlp_down", up, w_down, "nn", a_fn=_relu2, res=x1)
    sv.update(mixed=mixed, x1=x1, h2=h2, up=up)
    return x2, sv


def _layer_bwd(l, dx, p, sv, w_in, w_out, w_up, w_down):
    gs = {}
    dup = _matmul("mlp_down_dx", dx, w_down, "nt", epi=_drelu2, epi_in=sv["up"], out_dtype=BF16)
    d_w_down = _matmul("mlp_down_dw", sv["up"], dx, "tn", a_fn=_relu2)
    dh2 = _matmul("mlp_up_dx", dup, w_up, "nt")
    d_w_up = _matmul("mlp_up_dw", sv["h2"], dup, "tn")
    (gs["norm2_g"],), (dx1,) = _rowwise_bwd("norm2_bwd", _rmsnorm_fn, [p["norm2_g"]], [sv["x1"]], [dh2], 512, res=dx)
    dmixed = _matmul("proj_out_dx", dx1, w_out, "nt")
    d_w_out = _matmul("proj_out_dw", sv["mixed"], dx1, "tn")
    dy_a, dy_b, dy_c = dmixed[:, :A_W], dmixed[:, A_W:A_W + B_W], dmixed[:, A_W + B_W:]
    (gs["hgrn_norm_g"],), (do_a, dag) = _rowwise_bwd("hgrn_post_bwd", _hgrn_post_fn, [p["hgrn_norm_g"]],
                                                     [sv["o_a"], sv["ag"]], [dy_a], 512)
    dq, dk, dai, dlf = _hgrn_bwd("hgrn_chunk_bwd", sv["hq"], sv["hk"], sv["ai"], sv["hlf"], sv["sts_a"], do_a)
    (gs["lb_logits"],), (daq, daf) = _rowwise_bwd("hgrn_pre_bwd", functools.partial(_hgrn_pre_fn, l), [p["lb_logits"]],
                                                  [sv["aq"], sv["af"]], [dq, dk, dlf], 512)
    (gs["gdn_norm_g"],), (do_b, dbz) = _rowwise_bwd("gdn_post_bwd", _gdn_post_fn, [p["gdn_norm_g"]],
                                                    [sv["o_b"], sv["bz"]], [dy_b], 512)
    d_wy = _gdn_rec_bwd("gdn_rec_bwd", *sv["wy"], sv["sts_b"], do_b)
    dgq, dgk, dgv, dgb, dgla = _gdn_prep_bwd("gdn_prep_bwd", sv["gq"], sv["gk"], sv["gv"], sv["gb"], sv["gla"], d_wy)
    (gs["gdn_a_log"], gs["gdn_dt_bias"]), (dqkvc, dgate) = _rowwise_bwd(
        "gdn_pre_bwd", _gdn_pre_fn, [p["gdn_a_log"], p["gdn_dt_bias"]], [sv["qkvc"], sv["gate"]],
        [dgq, dgk, dgv, dgb, dgla], 256)
    dqkv, gs["gdn_conv_w"], _ = _conv_bwd("gdn_conv_bwd", dqkvc, sv["proj"], 0, 3 * B_W, p["gdn_conv_w"], 256)
    _, (dhs, dcy) = _rowwise_bwd("lru_post_bwd", _lru_post_fn, [], [sv["hs"], sv["cy"]], [dy_c], 512)
    g = _lru_scan_bwd("lru_scan_bwd", sv["la"], dhs)
    (da,) = _rowwise_fwd("lru_da", _lru_da_fn, [], [g, sv["h_prev"]], [C_W], [F32], 512)
    lru_params = [p["lru_w_a"], p["lru_b_a"], p["lru_w_x"], p["lru_b_x"], p["lru_lambda"]]
    dps, (dxc,) = _rowwise_bwd("lru_pre_bwd", _lru_pre_fn, lru_params, [sv["xc"]], [da, g], 512)
    gs["lru_w_a"], gs["lru_b_a"], gs["lru_w_x"], gs["lru_b_x"], gs["lru_lambda"] = dps
    dcx, gs["lru_conv_w"], gs["lru_conv_b"] = _conv_bwd("lru_conv_bwd", dxc, sv["proj"], COL_CX, C_W, p["lru_conv_w"], 512)
    s = dx.shape[0]
    dproj = jnp.concatenate([dqkv, dbz, daq, daf, dai, dag, dcx, dcy, dgate,
                             jnp.zeros((s, D_IN_PAD - GATE0 - 128), F32)], axis=1).astype(BF16)
    dh = _matmul("proj_in_dx", dproj, w_in, "nt")
    d_w_in = _matmul("proj_in_dw", sv["h"], dproj, "tn")
    (gs["norm1_g"],), (dx0,) = _rowwise_bwd("norm1_bwd", _rmsnorm_fn, [p["norm1_g"]], [sv["x0"]], [dh], 512, res=dx1)
    return dx0, dict(w_in=d_w_in, w_out=d_w_out, w_up=d_w_up, w_down=d_w_down), gs


def _layer_params(l, lb_logits, hgrn_norm_g, gdn_conv_w, gdn_a_log, gdn_dt_bias, gdn_norm_g, lru_conv_w, lru_conv_b,
                  lru_w_a, lru_b_a, lru_w_x, lru_b_x, lru_lambda, norm1_g, norm2_g):
    row = lambda v: v.reshape(1, -1)
    return dict(
        norm1_g=row(norm1_g[l]), norm2_g=row(norm2_g[l]), lb_logits=lb_logits,
        hgrn_norm_g=row(jnp.tile(hgrn_norm_g[l], A_W // A_HD)),
        gdn_conv_w=gdn_conv_w[l], gdn_a_log=_gate_row(gdn_a_log[l]), gdn_dt_bias=_gate_row(gdn_dt_bias[l]),
        gdn_norm_g=row(jnp.tile(gdn_norm_g[l], B_NH)),
        lru_conv_w=lru_conv_w[l], lru_conv_b=row(lru_conv_b[l]), lru_w_a=_block_diag(lru_w_a[l]), lru_b_a=row(lru_b_a[l]),
        lru_w_x=_block_diag(lru_w_x[l]), lru_b_x=row(lru_b_x[l]), lru_lambda=row(lru_lambda[l]))


def _small_grads_to_reference(gs_layers, d_final_g, d_logits):
    st = lambda k, f=lambda z: z: jnp.stack([f(g[k]) for g in gs_layers], axis=0)
    vec = lambda z: z.reshape(-1)
    return dict(
        norm1_g=st("norm1_g", vec), norm2_g=st("norm2_g", vec), hgrn_lb_logits=d_logits,
        hgrn_norm_g=st("hgrn_norm_g", lambda z: z.reshape(A_W // A_HD, A_HD).sum(0)),
        gdn_conv_w=st("gdn_conv_w"), gdn_a_log=st("gdn_a_log", lambda z: z[0, 4:8]),
        gdn_dt_bias=st("gdn_dt_bias", lambda z: z[0, 4:8]),
        gdn_norm_g=st("gdn_norm_g", lambda z: z.reshape(B_NH, B_HD).sum(0)),
        lru_conv_w=st("lru_conv_w"), lru_conv_b=st("lru_conv_b", vec), lru_w_a=st("lru_w_a", _diag_blocks),
        lru_b_a=st("lru_b_a", vec), lru_w_x=st("lru_w_x", _diag_blocks), lru_b_x=st("lru_b_x", vec),
        lru_lambda=st("lru_lambda", vec), final_norm_g=d_final_g.reshape(-1))


def _local_step(x, tgt, small, w_in_full, w_out_full, w_up_full, w_down_full):
    layer_p = [_layer_params(l, small["hgrn_lb_logits"], small["hgrn_norm_g"], small["gdn_conv_w"], small["gdn_a_log"],
                             small["gdn_dt_bias"], small["gdn_norm_g"], small["lru_conv_w"], small["lru_conv_b"],
                             small["lru_w_a"], small["lru_b_a"], small["lru_w_x"], small["lru_b_x"], small["lru_lambda"],
                             small["norm1_g"], small["norm2_g"]) for l in range(DEPTH)]
    saved = []
    for l in range(DEPTH):
        x, sv = _layer_fwd(l, x, layer_p[l], w_in_full[l], w_out_full[l], w_up_full[l], w_down_full[l])
        saved.append(sv)
    loss, dx, d_final_g = _loss_and_grad("final_loss", small["final_norm_g"].reshape(1, -1), x, tgt)
    big = [None] * DEPTH
    gs_layers = [None] * DEPTH
    d_logits = jnp.zeros_like(small["hgrn_lb_logits"])
    for l in reversed(range(DEPTH)):
        dx, big[l], gs_layers[l] = _layer_bwd(l, dx, layer_p[l], saved[l], w_in_full[l], w_out_full[l], w_up_full[l],
                                              w_down_full[l])
        d_logits = d_logits + gs_layers[l]["lb_logits"]
    return loss, dx, big, _small_grads_to_reference(gs_layers, d_final_g, d_logits)


def kernel(x, norm1_g, w_in, hgrn_lb_logits, hgrn_norm_g, gdn_conv_w, gdn_a_log, gdn_dt_bias, gdn_norm_g, lru_conv_w, lru_conv_b, lru_w_a, lru_b_a, lru_w_x, lru_b_x, lru_lambda, w_out, norm2_g, w_up, w_down, final_norm_g, loss_target, m_norm1_g, m_w_in, m_hgrn_lb_logits, m_hgrn_norm_g, m_gdn_conv_w, m_gdn_a_log, m_gdn_dt_bias, m_gdn_norm_g, m_lru_conv_w, m_lru_conv_b, m_lru_w_a, m_lru_b_a, m_lru_w_x, m_lru_b_x, m_lru_lambda, m_w_out, m_norm2_g, m_w_up, m_w_down, m_final_norm_g, v_norm1_g, v_w_in, v_hgrn_lb_logits, v_hgrn_norm_g, v_gdn_conv_w, v_gdn_a_log, v_gdn_dt_bias, v_gdn_norm_g, v_lru_conv_w, v_lru_conv_b, v_lru_w_a, v_lru_b_a, v_lru_w_x, v_lru_b_x, v_lru_lambda, v_w_out, v_norm2_g, v_w_up, v_w_down, v_final_norm_g):
    args = dict(locals())
    ix, iy, core = lax.axis_index("x"), lax.axis_index("y"), lax.axis_index("c")
    chip = 2 * ix + iy

    packed = _pack_big(w_in, w_out, w_up, w_down).astype(BF16)
    shares = _gather_big(packed)
    parts = [_unpack_big(shares[k]) for k in range(4)]
    w_in_full = _permute_in_cols(jnp.concatenate([pt[0] for pt in parts], axis=-1))
    w_out_full = jnp.concatenate([pt[1] for pt in parts], axis=1)
    w_up_full = jnp.concatenate([pt[2] for pt in parts], axis=-1)
    w_down_full = jnp.concatenate([pt[3] for pt in parts], axis=1)

    n_gc, n_lc = gdn_conv_w.shape[-1], lru_conv_w.shape[-1]
    conv_full = dict(
        gdn_conv_w=lax.dynamic_update_slice(jnp.zeros((DEPTH, 4, 4 * n_gc), F32), gdn_conv_w * 0.5, (0, 0, chip * n_gc)),
        lru_conv_w=lax.dynamic_update_slice(jnp.zeros((DEPTH, 4, 4 * n_lc), F32), lru_conv_w * 0.5, (0, 0, chip * n_lc)))
    conv_shapes = {k: v.shape for k, v in conv_full.items()}
    conv_full = _unflatten_small(_allreduce_small("convw", _flatten_small(conv_full)), conv_shapes)

    small = {k: args[k] for k in SMALL}
    small.update(conv_full)

    loss, grad_x, big, sg = _local_step(x[0], loss_target[0], small, w_in_full, w_out_full, w_up_full, w_down_full)

    sg["loss"] = loss[0, :1]
    shapes = {k: v.shape for k, v in sg.items()}
    sg = _unflatten_small(_allreduce_small("smallgrad", _flatten_small(sg)), shapes)
    loss_out = sg.pop("loss")[0]

    n_in = D_IN // 4
    pieces = []
    for k in range(4):
        for b in big:
            cols = jnp.concatenate(_orig_cols(b["w_in"], k * n_in, (k + 1) * n_in), axis=1)
            pieces.append(jnp.pad(cols, ((0, 0), (0, 1024 - n_in))))
        pieces += [b["w_up"][:, k * 1024:(k + 1) * 1024] for b in big]
        pieces += [b["w_down"][k * 1024:(k + 1) * 1024] for b in big]
        pieces += [b["w_out"][k * 256:(k + 1) * 256] for b in big]
    per_chip = jnp.concatenate(pieces, axis=0).reshape(4, PACK_ROWS, 1024)
    g_in, g_out, g_up, g_down = _unpack_big(_reduce_scatter_big(per_chip, chip, core))

    grads = dict(sg)
    grads["gdn_conv_w"] = lax.dynamic_slice_in_dim(sg["gdn_conv_w"], chip * n_gc, n_gc, axis=2)
    grads["lru_conv_w"] = lax.dynamic_slice_in_dim(sg["lru_conv_w"], chip * n_lc, n_lc, axis=2)
    grads.update(w_in=g_in, w_out=g_out, w_up=g_up, w_down=g_down)

    names = ['norm1_g', 'w_in', 'hgrn_lb_logits', 'hgrn_norm_g', 'gdn_conv_w', 'gdn_a_log', 'gdn_dt_bias', 'gdn_norm_g',
             'lru_conv_w', 'lru_conv_b', 'lru_w_a', 'lru_b_a', 'lru_w_x', 'lru_b_x', 'lru_lambda', 'w_out', 'norm2_g',
             'w_up', 'w_down', 'final_norm_g']
    big_names = ("w_in", "w_out", "w_up", "w_down")
    delta, new_m, new_v = {}, {}, {}
    for k in big_names:
        delta[k], new_m[k], new_v[k] = _adam("adam_" + k, args[k], grads[k], args["m_" + k], args["v_" + k])
    small_names = [k for k in names if k not in big_names]
    shapes = {k: args[k].shape for k in small_names}
    flat = [_flatten_small({k: src(k) for k in small_names})
            for src in (lambda k: args[k], lambda k: grads[k], lambda k: args["m_" + k], lambda k: args["v_" + k])]
    outs = _ew("adam_small", _adam_fn, flat, 3)
    for dst, o in zip((delta, new_m, new_v), outs):
        dst.update(_unflatten_small(o, shapes))
    return (loss_out, grad_x[None], *[grads[k] for k in names], *[delta[k] for k in names],
            *[new_m[k] for k in names], *[new_v[k] for k in names])
```

```python
import functools

import jax
import jax.numpy as jnp
from jax import lax
from jax.experimental import pallas as pl
from jax.experimental.pallas import tpu as pltpu

F32 = jnp.float32
BF16 = jnp.bfloat16
MESH = pl.DeviceIdType.MESH

DEPTH = 4
D_MODEL = 1024
A_W, B_W, C_W = 256, 512, 256
A_HD, B_HD = 64, 128
B_NH = B_W // B_HD
D_IN = 3592
D_IN_PAD = 3840
COL_BZ, COL_AQ, COL_AF, COL_AI, COL_AG, COL_CX, COL_CY = 1536, 2048, 2304, 2560, 2816, 3072, 3328
GATE0 = 3584
D_FF = 4096
CHUNK = 64
SUB = 16
EPS = 1e-6
TINY = 1e-30
RG_C = 8.0
LR, B1, B2, AEPS, WD, STEP = 0.001, 0.9, 0.999, 1e-08, 0.01, 10
VMEM_LIMIT = 56 * 1024 * 1024
PACK_ROWS = 13 * 1024

NN = ((1,), (0,))
NT = ((1,), (1,))
TN = ((0,), (0,))


DIMS = {"nn": NN, "nt": NT, "tn": TN}


def _split(a, n):
    parts, r = [], a
    for i in range(n):
        p = r.astype(BF16)
        parts.append(p)
        if i + 1 < n:
            r = r - p.astype(F32)
    return parts


def _split_dot(a, b, mode, na, nb):
    (ca,), (cb,) = DIMS[mode]
    dims = ((DIMS[mode]), ((), ())) if a.ndim == 2 else (((ca + 1,), (cb + 1,)), ((0,), (0,)))
    pa, pb = _split(a, na), _split(b, nb)
    acc = None
    for i in range(na):
        for j in range(nb):
            if i + j < max(na, nb):
                t = lax.dot_general(pa[i], pb[j], dims, preferred_element_type=F32)
                acc = t if acc is None else acc + t
    return acc


@functools.partial(jax.custom_vjp, nondiff_argnums=(2, 3, 4, 5))
def _mdot(a, b, mode, na, nb, need):
    return _split_dot(a, b, mode, na, nb)


def _mdot_fwd(a, b, mode, na, nb, need):
    return _split_dot(a, b, mode, na, nb), (a, b)


def _mdot_bwd(mode, na, nb, need, res, ct):
    a, b = res
    n = max(na, nb)
    da, db = jnp.zeros_like(a), jnp.zeros_like(b)
    if mode == "nn":
        if need[0]:
            da = _split_dot(ct, b, "nt", n, nb)
        if need[1]:
            db = _split_dot(a, ct, "tn", na, n)
    elif mode == "nt":
        if need[0]:
            da = _split_dot(ct, b, "nn", n, nb)
        if need[1]:
            db = _split_dot(ct, a, "tn", n, na)
    else:
        if need[0]:
            da = _split_dot(b, ct, "nt", nb, n)
        if need[1]:
            db = _split_dot(a, ct, "nn", na, n)
    return da, db


_mdot.defvjp(_mdot_fwd, _mdot_bwd)
BOTH = (True, True)


def _bdot(a, b, mode="nn"):
    return _mdot(a, b, mode, 1, 1, BOTH)


def _group_sum(x, ones):
    return _mdot(x, ones, "nn", 2, 1, (True, False))


def _running_sum(x):
    shape = x.shape[:-2] + (CHUNK, CHUNK)
    tri = (_iota(shape, len(shape) - 1) <= _iota(shape, len(shape) - 2)).astype(F32)
    return _mdot(tri, x, "nn", 1, 3, (False, True))


def _iota(shape, d):
    return lax.broadcasted_iota(jnp.int32, shape, d)


def _block_ones(width, group):
    return (_iota((width, width), 0) // group == _iota((width, width), 1) // group).astype(F32)


def _params(sem):
    return pltpu.CompilerParams(dimension_semantics=sem, vmem_limit_bytes=VMEM_LIMIT)


def _tile(dim, prefs):
    for p in prefs:
        if dim % p == 0:
            return p
    return dim


def _whole(shape):
    return pl.BlockSpec(shape, lambda i: (0,) * len(shape))


def _rowspec(t, w, col0=0):
    cb = col0 // w
    return pl.BlockSpec((t, w), lambda i: (i, cb))


def _cols(arr, col0, width):
    assert col0 % width == 0
    return (arr, col0, width)


def _row_operands(rows, t):
    arrs, specs = [], []
    for r in rows:
        arr, col0, w = r if isinstance(r, tuple) else (r, 0, r.shape[1])
        arrs.append(arr)
        specs.append(_rowspec(t, w, col0))
    return arrs, specs


def _rowwise_fwd(name, fn, params, rows, out_widths, out_dtypes, t):
    rows, row_specs = _row_operands(rows, t)
    s = rows[0].shape[0]
    n_p, n_r = len(params), len(rows)

    def body(*refs):
        p = [r[...] for r in refs[:n_p]]
        xs = [r[...].astype(F32) for r in refs[n_p:n_p + n_r]]
        outs = fn(pl.program_id(0) * t, *p, *xs)
        for o_ref, o in zip(refs[n_p + n_r:], outs):
            o_ref[...] = o.astype(o_ref.dtype)

    return pl.pallas_call(
        body, grid=(s // t,),
        in_specs=[_whole(p.shape) for p in params] + row_specs,
        out_specs=[_rowspec(t, w) for w in out_widths],
        out_shape=[jax.ShapeDtypeStruct((s, w), d) for w, d in zip(out_widths, out_dtypes)],
        name=name, compiler_params=_params(("arbitrary",)))(*params, *rows)


def _rowwise_bwd(name, fn, params, rows, douts, t, need=None, res=None):
    rows, row_specs = _row_operands(rows, t)
    s = rows[0].shape[0]
    n_p, n_r, n_o = len(params), len(rows), len(douts)
    need = [True] * n_r if need is None else need
    n_res = 0 if res is None else 1

    def body(*refs):
        p = [r[...] for r in refs[:n_p]]
        xs = [r[...].astype(F32) for r in refs[n_p:n_p + n_r]]
        dys = [r[...].astype(F32) for r in refs[n_p + n_r:n_p + n_r + n_o]]
        k = n_p + n_r + n_o
        res_ref = refs[k] if n_res else None
        dp_refs = refs[k + n_res:k + n_res + n_p]
        dx_refs = refs[k + n_res + n_p:]
        row0 = pl.program_id(0) * t
        _, vjp = jax.vjp(lambda *a: tuple(fn(row0, *a)), *p, *xs)
        g = vjp(tuple(dys))

        @pl.when(pl.program_id(0) == 0)
        def _():
            for r in dp_refs:
                r[...] = jnp.zeros_like(r)

        for r, gp in zip(dp_refs, g[:n_p]):
            r[...] += gp
        gx = [gi for gi, nd in zip(g[n_p:], need) if nd]
        for j, (r, gi) in enumerate(zip(dx_refs, gx)):
            if j == 0 and n_res:
                gi = gi + res_ref[...]
            r[...] = gi

    widths = [sp.block_shape[1] for sp, nd in zip(row_specs, need) if nd]
    outs = pl.pallas_call(
        body, grid=(s // t,),
        in_specs=([_whole(p.shape) for p in params] + row_specs
                  + [_rowspec(t, d.shape[1]) for d in douts] + ([_rowspec(t, res.shape[1])] if n_res else [])),
        out_specs=[_whole(p.shape) for p in params] + [_rowspec(t, w) for w in widths],
        out_shape=[jax.ShapeDtypeStruct(p.shape, F32) for p in params] + [jax.ShapeDtypeStruct((s, w), F32) for w in widths],
        name=name, compiler_params=_params(("arbitrary",)))(*params, *rows, *douts, *([res] if n_res else []))
    return list(outs[:n_p]), list(outs[n_p:])


def _rmsnorm_fn(row0, g, x):
    y = x * lax.rsqrt(jnp.mean(x * x, axis=-1, keepdims=True) + EPS)
    return (y * g,)


def _hgrn_pre_fn(layer, row0, logits, aq, af):
    m = jnp.max(logits, axis=0, keepdims=True)
    e = jnp.exp(logits - m)
    p = e / jnp.sum(e, axis=0, keepdims=True)
    if layer == 0:
        lb = jnp.zeros((1, A_W), F32)
    else:
        acc = p[1:2]
        for j in range(2, layer + 1):
            acc = acc + p[j:j + 1]
        lb = jnp.minimum(jnp.maximum(acc, 0.0), 1.0 - EPS)
    sig = jax.nn.sigmoid(af)
    f = lb + (1.0 - lb) * sig
    log_f = jnp.log(jnp.maximum(f, TINY))
    k = (1.0 - lb) * jax.nn.sigmoid(-af)
    return jax.nn.silu(aq), k, log_f


def _hgrn_post_fn(row0, g, o, ag):
    ms = _group_sum(o * o, _block_ones(A_W, A_HD)) * (1.0 / A_HD)
    return (o * lax.rsqrt(ms + EPS) * g * jax.nn.silu(ag),)


def _gdn_pre_fn(row0, a_log, dt_bias, qkvc, gate):
    act = jax.nn.silu(qkvc)
    q, k, v = act[:, :B_W], act[:, B_W:2 * B_W], act[:, 2 * B_W:]
    bo = _block_ones(B_W, B_HD)
    q = q * lax.rsqrt(_group_sum(q * q, bo) + EPS) * (B_HD ** -0.5)
    k = k * lax.rsqrt(_group_sum(k * k, bo) + EPS)
    beta = jax.nn.sigmoid(gate)
    log_alpha = -jnp.exp(a_log) * jax.nn.softplus(gate + dt_bias)
    t = gate.shape[0]
    spread = lambda z, lane0: jnp.concatenate(
        [jnp.broadcast_to(z[:, lane0 + h:lane0 + h + 1], (t, B_HD)) for h in range(B_NH)], axis=1)
    return q, k, v, spread(beta, 0), spread(log_alpha, B_NH)


def _gdn_post_fn(row0, g, o, bz):
    ms = _group_sum(o * o, _block_ones(B_W, B_HD)) * (1.0 / B_HD)
    return (o * lax.rsqrt(ms + EPS) * g * jax.nn.silu(bz),)


def _lru_pre_fn(row0, w_a, b_a, w_x, b_x, lam, xc):
    r = jax.nn.sigmoid(_mdot(xc, w_a, "nn", 2, 2, BOTH) + b_a)
    i = jax.nn.sigmoid(_mdot(xc, w_x, "nn", 2, 2, BOTH) + b_x)
    log_a = -RG_C * r * jax.nn.softplus(-lam)
    a = jnp.exp(log_a)
    one_minus = -jnp.tanh(log_a) * (a * a + 1.0)
    mult = jnp.sqrt(jnp.maximum(one_minus, EPS))
    pos = row0 + _iota(xc.shape, 0)
    mult = jnp.where(pos == 0, 1.0, mult)
    return a, mult * i * xc


def _lru_post_fn(row0, h, cy):
    return (jax.nn.gelu(cy, approximate=True) * h,)


def _lru_da_fn(row0, g, h_prev):
    return (g * h_prev,)


def _relu2(u):
    r = jnp.maximum(u, 0.0)
    return r * r


def _drelu2(acc, u):
    return acc * (2.0 * jnp.maximum(u, 0.0))


MM_VMEM_BUDGET = 44 * 1024 * 1024


def _mm_tiles(mode, m, n, k, a_bytes, b_bytes, n_extra):
    best = None
    for tk in {k, 4096, 2048, 1024, 512, 256}:
        for tn in {n, 1280, 1024, 768, 512, 256, 128}:
            for tm in {m, 1024, 512, 256, 128}:
                if m % tm or n % tn or k % tk or tm > 1024:
                    continue
                tiles = tm * tk * a_bytes + tk * tn * b_bytes + (1 + n_extra) * tm * tn * 4
                if 2 * tiles + (tm * tn * 4 if tk < k else 0) > MM_VMEM_BUDGET:
                    continue
                a_reads = 1 if tk == k else n // tn
                b_reads = 1 if (tk == k and tn == n) else m // tm
                cost = m * k * a_bytes * a_reads + k * n * b_bytes * b_reads + m * n * 4 * (k // tk - 1)
                if best is None or (cost, -tm) < best[0]:
                    best = ((cost, -tm), (tm, tn, tk))
    if best is None:
        raise ValueError("no matmul tiling fits VMEM")
    return best[1]


def _matmul(name, a, b, mode, a_fn=None, res=None, epi=None, epi_in=None, out_dtype=F32):
    if mode == "nn":
        (m, k), n = a.shape, b.shape[1]
    elif mode == "nt":
        (m, k), n = a.shape, b.shape[0]
    else:
        (k, m), n = a.shape, b.shape[1]
    n_extra = (res is not None) + (epi_in is not None)
    tm, tn, tk = _mm_tiles(mode, m, n, k, a.dtype.itemsize, b.dtype.itemsize, n_extra)
    nk = k // tk
    a_spec = {"nn": pl.BlockSpec((tm, tk), lambda i, j, kk: (i, kk)),
              "nt": pl.BlockSpec((tm, tk), lambda i, j, kk: (i, kk)),
              "tn": pl.BlockSpec((tk, tm), lambda i, j, kk: (kk, i))}[mode]
    b_spec = {"nn": pl.BlockSpec((tk, tn), lambda i, j, kk: (kk, j)),
              "nt": pl.BlockSpec((tn, tk), lambda i, j, kk: (j, kk)),
              "tn": pl.BlockSpec((tk, tn), lambda i, j, kk: (kk, j))}[mode]
    o_spec = pl.BlockSpec((tm, tn), lambda i, j, kk: (i, j))
    dims = {"nn": NN, "nt": NT, "tn": TN}[mode]
    extra = [x for x in (res, epi_in) if x is not None]

    def body(*refs):
        a_ref, b_ref = refs[0], refs[1]
        rest = list(refs[2:])
        res_ref = rest.pop(0) if res is not None else None
        epi_ref = rest.pop(0) if epi_in is not None else None
        o_ref = rest[0]
        acc = rest[1] if nk > 1 else None
        kk = pl.program_id(2)
        at = a_ref[...]
        if a_fn is not None:
            at = a_fn(at.astype(F32))
        part = lax.dot_general(at.astype(BF16), b_ref[...].astype(BF16), (dims, ((), ())),
                               preferred_element_type=F32)

        def finish(out):
            if res_ref is not None:
                out = out + res_ref[...]
            if epi is not None:
                out = epi(out, epi_ref[...])
            o_ref[...] = out.astype(o_ref.dtype)

        if nk == 1:
            finish(part)
        else:
            @pl.when(kk == 0)
            def _():
                acc[...] = part

            @pl.when(jnp.logical_and(kk > 0, kk < nk - 1))
            def _():
                acc[...] += part

            @pl.when(kk == nk - 1)
            def _():
                finish(acc[...] + part)

    return pl.pallas_call(
        body, grid=(m // tm, n // tn, nk),
        in_specs=[a_spec, b_spec] + [o_spec] * len(extra), out_specs=o_spec,
        out_shape=jax.ShapeDtypeStruct((m, n), out_dtype),
        scratch_shapes=[pltpu.VMEM((tm, tn), F32)] if nk > 1 else [],
        name=name, compiler_params=_params(("arbitrary", "arbitrary", "arbitrary")))(a, b, *extra)


HALO = 8


def _conv_fwd(name, x, col0, width, w, bias, t=512):
    s = x.shape[0]
    cb = col0 // width
    hb = t // HALO

    def body(x_ref, halo_ref, w_ref, b_ref, y_ref):
        i = pl.program_id(0)
        halo = jnp.where(i == 0, 0.0, halo_ref[...])
        xp = jnp.concatenate([halo, x_ref[...]], axis=0)
        wv = w_ref[...]
        y = b_ref[...] + wv[0:1] * xp[HALO - 3:HALO - 3 + t]
        for j in range(1, 4):
            y = y + wv[j:j + 1] * xp[HALO - 3 + j:HALO - 3 + j + t]
        y_ref[...] = y

    return pl.pallas_call(
        body, grid=(s // t,),
        in_specs=[pl.BlockSpec((t, width), lambda i: (i, cb)),
                  pl.BlockSpec((HALO, width), lambda i: (jnp.maximum(i * hb - 1, 0), cb)),
                  _whole(w.shape), _whole(bias.shape)],
        out_specs=_rowspec(t, width), out_shape=jax.ShapeDtypeStruct((s, width), F32),
        name=name, compiler_params=_params(("arbitrary",)))(x, x, w, bias)


def _conv_bwd(name, dy, x, col0, width, w, t=512):
    s = x.shape[0]
    cb = col0 // width
    hb = t // HALO
    nblk = s // t

    def body(dy_ref, dyn_ref, x_ref, halo_ref, w_ref, dx_ref, dw_ref, db_ref):
        i = pl.program_id(0)
        dyv = dy_ref[...]
        nxt = jnp.where(i == nblk - 1, 0.0, dyn_ref[...])
        dyp = jnp.concatenate([dyv, nxt], axis=0)
        wv = w_ref[...]
        dx = wv[3:4] * dyv
        for j in range(3):
            dx = dx + wv[j:j + 1] * dyp[3 - j:3 - j + t]
        dx_ref[...] = dx
        halo = jnp.where(i == 0, 0.0, halo_ref[...])
        xp = jnp.concatenate([halo, x_ref[...]], axis=0)
        dw = jnp.concatenate(
            [jnp.sum(dyv * xp[HALO - 3 + j:HALO - 3 + j + t], axis=0, keepdims=True) for j in range(4)], axis=0)

        @pl.when(i == 0)
        def _():
            dw_ref[...] = jnp.zeros_like(dw_ref)
            db_ref[...] = jnp.zeros_like(db_ref)

        dw_ref[...] += dw
        db_ref[...] += jnp.sum(dyv, axis=0, keepdims=True)

    return pl.pallas_call(
        body, grid=(nblk,),
        in_specs=[_rowspec(t, width),
                  pl.BlockSpec((HALO, width), lambda i: (jnp.minimum((i + 1) * hb, s // HALO - 1), 0)),
                  pl.BlockSpec((t, width), lambda i: (i, cb)),
                  pl.BlockSpec((HALO, width), lambda i: (jnp.maximum(i * hb - 1, 0), cb)),
                  _whole(w.shape)],
        out_specs=[_rowspec(t, width), _whole((4, width)), _whole((1, width))],
        out_shape=[jax.ShapeDtypeStruct((s, width), F32), jax.ShapeDtypeStruct((4, width), F32),
                   jax.ShapeDtypeStruct((1, width), F32)],
        name=name, compiler_params=_params(("arbitrary",)))(dy, dy, x, x, w)


def _hgrn_chunk(st, q, k, v, lf):
    cum = _running_sum(lf)
    tot = jnp.sum(lf, axis=0, keepdims=True)
    bm = _iota((A_W, A_W), 0) // A_HD == _iota((A_W, A_W), 1) // A_HD
    bo = bm.astype(F32)
    o_inter = _bdot(q * jnp.exp(cum), st, "nt")
    kd = k * jnp.exp(tot - cum)
    st_new = st * jnp.exp(tot) + jnp.where(bm, _bdot(v, kd, "tn"), 0.0)
    lane_h = _iota((SUB, A_W), 1) // A_HD
    n_h = A_W // A_HD
    t3 = _iota((SUB, SUB, A_W), 0)
    s3 = _iota((SUB, SUB, A_W), 1)
    outs = []
    for i in range(CHUNK // SUB):
        sl = slice(i * SUB, (i + 1) * SUB)
        qi, ki, vi, ci = q[sl], k[sl], v[sl], cum[sl]
        e = jnp.exp(jnp.minimum(ci[:, None, :] - ci[None, :, :], 0.0))
        d3 = jnp.where(s3 <= t3, qi[:, None, :] * ki[None, :, :] * e, 0.0)
        ds = _mdot(d3.reshape(SUB * SUB, A_W), bo, "nn", 1, 1, (True, False)).reshape(SUB, SUB, A_W)
        oi = jnp.sum(ds * vi[None, :, :], axis=1)
        if i > 0:
            n = i * SUB
            cb = cum[n - 1:n]
            qt = qi * jnp.exp(ci - cb)
            kt = k[:n] * jnp.exp(jnp.minimum(cb - cum[:n], 0.0))
            qs = jnp.concatenate([jnp.where(lane_h == h, qt, 0.0) for h in range(n_h)], axis=0)
            p = _bdot(_bdot(qs, kt, "nt"), v[:n])
            for h in range(n_h):
                oi = oi + jnp.where(lane_h == h, p[h * SUB:(h + 1) * SUB], 0.0)
        outs.append(oi)
    return st_new, o_inter + jnp.concatenate(outs, axis=0)


HGRN_CHUNKS = 2


def _hgrn_fwd(name, q, k, v, lf):
    s = q.shape[0]
    n = s // CHUNK
    g = HGRN_CHUNKS

    def body(q_ref, k_ref, v_ref, lf_ref, o_ref, sts_ref, st_ref):
        @pl.when(pl.program_id(0) == 0)
        def _():
            st_ref[...] = jnp.zeros_like(st_ref)

        st = st_ref[...]
        for c in range(g):
            rows = pl.ds(c * CHUNK, CHUNK)
            sts_ref[c] = st
            st, o = _hgrn_chunk(st, q_ref[rows], k_ref[rows], v_ref[rows], lf_ref[rows])
            o_ref[rows] = o
        st_ref[...] = st

    cs = _rowspec(g * CHUNK, A_W)
    return pl.pallas_call(
        body, grid=(n // g,), in_specs=[cs] * 4,
        out_specs=[cs, pl.BlockSpec((g, A_W, A_W), lambda i: (i, 0, 0))],
        out_shape=[jax.ShapeDtypeStruct((s, A_W), F32), jax.ShapeDtypeStruct((n, A_W, A_W), F32)],
        scratch_shapes=[pltpu.VMEM((A_W, A_W), F32)],
        name=name, compiler_params=_params(("arbitrary",)))(q, k, v, lf)


def _hgrn_bwd(name, q, k, v, lf, sts, do):
    s = q.shape[0]
    n = s // CHUNK
    g = HGRN_CHUNKS
    nsteps = n // g

    def body(q_ref, k_ref, v_ref, lf_ref, sts_ref, do_ref, dq_ref, dk_ref, dv_ref, dlf_ref, dst_ref):
        @pl.when(pl.program_id(0) == 0)
        def _():
            dst_ref[...] = jnp.zeros_like(dst_ref)

        dst = dst_ref[...]
        for c in reversed(range(g)):
            rows = pl.ds(c * CHUNK, CHUNK)
            _, vjp = jax.vjp(_hgrn_chunk, sts_ref[c], q_ref[rows], k_ref[rows], v_ref[rows], lf_ref[rows])
            ct = vjp((dst, do_ref[rows]))
            dst = ct[0]
            for r, gi in zip((dq_ref, dk_ref, dv_ref, dlf_ref), ct[1:]):
                r[rows] = gi
        dst_ref[...] = dst

    rs = pl.BlockSpec((g * CHUNK, A_W), lambda i: (nsteps - 1 - i, 0))
    return pl.pallas_call(
        body, grid=(nsteps,),
        in_specs=[rs] * 4 + [pl.BlockSpec((g, A_W, A_W), lambda i: (nsteps - 1 - i, 0, 0)), rs],
        out_specs=[rs] * 4, out_shape=[jax.ShapeDtypeStruct((s, A_W), F32)] * 4,
        scratch_shapes=[pltpu.VMEM((A_W, A_W), F32)],
        name=name, compiler_params=_params(("arbitrary",)))(q, k, v, lf, sts, do)


PREP_CHUNKS = 4
REC_CHUNKS = 4
DEC_ROWS = 8


def _to_heads(x, rows=CHUNK, width=B_HD):
    n = x.shape[0] // rows
    x3 = x.reshape(n, rows, x.shape[1])
    return jnp.concatenate([x3[:, :, h * B_HD:h * B_HD + width] for h in range(B_NH)], axis=0)


def _from_heads(y):
    n = y.shape[0] // B_NH
    x3 = jnp.concatenate([y[h * n:(h + 1) * n] for h in range(B_NH)], axis=2)
    return x3.reshape(n * y.shape[1], B_NH * B_HD)


@jax.custom_vjp
def _unit_lower_inverse(a_mat):
    nil = -a_mat
    eye = (_iota(a_mat.shape, 1) == _iota(a_mat.shape, 2)).astype(F32)
    inv = eye + nil
    sq = 2
    while sq < CHUNK:
        nil = _split_dot(nil, nil, "nn", 2, 2)
        inv = inv + _split_dot(inv, nil, "nn", 2, 2)
        sq *= 2
    return inv


def _unit_lower_inverse_fwd(a_mat):
    inv = _unit_lower_inverse(a_mat)
    return inv, inv


def _unit_lower_inverse_bwd(inv, ct):
    return (-_split_dot(_split_dot(inv, ct, "tn", 2, 2), inv, "nt", 2, 2),)


_unit_lower_inverse.defvjp(_unit_lower_inverse_fwd, _unit_lower_inverse_bwd)


def _gdn_prep(q, k, v, beta, la):
    q, k, v, beta, la = [_to_heads(z) for z in (q, k, v, beta, la)]
    nb = q.shape[0]
    r = _iota((nb, CHUNK, CHUNK), 1)
    c = _iota((nb, CHUNK, CHUNK), 2)
    causal = c <= r
    g = _running_sum(la)
    tot = jnp.sum(la, axis=1, keepdims=True)
    gcol = g[:, :, :CHUNK]
    decay = jnp.where(causal, jnp.exp(jnp.where(causal, gcol - jnp.swapaxes(gcol, 1, 2), 0.0)), 0.0)
    kb = k * beta
    a_mat = jnp.where(c < r, _bdot(kb, k, "nt") * decay, 0.0)
    inv = _unit_lower_inverse(a_mat)
    eg = jnp.exp(g)
    u = _bdot(inv, v * beta)
    w = _bdot(inv, kb * eg)
    qk = jnp.where(causal, _bdot(q, k, "nt") * decay, 0.0)
    qk = jnp.concatenate([qk, jnp.zeros((nb, CHUNK, B_HD - CHUNK), F32)], axis=2)
    dec = jnp.broadcast_to(jnp.exp(tot), (nb, DEC_ROWS, B_HD))
    return tuple(_from_heads(z) for z in (u, w, qk, q * eg, k * jnp.exp(tot - g), dec))


def _gdn_rec(st, u, w, qk, qd, kd, dec):
    u, w, qd, kd = [_to_heads(z) for z in (u, w, qd, kd)]
    qk = _to_heads(qk, width=CHUNK)
    v_new = u - _bdot(w, st)
    o = _bdot(qd, st) + _bdot(qk, v_new)
    st_new = st * _to_heads(dec, rows=DEC_ROWS)[:, :1, :1] + _bdot(kd, v_new, "tn")
    return st_new, _from_heads(o)


def _gdn_prep_fwd(name, q, k, v, beta, la):
    s = q.shape[0]
    t = PREP_CHUNKS * CHUNK
    td = PREP_CHUNKS * DEC_ROWS

    def body(*refs):
        outs = _gdn_prep(*[r[...] for r in refs[:5]])
        for o_ref, o in zip(refs[5:], outs):
            o_ref[...] = o

    return pl.pallas_call(
        body, grid=(s // t,), in_specs=[_rowspec(t, B_W)] * 5,
        out_specs=[_rowspec(t, B_W)] * 5 + [_rowspec(td, B_W)],
        out_shape=[jax.ShapeDtypeStruct((s, B_W), F32)] * 5 + [jax.ShapeDtypeStruct((s // CHUNK * DEC_ROWS, B_W), F32)],
        name=name, compiler_params=_params(("arbitrary",)))(q, k, v, beta, la)


def _gdn_prep_bwd(name, q, k, v, beta, la, cts):
    s = q.shape[0]
    t = PREP_CHUNKS * CHUNK
    td = PREP_CHUNKS * DEC_ROWS

    def body(*refs):
        _, vjp = jax.vjp(_gdn_prep, *[r[...] for r in refs[:5]])
        g = vjp(tuple(r[...] for r in refs[5:11]))
        for o_ref, o in zip(refs[11:], g):
            o_ref[...] = o

    return pl.pallas_call(
        body, grid=(s // t,), in_specs=[_rowspec(t, B_W)] * 10 + [_rowspec(td, B_W)],
        out_specs=[_rowspec(t, B_W)] * 5, out_shape=[jax.ShapeDtypeStruct((s, B_W), F32)] * 5,
        name=name, compiler_params=_params(("arbitrary",)))(q, k, v, beta, la, *cts)


def _gdn_rec_fwd(name, u, w, qk, qd, kd, dec):
    s = u.shape[0]
    n = s // CHUNK
    g = REC_CHUNKS

    def body(u_ref, w_ref, qk_ref, qd_ref, kd_ref, dec_ref, o_ref, sts_ref, st_ref):
        @pl.when(pl.program_id(0) == 0)
        def _():
            st_ref[...] = jnp.zeros_like(st_ref)

        st = st_ref[...]
        for c in range(g):
            rows, drows = pl.ds(c * CHUNK, CHUNK), pl.ds(c * DEC_ROWS, DEC_ROWS)
            sts_ref[c] = st
            st, o = _gdn_rec(st, u_ref[rows], w_ref[rows], qk_ref[rows], qd_ref[rows], kd_ref[rows], dec_ref[drows])
            o_ref[rows] = o
        st_ref[...] = st

    cs = _rowspec(g * CHUNK, B_W)
    return pl.pallas_call(
        body, grid=(n // g,), in_specs=[cs] * 5 + [_rowspec(g * DEC_ROWS, B_W)],
        out_specs=[cs, pl.BlockSpec((g, B_NH, B_HD, B_HD), lambda i: (i, 0, 0, 0))],
        out_shape=[jax.ShapeDtypeStruct((s, B_W), F32), jax.ShapeDtypeStruct((n, B_NH, B_HD, B_HD), F32)],
        scratch_shapes=[pltpu.VMEM((B_NH, B_HD, B_HD), F32)],
        name=name, compiler_params=_params(("arbitrary",)))(u, w, qk, qd, kd, dec)


def _gdn_rec_bwd(name, u, w, qk, qd, kd, dec, sts, do):
    s = u.shape[0]
    n = s // CHUNK
    g = REC_CHUNKS
    nsteps = n // g

    def body(u_ref, w_ref, qk_ref, qd_ref, kd_ref, dec_ref, sts_ref, do_ref,
             du_ref, dw_ref, dqk_ref, dqd_ref, dkd_ref, ddec_ref, dst_ref):
        @pl.when(pl.program_id(0) == 0)
        def _():
            dst_ref[...] = jnp.zeros_like(dst_ref)

        dst = dst_ref[...]
        for c in reversed(range(g)):
            rows, drows = pl.ds(c * CHUNK, CHUNK), pl.ds(c * DEC_ROWS, DEC_ROWS)
            _, vjp = jax.vjp(_gdn_rec, sts_ref[c], u_ref[rows], w_ref[rows], qk_ref[rows], qd_ref[rows], kd_ref[rows],
                             dec_ref[drows])
            ct = vjp((dst, do_ref[rows]))
            dst = ct[0]
            for r, gi in zip((du_ref, dw_ref, dqk_ref, dqd_ref, dkd_ref), ct[1:6]):
                r[rows] = gi
            ddec_ref[drows] = ct[6]
        dst_ref[...] = dst

    rs = pl.BlockSpec((g * CHUNK, B_W), lambda i: (nsteps - 1 - i, 0))
    ds = pl.BlockSpec((g * DEC_ROWS, B_W), lambda i: (nsteps - 1 - i, 0))
    return pl.pallas_call(
        body, grid=(nsteps,),
        in_specs=[rs] * 5 + [ds, pl.BlockSpec((g, B_NH, B_HD, B_HD), lambda i: (nsteps - 1 - i, 0, 0, 0)), rs],
        out_specs=[rs] * 5 + [ds],
        out_shape=[jax.ShapeDtypeStruct((s, B_W), F32)] * 5 + [jax.ShapeDtypeStruct((n * DEC_ROWS, B_W), F32)],
        scratch_shapes=[pltpu.VMEM((B_NH, B_HD, B_HD), F32)],
        name=name, compiler_params=_params(("arbitrary",)))(u, w, qk, qd, kd, dec, sts, do)


SCAN_T = 512


def _block_scan(a, b, reverse):
    t = a.shape[0]
    rows = _iota(a.shape, 0)
    sft = 1
    while sft < t:
        if reverse:
            a_s, b_s = pltpu.roll(a, t - sft, 0), pltpu.roll(b, t - sft, 0)
            edge = rows >= t - sft
        else:
            a_s, b_s = pltpu.roll(a, sft, 0), pltpu.roll(b, sft, 0)
            edge = rows < sft
        a_s = jnp.where(edge, 1.0, a_s)
        b_s = jnp.where(edge, 0.0, b_s)
        b = a * b_s + b
        a = a * a_s
        sft *= 2
    return a, b


def _lru_scan_fwd(name, a, b):
    s, w = a.shape
    t = SCAN_T

    def body(a_ref, b_ref, h_ref, hp_ref, carry):
        @pl.when(pl.program_id(0) == 0)
        def _():
            carry[...] = jnp.zeros_like(carry)

        h_in = carry[0:1]
        ca, cb = _block_scan(a_ref[...], b_ref[...], False)
        h = ca * h_in + cb
        h_ref[...] = h
        hp_ref[...] = jnp.where(_iota(h.shape, 0) == 0, h_in, pltpu.roll(h, 1, 0))
        carry[...] = jnp.broadcast_to(h[t - 1:t], carry.shape)

    return pl.pallas_call(
        body, grid=(s // t,), in_specs=[_rowspec(t, w)] * 2, out_specs=[_rowspec(t, w)] * 2,
        out_shape=[jax.ShapeDtypeStruct((s, w), F32)] * 2, scratch_shapes=[pltpu.VMEM((8, w), F32)],
        name=name, compiler_params=_params(("arbitrary",)))(a, b)


def _lru_scan_bwd(name, a, dh):
    s, w = a.shape
    t = SCAN_T
    n = s // t

    def body(a_ref, dh_ref, g_ref, a_next, g_next):
        @pl.when(pl.program_id(0) == 0)
        def _():
            a_next[...] = jnp.zeros_like(a_next)
            g_next[...] = jnp.zeros_like(g_next)

        av = a_ref[...]
        a_up = jnp.where(_iota(av.shape, 0) == t - 1, a_next[0:1], pltpu.roll(av, t - 1, 0))
        ca, cb = _block_scan(a_up, dh_ref[...], True)
        g = ca * g_next[0:1] + cb
        g_ref[...] = g
        a_next[...] = jnp.broadcast_to(av[0:1], a_next.shape)
        g_next[...] = jnp.broadcast_to(g[0:1], g_next.shape)

    rs = pl.BlockSpec((t, w), lambda i: (n - 1 - i, 0))
    return pl.pallas_call(
        body, grid=(n,), in_specs=[rs] * 2, out_specs=rs, out_shape=jax.ShapeDtypeStruct((s, w), F32),
        scratch_shapes=[pltpu.VMEM((8, w), F32), pltpu.VMEM((8, w), F32)],
        name=name, compiler_params=_params(("arbitrary",)))(a, dh)


def _loss_fn(g, x, tgt):
    y = x * lax.rsqrt(jnp.mean(x * x, axis=-1, keepdims=True) + EPS) * g
    err = y - tgt
    return 0.5 * jnp.sum(jnp.mean(err * err, axis=-1, keepdims=True), axis=0, keepdims=True)


def _loss_and_grad(name, g, x, tgt, t=256):
    s, d = x.shape

    def body(g_ref, x_ref, t_ref, loss_ref, dx_ref, dg_ref):
        val, vjp = jax.vjp(_loss_fn, g_ref[...], x_ref[...], t_ref[...])
        dg, dx, _ = vjp(jnp.ones((1, 1), F32))

        @pl.when(pl.program_id(0) == 0)
        def _():
            loss_ref[...] = jnp.zeros_like(loss_ref)
            dg_ref[...] = jnp.zeros_like(dg_ref)

        loss_ref[...] += jnp.broadcast_to(val, loss_ref.shape)
        dg_ref[...] += dg
        dx_ref[...] = dx

    return pl.pallas_call(
        body, grid=(s // t,), in_specs=[_whole(g.shape), _rowspec(t, d), _rowspec(t, d)],
        out_specs=[_whole((1, 128)), _rowspec(t, d), _whole(g.shape)],
        out_shape=[jax.ShapeDtypeStruct((1, 128), F32), jax.ShapeDtypeStruct((s, d), F32),
                   jax.ShapeDtypeStruct(g.shape, F32)],
        name=name, compiler_params=_params(("arbitrary",)))(g, x, tgt)


def _ew(name, fn, ins, n_out, t=None):
    r, c = ins[0].shape
    t = _tile(r, (512, 256, 128, 64, 32, 16, 8)) if t is None else t

    def body(*refs):
        outs = fn(*[x[...] for x in refs[:len(ins)]])
        for o_ref, o in zip(refs[len(ins):], outs):
            o_ref[...] = o

    return pl.pallas_call(
        body, grid=(r // t,), in_specs=[_rowspec(t, c)] * len(ins), out_specs=[_rowspec(t, c)] * n_out,
        out_shape=[jax.ShapeDtypeStruct((r, c), F32)] * n_out,
        name=name, compiler_params=_params(("arbitrary",)))(*ins)


def _adam_fn(w, g, m, v):
    m = B1 * m + (1.0 - B1) * g
    v = B2 * v + (1.0 - B2) * (g * g)
    m_hat = m / (1.0 - B1 ** STEP)
    v_hat = v / (1.0 - B2 ** STEP)
    delta = -LR * (m_hat / (jnp.sqrt(v_hat) + AEPS) + WD * w)
    return delta, m, v


def _add2_fn(a, b):
    return (a + b,)


def _adam(name, w, g, m, v):
    shp = w.shape
    two = lambda z: z.reshape(-1, shp[-1])
    outs = _ew(name, _adam_fn, [two(w), two(g), two(m), two(v)], 3)
    return [o.reshape(shp) for o in outs]


def _swap(name, bufs, flips):
    n = len(bufs)
    hbm = pl.BlockSpec(memory_space=pltpu.HBM)

    def body(*refs):
        srcs, dsts = refs[:n], refs[n:2 * n]
        send_sems, recv_sems = refs[2 * n], refs[2 * n + 1]
        me = (lax.axis_index("x"), lax.axis_index("y"), lax.axis_index("c"))
        copies = []
        for i in range(n):
            peer = tuple(1 - m if f else m for m, f in zip(me, flips[i]))
            cp = pltpu.make_async_remote_copy(src_ref=srcs[i], dst_ref=dsts[i], send_sem=send_sems.at[i],
                                              recv_sem=recv_sems.at[i], device_id=peer, device_id_type=MESH)
            cp.start()
            copies.append(cp)
        for cp in copies:
            cp.wait()

    return pl.pallas_call(
        body, in_specs=[hbm] * n, out_specs=[hbm] * n,
        out_shape=[jax.ShapeDtypeStruct(b.shape, b.dtype) for b in bufs],
        scratch_shapes=[pltpu.SemaphoreType.DMA((n,)), pltpu.SemaphoreType.DMA((n,))],
        name=name, compiler_params=pltpu.CompilerParams(has_side_effects=True))(*bufs)


FLIP_C = (False, False, True)
FLIP_Y = (False, True, False)
FLIP_X = (True, False, False)


HALF_ROWS = PACK_ROWS // 2
SUM_ROWS = 512


def _mesh_pos():
    ix, iy, core = lax.axis_index("x"), lax.axis_index("y"), lax.axis_index("c")
    others = [(ix, 1 - iy), (1 - ix, iy), (1 - ix, 1 - iy)]
    return ix, iy, core, others


def _pack_big(w_in, w_out, w_up, w_down):
    pad = jnp.pad(w_in, ((0, 0), (0, 0), (0, 1024 - w_in.shape[-1])))
    return jnp.concatenate([pad.reshape(-1, 1024), w_up.reshape(-1, 1024), w_down.reshape(-1, 1024),
                            w_out.reshape(-1, 1024)], axis=0)


def _unpack_big(p):
    w_in = p[0:4096].reshape(4, 1024, 1024)[:, :, :D_IN // 4]
    w_up = p[4096:8192].reshape(4, 1024, 1024)
    w_down = p[8192:12288].reshape(4, 1024, 1024)
    w_out = p[12288:13312].reshape(4, 256, 1024)
    return w_in, w_out, w_up, w_down


def _gather_big(packed):
    hbm = pl.BlockSpec(memory_space=pltpu.HBM)

    def body(src, out, send_sems, recv_sems):
        ix, iy, core, others = _mesh_pos()
        mine = pl.ds(core * HALF_ROWS, HALF_ROWS)
        theirs = pl.ds((1 - core) * HALF_ROWS, HALF_ROWS)

        def copy(k, src_ref, owner, rows, to):
            return pltpu.make_async_remote_copy(
                src_ref=src_ref, dst_ref=out.at[owner, rows], send_sem=send_sems.at[k], recv_sem=recv_sems.at[k],
                device_id=to, device_id_type=MESH)

        chip = 2 * ix + iy
        first = [copy(j, src.at[mine], chip, mine, (*others[j], core)) for j in range(3)]
        for cp in first:
            cp.start()
        passed = []
        for j, (ox, oy) in enumerate(others):
            owner = 2 * ox + oy
            copy(j, src.at[mine], owner, mine, (ix, iy, core)).wait_recv()
            fwd = copy(3 + j, out.at[owner, mine], owner, mine, (ix, iy, 1 - core))
            fwd.start()
            passed.append(fwd)
        for j, (ox, oy) in enumerate(others):
            copy(3 + j, src.at[theirs], 2 * ox + oy, theirs, (ix, iy, core)).wait_recv()
        for cp in first + passed:
            cp.wait_send()

    return pl.pallas_call(
        body, in_specs=[hbm], out_specs=hbm, out_shape=jax.ShapeDtypeStruct((4,) + packed.shape, packed.dtype),
        scratch_shapes=[pltpu.SemaphoreType.DMA((6,)), pltpu.SemaphoreType.DMA((6,))],
        name="gather_big", compiler_params=pltpu.CompilerParams(has_side_effects=True))(packed)


def _rs_sibling(per_chip):
    hbm = pl.BlockSpec(memory_space=pltpu.HBM)

    def body(src, out, send_sem, recv_sem):
        ix, iy, core, _ = _mesh_pos()
        give = pl.ds((1 - core) * HALF_ROWS, HALF_ROWS)
        cp = pltpu.make_async_remote_copy(src_ref=src.at[pl.ds(0, 4), give], dst_ref=out, send_sem=send_sem,
                                          recv_sem=recv_sem, device_id=(ix, iy, 1 - core), device_id_type=MESH)
        cp.start()
        cp.wait()

    return pl.pallas_call(
        body, in_specs=[hbm], out_specs=hbm, out_shape=jax.ShapeDtypeStruct((4, HALF_ROWS, 1024), F32),
        scratch_shapes=[pltpu.SemaphoreType.DMA, pltpu.SemaphoreType.DMA],
        name="rs_sibling", compiler_params=pltpu.CompilerParams(has_side_effects=True))(per_chip)


def _rs_add_pair(per_chip, sib, chip, core):
    nb = HALF_ROWS // SUM_ROWS
    blk = (1, SUM_ROWS, 1024)

    def body(chip_ref, core_ref, a_ref, b_ref, own_ref, send_ref):
        total = a_ref[0] + b_ref[0]
        send_ref[0] = total.astype(BF16)

        @pl.when(pl.program_id(1) == chip_ref[0])
        def _():
            own_ref[...] = total

    return pl.pallas_call(
        body,
        grid_spec=pltpu.PrefetchScalarGridSpec(
            num_scalar_prefetch=2, grid=(nb, 4),
            in_specs=[pl.BlockSpec(blk, lambda i, k, ch, co: (k, co[0] * nb + i, 0)),
                      pl.BlockSpec(blk, lambda i, k, ch, co: (k, i, 0))],
            out_specs=[pl.BlockSpec((SUM_ROWS, 1024), lambda i, k, ch, co: (i, 0)),
                       pl.BlockSpec(blk, lambda i, k, ch, co: (k, i, 0))]),
        out_shape=[jax.ShapeDtypeStruct((HALF_ROWS, 1024), F32), jax.ShapeDtypeStruct((4, HALF_ROWS, 1024), BF16)],
        name="rs_add_pair", compiler_params=_params(("arbitrary", "arbitrary")))(
            chip.reshape(1), core.reshape(1), per_chip, sib)


def _rs_chips(pair):
    hbm = pl.BlockSpec(memory_space=pltpu.HBM)

    def body(src, out, send_sems, recv_sems):
        ix, iy, core, others = _mesh_pos()
        copies = []
        for j, (ox, oy) in enumerate(others):
            cp = pltpu.make_async_remote_copy(src_ref=src.at[2 * ox + oy], dst_ref=out.at[j], send_sem=send_sems.at[j],
                                              recv_sem=recv_sems.at[j], device_id=(ox, oy, core), device_id_type=MESH)
            cp.start()
            copies.append(cp)
        for cp in copies:
            cp.wait()

    return pl.pallas_call(
        body, in_specs=[hbm], out_specs=hbm, out_shape=jax.ShapeDtypeStruct((3, HALF_ROWS, 1024), pair.dtype),
        scratch_shapes=[pltpu.SemaphoreType.DMA((3,)), pltpu.SemaphoreType.DMA((3,))],
        name="rs_chips", compiler_params=pltpu.CompilerParams(has_side_effects=True))(pair)


def _rs_add_chips(own, got, core):
    nb = HALF_ROWS // SUM_ROWS
    blk = (1, SUM_ROWS, 1024)

    def body(core_ref, a_ref, g0_ref, g1_ref, g2_ref, o_ref):
        o_ref[...] = ((a_ref[...] + g0_ref[0].astype(F32)) + g1_ref[0].astype(F32)) + g2_ref[0].astype(F32)

    return pl.pallas_call(
        body,
        grid_spec=pltpu.PrefetchScalarGridSpec(
            num_scalar_prefetch=1, grid=(nb,),
            in_specs=[pl.BlockSpec((SUM_ROWS, 1024), lambda i, c: (i, 0))]
            + [pl.BlockSpec(blk, functools.partial(lambda j, i, c: (j, i, 0), j)) for j in range(3)],
            out_specs=pl.BlockSpec((SUM_ROWS, 1024), lambda i, c: (c[0] * nb + i, 0))),
        out_shape=jax.ShapeDtypeStruct((PACK_ROWS, 1024), F32),
        name="rs_add_chips", compiler_params=_params(("arbitrary",)))(core.reshape(1), own, got, got, got)


def _rs_share(buf):
    hbm = pl.BlockSpec(memory_space=pltpu.HBM)

    def body(src, out, send_sem, recv_sem):
        ix, iy, core, _ = _mesh_pos()
        mine = pl.ds(core * HALF_ROWS, HALF_ROWS)
        theirs = pl.ds((1 - core) * HALF_ROWS, HALF_ROWS)
        cp = pltpu.make_async_remote_copy(src_ref=src.at[mine], dst_ref=out.at[mine], send_sem=send_sem,
                                          recv_sem=recv_sem, device_id=(ix, iy, 1 - core), device_id_type=MESH)
        cp.start()
        pltpu.make_async_remote_copy(src_ref=src.at[theirs], dst_ref=out.at[theirs], send_sem=send_sem,
                                     recv_sem=recv_sem, device_id=(ix, iy, core), device_id_type=MESH).wait_recv()
        cp.wait_send()

    return pl.pallas_call(
        body, in_specs=[hbm], out_specs=hbm, out_shape=jax.ShapeDtypeStruct(buf.shape, buf.dtype),
        input_output_aliases={0: 0}, scratch_shapes=[pltpu.SemaphoreType.DMA, pltpu.SemaphoreType.DMA],
        name="rs_share", compiler_params=pltpu.CompilerParams(has_side_effects=True))(buf)


def _reduce_scatter_big(per_chip, chip, core):
    own, pair = _rs_add_pair(per_chip, _rs_sibling(per_chip), chip, core)
    return _rs_share(_rs_add_chips(own, _rs_chips(pair), core))


def _allreduce_small(tag, buf):
    for name, flip in (("c", FLIP_C), ("y", FLIP_Y), ("x", FLIP_X)):
        (other,) = _swap(tag + "_swap_" + name, [buf], [flip])
        (buf,) = _ew(tag + "_add_" + name, _add2_fn, [buf, other], 1)
    return buf


def _permute_in_cols(w):
    z = jnp.zeros(w.shape[:-1] + (D_IN_PAD - D_IN,), w.dtype)
    return jnp.concatenate([w[..., 1024:3072], w[..., :1024], w[..., 3080:D_IN], w[..., 3072:3080], z], axis=-1)


IN_COL_SEGMENTS = ((0, 1024, 2048), (1024, 3072, -1024), (3072, 3080, GATE0 - 3072), (3080, D_IN, COL_CX - 3080))


def _orig_cols(w_pad, lo, hi):
    parts = []
    for first, last, shift in IN_COL_SEGMENTS:
        a, b = max(lo, first), min(hi, last)
        if a < b:
            parts.append(w_pad[..., a + shift:b + shift])
    return parts


def _block_diag(w):
    out = jnp.zeros((C_W, C_W), w.dtype)
    for n in range(4):
        out = lax.dynamic_update_slice(out, w[n], (64 * n, 64 * n))
    return out


def _diag_blocks(w):
    return jnp.stack([w[64 * n:64 * (n + 1), 64 * n:64 * (n + 1)] for n in range(4)], axis=0)


def _gate_row(v):
    return jnp.concatenate([jnp.zeros((4,), F32), v, jnp.zeros((120,), F32)]).reshape(1, 128)


SMALL = ("norm1_g", "hgrn_lb_logits", "hgrn_norm_g", "gdn_conv_w", "gdn_a_log", "gdn_dt_bias", "gdn_norm_g",
         "lru_conv_w", "lru_conv_b", "lru_w_a", "lru_b_a", "lru_w_x", "lru_b_x", "lru_lambda", "norm2_g",
         "final_norm_g")


def _flatten_small(tree):
    flat = jnp.concatenate([tree[k].reshape(-1) for k in sorted(tree)])
    rows = -(-flat.shape[0] // (512 * 128)) * 512
    return jnp.pad(flat, (0, rows * 128 - flat.shape[0])).reshape(rows, 128)


def _unflatten_small(buf, shapes):
    flat = buf.reshape(-1)
    out, off = {}, 0
    for k in sorted(shapes):
        size = 1
        for d in shapes[k]:
            size *= d
        out[k] = flat[off:off + size].reshape(shapes[k])
        off += size
    return out


def _layer_fwd(l, x, p, w_in, w_out, w_up, w_down):
    sv = {"x0": x}
    (h,) = _rowwise_fwd("norm1", _rmsnorm_fn, [p["norm1_g"]], [x], [D_MODEL], [BF16], 512)
    proj = _matmul("proj_in", h, w_in, "nn")
    sv["h"], sv["proj"] = h, proj
    aq, af, ag = _cols(proj, COL_AQ, A_W), _cols(proj, COL_AF, A_W), _cols(proj, COL_AG, A_W)
    ai = lax.slice_in_dim(proj, COL_AI, COL_AI + A_W, axis=1)
    bz, gate, cy = _cols(proj, COL_BZ, B_W), _cols(proj, GATE0, 128), _cols(proj, COL_CY, C_W)
    q, k, lf = _rowwise_fwd("hgrn_pre", functools.partial(_hgrn_pre_fn, l), [p["lb_logits"]], [aq, af],
                            [A_W] * 3, [F32] * 3, 512)
    o_a, sts_a = _hgrn_fwd("hgrn_chunk", q, k, ai, lf)
    (y_a,) = _rowwise_fwd("hgrn_post", _hgrn_post_fn, [p["hgrn_norm_g"]], [o_a, ag], [A_W], [BF16], 512)
    sv.update(aq=aq, af=af, ai=ai, ag=ag, hq=q, hk=k, hlf=lf, o_a=o_a, sts_a=sts_a)
    qkvc = _conv_fwd("gdn_conv", proj, 0, 3 * B_W, p["gdn_conv_w"], jnp.zeros((1, 3 * B_W), F32), 256)
    gq, gk, gv, gb, gla = _rowwise_fwd("gdn_pre", _gdn_pre_fn, [p["gdn_a_log"], p["gdn_dt_bias"]], [qkvc, gate],
                                       [B_W] * 5, [F32] * 5, 256)
    wy = _gdn_prep_fwd("gdn_prep", gq, gk, gv, gb, gla)
    o_b, sts_b = _gdn_rec_fwd("gdn_rec", *wy)
    sv["wy"] = wy
    (y_b,) = _rowwise_fwd("gdn_post", _gdn_post_fn, [p["gdn_norm_g"]], [o_b, bz], [B_W], [BF16], 512)
    sv.update(qkvc=qkvc, gate=gate, bz=bz, gq=gq, gk=gk, gv=gv, gb=gb, gla=gla, o_b=o_b, sts_b=sts_b)
    xc = _conv_fwd("lru_conv", proj, COL_CX, C_W, p["lru_conv_w"], p["lru_conv_b"], 512)
    a, b = _rowwise_fwd("lru_pre", _lru_pre_fn, [p["lru_w_a"], p["lru_b_a"], p["lru_w_x"], p["lru_b_x"], p["lru_lambda"]],
                        [xc], [C_W] * 2, [F32] * 2, 512)
    hs, h_prev = _lru_scan_fwd("lru_scan", a, b)
    (y_c,) = _rowwise_fwd("lru_post", _lru_post_fn, [], [hs, cy], [C_W], [BF16], 512)
    sv.update(xc=xc, la=a, hs=hs, h_prev=h_prev, cy=cy)
    mixed = jnp.concatenate([y_a, y_b, y_c], axis=1)
    x1 = _matmul("proj_out", mixed, w_out, "nn", res=x)
    (h2,) = _rowwise_fwd("norm2", _rmsnorm_fn, [p["norm2_g"]], [x1], [D_MODEL], [BF16], 512)
    up = _matmul("mlp_up", h2, w_up, "nn")
    x2 = _matmul("mlp_down", up, w_down, "nn", a_fn=_relu2, res=x1)
    sv.update(mixed=mixed, x1=x1, h2=h2, up=up)
    return x2, sv


def _layer_bwd(l, dx, p, sv, w_in, w_out, w_up, w_down):
    gs = {}
    dup = _matmul("mlp_down_dx", dx, w_down, "nt", epi=_drelu2, epi_in=sv["up"], out_dtype=BF16)
    d_w_down = _matmul("mlp_down_dw", sv["up"], dx, "tn", a_fn=_relu2)
    dh2 = _matmul("mlp_up_dx", dup, w_up, "nt")
    d_w_up = _matmul("mlp_up_dw", sv["h2"], dup, "tn")
    (gs["norm2_g"],), (dx1,) = _rowwise_bwd("norm2_bwd", _rmsnorm_fn, [p["norm2_g"]], [sv["x1"]], [dh2], 512, res=dx)
    dmixed = _matmul("proj_out_dx", dx1, w_out, "nt")
    d_w_out = _matmul("proj_out_dw", sv["mixed"], dx1, "tn")
    dy_a, dy_b, dy_c = dmixed[:, :A_W], dmixed[:, A_W:A_W + B_W], dmixed[:, A_W + B_W:]
    (gs["hgrn_norm_g"],), (do_a, dag) = _rowwise_bwd("hgrn_post_bwd", _hgrn_post_fn, [p["hgrn_norm_g"]],
                                                     [sv["o_a"], sv["ag"]], [dy_a], 512)
    dq, dk, dai, dlf = _hgrn_bwd("hgrn_chunk_bwd", sv["hq"], sv["hk"], sv["ai"], sv["hlf"], sv["sts_a"], do_a)
    (gs["lb_logits"],), (daq, daf) = _rowwise_bwd("hgrn_pre_bwd", functools.partial(_hgrn_pre_fn, l), [p["lb_logits"]],
                                                  [sv["aq"], sv["af"]], [dq, dk, dlf], 512)
    (gs["gdn_norm_g"],), (do_b, dbz) = _rowwise_bwd("gdn_post_bwd", _gdn_post_fn, [p["gdn_norm_g"]],
                                                    [sv["o_b"], sv["bz"]], [dy_b], 512)
    d_wy = _gdn_rec_bwd("gdn_rec_bwd", *sv["wy"], sv["sts_b"], do_b)
    dgq, dgk, dgv, dgb, dgla = _gdn_prep_bwd("gdn_prep_bwd", sv["gq"], sv["gk"], sv["gv"], sv["gb"], sv["gla"], d_wy)
    (gs["gdn_a_log"], gs["gdn_dt_bias"]), (dqkvc, dgate) = _rowwise_bwd(
        "gdn_pre_bwd", _gdn_pre_fn, [p["gdn_a_log"], p["gdn_dt_bias"]], [sv["qkvc"], sv["gate"]],
        [dgq, dgk, dgv, dgb, dgla], 256)
    dqkv, gs["gdn_conv_w"], _ = _conv_bwd("gdn_conv_bwd", dqkvc, sv["proj"], 0, 3 * B_W, p["gdn_conv_w"], 256)
    _, (dhs, dcy) = _rowwise_bwd("lru_post_bwd", _lru_post_fn, [], [sv["hs"], sv["cy"]], [dy_c], 512)
    g = _lru_scan_bwd("lru_scan_bwd", sv["la"], dhs)
    (da,) = _rowwise_fwd("lru_da", _lru_da_fn, [], [g, sv["h_prev"]], [C_W], [F32], 512)
    lru_params = [p["lru_w_a"], p["lru_b_a"], p["lru_w_x"], p["lru_b_x"], p["lru_lambda"]]
    dps, (dxc,) = _rowwise_bwd("lru_pre_bwd", _lru_pre_fn, lru_params, [sv["xc"]], [da, g], 512)
    gs["lru_w_a"], gs["lru_b_a"], gs["lru_w_x"], gs["lru_b_x"], gs["lru_lambda"] = dps
    dcx, gs["lru_conv_w"], gs["lru_conv_b"] = _conv_bwd("lru_conv_bwd", dxc, sv["proj"], COL_CX, C_W, p["lru_conv_w"], 512)
    s = dx.shape[0]
    dproj = jnp.concatenate([dqkv, dbz, daq, daf, dai, dag, dcx, dcy, dgate,
                             jnp.zeros((s, D_IN_PAD - GATE0 - 128), F32)], axis=1).astype(BF16)
    dh = _matmul("proj_in_dx", dproj, w_in, "nt")
    d_w_in = _matmul("proj_in_dw", sv["h"], dproj, "tn")
    (gs["norm1_g"],), (dx0,) = _rowwise_bwd("norm1_bwd", _rmsnorm_fn, [p["norm1_g"]], [sv["x0"]], [dh], 512, res=dx1)
    return dx0, dict(w_in=d_w_in, w_out=d_w_out, w_up=d_w_up, w_down=d_w_down), gs


def _layer_params(l, lb_logits, hgrn_norm_g, gdn_conv_w, gdn_a_log, gdn_dt_bias, gdn_norm_g, lru_conv_w, lru_conv_b,
                  lru_w_a, lru_b_a, lru_w_x, lru_b_x, lru_lambda, norm1_g, norm2_g):
    row = lambda v: v.reshape(1, -1)
    return dict(
        norm1_g=row(norm1_g[l]), norm2_g=row(norm2_g[l]), lb_logits=lb_logits,
        hgrn_norm_g=row(jnp.tile(hgrn_norm_g[l], A_W // A_HD)),
        gdn_conv_w=gdn_conv_w[l], gdn_a_log=_gate_row(gdn_a_log[l]), gdn_dt_bias=_gate_row(gdn_dt_bias[l]),
        gdn_norm_g=row(jnp.tile(gdn_norm_g[l], B_NH)),
        lru_conv_w=lru_conv_w[l], lru_conv_b=row(lru_conv_b[l]), lru_w_a=_block_diag(lru_w_a[l]), lru_b_a=row(lru_b_a[l]),
        lru_w_x=_block_diag(lru_w_x[l]), lru_b_x=row(lru_b_x[l]), lru_lambda=row(lru_lambda[l]))


def _small_grads_to_reference(gs_layers, d_final_g, d_logits):
    st = lambda k, f=lambda z: z: jnp.stack([f(g[k]) for g in gs_layers], axis=0)
    vec = lambda z: z.reshape(-1)
    return dict(
        norm1_g=st("norm1_g", vec), norm2_g=st("norm2_g", vec), hgrn_lb_logits=d_logits,
        hgrn_norm_g=st("hgrn_norm_g", lambda z: z.reshape(A_W // A_HD, A_HD).sum(0)),
        gdn_conv_w=st("gdn_conv_w"), gdn_a_log=st("gdn_a_log", lambda z: z[0, 4:8]),
        gdn_dt_bias=st("gdn_dt_bias", lambda z: z[0, 4:8]),
        gdn_norm_g=st("gdn_norm_g", lambda z: z.reshape(B_NH, B_HD).sum(0)),
        lru_conv_w=st("lru_conv_w"), lru_conv_b=st("lru_conv_b", vec), lru_w_a=st("lru_w_a", _diag_blocks),
        lru_b_a=st("lru_b_a", vec), lru_w_x=st("lru_w_x", _diag_blocks), lru_b_x=st("lru_b_x", vec),
        lru_lambda=st("lru_lambda", vec), final_norm_g=d_final_g.reshape(-1))


def _local_step(x, tgt, small, w_in_full, w_out_full, w_up_full, w_down_full):
    layer_p = [_layer_params(l, small["hgrn_lb_logits"], small["hgrn_norm_g"], small["gdn_conv_w"], small["gdn_a_log"],
                             small["gdn_dt_bias"], small["gdn_norm_g"], small["lru_conv_w"], small["lru_conv_b"],
                             small["lru_w_a"], small["lru_b_a"], small["lru_w_x"], small["lru_b_x"], small["lru_lambda"],
                             small["norm1_g"], small["norm2_g"]) for l in range(DEPTH)]
    saved = []
    for l in range(DEPTH):
        x, sv = _layer_fwd(l, x, layer_p[l], w_in_full[l], w_out_full[l], w_up_full[l], w_down_full[l])
        saved.append(sv)
    loss, dx, d_final_g = _loss_and_grad("final_loss", small["final_norm_g"].reshape(1, -1), x, tgt)
    big = [None] * DEPTH
    gs_layers = [None] * DEPTH
    d_logits = jnp.zeros_like(small["hgrn_lb_logits"])
    for l in reversed(range(DEPTH)):
        dx, big[l], gs_layers[l] = _layer_bwd(l, dx, layer_p[l], saved[l], w_in_full[l], w_out_full[l], w_up_full[l],
                                              w_down_full[l])
        d_logits = d_logits + gs_layers[l]["lb_logits"]
    return loss, dx, big, _small_grads_to_reference(gs_layers, d_final_g, d_logits)


def kernel(x, norm1_g, w_in, hgrn_lb_logits, hgrn_norm_g, gdn_conv_w, gdn_a_log, gdn_dt_bias, gdn_norm_g, lru_conv_w, lru_conv_b, lru_w_a, lru_b_a, lru_w_x, lru_b_x, lru_lambda, w_out, norm2_g, w_up, w_down, final_norm_g, loss_target, m_norm1_g, m_w_in, m_hgrn_lb_logits, m_hgrn_norm_g, m_gdn_conv_w, m_gdn_a_log, m_gdn_dt_bias, m_gdn_norm_g, m_lru_conv_w, m_lru_conv_b, m_lru_w_a, m_lru_b_a, m_lru_w_x, m_lru_b_x, m_lru_lambda, m_w_out, m_norm2_g, m_w_up, m_w_down, m_final_norm_g, v_norm1_g, v_w_in, v_hgrn_lb_logits, v_hgrn_norm_g, v_gdn_conv_w, v_gdn_a_log, v_gdn_dt_bias, v_gdn_norm_g, v_lru_conv_w, v_lru_conv_b, v_lru_w_a, v_lru_b_a, v_lru_w_x, v_lru_b_x, v_lru_lambda, v_w_out, v_norm2_g, v_w_up, v_w_down, v_final_norm_g):
    args = dict(locals())
    ix, iy, core = lax.axis_index("x"), lax.axis_index("y"), lax.axis_index("c")
    chip = 2 * ix + iy

    packed = _pack_big(w_in, w_out, w_up, w_down).astype(BF16)
    shares = lax.dynamic_update_slice(_gather_big(packed), packed[None], (chip, 0, 0))
    parts = [_unpack_big(shares[k]) for k in range(4)]
    w_in_full = _permute_in_cols(jnp.concatenate([pt[0] for pt in parts], axis=-1))
    w_out_full = jnp.concatenate([pt[1] for pt in parts], axis=1)
    w_up_full = jnp.concatenate([pt[2] for pt in parts], axis=-1)
    w_down_full = jnp.concatenate([pt[3] for pt in parts], axis=1)

    n_gc, n_lc = gdn_conv_w.shape[-1], lru_conv_w.shape[-1]
    conv_full = dict(
        gdn_conv_w=lax.dynamic_update_slice(jnp.zeros((DEPTH, 4, 4 * n_gc), F32), gdn_conv_w * 0.5, (0, 0, chip * n_gc)),
        lru_conv_w=lax.dynamic_update_slice(jnp.zeros((DEPTH, 4, 4 * n_lc), F32), lru_conv_w * 0.5, (0, 0, chip * n_lc)))
    conv_shapes = {k: v.shape for k, v in conv_full.items()}
    conv_full = _unflatten_small(_allreduce_small("convw", _flatten_small(conv_full)), conv_shapes)

    small = {k: args[k] for k in SMALL}
    small.update(conv_full)

    loss, grad_x, big, sg = _local_step(x[0], loss_target[0], small, w_in_full, w_out_full, w_up_full, w_down_full)

    sg["loss"] = loss[0, :1]
    shapes = {k: v.shape for k, v in sg.items()}
    sg = _unflatten_small(_allreduce_small("smallgrad", _flatten_small(sg)), shapes)
    loss_out = sg.pop("loss")[0]

    n_in = D_IN // 4
    pieces = []
    for k in range(4):
        for b in big:
            cols = jnp.concatenate(_orig_cols(b["w_in"], k * n_in, (k + 1) * n_in), axis=1)
            pieces.append(jnp.pad(cols, ((0, 0), (0, 1024 - n_in))))
        pieces += [b["w_up"][:, k * 1024:(k + 1) * 1024] for b in big]
        pieces += [b["w_down"][k * 1024:(k + 1) * 1024] for b in big]
        pieces += [b["w_out"][k * 256:(k + 1) * 256] for b in big]
    per_chip = jnp.concatenate(pieces, axis=0).reshape(4, PACK_ROWS, 1024)
    g_in, g_out, g_up, g_down = _unpack_big(_reduce_scatter_big(per_chip, chip, core))

    grads = dict(sg)
    grads["gdn_conv_w"] = lax.dynamic_slice_in_dim(sg["gdn_conv_w"], chip * n_gc, n_gc, axis=2)
    grads["lru_conv_w"] = lax.dynamic_slice_in_dim(sg["lru_conv_w"], chip * n_lc, n_lc, axis=2)
    grads.update(w_in=g_in, w_out=g_out, w_up=g_up, w_down=g_down)

    names = ['norm1_g', 'w_in', 'hgrn_lb_logits', 'hgrn_norm_g', 'gdn_conv_w', 'gdn_a_log', 'gdn_dt_bias', 'gdn_norm_g',
             'lru_conv_w', 'lru_conv_b', 'lru_w_a', 'lru_b_a', 'lru_w_x', 'lru_b_x', 'lru_lambda', 'w_out', 'norm2_g',
             'w_up', 'w_down', 'final_norm_g']
    big_names = ("w_in", "w_out", "w_up", "w_down")
    delta, new_m, new_v = {}, {}, {}
    for k in big_names:
        delta[k], new_m[k], new_v[k] = _adam("adam_" + k, args[k], grads[k], args["m_" + k], args["v_" + k])
    small_names = [k for k in names if k not in big_names]
    shapes = {k: args[k].shape for k in small_names}
    flat = [_flatten_small({k: src(k) for k in small_names})
            for src in (lambda k: args[k], lambda k: grads[k], lambda k: args["m_" + k], lambda k: args["v_" + k])]
    outs = _ew("adam_small", _adam_fn, flat, 3)
    for dst, o in zip((delta, new_m, new_v), outs):
        dst.update(_unflatten_small(o, shapes))
    return (loss_out, grad_x[None], *[grads[k] for k in names], *[delta[k] for k in names],
            *[new_m[k] for k in names], *[new_v[k] for k in names])
```

```python
import functools

import jax
import jax.numpy as jnp
from jax import lax
from jax.experimental import pallas as pl
from jax.experimental.pallas import tpu as pltpu

F32 = jnp.float32
BF16 = jnp.bfloat16
MESH = pl.DeviceIdType.MESH

DEPTH = 4
D_MODEL = 1024
A_W, B_W, C_W = 256, 512, 256
A_HD, B_HD = 64, 128
B_NH = B_W // B_HD
D_IN = 3592
D_IN_PAD = 3840
COL_BZ, COL_AQ, COL_AF, COL_AI, COL_AG, COL_CX, COL_CY = 1536, 2048, 2304, 2560, 2816, 3072, 3328
W_OUT_CHIP_ORDER = (1, 2, 0, 3)
GATE0 = 3584
D_FF = 4096
CHUNK = 64
SUB = 16
EPS = 1e-6
TINY = 1e-30
RG_C = 8.0
LR, B1, B2, AEPS, WD, STEP = 0.001, 0.9, 0.999, 1e-08, 0.01, 10
VMEM_LIMIT = 56 * 1024 * 1024
PACK_ROWS = 13 * 1024

NN = ((1,), (0,))
NT = ((1,), (1,))
TN = ((0,), (0,))


DIMS = {"nn": NN, "nt": NT, "tn": TN}


def _split(a, n):
    parts, r = [], a
    for i in range(n):
        p = r.astype(BF16)
        parts.append(p)
        if i + 1 < n:
            r = r - p.astype(F32)
    return parts


def _split_dot(a, b, mode, na, nb):
    (ca,), (cb,) = DIMS[mode]
    dims = ((DIMS[mode]), ((), ())) if a.ndim == 2 else (((ca + 1,), (cb + 1,)), ((0,), (0,)))
    pa, pb = _split(a, na), _split(b, nb)
    acc = None
    for i in range(na):
        for j in range(nb):
            if i + j < max(na, nb):
                t = lax.dot_general(pa[i], pb[j], dims, preferred_element_type=F32)
                acc = t if acc is None else acc + t
    return acc


@functools.partial(jax.custom_vjp, nondiff_argnums=(2, 3, 4, 5))
def _mdot(a, b, mode, na, nb, need):
    return _split_dot(a, b, mode, na, nb)


def _mdot_fwd(a, b, mode, na, nb, need):
    return _split_dot(a, b, mode, na, nb), (a, b)


def _mdot_bwd(mode, na, nb, need, res, ct):
    a, b = res
    n = max(na, nb)
    da, db = jnp.zeros_like(a), jnp.zeros_like(b)
    if mode == "nn":
        if need[0]:
            da = _split_dot(ct, b, "nt", n, nb)
        if need[1]:
            db = _split_dot(a, ct, "tn", na, n)
    elif mode == "nt":
        if need[0]:
            da = _split_dot(ct, b, "nn", n, nb)
        if need[1]:
            db = _split_dot(ct, a, "tn", n, na)
    else:
        if need[0]:
            da = _split_dot(b, ct, "nt", nb, n)
        if need[1]:
            db = _split_dot(a, ct, "nn", na, n)
    return da, db


_mdot.defvjp(_mdot_fwd, _mdot_bwd)
BOTH = (True, True)


def _bdot(a, b, mode="nn"):
    return _mdot(a, b, mode, 1, 1, BOTH)


def _group_sum(x, ones):
    return _mdot(x, ones, "nn", 2, 1, (True, False))


def _running_sum(x):
    shape = x.shape[:-2] + (CHUNK, CHUNK)
    tri = (_iota(shape, len(shape) - 1) <= _iota(shape, len(shape) - 2)).astype(F32)
    return _mdot(tri, x, "nn", 1, 3, (False, True))


def _iota(shape, d):
    return lax.broadcasted_iota(jnp.int32, shape, d)


def _block_ones(width, group):
    return (_iota((width, width), 0) // group == _iota((width, width), 1) // group).astype(F32)


def _params(sem):
    return pltpu.CompilerParams(dimension_semantics=sem, vmem_limit_bytes=VMEM_LIMIT)


def _tile(dim, prefs):
    for p in prefs:
        if dim % p == 0:
            return p
    return dim


def _whole(shape):
    return pl.BlockSpec(shape, lambda i: (0,) * len(shape))


def _rowspec(t, w, col0=0):
    cb = col0 // w
    return pl.BlockSpec((t, w), lambda i: (i, cb))


def _cols(arr, col0, width):
    assert col0 % width == 0
    return (arr, col0, width)


def _row_operands(rows, t):
    arrs, specs = [], []
    for r in rows:
        arr, col0, w = r if isinstance(r, tuple) else (r, 0, r.shape[1])
        arrs.append(arr)
        specs.append(_rowspec(t, w, col0))
    return arrs, specs


def _rowwise_fwd(name, fn, params, rows, out_widths, out_dtypes, t):
    rows, row_specs = _row_operands(rows, t)
    s = rows[0].shape[0]
    n_p, n_r = len(params), len(rows)

    def body(*refs):
        p = [r[...] for r in refs[:n_p]]
        xs = [r[...].astype(F32) for r in refs[n_p:n_p + n_r]]
        outs = fn(pl.program_id(0) * t, *p, *xs)
        for o_ref, o in zip(refs[n_p + n_r:], outs):
            o_ref[...] = o.astype(o_ref.dtype)

    return pl.pallas_call(
        body, grid=(s // t,),
        in_specs=[_whole(p.shape) for p in params] + row_specs,
        out_specs=[_rowspec(t, w) for w in out_widths],
        out_shape=[jax.ShapeDtypeStruct((s, w), d) for w, d in zip(out_widths, out_dtypes)],
        name=name, compiler_params=_params(("arbitrary",)))(*params, *rows)


def _rowwise_bwd(name, fn, params, rows, douts, t, need=None, res=None, dtypes=None, copy16=False):
    rows, row_specs = _row_operands(rows, t)
    douts, dout_specs = _row_operands(douts, t)
    s = rows[0].shape[0]
    n_p, n_r, n_o = len(params), len(rows), len(douts)
    need = [True] * n_r if need is None else need
    n_res = 0 if res is None else 1

    def body(*refs):
        p = [r[...] for r in refs[:n_p]]
        xs = [r[...].astype(F32) for r in refs[n_p:n_p + n_r]]
        dys = [r[...].astype(F32) for r in refs[n_p + n_r:n_p + n_r + n_o]]
        k = n_p + n_r + n_o
        res_ref = refs[k] if n_res else None
        dp_refs = refs[k + n_res:k + n_res + n_p]
        dx_refs = refs[k + n_res + n_p:]
        row0 = pl.program_id(0) * t
        _, vjp = jax.vjp(lambda *a: tuple(fn(row0, *a)), *p, *xs)
        g = vjp(tuple(dys))

        @pl.when(pl.program_id(0) == 0)
        def _():
            for r in dp_refs:
                r[...] = jnp.zeros_like(r)

        for r, gp in zip(dp_refs, g[:n_p]):
            r[...] += gp
        gx = [gi for gi, nd in zip(g[n_p:], need) if nd]
        if n_res:
            gx[0] = gx[0] + res_ref[...]
        if copy16:
            gx.append(gx[0])
        for r, gi in zip(dx_refs, gx):
            r[...] = gi.astype(r.dtype)

    widths = [sp.block_shape[1] for sp, nd in zip(row_specs, need) if nd]
    dtypes = [F32] * len(widths) if dtypes is None else list(dtypes)
    if copy16:
        widths, dtypes = widths + widths[:1], dtypes + [BF16]
    outs = pl.pallas_call(
        body, grid=(s // t,),
        in_specs=([_whole(p.shape) for p in params] + row_specs + dout_specs
                  + ([_rowspec(t, res.shape[1])] if n_res else [])),
        out_specs=[_whole(p.shape) for p in params] + [_rowspec(t, w) for w in widths],
        out_shape=([jax.ShapeDtypeStruct(p.shape, F32) for p in params]
                   + [jax.ShapeDtypeStruct((s, w), d) for w, d in zip(widths, dtypes)]),
        name=name, compiler_params=_params(("arbitrary",)))(*params, *rows, *douts, *([res] if n_res else []))
    return list(outs[:n_p]), list(outs[n_p:])


def _rmsnorm_fn(row0, g, x):
    y = x * lax.rsqrt(jnp.mean(x * x, axis=-1, keepdims=True) + EPS)
    return (y * g,)


def _hgrn_pre_fn(layer, row0, logits, aq, af):
    m = jnp.max(logits, axis=0, keepdims=True)
    e = jnp.exp(logits - m)
    p = e / jnp.sum(e, axis=0, keepdims=True)
    if layer == 0:
        lb = jnp.zeros((1, A_W), F32)
    else:
        acc = p[1:2]
        for j in range(2, layer + 1):
            acc = acc + p[j:j + 1]
        lb = jnp.minimum(jnp.maximum(acc, 0.0), 1.0 - EPS)
    sig = jax.nn.sigmoid(af)
    f = lb + (1.0 - lb) * sig
    log_f = jnp.log(jnp.maximum(f, TINY))
    k = (1.0 - lb) * jax.nn.sigmoid(-af)
    return jax.nn.silu(aq), k, log_f


def _hgrn_post_fn(row0, g, o, ag):
    ms = _group_sum(o * o, _block_ones(A_W, A_HD)) * (1.0 / A_HD)
    return (o * lax.rsqrt(ms + EPS) * g * jax.nn.silu(ag),)


def _gdn_pre_fn(row0, a_log, dt_bias, qkvc, gate):
    act = jax.nn.silu(qkvc)
    q, k, v = act[:, :B_W], act[:, B_W:2 * B_W], act[:, 2 * B_W:]
    bo = _block_ones(B_W, B_HD)
    q = q * lax.rsqrt(_group_sum(q * q, bo) + EPS) * (B_HD ** -0.5)
    k = k * lax.rsqrt(_group_sum(k * k, bo) + EPS)
    beta = jax.nn.sigmoid(gate)
    log_alpha = -jnp.exp(a_log) * jax.nn.softplus(gate + dt_bias)
    t = gate.shape[0]
    spread = lambda z, lane0: jnp.concatenate(
        [jnp.broadcast_to(z[:, lane0 + h:lane0 + h + 1], (t, B_HD)) for h in range(B_NH)], axis=1)
    return q, k, v, spread(beta, 0), spread(log_alpha, B_NH)


def _gdn_post_fn(row0, g, o, bz):
    ms = _group_sum(o * o, _block_ones(B_W, B_HD)) * (1.0 / B_HD)
    return (o * lax.rsqrt(ms + EPS) * g * jax.nn.silu(bz),)


def _lru_pre_fn(row0, w_a, b_a, w_x, b_x, lam, xc):
    r = jax.nn.sigmoid(_mdot(xc, w_a, "nn", 2, 2, BOTH) + b_a)
    i = jax.nn.sigmoid(_mdot(xc, w_x, "nn", 2, 2, BOTH) + b_x)
    log_a = -RG_C * r * jax.nn.softplus(-lam)
    a = jnp.exp(log_a)
    one_minus = -jnp.tanh(log_a) * (a * a + 1.0)
    mult = jnp.sqrt(jnp.maximum(one_minus, EPS))
    pos = row0 + _iota(xc.shape, 0)
    mult = jnp.where(pos == 0, 1.0, mult)
    return a, mult * i * xc


def _lru_post_fn(row0, h, cy):
    return (jax.nn.gelu(cy, approximate=True) * h,)


def _lru_da_fn(row0, g, h_prev):
    return (g * h_prev,)


def _relu2(u):
    r = jnp.maximum(u, 0.0)
    return r * r


def _drelu2(acc, u):
    return acc * (2.0 * jnp.maximum(u, 0.0))


MM_VMEM_BUDGET = 44 * 1024 * 1024


def _mm_tiles(mode, m, n, k, a_bytes, b_bytes, n_extra):
    best = None
    for tk in {k, 4096, 2048, 1024, 512, 256}:
        for tn in {n, 1280, 1024, 768, 512, 256, 128}:
            for tm in {m, 1024, 512, 256, 128}:
                if m % tm or n % tn or k % tk or tm > 1024:
                    continue
                tiles = tm * tk * a_bytes + tk * tn * b_bytes + (1 + n_extra) * tm * tn * 4
                if 2 * tiles + (tm * tn * 4 if tk < k else 0) > MM_VMEM_BUDGET:
                    continue
                a_reads = 1 if tk == k else n // tn
                b_reads = 1 if (tk == k and tn == n) else m // tm
                cost = m * k * a_bytes * a_reads + k * n * b_bytes * b_reads + m * n * 4 * (k // tk - 1)
                if best is None or (cost, -tm) < best[0]:
                    best = ((cost, -tm), (tm, tn, tk))
    if best is None:
        raise ValueError("no matmul tiling fits VMEM")
    return best[1]


def _matmul(name, a, b, mode, a_fn=None, res=None, epi=None, epi_in=None, out_dtype=F32, also=None):
    if mode == "nn":
        (m, k), n = a.shape, b.shape[1]
    elif mode == "nt":
        (m, k), n = a.shape, b.shape[0]
    else:
        (k, m), n = a.shape, b.shape[1]
    n_extra = (res is not None) + (epi_in is not None) + (also is not None)
    tm, tn, tk = _mm_tiles(mode, m, n, k, a.dtype.itemsize, b.dtype.itemsize, n_extra)
    nk = k // tk
    a_spec = {"nn": pl.BlockSpec((tm, tk), lambda i, j, kk: (i, kk)),
              "nt": pl.BlockSpec((tm, tk), lambda i, j, kk: (i, kk)),
              "tn": pl.BlockSpec((tk, tm), lambda i, j, kk: (kk, i))}[mode]
    b_spec = {"nn": pl.BlockSpec((tk, tn), lambda i, j, kk: (kk, j)),
              "nt": pl.BlockSpec((tn, tk), lambda i, j, kk: (j, kk)),
              "tn": pl.BlockSpec((tk, tn), lambda i, j, kk: (kk, j))}[mode]
    o_spec = pl.BlockSpec((tm, tn), lambda i, j, kk: (i, j))
    dims = {"nn": NN, "nt": NT, "tn": TN}[mode]
    extra = [x for x in (res, epi_in) if x is not None]

    def body(*refs):
        a_ref, b_ref = refs[0], refs[1]
        rest = list(refs[2:])
        res_ref = rest.pop(0) if res is not None else None
        epi_ref = rest.pop(0) if epi_in is not None else None
        o_ref = rest.pop(0)
        also_ref = rest.pop(0) if also is not None else None
        acc = rest[0] if nk > 1 else None
        kk = pl.program_id(2)
        at = a_ref[...]
        if a_fn is not None:
            at = a_fn(at.astype(F32))
        part = lax.dot_general(at.astype(BF16), b_ref[...].astype(BF16), (dims, ((), ())),
                               preferred_element_type=F32)

        def finish(out):
            if res_ref is not None:
                out = out + res_ref[...]
            if epi is not None:
                out = epi(out, epi_ref[...])
            o_ref[...] = out.astype(o_ref.dtype)
            if also is not None:
                also_ref[...] = also(out).astype(also_ref.dtype)

        if nk == 1:
            finish(part)
        else:
            @pl.when(kk == 0)
            def _():
                acc[...] = part

            @pl.when(jnp.logical_and(kk > 0, kk < nk - 1))
            def _():
                acc[...] += part

            @pl.when(kk == nk - 1)
            def _():
                finish(acc[...] + part)

    out_shape = [jax.ShapeDtypeStruct((m, n), out_dtype)] + ([jax.ShapeDtypeStruct((m, n), BF16)] if also else [])
    outs = pl.pallas_call(
        body, grid=(m // tm, n // tn, nk),
        in_specs=[a_spec, b_spec] + [o_spec] * len(extra), out_specs=[o_spec] * len(out_shape), out_shape=out_shape,
        scratch_shapes=[pltpu.VMEM((tm, tn), F32)] if nk > 1 else [],
        name=name, compiler_params=_params(("arbitrary", "arbitrary", "arbitrary")))(a, b, *extra)
    return outs if also else outs[0]


HALO = 8


def _conv_fwd(name, x, col0, width, w, bias, t=512):
    s = x.shape[0]
    cb = col0 // width
    hb = t // HALO

    def body(x_ref, halo_ref, w_ref, b_ref, y_ref):
        i = pl.program_id(0)
        halo = jnp.where(i == 0, 0.0, halo_ref[...])
        xp = jnp.concatenate([halo, x_ref[...]], axis=0)
        wv = w_ref[...]
        y = b_ref[...] + wv[0:1] * xp[HALO - 3:HALO - 3 + t]
        for j in range(1, 4):
            y = y + wv[j:j + 1] * xp[HALO - 3 + j:HALO - 3 + j + t]
        y_ref[...] = y

    return pl.pallas_call(
        body, grid=(s // t,),
        in_specs=[pl.BlockSpec((t, width), lambda i: (i, cb)),
                  pl.BlockSpec((HALO, width), lambda i: (jnp.maximum(i * hb - 1, 0), cb)),
                  _whole(w.shape), _whole(bias.shape)],
        out_specs=_rowspec(t, width), out_shape=jax.ShapeDtypeStruct((s, width), F32),
        name=name, compiler_params=_params(("arbitrary",)))(x, x, w, bias)


def _conv_bwd(name, dy, x, col0, width, w, t=512):
    s = x.shape[0]
    cb = col0 // width
    hb = t // HALO
    nblk = s // t

    def body(dy_ref, dyn_ref, x_ref, halo_ref, w_ref, dx_ref, dw_ref, db_ref):
        i = pl.program_id(0)
        dyv = dy_ref[...]
        nxt = jnp.where(i == nblk - 1, 0.0, dyn_ref[...])
        dyp = jnp.concatenate([dyv, nxt], axis=0)
        wv = w_ref[...]
        dx = wv[3:4] * dyv
        for j in range(3):
            dx = dx + wv[j:j + 1] * dyp[3 - j:3 - j + t]
        dx_ref[...] = dx.astype(dx_ref.dtype)
        halo = jnp.where(i == 0, 0.0, halo_ref[...])
        xp = jnp.concatenate([halo, x_ref[...]], axis=0)
        dw = jnp.concatenate(
            [jnp.sum(dyv * xp[HALO - 3 + j:HALO - 3 + j + t], axis=0, keepdims=True) for j in range(4)], axis=0)

        @pl.when(i == 0)
        def _():
            dw_ref[...] = jnp.zeros_like(dw_ref)
            db_ref[...] = jnp.zeros_like(db_ref)

        dw_ref[...] += dw
        db_ref[...] += jnp.sum(dyv, axis=0, keepdims=True)

    return pl.pallas_call(
        body, grid=(nblk,),
        in_specs=[_rowspec(t, width),
                  pl.BlockSpec((HALO, width), lambda i: (jnp.minimum((i + 1) * hb, s // HALO - 1), 0)),
                  pl.BlockSpec((t, width), lambda i: (i, cb)),
                  pl.BlockSpec((HALO, width), lambda i: (jnp.maximum(i * hb - 1, 0), cb)),
                  _whole(w.shape)],
        out_specs=[_rowspec(t, width), _whole((4, width)), _whole((1, width))],
        out_shape=[jax.ShapeDtypeStruct((s, width), BF16), jax.ShapeDtypeStruct((4, width), F32),
                   jax.ShapeDtypeStruct((1, width), F32)],
        name=name, compiler_params=_params(("arbitrary",)))(dy, dy, x, x, w)


def _hgrn_chunk(st, q, k, v, lf):
    cum = _running_sum(lf)
    tot = jnp.sum(lf, axis=0, keepdims=True)
    bm = _iota((A_W, A_W), 0) // A_HD == _iota((A_W, A_W), 1) // A_HD
    bo = bm.astype(F32)
    o_inter = _bdot(q * jnp.exp(cum), st, "nt")
    kd = k * jnp.exp(tot - cum)
    st_new = st * jnp.exp(tot) + jnp.where(bm, _bdot(v, kd, "tn"), 0.0)
    lane_h = _iota((SUB, A_W), 1) // A_HD
    n_h = A_W // A_HD
    t3 = _iota((SUB, SUB, A_W), 0)
    s3 = _iota((SUB, SUB, A_W), 1)
    outs = []
    for i in range(CHUNK // SUB):
        sl = slice(i * SUB, (i + 1) * SUB)
        qi, ki, vi, ci = q[sl], k[sl], v[sl], cum[sl]
        e = jnp.exp(jnp.minimum(ci[:, None, :] - ci[None, :, :], 0.0))
        d3 = jnp.where(s3 <= t3, qi[:, None, :] * ki[None, :, :] * e, 0.0)
        ds = _mdot(d3.reshape(SUB * SUB, A_W), bo, "nn", 1, 1, (True, False)).reshape(SUB, SUB, A_W)
        oi = jnp.sum(ds * vi[None, :, :], axis=1)
        if i > 0:
            n = i * SUB
            cb = cum[n - 1:n]
            qt = qi * jnp.exp(ci - cb)
            kt = k[:n] * jnp.exp(jnp.minimum(cb - cum[:n], 0.0))
            qs = jnp.concatenate([jnp.where(lane_h == h, qt, 0.0) for h in range(n_h)], axis=0)
            p = _bdot(_bdot(qs, kt, "nt"), v[:n])
            for h in range(n_h):
                oi = oi + jnp.where(lane_h == h, p[h * SUB:(h + 1) * SUB], 0.0)
        outs.append(oi)
    return st_new, o_inter + jnp.concatenate(outs, axis=0)


HGRN_CHUNKS = 2


def _hgrn_fwd(name, q, k, v, lf):
    s = q.shape[0]
    n = s // CHUNK
    g = HGRN_CHUNKS

    def body(q_ref, k_ref, v_ref, lf_ref, o_ref, sts_ref, st_ref):
        @pl.when(pl.program_id(0) == 0)
        def _():
            st_ref[...] = jnp.zeros_like(st_ref)

        st = st_ref[...]
        for c in range(g):
            rows = pl.ds(c * CHUNK, CHUNK)
            sts_ref[c] = st
            st, o = _hgrn_chunk(st, q_ref[rows], k_ref[rows], v_ref[rows], lf_ref[rows])
            o_ref[rows] = o
        st_ref[...] = st

    cs = _rowspec(g * CHUNK, A_W)
    return pl.pallas_call(
        body, grid=(n // g,), in_specs=[cs] * 4,
        out_specs=[cs, pl.BlockSpec((g, A_W, A_W), lambda i: (i, 0, 0))],
        out_shape=[jax.ShapeDtypeStruct((s, A_W), F32), jax.ShapeDtypeStruct((n, A_W, A_W), F32)],
        scratch_shapes=[pltpu.VMEM((A_W, A_W), F32)],
        name=name, compiler_params=_params(("arbitrary",)))(q, k, v, lf)


def _hgrn_bwd(name, q, k, v, lf, sts, do):
    s = q.shape[0]
    n = s // CHUNK
    g = HGRN_CHUNKS
    nsteps = n // g

    def body(q_ref, k_ref, v_ref, lf_ref, sts_ref, do_ref, dq_ref, dk_ref, dv_ref, dlf_ref, dst_ref):
        @pl.when(pl.program_id(0) == 0)
        def _():
            dst_ref[...] = jnp.zeros_like(dst_ref)

        dst = dst_ref[...]
        for c in reversed(range(g)):
            rows = pl.ds(c * CHUNK, CHUNK)
            _, vjp = jax.vjp(_hgrn_chunk, sts_ref[c], q_ref[rows], k_ref[rows], v_ref[rows], lf_ref[rows])
            ct = vjp((dst, do_ref[rows]))
            dst = ct[0]
            for r, gi in zip((dq_ref, dk_ref, dv_ref, dlf_ref), ct[1:]):
                r[rows] = gi
        dst_ref[...] = dst

    rs = pl.BlockSpec((g * CHUNK, A_W), lambda i: (nsteps - 1 - i, 0))
    return pl.pallas_call(
        body, grid=(nsteps,),
        in_specs=[rs] * 4 + [pl.BlockSpec((g, A_W, A_W), lambda i: (nsteps - 1 - i, 0, 0)), rs],
        out_specs=[rs] * 4, out_shape=[jax.ShapeDtypeStruct((s, A_W), F32)] * 4,
        scratch_shapes=[pltpu.VMEM((A_W, A_W), F32)],
        name=name, compiler_params=_params(("arbitrary",)))(q, k, v, lf, sts, do)


PREP_CHUNKS = 4
REC_CHUNKS = 4
DEC_ROWS = 8


def _to_heads(x, rows=CHUNK, width=B_HD):
    n = x.shape[0] // rows
    x3 = x.reshape(n, rows, x.shape[1])
    return jnp.concatenate([x3[:, :, h * B_HD:h * B_HD + width] for h in range(B_NH)], axis=0)


def _from_heads(y):
    n = y.shape[0] // B_NH
    x3 = jnp.concatenate([y[h * n:(h + 1) * n] for h in range(B_NH)], axis=2)
    return x3.reshape(n * y.shape[1], B_NH * B_HD)


@jax.custom_vjp
def _unit_lower_inverse(a_mat):
    nil = -a_mat
    eye = (_iota(a_mat.shape, 1) == _iota(a_mat.shape, 2)).astype(F32)
    inv = eye + nil
    sq = 2
    while sq < CHUNK:
        nil = _split_dot(nil, nil, "nn", 2, 2)
        inv = inv + _split_dot(inv, nil, "nn", 2, 2)
        sq *= 2
    return inv


def _unit_lower_inverse_fwd(a_mat):
    inv = _unit_lower_inverse(a_mat)
    return inv, inv


def _unit_lower_inverse_bwd(inv, ct):
    return (-_split_dot(_split_dot(inv, ct, "tn", 2, 2), inv, "nt", 2, 2),)


_unit_lower_inverse.defvjp(_unit_lower_inverse_fwd, _unit_lower_inverse_bwd)


def _gdn_prep(q, k, v, beta, la):
    q, k, v, beta, la = [_to_heads(z) for z in (q, k, v, beta, la)]
    nb = q.shape[0]
    r = _iota((nb, CHUNK, CHUNK), 1)
    c = _iota((nb, CHUNK, CHUNK), 2)
    causal = c <= r
    g = _running_sum(la)
    tot = jnp.sum(la, axis=1, keepdims=True)
    gcol = g[:, :, :CHUNK]
    decay = jnp.where(causal, jnp.exp(jnp.where(causal, gcol - jnp.swapaxes(gcol, 1, 2), 0.0)), 0.0)
    kb = k * beta
    a_mat = jnp.where(c < r, _bdot(kb, k, "nt") * decay, 0.0)
    inv = _unit_lower_inverse(a_mat)
    eg = jnp.exp(g)
    u = _bdot(inv, v * beta)
    w = _bdot(inv, kb * eg)
    qk = jnp.where(causal, _bdot(q, k, "nt") * decay, 0.0)
    qk = jnp.concatenate([qk, jnp.zeros((nb, CHUNK, B_HD - CHUNK), F32)], axis=2)
    dec = jnp.broadcast_to(jnp.exp(tot), (nb, DEC_ROWS, B_HD))
    return tuple(_from_heads(z) for z in (u, w, qk, q * eg, k * jnp.exp(tot - g), dec))


def _gdn_rec(st, u, w, qk, qd, kd, dec):
    u, w, qd, kd = [_to_heads(z) for z in (u, w, qd, kd)]
    qk = _to_heads(qk, width=CHUNK)
    v_new = u - _bdot(w, st)
    o = _bdot(qd, st) + _bdot(qk, v_new)
    st_new = st * _to_heads(dec, rows=DEC_ROWS)[:, :1, :1] + _bdot(kd, v_new, "tn")
    return st_new, _from_heads(o)


def _gdn_prep_fwd(name, q, k, v, beta, la):
    s = q.shape[0]
    t = PREP_CHUNKS * CHUNK
    td = PREP_CHUNKS * DEC_ROWS

    def body(*refs):
        outs = _gdn_prep(*[r[...] for r in refs[:5]])
        for o_ref, o in zip(refs[5:], outs):
            o_ref[...] = o

    return pl.pallas_call(
        body, grid=(s // t,), in_specs=[_rowspec(t, B_W)] * 5,
        out_specs=[_rowspec(t, B_W)] * 5 + [_rowspec(td, B_W)],
        out_shape=[jax.ShapeDtypeStruct((s, B_W), F32)] * 5 + [jax.ShapeDtypeStruct((s // CHUNK * DEC_ROWS, B_W), F32)],
        name=name, compiler_params=_params(("arbitrary",)))(q, k, v, beta, la)


def _gdn_prep_bwd(name, q, k, v, beta, la, cts):
    s = q.shape[0]
    t = PREP_CHUNKS * CHUNK
    td = PREP_CHUNKS * DEC_ROWS

    def body(*refs):
        _, vjp = jax.vjp(_gdn_prep, *[r[...] for r in refs[:5]])
        g = vjp(tuple(r[...] for r in refs[5:11]))
        for o_ref, o in zip(refs[11:], g):
            o_ref[...] = o

    return pl.pallas_call(
        body, grid=(s // t,), in_specs=[_rowspec(t, B_W)] * 10 + [_rowspec(td, B_W)],
        out_specs=[_rowspec(t, B_W)] * 5, out_shape=[jax.ShapeDtypeStruct((s, B_W), F32)] * 5,
        name=name, compiler_params=_params(("arbitrary",)))(q, k, v, beta, la, *cts)


def _gdn_rec_fwd(name, u, w, qk, qd, kd, dec):
    s = u.shape[0]
    n = s // CHUNK
    g = REC_CHUNKS

    def body(u_ref, w_ref, qk_ref, qd_ref, kd_ref, dec_ref, o_ref, sts_ref, st_ref):
        @pl.when(pl.program_id(0) == 0)
        def _():
            st_ref[...] = jnp.zeros_like(st_ref)

        st = st_ref[...]
        for c in range(g):
            rows, drows = pl.ds(c * CHUNK, CHUNK), pl.ds(c * DEC_ROWS, DEC_ROWS)
            sts_ref[c] = st
            st, o = _gdn_rec(st, u_ref[rows], w_ref[rows], qk_ref[rows], qd_ref[rows], kd_ref[rows], dec_ref[drows])
            o_ref[rows] = o
        st_ref[...] = st

    cs = _rowspec(g * CHUNK, B_W)
    return pl.pallas_call(
        body, grid=(n // g,), in_specs=[cs] * 5 + [_rowspec(g * DEC_ROWS, B_W)],
        out_specs=[cs, pl.BlockSpec((g, B_NH, B_HD, B_HD), lambda i: (i, 0, 0, 0))],
        out_shape=[jax.ShapeDtypeStruct((s, B_W), F32), jax.ShapeDtypeStruct((n, B_NH, B_HD, B_HD), F32)],
        scratch_shapes=[pltpu.VMEM((B_NH, B_HD, B_HD), F32)],
        name=name, compiler_params=_params(("arbitrary",)))(u, w, qk, qd, kd, dec)


def _gdn_rec_bwd(name, u, w, qk, qd, kd, dec, sts, do):
    s = u.shape[0]
    n = s // CHUNK
    g = REC_CHUNKS
    nsteps = n // g

    def body(u_ref, w_ref, qk_ref, qd_ref, kd_ref, dec_ref, sts_ref, do_ref,
             du_ref, dw_ref, dqk_ref, dqd_ref, dkd_ref, ddec_ref, dst_ref):
        @pl.when(pl.program_id(0) == 0)
        def _():
            dst_ref[...] = jnp.zeros_like(dst_ref)

        dst = dst_ref[...]
        for c in reversed(range(g)):
            rows, drows = pl.ds(c * CHUNK, CHUNK), pl.ds(c * DEC_ROWS, DEC_ROWS)
            _, vjp = jax.vjp(_gdn_rec, sts_ref[c], u_ref[rows], w_ref[rows], qk_ref[rows], qd_ref[rows], kd_ref[rows],
                             dec_ref[drows])
            ct = vjp((dst, do_ref[rows]))
            dst = ct[0]
            for r, gi in zip((du_ref, dw_ref, dqk_ref, dqd_ref, dkd_ref), ct[1:6]):
                r[rows] = gi
            ddec_ref[drows] = ct[6]
        dst_ref[...] = dst

    rs = pl.BlockSpec((g * CHUNK, B_W), lambda i: (nsteps - 1 - i, 0))
    ds = pl.BlockSpec((g * DEC_ROWS, B_W), lambda i: (nsteps - 1 - i, 0))
    return pl.pallas_call(
        body, grid=(nsteps,),
        in_specs=[rs] * 5 + [ds, pl.BlockSpec((g, B_NH, B_HD, B_HD), lambda i: (nsteps - 1 - i, 0, 0, 0)), rs],
        out_specs=[rs] * 5 + [ds],
        out_shape=[jax.ShapeDtypeStruct((s, B_W), F32)] * 5 + [jax.ShapeDtypeStruct((n * DEC_ROWS, B_W), F32)],
        scratch_shapes=[pltpu.VMEM((B_NH, B_HD, B_HD), F32)],
        name=name, compiler_params=_params(("arbitrary",)))(u, w, qk, qd, kd, dec, sts, do)


SCAN_T = 512


def _block_scan(a, b, reverse):
    t = a.shape[0]
    rows = _iota(a.shape, 0)
    sft = 1
    while sft < t:
        if reverse:
            a_s, b_s = pltpu.roll(a, t - sft, 0), pltpu.roll(b, t - sft, 0)
            edge = rows >= t - sft
        else:
            a_s, b_s = pltpu.roll(a, sft, 0), pltpu.roll(b, sft, 0)
            edge = rows < sft
        a_s = jnp.where(edge, 1.0, a_s)
        b_s = jnp.where(edge, 0.0, b_s)
        b = a * b_s + b
        a = a * a_s
        sft *= 2
    return a, b


def _lru_scan_fwd(name, a, b):
    s, w = a.shape
    t = SCAN_T

    def body(a_ref, b_ref, h_ref, hp_ref, carry):
        @pl.when(pl.program_id(0) == 0)
        def _():
            carry[...] = jnp.zeros_like(carry)

        h_in = carry[0:1]
        ca, cb = _block_scan(a_ref[...], b_ref[...], False)
        h = ca * h_in + cb
        h_ref[...] = h
        hp_ref[...] = jnp.where(_iota(h.shape, 0) == 0, h_in, pltpu.roll(h, 1, 0))
        carry[...] = jnp.broadcast_to(h[t - 1:t], carry.shape)

    return pl.pallas_call(
        body, grid=(s // t,), in_specs=[_rowspec(t, w)] * 2, out_specs=[_rowspec(t, w)] * 2,
        out_shape=[jax.ShapeDtypeStruct((s, w), F32)] * 2, scratch_shapes=[pltpu.VMEM((8, w), F32)],
        name=name, compiler_params=_params(("arbitrary",)))(a, b)


def _lru_scan_bwd(name, a, dh):
    s, w = a.shape
    t = SCAN_T
    n = s // t

    def body(a_ref, dh_ref, g_ref, a_next, g_next):
        @pl.when(pl.program_id(0) == 0)
        def _():
            a_next[...] = jnp.zeros_like(a_next)
            g_next[...] = jnp.zeros_like(g_next)

        av = a_ref[...]
        a_up = jnp.where(_iota(av.shape, 0) == t - 1, a_next[0:1], pltpu.roll(av, t - 1, 0))
        ca, cb = _block_scan(a_up, dh_ref[...], True)
        g = ca * g_next[0:1] + cb
        g_ref[...] = g
        a_next[...] = jnp.broadcast_to(av[0:1], a_next.shape)
        g_next[...] = jnp.broadcast_to(g[0:1], g_next.shape)

    rs = pl.BlockSpec((t, w), lambda i: (n - 1 - i, 0))
    return pl.pallas_call(
        body, grid=(n,), in_specs=[rs] * 2, out_specs=rs, out_shape=jax.ShapeDtypeStruct((s, w), F32),
        scratch_shapes=[pltpu.VMEM((8, w), F32), pltpu.VMEM((8, w), F32)],
        name=name, compiler_params=_params(("arbitrary",)))(a, dh)


def _loss_fn(g, x, tgt):
    y = x * lax.rsqrt(jnp.mean(x * x, axis=-1, keepdims=True) + EPS) * g
    err = y - tgt
    return 0.5 * jnp.sum(jnp.mean(err * err, axis=-1, keepdims=True), axis=0, keepdims=True)


def _loss_and_grad(name, g, x, tgt, t=256):
    s, d = x.shape

    def body(g_ref, x_ref, t_ref, loss_ref, dx_ref, dg_ref, dx16_ref):
        val, vjp = jax.vjp(_loss_fn, g_ref[...], x_ref[...], t_ref[...])
        dg, dx, _ = vjp(jnp.ones((1, 1), F32))

        @pl.when(pl.program_id(0) == 0)
        def _():
            loss_ref[...] = jnp.zeros_like(loss_ref)
            dg_ref[...] = jnp.zeros_like(dg_ref)

        loss_ref[...] += jnp.broadcast_to(val, loss_ref.shape)
        dg_ref[...] += dg
        dx_ref[...] = dx
        dx16_ref[...] = dx.astype(BF16)

    return pl.pallas_call(
        body, grid=(s // t,), in_specs=[_whole(g.shape), _rowspec(t, d), _rowspec(t, d)],
        out_specs=[_whole((1, 128)), _rowspec(t, d), _whole(g.shape), _rowspec(t, d)],
        out_shape=[jax.ShapeDtypeStruct((1, 128), F32), jax.ShapeDtypeStruct((s, d), F32),
                   jax.ShapeDtypeStruct(g.shape, F32), jax.ShapeDtypeStruct((s, d), BF16)],
        name=name, compiler_params=_params(("arbitrary",)))(g, x, tgt)


def _ew(name, fn, ins, n_out, t=None):
    r, c = ins[0].shape
    t = _tile(r, (512, 256, 128, 64, 32, 16, 8)) if t is None else t

    def body(*refs):
        outs = fn(*[x[...] for x in refs[:len(ins)]])
        for o_ref, o in zip(refs[len(ins):], outs):
            o_ref[...] = o

    return pl.pallas_call(
        body, grid=(r // t,), in_specs=[_rowspec(t, c)] * len(ins), out_specs=[_rowspec(t, c)] * n_out,
        out_shape=[jax.ShapeDtypeStruct((r, c), F32)] * n_out,
        name=name, compiler_params=_params(("arbitrary",)))(*ins)


def _adam_fn(w, g, m, v):
    m = B1 * m + (1.0 - B1) * g
    v = B2 * v + (1.0 - B2) * (g * g)
    m_hat = m / (1.0 - B1 ** STEP)
    v_hat = v / (1.0 - B2 ** STEP)
    delta = -LR * (m_hat / (jnp.sqrt(v_hat) + AEPS) + WD * w)
    return delta, m, v


def _add2_fn(a, b):
    return (a + b,)


def _adam(name, w, g, m, v):
    shp = w.shape
    two = lambda z: z.reshape(-1, shp[-1])
    outs = _ew(name, _adam_fn, [two(w), two(g), two(m), two(v)], 3)
    return [o.reshape(shp) for o in outs]


def _swap(name, bufs, flips):
    n = len(bufs)
    hbm = pl.BlockSpec(memory_space=pltpu.HBM)

    def body(*refs):
        srcs, dsts = refs[:n], refs[n:2 * n]
        send_sems, recv_sems = refs[2 * n], refs[2 * n + 1]
        me = (lax.axis_index("x"), lax.axis_index("y"), lax.axis_index("c"))
        copies = []
        for i in range(n):
            peer = tuple(1 - m if f else m for m, f in zip(me, flips[i]))
            cp = pltpu.make_async_remote_copy(src_ref=srcs[i], dst_ref=dsts[i], send_sem=send_sems.at[i],
                                              recv_sem=recv_sems.at[i], device_id=peer, device_id_type=MESH)
            cp.start()
            copies.append(cp)
        for cp in copies:
            cp.wait()

    return pl.pallas_call(
        body, in_specs=[hbm] * n, out_specs=[hbm] * n,
        out_shape=[jax.ShapeDtypeStruct(b.shape, b.dtype) for b in bufs],
        scratch_shapes=[pltpu.SemaphoreType.DMA((n,)), pltpu.SemaphoreType.DMA((n,))],
        name=name, compiler_params=pltpu.CompilerParams(has_side_effects=True))(*bufs)


FLIP_C = (False, False, True)
FLIP_Y = (False, True, False)
FLIP_X = (True, False, False)


HALF_ROWS = PACK_ROWS // 2
SUM_ROWS = 512


def _mesh_pos():
    ix, iy, core = lax.axis_index("x"), lax.axis_index("y"), lax.axis_index("c")
    others = [(ix, 1 - iy), (1 - ix, iy), (1 - ix, 1 - iy)]
    return ix, iy, core, others


def _pack_big(w_in, w_out, w_up, w_down):
    pad = jnp.pad(w_in, ((0, 0), (0, 0), (0, 1024 - w_in.shape[-1])))
    return jnp.concatenate([pad.reshape(-1, 1024), w_up.reshape(-1, 1024), w_down.reshape(-1, 1024),
                            w_out.reshape(-1, 1024)], axis=0)


def _unpack_big(p):
    w_in = p[0:4096].reshape(4, 1024, 1024)[:, :, :D_IN // 4]
    w_up = p[4096:8192].reshape(4, 1024, 1024)
    w_down = p[8192:12288].reshape(4, 1024, 1024)
    w_out = p[12288:13312].reshape(4, 256, 1024)
    return w_in, w_out, w_up, w_down


def _gather_big(packed):
    hbm = pl.BlockSpec(memory_space=pltpu.HBM)

    def body(src, out, send_sems, recv_sems):
        ix, iy, core, others = _mesh_pos()
        mine = pl.ds(core * HALF_ROWS, HALF_ROWS)
        theirs = pl.ds((1 - core) * HALF_ROWS, HALF_ROWS)

        def copy(k, src_ref, owner, rows, to):
            return pltpu.make_async_remote_copy(
                src_ref=src_ref, dst_ref=out.at[owner, rows], send_sem=send_sems.at[k], recv_sem=recv_sems.at[k],
                device_id=to, device_id_type=MESH)

        chip = 2 * ix + iy
        first = [copy(j, src.at[mine], chip, mine, (*others[j], core)) for j in range(3)]
        for cp in first:
            cp.start()
        passed = []
        for j, (ox, oy) in enumerate(others):
            owner = 2 * ox + oy
            copy(j, src.at[mine], owner, mine, (ix, iy, core)).wait_recv()
            fwd = copy(3 + j, out.at[owner, mine], owner, mine, (ix, iy, 1 - core))
            fwd.start()
            passed.append(fwd)
        for j, (ox, oy) in enumerate(others):
            copy(3 + j, src.at[theirs], 2 * ox + oy, theirs, (ix, iy, core)).wait_recv()
        for cp in first + passed:
            cp.wait_send()

    return pl.pallas_call(
        body, in_specs=[hbm], out_specs=hbm, out_shape=jax.ShapeDtypeStruct((4,) + packed.shape, packed.dtype),
        scratch_shapes=[pltpu.SemaphoreType.DMA((6,)), pltpu.SemaphoreType.DMA((6,))],
        name="gather_big", compiler_params=pltpu.CompilerParams(has_side_effects=True))(packed)


def _rs_sibling(per_chip):
    hbm = pl.BlockSpec(memory_space=pltpu.HBM)

    def body(src, out, send_sem, recv_sem):
        ix, iy, core, _ = _mesh_pos()
        give = pl.ds((1 - core) * HALF_ROWS, HALF_ROWS)
        cp = pltpu.make_async_remote_copy(src_ref=src.at[pl.ds(0, 4), give], dst_ref=out, send_sem=send_sem,
                                          recv_sem=recv_sem, device_id=(ix, iy, 1 - core), device_id_type=MESH)
        cp.start()
        cp.wait()

    return pl.pallas_call(
        body, in_specs=[hbm], out_specs=hbm, out_shape=jax.ShapeDtypeStruct((4, HALF_ROWS, 1024), F32),
        scratch_shapes=[pltpu.SemaphoreType.DMA, pltpu.SemaphoreType.DMA],
        name="rs_sibling", compiler_params=pltpu.CompilerParams(has_side_effects=True))(per_chip)


def _rs_add_pair(per_chip, sib, chip, core):
    nb = HALF_ROWS // SUM_ROWS
    blk = (1, SUM_ROWS, 1024)

    def body(chip_ref, core_ref, a_ref, b_ref, own_ref, send_ref):
        total = a_ref[0] + b_ref[0]
        send_ref[0] = total.astype(BF16)

        @pl.when(pl.program_id(1) == chip_ref[0])
        def _():
            own_ref[...] = total

    return pl.pallas_call(
        body,
        grid_spec=pltpu.PrefetchScalarGridSpec(
            num_scalar_prefetch=2, grid=(nb, 4),
            in_specs=[pl.BlockSpec(blk, lambda i, k, ch, co: (k, co[0] * nb + i, 0)),
                      pl.BlockSpec(blk, lambda i, k, ch, co: (k, i, 0))],
            out_specs=[pl.BlockSpec((SUM_ROWS, 1024), lambda i, k, ch, co: (i, 0)),
                       pl.BlockSpec(blk, lambda i, k, ch, co: (k, i, 0))]),
        out_shape=[jax.ShapeDtypeStruct((HALF_ROWS, 1024), F32), jax.ShapeDtypeStruct((4, HALF_ROWS, 1024), BF16)],
        name="rs_add_pair", compiler_params=_params(("arbitrary", "arbitrary")))(
            chip.reshape(1), core.reshape(1), per_chip, sib)


def _rs_chips(pair):
    hbm = pl.BlockSpec(memory_space=pltpu.HBM)

    def body(src, out, send_sems, recv_sems):
        ix, iy, core, others = _mesh_pos()
        copies = []
        for j, (ox, oy) in enumerate(others):
            cp = pltpu.make_async_remote_copy(src_ref=src.at[2 * ox + oy], dst_ref=out.at[j], send_sem=send_sems.at[j],
                                              recv_sem=recv_sems.at[j], device_id=(ox, oy, core), device_id_type=MESH)
            cp.start()
            copies.append(cp)
        for cp in copies:
            cp.wait()

    return pl.pallas_call(
        body, in_specs=[hbm], out_specs=hbm, out_shape=jax.ShapeDtypeStruct((3, HALF_ROWS, 1024), pair.dtype),
        scratch_shapes=[pltpu.SemaphoreType.DMA((3,)), pltpu.SemaphoreType.DMA((3,))],
        name="rs_chips", compiler_params=pltpu.CompilerParams(has_side_effects=True))(pair)


def _rs_add_chips(own, got, core):
    nb = HALF_ROWS // SUM_ROWS
    blk = (1, SUM_ROWS, 1024)

    def body(core_ref, a_ref, g0_ref, g1_ref, g2_ref, o_ref):
        o_ref[...] = ((a_ref[...] + g0_ref[0].astype(F32)) + g1_ref[0].astype(F32)) + g2_ref[0].astype(F32)

    return pl.pallas_call(
        body,
        grid_spec=pltpu.PrefetchScalarGridSpec(
            num_scalar_prefetch=1, grid=(nb,),
            in_specs=[pl.BlockSpec((SUM_ROWS, 1024), lambda i, c: (i, 0))]
            + [pl.BlockSpec(blk, functools.partial(lambda j, i, c: (j, i, 0), j)) for j in range(3)],
            out_specs=pl.BlockSpec((SUM_ROWS, 1024), lambda i, c: (c[0] * nb + i, 0))),
        out_shape=jax.ShapeDtypeStruct((PACK_ROWS, 1024), F32),
        name="rs_add_chips", compiler_params=_params(("arbitrary",)))(core.reshape(1), own, got, got, got)


def _rs_share(buf):
    hbm = pl.BlockSpec(memory_space=pltpu.HBM)

    def body(src, out, send_sem, recv_sem):
        ix, iy, core, _ = _mesh_pos()
        mine = pl.ds(core * HALF_ROWS, HALF_ROWS)
        theirs = pl.ds((1 - core) * HALF_ROWS, HALF_ROWS)
        cp = pltpu.make_async_remote_copy(src_ref=src.at[mine], dst_ref=out.at[mine], send_sem=send_sem,
                                          recv_sem=recv_sem, device_id=(ix, iy, 1 - core), device_id_type=MESH)
        cp.start()
        pltpu.make_async_remote_copy(src_ref=src.at[theirs], dst_ref=out.at[theirs], send_sem=send_sem,
                                     recv_sem=recv_sem, device_id=(ix, iy, core), device_id_type=MESH).wait_recv()
        cp.wait_send()

    return pl.pallas_call(
        body, in_specs=[hbm], out_specs=hbm, out_shape=jax.ShapeDtypeStruct(buf.shape, buf.dtype),
        input_output_aliases={0: 0}, scratch_shapes=[pltpu.SemaphoreType.DMA, pltpu.SemaphoreType.DMA],
        name="rs_share", compiler_params=pltpu.CompilerParams(has_side_effects=True))(buf)


def _reduce_scatter_big(per_chip, chip, core):
    own, pair = _rs_add_pair(per_chip, _rs_sibling(per_chip), chip, core)
    return _rs_share(_rs_add_chips(own, _rs_chips(pair), core))


def _allreduce_small(tag, buf):
    for name, flip in (("c", FLIP_C), ("y", FLIP_Y), ("x", FLIP_X)):
        (other,) = _swap(tag + "_swap_" + name, [buf], [flip])
        (buf,) = _ew(tag + "_add_" + name, _add2_fn, [buf, other], 1)
    return buf


def _permute_in_cols(w):
    z = jnp.zeros(w.shape[:-1] + (D_IN_PAD - D_IN,), w.dtype)
    return jnp.concatenate([w[..., 1024:3072], w[..., :1024], w[..., 3080:D_IN], w[..., 3072:3080], z], axis=-1)


IN_COL_SEGMENTS = ((0, 1024, 2048), (1024, 3072, -1024), (3072, 3080, GATE0 - 3072), (3080, D_IN, COL_CX - 3080))


def _orig_cols(w_pad, lo, hi):
    parts = []
    for first, last, shift in IN_COL_SEGMENTS:
        a, b = max(lo, first), min(hi, last)
        if a < b:
            parts.append(w_pad[..., a + shift:b + shift])
    return parts


def _block_diag(w):
    out = jnp.zeros((C_W, C_W), w.dtype)
    for n in range(4):
        out = lax.dynamic_update_slice(out, w[n], (64 * n, 64 * n))
    return out


def _diag_blocks(w):
    return jnp.stack([w[64 * n:64 * (n + 1), 64 * n:64 * (n + 1)] for n in range(4)], axis=0)


def _gate_row(v):
    return jnp.concatenate([jnp.zeros((4,), F32), v, jnp.zeros((120,), F32)]).reshape(1, 128)


SMALL = ("norm1_g", "hgrn_lb_logits", "hgrn_norm_g", "gdn_conv_w", "gdn_a_log", "gdn_dt_bias", "gdn_norm_g",
         "lru_conv_w", "lru_conv_b", "lru_w_a", "lru_b_a", "lru_w_x", "lru_b_x", "lru_lambda", "norm2_g",
         "final_norm_g")


def _flatten_small(tree):
    flat = jnp.concatenate([tree[k].reshape(-1) for k in sorted(tree)])
    rows = -(-flat.shape[0] // (512 * 128)) * 512
    return jnp.pad(flat, (0, rows * 128 - flat.shape[0])).reshape(rows, 128)


def _unflatten_small(buf, shapes):
    flat = buf.reshape(-1)
    out, off = {}, 0
    for k in sorted(shapes):
        size = 1
        for d in shapes[k]:
            size *= d
        out[k] = flat[off:off + size].reshape(shapes[k])
        off += size
    return out


def _layer_fwd(l, x, p, w_in, w_out, w_up, w_down):
    sv = {"x0": x}
    (h,) = _rowwise_fwd("norm1", _rmsnorm_fn, [p["norm1_g"]], [x], [D_MODEL], [BF16], 512)
    proj = _matmul("proj_in", h, w_in, "nn")
    sv["h"], sv["proj"] = h, proj
    aq, af, ag = _cols(proj, COL_AQ, A_W), _cols(proj, COL_AF, A_W), _cols(proj, COL_AG, A_W)
    ai = lax.slice_in_dim(proj, COL_AI, COL_AI + A_W, axis=1)
    bz, gate, cy = _cols(proj, COL_BZ, B_W), _cols(proj, GATE0, 128), _cols(proj, COL_CY, C_W)
    q, k, lf = _rowwise_fwd("hgrn_pre", functools.partial(_hgrn_pre_fn, l), [p["lb_logits"]], [aq, af],
                            [A_W] * 3, [F32] * 3, 512)
    o_a, sts_a = _hgrn_fwd("hgrn_chunk", q, k, ai, lf)
    (y_a,) = _rowwise_fwd("hgrn_post", _hgrn_post_fn, [p["hgrn_norm_g"]], [o_a, ag], [A_W], [BF16], 512)
    sv.update(aq=aq, af=af, ai=ai, ag=ag, hq=q, hk=k, hlf=lf, o_a=o_a, sts_a=sts_a)
    qkvc = _conv_fwd("gdn_conv", proj, 0, 3 * B_W, p["gdn_conv_w"], jnp.zeros((1, 3 * B_W), F32), 256)
    gq, gk, gv, gb, gla = _rowwise_fwd("gdn_pre", _gdn_pre_fn, [p["gdn_a_log"], p["gdn_dt_bias"]], [qkvc, gate],
                                       [B_W] * 5, [F32] * 5, 256)
    wy = _gdn_prep_fwd("gdn_prep", gq, gk, gv, gb, gla)
    o_b, sts_b = _gdn_rec_fwd("gdn_rec", *wy)
    sv["wy"] = wy
    (y_b,) = _rowwise_fwd("gdn_post", _gdn_post_fn, [p["gdn_norm_g"]], [o_b, bz], [B_W], [BF16], 512)
    sv.update(qkvc=qkvc, gate=gate, bz=bz, gq=gq, gk=gk, gv=gv, gb=gb, gla=gla, o_b=o_b, sts_b=sts_b)
    xc = _conv_fwd("lru_conv", proj, COL_CX, C_W, p["lru_conv_w"], p["lru_conv_b"], 512)
    a, b = _rowwise_fwd("lru_pre", _lru_pre_fn, [p["lru_w_a"], p["lru_b_a"], p["lru_w_x"], p["lru_b_x"], p["lru_lambda"]],
                        [xc], [C_W] * 2, [F32] * 2, 512)
    hs, h_prev = _lru_scan_fwd("lru_scan", a, b)
    (y_c,) = _rowwise_fwd("lru_post", _lru_post_fn, [], [hs, cy], [C_W], [BF16], 512)
    sv.update(xc=xc, la=a, hs=hs, h_prev=h_prev, cy=cy)
    mixed = jnp.concatenate([y_b, y_a, y_c], axis=1)
    x1 = _matmul("proj_out", mixed, w_out, "nn", res=x)
    (h2,) = _rowwise_fwd("norm2", _rmsnorm_fn, [p["norm2_g"]], [x1], [D_MODEL], [BF16], 512)
    up, act = _matmul("mlp_up", h2, w_up, "nn", also=_relu2)
    x2 = _matmul("mlp_down", act, w_down, "nn", res=x1)
    sv.update(mixed=mixed, x1=x1, h2=h2, up=up, act=act)
    return x2, sv


def _layer_bwd(l, dx, dx16, p, sv, w_in, w_out, w_up, w_down):
    gs = {}
    dup = _matmul("mlp_down_dx", dx16, w_down, "nt", epi=_drelu2, epi_in=sv["up"], out_dtype=BF16)
    d_w_down = _matmul("mlp_down_dw", sv["act"], dx16, "tn")
    dh2 = _matmul("mlp_up_dx", dup, w_up, "nt")
    d_w_up = _matmul("mlp_up_dw", sv["h2"], dup, "tn")
    (gs["norm2_g"],), (dx1, dx1_16) = _rowwise_bwd("norm2_bwd", _rmsnorm_fn, [p["norm2_g"]], [sv["x1"]], [dh2], 512,
                                                   res=dx, copy16=True)
    dmixed = _matmul("proj_out_dx", dx1_16, w_out, "nt")
    d_w_out = _matmul("proj_out_dw", sv["mixed"], dx1_16, "tn")
    dy_b, dy_a, dy_c = _cols(dmixed, 0, B_W), _cols(dmixed, B_W, A_W), _cols(dmixed, B_W + A_W, C_W)
    (gs["hgrn_norm_g"],), (do_a, dag) = _rowwise_bwd("hgrn_post_bwd", _hgrn_post_fn, [p["hgrn_norm_g"]],
                                                     [sv["o_a"], sv["ag"]], [dy_a], 512, dtypes=[F32, BF16])
    dq, dk, dai, dlf = _hgrn_bwd("hgrn_chunk_bwd", sv["hq"], sv["hk"], sv["ai"], sv["hlf"], sv["sts_a"], do_a)
    (gs["lb_logits"],), (daq, daf) = _rowwise_bwd("hgrn_pre_bwd", functools.partial(_hgrn_pre_fn, l), [p["lb_logits"]],
                                                  [sv["aq"], sv["af"]], [dq, dk, dlf], 512, dtypes=[BF16, BF16])
    (gs["gdn_norm_g"],), (do_b, dbz) = _rowwise_bwd("gdn_post_bwd", _gdn_post_fn, [p["gdn_norm_g"]],
                                                    [sv["o_b"], sv["bz"]], [dy_b], 512, dtypes=[F32, BF16])
    d_wy = _gdn_rec_bwd("gdn_rec_bwd", *sv["wy"], sv["sts_b"], do_b)
    dgq, dgk, dgv, dgb, dgla = _gdn_prep_bwd("gdn_prep_bwd", sv["gq"], sv["gk"], sv["gv"], sv["gb"], sv["gla"], d_wy)
    (gs["gdn_a_log"], gs["gdn_dt_bias"]), (dqkvc, dgate) = _rowwise_bwd(
        "gdn_pre_bwd", _gdn_pre_fn, [p["gdn_a_log"], p["gdn_dt_bias"]], [sv["qkvc"], sv["gate"]],
        [dgq, dgk, dgv, dgb, dgla], 256, dtypes=[F32, BF16])
    dqkv, gs["gdn_conv_w"], _ = _conv_bwd("gdn_conv_bwd", dqkvc, sv["proj"], 0, 3 * B_W, p["gdn_conv_w"], 256)
    _, (dhs, dcy) = _rowwise_bwd("lru_post_bwd", _lru_post_fn, [], [sv["hs"], sv["cy"]], [dy_c], 512, dtypes=[F32, BF16])
    g = _lru_scan_bwd("lru_scan_bwd", sv["la"], dhs)
    (da,) = _rowwise_fwd("lru_da", _lru_da_fn, [], [g, sv["h_prev"]], [C_W], [F32], 512)
    lru_params = [p["lru_w_a"], p["lru_b_a"], p["lru_w_x"], p["lru_b_x"], p["lru_lambda"]]
    dps, (dxc,) = _rowwise_bwd("lru_pre_bwd", _lru_pre_fn, lru_params, [sv["xc"]], [da, g], 512)
    gs["lru_w_a"], gs["lru_b_a"], gs["lru_w_x"], gs["lru_b_x"], gs["lru_lambda"] = dps
    dcx, gs["lru_conv_w"], gs["lru_conv_b"] = _conv_bwd("lru_conv_bwd", dxc, sv["proj"], COL_CX, C_W, p["lru_conv_w"], 512)
    s = dx.shape[0]
    dproj = jnp.concatenate([dqkv, dbz, daq, daf, dai.astype(BF16), dag, dcx, dcy, dgate,
                             jnp.zeros((s, D_IN_PAD - GATE0 - 128), BF16)], axis=1)
    dh = _matmul("proj_in_dx", dproj, w_in, "nt")
    d_w_in = _matmul("proj_in_dw", sv["h"], dproj, "tn")
    (gs["norm1_g"],), (dx0, dx0_16) = _rowwise_bwd("norm1_bwd", _rmsnorm_fn, [p["norm1_g"]], [sv["x0"]], [dh], 512,
                                                   res=dx1, copy16=True)
    return dx0, dx0_16, dict(w_in=d_w_in, w_out=d_w_out, w_up=d_w_up, w_down=d_w_down), gs


def _layer_params(l, lb_logits, hgrn_norm_g, gdn_conv_w, gdn_a_log, gdn_dt_bias, gdn_norm_g, lru_conv_w, lru_conv_b,
                  lru_w_a, lru_b_a, lru_w_x, lru_b_x, lru_lambda, norm1_g, norm2_g):
    row = lambda v: v.reshape(1, -1)
    return dict(
        norm1_g=row(norm1_g[l]), norm2_g=row(norm2_g[l]), lb_logits=lb_logits,
        hgrn_norm_g=row(jnp.tile(hgrn_norm_g[l], A_W // A_HD)),
        gdn_conv_w=gdn_conv_w[l], gdn_a_log=_gate_row(gdn_a_log[l]), gdn_dt_bias=_gate_row(gdn_dt_bias[l]),
        gdn_norm_g=row(jnp.tile(gdn_norm_g[l], B_NH)),
        lru_conv_w=lru_conv_w[l], lru_conv_b=row(lru_conv_b[l]), lru_w_a=_block_diag(lru_w_a[l]), lru_b_a=row(lru_b_a[l]),
        lru_w_x=_block_diag(lru_w_x[l]), lru_b_x=row(lru_b_x[l]), lru_lambda=row(lru_lambda[l]))


def _small_grads_to_reference(gs_layers, d_final_g, d_logits):
    st = lambda k, f=lambda z: z: jnp.stack([f(g[k]) for g in gs_layers], axis=0)
    vec = lambda z: z.reshape(-1)
    return dict(
        norm1_g=st("norm1_g", vec), norm2_g=st("norm2_g", vec), hgrn_lb_logits=d_logits,
        hgrn_norm_g=st("hgrn_norm_g", lambda z: z.reshape(A_W // A_HD, A_HD).sum(0)),
        gdn_conv_w=st("gdn_conv_w"), gdn_a_log=st("gdn_a_log", lambda z: z[0, 4:8]),
        gdn_dt_bias=st("gdn_dt_bias", lambda z: z[0, 4:8]),
        gdn_norm_g=st("gdn_norm_g", lambda z: z.reshape(B_NH, B_HD).sum(0)),
        lru_conv_w=st("lru_conv_w"), lru_conv_b=st("lru_conv_b", vec), lru_w_a=st("lru_w_a", _diag_blocks),
        lru_b_a=st("lru_b_a", vec), lru_w_x=st("lru_w_x", _diag_blocks), lru_b_x=st("lru_b_x", vec),
        lru_lambda=st("lru_lambda", vec), final_norm_g=d_final_g.reshape(-1))


def _local_step(x, tgt, small, w_in_full, w_out_full, w_up_full, w_down_full):
    layer_p = [_layer_params(l, small["hgrn_lb_logits"], small["hgrn_norm_g"], small["gdn_conv_w"], small["gdn_a_log"],
                             small["gdn_dt_bias"], small["gdn_norm_g"], small["lru_conv_w"], small["lru_conv_b"],
                             small["lru_w_a"], small["lru_b_a"], small["lru_w_x"], small["lru_b_x"], small["lru_lambda"],
                             small["norm1_g"], small["norm2_g"]) for l in range(DEPTH)]
    saved = []
    for l in range(DEPTH):
        x, sv = _layer_fwd(l, x, layer_p[l], w_in_full[l], w_out_full[l], w_up_full[l], w_down_full[l])
        saved.append(sv)
    loss, dx, d_final_g, dx16 = _loss_and_grad("final_loss", small["final_norm_g"].reshape(1, -1), x, tgt)
    big = [None] * DEPTH
    gs_layers = [None] * DEPTH
    d_logits = jnp.zeros_like(small["hgrn_lb_logits"])
    for l in reversed(range(DEPTH)):
        dx, dx16, big[l], gs_layers[l] = _layer_bwd(l, dx, dx16, layer_p[l], saved[l], w_in_full[l], w_out_full[l],
                                                    w_up_full[l], w_down_full[l])
        d_logits = d_logits + gs_layers[l]["lb_logits"]
    return loss, dx, big, _small_grads_to_reference(gs_layers, d_final_g, d_logits)


def kernel(x, norm1_g, w_in, hgrn_lb_logits, hgrn_norm_g, gdn_conv_w, gdn_a_log, gdn_dt_bias, gdn_norm_g, lru_conv_w, lru_conv_b, lru_w_a, lru_b_a, lru_w_x, lru_b_x, lru_lambda, w_out, norm2_g, w_up, w_down, final_norm_g, loss_target, m_norm1_g, m_w_in, m_hgrn_lb_logits, m_hgrn_norm_g, m_gdn_conv_w, m_gdn_a_log, m_gdn_dt_bias, m_gdn_norm_g, m_lru_conv_w, m_lru_conv_b, m_lru_w_a, m_lru_b_a, m_lru_w_x, m_lru_b_x, m_lru_lambda, m_w_out, m_norm2_g, m_w_up, m_w_down, m_final_norm_g, v_norm1_g, v_w_in, v_hgrn_lb_logits, v_hgrn_norm_g, v_gdn_conv_w, v_gdn_a_log, v_gdn_dt_bias, v_gdn_norm_g, v_lru_conv_w, v_lru_conv_b, v_lru_w_a, v_lru_b_a, v_lru_w_x, v_lru_b_x, v_lru_lambda, v_w_out, v_norm2_g, v_w_up, v_w_down, v_final_norm_g):
    args = dict(locals())
    ix, iy, core = lax.axis_index("x"), lax.axis_index("y"), lax.axis_index("c")
    chip = 2 * ix + iy

    packed = _pack_big(w_in, w_out, w_up, w_down).astype(BF16)
    shares = lax.dynamic_update_slice(_gather_big(packed), packed[None], (chip, 0, 0))
    parts = [_unpack_big(shares[k]) for k in range(4)]
    w_in_full = _permute_in_cols(jnp.concatenate([pt[0] for pt in parts], axis=-1))
    w_out_full = jnp.concatenate([parts[k][1] for k in W_OUT_CHIP_ORDER], axis=1)
    w_up_full = jnp.concatenate([pt[2] for pt in parts], axis=-1)
    w_down_full = jnp.concatenate([pt[3] for pt in parts], axis=1)

    n_gc, n_lc = gdn_conv_w.shape[-1], lru_conv_w.shape[-1]
    conv_full = dict(
        gdn_conv_w=lax.dynamic_update_slice(jnp.zeros((DEPTH, 4, 4 * n_gc), F32), gdn_conv_w * 0.5, (0, 0, chip * n_gc)),
        lru_conv_w=lax.dynamic_update_slice(jnp.zeros((DEPTH, 4, 4 * n_lc), F32), lru_conv_w * 0.5, (0, 0, chip * n_lc)))
    conv_shapes = {k: v.shape for k, v in conv_full.items()}
    conv_full = _unflatten_small(_allreduce_small("convw", _flatten_small(conv_full)), conv_shapes)

    small = {k: args[k] for k in SMALL}
    small.update(conv_full)

    loss, grad_x, big, sg = _local_step(x[0], loss_target[0], small, w_in_full, w_out_full, w_up_full, w_down_full)

    sg["loss"] = loss[0, :1]
    shapes = {k: v.shape for k, v in sg.items()}
    sg = _unflatten_small(_allreduce_small("smallgrad", _flatten_small(sg)), shapes)
    loss_out = sg.pop("loss")[0]

    n_in = D_IN // 4
    pieces = []
    for k in range(4):
        for b in big:
            cols = jnp.concatenate(_orig_cols(b["w_in"], k * n_in, (k + 1) * n_in), axis=1)
            pieces.append(jnp.pad(cols, ((0, 0), (0, 1024 - n_in))))
        pieces += [b["w_up"][:, k * 1024:(k + 1) * 1024] for b in big]
        pieces += [b["w_down"][k * 1024:(k + 1) * 1024] for b in big]
        r0 = 256 * W_OUT_CHIP_ORDER.index(k)
        pieces += [b["w_out"][r0:r0 + 256] for b in big]
    per_chip = jnp.concatenate(pieces, axis=0).reshape(4, PACK_ROWS, 1024)
    g_in, g_out, g_up, g_down = _unpack_big(_reduce_scatter_big(per_chip, chip, core))

    grads = dict(sg)
    grads["gdn_conv_w"] = lax.dynamic_slice_in_dim(sg["gdn_conv_w"], chip * n_gc, n_gc, axis=2)
    grads["lru_conv_w"] = lax.dynamic_slice_in_dim(sg["lru_conv_w"], chip * n_lc, n_lc, axis=2)
    grads.update(w_in=g_in, w_out=g_out, w_up=g_up, w_down=g_down)

    names = ['norm1_g', 'w_in', 'hgrn_lb_logits', 'hgrn_norm_g', 'gdn_conv_w', 'gdn_a_log', 'gdn_dt_bias', 'gdn_norm_g',
             'lru_conv_w', 'lru_conv_b', 'lru_w_a', 'lru_b_a', 'lru_w_x', 'lru_b_x', 'lru_lambda', 'w_out', 'norm2_g',
             'w_up', 'w_down', 'final_norm_g']
    big_names = ("w_in", "w_out", "w_up", "w_down")
    delta, new_m, new_v = {}, {}, {}
    for k in big_names:
        delta[k], new_m[k], new_v[k] = _adam("adam_" + k, args[k], grads[k], args["m_" + k], args["v_" + k])
    small_names = [k for k in names if k not in big_names]
    shapes = {k: args[k].shape for k in small_names}
    flat = [_flatten_small({k: src(k) for k in small_names})
            for src in (lambda k: args[k], lambda k: grads[k], lambda k: args["m_" + k], lambda k: args["v_" + k])]
    outs = _ew("adam_small", _adam_fn, flat, 3)
    for dst, o in zip((delta, new_m, new_v), outs):
        dst.update(_unflatten_small(o, shapes))
    return (loss_out, grad_x[None], *[grads[k] for k in names], *[delta[k] for k in names],
            *[new_m[k] for k in names], *[new_v[k] for k in names])
```

```python
import functools

import jax
import jax.numpy as jnp
from jax import lax
from jax.experimental import pallas as pl
from jax.experimental.pallas import tpu as pltpu
from jax.experimental.pallas import tpu_sc as plsc

F32 = jnp.float32
BF16 = jnp.bfloat16
MESH = pl.DeviceIdType.MESH

DEPTH = 4
D_MODEL = 1024
A_W, B_W, C_W = 256, 512, 256
A_HD, B_HD = 64, 128
B_NH = B_W // B_HD
D_IN = 3592
D_IN_PAD = 3840
COL_BZ, COL_AQ, COL_AF, COL_AI, COL_AG, COL_CX, COL_CY = 1536, 2048, 2304, 2560, 2816, 3072, 3328
W_OUT_CHIP_ORDER = (1, 2, 0, 3)
GATE0 = 3584
D_FF = 4096
CHUNK = 64
SUB = 16
EPS = 1e-6
TINY = 1e-30
RG_C = 8.0
LR, B1, B2, AEPS, WD, STEP = 0.001, 0.9, 0.999, 1e-08, 0.01, 10
VMEM_LIMIT = 56 * 1024 * 1024
PACK_ROWS = 13 * 1024

NN = ((1,), (0,))
NT = ((1,), (1,))
TN = ((0,), (0,))


DIMS = {"nn": NN, "nt": NT, "tn": TN}


def _split(a, n):
    parts, r = [], a
    for i in range(n):
        p = r.astype(BF16)
        parts.append(p)
        if i + 1 < n:
            r = r - p.astype(F32)
    return parts


def _split_dot(a, b, mode, na, nb):
    (ca,), (cb,) = DIMS[mode]
    dims = ((DIMS[mode]), ((), ())) if a.ndim == 2 else (((ca + 1,), (cb + 1,)), ((0,), (0,)))
    pa, pb = _split(a, na), _split(b, nb)
    acc = None
    for i in range(na):
        for j in range(nb):
            if i + j < max(na, nb):
                t = lax.dot_general(pa[i], pb[j], dims, preferred_element_type=F32)
                acc = t if acc is None else acc + t
    return acc


@functools.partial(jax.custom_vjp, nondiff_argnums=(2, 3, 4, 5))
def _mdot(a, b, mode, na, nb, need):
    return _split_dot(a, b, mode, na, nb)


def _mdot_fwd(a, b, mode, na, nb, need):
    return _split_dot(a, b, mode, na, nb), (a, b)


def _mdot_bwd(mode, na, nb, need, res, ct):
    a, b = res
    n = max(na, nb)
    da, db = jnp.zeros_like(a), jnp.zeros_like(b)
    if mode == "nn":
        if need[0]:
            da = _split_dot(ct, b, "nt", n, nb)
        if need[1]:
            db = _split_dot(a, ct, "tn", na, n)
    elif mode == "nt":
        if need[0]:
            da = _split_dot(ct, b, "nn", n, nb)
        if need[1]:
            db = _split_dot(ct, a, "tn", n, na)
    else:
        if need[0]:
            da = _split_dot(b, ct, "nt", nb, n)
        if need[1]:
            db = _split_dot(a, ct, "nn", na, n)
    return da, db


_mdot.defvjp(_mdot_fwd, _mdot_bwd)
BOTH = (True, True)


def _bdot(a, b, mode="nn"):
    return _mdot(a, b, mode, 1, 1, BOTH)


def _group_sum(x, ones):
    return _mdot(x, ones, "nn", 2, 1, (True, False))


def _running_sum(x):
    shape = x.shape[:-2] + (CHUNK, CHUNK)
    tri = (_iota(shape, len(shape) - 1) <= _iota(shape, len(shape) - 2)).astype(F32)
    return _mdot(tri, x, "nn", 1, 3, (False, True))


def _iota(shape, d):
    return lax.broadcasted_iota(jnp.int32, shape, d)


def _block_ones(width, group):
    return (_iota((width, width), 0) // group == _iota((width, width), 1) // group).astype(F32)


def _params(sem):
    return pltpu.CompilerParams(dimension_semantics=sem, vmem_limit_bytes=VMEM_LIMIT)


def _tile(dim, prefs):
    for p in prefs:
        if dim % p == 0:
            return p
    return dim


def _whole(shape):
    return pl.BlockSpec(shape, lambda i: (0,) * len(shape))


def _rowspec(t, w, col0=0):
    cb = col0 // w
    return pl.BlockSpec((t, w), lambda i: (i, cb))


def _cols(arr, col0, width):
    assert col0 % width == 0
    return (arr, col0, width)


def _row_operands(rows, t):
    arrs, specs = [], []
    for r in rows:
        arr, col0, w = r if isinstance(r, tuple) else (r, 0, r.shape[1])
        arrs.append(arr)
        specs.append(_rowspec(t, w, col0))
    return arrs, specs


def _rowwise_fwd(name, fn, params, rows, out_widths, out_dtypes, t):
    rows, row_specs = _row_operands(rows, t)
    s = rows[0].shape[0]
    n_p, n_r = len(params), len(rows)

    def body(*refs):
        p = [r[...] for r in refs[:n_p]]
        xs = [r[...].astype(F32) for r in refs[n_p:n_p + n_r]]
        outs = fn(pl.program_id(0) * t, *p, *xs)
        for o_ref, o in zip(refs[n_p + n_r:], outs):
            o_ref[...] = o.astype(o_ref.dtype)

    return pl.pallas_call(
        body, grid=(s // t,),
        in_specs=[_whole(p.shape) for p in params] + row_specs,
        out_specs=[_rowspec(t, w) for w in out_widths],
        out_shape=[jax.ShapeDtypeStruct((s, w), d) for w, d in zip(out_widths, out_dtypes)],
        name=name, compiler_params=_params(("arbitrary",)))(*params, *rows)


def _rowwise_bwd(name, fn, params, rows, douts, t, need=None, res=None, dtypes=None, copy16=False):
    rows, row_specs = _row_operands(rows, t)
    douts, dout_specs = _row_operands(douts, t)
    s = rows[0].shape[0]
    n_p, n_r, n_o = len(params), len(rows), len(douts)
    need = [True] * n_r if need is None else need
    n_res = 0 if res is None else 1

    def body(*refs):
        p = [r[...] for r in refs[:n_p]]
        xs = [r[...].astype(F32) for r in refs[n_p:n_p + n_r]]
        dys = [r[...].astype(F32) for r in refs[n_p + n_r:n_p + n_r + n_o]]
        k = n_p + n_r + n_o
        res_ref = refs[k] if n_res else None
        dp_refs = refs[k + n_res:k + n_res + n_p]
        dx_refs = refs[k + n_res + n_p:]
        row0 = pl.program_id(0) * t
        _, vjp = jax.vjp(lambda *a: tuple(fn(row0, *a)), *p, *xs)
        g = vjp(tuple(dys))

        @pl.when(pl.program_id(0) == 0)
        def _():
            for r in dp_refs:
                r[...] = jnp.zeros_like(r)

        for r, gp in zip(dp_refs, g[:n_p]):
            r[...] += gp
        gx = [gi for gi, nd in zip(g[n_p:], need) if nd]
        if n_res:
            gx[0] = gx[0] + res_ref[...]
        if copy16:
            gx.append(gx[0])
        for r, gi in zip(dx_refs, gx):
            r[...] = gi.astype(r.dtype)

    widths = [sp.block_shape[1] for sp, nd in zip(row_specs, need) if nd]
    dtypes = [F32] * len(widths) if dtypes is None else list(dtypes)
    if copy16:
        widths, dtypes = widths + widths[:1], dtypes + [BF16]
    outs = pl.pallas_call(
        body, grid=(s // t,),
        in_specs=([_whole(p.shape) for p in params] + row_specs + dout_specs
                  + ([_rowspec(t, res.shape[1])] if n_res else [])),
        out_specs=[_whole(p.shape) for p in params] + [_rowspec(t, w) for w in widths],
        out_shape=([jax.ShapeDtypeStruct(p.shape, F32) for p in params]
                   + [jax.ShapeDtypeStruct((s, w), d) for w, d in zip(widths, dtypes)]),
        name=name, compiler_params=_params(("arbitrary",)))(*params, *rows, *douts, *([res] if n_res else []))
    return list(outs[:n_p]), list(outs[n_p:])


def _rmsnorm_fn(row0, g, x):
    y = x * lax.rsqrt(jnp.mean(x * x, axis=-1, keepdims=True) + EPS)
    return (y * g,)


def _hgrn_pre_fn(layer, row0, logits, aq, af):
    m = jnp.max(logits, axis=0, keepdims=True)
    e = jnp.exp(logits - m)
    p = e / jnp.sum(e, axis=0, keepdims=True)
    if layer == 0:
        lb = jnp.zeros((1, A_W), F32)
    else:
        acc = p[1:2]
        for j in range(2, layer + 1):
            acc = acc + p[j:j + 1]
        lb = jnp.minimum(jnp.maximum(acc, 0.0), 1.0 - EPS)
    sig = jax.nn.sigmoid(af)
    f = lb + (1.0 - lb) * sig
    log_f = jnp.log(jnp.maximum(f, TINY))
    k = (1.0 - lb) * jax.nn.sigmoid(-af)
    return jax.nn.silu(aq), k, log_f


def _hgrn_post_fn(row0, g, o, ag):
    ms = _group_sum(o * o, _block_ones(A_W, A_HD)) * (1.0 / A_HD)
    return (o * lax.rsqrt(ms + EPS) * g * jax.nn.silu(ag),)


def _gdn_pre_fn(row0, a_log, dt_bias, qkvc, gate):
    act = jax.nn.silu(qkvc)
    q, k, v = act[:, :B_W], act[:, B_W:2 * B_W], act[:, 2 * B_W:]
    bo = _block_ones(B_W, B_HD)
    q = q * lax.rsqrt(_group_sum(q * q, bo) + EPS) * (B_HD ** -0.5)
    k = k * lax.rsqrt(_group_sum(k * k, bo) + EPS)
    beta = jax.nn.sigmoid(gate)
    log_alpha = -jnp.exp(a_log) * jax.nn.softplus(gate + dt_bias)
    t = gate.shape[0]
    spread = lambda z, lane0: jnp.concatenate(
        [jnp.broadcast_to(z[:, lane0 + h:lane0 + h + 1], (t, B_HD)) for h in range(B_NH)], axis=1)
    return q, k, v, spread(beta, 0), spread(log_alpha, B_NH)


def _gdn_post_fn(row0, g, o, bz):
    ms = _group_sum(o * o, _block_ones(B_W, B_HD)) * (1.0 / B_HD)
    return (o * lax.rsqrt(ms + EPS) * g * jax.nn.silu(bz),)


def _lru_pre_fn(row0, w_a, b_a, w_x, b_x, lam, xc):
    r = jax.nn.sigmoid(_mdot(xc, w_a, "nn", 2, 2, BOTH) + b_a)
    i = jax.nn.sigmoid(_mdot(xc, w_x, "nn", 2, 2, BOTH) + b_x)
    log_a = -RG_C * r * jax.nn.softplus(-lam)
    a = jnp.exp(log_a)
    one_minus = -jnp.tanh(log_a) * (a * a + 1.0)
    mult = jnp.sqrt(jnp.maximum(one_minus, EPS))
    pos = row0 + _iota(xc.shape, 0)
    mult = jnp.where(pos == 0, 1.0, mult)
    return a, mult * i * xc


def _lru_post_fn(row0, h, cy):
    return (jax.nn.gelu(cy, approximate=True) * h,)


def _lru_da_fn(row0, g, h_prev):
    return (g * h_prev,)


def _relu2(u):
    r = jnp.maximum(u, 0.0)
    return r * r


def _drelu2(acc, u):
    return acc * (2.0 * jnp.maximum(u, 0.0))


MM_VMEM_BUDGET = 44 * 1024 * 1024


def _mm_tiles(mode, m, n, k, a_bytes, b_bytes, n_extra):
    best = None
    for tk in {k, 4096, 2048, 1024, 512, 256}:
        for tn in {n, 1280, 1024, 768, 512, 256, 128}:
            for tm in {m, 1024, 512, 256, 128}:
                if m % tm or n % tn or k % tk or tm > 1024:
                    continue
                tiles = tm * tk * a_bytes + tk * tn * b_bytes + (1 + n_extra) * tm * tn * 4
                if 2 * tiles + (tm * tn * 4 if tk < k else 0) > MM_VMEM_BUDGET:
                    continue
                a_reads = 1 if tk == k else n // tn
                b_reads = 1 if (tk == k and tn == n) else m // tm
                cost = m * k * a_bytes * a_reads + k * n * b_bytes * b_reads + m * n * 4 * (k // tk - 1)
                if best is None or (cost, -tm) < best[0]:
                    best = ((cost, -tm), (tm, tn, tk))
    if best is None:
        raise ValueError("no matmul tiling fits VMEM")
    return best[1]


def _matmul(name, a, b, mode, a_fn=None, res=None, epi=None, epi_in=None, out_dtype=F32, also=None):
    if mode == "nn":
        (m, k), n = a.shape, b.shape[1]
    elif mode == "nt":
        (m, k), n = a.shape, b.shape[0]
    else:
        (k, m), n = a.shape, b.shape[1]
    n_extra = (res is not None) + (epi_in is not None) + (also is not None)
    tm, tn, tk = _mm_tiles(mode, m, n, k, a.dtype.itemsize, b.dtype.itemsize, n_extra)
    nk = k // tk
    a_spec = {"nn": pl.BlockSpec((tm, tk), lambda i, j, kk: (i, kk)),
              "nt": pl.BlockSpec((tm, tk), lambda i, j, kk: (i, kk)),
              "tn": pl.BlockSpec((tk, tm), lambda i, j, kk: (kk, i))}[mode]
    b_spec = {"nn": pl.BlockSpec((tk, tn), lambda i, j, kk: (kk, j)),
              "nt": pl.BlockSpec((tn, tk), lambda i, j, kk: (j, kk)),
              "tn": pl.BlockSpec((tk, tn), lambda i, j, kk: (kk, j))}[mode]
    o_spec = pl.BlockSpec((tm, tn), lambda i, j, kk: (i, j))
    dims = {"nn": NN, "nt": NT, "tn": TN}[mode]
    extra = [x for x in (res, epi_in) if x is not None]

    def body(*refs):
        a_ref, b_ref = refs[0], refs[1]
        rest = list(refs[2:])
        res_ref = rest.pop(0) if res is not None else None
        epi_ref = rest.pop(0) if epi_in is not None else None
        o_ref = rest.pop(0)
        also_ref = rest.pop(0) if also is not None else None
        acc = rest[0] if nk > 1 else None
        kk = pl.program_id(2)
        at = a_ref[...]
        if a_fn is not None:
            at = a_fn(at.astype(F32))
        part = lax.dot_general(at.astype(BF16), b_ref[...].astype(BF16), (dims, ((), ())),
                               preferred_element_type=F32)

        def finish(out):
            if res_ref is not None:
                out = out + res_ref[...]
            if epi is not None:
                out = epi(out, epi_ref[...])
            o_ref[...] = out.astype(o_ref.dtype)
            if also is not None:
                also_ref[...] = also(out).astype(also_ref.dtype)

        if nk == 1:
            finish(part)
        else:
            @pl.when(kk == 0)
            def _():
                acc[...] = part

            @pl.when(jnp.logical_and(kk > 0, kk < nk - 1))
            def _():
                acc[...] += part

            @pl.when(kk == nk - 1)
            def _():
                finish(acc[...] + part)

    out_shape = [jax.ShapeDtypeStruct((m, n), out_dtype)] + ([jax.ShapeDtypeStruct((m, n), BF16)] if also else [])
    outs = pl.pallas_call(
        body, grid=(m // tm, n // tn, nk),
        in_specs=[a_spec, b_spec] + [o_spec] * len(extra), out_specs=[o_spec] * len(out_shape), out_shape=out_shape,
        scratch_shapes=[pltpu.VMEM((tm, tn), F32)] if nk > 1 else [],
        name=name, compiler_params=_params(("arbitrary", "arbitrary", "arbitrary")))(a, b, *extra)
    return outs if also else outs[0]


HALO = 8


def _conv_fwd(name, x, col0, width, w, bias, t=512):
    s = x.shape[0]
    cb = col0 // width
    hb = t // HALO

    def body(x_ref, halo_ref, w_ref, b_ref, y_ref):
        i = pl.program_id(0)
        halo = jnp.where(i == 0, 0.0, halo_ref[...])
        xp = jnp.concatenate([halo, x_ref[...]], axis=0)
        wv = w_ref[...]
        y = b_ref[...] + wv[0:1] * xp[HALO - 3:HALO - 3 + t]
        for j in range(1, 4):
            y = y + wv[j:j + 1] * xp[HALO - 3 + j:HALO - 3 + j + t]
        y_ref[...] = y

    return pl.pallas_call(
        body, grid=(s // t,),
        in_specs=[pl.BlockSpec((t, width), lambda i: (i, cb)),
                  pl.BlockSpec((HALO, width), lambda i: (jnp.maximum(i * hb - 1, 0), cb)),
                  _whole(w.shape), _whole(bias.shape)],
        out_specs=_rowspec(t, width), out_shape=jax.ShapeDtypeStruct((s, width), F32),
        name=name, compiler_params=_params(("arbitrary",)))(x, x, w, bias)


def _conv_bwd(name, dy, x, col0, width, w, t=512):
    s = x.shape[0]
    cb = col0 // width
    hb = t // HALO
    nblk = s // t

    def body(dy_ref, dyn_ref, x_ref, halo_ref, w_ref, dx_ref, dw_ref, db_ref):
        i = pl.program_id(0)
        dyv = dy_ref[...]
        nxt = jnp.where(i == nblk - 1, 0.0, dyn_ref[...])
        dyp = jnp.concatenate([dyv, nxt], axis=0)
        wv = w_ref[...]
        dx = wv[3:4] * dyv
        for j in range(3):
            dx = dx + wv[j:j + 1] * dyp[3 - j:3 - j + t]
        dx_ref[...] = dx.astype(dx_ref.dtype)
        halo = jnp.where(i == 0, 0.0, halo_ref[...])
        xp = jnp.concatenate([halo, x_ref[...]], axis=0)
        dw = jnp.concatenate(
            [jnp.sum(dyv * xp[HALO - 3 + j:HALO - 3 + j + t], axis=0, keepdims=True) for j in range(4)], axis=0)

        @pl.when(i == 0)
        def _():
            dw_ref[...] = jnp.zeros_like(dw_ref)
            db_ref[...] = jnp.zeros_like(db_ref)

        dw_ref[...] += dw
        db_ref[...] += jnp.sum(dyv, axis=0, keepdims=True)

    return pl.pallas_call(
        body, grid=(nblk,),
        in_specs=[_rowspec(t, width),
                  pl.BlockSpec((HALO, width), lambda i: (jnp.minimum((i + 1) * hb, s // HALO - 1), 0)),
                  pl.BlockSpec((t, width), lambda i: (i, cb)),
                  pl.BlockSpec((HALO, width), lambda i: (jnp.maximum(i * hb - 1, 0), cb)),
                  _whole(w.shape)],
        out_specs=[_rowspec(t, width), _whole((4, width)), _whole((1, width))],
        out_shape=[jax.ShapeDtypeStruct((s, width), BF16), jax.ShapeDtypeStruct((4, width), F32),
                   jax.ShapeDtypeStruct((1, width), F32)],
        name=name, compiler_params=_params(("arbitrary",)))(dy, dy, x, x, w)


def _hgrn_chunk(st, q, k, v, lf):
    cum = _running_sum(lf)
    tot = jnp.sum(lf, axis=0, keepdims=True)
    bm = _iota((A_W, A_W), 0) // A_HD == _iota((A_W, A_W), 1) // A_HD
    bo = bm.astype(F32)
    o_inter = _bdot(q * jnp.exp(cum), st, "nt")
    kd = k * jnp.exp(tot - cum)
    st_new = st * jnp.exp(tot) + jnp.where(bm, _bdot(v, kd, "tn"), 0.0)
    lane_h = _iota((SUB, A_W), 1) // A_HD
    n_h = A_W // A_HD
    t3 = _iota((SUB, SUB, A_W), 0)
    s3 = _iota((SUB, SUB, A_W), 1)
    outs = []
    for i in range(CHUNK // SUB):
        sl = slice(i * SUB, (i + 1) * SUB)
        qi, ki, vi, ci = q[sl], k[sl], v[sl], cum[sl]
        e = jnp.exp(jnp.minimum(ci[:, None, :] - ci[None, :, :], 0.0))
        d3 = jnp.where(s3 <= t3, qi[:, None, :] * ki[None, :, :] * e, 0.0)
        ds = _mdot(d3.reshape(SUB * SUB, A_W), bo, "nn", 1, 1, (True, False)).reshape(SUB, SUB, A_W)
        oi = jnp.sum(ds * vi[None, :, :], axis=1)
        if i > 0:
            n = i * SUB
            cb = cum[n - 1:n]
            qt = qi * jnp.exp(ci - cb)
            kt = k[:n] * jnp.exp(jnp.minimum(cb - cum[:n], 0.0))
            qs = jnp.concatenate([jnp.where(lane_h == h, qt, 0.0) for h in range(n_h)], axis=0)
            p = _bdot(_bdot(qs, kt, "nt"), v[:n])
            for h in range(n_h):
                oi = oi + jnp.where(lane_h == h, p[h * SUB:(h + 1) * SUB], 0.0)
        outs.append(oi)
    return st_new, o_inter + jnp.concatenate(outs, axis=0)


HGRN_CHUNKS = 2


def _hgrn_fwd(name, q, k, v, lf):
    s = q.shape[0]
    n = s // CHUNK
    g = HGRN_CHUNKS

    def body(q_ref, k_ref, v_ref, lf_ref, o_ref, sts_ref, st_ref):
        @pl.when(pl.program_id(0) == 0)
        def _():
            st_ref[...] = jnp.zeros_like(st_ref)

        st = st_ref[...]
        for c in range(g):
            rows = pl.ds(c * CHUNK, CHUNK)
            sts_ref[c] = st
            st, o = _hgrn_chunk(st, q_ref[rows], k_ref[rows], v_ref[rows], lf_ref[rows])
            o_ref[rows] = o
        st_ref[...] = st

    cs = _rowspec(g * CHUNK, A_W)
    return pl.pallas_call(
        body, grid=(n // g,), in_specs=[cs] * 4,
        out_specs=[cs, pl.BlockSpec((g, A_W, A_W), lambda i: (i, 0, 0))],
        out_shape=[jax.ShapeDtypeStruct((s, A_W), F32), jax.ShapeDtypeStruct((n, A_W, A_W), F32)],
        scratch_shapes=[pltpu.VMEM((A_W, A_W), F32)],
        name=name, compiler_params=_params(("arbitrary",)))(q, k, v, lf)


def _hgrn_bwd(name, q, k, v, lf, sts, do):
    s = q.shape[0]
    n = s // CHUNK
    g = HGRN_CHUNKS
    nsteps = n // g

    def body(q_ref, k_ref, v_ref, lf_ref, sts_ref, do_ref, dq_ref, dk_ref, dv_ref, dlf_ref, dst_ref):
        @pl.when(pl.program_id(0) == 0)
        def _():
            dst_ref[...] = jnp.zeros_like(dst_ref)

        dst = dst_ref[...]
        for c in reversed(range(g)):
            rows = pl.ds(c * CHUNK, CHUNK)
            _, vjp = jax.vjp(_hgrn_chunk, sts_ref[c], q_ref[rows], k_ref[rows], v_ref[rows], lf_ref[rows])
            ct = vjp((dst, do_ref[rows]))
            dst = ct[0]
            for r, gi in zip((dq_ref, dk_ref, dv_ref, dlf_ref), ct[1:]):
                r[rows] = gi
        dst_ref[...] = dst

    rs = pl.BlockSpec((g * CHUNK, A_W), lambda i: (nsteps - 1 - i, 0))
    return pl.pallas_call(
        body, grid=(nsteps,),
        in_specs=[rs] * 4 + [pl.BlockSpec((g, A_W, A_W), lambda i: (nsteps - 1 - i, 0, 0)), rs],
        out_specs=[rs] * 4, out_shape=[jax.ShapeDtypeStruct((s, A_W), F32)] * 4,
        scratch_shapes=[pltpu.VMEM((A_W, A_W), F32)],
        name=name, compiler_params=_params(("arbitrary",)))(q, k, v, lf, sts, do)


PREP_CHUNKS = 4
REC_CHUNKS = 4
DEC_ROWS = 8


def _to_heads(x, rows=CHUNK, width=B_HD):
    n = x.shape[0] // rows
    x3 = x.reshape(n, rows, x.shape[1])
    return jnp.concatenate([x3[:, :, h * B_HD:h * B_HD + width] for h in range(B_NH)], axis=0)


def _from_heads(y):
    n = y.shape[0] // B_NH
    x3 = jnp.concatenate([y[h * n:(h + 1) * n] for h in range(B_NH)], axis=2)
    return x3.reshape(n * y.shape[1], B_NH * B_HD)


@jax.custom_vjp
def _unit_lower_inverse(a_mat):
    nil = -a_mat
    eye = (_iota(a_mat.shape, 1) == _iota(a_mat.shape, 2)).astype(F32)
    inv = eye + nil
    sq = 2
    while sq < CHUNK:
        nil = _split_dot(nil, nil, "nn", 2, 2)
        inv = inv + _split_dot(inv, nil, "nn", 2, 2)
        sq *= 2
    return inv


def _unit_lower_inverse_fwd(a_mat):
    inv = _unit_lower_inverse(a_mat)
    return inv, inv


def _unit_lower_inverse_bwd(inv, ct):
    return (-_split_dot(_split_dot(inv, ct, "tn", 2, 2), inv, "nt", 2, 2),)


_unit_lower_inverse.defvjp(_unit_lower_inverse_fwd, _unit_lower_inverse_bwd)


def _gdn_prep(q, k, v, beta, la):
    q, k, v, beta, la = [_to_heads(z) for z in (q, k, v, beta, la)]
    nb = q.shape[0]
    r = _iota((nb, CHUNK, CHUNK), 1)
    c = _iota((nb, CHUNK, CHUNK), 2)
    causal = c <= r
    g = _running_sum(la)
    tot = jnp.sum(la, axis=1, keepdims=True)
    gcol = g[:, :, :CHUNK]
    decay = jnp.where(causal, jnp.exp(jnp.where(causal, gcol - jnp.swapaxes(gcol, 1, 2), 0.0)), 0.0)
    kb = k * beta
    a_mat = jnp.where(c < r, _bdot(kb, k, "nt") * decay, 0.0)
    inv = _unit_lower_inverse(a_mat)
    eg = jnp.exp(g)
    u = _bdot(inv, v * beta)
    w = _bdot(inv, kb * eg)
    qk = jnp.where(causal, _bdot(q, k, "nt") * decay, 0.0)
    qk = jnp.concatenate([qk, jnp.zeros((nb, CHUNK, B_HD - CHUNK), F32)], axis=2)
    dec = jnp.broadcast_to(jnp.exp(tot), (nb, DEC_ROWS, B_HD))
    return tuple(_from_heads(z) for z in (u, w, qk, q * eg, k * jnp.exp(tot - g), dec))


def _gdn_rec(st, u, w, qk, qd, kd, dec):
    u, w, qd, kd = [_to_heads(z) for z in (u, w, qd, kd)]
    qk = _to_heads(qk, width=CHUNK)
    v_new = u - _bdot(w, st)
    o = _bdot(qd, st) + _bdot(qk, v_new)
    st_new = st * _to_heads(dec, rows=DEC_ROWS)[:, :1, :1] + _bdot(kd, v_new, "tn")
    return st_new, _from_heads(o)


def _gdn_prep_fwd(name, q, k, v, beta, la):
    s = q.shape[0]
    t = PREP_CHUNKS * CHUNK
    td = PREP_CHUNKS * DEC_ROWS

    def body(*refs):
        outs = _gdn_prep(*[r[...] for r in refs[:5]])
        for o_ref, o in zip(refs[5:], outs):
            o_ref[...] = o

    return pl.pallas_call(
        body, grid=(s // t,), in_specs=[_rowspec(t, B_W)] * 5,
        out_specs=[_rowspec(t, B_W)] * 5 + [_rowspec(td, B_W)],
        out_shape=[jax.ShapeDtypeStruct((s, B_W), F32)] * 5 + [jax.ShapeDtypeStruct((s // CHUNK * DEC_ROWS, B_W), F32)],
        name=name, compiler_params=_params(("arbitrary",)))(q, k, v, beta, la)


def _gdn_prep_bwd(name, q, k, v, beta, la, cts):
    s = q.shape[0]
    t = PREP_CHUNKS * CHUNK
    td = PREP_CHUNKS * DEC_ROWS

    def body(*refs):
        _, vjp = jax.vjp(_gdn_prep, *[r[...] for r in refs[:5]])
        g = vjp(tuple(r[...] for r in refs[5:11]))
        for o_ref, o in zip(refs[11:], g):
            o_ref[...] = o

    return pl.pallas_call(
        body, grid=(s // t,), in_specs=[_rowspec(t, B_W)] * 10 + [_rowspec(td, B_W)],
        out_specs=[_rowspec(t, B_W)] * 5, out_shape=[jax.ShapeDtypeStruct((s, B_W), F32)] * 5,
        name=name, compiler_params=_params(("arbitrary",)))(q, k, v, beta, la, *cts)


def _gdn_rec_fwd(name, u, w, qk, qd, kd, dec):
    s = u.shape[0]
    n = s // CHUNK
    g = REC_CHUNKS

    def body(u_ref, w_ref, qk_ref, qd_ref, kd_ref, dec_ref, o_ref, sts_ref, st_ref):
        @pl.when(pl.program_id(0) == 0)
        def _():
            st_ref[...] = jnp.zeros_like(st_ref)

        st = st_ref[...]
        for c in range(g):
            rows, drows = pl.ds(c * CHUNK, CHUNK), pl.ds(c * DEC_ROWS, DEC_ROWS)
            sts_ref[c] = st
            st, o = _gdn_rec(st, u_ref[rows], w_ref[rows], qk_ref[rows], qd_ref[rows], kd_ref[rows], dec_ref[drows])
            o_ref[rows] = o
        st_ref[...] = st

    cs = _rowspec(g * CHUNK, B_W)
    return pl.pallas_call(
        body, grid=(n // g,), in_specs=[cs] * 5 + [_rowspec(g * DEC_ROWS, B_W)],
        out_specs=[cs, pl.BlockSpec((g, B_NH, B_HD, B_HD), lambda i: (i, 0, 0, 0))],
        out_shape=[jax.ShapeDtypeStruct((s, B_W), F32), jax.ShapeDtypeStruct((n, B_NH, B_HD, B_HD), F32)],
        scratch_shapes=[pltpu.VMEM((B_NH, B_HD, B_HD), F32)],
        name=name, compiler_params=_params(("arbitrary",)))(u, w, qk, qd, kd, dec)


def _gdn_rec_bwd(name, u, w, qk, qd, kd, dec, sts, do):
    s = u.shape[0]
    n = s // CHUNK
    g = REC_CHUNKS
    nsteps = n // g

    def body(u_ref, w_ref, qk_ref, qd_ref, kd_ref, dec_ref, sts_ref, do_ref,
             du_ref, dw_ref, dqk_ref, dqd_ref, dkd_ref, ddec_ref, dst_ref):
        @pl.when(pl.program_id(0) == 0)
        def _():
            dst_ref[...] = jnp.zeros_like(dst_ref)

        dst = dst_ref[...]
        for c in reversed(range(g)):
            rows, drows = pl.ds(c * CHUNK, CHUNK), pl.ds(c * DEC_ROWS, DEC_ROWS)
            _, vjp = jax.vjp(_gdn_rec, sts_ref[c], u_ref[rows], w_ref[rows], qk_ref[rows], qd_ref[rows], kd_ref[rows],
                             dec_ref[drows])
            ct = vjp((dst, do_ref[rows]))
            dst = ct[0]
            for r, gi in zip((du_ref, dw_ref, dqk_ref, dqd_ref, dkd_ref), ct[1:6]):
                r[rows] = gi
            ddec_ref[drows] = ct[6]
        dst_ref[...] = dst

    rs = pl.BlockSpec((g * CHUNK, B_W), lambda i: (nsteps - 1 - i, 0))
    ds = pl.BlockSpec((g * DEC_ROWS, B_W), lambda i: (nsteps - 1 - i, 0))
    return pl.pallas_call(
        body, grid=(nsteps,),
        in_specs=[rs] * 5 + [ds, pl.BlockSpec((g, B_NH, B_HD, B_HD), lambda i: (nsteps - 1 - i, 0, 0, 0)), rs],
        out_specs=[rs] * 5 + [ds],
        out_shape=[jax.ShapeDtypeStruct((s, B_W), F32)] * 5 + [jax.ShapeDtypeStruct((n * DEC_ROWS, B_W), F32)],
        scratch_shapes=[pltpu.VMEM((B_NH, B_HD, B_HD), F32)],
        name=name, compiler_params=_params(("arbitrary",)))(u, w, qk, qd, kd, dec, sts, do)


SCAN_T = 512


def _block_scan(a, b, reverse):
    t = a.shape[0]
    rows = _iota(a.shape, 0)
    sft = 1
    while sft < t:
        if reverse:
            a_s, b_s = pltpu.roll(a, t - sft, 0), pltpu.roll(b, t - sft, 0)
            edge = rows >= t - sft
        else:
            a_s, b_s = pltpu.roll(a, sft, 0), pltpu.roll(b, sft, 0)
            edge = rows < sft
        a_s = jnp.where(edge, 1.0, a_s)
        b_s = jnp.where(edge, 0.0, b_s)
        b = a * b_s + b
        a = a * a_s
        sft *= 2
    return a, b


def _lru_scan_fwd(name, a, b):
    s, w = a.shape
    t = SCAN_T

    def body(a_ref, b_ref, h_ref, hp_ref, carry):
        @pl.when(pl.program_id(0) == 0)
        def _():
            carry[...] = jnp.zeros_like(carry)

        h_in = carry[0:1]
        ca, cb = _block_scan(a_ref[...], b_ref[...], False)
        h = ca * h_in + cb
        h_ref[...] = h
        hp_ref[...] = jnp.where(_iota(h.shape, 0) == 0, h_in, pltpu.roll(h, 1, 0))
        carry[...] = jnp.broadcast_to(h[t - 1:t], carry.shape)

    return pl.pallas_call(
        body, grid=(s // t,), in_specs=[_rowspec(t, w)] * 2, out_specs=[_rowspec(t, w)] * 2,
        out_shape=[jax.ShapeDtypeStruct((s, w), F32)] * 2, scratch_shapes=[pltpu.VMEM((8, w), F32)],
        name=name, compiler_params=_params(("arbitrary",)))(a, b)


def _lru_scan_bwd(name, a, dh):
    s, w = a.shape
    t = SCAN_T
    n = s // t

    def body(a_ref, dh_ref, g_ref, a_next, g_next):
        @pl.when(pl.program_id(0) == 0)
        def _():
            a_next[...] = jnp.zeros_like(a_next)
            g_next[...] = jnp.zeros_like(g_next)

        av = a_ref[...]
        a_up = jnp.where(_iota(av.shape, 0) == t - 1, a_next[0:1], pltpu.roll(av, t - 1, 0))
        ca, cb = _block_scan(a_up, dh_ref[...], True)
        g = ca * g_next[0:1] + cb
        g_ref[...] = g
        a_next[...] = jnp.broadcast_to(av[0:1], a_next.shape)
        g_next[...] = jnp.broadcast_to(g[0:1], g_next.shape)

    rs = pl.BlockSpec((t, w), lambda i: (n - 1 - i, 0))
    return pl.pallas_call(
        body, grid=(n,), in_specs=[rs] * 2, out_specs=rs, out_shape=jax.ShapeDtypeStruct((s, w), F32),
        scratch_shapes=[pltpu.VMEM((8, w), F32), pltpu.VMEM((8, w), F32)],
        name=name, compiler_params=_params(("arbitrary",)))(a, dh)


def _loss_fn(g, x, tgt):
    y = x * lax.rsqrt(jnp.mean(x * x, axis=-1, keepdims=True) + EPS) * g
    err = y - tgt
    return 0.5 * jnp.sum(jnp.mean(err * err, axis=-1, keepdims=True), axis=0, keepdims=True)


def _loss_and_grad(name, g, x, tgt, t=256):
    s, d = x.shape

    def body(g_ref, x_ref, t_ref, loss_ref, dx_ref, dg_ref, dx16_ref):
        val, vjp = jax.vjp(_loss_fn, g_ref[...], x_ref[...], t_ref[...])
        dg, dx, _ = vjp(jnp.ones((1, 1), F32))

        @pl.when(pl.program_id(0) == 0)
        def _():
            loss_ref[...] = jnp.zeros_like(loss_ref)
            dg_ref[...] = jnp.zeros_like(dg_ref)

        loss_ref[...] += jnp.broadcast_to(val, loss_ref.shape)
        dg_ref[...] += dg
        dx_ref[...] = dx
        dx16_ref[...] = dx.astype(BF16)

    return pl.pallas_call(
        body, grid=(s // t,), in_specs=[_whole(g.shape), _rowspec(t, d), _rowspec(t, d)],
        out_specs=[_whole((1, 128)), _rowspec(t, d), _whole(g.shape), _rowspec(t, d)],
        out_shape=[jax.ShapeDtypeStruct((1, 128), F32), jax.ShapeDtypeStruct((s, d), F32),
                   jax.ShapeDtypeStruct(g.shape, F32), jax.ShapeDtypeStruct((s, d), BF16)],
        name=name, compiler_params=_params(("arbitrary",)))(g, x, tgt)


def _ew(name, fn, ins, n_out, t=None):
    r, c = ins[0].shape
    t = _tile(r, (512, 256, 128, 64, 32, 16, 8)) if t is None else t

    def body(*refs):
        outs = fn(*[x[...] for x in refs[:len(ins)]])
        for o_ref, o in zip(refs[len(ins):], outs):
            o_ref[...] = o

    return pl.pallas_call(
        body, grid=(r // t,), in_specs=[_rowspec(t, c)] * len(ins), out_specs=[_rowspec(t, c)] * n_out,
        out_shape=[jax.ShapeDtypeStruct((r, c), F32)] * n_out,
        name=name, compiler_params=_params(("arbitrary",)))(*ins)


def _adam_fn(w, g, m, v):
    m = B1 * m + (1.0 - B1) * g
    v = B2 * v + (1.0 - B2) * (g * g)
    m_hat = m / (1.0 - B1 ** STEP)
    v_hat = v / (1.0 - B2 ** STEP)
    delta = -LR * (m_hat / (jnp.sqrt(v_hat) + AEPS) + WD * w)
    return delta, m, v


def _add2_fn(a, b):
    return (a + b,)


def _adam(name, w, g, m, v):
    shp = w.shape
    two = lambda z: z.reshape(-1, shp[-1])
    outs = _ew(name, _adam_fn, [two(w), two(g), two(m), two(v)], 3)
    return [o.reshape(shp) for o in outs]


def _swap(name, bufs, flips):
    n = len(bufs)
    hbm = pl.BlockSpec(memory_space=pltpu.HBM)

    def body(*refs):
        srcs, dsts = refs[:n], refs[n:2 * n]
        send_sems, recv_sems = refs[2 * n], refs[2 * n + 1]
        me = (lax.axis_index("x"), lax.axis_index("y"), lax.axis_index("c"))
        copies = []
        for i in range(n):
            peer = tuple(1 - m if f else m for m, f in zip(me, flips[i]))
            cp = pltpu.make_async_remote_copy(src_ref=srcs[i], dst_ref=dsts[i], send_sem=send_sems.at[i],
                                              recv_sem=recv_sems.at[i], device_id=peer, device_id_type=MESH)
            cp.start()
            copies.append(cp)
        for cp in copies:
            cp.wait()

    return pl.pallas_call(
        body, in_specs=[hbm] * n, out_specs=[hbm] * n,
        out_shape=[jax.ShapeDtypeStruct(b.shape, b.dtype) for b in bufs],
        scratch_shapes=[pltpu.SemaphoreType.DMA((n,)), pltpu.SemaphoreType.DMA((n,))],
        name=name, compiler_params=pltpu.CompilerParams(has_side_effects=True))(*bufs)


FLIP_C = (False, False, True)
FLIP_Y = (False, True, False)
FLIP_X = (True, False, False)


LAYER_ROWS = PACK_ROWS // DEPTH
HALF_ROWS = LAYER_ROWS // 2
SUM_ROWS = 128
GATHER_IDS = (1, 2, 3, 4)
SCATTER_IDS = (5, 6, 7, 8)


def _mesh_pos():
    ix, iy, core = lax.axis_index("x"), lax.axis_index("y"), lax.axis_index("c")
    others = [(ix, 1 - iy), (1 - ix, iy), (1 - ix, 1 - iy)]
    return ix, iy, core, others


def _pack_big(w_in, w_out, w_up, w_down):
    pad = jnp.pad(w_in, ((0, 0), (0, 0), (0, 1024 - w_in.shape[-1])))
    return jnp.concatenate([pad, w_up, w_down, w_out], axis=1)


def _unpack_big(p):
    w_in = p[..., 0:1024, :D_IN // 4]
    return w_in, p[..., 3072:LAYER_ROWS, :], p[..., 1024:2048, :], p[..., 2048:3072, :]


def _gather_layer(packed, collective_id):
    src = jax.new_ref(packed, memory_space=pltpu.MemorySpace.HBM)
    out = jax.empty_ref(jax.ShapeDtypeStruct((4,) + packed.shape, packed.dtype), memory_space=pltpu.MemorySpace.HBM)
    dma = pltpu.SemaphoreType.DMA

    @pl.kernel(mesh=plsc.ScalarSubcoreMesh(axis_name="seq", num_cores=1), name="gather_%d" % collective_id,
               scratch_types=(dma,) * 12, compiler_params=pltpu.CompilerParams(collective_id=collective_id))
    def launch(*sems):
        send_sems, recv_sems = sems[:6], sems[6:]
        ix, iy, core, others = _mesh_pos()
        barrier = pltpu.get_barrier_semaphore()
        for peer in [(ox, oy, core) for ox, oy in others] + [(ix, iy, 1 - core)]:
            pl.semaphore_signal(barrier, inc=1, device_id=peer, device_id_type=MESH)
        pl.semaphore_wait(barrier, 4)
        mine = pl.ds(core * HALF_ROWS, HALF_ROWS)
        theirs = pl.ds((1 - core) * HALF_ROWS, HALF_ROWS)

        def copy(k, src_ref, owner, rows, to):
            return pltpu.make_async_remote_copy(
                src_ref=src_ref, dst_ref=out.at[owner, rows], send_sem=send_sems[k], recv_sem=recv_sems[k],
                device_id=to, device_id_type=MESH)

        chip = 2 * ix + iy
        first = [copy(j, src.at[mine], chip, mine, (*others[j], core)) for j in range(3)]
        for cp in first:
            cp.start()
        passed = []
        for j, (ox, oy) in enumerate(others):
            owner = 2 * ox + oy
            copy(j, src.at[mine], owner, mine, (ix, iy, core)).wait_recv()
            fwd = copy(3 + j, out.at[owner, mine], owner, mine, (ix, iy, 1 - core))
            fwd.start()
            passed.append(fwd)
        for j, (ox, oy) in enumerate(others):
            copy(3 + j, src.at[theirs], 2 * ox + oy, theirs, (ix, iy, core)).wait_recv()
        for cp in first + passed:
            cp.wait_send()

    launch()
    return out[...]


def _rs_sibling(per_chip):
    hbm = pl.BlockSpec(memory_space=pltpu.HBM)

    def body(src, out, send_sem, recv_sem):
        ix, iy, core, _ = _mesh_pos()
        give = pl.ds((1 - core) * HALF_ROWS, HALF_ROWS)
        cp = pltpu.make_async_remote_copy(src_ref=src.at[pl.ds(0, 4), give], dst_ref=out, send_sem=send_sem,
                                          recv_sem=recv_sem, device_id=(ix, iy, 1 - core), device_id_type=MESH)
        cp.start()
        cp.wait()

    return pl.pallas_call(
        body, in_specs=[hbm], out_specs=hbm, out_shape=jax.ShapeDtypeStruct((4, HALF_ROWS, 1024), F32),
        scratch_shapes=[pltpu.SemaphoreType.DMA, pltpu.SemaphoreType.DMA],
        name="rs_sibling", compiler_params=pltpu.CompilerParams(has_side_effects=True))(per_chip)


def _rs_add_pair(per_chip, sib, chip, core):
    nb = HALF_ROWS // SUM_ROWS
    blk = (1, SUM_ROWS, 1024)

    def body(chip_ref, core_ref, a_ref, b_ref, own_ref, send_ref):
        total = a_ref[0] + b_ref[0]
        send_ref[0] = total.astype(BF16)

        @pl.when(pl.program_id(1) == chip_ref[0])
        def _():
            own_ref[...] = total

    return pl.pallas_call(
        body,
        grid_spec=pltpu.PrefetchScalarGridSpec(
            num_scalar_prefetch=2, grid=(nb, 4),
            in_specs=[pl.BlockSpec(blk, lambda i, k, ch, co: (k, co[0] * nb + i, 0)),
                      pl.BlockSpec(blk, lambda i, k, ch, co: (k, i, 0))],
            out_specs=[pl.BlockSpec((SUM_ROWS, 1024), lambda i, k, ch, co: (i, 0)),
                       pl.BlockSpec(blk, lambda i, k, ch, co: (k, i, 0))]),
        out_shape=[jax.ShapeDtypeStruct((HALF_ROWS, 1024), F32), jax.ShapeDtypeStruct((4, HALF_ROWS, 1024), BF16)],
        name="rs_add_pair", compiler_params=_params(("arbitrary", "arbitrary")))(
            chip.reshape(1), core.reshape(1), per_chip, sib)


def _rs_chips(pair, collective_id):
    src_ref = jax.new_ref(pair, memory_space=pltpu.MemorySpace.HBM)
    got_ref = jax.empty_ref(jax.ShapeDtypeStruct((3,) + pair.shape[1:], pair.dtype), memory_space=pltpu.MemorySpace.HBM)
    dma = pltpu.SemaphoreType.DMA

    @pl.kernel(mesh=plsc.ScalarSubcoreMesh(axis_name="seq", num_cores=1), name="rs_chips_%d" % collective_id,
               scratch_types=(dma,) * 6, compiler_params=pltpu.CompilerParams(collective_id=collective_id))
    def launch(s0, s1, s2, r0, r1, r2):
        ix, iy, core, others = _mesh_pos()
        barrier = pltpu.get_barrier_semaphore()
        for ox, oy in others:
            pl.semaphore_signal(barrier, inc=1, device_id=(ox, oy, core), device_id_type=MESH)
        pl.semaphore_wait(barrier, 3)
        copies = []
        for j, ((ox, oy), send_sem, recv_sem) in enumerate(zip(others, (s0, s1, s2), (r0, r1, r2))):
            cp = pltpu.make_async_remote_copy(src_ref=src_ref.at[2 * ox + oy], dst_ref=got_ref.at[j], send_sem=send_sem,
                                              recv_sem=recv_sem, device_id=(ox, oy, core), device_id_type=MESH)
            cp.start()
            copies.append(cp)
        for cp in copies:
            cp.wait()

    launch()
    return got_ref[...]


def _rs_add_chips(own, got, core, layer, buf):
    nb = HALF_ROWS // SUM_ROWS
    blk = (1, SUM_ROWS, 1024)

    def body(core_ref, a_ref, g0_ref, g1_ref, g2_ref, *rest):
        o_ref = rest[-1]
        o_ref[0, 0] = ((a_ref[...] + g0_ref[0].astype(F32)) + g1_ref[0].astype(F32)) + g2_ref[0].astype(F32)

    aliased = [] if buf is None else [buf]
    return pl.pallas_call(
        body,
        grid_spec=pltpu.PrefetchScalarGridSpec(
            num_scalar_prefetch=1, grid=(nb,),
            in_specs=[pl.BlockSpec((SUM_ROWS, 1024), lambda i, c: (i, 0))]
            + [pl.BlockSpec(blk, functools.partial(lambda j, i, c: (j, i, 0), j)) for j in range(3)]
            + [pl.BlockSpec(memory_space=pl.ANY)] * len(aliased),
            out_specs=pl.BlockSpec((1, 1, SUM_ROWS, 1024), lambda i, c: (layer, c[0], i, 0))),
        out_shape=jax.ShapeDtypeStruct((DEPTH, 2, HALF_ROWS, 1024), F32),
        input_output_aliases={} if buf is None else {5: 0},
        name="rs_add_chips", compiler_params=_params(("arbitrary",)))(core.reshape(1), own, got, got, got, *aliased)


def _rs_share(buf):
    hbm = pl.BlockSpec(memory_space=pltpu.HBM)

    def body(src, out, send_sem, recv_sem):
        ix, iy, core, _ = _mesh_pos()
        layers = pl.ds(0, DEPTH)
        cp = pltpu.make_async_remote_copy(src_ref=src.at[layers, core], dst_ref=out.at[layers, core], send_sem=send_sem,
                                          recv_sem=recv_sem, device_id=(ix, iy, 1 - core), device_id_type=MESH)
        cp.start()
        pltpu.make_async_remote_copy(src_ref=src.at[layers, 1 - core], dst_ref=out.at[layers, 1 - core],
                                     send_sem=send_sem, recv_sem=recv_sem, device_id=(ix, iy, core),
                                     device_id_type=MESH).wait_recv()
        cp.wait_send()

    return pl.pallas_call(
        body, in_specs=[hbm], out_specs=hbm, out_shape=jax.ShapeDtypeStruct(buf.shape, buf.dtype),
        input_output_aliases={0: 0}, scratch_shapes=[pltpu.SemaphoreType.DMA, pltpu.SemaphoreType.DMA],
        name="rs_share", compiler_params=pltpu.CompilerParams(has_side_effects=True))(buf)


def _scatter_start(per_chip, chip, core, layer):
    own, pair = _rs_add_pair(per_chip, _rs_sibling(per_chip), chip, core)
    return own, _rs_chips(pair, SCATTER_IDS[layer])


def _allreduce_small(tag, buf):
    for name, flip in (("c", FLIP_C), ("y", FLIP_Y), ("x", FLIP_X)):
        (other,) = _swap(tag + "_swap_" + name, [buf], [flip])
        (buf,) = _ew(tag + "_add_" + name, _add2_fn, [buf, other], 1)
    return buf


def _permute_in_cols(w):
    z = jnp.zeros(w.shape[:-1] + (D_IN_PAD - D_IN,), w.dtype)
    return jnp.concatenate([w[..., 1024:3072], w[..., :1024], w[..., 3080:D_IN], w[..., 3072:3080], z], axis=-1)


IN_COL_SEGMENTS = ((0, 1024, 2048), (1024, 3072, -1024), (3072, 3080, GATE0 - 3072), (3080, D_IN, COL_CX - 3080))


def _orig_cols(w_pad, lo, hi):
    parts = []
    for first, last, shift in IN_COL_SEGMENTS:
        a, b = max(lo, first), min(hi, last)
        if a < b:
            parts.append(w_pad[..., a + shift:b + shift])
    return parts


def _block_diag(w):
    out = jnp.zeros((C_W, C_W), w.dtype)
    for n in range(4):
        out = lax.dynamic_update_slice(out, w[n], (64 * n, 64 * n))
    return out


def _diag_blocks(w):
    return jnp.stack([w[64 * n:64 * (n + 1), 64 * n:64 * (n + 1)] for n in range(4)], axis=0)


def _gate_row(v):
    return jnp.concatenate([jnp.zeros((4,), F32), v, jnp.zeros((120,), F32)]).reshape(1, 128)


SMALL = ("norm1_g", "hgrn_lb_logits", "hgrn_norm_g", "gdn_conv_w", "gdn_a_log", "gdn_dt_bias", "gdn_norm_g",
         "lru_conv_w", "lru_conv_b", "lru_w_a", "lru_b_a", "lru_w_x", "lru_b_x", "lru_lambda", "norm2_g",
         "final_norm_g")


def _flatten_small(tree):
    flat = jnp.concatenate([tree[k].reshape(-1) for k in sorted(tree)])
    rows = -(-flat.shape[0] // (512 * 128)) * 512
    return jnp.pad(flat, (0, rows * 128 - flat.shape[0])).reshape(rows, 128)


def _unflatten_small(buf, shapes):
    flat = buf.reshape(-1)
    out, off = {}, 0
    for k in sorted(shapes):
        size = 1
        for d in shapes[k]:
            size *= d
        out[k] = flat[off:off + size].reshape(shapes[k])
        off += size
    return out


def _layer_fwd(l, x, p, w_in, w_out, w_up, w_down):
    sv = {"x0": x}
    (h,) = _rowwise_fwd("norm1", _rmsnorm_fn, [p["norm1_g"]], [x], [D_MODEL], [BF16], 512)
    proj = _matmul("proj_in", h, w_in, "nn")
    sv["h"], sv["proj"] = h, proj
    aq, af, ag = _cols(proj, COL_AQ, A_W), _cols(proj, COL_AF, A_W), _cols(proj, COL_AG, A_W)
    ai = lax.slice_in_dim(proj, COL_AI, COL_AI + A_W, axis=1)
    bz, gate, cy = _cols(proj, COL_BZ, B_W), _cols(proj, GATE0, 128), _cols(proj, COL_CY, C_W)
    q, k, lf = _rowwise_fwd("hgrn_pre", functools.partial(_hgrn_pre_fn, l), [p["lb_logits"]], [aq, af],
                            [A_W] * 3, [F32] * 3, 512)
    o_a, sts_a = _hgrn_fwd("hgrn_chunk", q, k, ai, lf)
    (y_a,) = _rowwise_fwd("hgrn_post", _hgrn_post_fn, [p["hgrn_norm_g"]], [o_a, ag], [A_W], [BF16], 512)
    sv.update(aq=aq, af=af, ai=ai, ag=ag, hq=q, hk=k, hlf=lf, o_a=o_a, sts_a=sts_a)
    qkvc = _conv_fwd("gdn_conv", proj, 0, 3 * B_W, p["gdn_conv_w"], jnp.zeros((1, 3 * B_W), F32), 256)
    gq, gk, gv, gb, gla = _rowwise_fwd("gdn_pre", _gdn_pre_fn, [p["gdn_a_log"], p["gdn_dt_bias"]], [qkvc, gate],
                                       [B_W] * 5, [F32] * 5, 256)
    wy = _gdn_prep_fwd("gdn_prep", gq, gk, gv, gb, gla)
    o_b, sts_b = _gdn_rec_fwd("gdn_rec", *wy)
    sv["wy"] = wy
    (y_b,) = _rowwise_fwd("gdn_post", _gdn_post_fn, [p["gdn_norm_g"]], [o_b, bz], [B_W], [BF16], 512)
    sv.update(qkvc=qkvc, gate=gate, bz=bz, gq=gq, gk=gk, gv=gv, gb=gb, gla=gla, o_b=o_b, sts_b=sts_b)
    xc = _conv_fwd("lru_conv", proj, COL_CX, C_W, p["lru_conv_w"], p["lru_conv_b"], 512)
    a, b = _rowwise_fwd("lru_pre", _lru_pre_fn, [p["lru_w_a"], p["lru_b_a"], p["lru_w_x"], p["lru_b_x"], p["lru_lambda"]],
                        [xc], [C_W] * 2, [F32] * 2, 512)
    hs, h_prev = _lru_scan_fwd("lru_scan", a, b)
    (y_c,) = _rowwise_fwd("lru_post", _lru_post_fn, [], [hs, cy], [C_W], [BF16], 512)
    sv.update(xc=xc, la=a, hs=hs, h_prev=h_prev, cy=cy)
    mixed = jnp.concatenate([y_b, y_a, y_c], axis=1)
    x1 = _matmul("proj_out", mixed, w_out, "nn", res=x)
    (h2,) = _rowwise_fwd("norm2", _rmsnorm_fn, [p["norm2_g"]], [x1], [D_MODEL], [BF16], 512)
    up, act = _matmul("mlp_up", h2, w_up, "nn", also=_relu2)
    x2 = _matmul("mlp_down", act, w_down, "nn", res=x1)
    sv.update(mixed=mixed, x1=x1, h2=h2, up=up, act=act)
    return x2, sv


def _layer_bwd(l, dx, dx16, p, sv, w_in, w_out, w_up, w_down):
    gs = {}
    dup = _matmul("mlp_down_dx", dx16, w_down, "nt", epi=_drelu2, epi_in=sv["up"], out_dtype=BF16)
    d_w_down = _matmul("mlp_down_dw", sv["act"], dx16, "tn")
    dh2 = _matmul("mlp_up_dx", dup, w_up, "nt")
    d_w_up = _matmul("mlp_up_dw", sv["h2"], dup, "tn")
    (gs["norm2_g"],), (dx1, dx1_16) = _rowwise_bwd("norm2_bwd", _rmsnorm_fn, [p["norm2_g"]], [sv["x1"]], [dh2], 512,
                                                   res=dx, copy16=True)
    dmixed = _matmul("proj_out_dx", dx1_16, w_out, "nt")
    d_w_out = _matmul("proj_out_dw", sv["mixed"], dx1_16, "tn")
    dy_b, dy_a, dy_c = _cols(dmixed, 0, B_W), _cols(dmixed, B_W, A_W), _cols(dmixed, B_W + A_W, C_W)
    (gs["hgrn_norm_g"],), (do_a, dag) = _rowwise_bwd("hgrn_post_bwd", _hgrn_post_fn, [p["hgrn_norm_g"]],
                                                     [sv["o_a"], sv["ag"]], [dy_a], 512, dtypes=[F32, BF16])
    dq, dk, dai, dlf = _hgrn_bwd("hgrn_chunk_bwd", sv["hq"], sv["hk"], sv["ai"], sv["hlf"], sv["sts_a"], do_a)
    (gs["lb_logits"],), (daq, daf) = _rowwise_bwd("hgrn_pre_bwd", functools.partial(_hgrn_pre_fn, l), [p["lb_logits"]],
                                                  [sv["aq"], sv["af"]], [dq, dk, dlf], 512, dtypes=[BF16, BF16])
    (gs["gdn_norm_g"],), (do_b, dbz) = _rowwise_bwd("gdn_post_bwd", _gdn_post_fn, [p["gdn_norm_g"]],
                                                    [sv["o_b"], sv["bz"]], [dy_b], 512, dtypes=[F32, BF16])
    d_wy = _gdn_rec_bwd("gdn_rec_bwd", *sv["wy"], sv["sts_b"], do_b)
    dgq, dgk, dgv, dgb, dgla = _gdn_prep_bwd("gdn_prep_bwd", sv["gq"], sv["gk"], sv["gv"], sv["gb"], sv["gla"], d_wy)
    (gs["gdn_a_log"], gs["gdn_dt_bias"]), (dqkvc, dgate) = _rowwise_bwd(
        "gdn_pre_bwd", _gdn_pre_fn, [p["gdn_a_log"], p["gdn_dt_bias"]], [sv["qkvc"], sv["gate"]],
        [dgq, dgk, dgv, dgb, dgla], 256, dtypes=[F32, BF16])
    dqkv, gs["gdn_conv_w"], _ = _conv_bwd("gdn_conv_bwd", dqkvc, sv["proj"], 0, 3 * B_W, p["gdn_conv_w"], 256)
    _, (dhs, dcy) = _rowwise_bwd("lru_post_bwd", _lru_post_fn, [], [sv["hs"], sv["cy"]], [dy_c], 512, dtypes=[F32, BF16])
    g = _lru_scan_bwd("lru_scan_bwd", sv["la"], dhs)
    (da,) = _rowwise_fwd("lru_da", _lru_da_fn, [], [g, sv["h_prev"]], [C_W], [F32], 512)
    lru_params = [p["lru_w_a"], p["lru_b_a"], p["lru_w_x"], p["lru_b_x"], p["lru_lambda"]]
    dps, (dxc,) = _rowwise_bwd("lru_pre_bwd", _lru_pre_fn, lru_params, [sv["xc"]], [da, g], 512)
    gs["lru_w_a"], gs["lru_b_a"], gs["lru_w_x"], gs["lru_b_x"], gs["lru_lambda"] = dps
    dcx, gs["lru_conv_w"], gs["lru_conv_b"] = _conv_bwd("lru_conv_bwd", dxc, sv["proj"], COL_CX, C_W, p["lru_conv_w"], 512)
    s = dx.shape[0]
    dproj = jnp.concatenate([dqkv, dbz, daq, daf, dai.astype(BF16), dag, dcx, dcy, dgate,
                             jnp.zeros((s, D_IN_PAD - GATE0 - 128), BF16)], axis=1)
    dh = _matmul("proj_in_dx", dproj, w_in, "nt")
    d_w_in = _matmul("proj_in_dw", sv["h"], dproj, "tn")
    (gs["norm1_g"],), (dx0, dx0_16) = _rowwise_bwd("norm1_bwd", _rmsnorm_fn, [p["norm1_g"]], [sv["x0"]], [dh], 512,
                                                   res=dx1, copy16=True)
    return dx0, dx0_16, dict(w_in=d_w_in, w_out=d_w_out, w_up=d_w_up, w_down=d_w_down), gs


def _layer_params(l, lb_logits, hgrn_norm_g, gdn_conv_w, gdn_a_log, gdn_dt_bias, gdn_norm_g, lru_conv_w, lru_conv_b,
                  lru_w_a, lru_b_a, lru_w_x, lru_b_x, lru_lambda, norm1_g, norm2_g):
    row = lambda v: v.reshape(1, -1)
    return dict(
        norm1_g=row(norm1_g[l]), norm2_g=row(norm2_g[l]), lb_logits=lb_logits,
        hgrn_norm_g=row(jnp.tile(hgrn_norm_g[l], A_W // A_HD)),
        gdn_conv_w=gdn_conv_w[l], gdn_a_log=_gate_row(gdn_a_log[l]), gdn_dt_bias=_gate_row(gdn_dt_bias[l]),
        gdn_norm_g=row(jnp.tile(gdn_norm_g[l], B_NH)),
        lru_conv_w=lru_conv_w[l], lru_conv_b=row(lru_conv_b[l]), lru_w_a=_block_diag(lru_w_a[l]), lru_b_a=row(lru_b_a[l]),
        lru_w_x=_block_diag(lru_w_x[l]), lru_b_x=row(lru_b_x[l]), lru_lambda=row(lru_lambda[l]))


def _small_grads_to_reference(gs_layers, d_final_g, d_logits):
    st = lambda k, f=lambda z: z: jnp.stack([f(g[k]) for g in gs_layers], axis=0)
    vec = lambda z: z.reshape(-1)
    return dict(
        norm1_g=st("norm1_g", vec), norm2_g=st("norm2_g", vec), hgrn_lb_logits=d_logits,
        hgrn_norm_g=st("hgrn_norm_g", lambda z: z.reshape(A_W // A_HD, A_HD).sum(0)),
        gdn_conv_w=st("gdn_conv_w"), gdn_a_log=st("gdn_a_log", lambda z: z[0, 4:8]),
        gdn_dt_bias=st("gdn_dt_bias", lambda z: z[0, 4:8]),
        gdn_norm_g=st("gdn_norm_g", lambda z: z.reshape(B_NH, B_HD).sum(0)),
        lru_conv_w=st("lru_conv_w"), lru_conv_b=st("lru_conv_b", vec), lru_w_a=st("lru_w_a", _diag_blocks),
        lru_b_a=st("lru_b_a", vec), lru_w_x=st("lru_w_x", _diag_blocks), lru_b_x=st("lru_b_x", vec),
        lru_lambda=st("lru_lambda", vec), final_norm_g=d_final_g.reshape(-1))


def _local_step(x, tgt, small, w_in_full, w_out_full, w_up_full, w_down_full, on_layer_grads=None):
    layer_p = [_layer_params(l, small["hgrn_lb_logits"], small["hgrn_norm_g"], small["gdn_conv_w"], small["gdn_a_log"],
                             small["gdn_dt_bias"], small["gdn_norm_g"], small["lru_conv_w"], small["lru_conv_b"],
                             small["lru_w_a"], small["lru_b_a"], small["lru_w_x"], small["lru_b_x"], small["lru_lambda"],
                             small["norm1_g"], small["norm2_g"]) for l in range(DEPTH)]
    saved = []
    for l in range(DEPTH):
        x, sv = _layer_fwd(l, x, layer_p[l], w_in_full[l], w_out_full[l], w_up_full[l], w_down_full[l])
        saved.append(sv)
    loss, dx, d_final_g, dx16 = _loss_and_grad("final_loss", small["final_norm_g"].reshape(1, -1), x, tgt)
    big = [None] * DEPTH
    gs_layers = [None] * DEPTH
    d_logits = jnp.zeros_like(small["hgrn_lb_logits"])
    for l in reversed(range(DEPTH)):
        dx, dx16, big[l], gs_layers[l] = _layer_bwd(l, dx, dx16, layer_p[l], saved[l], w_in_full[l], w_out_full[l],
                                                    w_up_full[l], w_down_full[l])
        d_logits = d_logits + gs_layers[l]["lb_logits"]
        if on_layer_grads is not None:
            big[l] = on_layer_grads(l, big[l])
    return loss, dx, big, _small_grads_to_reference(gs_layers, d_final_g, d_logits)


def kernel(x, norm1_g, w_in, hgrn_lb_logits, hgrn_norm_g, gdn_conv_w, gdn_a_log, gdn_dt_bias, gdn_norm_g, lru_conv_w, lru_conv_b, lru_w_a, lru_b_a, lru_w_x, lru_b_x, lru_lambda, w_out, norm2_g, w_up, w_down, final_norm_g, loss_target, m_norm1_g, m_w_in, m_hgrn_lb_logits, m_hgrn_norm_g, m_gdn_conv_w, m_gdn_a_log, m_gdn_dt_bias, m_gdn_norm_g, m_lru_conv_w, m_lru_conv_b, m_lru_w_a, m_lru_b_a, m_lru_w_x, m_lru_b_x, m_lru_lambda, m_w_out, m_norm2_g, m_w_up, m_w_down, m_final_norm_g, v_norm1_g, v_w_in, v_hgrn_lb_logits, v_hgrn_norm_g, v_gdn_conv_w, v_gdn_a_log, v_gdn_dt_bias, v_gdn_norm_g, v_lru_conv_w, v_lru_conv_b, v_lru_w_a, v_lru_b_a, v_lru_w_x, v_lru_b_x, v_lru_lambda, v_w_out, v_norm2_g, v_w_up, v_w_down, v_final_norm_g):
    args = dict(locals())
    ix, iy, core = lax.axis_index("x"), lax.axis_index("y"), lax.axis_index("c")
    chip = 2 * ix + iy

    packed = _pack_big(w_in, w_out, w_up, w_down).astype(BF16)
    w_in_full, w_out_full, w_up_full, w_down_full = [], [], [], []
    for l in range(DEPTH):
        shares = lax.dynamic_update_slice(_gather_layer(packed[l], GATHER_IDS[l]), packed[l][None], (chip, 0, 0))
        parts = [_unpack_big(shares[k]) for k in range(4)]
        w_in_full.append(_permute_in_cols(jnp.concatenate([pt[0] for pt in parts], axis=-1)))
        w_out_full.append(jnp.concatenate([parts[k][1] for k in W_OUT_CHIP_ORDER], axis=0))
        w_up_full.append(jnp.concatenate([pt[2] for pt in parts], axis=-1))
        w_down_full.append(jnp.concatenate([pt[3] for pt in parts], axis=0))

    n_gc, n_lc = gdn_conv_w.shape[-1], lru_conv_w.shape[-1]
    conv_full = dict(
        gdn_conv_w=lax.dynamic_update_slice(jnp.zeros((DEPTH, 4, 4 * n_gc), F32), gdn_conv_w * 0.5, (0, 0, chip * n_gc)),
        lru_conv_w=lax.dynamic_update_slice(jnp.zeros((DEPTH, 4, 4 * n_lc), F32), lru_conv_w * 0.5, (0, 0, chip * n_lc)))
    conv_shapes = {k: v.shape for k, v in conv_full.items()}
    conv_full = _unflatten_small(_allreduce_small("convw", _flatten_small(conv_full)), conv_shapes)

    small = {k: args[k] for k in SMALL}
    small.update(conv_full)

    n_in = D_IN // 4

    def start_scatter(l, b):
        pieces = []
        for k in range(4):
            cols = jnp.concatenate(_orig_cols(b["w_in"], k * n_in, (k + 1) * n_in), axis=1)
            r0 = 256 * W_OUT_CHIP_ORDER.index(k)
            pieces += [jnp.pad(cols, ((0, 0), (0, 1024 - n_in))), b["w_up"][:, k * 1024:(k + 1) * 1024],
                       b["w_down"][k * 1024:(k + 1) * 1024], b["w_out"][r0:r0 + 256]]
        per_chip = jnp.concatenate(pieces, axis=0).reshape(4, LAYER_ROWS, 1024)
        return _scatter_start(per_chip, chip, core, l)

    loss, grad_x, started, sg = _local_step(x[0], loss_target[0], small, w_in_full, w_out_full, w_up_full, w_down_full,
                                            start_scatter)

    sg["loss"] = loss[0, :1]
    shapes = {k: v.shape for k, v in sg.items()}
    sg = _unflatten_small(_allreduce_small("smallgrad", _flatten_small(sg)), shapes)
    loss_out = sg.pop("loss")[0]

    buf = None
    for l in reversed(range(DEPTH)):
        own, got = started[l]
        buf = _rs_add_chips(own, got, core, l, buf)
    g_in, g_out, g_up, g_down = _unpack_big(_rs_share(buf).reshape(DEPTH, LAYER_ROWS, 1024))

    grads = dict(sg)
    grads["gdn_conv_w"] = lax.dynamic_slice_in_dim(sg["gdn_conv_w"], chip * n_gc, n_gc, axis=2)
    grads["lru_conv_w"] = lax.dynamic_slice_in_dim(sg["lru_conv_w"], chip * n_lc, n_lc, axis=2)
    grads.update(w_in=g_in, w_out=g_out, w_up=g_up, w_down=g_down)

    names = ['norm1_g', 'w_in', 'hgrn_lb_logits', 'hgrn_norm_g', 'gdn_conv_w', 'gdn_a_log', 'gdn_dt_bias', 'gdn_norm_g',
             'lru_conv_w', 'lru_conv_b', 'lru_w_a', 'lru_b_a', 'lru_w_x', 'lru_b_x', 'lru_lambda', 'w_out', 'norm2_g',
             'w_up', 'w_down', 'final_norm_g']
    big_names = ("w_in", "w_out", "w_up", "w_down")
    delta, new_m, new_v = {}, {}, {}
    for k in big_names:
        delta[k], new_m[k], new_v[k] = _adam("adam_" + k, args[k], grads[k], args["m_" + k], args["v_" + k])
    small_names = [k for k in names if k not in big_names]
    shapes = {k: args[k].shape for k in small_names}
    flat = [_flatten_small({k: src(k) for k in small_names})
            for src in (lambda k: args[k], lambda k: grads[k], lambda k: args["m_" + k], lambda k: args["v_" + k])]
    outs = _ew("adam_small", _adam_fn, flat, 3)
    for dst, o in zip((delta, new_m, new_v), outs):
        dst.update(_unflatten_small(o, shapes))
    return (loss_out, grad_x[None], *[grads[k] for k in names], *[delta[k] for k in names],
            *[new_m[k] for k in names], *[new_v[k] for k in names])
```

```python
import functools

import jax
import jax.numpy as jnp
from jax import lax
from jax.experimental import pallas as pl
from jax.experimental.pallas import tpu as pltpu
from jax.experimental.pallas import tpu_sc as plsc

F32 = jnp.float32
BF16 = jnp.bfloat16
MESH = pl.DeviceIdType.MESH

DEPTH = 4
D_MODEL = 1024
A_W, B_W, C_W = 256, 512, 256
A_HD, B_HD = 64, 128
B_NH = B_W // B_HD
D_IN = 3592
D_IN_PAD = 3840
COL_BZ, COL_AQ, COL_AF, COL_AI, COL_AG, COL_CX, COL_CY = 1536, 2048, 2304, 2560, 2816, 3072, 3328
W_OUT_CHIP_ORDER = (1, 2, 0, 3)
GATE0 = 3584
D_FF = 4096
CHUNK = 64
SUB = 16
EPS = 1e-6
TINY = 1e-30
RG_C = 8.0
LR, B1, B2, AEPS, WD, STEP = 0.001, 0.9, 0.999, 1e-08, 0.01, 10
VMEM_LIMIT = 56 * 1024 * 1024
PACK_ROWS = 13 * 1024

NN = ((1,), (0,))
NT = ((1,), (1,))
TN = ((0,), (0,))


DIMS = {"nn": NN, "nt": NT, "tn": TN}


def _split(a, n):
    parts, r = [], a
    for i in range(n):
        p = r.astype(BF16)
        parts.append(p)
        if i + 1 < n:
            r = r - p.astype(F32)
    return parts


def _split_dot(a, b, mode, na, nb):
    (ca,), (cb,) = DIMS[mode]
    dims = ((DIMS[mode]), ((), ())) if a.ndim == 2 else (((ca + 1,), (cb + 1,)), ((0,), (0,)))
    pa, pb = _split(a, na), _split(b, nb)
    acc = None
    for i in range(na):
        for j in range(nb):
            if i + j < max(na, nb):
                t = lax.dot_general(pa[i], pb[j], dims, preferred_element_type=F32)
                acc = t if acc is None else acc + t
    return acc


@functools.partial(jax.custom_vjp, nondiff_argnums=(2, 3, 4, 5))
def _mdot(a, b, mode, na, nb, need):
    return _split_dot(a, b, mode, na, nb)


def _mdot_fwd(a, b, mode, na, nb, need):
    return _split_dot(a, b, mode, na, nb), (a, b)


def _mdot_bwd(mode, na, nb, need, res, ct):
    a, b = res
    n = max(na, nb)
    da, db = jnp.zeros_like(a), jnp.zeros_like(b)
    if mode == "nn":
        if need[0]:
            da = _split_dot(ct, b, "nt", n, nb)
        if need[1]:
            db = _split_dot(a, ct, "tn", na, n)
    elif mode == "nt":
        if need[0]:
            da = _split_dot(ct, b, "nn", n, nb)
        if need[1]:
            db = _split_dot(ct, a, "tn", n, na)
    else:
        if need[0]:
            da = _split_dot(b, ct, "nt", nb, n)
        if need[1]:
            db = _split_dot(a, ct, "nn", na, n)
    return da, db


_mdot.defvjp(_mdot_fwd, _mdot_bwd)
BOTH = (True, True)


def _bdot(a, b, mode="nn"):
    return _mdot(a, b, mode, 1, 1, BOTH)


def _group_sum(x, ones):
    return _mdot(x, ones, "nn", 2, 1, (True, False))


def _running_sum(x):
    shape = x.shape[:-2] + (CHUNK, CHUNK)
    tri = (_iota(shape, len(shape) - 1) <= _iota(shape, len(shape) - 2)).astype(F32)
    return _mdot(tri, x, "nn", 1, 3, (False, True))


def _iota(shape, d):
    return lax.broadcasted_iota(jnp.int32, shape, d)


def _block_ones(width, group):
    return (_iota((width, width), 0) // group == _iota((width, width), 1) // group).astype(F32)


def _params(sem):
    return pltpu.CompilerParams(dimension_semantics=sem, vmem_limit_bytes=VMEM_LIMIT)


def _tile(dim, prefs):
    for p in prefs:
        if dim % p == 0:
            return p
    return dim


def _whole(shape):
    return pl.BlockSpec(shape, lambda i: (0,) * len(shape))


def _rowspec(t, w, col0=0):
    cb = col0 // w
    return pl.BlockSpec((t, w), lambda i: (i, cb))


def _cols(arr, col0, width):
    assert col0 % width == 0
    return (arr, col0, width)


def _row_operands(rows, t):
    arrs, specs = [], []
    for r in rows:
        arr, col0, w = r if isinstance(r, tuple) else (r, 0, r.shape[1])
        arrs.append(arr)
        specs.append(_rowspec(t, w, col0))
    return arrs, specs


def _rowwise_fwd(name, fn, params, rows, out_widths, out_dtypes, t):
    rows, row_specs = _row_operands(rows, t)
    s = rows[0].shape[0]
    n_p, n_r = len(params), len(rows)

    def body(*refs):
        p = [r[...] for r in refs[:n_p]]
        xs = [r[...].astype(F32) for r in refs[n_p:n_p + n_r]]
        outs = fn(pl.program_id(0) * t, *p, *xs)
        for o_ref, o in zip(refs[n_p + n_r:], outs):
            o_ref[...] = o.astype(o_ref.dtype)

    return pl.pallas_call(
        body, grid=(s // t,),
        in_specs=[_whole(p.shape) for p in params] + row_specs,
        out_specs=[_rowspec(t, w) for w in out_widths],
        out_shape=[jax.ShapeDtypeStruct((s, w), d) for w, d in zip(out_widths, out_dtypes)],
        name=name, compiler_params=_params(("arbitrary",)))(*params, *rows)


def _rowwise_bwd(name, fn, params, rows, douts, t, need=None, res=None, dtypes=None, copy16=False):
    rows, row_specs = _row_operands(rows, t)
    douts, dout_specs = _row_operands(douts, t)
    s = rows[0].shape[0]
    n_p, n_r, n_o = len(params), len(rows), len(douts)
    need = [True] * n_r if need is None else need
    n_res = 0 if res is None else 1

    def body(*refs):
        p = [r[...] for r in refs[:n_p]]
        xs = [r[...].astype(F32) for r in refs[n_p:n_p + n_r]]
        dys = [r[...].astype(F32) for r in refs[n_p + n_r:n_p + n_r + n_o]]
        k = n_p + n_r + n_o
        res_ref = refs[k] if n_res else None
        dp_refs = refs[k + n_res:k + n_res + n_p]
        dx_refs = refs[k + n_res + n_p:]
        row0 = pl.program_id(0) * t
        _, vjp = jax.vjp(lambda *a: tuple(fn(row0, *a)), *p, *xs)
        g = vjp(tuple(dys))

        @pl.when(pl.program_id(0) == 0)
        def _():
            for r in dp_refs:
                r[...] = jnp.zeros_like(r)

        for r, gp in zip(dp_refs, g[:n_p]):
            r[...] += gp
        gx = [gi for gi, nd in zip(g[n_p:], need) if nd]
        if n_res:
            gx[0] = gx[0] + res_ref[...]
        if copy16:
            gx.append(gx[0])
        for r, gi in zip(dx_refs, gx):
            r[...] = gi.astype(r.dtype)

    widths = [sp.block_shape[1] for sp, nd in zip(row_specs, need) if nd]
    dtypes = [F32] * len(widths) if dtypes is None else list(dtypes)
    if copy16:
        widths, dtypes = widths + widths[:1], dtypes + [BF16]
    outs = pl.pallas_call(
        body, grid=(s // t,),
        in_specs=([_whole(p.shape) for p in params] + row_specs + dout_specs
                  + ([_rowspec(t, res.shape[1])] if n_res else [])),
        out_specs=[_whole(p.shape) for p in params] + [_rowspec(t, w) for w in widths],
        out_shape=([jax.ShapeDtypeStruct(p.shape, F32) for p in params]
                   + [jax.ShapeDtypeStruct((s, w), d) for w, d in zip(widths, dtypes)]),
        name=name, compiler_params=_params(("arbitrary",)))(*params, *rows, *douts, *([res] if n_res else []))
    return list(outs[:n_p]), list(outs[n_p:])


def _rmsnorm_fn(row0, g, x):
    y = x * lax.rsqrt(jnp.mean(x * x, axis=-1, keepdims=True) + EPS)
    return (y * g,)


def _hgrn_pre_fn(layer, row0, logits, aq, af):
    m = jnp.max(logits, axis=0, keepdims=True)
    e = jnp.exp(logits - m)
    p = e / jnp.sum(e, axis=0, keepdims=True)
    if layer == 0:
        lb = jnp.zeros((1, A_W), F32)
    else:
        acc = p[1:2]
        for j in range(2, layer + 1):
            acc = acc + p[j:j + 1]
        lb = jnp.minimum(jnp.maximum(acc, 0.0), 1.0 - EPS)
    sig = jax.nn.sigmoid(af)
    f = lb + (1.0 - lb) * sig
    log_f = jnp.log(jnp.maximum(f, TINY))
    k = (1.0 - lb) * jax.nn.sigmoid(-af)
    return jax.nn.silu(aq), k, log_f


def _hgrn_post_fn(row0, g, o, ag):
    ms = _group_sum(o * o, _block_ones(A_W, A_HD)) * (1.0 / A_HD)
    return (o * lax.rsqrt(ms + EPS) * g * jax.nn.silu(ag),)


def _gdn_pre_fn(row0, a_log, dt_bias, qkvc, gate):
    act = jax.nn.silu(qkvc)
    q, k, v = act[:, :B_W], act[:, B_W:2 * B_W], act[:, 2 * B_W:]
    bo = _block_ones(B_W, B_HD)
    q = q * lax.rsqrt(_group_sum(q * q, bo) + EPS) * (B_HD ** -0.5)
    k = k * lax.rsqrt(_group_sum(k * k, bo) + EPS)
    beta = jax.nn.sigmoid(gate)
    log_alpha = -jnp.exp(a_log) * jax.nn.softplus(gate + dt_bias)
    t = gate.shape[0]
    spread = lambda z, lane0: jnp.concatenate(
        [jnp.broadcast_to(z[:, lane0 + h:lane0 + h + 1], (t, B_HD)) for h in range(B_NH)], axis=1)
    return q, k, v, spread(beta, 0), spread(log_alpha, B_NH)


def _gdn_post_fn(row0, g, o, bz):
    ms = _group_sum(o * o, _block_ones(B_W, B_HD)) * (1.0 / B_HD)
    return (o * lax.rsqrt(ms + EPS) * g * jax.nn.silu(bz),)


def _lru_pre_fn(row0, w_a, b_a, w_x, b_x, lam, xc):
    r = jax.nn.sigmoid(_mdot(xc, w_a, "nn", 2, 2, BOTH) + b_a)
    i = jax.nn.sigmoid(_mdot(xc, w_x, "nn", 2, 2, BOTH) + b_x)
    log_a = -RG_C * r * jax.nn.softplus(-lam)
    a = jnp.exp(log_a)
    one_minus = -jnp.tanh(log_a) * (a * a + 1.0)
    mult = jnp.sqrt(jnp.maximum(one_minus, EPS))
    pos = row0 + _iota(xc.shape, 0)
    mult = jnp.where(pos == 0, 1.0, mult)
    return a, mult * i * xc


def _lru_post_fn(row0, h, cy):
    return (jax.nn.gelu(cy, approximate=True) * h,)


def _lru_da_fn(row0, g, h_prev):
    return (g * h_prev,)


def _relu2(u):
    r = jnp.maximum(u, 0.0)
    return r * r


def _drelu2(acc, u):
    return acc * (2.0 * jnp.maximum(u, 0.0))


MM_VMEM_BUDGET = 44 * 1024 * 1024


def _mm_tiles(mode, m, n, k, a_bytes, b_bytes, n_extra):
    best = None
    for tk in {k, 4096, 2048, 1024, 512, 256}:
        for tn in {n, 1280, 1024, 768, 512, 256, 128}:
            for tm in {m, 1024, 512, 256, 128}:
                if m % tm or n % tn or k % tk or tm > 1024:
                    continue
                tiles = tm * tk * a_bytes + tk * tn * b_bytes + (1 + n_extra) * tm * tn * 4
                if 2 * tiles + (tm * tn * 4 if tk < k else 0) > MM_VMEM_BUDGET:
                    continue
                a_reads = 1 if tk == k else n // tn
                b_reads = 1 if (tk == k and tn == n) else m // tm
                cost = m * k * a_bytes * a_reads + k * n * b_bytes * b_reads + m * n * 4 * (k // tk - 1)
                if best is None or (cost, -tm) < best[0]:
                    best = ((cost, -tm), (tm, tn, tk))
    if best is None:
        raise ValueError("no matmul tiling fits VMEM")
    return best[1]


def _matmul(name, a, b, mode, a_fn=None, res=None, epi=None, epi_in=None, out_dtype=F32, also=None):
    if mode == "nn":
        (m, k), n = a.shape, b.shape[1]
    elif mode == "nt":
        (m, k), n = a.shape, b.shape[0]
    else:
        (k, m), n = a.shape, b.shape[1]
    n_extra = (res is not None) + (epi_in is not None) + (also is not None)
    tm, tn, tk = _mm_tiles(mode, m, n, k, a.dtype.itemsize, b.dtype.itemsize, n_extra)
    nk = k // tk
    a_spec = {"nn": pl.BlockSpec((tm, tk), lambda i, j, kk: (i, kk)),
              "nt": pl.BlockSpec((tm, tk), lambda i, j, kk: (i, kk)),
              "tn": pl.BlockSpec((tk, tm), lambda i, j, kk: (kk, i))}[mode]
    b_spec = {"nn": pl.BlockSpec((tk, tn), lambda i, j, kk: (kk, j)),
              "nt": pl.BlockSpec((tn, tk), lambda i, j, kk: (j, kk)),
              "tn": pl.BlockSpec((tk, tn), lambda i, j, kk: (kk, j))}[mode]
    o_spec = pl.BlockSpec((tm, tn), lambda i, j, kk: (i, j))
    dims = {"nn": NN, "nt": NT, "tn": TN}[mode]
    extra = [x for x in (res, epi_in) if x is not None]

    def body(*refs):
        a_ref, b_ref = refs[0], refs[1]
        rest = list(refs[2:])
        res_ref = rest.pop(0) if res is not None else None
        epi_ref = rest.pop(0) if epi_in is not None else None
        o_ref = rest.pop(0)
        also_ref = rest.pop(0) if also is not None else None
        acc = rest[0] if nk > 1 else None
        kk = pl.program_id(2)
        at = a_ref[...]
        if a_fn is not None:
            at = a_fn(at.astype(F32))
        part = lax.dot_general(at.astype(BF16), b_ref[...].astype(BF16), (dims, ((), ())),
                               preferred_element_type=F32)

        def finish(out):
            if res_ref is not None:
                out = out + res_ref[...]
            if epi is not None:
                out = epi(out, epi_ref[...])
            o_ref[...] = out.astype(o_ref.dtype)
            if also is not None:
                also_ref[...] = also(out).astype(also_ref.dtype)

        if nk == 1:
            finish(part)
        else:
            @pl.when(kk == 0)
            def _():
                acc[...] = part

            @pl.when(jnp.logical_and(kk > 0, kk < nk - 1))
            def _():
                acc[...] += part

            @pl.when(kk == nk - 1)
            def _():
                finish(acc[...] + part)

    out_shape = [jax.ShapeDtypeStruct((m, n), out_dtype)] + ([jax.ShapeDtypeStruct((m, n), BF16)] if also else [])
    outs = pl.pallas_call(
        body, grid=(m // tm, n // tn, nk),
        in_specs=[a_spec, b_spec] + [o_spec] * len(extra), out_specs=[o_spec] * len(out_shape), out_shape=out_shape,
        scratch_shapes=[pltpu.VMEM((tm, tn), F32)] if nk > 1 else [],
        name=name, compiler_params=_params(("arbitrary", "arbitrary", "arbitrary")))(a, b, *extra)
    return outs if also else outs[0]


HALO = 8


def _conv_fwd(name, x, col0, width, w, bias, t=512):
    s = x.shape[0]
    cb = col0 // width
    hb = t // HALO

    def body(x_ref, halo_ref, w_ref, b_ref, y_ref):
        i = pl.program_id(0)
        halo = jnp.where(i == 0, 0.0, halo_ref[...])
        xp = jnp.concatenate([halo, x_ref[...]], axis=0)
        wv = w_ref[...]
        y = b_ref[...] + wv[0:1] * xp[HALO - 3:HALO - 3 + t]
        for j in range(1, 4):
            y = y + wv[j:j + 1] * xp[HALO - 3 + j:HALO - 3 + j + t]
        y_ref[...] = y

    return pl.pallas_call(
        body, grid=(s // t,),
        in_specs=[pl.BlockSpec((t, width), lambda i: (i, cb)),
                  pl.BlockSpec((HALO, width), lambda i: (jnp.maximum(i * hb - 1, 0), cb)),
                  _whole(w.shape), _whole(bias.shape)],
        out_specs=_rowspec(t, width), out_shape=jax.ShapeDtypeStruct((s, width), F32),
        name=name, compiler_params=_params(("arbitrary",)))(x, x, w, bias)


def _conv_bwd(name, dy, x, col0, width, w, t=512):
    s = x.shape[0]
    cb = col0 // width
    hb = t // HALO
    nblk = s // t

    def body(dy_ref, dyn_ref, x_ref, halo_ref, w_ref, dx_ref, dw_ref, db_ref):
        i = pl.program_id(0)
        dyv = dy_ref[...]
        nxt = jnp.where(i == nblk - 1, 0.0, dyn_ref[...])
        dyp = jnp.concatenate([dyv, nxt], axis=0)
        wv = w_ref[...]
        dx = wv[3:4] * dyv
        for j in range(3):
            dx = dx + wv[j:j + 1] * dyp[3 - j:3 - j + t]
        dx_ref[...] = dx.astype(dx_ref.dtype)
        halo = jnp.where(i == 0, 0.0, halo_ref[...])
        xp = jnp.concatenate([halo, x_ref[...]], axis=0)
        dw = jnp.concatenate(
            [jnp.sum(dyv * xp[HALO - 3 + j:HALO - 3 + j + t], axis=0, keepdims=True) for j in range(4)], axis=0)

        @pl.when(i == 0)
        def _():
            dw_ref[...] = jnp.zeros_like(dw_ref)
            db_ref[...] = jnp.zeros_like(db_ref)

        dw_ref[...] += dw
        db_ref[...] += jnp.sum(dyv, axis=0, keepdims=True)

    return pl.pallas_call(
        body, grid=(nblk,),
        in_specs=[_rowspec(t, width),
                  pl.BlockSpec((HALO, width), lambda i: (jnp.minimum((i + 1) * hb, s // HALO - 1), 0)),
                  pl.BlockSpec((t, width), lambda i: (i, cb)),
                  pl.BlockSpec((HALO, width), lambda i: (jnp.maximum(i * hb - 1, 0), cb)),
                  _whole(w.shape)],
        out_specs=[_rowspec(t, width), _whole((4, width)), _whole((1, width))],
        out_shape=[jax.ShapeDtypeStruct((s, width), BF16), jax.ShapeDtypeStruct((4, width), F32),
                   jax.ShapeDtypeStruct((1, width), F32)],
        name=name, compiler_params=_params(("arbitrary",)))(dy, dy, x, x, w)


def _hgrn_chunk(st, q, k, v, lf):
    cum = _running_sum(lf)
    tot = jnp.sum(lf, axis=0, keepdims=True)
    bm = _iota((A_W, A_W), 0) // A_HD == _iota((A_W, A_W), 1) // A_HD
    bo = bm.astype(F32)
    o_inter = _bdot(q * jnp.exp(cum), st, "nt")
    kd = k * jnp.exp(tot - cum)
    st_new = st * jnp.exp(tot) + jnp.where(bm, _bdot(v, kd, "tn"), 0.0)
    lane_h = _iota((SUB, A_W), 1) // A_HD
    n_h = A_W // A_HD
    t3 = _iota((SUB, SUB, A_W), 0)
    s3 = _iota((SUB, SUB, A_W), 1)
    outs = []
    for i in range(CHUNK // SUB):
        sl = slice(i * SUB, (i + 1) * SUB)
        qi, ki, vi, ci = q[sl], k[sl], v[sl], cum[sl]
        e = jnp.exp(jnp.minimum(ci[:, None, :] - ci[None, :, :], 0.0))
        d3 = jnp.where(s3 <= t3, qi[:, None, :] * ki[None, :, :] * e, 0.0)
        ds = _mdot(d3.reshape(SUB * SUB, A_W), bo, "nn", 1, 1, (True, False)).reshape(SUB, SUB, A_W)
        oi = jnp.sum(ds * vi[None, :, :], axis=1)
        if i > 0:
            n = i * SUB
            cb = cum[n - 1:n]
            qt = qi * jnp.exp(ci - cb)
            kt = k[:n] * jnp.exp(jnp.minimum(cb - cum[:n], 0.0))
            qs = jnp.concatenate([jnp.where(lane_h == h, qt, 0.0) for h in range(n_h)], axis=0)
            p = _bdot(_bdot(qs, kt, "nt"), v[:n])
            for h in range(n_h):
                oi = oi + jnp.where(lane_h == h, p[h * SUB:(h + 1) * SUB], 0.0)
        outs.append(oi)
    return st_new, o_inter + jnp.concatenate(outs, axis=0)


HGRN_CHUNKS = 2


def _hgrn_fwd(name, q, k, v, lf):
    s = q.shape[0]
    n = s // CHUNK
    g = HGRN_CHUNKS

    def body(q_ref, k_ref, v_ref, lf_ref, o_ref, sts_ref, st_ref):
        @pl.when(pl.program_id(0) == 0)
        def _():
            st_ref[...] = jnp.zeros_like(st_ref)

        st = st_ref[...]
        for c in range(g):
            rows = pl.ds(c * CHUNK, CHUNK)
            sts_ref[c] = st
            st, o = _hgrn_chunk(st, q_ref[rows], k_ref[rows], v_ref[rows], lf_ref[rows])
            o_ref[rows] = o
        st_ref[...] = st

    cs = _rowspec(g * CHUNK, A_W)
    return pl.pallas_call(
        body, grid=(n // g,), in_specs=[cs] * 4,
        out_specs=[cs, pl.BlockSpec((g, A_W, A_W), lambda i: (i, 0, 0))],
        out_shape=[jax.ShapeDtypeStruct((s, A_W), F32), jax.ShapeDtypeStruct((n, A_W, A_W), F32)],
        scratch_shapes=[pltpu.VMEM((A_W, A_W), F32)],
        name=name, compiler_params=_params(("arbitrary",)))(q, k, v, lf)


def _hgrn_bwd(name, q, k, v, lf, sts, do):
    s = q.shape[0]
    n = s // CHUNK
    g = HGRN_CHUNKS
    nsteps = n // g

    def body(q_ref, k_ref, v_ref, lf_ref, sts_ref, do_ref, dq_ref, dk_ref, dv_ref, dlf_ref, dst_ref):
        @pl.when(pl.program_id(0) == 0)
        def _():
            dst_ref[...] = jnp.zeros_like(dst_ref)

        dst = dst_ref[...]
        for c in reversed(range(g)):
            rows = pl.ds(c * CHUNK, CHUNK)
            _, vjp = jax.vjp(_hgrn_chunk, sts_ref[c], q_ref[rows], k_ref[rows], v_ref[rows], lf_ref[rows])
            ct = vjp((dst, do_ref[rows]))
            dst = ct[0]
            for r, gi in zip((dq_ref, dk_ref, dv_ref, dlf_ref), ct[1:]):
                r[rows] = gi
        dst_ref[...] = dst

    rs = pl.BlockSpec((g * CHUNK, A_W), lambda i: (nsteps - 1 - i, 0))
    return pl.pallas_call(
        body, grid=(nsteps,),
        in_specs=[rs] * 4 + [pl.BlockSpec((g, A_W, A_W), lambda i: (nsteps - 1 - i, 0, 0)), rs],
        out_specs=[rs] * 4, out_shape=[jax.ShapeDtypeStruct((s, A_W), F32)] * 4,
        scratch_shapes=[pltpu.VMEM((A_W, A_W), F32)],
        name=name, compiler_params=_params(("arbitrary",)))(q, k, v, lf, sts, do)


PREP_CHUNKS = 4
REC_CHUNKS = 4
DEC_ROWS = 8


def _to_heads(x, rows=CHUNK, width=B_HD):
    n = x.shape[0] // rows
    x3 = x.reshape(n, rows, x.shape[1])
    return jnp.concatenate([x3[:, :, h * B_HD:h * B_HD + width] for h in range(B_NH)], axis=0)


def _from_heads(y):
    n = y.shape[0] // B_NH
    x3 = jnp.concatenate([y[h * n:(h + 1) * n] for h in range(B_NH)], axis=2)
    return x3.reshape(n * y.shape[1], B_NH * B_HD)


@jax.custom_vjp
def _unit_lower_inverse(a_mat):
    nil = -a_mat
    eye = (_iota(a_mat.shape, 1) == _iota(a_mat.shape, 2)).astype(F32)
    inv = eye + nil
    sq = 2
    while sq < CHUNK:
        nil = _split_dot(nil, nil, "nn", 2, 2)
        inv = inv + _split_dot(inv, nil, "nn", 2, 2)
        sq *= 2
    return inv


def _unit_lower_inverse_fwd(a_mat):
    inv = _unit_lower_inverse(a_mat)
    return inv, inv


def _unit_lower_inverse_bwd(inv, ct):
    return (-_split_dot(_split_dot(inv, ct, "tn", 2, 2), inv, "nt", 2, 2),)


_unit_lower_inverse.defvjp(_unit_lower_inverse_fwd, _unit_lower_inverse_bwd)


def _gdn_prep(q, k, v, beta, la):
    q, k, v, beta, la = [_to_heads(z) for z in (q, k, v, beta, la)]
    nb = q.shape[0]
    r = _iota((nb, CHUNK, CHUNK), 1)
    c = _iota((nb, CHUNK, CHUNK), 2)
    causal = c <= r
    g = _running_sum(la)
    tot = jnp.sum(la, axis=1, keepdims=True)
    gcol = g[:, :, :CHUNK]
    decay = jnp.where(causal, jnp.exp(jnp.where(causal, gcol - jnp.swapaxes(gcol, 1, 2), 0.0)), 0.0)
    kb = k * beta
    a_mat = jnp.where(c < r, _bdot(kb, k, "nt") * decay, 0.0)
    inv = _unit_lower_inverse(a_mat)
    eg = jnp.exp(g)
    u = _bdot(inv, v * beta)
    w = _bdot(inv, kb * eg)
    qk = jnp.where(causal, _bdot(q, k, "nt") * decay, 0.0)
    qk = jnp.concatenate([qk, jnp.zeros((nb, CHUNK, B_HD - CHUNK), F32)], axis=2)
    dec = jnp.broadcast_to(jnp.exp(tot), (nb, DEC_ROWS, B_HD))
    return tuple(_from_heads(z) for z in (u, w, qk, q * eg, k * jnp.exp(tot - g), dec))


def _gdn_rec(st, u, w, qk, qd, kd, dec):
    u, w, qd, kd = [_to_heads(z) for z in (u, w, qd, kd)]
    qk = _to_heads(qk, width=CHUNK)
    v_new = u - _bdot(w, st)
    o = _bdot(qd, st) + _bdot(qk, v_new)
    st_new = st * _to_heads(dec, rows=DEC_ROWS)[:, :1, :1] + _bdot(kd, v_new, "tn")
    return st_new, _from_heads(o)


def _gdn_prep_fwd(name, q, k, v, beta, la):
    s = q.shape[0]
    t = PREP_CHUNKS * CHUNK
    td = PREP_CHUNKS * DEC_ROWS

    def body(*refs):
        outs = _gdn_prep(*[r[...] for r in refs[:5]])
        for o_ref, o in zip(refs[5:], outs):
            o_ref[...] = o

    return pl.pallas_call(
        body, grid=(s // t,), in_specs=[_rowspec(t, B_W)] * 5,
        out_specs=[_rowspec(t, B_W)] * 5 + [_rowspec(td, B_W)],
        out_shape=[jax.ShapeDtypeStruct((s, B_W), F32)] * 5 + [jax.ShapeDtypeStruct((s // CHUNK * DEC_ROWS, B_W), F32)],
        name=name, compiler_params=_params(("arbitrary",)))(q, k, v, beta, la)


def _gdn_prep_bwd(name, q, k, v, beta, la, cts):
    s = q.shape[0]
    t = PREP_CHUNKS * CHUNK
    td = PREP_CHUNKS * DEC_ROWS

    def body(*refs):
        _, vjp = jax.vjp(_gdn_prep, *[r[...] for r in refs[:5]])
        g = vjp(tuple(r[...] for r in refs[5:11]))
        for o_ref, o in zip(refs[11:], g):
            o_ref[...] = o

    return pl.pallas_call(
        body, grid=(s // t,), in_specs=[_rowspec(t, B_W)] * 10 + [_rowspec(td, B_W)],
        out_specs=[_rowspec(t, B_W)] * 5, out_shape=[jax.ShapeDtypeStruct((s, B_W), F32)] * 5,
        name=name, compiler_params=_params(("arbitrary",)))(q, k, v, beta, la, *cts)


def _gdn_rec_fwd(name, u, w, qk, qd, kd, dec):
    s = u.shape[0]
    n = s // CHUNK
    g = REC_CHUNKS

    def body(u_ref, w_ref, qk_ref, qd_ref, kd_ref, dec_ref, o_ref, sts_ref, st_ref):
        @pl.when(pl.program_id(0) == 0)
        def _():
            st_ref[...] = jnp.zeros_like(st_ref)

        st = st_ref[...]
        for c in range(g):
            rows, drows = pl.ds(c * CHUNK, CHUNK), pl.ds(c * DEC_ROWS, DEC_ROWS)
            sts_ref[c] = st
            st, o = _gdn_rec(st, u_ref[rows], w_ref[rows], qk_ref[rows], qd_ref[rows], kd_ref[rows], dec_ref[drows])
            o_ref[rows] = o
        st_ref[...] = st

    cs = _rowspec(g * CHUNK, B_W)
    return pl.pallas_call(
        body, grid=(n // g,), in_specs=[cs] * 5 + [_rowspec(g * DEC_ROWS, B_W)],
        out_specs=[cs, pl.BlockSpec((g, B_NH, B_HD, B_HD), lambda i: (i, 0, 0, 0))],
        out_shape=[jax.ShapeDtypeStruct((s, B_W), F32), jax.ShapeDtypeStruct((n, B_NH, B_HD, B_HD), F32)],
        scratch_shapes=[pltpu.VMEM((B_NH, B_HD, B_HD), F32)],
        name=name, compiler_params=_params(("arbitrary",)))(u, w, qk, qd, kd, dec)


def _gdn_rec_bwd(name, u, w, qk, qd, kd, dec, sts, do):
    s = u.shape[0]
    n = s // CHUNK
    g = REC_CHUNKS
    nsteps = n // g

    def body(u_ref, w_ref, qk_ref, qd_ref, kd_ref, dec_ref, sts_ref, do_ref,
             du_ref, dw_ref, dqk_ref, dqd_ref, dkd_ref, ddec_ref, dst_ref):
        @pl.when(pl.program_id(0) == 0)
        def _():
            dst_ref[...] = jnp.zeros_like(dst_ref)

        dst = dst_ref[...]
        for c in reversed(range(g)):
            rows, drows = pl.ds(c * CHUNK, CHUNK), pl.ds(c * DEC_ROWS, DEC_ROWS)
            _, vjp = jax.vjp(_gdn_rec, sts_ref[c], u_ref[rows], w_ref[rows], qk_ref[rows], qd_ref[rows], kd_ref[rows],
                             dec_ref[drows])
            ct = vjp((dst, do_ref[rows]))
            dst = ct[0]
            for r, gi in zip((du_ref, dw_ref, dqk_ref, dqd_ref, dkd_ref), ct[1:6]):
                r[rows] = gi
            ddec_ref[drows] = ct[6]
        dst_ref[...] = dst

    rs = pl.BlockSpec((g * CHUNK, B_W), lambda i: (nsteps - 1 - i, 0))
    ds = pl.BlockSpec((g * DEC_ROWS, B_W), lambda i: (nsteps - 1 - i, 0))
    return pl.pallas_call(
        body, grid=(nsteps,),
        in_specs=[rs] * 5 + [ds, pl.BlockSpec((g, B_NH, B_HD, B_HD), lambda i: (nsteps - 1 - i, 0, 0, 0)), rs],
        out_specs=[rs] * 5 + [ds],
        out_shape=[jax.ShapeDtypeStruct((s, B_W), F32)] * 5 + [jax.ShapeDtypeStruct((n * DEC_ROWS, B_W), F32)],
        scratch_shapes=[pltpu.VMEM((B_NH, B_HD, B_HD), F32)],
        name=name, compiler_params=_params(("arbitrary",)))(u, w, qk, qd, kd, dec, sts, do)


SCAN_T = 512


def _block_scan(a, b, reverse):
    t = a.shape[0]
    rows = _iota(a.shape, 0)
    sft = 1
    while sft < t:
        if reverse:
            a_s, b_s = pltpu.roll(a, t - sft, 0), pltpu.roll(b, t - sft, 0)
            edge = rows >= t - sft
        else:
            a_s, b_s = pltpu.roll(a, sft, 0), pltpu.roll(b, sft, 0)
            edge = rows < sft
        a_s = jnp.where(edge, 1.0, a_s)
        b_s = jnp.where(edge, 0.0, b_s)
        b = a * b_s + b
        a = a * a_s
        sft *= 2
    return a, b


def _lru_scan_fwd(name, a, b):
    s, w = a.shape
    t = SCAN_T

    def body(a_ref, b_ref, h_ref, hp_ref, carry):
        @pl.when(pl.program_id(0) == 0)
        def _():
            carry[...] = jnp.zeros_like(carry)

        h_in = carry[0:1]
        ca, cb = _block_scan(a_ref[...], b_ref[...], False)
        h = ca * h_in + cb
        h_ref[...] = h
        hp_ref[...] = jnp.where(_iota(h.shape, 0) == 0, h_in, pltpu.roll(h, 1, 0))
        carry[...] = jnp.broadcast_to(h[t - 1:t], carry.shape)

    return pl.pallas_call(
        body, grid=(s // t,), in_specs=[_rowspec(t, w)] * 2, out_specs=[_rowspec(t, w)] * 2,
        out_shape=[jax.ShapeDtypeStruct((s, w), F32)] * 2, scratch_shapes=[pltpu.VMEM((8, w), F32)],
        name=name, compiler_params=_params(("arbitrary",)))(a, b)


def _lru_scan_bwd(name, a, dh):
    s, w = a.shape
    t = SCAN_T
    n = s // t

    def body(a_ref, dh_ref, g_ref, a_next, g_next):
        @pl.when(pl.program_id(0) == 0)
        def _():
            a_next[...] = jnp.zeros_like(a_next)
            g_next[...] = jnp.zeros_like(g_next)

        av = a_ref[...]
        a_up = jnp.where(_iota(av.shape, 0) == t - 1, a_next[0:1], pltpu.roll(av, t - 1, 0))
        ca, cb = _block_scan(a_up, dh_ref[...], True)
        g = ca * g_next[0:1] + cb
        g_ref[...] = g
        a_next[...] = jnp.broadcast_to(av[0:1], a_next.shape)
        g_next[...] = jnp.broadcast_to(g[0:1], g_next.shape)

    rs = pl.BlockSpec((t, w), lambda i: (n - 1 - i, 0))
    return pl.pallas_call(
        body, grid=(n,), in_specs=[rs] * 2, out_specs=rs, out_shape=jax.ShapeDtypeStruct((s, w), F32),
        scratch_shapes=[pltpu.VMEM((8, w), F32), pltpu.VMEM((8, w), F32)],
        name=name, compiler_params=_params(("arbitrary",)))(a, dh)


def _loss_fn(g, x, tgt):
    y = x * lax.rsqrt(jnp.mean(x * x, axis=-1, keepdims=True) + EPS) * g
    err = y - tgt
    return 0.5 * jnp.sum(jnp.mean(err * err, axis=-1, keepdims=True), axis=0, keepdims=True)


def _loss_and_grad(name, g, x, tgt, t=256):
    s, d = x.shape

    def body(g_ref, x_ref, t_ref, loss_ref, dx_ref, dg_ref, dx16_ref):
        val, vjp = jax.vjp(_loss_fn, g_ref[...], x_ref[...], t_ref[...])
        dg, dx, _ = vjp(jnp.ones((1, 1), F32))

        @pl.when(pl.program_id(0) == 0)
        def _():
            loss_ref[...] = jnp.zeros_like(loss_ref)
            dg_ref[...] = jnp.zeros_like(dg_ref)

        loss_ref[...] += jnp.broadcast_to(val, loss_ref.shape)
        dg_ref[...] += dg
        dx_ref[...] = dx
        dx16_ref[...] = dx.astype(BF16)

    return pl.pallas_call(
        body, grid=(s // t,), in_specs=[_whole(g.shape), _rowspec(t, d), _rowspec(t, d)],
        out_specs=[_whole((1, 128)), _rowspec(t, d), _whole(g.shape), _rowspec(t, d)],
        out_shape=[jax.ShapeDtypeStruct((1, 128), F32), jax.ShapeDtypeStruct((s, d), F32),
                   jax.ShapeDtypeStruct(g.shape, F32), jax.ShapeDtypeStruct((s, d), BF16)],
        name=name, compiler_params=_params(("arbitrary",)))(g, x, tgt)


def _ew(name, fn, ins, n_out, t=None):
    r, c = ins[0].shape
    t = _tile(r, (512, 256, 128, 64, 32, 16, 8)) if t is None else t

    def body(*refs):
        outs = fn(*[x[...] for x in refs[:len(ins)]])
        for o_ref, o in zip(refs[len(ins):], outs):
            o_ref[...] = o

    return pl.pallas_call(
        body, grid=(r // t,), in_specs=[_rowspec(t, c)] * len(ins), out_specs=[_rowspec(t, c)] * n_out,
        out_shape=[jax.ShapeDtypeStruct((r, c), F32)] * n_out,
        name=name, compiler_params=_params(("arbitrary",)))(*ins)


def _adam_fn(w, g, m, v):
    m = B1 * m + (1.0 - B1) * g
    v = B2 * v + (1.0 - B2) * (g * g)
    m_hat = m / (1.0 - B1 ** STEP)
    v_hat = v / (1.0 - B2 ** STEP)
    delta = -LR * (m_hat / (jnp.sqrt(v_hat) + AEPS) + WD * w)
    return delta, m, v


def _add2_fn(a, b):
    return (a + b,)


def _adam(name, w, g, m, v):
    shp = w.shape
    two = lambda z: z.reshape(-1, shp[-1])
    outs = _ew(name, _adam_fn, [two(w), two(g), two(m), two(v)], 3)
    return [o.reshape(shp) for o in outs]


def _swap(name, bufs, flips):
    n = len(bufs)
    hbm = pl.BlockSpec(memory_space=pltpu.HBM)

    def body(*refs):
        srcs, dsts = refs[:n], refs[n:2 * n]
        send_sems, recv_sems = refs[2 * n], refs[2 * n + 1]
        me = (lax.axis_index("x"), lax.axis_index("y"), lax.axis_index("c"))
        copies = []
        for i in range(n):
            peer = tuple(1 - m if f else m for m, f in zip(me, flips[i]))
            cp = pltpu.make_async_remote_copy(src_ref=srcs[i], dst_ref=dsts[i], send_sem=send_sems.at[i],
                                              recv_sem=recv_sems.at[i], device_id=peer, device_id_type=MESH)
            cp.start()
            copies.append(cp)
        for cp in copies:
            cp.wait()

    return pl.pallas_call(
        body, in_specs=[hbm] * n, out_specs=[hbm] * n,
        out_shape=[jax.ShapeDtypeStruct(b.shape, b.dtype) for b in bufs],
        scratch_shapes=[pltpu.SemaphoreType.DMA((n,)), pltpu.SemaphoreType.DMA((n,))],
        name=name, compiler_params=pltpu.CompilerParams(has_side_effects=True))(*bufs)


FLIP_C = (False, False, True)
FLIP_Y = (False, True, False)
FLIP_X = (True, False, False)


LAYER_ROWS = PACK_ROWS // DEPTH
HALF_ROWS = LAYER_ROWS // 2
SUM_ROWS = 128
GATHER_IDS = (1, 2, 3, 4)
SCATTER_IDS = (5, 6, 7, 8)


def _mesh_pos():
    ix, iy, core = lax.axis_index("x"), lax.axis_index("y"), lax.axis_index("c")
    others = [(ix, 1 - iy), (1 - ix, iy), (1 - ix, 1 - iy)]
    return ix, iy, core, others


def _pack_big(w_in, w_out, w_up, w_down):
    pad = jnp.pad(w_in, ((0, 0), (0, 0), (0, 1024 - w_in.shape[-1])))
    return jnp.concatenate([pad, w_up, w_down, w_out], axis=1)


def _unpack_big(p):
    w_in = p[..., 0:1024, :D_IN // 4]
    return w_in, p[..., 3072:LAYER_ROWS, :], p[..., 1024:2048, :], p[..., 2048:3072, :]


def _gather_layer(packed, collective_id):
    src = jax.new_ref(packed, memory_space=pltpu.MemorySpace.HBM)
    out = jax.empty_ref(jax.ShapeDtypeStruct((4,) + packed.shape, packed.dtype), memory_space=pltpu.MemorySpace.HBM)
    dma = pltpu.SemaphoreType.DMA

    @pl.kernel(mesh=plsc.ScalarSubcoreMesh(axis_name="seq", num_cores=1), name="gather_%d" % collective_id,
               scratch_types=(dma,) * 12, compiler_params=pltpu.CompilerParams(collective_id=collective_id))
    def launch(*sems):
        send_sems, recv_sems = sems[:6], sems[6:]
        ix, iy, core, others = _mesh_pos()
        barrier = pltpu.get_barrier_semaphore()
        for peer in [(ox, oy, core) for ox, oy in others] + [(ix, iy, 1 - core)]:
            pl.semaphore_signal(barrier, inc=1, device_id=peer, device_id_type=MESH)
        pl.semaphore_wait(barrier, 4)
        mine = pl.ds(core * HALF_ROWS, HALF_ROWS)
        theirs = pl.ds((1 - core) * HALF_ROWS, HALF_ROWS)

        def copy(k, src_ref, owner, rows, to):
            return pltpu.make_async_remote_copy(
                src_ref=src_ref, dst_ref=out.at[owner, rows], send_sem=send_sems[k], recv_sem=recv_sems[k],
                device_id=to, device_id_type=MESH)

        chip = 2 * ix + iy
        first = [copy(j, src.at[mine], chip, mine, (*others[j], core)) for j in range(3)]
        for cp in first:
            cp.start()
        passed = []
        for j, (ox, oy) in enumerate(others):
            owner = 2 * ox + oy
            copy(j, src.at[mine], owner, mine, (ix, iy, core)).wait_recv()
            fwd = copy(3 + j, out.at[owner, mine], owner, mine, (ix, iy, 1 - core))
            fwd.start()
            passed.append(fwd)
        for j, (ox, oy) in enumerate(others):
            copy(3 + j, src.at[theirs], 2 * ox + oy, theirs, (ix, iy, core)).wait_recv()
        for cp in first + passed:
            cp.wait_send()

    launch()
    return out[...]


def _rs_sibling(per_chip):
    hbm = pl.BlockSpec(memory_space=pltpu.HBM)

    def body(src, out, send_sem, recv_sem):
        ix, iy, core, _ = _mesh_pos()
        give = pl.ds((1 - core) * HALF_ROWS, HALF_ROWS)
        cp = pltpu.make_async_remote_copy(src_ref=src.at[pl.ds(0, 4), give], dst_ref=out, send_sem=send_sem,
                                          recv_sem=recv_sem, device_id=(ix, iy, 1 - core), device_id_type=MESH)
        cp.start()
        cp.wait()

    return pl.pallas_call(
        body, in_specs=[hbm], out_specs=hbm, out_shape=jax.ShapeDtypeStruct((4, HALF_ROWS, 1024), F32),
        scratch_shapes=[pltpu.SemaphoreType.DMA, pltpu.SemaphoreType.DMA],
        name="rs_sibling", compiler_params=pltpu.CompilerParams(has_side_effects=True))(per_chip)


def _rs_add_pair(per_chip, sib, chip, core):
    nb = HALF_ROWS // SUM_ROWS
    blk = (1, SUM_ROWS, 1024)

    def body(chip_ref, core_ref, a_ref, b_ref, own_ref, send_ref):
        total = a_ref[0] + b_ref[0]
        send_ref[0] = total.astype(BF16)

        @pl.when(pl.program_id(1) == chip_ref[0])
        def _():
            own_ref[...] = total

    return pl.pallas_call(
        body,
        grid_spec=pltpu.PrefetchScalarGridSpec(
            num_scalar_prefetch=2, grid=(nb, 4),
            in_specs=[pl.BlockSpec(blk, lambda i, k, ch, co: (k, co[0] * nb + i, 0)),
                      pl.BlockSpec(blk, lambda i, k, ch, co: (k, i, 0))],
            out_specs=[pl.BlockSpec((SUM_ROWS, 1024), lambda i, k, ch, co: (i, 0)),
                       pl.BlockSpec(blk, lambda i, k, ch, co: (k, i, 0))]),
        out_shape=[jax.ShapeDtypeStruct((HALF_ROWS, 1024), F32), jax.ShapeDtypeStruct((4, HALF_ROWS, 1024), BF16)],
        name="rs_add_pair", compiler_params=_params(("arbitrary", "arbitrary")))(
            chip.reshape(1), core.reshape(1), per_chip, sib)


def _rs_chips(pair, collective_id):
    src_ref = jax.new_ref(pair, memory_space=pltpu.MemorySpace.HBM)
    got_ref = jax.empty_ref(jax.ShapeDtypeStruct((3,) + pair.shape[1:], pair.dtype), memory_space=pltpu.MemorySpace.HBM)
    dma = pltpu.SemaphoreType.DMA

    @pl.kernel(mesh=plsc.ScalarSubcoreMesh(axis_name="seq", num_cores=1), name="rs_chips_%d" % collective_id,
               scratch_types=(dma,) * 6, compiler_params=pltpu.CompilerParams(collective_id=collective_id))
    def launch(s0, s1, s2, r0, r1, r2):
        ix, iy, core, others = _mesh_pos()
        barrier = pltpu.get_barrier_semaphore()
        for ox, oy in others:
            pl.semaphore_signal(barrier, inc=1, device_id=(ox, oy, core), device_id_type=MESH)
        pl.semaphore_wait(barrier, 3)
        copies = []
        for j, ((ox, oy), send_sem, recv_sem) in enumerate(zip(others, (s0, s1, s2), (r0, r1, r2))):
            cp = pltpu.make_async_remote_copy(src_ref=src_ref.at[2 * ox + oy], dst_ref=got_ref.at[j], send_sem=send_sem,
                                              recv_sem=recv_sem, device_id=(ox, oy, core), device_id_type=MESH)
            cp.start()
            copies.append(cp)
        for cp in copies:
            cp.wait()

    launch()
    return got_ref[...]


def _rs_add_chips(own, got, core, layer, buf):
    nb = HALF_ROWS // SUM_ROWS
    blk = (1, SUM_ROWS, 1024)

    def body(core_ref, a_ref, g0_ref, g1_ref, g2_ref, *rest):
        o_ref = rest[-1]
        o_ref[0, 0] = ((a_ref[...] + g0_ref[0].astype(F32)) + g1_ref[0].astype(F32)) + g2_ref[0].astype(F32)

    aliased = [] if buf is None else [buf]
    return pl.pallas_call(
        body,
        grid_spec=pltpu.PrefetchScalarGridSpec(
            num_scalar_prefetch=1, grid=(nb,),
            in_specs=[pl.BlockSpec((SUM_ROWS, 1024), lambda i, c: (i, 0))]
            + [pl.BlockSpec(blk, functools.partial(lambda j, i, c: (j, i, 0), j)) for j in range(3)]
            + [pl.BlockSpec(memory_space=pl.ANY)] * len(aliased),
            out_specs=pl.BlockSpec((1, 1, SUM_ROWS, 1024), lambda i, c: (layer, c[0], i, 0))),
        out_shape=jax.ShapeDtypeStruct((DEPTH, 2, HALF_ROWS, 1024), F32),
        input_output_aliases={} if buf is None else {5: 0},
        name="rs_add_chips", compiler_params=_params(("arbitrary",)))(core.reshape(1), own, got, got, got, *aliased)


def _rs_share(buf):
    hbm = pl.BlockSpec(memory_space=pltpu.HBM)

    def body(src, out, send_sem, recv_sem):
        ix, iy, core, _ = _mesh_pos()
        layers = pl.ds(0, DEPTH)
        cp = pltpu.make_async_remote_copy(src_ref=src.at[layers, core], dst_ref=out.at[layers, core], send_sem=send_sem,
                                          recv_sem=recv_sem, device_id=(ix, iy, 1 - core), device_id_type=MESH)
        cp.start()
        pltpu.make_async_remote_copy(src_ref=src.at[layers, 1 - core], dst_ref=out.at[layers, 1 - core],
                                     send_sem=send_sem, recv_sem=recv_sem, device_id=(ix, iy, core),
                                     device_id_type=MESH).wait_recv()
        cp.wait_send()

    return pl.pallas_call(
        body, in_specs=[hbm], out_specs=hbm, out_shape=jax.ShapeDtypeStruct(buf.shape, buf.dtype),
        input_output_aliases={0: 0}, scratch_shapes=[pltpu.SemaphoreType.DMA, pltpu.SemaphoreType.DMA],
        name="rs_share", compiler_params=pltpu.CompilerParams(has_side_effects=True))(buf)


def _scatter_start(per_chip, chip, core, layer):
    own, pair = _rs_add_pair(per_chip, _rs_sibling(per_chip), chip, core)
    return own, _rs_chips(pair, SCATTER_IDS[layer])


def _allreduce_small(tag, buf):
    for name, flip in (("c", FLIP_C), ("y", FLIP_Y), ("x", FLIP_X)):
        (other,) = _swap(tag + "_swap_" + name, [buf], [flip])
        (buf,) = _ew(tag + "_add_" + name, _add2_fn, [buf, other], 1)
    return buf


def _permute_in_cols(w):
    z = jnp.zeros(w.shape[:-1] + (D_IN_PAD - D_IN,), w.dtype)
    return jnp.concatenate([w[..., 1024:3072], w[..., :1024], w[..., 3080:D_IN], w[..., 3072:3080], z], axis=-1)


IN_COL_SEGMENTS = ((0, 1024, 2048), (1024, 3072, -1024), (3072, 3080, GATE0 - 3072), (3080, D_IN, COL_CX - 3080))


def _orig_cols(w_pad, lo, hi):
    parts = []
    for first, last, shift in IN_COL_SEGMENTS:
        a, b = max(lo, first), min(hi, last)
        if a < b:
            parts.append(w_pad[..., a + shift:b + shift])
    return parts


def _block_diag(w):
    out = jnp.zeros((C_W, C_W), w.dtype)
    for n in range(4):
        out = lax.dynamic_update_slice(out, w[n], (64 * n, 64 * n))
    return out


def _diag_blocks(w):
    return jnp.stack([w[64 * n:64 * (n + 1), 64 * n:64 * (n + 1)] for n in range(4)], axis=0)


def _gate_row(v):
    return jnp.concatenate([jnp.zeros((4,), F32), v, jnp.zeros((120,), F32)]).reshape(1, 128)


SMALL = ("norm1_g", "hgrn_lb_logits", "hgrn_norm_g", "gdn_conv_w", "gdn_a_log", "gdn_dt_bias", "gdn_norm_g",
         "lru_conv_w", "lru_conv_b", "lru_w_a", "lru_b_a", "lru_w_x", "lru_b_x", "lru_lambda", "norm2_g",
         "final_norm_g")


def _flatten_small(tree):
    flat = jnp.concatenate([tree[k].reshape(-1) for k in sorted(tree)])
    rows = -(-flat.shape[0] // (512 * 128)) * 512
    return jnp.pad(flat, (0, rows * 128 - flat.shape[0])).reshape(rows, 128)


def _unflatten_small(buf, shapes):
    flat = buf.reshape(-1)
    out, off = {}, 0
    for k in sorted(shapes):
        size = 1
        for d in shapes[k]:
            size *= d
        out[k] = flat[off:off + size].reshape(shapes[k])
        off += size
    return out


def _layer_fwd(l, x, p, w_in, w_out, w_up, w_down):
    sv = {"x0": x}
    (h,) = _rowwise_fwd("norm1", _rmsnorm_fn, [p["norm1_g"]], [x], [D_MODEL], [BF16], 512)
    proj = _matmul("proj_in", h, w_in, "nn")
    sv["h"], sv["proj"] = h, proj
    aq, af, ag = _cols(proj, COL_AQ, A_W), _cols(proj, COL_AF, A_W), _cols(proj, COL_AG, A_W)
    ai = lax.slice_in_dim(proj, COL_AI, COL_AI + A_W, axis=1)
    bz, gate, cy = _cols(proj, COL_BZ, B_W), _cols(proj, GATE0, 128), _cols(proj, COL_CY, C_W)
    q, k, lf = _rowwise_fwd("hgrn_pre", functools.partial(_hgrn_pre_fn, l), [p["lb_logits"]], [aq, af],
                            [A_W] * 3, [F32] * 3, 512)
    o_a, sts_a = _hgrn_fwd("hgrn_chunk", q, k, ai, lf)
    (y_a,) = _rowwise_fwd("hgrn_post", _hgrn_post_fn, [p["hgrn_norm_g"]], [o_a, ag], [A_W], [BF16], 512)
    sv.update(aq=aq, af=af, ai=ai, ag=ag, hq=q, hk=k, hlf=lf, o_a=o_a, sts_a=sts_a)
    qkvc = _conv_fwd("gdn_conv", proj, 0, 3 * B_W, p["gdn_conv_w"], jnp.zeros((1, 3 * B_W), F32), 256)
    gq, gk, gv, gb, gla = _rowwise_fwd("gdn_pre", _gdn_pre_fn, [p["gdn_a_log"], p["gdn_dt_bias"]], [qkvc, gate],
                                       [B_W] * 5, [F32] * 5, 256)
    wy = _gdn_prep_fwd("gdn_prep", gq, gk, gv, gb, gla)
    o_b, sts_b = _gdn_rec_fwd("gdn_rec", *wy)
    sv["wy"] = wy
    (y_b,) = _rowwise_fwd("gdn_post", _gdn_post_fn, [p["gdn_norm_g"]], [o_b, bz], [B_W], [BF16], 512)
    sv.update(qkvc=qkvc, gate=gate, bz=bz, gq=gq, gk=gk, gv=gv, gb=gb, gla=gla, o_b=o_b, sts_b=sts_b)
    xc = _conv_fwd("lru_conv", proj, COL_CX, C_W, p["lru_conv_w"], p["lru_conv_b"], 512)
    a, b = _rowwise_fwd("lru_pre", _lru_pre_fn, [p["lru_w_a"], p["lru_b_a"], p["lru_w_x"], p["lru_b_x"], p["lru_lambda"]],
                        [xc], [C_W] * 2, [F32] * 2, 512)
    hs, h_prev = _lru_scan_fwd("lru_scan", a, b)
    (y_c,) = _rowwise_fwd("lru_post", _lru_post_fn, [], [hs, cy], [C_W], [BF16], 512)
    sv.update(xc=xc, la=a, hs=hs, h_prev=h_prev, cy=cy)
    mixed = jnp.concatenate([y_b, y_a, y_c], axis=1)
    x1 = _matmul("proj_out", mixed, w_out, "nn", res=x)
    (h2,) = _rowwise_fwd("norm2", _rmsnorm_fn, [p["norm2_g"]], [x1], [D_MODEL], [BF16], 512)
    up, act = _matmul("mlp_up", h2, w_up, "nn", also=_relu2)
    x2 = _matmul("mlp_down", act, w_down, "nn", res=x1)
    sv.update(mixed=mixed, x1=x1, h2=h2, up=up, act=act)
    return x2, sv


def _layer_bwd(l, dx, dx16, p, sv, w_in, w_out, w_up, w_down):
    gs = {}
    dup = _matmul("mlp_down_dx", dx16, w_down, "nt", epi=_drelu2, epi_in=sv["up"], out_dtype=BF16)
    d_w_down = _matmul("mlp_down_dw", sv["act"], dx16, "tn")
    dh2 = _matmul("mlp_up_dx", dup, w_up, "nt")
    d_w_up = _matmul("mlp_up_dw", sv["h2"], dup, "tn")
    (gs["norm2_g"],), (dx1, dx1_16) = _rowwise_bwd("norm2_bwd", _rmsnorm_fn, [p["norm2_g"]], [sv["x1"]], [dh2], 512,
                                                   res=dx, copy16=True)
    dmixed = _matmul("proj_out_dx", dx1_16, w_out, "nt")
    d_w_out = _matmul("proj_out_dw", sv["mixed"], dx1_16, "tn")
    dy_b, dy_a, dy_c = _cols(dmixed, 0, B_W), _cols(dmixed, B_W, A_W), _cols(dmixed, B_W + A_W, C_W)
    (gs["hgrn_norm_g"],), (do_a, dag) = _rowwise_bwd("hgrn_post_bwd", _hgrn_post_fn, [p["hgrn_norm_g"]],
                                                     [sv["o_a"], sv["ag"]], [dy_a], 512, dtypes=[F32, BF16])
    dq, dk, dai, dlf = _hgrn_bwd("hgrn_chunk_bwd", sv["hq"], sv["hk"], sv["ai"], sv["hlf"], sv["sts_a"], do_a)
    (gs["lb_logits"],), (daq, daf) = _rowwise_bwd("hgrn_pre_bwd", functools.partial(_hgrn_pre_fn, l), [p["lb_logits"]],
                                                  [sv["aq"], sv["af"]], [dq, dk, dlf], 512, dtypes=[BF16, BF16])
    (gs["gdn_norm_g"],), (do_b, dbz) = _rowwise_bwd("gdn_post_bwd", _gdn_post_fn, [p["gdn_norm_g"]],
                                                    [sv["o_b"], sv["bz"]], [dy_b], 512, dtypes=[F32, BF16])
    d_wy = _gdn_rec_bwd("gdn_rec_bwd", *sv["wy"], sv["sts_b"], do_b)
    dgq, dgk, dgv, dgb, dgla = _gdn_prep_bwd("gdn_prep_bwd", sv["gq"], sv["gk"], sv["gv"], sv["gb"], sv["gla"], d_wy)
    (gs["gdn_a_log"], gs["gdn_dt_bias"]), (dqkvc, dgate) = _rowwise_bwd(
        "gdn_pre_bwd", _gdn_pre_fn, [p["gdn_a_log"], p["gdn_dt_bias"]], [sv["qkvc"], sv["gate"]],
        [dgq, dgk, dgv, dgb, dgla], 256, dtypes=[F32, BF16])
    dqkv, gs["gdn_conv_w"], _ = _conv_bwd("gdn_conv_bwd", dqkvc, sv["proj"], 0, 3 * B_W, p["gdn_conv_w"], 256)
    _, (dhs, dcy) = _rowwise_bwd("lru_post_bwd", _lru_post_fn, [], [sv["hs"], sv["cy"]], [dy_c], 512, dtypes=[F32, BF16])
    g = _lru_scan_bwd("lru_scan_bwd", sv["la"], dhs)
    (da,) = _rowwise_fwd("lru_da", _lru_da_fn, [], [g, sv["h_prev"]], [C_W], [F32], 512)
    lru_params = [p["lru_w_a"], p["lru_b_a"], p["lru_w_x"], p["lru_b_x"], p["lru_lambda"]]
    dps, (dxc,) = _rowwise_bwd("lru_pre_bwd", _lru_pre_fn, lru_params, [sv["xc"]], [da, g], 512)
    gs["lru_w_a"], gs["lru_b_a"], gs["lru_w_x"], gs["lru_b_x"], gs["lru_lambda"] = dps
    dcx, gs["lru_conv_w"], gs["lru_conv_b"] = _conv_bwd("lru_conv_bwd", dxc, sv["proj"], COL_CX, C_W, p["lru_conv_w"], 512)
    s = dx.shape[0]
    dproj = jnp.concatenate([dqkv, dbz, daq, daf, dai.astype(BF16), dag, dcx, dcy, dgate,
                             jnp.zeros((s, D_IN_PAD - GATE0 - 128), BF16)], axis=1)
    dh = _matmul("proj_in_dx", dproj, w_in, "nt")
    d_w_in = _matmul("proj_in_dw", sv["h"], dproj, "tn")
    (gs["norm1_g"],), (dx0, dx0_16) = _rowwise_bwd("norm1_bwd", _rmsnorm_fn, [p["norm1_g"]], [sv["x0"]], [dh], 512,
                                                   res=dx1, copy16=True)
    return dx0, dx0_16, dict(w_in=d_w_in, w_out=d_w_out, w_up=d_w_up, w_down=d_w_down), gs


def _layer_params(l, lb_logits, hgrn_norm_g, gdn_conv_w, gdn_a_log, gdn_dt_bias, gdn_norm_g, lru_conv_w, lru_conv_b,
                  lru_w_a, lru_b_a, lru_w_x, lru_b_x, lru_lambda, norm1_g, norm2_g):
    row = lambda v: v.reshape(1, -1)
    return dict(
        norm1_g=row(norm1_g[l]), norm2_g=row(norm2_g[l]), lb_logits=lb_logits,
        hgrn_norm_g=row(jnp.tile(hgrn_norm_g[l], A_W // A_HD)),
        gdn_conv_w=gdn_conv_w[l], gdn_a_log=_gate_row(gdn_a_log[l]), gdn_dt_bias=_gate_row(gdn_dt_bias[l]),
        gdn_norm_g=row(jnp.tile(gdn_norm_g[l], B_NH)),
        lru_conv_w=lru_conv_w[l], lru_conv_b=row(lru_conv_b[l]), lru_w_a=_block_diag(lru_w_a[l]), lru_b_a=row(lru_b_a[l]),
        lru_w_x=_block_diag(lru_w_x[l]), lru_b_x=row(lru_b_x[l]), lru_lambda=row(lru_lambda[l]))


def _small_grads_to_reference(gs_layers, d_final_g, d_logits):
    st = lambda k, f=lambda z: z: jnp.stack([f(g[k]) for g in gs_layers], axis=0)
    vec = lambda z: z.reshape(-1)
    return dict(
        norm1_g=st("norm1_g", vec), norm2_g=st("norm2_g", vec), hgrn_lb_logits=d_logits,
        hgrn_norm_g=st("hgrn_norm_g", lambda z: z.reshape(A_W // A_HD, A_HD).sum(0)),
        gdn_conv_w=st("gdn_conv_w"), gdn_a_log=st("gdn_a_log", lambda z: z[0, 4:8]),
        gdn_dt_bias=st("gdn_dt_bias", lambda z: z[0, 4:8]),
        gdn_norm_g=st("gdn_norm_g", lambda z: z.reshape(B_NH, B_HD).sum(0)),
        lru_conv_w=st("lru_conv_w"), lru_conv_b=st("lru_conv_b", vec), lru_w_a=st("lru_w_a", _diag_blocks),
        lru_b_a=st("lru_b_a", vec), lru_w_x=st("lru_w_x", _diag_blocks), lru_b_x=st("lru_b_x", vec),
        lru_lambda=st("lru_lambda", vec), final_norm_g=d_final_g.reshape(-1))


def _local_step(x, tgt, small, layer_weights, on_layer_grads=None):
    layer_p = [_layer_params(l, small["hgrn_lb_logits"], small["hgrn_norm_g"], small["gdn_conv_w"], small["gdn_a_log"],
                             small["gdn_dt_bias"], small["gdn_norm_g"], small["lru_conv_w"], small["lru_conv_b"],
                             small["lru_w_a"], small["lru_b_a"], small["lru_w_x"], small["lru_b_x"], small["lru_lambda"],
                             small["norm1_g"], small["norm2_g"]) for l in range(DEPTH)]
    saved, weights = [], []
    for l in range(DEPTH):
        x, w = layer_weights(l, x)
        x, sv = _layer_fwd(l, x, layer_p[l], *w)
        saved.append(sv)
        weights.append(w)
    loss, dx, d_final_g, dx16 = _loss_and_grad("final_loss", small["final_norm_g"].reshape(1, -1), x, tgt)
    big = [None] * DEPTH
    gs_layers = [None] * DEPTH
    d_logits = jnp.zeros_like(small["hgrn_lb_logits"])
    for l in reversed(range(DEPTH)):
        dx, dx16, big[l], gs_layers[l] = _layer_bwd(l, dx, dx16, layer_p[l], saved[l], *weights[l])
        d_logits = d_logits + gs_layers[l]["lb_logits"]
        if on_layer_grads is not None:
            big[l] = on_layer_grads(l, big[l])
    return loss, dx, big, _small_grads_to_reference(gs_layers, d_final_g, d_logits)


def kernel(x, norm1_g, w_in, hgrn_lb_logits, hgrn_norm_g, gdn_conv_w, gdn_a_log, gdn_dt_bias, gdn_norm_g, lru_conv_w, lru_conv_b, lru_w_a, lru_b_a, lru_w_x, lru_b_x, lru_lambda, w_out, norm2_g, w_up, w_down, final_norm_g, loss_target, m_norm1_g, m_w_in, m_hgrn_lb_logits, m_hgrn_norm_g, m_gdn_conv_w, m_gdn_a_log, m_gdn_dt_bias, m_gdn_norm_g, m_lru_conv_w, m_lru_conv_b, m_lru_w_a, m_lru_b_a, m_lru_w_x, m_lru_b_x, m_lru_lambda, m_w_out, m_norm2_g, m_w_up, m_w_down, m_final_norm_g, v_norm1_g, v_w_in, v_hgrn_lb_logits, v_hgrn_norm_g, v_gdn_conv_w, v_gdn_a_log, v_gdn_dt_bias, v_gdn_norm_g, v_lru_conv_w, v_lru_conv_b, v_lru_w_a, v_lru_b_a, v_lru_w_x, v_lru_b_x, v_lru_lambda, v_w_out, v_norm2_g, v_w_up, v_w_down, v_final_norm_g):
    args = dict(locals())
    ix, iy, core = lax.axis_index("x"), lax.axis_index("y"), lax.axis_index("c")
    chip = 2 * ix + iy

    packed = _pack_big(w_in, w_out, w_up, w_down).astype(BF16)
    gathered = [_gather_layer(packed[l], GATHER_IDS[l]) for l in range(DEPTH)]

    def layer_weights(l, x):
        got, own = gathered[l], packed[l]
        if l:
            got, own, x = lax.optimization_barrier((got, own, x))
        shares = lax.dynamic_update_slice(got, own[None], (chip, 0, 0))
        parts = [_unpack_big(shares[k]) for k in range(4)]
        return x, (_permute_in_cols(jnp.concatenate([pt[0] for pt in parts], axis=-1)),
                   jnp.concatenate([parts[k][1] for k in W_OUT_CHIP_ORDER], axis=0),
                   jnp.concatenate([pt[2] for pt in parts], axis=-1),
                   jnp.concatenate([pt[3] for pt in parts], axis=0))

    n_gc, n_lc = gdn_conv_w.shape[-1], lru_conv_w.shape[-1]
    conv_full = dict(
        gdn_conv_w=lax.dynamic_update_slice(jnp.zeros((DEPTH, 4, 4 * n_gc), F32), gdn_conv_w * 0.5, (0, 0, chip * n_gc)),
        lru_conv_w=lax.dynamic_update_slice(jnp.zeros((DEPTH, 4, 4 * n_lc), F32), lru_conv_w * 0.5, (0, 0, chip * n_lc)))
    conv_shapes = {k: v.shape for k, v in conv_full.items()}
    conv_full = _unflatten_small(_allreduce_small("convw", _flatten_small(conv_full)), conv_shapes)

    small = {k: args[k] for k in SMALL}
    small.update(conv_full)

    n_in = D_IN // 4

    def start_scatter(l, b):
        pieces = []
        for k in range(4):
            cols = jnp.concatenate(_orig_cols(b["w_in"], k * n_in, (k + 1) * n_in), axis=1)
            r0 = 256 * W_OUT_CHIP_ORDER.index(k)
            pieces += [jnp.pad(cols, ((0, 0), (0, 1024 - n_in))), b["w_up"][:, k * 1024:(k + 1) * 1024],
                       b["w_down"][k * 1024:(k + 1) * 1024], b["w_out"][r0:r0 + 256]]
        per_chip = jnp.concatenate(pieces, axis=0).reshape(4, LAYER_ROWS, 1024)
        return _scatter_start(per_chip, chip, core, l)

    loss, grad_x, started, sg = _local_step(x[0], loss_target[0], small, layer_weights, start_scatter)

    sg["loss"] = loss[0, :1]
    shapes = {k: v.shape for k, v in sg.items()}
    sg = _unflatten_small(_allreduce_small("smallgrad", _flatten_small(sg)), shapes)
    loss_out = sg.pop("loss")[0]

    buf = None
    for l in reversed(range(DEPTH)):
        own, got = started[l]
        buf = _rs_add_chips(own, got, core, l, buf)
    g_in, g_out, g_up, g_down = _unpack_big(_rs_share(buf).reshape(DEPTH, LAYER_ROWS, 1024))

    grads = dict(sg)
    grads["gdn_conv_w"] = lax.dynamic_slice_in_dim(sg["gdn_conv_w"], chip * n_gc, n_gc, axis=2)
    grads["lru_conv_w"] = lax.dynamic_slice_in_dim(sg["lru_conv_w"], chip * n_lc, n_lc, axis=2)
    grads.update(w_in=g_in, w_out=g_out, w_up=g_up, w_down=g_down)

    names = ['norm1_g', 'w_in', 'hgrn_lb_logits', 'hgrn_norm_g', 'gdn_conv_w', 'gdn_a_log', 'gdn_dt_bias', 'gdn_norm_g',
             'lru_conv_w', 'lru_conv_b', 'lru_w_a', 'lru_b_a', 'lru_w_x', 'lru_b_x', 'lru_lambda', 'w_out', 'norm2_g',
             'w_up', 'w_down', 'final_norm_g']
    big_names = ("w_in", "w_out", "w_up", "w_down")
    delta, new_m, new_v = {}, {}, {}
    for k in big_names:
        delta[k], new_m[k], new_v[k] = _adam("adam_" + k, args[k], grads[k], args["m_" + k], args["v_" + k])
    small_names = [k for k in names if k not in big_names]
    shapes = {k: args[k].shape for k in small_names}
    flat = [_flatten_small({k: src(k) for k in small_names})
            for src in (lambda k: args[k], lambda k: grads[k], lambda k: args["m_" + k], lambda k: args["v_" + k])]
    outs = _ew("adam_small", _adam_fn, flat, 3)
    for dst, o in zip((delta, new_m, new_v), outs):
        dst.update(_unflatten_small(o, shapes))
    return (loss_out, grad_x[None], *[grads[k] for k in names], *[delta[k] for k in names],
            *[new_m[k] for k in names], *[new_v[k] for k in names])
```

```python
import functools

import jax
import jax.numpy as jnp
from jax import lax
from jax.experimental import pallas as pl
from jax.experimental.pallas import tpu as pltpu
from jax.experimental.pallas import tpu_sc as plsc

F32 = jnp.float32
BF16 = jnp.bfloat16
MESH = pl.DeviceIdType.MESH

DEPTH = 4
D_MODEL = 1024
A_W, B_W, C_W = 256, 512, 256
A_HD, B_HD = 64, 128
B_NH = B_W // B_HD
D_IN = 3592
D_IN_PAD = 3840
COL_BZ, COL_AQ, COL_AF, COL_AI, COL_AG, COL_CX, COL_CY = 1536, 2048, 2304, 2560, 2816, 3072, 3328
W_OUT_CHIP_ORDER = (1, 2, 0, 3)
GATE0 = 3584
D_FF = 4096
CHUNK = 64
SUB = 16
EPS = 1e-6
TINY = 1e-30
RG_C = 8.0
LR, B1, B2, AEPS, WD, STEP = 0.001, 0.9, 0.999, 1e-08, 0.01, 10
VMEM_LIMIT = 56 * 1024 * 1024
PACK_ROWS = 13 * 1024

NN = ((1,), (0,))
NT = ((1,), (1,))
TN = ((0,), (0,))


DIMS = {"nn": NN, "nt": NT, "tn": TN}


def _split(a, n):
    parts, r = [], a
    for i in range(n):
        p = r.astype(BF16)
        parts.append(p)
        if i + 1 < n:
            r = r - p.astype(F32)
    return parts


def _split_dot(a, b, mode, na, nb):
    (ca,), (cb,) = DIMS[mode]
    dims = ((DIMS[mode]), ((), ())) if a.ndim == 2 else (((ca + 1,), (cb + 1,)), ((0,), (0,)))
    pa, pb = _split(a, na), _split(b, nb)
    acc = None
    for i in range(na):
        for j in range(nb):
            if i + j < max(na, nb):
                t = lax.dot_general(pa[i], pb[j], dims, preferred_element_type=F32)
                acc = t if acc is None else acc + t
    return acc


@functools.partial(jax.custom_vjp, nondiff_argnums=(2, 3, 4, 5))
def _mdot(a, b, mode, na, nb, need):
    return _split_dot(a, b, mode, na, nb)


def _mdot_fwd(a, b, mode, na, nb, need):
    return _split_dot(a, b, mode, na, nb), (a, b)


def _mdot_bwd(mode, na, nb, need, res, ct):
    a, b = res
    n = max(na, nb)
    da, db = jnp.zeros_like(a), jnp.zeros_like(b)
    if mode == "nn":
        if need[0]:
            da = _split_dot(ct, b, "nt", n, nb)
        if need[1]:
            db = _split_dot(a, ct, "tn", na, n)
    elif mode == "nt":
        if need[0]:
            da = _split_dot(ct, b, "nn", n, nb)
        if need[1]:
            db = _split_dot(ct, a, "tn", n, na)
    else:
        if need[0]:
            da = _split_dot(b, ct, "nt", nb, n)
        if need[1]:
            db = _split_dot(a, ct, "nn", na, n)
    return da, db


_mdot.defvjp(_mdot_fwd, _mdot_bwd)
BOTH = (True, True)


def _bdot(a, b, mode="nn"):
    return _mdot(a, b, mode, 1, 1, BOTH)


def _group_sum(x, ones):
    return _mdot(x, ones, "nn", 2, 1, (True, False))


def _running_sum(x):
    shape = x.shape[:-2] + (CHUNK, CHUNK)
    tri = (_iota(shape, len(shape) - 1) <= _iota(shape, len(shape) - 2)).astype(F32)
    return _mdot(tri, x, "nn", 1, 3, (False, True))


def _iota(shape, d):
    return lax.broadcasted_iota(jnp.int32, shape, d)


def _block_ones(width, group):
    return (_iota((width, width), 0) // group == _iota((width, width), 1) // group).astype(F32)


def _params(sem):
    return pltpu.CompilerParams(dimension_semantics=sem, vmem_limit_bytes=VMEM_LIMIT)


def _tile(dim, prefs):
    for p in prefs:
        if dim % p == 0:
            return p
    return dim


def _whole(shape):
    return pl.BlockSpec(shape, lambda i: (0,) * len(shape))


def _rowspec(t, w, col0=0):
    cb = col0 // w
    return pl.BlockSpec((t, w), lambda i: (i, cb))


def _cols(arr, col0, width):
    assert col0 % width == 0
    return (arr, col0, width)


def _row_operands(rows, t):
    arrs, specs = [], []
    for r in rows:
        arr, col0, w = r if isinstance(r, tuple) else (r, 0, r.shape[1])
        arrs.append(arr)
        specs.append(_rowspec(t, w, col0))
    return arrs, specs


def _rowwise_fwd(name, fn, params, rows, out_widths, out_dtypes, t):
    rows, row_specs = _row_operands(rows, t)
    s = rows[0].shape[0]
    n_p, n_r = len(params), len(rows)

    def body(*refs):
        p = [r[...] for r in refs[:n_p]]
        xs = [r[...].astype(F32) for r in refs[n_p:n_p + n_r]]
        outs = fn(pl.program_id(0) * t, *p, *xs)
        for o_ref, o in zip(refs[n_p + n_r:], outs):
            o_ref[...] = o.astype(o_ref.dtype)

    return pl.pallas_call(
        body, grid=(s // t,),
        in_specs=[_whole(p.shape) for p in params] + row_specs,
        out_specs=[_rowspec(t, w) for w in out_widths],
        out_shape=[jax.ShapeDtypeStruct((s, w), d) for w, d in zip(out_widths, out_dtypes)],
        name=name, compiler_params=_params(("arbitrary",)))(*params, *rows)


def _rowwise_bwd(name, fn, params, rows, douts, t, need=None, res=None, dtypes=None, copy16=False):
    rows, row_specs = _row_operands(rows, t)
    douts, dout_specs = _row_operands(douts, t)
    s = rows[0].shape[0]
    n_p, n_r, n_o = len(params), len(rows), len(douts)
    need = [True] * n_r if need is None else need
    n_res = 0 if res is None else 1

    def body(*refs):
        p = [r[...] for r in refs[:n_p]]
        xs = [r[...].astype(F32) for r in refs[n_p:n_p + n_r]]
        dys = [r[...].astype(F32) for r in refs[n_p + n_r:n_p + n_r + n_o]]
        k = n_p + n_r + n_o
        res_ref = refs[k] if n_res else None
        dp_refs = refs[k + n_res:k + n_res + n_p]
        dx_refs = refs[k + n_res + n_p:]
        row0 = pl.program_id(0) * t
        _, vjp = jax.vjp(lambda *a: tuple(fn(row0, *a)), *p, *xs)
        g = vjp(tuple(dys))

        @pl.when(pl.program_id(0) == 0)
        def _():
            for r in dp_refs:
                r[...] = jnp.zeros_like(r)

        for r, gp in zip(dp_refs, g[:n_p]):
            r[...] += gp
        gx = [gi for gi, nd in zip(g[n_p:], need) if nd]
        if n_res:
            gx[0] = gx[0] + res_ref[...]
        if copy16:
            gx.append(gx[0])
        for r, gi in zip(dx_refs, gx):
            r[...] = gi.astype(r.dtype)

    widths = [sp.block_shape[1] for sp, nd in zip(row_specs, need) if nd]
    dtypes = [F32] * len(widths) if dtypes is None else list(dtypes)
    if copy16:
        widths, dtypes = widths + widths[:1], dtypes + [BF16]
    outs = pl.pallas_call(
        body, grid=(s // t,),
        in_specs=([_whole(p.shape) for p in params] + row_specs + dout_specs
                  + ([_rowspec(t, res.shape[1])] if n_res else [])),
        out_specs=[_whole(p.shape) for p in params] + [_rowspec(t, w) for w in widths],
        out_shape=([jax.ShapeDtypeStruct(p.shape, F32) for p in params]
                   + [jax.ShapeDtypeStruct((s, w), d) for w, d in zip(widths, dtypes)]),
        name=name, compiler_params=_params(("arbitrary",)))(*params, *rows, *douts, *([res] if n_res else []))
    return list(outs[:n_p]), list(outs[n_p:])


def _rmsnorm_fn(row0, g, x):
    y = x * lax.rsqrt(jnp.mean(x * x, axis=-1, keepdims=True) + EPS)
    return (y * g,)


def _hgrn_pre_fn(layer, row0, logits, aq, af):
    m = jnp.max(logits, axis=0, keepdims=True)
    e = jnp.exp(logits - m)
    p = e / jnp.sum(e, axis=0, keepdims=True)
    if layer == 0:
        lb = jnp.zeros((1, A_W), F32)
    else:
        acc = p[1:2]
        for j in range(2, layer + 1):
            acc = acc + p[j:j + 1]
        lb = jnp.minimum(jnp.maximum(acc, 0.0), 1.0 - EPS)
    sig = jax.nn.sigmoid(af)
    f = lb + (1.0 - lb) * sig
    log_f = jnp.log(jnp.maximum(f, TINY))
    k = (1.0 - lb) * jax.nn.sigmoid(-af)
    return jax.nn.silu(aq), k, log_f


def _hgrn_post_fn(row0, g, o, ag):
    ms = _group_sum(o * o, _block_ones(A_W, A_HD)) * (1.0 / A_HD)
    return (o * lax.rsqrt(ms + EPS) * g * jax.nn.silu(ag),)


def _gdn_pre_fn(row0, a_log, dt_bias, qkvc, gate):
    act = jax.nn.silu(qkvc)
    q, k, v = act[:, :B_W], act[:, B_W:2 * B_W], act[:, 2 * B_W:]
    bo = _block_ones(B_W, B_HD)
    q = q * lax.rsqrt(_group_sum(q * q, bo) + EPS) * (B_HD ** -0.5)
    k = k * lax.rsqrt(_group_sum(k * k, bo) + EPS)
    beta = jax.nn.sigmoid(gate)
    log_alpha = -jnp.exp(a_log) * jax.nn.softplus(gate + dt_bias)
    t = gate.shape[0]
    spread = lambda z, lane0: jnp.concatenate(
        [jnp.broadcast_to(z[:, lane0 + h:lane0 + h + 1], (t, B_HD)) for h in range(B_NH)], axis=1)
    return q, k, v, spread(beta, 0), spread(log_alpha, B_NH)


def _gdn_post_fn(row0, g, o, bz):
    ms = _group_sum(o * o, _block_ones(B_W, B_HD)) * (1.0 / B_HD)
    return (o * lax.rsqrt(ms + EPS) * g * jax.nn.silu(bz),)


def _lru_pre_fn(row0, w_a, b_a, w_x, b_x, lam, xc):
    r = jax.nn.sigmoid(_mdot(xc, w_a, "nn", 2, 2, BOTH) + b_a)
    i = jax.nn.sigmoid(_mdot(xc, w_x, "nn", 2, 2, BOTH) + b_x)
    log_a = -RG_C * r * jax.nn.softplus(-lam)
    a = jnp.exp(log_a)
    one_minus = -jnp.tanh(log_a) * (a * a + 1.0)
    mult = jnp.sqrt(jnp.maximum(one_minus, EPS))
    pos = row0 + _iota(xc.shape, 0)
    mult = jnp.where(pos == 0, 1.0, mult)
    return a, mult * i * xc


def _lru_post_fn(row0, h, cy):
    return (jax.nn.gelu(cy, approximate=True) * h,)


def _lru_da_fn(row0, g, h_prev):
    return (g * h_prev,)


def _relu2(u):
    r = jnp.maximum(u, 0.0)
    return r * r


def _drelu2(acc, u):
    return acc * (2.0 * jnp.maximum(u, 0.0))


MM_VMEM_BUDGET = 44 * 1024 * 1024


def _mm_tiles(mode, m, n, k, a_bytes, b_bytes, n_extra):
    best = None
    for tk in {k, 4096, 2048, 1024, 512, 256}:
        for tn in {n, 1280, 1024, 768, 512, 256, 128}:
            for tm in {m, 1024, 512, 256, 128}:
                if m % tm or n % tn or k % tk or tm > 1024:
                    continue
                tiles = tm * tk * a_bytes + tk * tn * b_bytes + (1 + n_extra) * tm * tn * 4
                if 2 * tiles + (tm * tn * 4 if tk < k else 0) > MM_VMEM_BUDGET:
                    continue
                a_reads = 1 if tk == k else n // tn
                b_reads = 1 if (tk == k and tn == n) else m // tm
                cost = m * k * a_bytes * a_reads + k * n * b_bytes * b_reads + m * n * 4 * (k // tk - 1)
                if best is None or (cost, -tm) < best[0]:
                    best = ((cost, -tm), (tm, tn, tk))
    if best is None:
        raise ValueError("no matmul tiling fits VMEM")
    return best[1]


def _matmul(name, a, b, mode, a_fn=None, res=None, epi=None, epi_in=None, out_dtype=F32, also=None):
    if mode == "nn":
        (m, k), n = a.shape, b.shape[1]
    elif mode == "nt":
        (m, k), n = a.shape, b.shape[0]
    else:
        (k, m), n = a.shape, b.shape[1]
    n_extra = (res is not None) + (epi_in is not None) + (also is not None)
    tm, tn, tk = _mm_tiles(mode, m, n, k, a.dtype.itemsize, b.dtype.itemsize, n_extra)
    nk = k // tk
    a_spec = {"nn": pl.BlockSpec((tm, tk), lambda i, j, kk: (i, kk)),
              "nt": pl.BlockSpec((tm, tk), lambda i, j, kk: (i, kk)),
              "tn": pl.BlockSpec((tk, tm), lambda i, j, kk: (kk, i))}[mode]
    b_spec = {"nn": pl.BlockSpec((tk, tn), lambda i, j, kk: (kk, j)),
              "nt": pl.BlockSpec((tn, tk), lambda i, j, kk: (j, kk)),
              "tn": pl.BlockSpec((tk, tn), lambda i, j, kk: (kk, j))}[mode]
    o_spec = pl.BlockSpec((tm, tn), lambda i, j, kk: (i, j))
    dims = {"nn": NN, "nt": NT, "tn": TN}[mode]
    extra = [x for x in (res, epi_in) if x is not None]

    def body(*refs):
        a_ref, b_ref = refs[0], refs[1]
        rest = list(refs[2:])
        res_ref = rest.pop(0) if res is not None else None
        epi_ref = rest.pop(0) if epi_in is not None else None
        o_ref = rest.pop(0)
        also_ref = rest.pop(0) if also is not None else None
        acc = rest[0] if nk > 1 else None
        kk = pl.program_id(2)
        at = a_ref[...]
        if a_fn is not None:
            at = a_fn(at.astype(F32))
        part = lax.dot_general(at.astype(BF16), b_ref[...].astype(BF16), (dims, ((), ())),
                               preferred_element_type=F32)

        def finish(out):
            if res_ref is not None:
                out = out + res_ref[...]
            if epi is not None:
                out = epi(out, epi_ref[...])
            o_ref[...] = out.astype(o_ref.dtype)
            if also is not None:
                also_ref[...] = also(out).astype(also_ref.dtype)

        if nk == 1:
            finish(part)
        else:
            @pl.when(kk == 0)
            def _():
                acc[...] = part

            @pl.when(jnp.logical_and(kk > 0, kk < nk - 1))
            def _():
                acc[...] += part

            @pl.when(kk == nk - 1)
            def _():
                finish(acc[...] + part)

    out_shape = [jax.ShapeDtypeStruct((m, n), out_dtype)] + ([jax.ShapeDtypeStruct((m, n), BF16)] if also else [])
    outs = pl.pallas_call(
        body, grid=(m // tm, n // tn, nk),
        in_specs=[a_spec, b_spec] + [o_spec] * len(extra), out_specs=[o_spec] * len(out_shape), out_shape=out_shape,
        scratch_shapes=[pltpu.VMEM((tm, tn), F32)] if nk > 1 else [],
        name=name, compiler_params=_params(("arbitrary", "arbitrary", "arbitrary")))(a, b, *extra)
    return outs if also else outs[0]


HALO = 8


def _conv_fwd(name, x, col0, width, w, bias, t=512):
    s = x.shape[0]
    cb = col0 // width
    hb = t // HALO

    def body(x_ref, halo_ref, w_ref, b_ref, y_ref):
        i = pl.program_id(0)
        halo = jnp.where(i == 0, 0.0, halo_ref[...])
        xp = jnp.concatenate([halo, x_ref[...]], axis=0)
        wv = w_ref[...]
        y = b_ref[...] + wv[0:1] * xp[HALO - 3:HALO - 3 + t]
        for j in range(1, 4):
            y = y + wv[j:j + 1] * xp[HALO - 3 + j:HALO - 3 + j + t]
        y_ref[...] = y

    return pl.pallas_call(
        body, grid=(s // t,),
        in_specs=[pl.BlockSpec((t, width), lambda i: (i, cb)),
                  pl.BlockSpec((HALO, width), lambda i: (jnp.maximum(i * hb - 1, 0), cb)),
                  _whole(w.shape), _whole(bias.shape)],
        out_specs=_rowspec(t, width), out_shape=jax.ShapeDtypeStruct((s, width), F32),
        name=name, compiler_params=_params(("arbitrary",)))(x, x, w, bias)


def _conv_bwd(name, dy, x, col0, width, w, t=512):
    s = x.shape[0]
    cb = col0 // width
    hb = t // HALO
    nblk = s // t

    def body(dy_ref, dyn_ref, x_ref, halo_ref, w_ref, dx_ref, dw_ref, db_ref):
        i = pl.program_id(0)
        dyv = dy_ref[...]
        nxt = jnp.where(i == nblk - 1, 0.0, dyn_ref[...])
        dyp = jnp.concatenate([dyv, nxt], axis=0)
        wv = w_ref[...]
        dx = wv[3:4] * dyv
        for j in range(3):
            dx = dx + wv[j:j + 1] * dyp[3 - j:3 - j + t]
        dx_ref[...] = dx.astype(dx_ref.dtype)
        halo = jnp.where(i == 0, 0.0, halo_ref[...])
        xp = jnp.concatenate([halo, x_ref[...]], axis=0)
        dw = jnp.concatenate(
            [jnp.sum(dyv * xp[HALO - 3 + j:HALO - 3 + j + t], axis=0, keepdims=True) for j in range(4)], axis=0)

        @pl.when(i == 0)
        def _():
            dw_ref[...] = jnp.zeros_like(dw_ref)
            db_ref[...] = jnp.zeros_like(db_ref)

        dw_ref[...] += dw
        db_ref[...] += jnp.sum(dyv, axis=0, keepdims=True)

    return pl.pallas_call(
        body, grid=(nblk,),
        in_specs=[_rowspec(t, width),
                  pl.BlockSpec((HALO, width), lambda i: (jnp.minimum((i + 1) * hb, s // HALO - 1), 0)),
                  pl.BlockSpec((t, width), lambda i: (i, cb)),
                  pl.BlockSpec((HALO, width), lambda i: (jnp.maximum(i * hb - 1, 0), cb)),
                  _whole(w.shape)],
        out_specs=[_rowspec(t, width), _whole((4, width)), _whole((1, width))],
        out_shape=[jax.ShapeDtypeStruct((s, width), BF16), jax.ShapeDtypeStruct((4, width), F32),
                   jax.ShapeDtypeStruct((1, width), F32)],
        name=name, compiler_params=_params(("arbitrary",)))(dy, dy, x, x, w)


def _hgrn_chunk(st, q, k, v, lf):
    cum = _running_sum(lf)
    tot = jnp.sum(lf, axis=0, keepdims=True)
    bm = _iota((A_W, A_W), 0) // A_HD == _iota((A_W, A_W), 1) // A_HD
    bo = bm.astype(F32)
    o_inter = _bdot(q * jnp.exp(cum), st, "nt")
    kd = k * jnp.exp(tot - cum)
    st_new = st * jnp.exp(tot) + jnp.where(bm, _bdot(v, kd, "tn"), 0.0)
    lane_h = _iota((SUB, A_W), 1) // A_HD
    n_h = A_W // A_HD
    t3 = _iota((SUB, SUB, A_W), 0)
    s3 = _iota((SUB, SUB, A_W), 1)
    outs = []
    for i in range(CHUNK // SUB):
        sl = slice(i * SUB, (i + 1) * SUB)
        qi, ki, vi, ci = q[sl], k[sl], v[sl], cum[sl]
        e = jnp.exp(jnp.minimum(ci[:, None, :] - ci[None, :, :], 0.0))
        d3 = jnp.where(s3 <= t3, qi[:, None, :] * ki[None, :, :] * e, 0.0)
        ds = _mdot(d3.reshape(SUB * SUB, A_W), bo, "nn", 1, 1, (True, False)).reshape(SUB, SUB, A_W)
        oi = jnp.sum(ds * vi[None, :, :], axis=1)
        if i > 0:
            n = i * SUB
            cb = cum[n - 1:n]
            qt = qi * jnp.exp(ci - cb)
            kt = k[:n] * jnp.exp(jnp.minimum(cb - cum[:n], 0.0))
            qs = jnp.concatenate([jnp.where(lane_h == h, qt, 0.0) for h in range(n_h)], axis=0)
            p = _bdot(_bdot(qs, kt, "nt"), v[:n])
            for h in range(n_h):
                oi = oi + jnp.where(lane_h == h, p[h * SUB:(h + 1) * SUB], 0.0)
        outs.append(oi)
    return st_new, o_inter + jnp.concatenate(outs, axis=0)


HGRN_CHUNKS = 2


def _hgrn_fwd(name, q, k, v, lf):
    s = q.shape[0]
    n = s // CHUNK
    g = HGRN_CHUNKS

    def body(q_ref, k_ref, v_ref, lf_ref, o_ref, sts_ref, st_ref):
        @pl.when(pl.program_id(0) == 0)
        def _():
            st_ref[...] = jnp.zeros_like(st_ref)

        st = st_ref[...]
        for c in range(g):
            rows = pl.ds(c * CHUNK, CHUNK)
            sts_ref[c] = st
            st, o = _hgrn_chunk(st, q_ref[rows], k_ref[rows], v_ref[rows], lf_ref[rows])
            o_ref[rows] = o
        st_ref[...] = st

    cs = _rowspec(g * CHUNK, A_W)
    return pl.pallas_call(
        body, grid=(n // g,), in_specs=[cs] * 4,
        out_specs=[cs, pl.BlockSpec((g, A_W, A_W), lambda i: (i, 0, 0))],
        out_shape=[jax.ShapeDtypeStruct((s, A_W), F32), jax.ShapeDtypeStruct((n, A_W, A_W), F32)],
        scratch_shapes=[pltpu.VMEM((A_W, A_W), F32)],
        name=name, compiler_params=_params(("arbitrary",)))(q, k, v, lf)


def _hgrn_bwd(name, q, k, v, lf, sts, do):
    s = q.shape[0]
    n = s // CHUNK
    g = HGRN_CHUNKS
    nsteps = n // g

    def body(q_ref, k_ref, v_ref, lf_ref, sts_ref, do_ref, dq_ref, dk_ref, dv_ref, dlf_ref, dst_ref):
        @pl.when(pl.program_id(0) == 0)
        def _():
            dst_ref[...] = jnp.zeros_like(dst_ref)

        dst = dst_ref[...]
        for c in reversed(range(g)):
            rows = pl.ds(c * CHUNK, CHUNK)
            _, vjp = jax.vjp(_hgrn_chunk, sts_ref[c], q_ref[rows], k_ref[rows], v_ref[rows], lf_ref[rows])
            ct = vjp((dst, do_ref[rows]))
            dst = ct[0]
            for r, gi in zip((dq_ref, dk_ref, dv_ref, dlf_ref), ct[1:]):
                r[rows] = gi
        dst_ref[...] = dst

    rs = pl.BlockSpec((g * CHUNK, A_W), lambda i: (nsteps - 1 - i, 0))
    return pl.pallas_call(
        body, grid=(nsteps,),
        in_specs=[rs] * 4 + [pl.BlockSpec((g, A_W, A_W), lambda i: (nsteps - 1 - i, 0, 0)), rs],
        out_specs=[rs] * 4, out_shape=[jax.ShapeDtypeStruct((s, A_W), F32)] * 4,
        scratch_shapes=[pltpu.VMEM((A_W, A_W), F32)],
        name=name, compiler_params=_params(("arbitrary",)))(q, k, v, lf, sts, do)


PREP_CHUNKS = 4
REC_CHUNKS = 4
DEC_ROWS = 8


def _to_heads(x, rows=CHUNK, width=B_HD):
    n = x.shape[0] // rows
    x3 = x.reshape(n, rows, x.shape[1])
    return jnp.concatenate([x3[:, :, h * B_HD:h * B_HD + width] for h in range(B_NH)], axis=0)


def _from_heads(y):
    n = y.shape[0] // B_NH
    x3 = jnp.concatenate([y[h * n:(h + 1) * n] for h in range(B_NH)], axis=2)
    return x3.reshape(n * y.shape[1], B_NH * B_HD)


@jax.custom_vjp
def _unit_lower_inverse(a_mat):
    nil = -a_mat
    eye = (_iota(a_mat.shape, 1) == _iota(a_mat.shape, 2)).astype(F32)
    inv = eye + nil
    sq = 2
    while sq < CHUNK:
        nil = _split_dot(nil, nil, "nn", 2, 2)
        inv = inv + _split_dot(inv, nil, "nn", 2, 2)
        sq *= 2
    return inv


def _unit_lower_inverse_fwd(a_mat):
    inv = _unit_lower_inverse(a_mat)
    return inv, inv


def _unit_lower_inverse_bwd(inv, ct):
    return (-_split_dot(_split_dot(inv, ct, "tn", 2, 2), inv, "nt", 2, 2),)


_unit_lower_inverse.defvjp(_unit_lower_inverse_fwd, _unit_lower_inverse_bwd)


@jax.custom_vjp
def _known_inverse(a_mat, inv):
    return inv


def _known_inverse_fwd(a_mat, inv):
    return inv, inv


def _known_inverse_bwd(inv, ct):
    return _unit_lower_inverse_bwd(inv, ct)[0], jnp.zeros_like(inv)


_known_inverse.defvjp(_known_inverse_fwd, _known_inverse_bwd)


def _gdn_prep(q, k, v, beta, la, inv_saved=None):
    q, k, v, beta, la = [_to_heads(z) for z in (q, k, v, beta, la)]
    nb = q.shape[0]
    r = _iota((nb, CHUNK, CHUNK), 1)
    c = _iota((nb, CHUNK, CHUNK), 2)
    causal = c <= r
    g = _running_sum(la)
    tot = jnp.sum(la, axis=1, keepdims=True)
    gcol = g[:, :, :CHUNK]
    decay = jnp.where(causal, jnp.exp(jnp.where(causal, gcol - jnp.swapaxes(gcol, 1, 2), 0.0)), 0.0)
    kb = k * beta
    a_mat = jnp.where(c < r, _bdot(kb, k, "nt") * decay, 0.0)
    if inv_saved is None:
        inv = _unit_lower_inverse(a_mat)
    else:
        inv = _known_inverse(a_mat, _to_heads(inv_saved, width=CHUNK))
    eg = jnp.exp(g)
    u = _bdot(inv, v * beta)
    w = _bdot(inv, kb * eg)
    qk = jnp.where(causal, _bdot(q, k, "nt") * decay, 0.0)
    widen = lambda z: jnp.concatenate([z, jnp.zeros((nb, CHUNK, B_HD - CHUNK), F32)], axis=2)
    dec = jnp.broadcast_to(jnp.exp(tot), (nb, DEC_ROWS, B_HD))
    return tuple(_from_heads(z) for z in (u, w, widen(qk), q * eg, k * jnp.exp(tot - g), dec, widen(inv)))


def _gdn_rec(st, u, w, qk, qd, kd, dec):
    u, w, qd, kd = [_to_heads(z) for z in (u, w, qd, kd)]
    qk = _to_heads(qk, width=CHUNK)
    v_new = u - _bdot(w, st)
    o = _bdot(qd, st) + _bdot(qk, v_new)
    st_new = st * _to_heads(dec, rows=DEC_ROWS)[:, :1, :1] + _bdot(kd, v_new, "tn")
    return st_new, _from_heads(o)


def _gdn_prep_fwd(name, q, k, v, beta, la):
    s = q.shape[0]
    t = PREP_CHUNKS * CHUNK
    td = PREP_CHUNKS * DEC_ROWS

    def body(*refs):
        outs = _gdn_prep(*[r[...] for r in refs[:5]])
        for o_ref, o in zip(refs[5:], outs):
            o_ref[...] = o

    rows, dec_rows = jax.ShapeDtypeStruct((s, B_W), F32), jax.ShapeDtypeStruct((s // CHUNK * DEC_ROWS, B_W), F32)
    return pl.pallas_call(
        body, grid=(s // t,), in_specs=[_rowspec(t, B_W)] * 5,
        out_specs=[_rowspec(t, B_W)] * 5 + [_rowspec(td, B_W), _rowspec(t, B_W)],
        out_shape=[rows] * 5 + [dec_rows, rows],
        name=name, compiler_params=_params(("arbitrary",)))(q, k, v, beta, la)


def _gdn_prep_bwd(name, q, k, v, beta, la, inv, cts):
    s = q.shape[0]
    t = PREP_CHUNKS * CHUNK
    td = PREP_CHUNKS * DEC_ROWS

    def body(*refs):
        inv_saved = refs[5][...]
        _, vjp = jax.vjp(lambda *a: _gdn_prep(*a, inv_saved=inv_saved)[:6], *[r[...] for r in refs[:5]])
        g = vjp(tuple(r[...] for r in refs[6:12]))
        for o_ref, o in zip(refs[12:], g):
            o_ref[...] = o

    return pl.pallas_call(
        body, grid=(s // t,), in_specs=[_rowspec(t, B_W)] * 11 + [_rowspec(td, B_W)],
        out_specs=[_rowspec(t, B_W)] * 5, out_shape=[jax.ShapeDtypeStruct((s, B_W), F32)] * 5,
        name=name, compiler_params=_params(("arbitrary",)))(q, k, v, beta, la, inv, *cts)


def _gdn_rec_fwd(name, u, w, qk, qd, kd, dec):
    s = u.shape[0]
    n = s // CHUNK
    g = REC_CHUNKS

    def body(u_ref, w_ref, qk_ref, qd_ref, kd_ref, dec_ref, o_ref, sts_ref, st_ref):
        @pl.when(pl.program_id(0) == 0)
        def _():
            st_ref[...] = jnp.zeros_like(st_ref)

        st = st_ref[...]
        for c in range(g):
            rows, drows = pl.ds(c * CHUNK, CHUNK), pl.ds(c * DEC_ROWS, DEC_ROWS)
            sts_ref[c] = st
            st, o = _gdn_rec(st, u_ref[rows], w_ref[rows], qk_ref[rows], qd_ref[rows], kd_ref[rows], dec_ref[drows])
            o_ref[rows] = o
        st_ref[...] = st

    cs = _rowspec(g * CHUNK, B_W)
    return pl.pallas_call(
        body, grid=(n // g,), in_specs=[cs] * 5 + [_rowspec(g * DEC_ROWS, B_W)],
        out_specs=[cs, pl.BlockSpec((g, B_NH, B_HD, B_HD), lambda i: (i, 0, 0, 0))],
        out_shape=[jax.ShapeDtypeStruct((s, B_W), F32), jax.ShapeDtypeStruct((n, B_NH, B_HD, B_HD), F32)],
        scratch_shapes=[pltpu.VMEM((B_NH, B_HD, B_HD), F32)],
        name=name, compiler_params=_params(("arbitrary",)))(u, w, qk, qd, kd, dec)


def _gdn_rec_bwd(name, u, w, qk, qd, kd, dec, sts, do):
    s = u.shape[0]
    n = s // CHUNK
    g = REC_CHUNKS
    nsteps = n // g

    def body(u_ref, w_ref, qk_ref, qd_ref, kd_ref, dec_ref, sts_ref, do_ref,
             du_ref, dw_ref, dqk_ref, dqd_ref, dkd_ref, ddec_ref, dst_ref):
        @pl.when(pl.program_id(0) == 0)
        def _():
            dst_ref[...] = jnp.zeros_like(dst_ref)

        dst = dst_ref[...]
        for c in reversed(range(g)):
            rows, drows = pl.ds(c * CHUNK, CHUNK), pl.ds(c * DEC_ROWS, DEC_ROWS)
            _, vjp = jax.vjp(_gdn_rec, sts_ref[c], u_ref[rows], w_ref[rows], qk_ref[rows], qd_ref[rows], kd_ref[rows],
                             dec_ref[drows])
            ct = vjp((dst, do_ref[rows]))
            dst = ct[0]
            for r, gi in zip((du_ref, dw_ref, dqk_ref, dqd_ref, dkd_ref), ct[1:6]):
                r[rows] = gi
            ddec_ref[drows] = ct[6]
        dst_ref[...] = dst

    rs = pl.BlockSpec((g * CHUNK, B_W), lambda i: (nsteps - 1 - i, 0))
    ds = pl.BlockSpec((g * DEC_ROWS, B_W), lambda i: (nsteps - 1 - i, 0))
    return pl.pallas_call(
        body, grid=(nsteps,),
        in_specs=[rs] * 5 + [ds, pl.BlockSpec((g, B_NH, B_HD, B_HD), lambda i: (nsteps - 1 - i, 0, 0, 0)), rs],
        out_specs=[rs] * 5 + [ds],
        out_shape=[jax.ShapeDtypeStruct((s, B_W), F32)] * 5 + [jax.ShapeDtypeStruct((n * DEC_ROWS, B_W), F32)],
        scratch_shapes=[pltpu.VMEM((B_NH, B_HD, B_HD), F32)],
        name=name, compiler_params=_params(("arbitrary",)))(u, w, qk, qd, kd, dec, sts, do)


SCAN_T = 512


def _block_scan(a, b, reverse):
    t = a.shape[0]
    rows = _iota(a.shape, 0)
    sft = 1
    while sft < t:
        if reverse:
            a_s, b_s = pltpu.roll(a, t - sft, 0), pltpu.roll(b, t - sft, 0)
            edge = rows >= t - sft
        else:
            a_s, b_s = pltpu.roll(a, sft, 0), pltpu.roll(b, sft, 0)
            edge = rows < sft
        a_s = jnp.where(edge, 1.0, a_s)
        b_s = jnp.where(edge, 0.0, b_s)
        b = a * b_s + b
        a = a * a_s
        sft *= 2
    return a, b


def _lru_scan_fwd(name, a, b):
    s, w = a.shape
    t = SCAN_T

    def body(a_ref, b_ref, h_ref, hp_ref, carry):
        @pl.when(pl.program_id(0) == 0)
        def _():
            carry[...] = jnp.zeros_like(carry)

        h_in = carry[0:1]
        ca, cb = _block_scan(a_ref[...], b_ref[...], False)
        h = ca * h_in + cb
        h_ref[...] = h
        hp_ref[...] = jnp.where(_iota(h.shape, 0) == 0, h_in, pltpu.roll(h, 1, 0))
        carry[...] = jnp.broadcast_to(h[t - 1:t], carry.shape)

    return pl.pallas_call(
        body, grid=(s // t,), in_specs=[_rowspec(t, w)] * 2, out_specs=[_rowspec(t, w)] * 2,
        out_shape=[jax.ShapeDtypeStruct((s, w), F32)] * 2, scratch_shapes=[pltpu.VMEM((8, w), F32)],
        name=name, compiler_params=_params(("arbitrary",)))(a, b)


def _lru_scan_bwd(name, a, dh):
    s, w = a.shape
    t = SCAN_T
    n = s // t

    def body(a_ref, dh_ref, g_ref, a_next, g_next):
        @pl.when(pl.program_id(0) == 0)
        def _():
            a_next[...] = jnp.zeros_like(a_next)
            g_next[...] = jnp.zeros_like(g_next)

        av = a_ref[...]
        a_up = jnp.where(_iota(av.shape, 0) == t - 1, a_next[0:1], pltpu.roll(av, t - 1, 0))
        ca, cb = _block_scan(a_up, dh_ref[...], True)
        g = ca * g_next[0:1] + cb
        g_ref[...] = g
        a_next[...] = jnp.broadcast_to(av[0:1], a_next.shape)
        g_next[...] = jnp.broadcast_to(g[0:1], g_next.shape)

    rs = pl.BlockSpec((t, w), lambda i: (n - 1 - i, 0))
    return pl.pallas_call(
        body, grid=(n,), in_specs=[rs] * 2, out_specs=rs, out_shape=jax.ShapeDtypeStruct((s, w), F32),
        scratch_shapes=[pltpu.VMEM((8, w), F32), pltpu.VMEM((8, w), F32)],
        name=name, compiler_params=_params(("arbitrary",)))(a, dh)


def _loss_fn(g, x, tgt):
    y = x * lax.rsqrt(jnp.mean(x * x, axis=-1, keepdims=True) + EPS) * g
    err = y - tgt
    return 0.5 * jnp.sum(jnp.mean(err * err, axis=-1, keepdims=True), axis=0, keepdims=True)


def _loss_and_grad(name, g, x, tgt, t=256):
    s, d = x.shape

    def body(g_ref, x_ref, t_ref, loss_ref, dx_ref, dg_ref, dx16_ref):
        val, vjp = jax.vjp(_loss_fn, g_ref[...], x_ref[...], t_ref[...])
        dg, dx, _ = vjp(jnp.ones((1, 1), F32))

        @pl.when(pl.program_id(0) == 0)
        def _():
            loss_ref[...] = jnp.zeros_like(loss_ref)
            dg_ref[...] = jnp.zeros_like(dg_ref)

        loss_ref[...] += jnp.broadcast_to(val, loss_ref.shape)
        dg_ref[...] += dg
        dx_ref[...] = dx
        dx16_ref[...] = dx.astype(BF16)

    return pl.pallas_call(
        body, grid=(s // t,), in_specs=[_whole(g.shape), _rowspec(t, d), _rowspec(t, d)],
        out_specs=[_whole((1, 128)), _rowspec(t, d), _whole(g.shape), _rowspec(t, d)],
        out_shape=[jax.ShapeDtypeStruct((1, 128), F32), jax.ShapeDtypeStruct((s, d), F32),
                   jax.ShapeDtypeStruct(g.shape, F32), jax.ShapeDtypeStruct((s, d), BF16)],
        name=name, compiler_params=_params(("arbitrary",)))(g, x, tgt)


def _ew(name, fn, ins, n_out, t=None):
    r, c = ins[0].shape
    t = _tile(r, (512, 256, 128, 64, 32, 16, 8)) if t is None else t

    def body(*refs):
        outs = fn(*[x[...] for x in refs[:len(ins)]])
        for o_ref, o in zip(refs[len(ins):], outs):
            o_ref[...] = o

    return pl.pallas_call(
        body, grid=(r // t,), in_specs=[_rowspec(t, c)] * len(ins), out_specs=[_rowspec(t, c)] * n_out,
        out_shape=[jax.ShapeDtypeStruct((r, c), F32)] * n_out,
        name=name, compiler_params=_params(("arbitrary",)))(*ins)


def _adam_fn(w, g, m, v):
    m = B1 * m + (1.0 - B1) * g
    v = B2 * v + (1.0 - B2) * (g * g)
    m_hat = m / (1.0 - B1 ** STEP)
    v_hat = v / (1.0 - B2 ** STEP)
    delta = -LR * (m_hat / (jnp.sqrt(v_hat) + AEPS) + WD * w)
    return delta, m, v


def _add2_fn(a, b):
    return (a + b,)


def _adam(name, w, g, m, v):
    shp = w.shape
    two = lambda z: z.reshape(-1, shp[-1])
    outs = _ew(name, _adam_fn, [two(w), two(g), two(m), two(v)], 3)
    return [o.reshape(shp) for o in outs]


def _swap(name, bufs, flips):
    n = len(bufs)
    hbm = pl.BlockSpec(memory_space=pltpu.HBM)

    def body(*refs):
        srcs, dsts = refs[:n], refs[n:2 * n]
        send_sems, recv_sems = refs[2 * n], refs[2 * n + 1]
        me = (lax.axis_index("x"), lax.axis_index("y"), lax.axis_index("c"))
        copies = []
        for i in range(n):
            peer = tuple(1 - m if f else m for m, f in zip(me, flips[i]))
            cp = pltpu.make_async_remote_copy(src_ref=srcs[i], dst_ref=dsts[i], send_sem=send_sems.at[i],
                                              recv_sem=recv_sems.at[i], device_id=peer, device_id_type=MESH)
            cp.start()
            copies.append(cp)
        for cp in copies:
            cp.wait()

    return pl.pallas_call(
        body, in_specs=[hbm] * n, out_specs=[hbm] * n,
        out_shape=[jax.ShapeDtypeStruct(b.shape, b.dtype) for b in bufs],
        scratch_shapes=[pltpu.SemaphoreType.DMA((n,)), pltpu.SemaphoreType.DMA((n,))],
        name=name, compiler_params=pltpu.CompilerParams(has_side_effects=True))(*bufs)


FLIP_C = (False, False, True)
FLIP_Y = (False, True, False)
FLIP_X = (True, False, False)


LAYER_ROWS = PACK_ROWS // DEPTH
HALF_ROWS = LAYER_ROWS // 2
SUM_ROWS = 128
GATHER_IDS = (1, 2, 3, 4)
SCATTER_IDS = (5, 6, 7, 8)


def _mesh_pos():
    ix, iy, core = lax.axis_index("x"), lax.axis_index("y"), lax.axis_index("c")
    others = [(ix, 1 - iy), (1 - ix, iy), (1 - ix, 1 - iy)]
    return ix, iy, core, others


def _pack_big(w_in, w_out, w_up, w_down):
    pad = jnp.pad(w_in, ((0, 0), (0, 0), (0, 1024 - w_in.shape[-1])))
    return jnp.concatenate([pad, w_up, w_down, w_out], axis=1)


def _unpack_big(p):
    w_in = p[..., 0:1024, :D_IN // 4]
    return w_in, p[..., 3072:LAYER_ROWS, :], p[..., 1024:2048, :], p[..., 2048:3072, :]


def _gather_layer(packed, collective_id):
    src = jax.new_ref(packed, memory_space=pltpu.MemorySpace.HBM)
    out = jax.empty_ref(jax.ShapeDtypeStruct((4,) + packed.shape, packed.dtype), memory_space=pltpu.MemorySpace.HBM)
    dma = pltpu.SemaphoreType.DMA

    @pl.kernel(mesh=plsc.ScalarSubcoreMesh(axis_name="seq", num_cores=1), name="gather_%d" % collective_id,
               scratch_types=(dma,) * 12, compiler_params=pltpu.CompilerParams(collective_id=collective_id))
    def launch(*sems):
        send_sems, recv_sems = sems[:6], sems[6:]
        ix, iy, core, others = _mesh_pos()
        barrier = pltpu.get_barrier_semaphore()
        for peer in [(ox, oy, core) for ox, oy in others] + [(ix, iy, 1 - core)]:
            pl.semaphore_signal(barrier, inc=1, device_id=peer, device_id_type=MESH)
        pl.semaphore_wait(barrier, 4)
        mine = pl.ds(core * HALF_ROWS, HALF_ROWS)
        theirs = pl.ds((1 - core) * HALF_ROWS, HALF_ROWS)

        def copy(k, src_ref, owner, rows, to):
            return pltpu.make_async_remote_copy(
                src_ref=src_ref, dst_ref=out.at[owner, rows], send_sem=send_sems[k], recv_sem=recv_sems[k],
                device_id=to, device_id_type=MESH)

        chip = 2 * ix + iy
        first = [copy(j, src.at[mine], chip, mine, (*others[j], core)) for j in range(3)]
        for cp in first:
            cp.start()
        passed = []
        for j, (ox, oy) in enumerate(others):
            owner = 2 * ox + oy
            copy(j, src.at[mine], owner, mine, (ix, iy, core)).wait_recv()
            fwd = copy(3 + j, out.at[owner, mine], owner, mine, (ix, iy, 1 - core))
            fwd.start()
            passed.append(fwd)
        for j, (ox, oy) in enumerate(others):
            copy(3 + j, src.at[theirs], 2 * ox + oy, theirs, (ix, iy, core)).wait_recv()
        for cp in first + passed:
            cp.wait_send()

    launch()
    return out[...]


def _rs_sibling(per_chip):
    hbm = pl.BlockSpec(memory_space=pltpu.HBM)

    def body(src, out, send_sem, recv_sem):
        ix, iy, core, _ = _mesh_pos()
        give = pl.ds((1 - core) * HALF_ROWS, HALF_ROWS)
        cp = pltpu.make_async_remote_copy(src_ref=src.at[pl.ds(0, 4), give], dst_ref=out, send_sem=send_sem,
                                          recv_sem=recv_sem, device_id=(ix, iy, 1 - core), device_id_type=MESH)
        cp.start()
        cp.wait()

    return pl.pallas_call(
        body, in_specs=[hbm], out_specs=hbm, out_shape=jax.ShapeDtypeStruct((4, HALF_ROWS, 1024), F32),
        scratch_shapes=[pltpu.SemaphoreType.DMA, pltpu.SemaphoreType.DMA],
        name="rs_sibling", compiler_params=pltpu.CompilerParams(has_side_effects=True))(per_chip)


def _rs_add_pair(per_chip, sib, chip, core):
    nb = HALF_ROWS // SUM_ROWS
    blk = (1, SUM_ROWS, 1024)

    def body(chip_ref, core_ref, a_ref, b_ref, own_ref, send_ref):
        total = a_ref[0] + b_ref[0]
        send_ref[0] = total.astype(BF16)

        @pl.when(pl.program_id(1) == chip_ref[0])
        def _():
            own_ref[...] = total

    return pl.pallas_call(
        body,
        grid_spec=pltpu.PrefetchScalarGridSpec(
            num_scalar_prefetch=2, grid=(nb, 4),
            in_specs=[pl.BlockSpec(blk, lambda i, k, ch, co: (k, co[0] * nb + i, 0)),
                      pl.BlockSpec(blk, lambda i, k, ch, co: (k, i, 0))],
            out_specs=[pl.BlockSpec((SUM_ROWS, 1024), lambda i, k, ch, co: (i, 0)),
                       pl.BlockSpec(blk, lambda i, k, ch, co: (k, i, 0))]),
        out_shape=[jax.ShapeDtypeStruct((HALF_ROWS, 1024), F32), jax.ShapeDtypeStruct((4, HALF_ROWS, 1024), BF16)],
        name="rs_add_pair", compiler_params=_params(("arbitrary", "arbitrary")))(
            chip.reshape(1), core.reshape(1), per_chip, sib)


def _rs_chips(pair, collective_id):
    src_ref = jax.new_ref(pair, memory_space=pltpu.MemorySpace.HBM)
    got_ref = jax.empty_ref(jax.ShapeDtypeStruct((3,) + pair.shape[1:], pair.dtype), memory_space=pltpu.MemorySpace.HBM)
    dma = pltpu.SemaphoreType.DMA

    @pl.kernel(mesh=plsc.ScalarSubcoreMesh(axis_name="seq", num_cores=1), name="rs_chips_%d" % collective_id,
               scratch_types=(dma,) * 6, compiler_params=pltpu.CompilerParams(collective_id=collective_id))
    def launch(s0, s1, s2, r0, r1, r2):
        ix, iy, core, others = _mesh_pos()
        barrier = pltpu.get_barrier_semaphore()
        for ox, oy in others:
            pl.semaphore_signal(barrier, inc=1, device_id=(ox, oy, core), device_id_type=MESH)
        pl.semaphore_wait(barrier, 3)
        copies = []
        for j, ((ox, oy), send_sem, recv_sem) in enumerate(zip(others, (s0, s1, s2), (r0, r1, r2))):
            cp = pltpu.make_async_remote_copy(src_ref=src_ref.at[2 * ox + oy], dst_ref=got_ref.at[j], send_sem=send_sem,
                                              recv_sem=recv_sem, device_id=(ox, oy, core), device_id_type=MESH)
            cp.start()
            copies.append(cp)
        for cp in copies:
            cp.wait()

    launch()
    return got_ref[...]


def _rs_add_chips(own, got, core, layer, buf):
    nb = HALF_ROWS // SUM_ROWS
    blk = (1, SUM_ROWS, 1024)

    def body(core_ref, a_ref, g0_ref, g1_ref, g2_ref, *rest):
        o_ref = rest[-1]
        o_ref[0, 0] = ((a_ref[...] + g0_ref[0].astype(F32)) + g1_ref[0].astype(F32)) + g2_ref[0].astype(F32)

    aliased = [] if buf is None else [buf]
    return pl.pallas_call(
        body,
        grid_spec=pltpu.PrefetchScalarGridSpec(
            num_scalar_prefetch=1, grid=(nb,),
            in_specs=[pl.BlockSpec((SUM_ROWS, 1024), lambda i, c: (i, 0))]
            + [pl.BlockSpec(blk, functools.partial(lambda j, i, c: (j, i, 0), j)) for j in range(3)]
            + [pl.BlockSpec(memory_space=pl.ANY)] * len(aliased),
            out_specs=pl.BlockSpec((1, 1, SUM_ROWS, 1024), lambda i, c: (layer, c[0], i, 0))),
        out_shape=jax.ShapeDtypeStruct((DEPTH, 2, HALF_ROWS, 1024), F32),
        input_output_aliases={} if buf is None else {5: 0},
        name="rs_add_chips", compiler_params=_params(("arbitrary",)))(core.reshape(1), own, got, got, got, *aliased)


def _rs_share(buf):
    hbm = pl.BlockSpec(memory_space=pltpu.HBM)

    def body(src, out, send_sem, recv_sem):
        ix, iy, core, _ = _mesh_pos()
        layers = pl.ds(0, DEPTH)
        cp = pltpu.make_async_remote_copy(src_ref=src.at[layers, core], dst_ref=out.at[layers, core], send_sem=send_sem,
                                          recv_sem=recv_sem, device_id=(ix, iy, 1 - core), device_id_type=MESH)
        cp.start()
        pltpu.make_async_remote_copy(src_ref=src.at[layers, 1 - core], dst_ref=out.at[layers, 1 - core],
                                     send_sem=send_sem, recv_sem=recv_sem, device_id=(ix, iy, core),
                                     device_id_type=MESH).wait_recv()
        cp.wait_send()

    return pl.pallas_call(
        body, in_specs=[hbm], out_specs=hbm, out_shape=jax.ShapeDtypeStruct(buf.shape, buf.dtype),
        input_output_aliases={0: 0}, scratch_shapes=[pltpu.SemaphoreType.DMA, pltpu.SemaphoreType.DMA],
        name="rs_share", compiler_params=pltpu.CompilerParams(has_side_effects=True))(buf)


def _scatter_start(per_chip, chip, core, layer):
    own, pair = _rs_add_pair(per_chip, _rs_sibling(per_chip), chip, core)
    return own, _rs_chips(pair, SCATTER_IDS[layer])


def _allreduce_small(tag, buf):
    for name, flip in (("c", FLIP_C), ("y", FLIP_Y), ("x", FLIP_X)):
        (other,) = _swap(tag + "_swap_" + name, [buf], [flip])
        (buf,) = _ew(tag + "_add_" + name, _add2_fn, [buf, other], 1)
    return buf


def _permute_in_cols(w):
    z = jnp.zeros(w.shape[:-1] + (D_IN_PAD - D_IN,), w.dtype)
    return jnp.concatenate([w[..., 1024:3072], w[..., :1024], w[..., 3080:D_IN], w[..., 3072:3080], z], axis=-1)


IN_COL_SEGMENTS = ((0, 1024, 2048), (1024, 3072, -1024), (3072, 3080, GATE0 - 3072), (3080, D_IN, COL_CX - 3080))


def _orig_cols(w_pad, lo, hi):
    parts = []
    for first, last, shift in IN_COL_SEGMENTS:
        a, b = max(lo, first), min(hi, last)
        if a < b:
            parts.append(w_pad[..., a + shift:b + shift])
    return parts


def _block_diag(w):
    n, d = w.shape[0], w.shape[1]
    on_diag = jnp.arange(n)[:, None] == jnp.arange(n)[None, :]
    return jnp.where(on_diag[:, None, :, None], w[:, :, None, :], 0.0).reshape(n * d, n * d)


def _diag_blocks(w):
    return jnp.stack([w[64 * n:64 * (n + 1), 64 * n:64 * (n + 1)] for n in range(4)], axis=0)


def _gate_row(v):
    return jnp.concatenate([jnp.zeros((4,), F32), v, jnp.zeros((120,), F32)]).reshape(1, 128)


SMALL = ("norm1_g", "hgrn_lb_logits", "hgrn_norm_g", "gdn_conv_w", "gdn_a_log", "gdn_dt_bias", "gdn_norm_g",
         "lru_conv_w", "lru_conv_b", "lru_w_a", "lru_b_a", "lru_w_x", "lru_b_x", "lru_lambda", "norm2_g",
         "final_norm_g")


def _flatten_small(tree):
    flat = jnp.concatenate([tree[k].reshape(-1) for k in sorted(tree)])
    rows = -(-flat.shape[0] // (512 * 128)) * 512
    return jnp.pad(flat, (0, rows * 128 - flat.shape[0])).reshape(rows, 128)


def _unflatten_small(buf, shapes):
    flat = buf.reshape(-1)
    out, off = {}, 0
    for k in sorted(shapes):
        size = 1
        for d in shapes[k]:
            size *= d
        out[k] = flat[off:off + size].reshape(shapes[k])
        off += size
    return out


def _layer_fwd(l, x, p, w_in, w_out, w_up, w_down):
    sv = {"x0": x}
    (h,) = _rowwise_fwd("norm1", _rmsnorm_fn, [p["norm1_g"]], [x], [D_MODEL], [BF16], 512)
    proj = _matmul("proj_in", h, w_in, "nn")
    sv["h"], sv["proj"] = h, proj
    aq, af, ag = _cols(proj, COL_AQ, A_W), _cols(proj, COL_AF, A_W), _cols(proj, COL_AG, A_W)
    ai = lax.slice_in_dim(proj, COL_AI, COL_AI + A_W, axis=1)
    bz, gate, cy = _cols(proj, COL_BZ, B_W), _cols(proj, GATE0, 128), _cols(proj, COL_CY, C_W)
    q, k, lf = _rowwise_fwd("hgrn_pre", functools.partial(_hgrn_pre_fn, l), [p["lb_logits"]], [aq, af],
                            [A_W] * 3, [F32] * 3, 512)
    o_a, sts_a = _hgrn_fwd("hgrn_chunk", q, k, ai, lf)
    (y_a,) = _rowwise_fwd("hgrn_post", _hgrn_post_fn, [p["hgrn_norm_g"]], [o_a, ag], [A_W], [BF16], 512)
    sv.update(aq=aq, af=af, ai=ai, ag=ag, hq=q, hk=k, hlf=lf, o_a=o_a, sts_a=sts_a)
    qkvc = _conv_fwd("gdn_conv", proj, 0, 3 * B_W, p["gdn_conv_w"], jnp.zeros((1, 3 * B_W), F32), 256)
    gq, gk, gv, gb, gla = _rowwise_fwd("gdn_pre", _gdn_pre_fn, [p["gdn_a_log"], p["gdn_dt_bias"]], [qkvc, gate],
                                       [B_W] * 5, [F32] * 5, 256)
    *wy, sv["gdn_inv"] = _gdn_prep_fwd("gdn_prep", gq, gk, gv, gb, gla)
    o_b, sts_b = _gdn_rec_fwd("gdn_rec", *wy)
    sv["wy"] = wy
    (y_b,) = _rowwise_fwd("gdn_post", _gdn_post_fn, [p["gdn_norm_g"]], [o_b, bz], [B_W], [BF16], 512)
    sv.update(qkvc=qkvc, gate=gate, bz=bz, gq=gq, gk=gk, gv=gv, gb=gb, gla=gla, o_b=o_b, sts_b=sts_b)
    xc = _conv_fwd("lru_conv", proj, COL_CX, C_W, p["lru_conv_w"], p["lru_conv_b"], 512)
    a, b = _rowwise_fwd("lru_pre", _lru_pre_fn, [p["lru_w_a"], p["lru_b_a"], p["lru_w_x"], p["lru_b_x"], p["lru_lambda"]],
                        [xc], [C_W] * 2, [F32] * 2, 512)
    hs, h_prev = _lru_scan_fwd("lru_scan", a, b)
    (y_c,) = _rowwise_fwd("lru_post", _lru_post_fn, [], [hs, cy], [C_W], [BF16], 512)
    sv.update(xc=xc, la=a, hs=hs, h_prev=h_prev, cy=cy)
    mixed = jnp.concatenate([y_b, y_a, y_c], axis=1)
    x1 = _matmul("proj_out", mixed, w_out, "nn", res=x)
    (h2,) = _rowwise_fwd("norm2", _rmsnorm_fn, [p["norm2_g"]], [x1], [D_MODEL], [BF16], 512)
    up, act = _matmul("mlp_up", h2, w_up, "nn", also=_relu2)
    x2 = _matmul("mlp_down", act, w_down, "nn", res=x1)
    sv.update(mixed=mixed, x1=x1, h2=h2, up=up, act=act)
    return x2, sv


def _layer_bwd(l, dx, dx16, p, sv, w_in, w_out, w_up, w_down):
    gs = {}
    dup = _matmul("mlp_down_dx", dx16, w_down, "nt", epi=_drelu2, epi_in=sv["up"], out_dtype=BF16)
    d_w_down = _matmul("mlp_down_dw", sv["act"], dx16, "tn")
    dh2 = _matmul("mlp_up_dx", dup, w_up, "nt")
    d_w_up = _matmul("mlp_up_dw", sv["h2"], dup, "tn")
    (gs["norm2_g"],), (dx1, dx1_16) = _rowwise_bwd("norm2_bwd", _rmsnorm_fn, [p["norm2_g"]], [sv["x1"]], [dh2], 512,
                                                   res=dx, copy16=True)
    dmixed = _matmul("proj_out_dx", dx1_16, w_out, "nt")
    d_w_out = _matmul("proj_out_dw", sv["mixed"], dx1_16, "tn")
    dy_b, dy_a, dy_c = _cols(dmixed, 0, B_W), _cols(dmixed, B_W, A_W), _cols(dmixed, B_W + A_W, C_W)
    (gs["hgrn_norm_g"],), (do_a, dag) = _rowwise_bwd("hgrn_post_bwd", _hgrn_post_fn, [p["hgrn_norm_g"]],
                                                     [sv["o_a"], sv["ag"]], [dy_a], 512, dtypes=[F32, BF16])
    dq, dk, dai, dlf = _hgrn_bwd("hgrn_chunk_bwd", sv["hq"], sv["hk"], sv["ai"], sv["hlf"], sv["sts_a"], do_a)
    (gs["lb_logits"],), (daq, daf) = _rowwise_bwd("hgrn_pre_bwd", functools.partial(_hgrn_pre_fn, l), [p["lb_logits"]],
                                                  [sv["aq"], sv["af"]], [dq, dk, dlf], 512, dtypes=[BF16, BF16])
    (gs["gdn_norm_g"],), (do_b, dbz) = _rowwise_bwd("gdn_post_bwd", _gdn_post_fn, [p["gdn_norm_g"]],
                                                    [sv["o_b"], sv["bz"]], [dy_b], 512, dtypes=[F32, BF16])
    d_wy = _gdn_rec_bwd("gdn_rec_bwd", *sv["wy"], sv["sts_b"], do_b)
    dgq, dgk, dgv, dgb, dgla = _gdn_prep_bwd("gdn_prep_bwd", sv["gq"], sv["gk"], sv["gv"], sv["gb"], sv["gla"],
                                             sv["gdn_inv"], d_wy)
    (gs["gdn_a_log"], gs["gdn_dt_bias"]), (dqkvc, dgate) = _rowwise_bwd(
        "gdn_pre_bwd", _gdn_pre_fn, [p["gdn_a_log"], p["gdn_dt_bias"]], [sv["qkvc"], sv["gate"]],
        [dgq, dgk, dgv, dgb, dgla], 256, dtypes=[F32, BF16])
    dqkv, gs["gdn_conv_w"], _ = _conv_bwd("gdn_conv_bwd", dqkvc, sv["proj"], 0, 3 * B_W, p["gdn_conv_w"], 256)
    _, (dhs, dcy) = _rowwise_bwd("lru_post_bwd", _lru_post_fn, [], [sv["hs"], sv["cy"]], [dy_c], 512, dtypes=[F32, BF16])
    g = _lru_scan_bwd("lru_scan_bwd", sv["la"], dhs)
    (da,) = _rowwise_fwd("lru_da", _lru_da_fn, [], [g, sv["h_prev"]], [C_W], [F32], 512)
    lru_params = [p["lru_w_a"], p["lru_b_a"], p["lru_w_x"], p["lru_b_x"], p["lru_lambda"]]
    dps, (dxc,) = _rowwise_bwd("lru_pre_bwd", _lru_pre_fn, lru_params, [sv["xc"]], [da, g], 512)
    gs["lru_w_a"], gs["lru_b_a"], gs["lru_w_x"], gs["lru_b_x"], gs["lru_lambda"] = dps
    dcx, gs["lru_conv_w"], gs["lru_conv_b"] = _conv_bwd("lru_conv_bwd", dxc, sv["proj"], COL_CX, C_W, p["lru_conv_w"], 512)
    s = dx.shape[0]
    dproj = jnp.concatenate([dqkv, dbz, daq, daf, dai.astype(BF16), dag, dcx, dcy, dgate,
                             jnp.zeros((s, D_IN_PAD - GATE0 - 128), BF16)], axis=1)
    dh = _matmul("proj_in_dx", dproj, w_in, "nt")
    d_w_in = _matmul("proj_in_dw", sv["h"], dproj, "tn")
    (gs["norm1_g"],), (dx0, dx0_16) = _rowwise_bwd("norm1_bwd", _rmsnorm_fn, [p["norm1_g"]], [sv["x0"]], [dh], 512,
                                                   res=dx1, copy16=True)
    return dx0, dx0_16, dict(w_in=d_w_in, w_out=d_w_out, w_up=d_w_up, w_down=d_w_down), gs


def _layer_params(l, lb_logits, hgrn_norm_g, gdn_conv_w, gdn_a_log, gdn_dt_bias, gdn_norm_g, lru_conv_w, lru_conv_b,
                  lru_w_a, lru_b_a, lru_w_x, lru_b_x, lru_lambda, norm1_g, norm2_g):
    row = lambda v: v.reshape(1, -1)
    return dict(
        norm1_g=row(norm1_g[l]), norm2_g=row(norm2_g[l]), lb_logits=lb_logits,
        hgrn_norm_g=row(jnp.tile(hgrn_norm_g[l], A_W // A_HD)),
        gdn_conv_w=gdn_conv_w[l], gdn_a_log=_gate_row(gdn_a_log[l]), gdn_dt_bias=_gate_row(gdn_dt_bias[l]),
        gdn_norm_g=row(jnp.tile(gdn_norm_g[l], B_NH)),
        lru_conv_w=lru_conv_w[l], lru_conv_b=row(lru_conv_b[l]), lru_w_a=_block_diag(lru_w_a[l]), lru_b_a=row(lru_b_a[l]),
        lru_w_x=_block_diag(lru_w_x[l]), lru_b_x=row(lru_b_x[l]), lru_lambda=row(lru_lambda[l]))


def _small_grads_to_reference(gs_layers, d_final_g, d_logits):
    st = lambda k, f=lambda z: z: jnp.stack([f(g[k]) for g in gs_layers], axis=0)
    vec = lambda z: z.reshape(-1)
    return dict(
        norm1_g=st("norm1_g", vec), norm2_g=st("norm2_g", vec), hgrn_lb_logits=d_logits,
        hgrn_norm_g=st("hgrn_norm_g", lambda z: z.reshape(A_W // A_HD, A_HD).sum(0)),
        gdn_conv_w=st("gdn_conv_w"), gdn_a_log=st("gdn_a_log", lambda z: z[0, 4:8]),
        gdn_dt_bias=st("gdn_dt_bias", lambda z: z[0, 4:8]),
        gdn_norm_g=st("gdn_norm_g", lambda z: z.reshape(B_NH, B_HD).sum(0)),
        lru_conv_w=st("lru_conv_w"), lru_conv_b=st("lru_conv_b", vec), lru_w_a=st("lru_w_a", _diag_blocks),
        lru_b_a=st("lru_b_a", vec), lru_w_x=st("lru_w_x", _diag_blocks), lru_b_x=st("lru_b_x", vec),
        lru_lambda=st("lru_lambda", vec), final_norm_g=d_final_g.reshape(-1))


def _local_step(x, tgt, small, layer_weights, on_layer_grads=None):
    layer_p = [_layer_params(l, small["hgrn_lb_logits"], small["hgrn_norm_g"], small["gdn_conv_w"], small["gdn_a_log"],
                             small["gdn_dt_bias"], small["gdn_norm_g"], small["lru_conv_w"], small["lru_conv_b"],
                             small["lru_w_a"], small["lru_b_a"], small["lru_w_x"], small["lru_b_x"], small["lru_lambda"],
                             small["norm1_g"], small["norm2_g"]) for l in range(DEPTH)]
    saved, weights = [], []
    for l in range(DEPTH):
        x, w = layer_weights(l, x)
        x, sv = _layer_fwd(l, x, layer_p[l], *w)
        saved.append(sv)
        weights.append(w)
    loss, dx, d_final_g, dx16 = _loss_and_grad("final_loss", small["final_norm_g"].reshape(1, -1), x, tgt)
    big = [None] * DEPTH
    gs_layers = [None] * DEPTH
    d_logits = jnp.zeros_like(small["hgrn_lb_logits"])
    for l in reversed(range(DEPTH)):
        dx, dx16, big[l], gs_layers[l] = _layer_bwd(l, dx, dx16, layer_p[l], saved[l], *weights[l])
        d_logits = d_logits + gs_layers[l]["lb_logits"]
        if on_layer_grads is not None:
            big[l] = on_layer_grads(l, big[l])
    return loss, dx, big, _small_grads_to_reference(gs_layers, d_final_g, d_logits)


def kernel(x, norm1_g, w_in, hgrn_lb_logits, hgrn_norm_g, gdn_conv_w, gdn_a_log, gdn_dt_bias, gdn_norm_g, lru_conv_w, lru_conv_b, lru_w_a, lru_b_a, lru_w_x, lru_b_x, lru_lambda, w_out, norm2_g, w_up, w_down, final_norm_g, loss_target, m_norm1_g, m_w_in, m_hgrn_lb_logits, m_hgrn_norm_g, m_gdn_conv_w, m_gdn_a_log, m_gdn_dt_bias, m_gdn_norm_g, m_lru_conv_w, m_lru_conv_b, m_lru_w_a, m_lru_b_a, m_lru_w_x, m_lru_b_x, m_lru_lambda, m_w_out, m_norm2_g, m_w_up, m_w_down, m_final_norm_g, v_norm1_g, v_w_in, v_hgrn_lb_logits, v_hgrn_norm_g, v_gdn_conv_w, v_gdn_a_log, v_gdn_dt_bias, v_gdn_norm_g, v_lru_conv_w, v_lru_conv_b, v_lru_w_a, v_lru_b_a, v_lru_w_x, v_lru_b_x, v_lru_lambda, v_w_out, v_norm2_g, v_w_up, v_w_down, v_final_norm_g):
    args = dict(locals())
    ix, iy, core = lax.axis_index("x"), lax.axis_index("y"), lax.axis_index("c")
    chip = 2 * ix + iy

    packed = _pack_big(w_in, w_out, w_up, w_down).astype(BF16)
    gathered = [_gather_layer(packed[l], GATHER_IDS[l]) for l in range(DEPTH)]

    def layer_weights(l, x):
        got, own = gathered[l], packed[l]
        if l:
            got, own, x = lax.optimization_barrier((got, own, x))
        shares = lax.dynamic_update_slice(got, own[None], (chip, 0, 0))
        parts = [_unpack_big(shares[k]) for k in range(4)]
        return x, (_permute_in_cols(jnp.concatenate([pt[0] for pt in parts], axis=-1)),
                   jnp.concatenate([parts[k][1] for k in W_OUT_CHIP_ORDER], axis=0),
                   jnp.concatenate([pt[2] for pt in parts], axis=-1),
                   jnp.concatenate([pt[3] for pt in parts], axis=0))

    n_gc, n_lc = gdn_conv_w.shape[-1], lru_conv_w.shape[-1]
    conv_full = dict(
        gdn_conv_w=lax.dynamic_update_slice(jnp.zeros((DEPTH, 4, 4 * n_gc), F32), gdn_conv_w * 0.5, (0, 0, chip * n_gc)),
        lru_conv_w=lax.dynamic_update_slice(jnp.zeros((DEPTH, 4, 4 * n_lc), F32), lru_conv_w * 0.5, (0, 0, chip * n_lc)))
    conv_shapes = {k: v.shape for k, v in conv_full.items()}
    conv_full = _unflatten_small(_allreduce_small("convw", _flatten_small(conv_full)), conv_shapes)

    small = {k: args[k] for k in SMALL}
    small.update(conv_full)

    n_in = D_IN // 4

    def start_scatter(l, b):
        pieces = []
        for k in range(4):
            cols = jnp.concatenate(_orig_cols(b["w_in"], k * n_in, (k + 1) * n_in), axis=1)
            r0 = 256 * W_OUT_CHIP_ORDER.index(k)
            pieces += [jnp.pad(cols, ((0, 0), (0, 1024 - n_in))), b["w_up"][:, k * 1024:(k + 1) * 1024],
                       b["w_down"][k * 1024:(k + 1) * 1024], b["w_out"][r0:r0 + 256]]
        per_chip = jnp.concatenate(pieces, axis=0).reshape(4, LAYER_ROWS, 1024)
        return _scatter_start(per_chip, chip, core, l)

    loss, grad_x, started, sg = _local_step(x[0], loss_target[0], small, layer_weights, start_scatter)

    sg["loss"] = loss[0, :1]
    shapes = {k: v.shape for k, v in sg.items()}
    sg = _unflatten_small(_allreduce_small("smallgrad", _flatten_small(sg)), shapes)
    loss_out = sg.pop("loss")[0]

    buf = None
    for l in reversed(range(DEPTH)):
        own, got = started[l]
        buf = _rs_add_chips(own, got, core, l, buf)
    g_in, g_out, g_up, g_down = _unpack_big(_rs_share(buf).reshape(DEPTH, LAYER_ROWS, 1024))

    grads = dict(sg)
    grads["gdn_conv_w"] = lax.dynamic_slice_in_dim(sg["gdn_conv_w"], chip * n_gc, n_gc, axis=2)
    grads["lru_conv_w"] = lax.dynamic_slice_in_dim(sg["lru_conv_w"], chip * n_lc, n_lc, axis=2)
    grads.update(w_in=g_in, w_out=g_out, w_up=g_up, w_down=g_down)

    names = ['norm1_g', 'w_in', 'hgrn_lb_logits', 'hgrn_norm_g', 'gdn_conv_w', 'gdn_a_log', 'gdn_dt_bias', 'gdn_norm_g',
             'lru_conv_w', 'lru_conv_b', 'lru_w_a', 'lru_b_a', 'lru_w_x', 'lru_b_x', 'lru_lambda', 'w_out', 'norm2_g',
             'w_up', 'w_down', 'final_norm_g']
    big_names = ("w_in", "w_out", "w_up", "w_down")
    delta, new_m, new_v = {}, {}, {}
    for k in big_names:
        delta[k], new_m[k], new_v[k] = _adam("adam_" + k, args[k], grads[k], args["m_" + k], args["v_" + k])
    small_names = [k for k in names if k not in big_names]
    shapes = {k: args[k].shape for k in small_names}
    flat = [_flatten_small({k: src(k) for k in small_names})
            for src in (lambda k: args[k], lambda k: grads[k], lambda k: args["m_" + k], lambda k: args["v_" + k])]
    outs = _ew("adam_small", _adam_fn, flat, 3)
    for dst, o in zip((delta, new_m, new_v), outs):
        dst.update(_unflatten_small(o, shapes))
    return (loss_out, grad_x[None], *[grads[k] for k in names], *[delta[k] for k in names],
            *[new_m[k] for k in names], *[new_v[k] for k in names])
```

```python
import functools

import jax
import jax.numpy as jnp
from jax import lax
from jax.experimental import pallas as pl
from jax.experimental.pallas import tpu as pltpu
from jax.experimental.pallas import tpu_sc as plsc

F32 = jnp.float32
BF16 = jnp.bfloat16
MESH = pl.DeviceIdType.MESH

DEPTH = 4
D_MODEL = 1024
A_W, B_W, C_W = 256, 512, 256
A_HD, B_HD = 64, 128
B_NH = B_W // B_HD
D_IN = 3592
D_IN_PAD = 3840
COL_BZ, COL_AQ, COL_AF, COL_AI, COL_AG, COL_CX, COL_CY = 1536, 2048, 2304, 2560, 2816, 3072, 3328
W_OUT_CHIP_ORDER = (1, 2, 0, 3)
GATE0 = 3584
D_FF = 4096
CHUNK = 64
SUB = 16
EPS = 1e-6
TINY = 1e-30
RG_C = 8.0
LR, B1, B2, AEPS, WD, STEP = 0.001, 0.9, 0.999, 1e-08, 0.01, 10
VMEM_LIMIT = 56 * 1024 * 1024
PACK_ROWS = 13 * 1024

NN = ((1,), (0,))
NT = ((1,), (1,))
TN = ((0,), (0,))


DIMS = {"nn": NN, "nt": NT, "tn": TN}


def _split(a, n):
    parts, r = [], a
    for i in range(n):
        p = r.astype(BF16)
        parts.append(p)
        if i + 1 < n:
            r = r - p.astype(F32)
    return parts


def _split_dot(a, b, mode, na, nb):
    (ca,), (cb,) = DIMS[mode]
    dims = ((DIMS[mode]), ((), ())) if a.ndim == 2 else (((ca + 1,), (cb + 1,)), ((0,), (0,)))
    pa, pb = _split(a, na), _split(b, nb)
    acc = None
    for i in range(na):
        for j in range(nb):
            if i + j < max(na, nb):
                t = lax.dot_general(pa[i], pb[j], dims, preferred_element_type=F32)
                acc = t if acc is None else acc + t
    return acc


@functools.partial(jax.custom_vjp, nondiff_argnums=(2, 3, 4, 5))
def _mdot(a, b, mode, na, nb, need):
    return _split_dot(a, b, mode, na, nb)


def _mdot_fwd(a, b, mode, na, nb, need):
    return _split_dot(a, b, mode, na, nb), (a, b)


def _mdot_bwd(mode, na, nb, need, res, ct):
    a, b = res
    n = max(na, nb)
    da, db = jnp.zeros_like(a), jnp.zeros_like(b)
    if mode == "nn":
        if need[0]:
            da = _split_dot(ct, b, "nt", n, nb)
        if need[1]:
            db = _split_dot(a, ct, "tn", na, n)
    elif mode == "nt":
        if need[0]:
            da = _split_dot(ct, b, "nn", n, nb)
        if need[1]:
            db = _split_dot(ct, a, "tn", n, na)
    else:
        if need[0]:
            da = _split_dot(b, ct, "nt", nb, n)
        if need[1]:
            db = _split_dot(a, ct, "nn", na, n)
    return da, db


_mdot.defvjp(_mdot_fwd, _mdot_bwd)
BOTH = (True, True)


def _bdot(a, b, mode="nn"):
    return _mdot(a, b, mode, 1, 1, BOTH)


def _group_sum(x, ones):
    return _mdot(x, ones, "nn", 2, 1, (True, False))


def _running_sum(x):
    shape = x.shape[:-2] + (CHUNK, CHUNK)
    tri = (_iota(shape, len(shape) - 1) <= _iota(shape, len(shape) - 2)).astype(F32)
    return _mdot(tri, x, "nn", 1, 3, (False, True))


def _iota(shape, d):
    return lax.broadcasted_iota(jnp.int32, shape, d)


def _block_ones(width, group):
    return (_iota((width, width), 0) // group == _iota((width, width), 1) // group).astype(F32)


def _params(sem):
    return pltpu.CompilerParams(dimension_semantics=sem, vmem_limit_bytes=VMEM_LIMIT)


def _tile(dim, prefs):
    for p in prefs:
        if dim % p == 0:
            return p
    return dim


def _whole(shape):
    return pl.BlockSpec(shape, lambda i: (0,) * len(shape))


def _rowspec(t, w, col0=0):
    cb = col0 // w
    return pl.BlockSpec((t, w), lambda i: (i, cb))


def _cols(arr, col0, width):
    assert col0 % width == 0
    return (arr, col0, width)


def _row_operands(rows, t):
    arrs, specs = [], []
    for r in rows:
        arr, col0, w = r if isinstance(r, tuple) else (r, 0, r.shape[1])
        arrs.append(arr)
        specs.append(_rowspec(t, w, col0))
    return arrs, specs


def _rowwise_fwd(name, fn, params, rows, out_widths, out_dtypes, t):
    rows, row_specs = _row_operands(rows, t)
    s = rows[0].shape[0]
    n_p, n_r = len(params), len(rows)

    def body(*refs):
        p = [r[...] for r in refs[:n_p]]
        xs = [r[...].astype(F32) for r in refs[n_p:n_p + n_r]]
        outs = fn(pl.program_id(0) * t, *p, *xs)
        for o_ref, o in zip(refs[n_p + n_r:], outs):
            o_ref[...] = o.astype(o_ref.dtype)

    return pl.pallas_call(
        body, grid=(s // t,),
        in_specs=[_whole(p.shape) for p in params] + row_specs,
        out_specs=[_rowspec(t, w) for w in out_widths],
        out_shape=[jax.ShapeDtypeStruct((s, w), d) for w, d in zip(out_widths, out_dtypes)],
        name=name, compiler_params=_params(("arbitrary",)))(*params, *rows)


def _rowwise_bwd(name, fn, params, rows, douts, t, need=None, res=None, dtypes=None, copy16=False):
    rows, row_specs = _row_operands(rows, t)
    douts, dout_specs = _row_operands(douts, t)
    s = rows[0].shape[0]
    n_p, n_r, n_o = len(params), len(rows), len(douts)
    need = [True] * n_r if need is None else need
    n_res = 0 if res is None else 1

    def body(*refs):
        p = [r[...] for r in refs[:n_p]]
        xs = [r[...].astype(F32) for r in refs[n_p:n_p + n_r]]
        dys = [r[...].astype(F32) for r in refs[n_p + n_r:n_p + n_r + n_o]]
        k = n_p + n_r + n_o
        res_ref = refs[k] if n_res else None
        dp_refs = refs[k + n_res:k + n_res + n_p]
        dx_refs = refs[k + n_res + n_p:]
        row0 = pl.program_id(0) * t
        _, vjp = jax.vjp(lambda *a: tuple(fn(row0, *a)), *p, *xs)
        g = vjp(tuple(dys))

        @pl.when(pl.program_id(0) == 0)
        def _():
            for r in dp_refs:
                r[...] = jnp.zeros_like(r)

        for r, gp in zip(dp_refs, g[:n_p]):
            r[...] += gp
        gx = [gi for gi, nd in zip(g[n_p:], need) if nd]
        if n_res:
            gx[0] = gx[0] + res_ref[...]
        if copy16:
            gx.append(gx[0])
        for r, gi in zip(dx_refs, gx):
            r[...] = gi.astype(r.dtype)

    widths = [sp.block_shape[1] for sp, nd in zip(row_specs, need) if nd]
    dtypes = [F32] * len(widths) if dtypes is None else list(dtypes)
    if copy16:
        widths, dtypes = widths + widths[:1], dtypes + [BF16]
    outs = pl.pallas_call(
        body, grid=(s // t,),
        in_specs=([_whole(p.shape) for p in params] + row_specs + dout_specs
                  + ([_rowspec(t, res.shape[1])] if n_res else [])),
        out_specs=[_whole(p.shape) for p in params] + [_rowspec(t, w) for w in widths],
        out_shape=([jax.ShapeDtypeStruct(p.shape, F32) for p in params]
                   + [jax.ShapeDtypeStruct((s, w), d) for w, d in zip(widths, dtypes)]),
        name=name, compiler_params=_params(("arbitrary",)))(*params, *rows, *douts, *([res] if n_res else []))
    return list(outs[:n_p]), list(outs[n_p:])


def _rmsnorm_fn(row0, g, x):
    y = x * lax.rsqrt(jnp.mean(x * x, axis=-1, keepdims=True) + EPS)
    return (y * g,)


def _hgrn_pre_fn(layer, row0, logits, aq, af):
    m = jnp.max(logits, axis=0, keepdims=True)
    e = jnp.exp(logits - m)
    p = e / jnp.sum(e, axis=0, keepdims=True)
    if layer == 0:
        lb = jnp.zeros((1, A_W), F32)
    else:
        acc = p[1:2]
        for j in range(2, layer + 1):
            acc = acc + p[j:j + 1]
        lb = jnp.minimum(jnp.maximum(acc, 0.0), 1.0 - EPS)
    sig = jax.nn.sigmoid(af)
    f = lb + (1.0 - lb) * sig
    log_f = jnp.log(jnp.maximum(f, TINY))
    k = (1.0 - lb) * jax.nn.sigmoid(-af)
    return jax.nn.silu(aq), k, log_f


def _hgrn_post_fn(row0, g, o, ag):
    ms = _group_sum(o * o, _block_ones(A_W, A_HD)) * (1.0 / A_HD)
    return (o * lax.rsqrt(ms + EPS) * g * jax.nn.silu(ag),)


def _gdn_pre_fn(row0, a_log, dt_bias, qkvc, gate):
    act = jax.nn.silu(qkvc)
    q, k, v = act[:, :B_W], act[:, B_W:2 * B_W], act[:, 2 * B_W:]
    bo = _block_ones(B_W, B_HD)
    q = q * lax.rsqrt(_group_sum(q * q, bo) + EPS) * (B_HD ** -0.5)
    k = k * lax.rsqrt(_group_sum(k * k, bo) + EPS)
    beta = jax.nn.sigmoid(gate)
    log_alpha = -jnp.exp(a_log) * jax.nn.softplus(gate + dt_bias)
    t = gate.shape[0]
    spread = lambda z, lane0: jnp.concatenate(
        [jnp.broadcast_to(z[:, lane0 + h:lane0 + h + 1], (t, B_HD)) for h in range(B_NH)], axis=1)
    return q, k, v, spread(beta, 0), spread(log_alpha, B_NH)


def _gdn_post_fn(row0, g, o, bz):
    ms = _group_sum(o * o, _block_ones(B_W, B_HD)) * (1.0 / B_HD)
    return (o * lax.rsqrt(ms + EPS) * g * jax.nn.silu(bz),)


def _lru_pre_fn(row0, w_a, b_a, w_x, b_x, lam, xc):
    r = jax.nn.sigmoid(_mdot(xc, w_a, "nn", 2, 2, BOTH) + b_a)
    i = jax.nn.sigmoid(_mdot(xc, w_x, "nn", 2, 2, BOTH) + b_x)
    log_a = -RG_C * r * jax.nn.softplus(-lam)
    a = jnp.exp(log_a)
    one_minus = -jnp.tanh(log_a) * (a * a + 1.0)
    mult = jnp.sqrt(jnp.maximum(one_minus, EPS))
    pos = row0 + _iota(xc.shape, 0)
    mult = jnp.where(pos == 0, 1.0, mult)
    return a, mult * i * xc


def _lru_post_fn(row0, h, cy):
    return (jax.nn.gelu(cy, approximate=True) * h,)


def _lru_da_fn(row0, g, h_prev):
    return (g * h_prev,)


def _relu2(u, _=None):
    r = jnp.maximum(u, 0.0)
    return r * r


def _drelu2(acc, act):
    return acc * (2.0 * jnp.sqrt(act.astype(F32)))


MM_VMEM_BUDGET = 44 * 1024 * 1024


def _mm_tiles(mode, m, n, k, a_bytes, b_bytes, n_extra):
    best = None
    for tk in {k, 4096, 2048, 1024, 512, 256}:
        for tn in {n, 1280, 1024, 768, 512, 256, 128}:
            for tm in {m, 1024, 512, 256, 128}:
                if m % tm or n % tn or k % tk or tm > 1024:
                    continue
                tiles = tm * tk * a_bytes + tk * tn * b_bytes + (1 + n_extra) * tm * tn * 4
                if 2 * tiles + (tm * tn * 4 if tk < k else 0) > MM_VMEM_BUDGET:
                    continue
                a_reads = 1 if tk == k else n // tn
                b_reads = 1 if (tk == k and tn == n) else m // tm
                cost = m * k * a_bytes * a_reads + k * n * b_bytes * b_reads + m * n * 4 * (k // tk - 1)
                if best is None or (cost, -tm) < best[0]:
                    best = ((cost, -tm), (tm, tn, tk))
    if best is None:
        raise ValueError("no matmul tiling fits VMEM")
    return best[1]


def _matmul(name, a, b, mode, a_fn=None, res=None, epi=None, epi_in=None, out_dtype=F32):
    if mode == "nn":
        (m, k), n = a.shape, b.shape[1]
    elif mode == "nt":
        (m, k), n = a.shape, b.shape[0]
    else:
        (k, m), n = a.shape, b.shape[1]
    n_extra = (res is not None) + (epi_in is not None)
    tm, tn, tk = _mm_tiles(mode, m, n, k, a.dtype.itemsize, b.dtype.itemsize, n_extra)
    nk = k // tk
    a_spec = {"nn": pl.BlockSpec((tm, tk), lambda i, j, kk: (i, kk)),
              "nt": pl.BlockSpec((tm, tk), lambda i, j, kk: (i, kk)),
              "tn": pl.BlockSpec((tk, tm), lambda i, j, kk: (kk, i))}[mode]
    b_spec = {"nn": pl.BlockSpec((tk, tn), lambda i, j, kk: (kk, j)),
              "nt": pl.BlockSpec((tn, tk), lambda i, j, kk: (j, kk)),
              "tn": pl.BlockSpec((tk, tn), lambda i, j, kk: (kk, j))}[mode]
    o_spec = pl.BlockSpec((tm, tn), lambda i, j, kk: (i, j))
    dims = {"nn": NN, "nt": NT, "tn": TN}[mode]
    extra = [x for x in (res, epi_in) if x is not None]

    def body(*refs):
        a_ref, b_ref = refs[0], refs[1]
        rest = list(refs[2:])
        res_ref = rest.pop(0) if res is not None else None
        epi_ref = rest.pop(0) if epi_in is not None else None
        o_ref = rest.pop(0)
        acc = rest[0] if nk > 1 else None
        kk = pl.program_id(2)
        at = a_ref[...]
        if a_fn is not None:
            at = a_fn(at.astype(F32))
        part = lax.dot_general(at.astype(BF16), b_ref[...].astype(BF16), (dims, ((), ())),
                               preferred_element_type=F32)

        def finish(out):
            if res_ref is not None:
                out = out + res_ref[...]
            if epi is not None:
                out = epi(out, None if epi_ref is None else epi_ref[...])
            o_ref[...] = out.astype(o_ref.dtype)

        if nk == 1:
            finish(part)
        else:
            @pl.when(kk == 0)
            def _():
                acc[...] = part

            @pl.when(jnp.logical_and(kk > 0, kk < nk - 1))
            def _():
                acc[...] += part

            @pl.when(kk == nk - 1)
            def _():
                finish(acc[...] + part)

    return pl.pallas_call(
        body, grid=(m // tm, n // tn, nk),
        in_specs=[a_spec, b_spec] + [o_spec] * len(extra), out_specs=o_spec,
        out_shape=jax.ShapeDtypeStruct((m, n), out_dtype),
        scratch_shapes=[pltpu.VMEM((tm, tn), F32)] if nk > 1 else [],
        name=name, compiler_params=_params(("arbitrary", "arbitrary", "arbitrary")))(a, b, *extra)


HALO = 8


def _conv_fwd(name, x, col0, width, w, bias, t=512):
    s = x.shape[0]
    cb = col0 // width
    hb = t // HALO

    def body(x_ref, halo_ref, w_ref, b_ref, y_ref):
        i = pl.program_id(0)
        halo = jnp.where(i == 0, 0.0, halo_ref[...])
        xp = jnp.concatenate([halo, x_ref[...]], axis=0)
        wv = w_ref[...]
        y = b_ref[...] + wv[0:1] * xp[HALO - 3:HALO - 3 + t]
        for j in range(1, 4):
            y = y + wv[j:j + 1] * xp[HALO - 3 + j:HALO - 3 + j + t]
        y_ref[...] = y

    return pl.pallas_call(
        body, grid=(s // t,),
        in_specs=[pl.BlockSpec((t, width), lambda i: (i, cb)),
                  pl.BlockSpec((HALO, width), lambda i: (jnp.maximum(i * hb - 1, 0), cb)),
                  _whole(w.shape), _whole(bias.shape)],
        out_specs=_rowspec(t, width), out_shape=jax.ShapeDtypeStruct((s, width), F32),
        name=name, compiler_params=_params(("arbitrary",)))(x, x, w, bias)


def _conv_bwd(name, dy, x, col0, width, w, t=512):
    s = x.shape[0]
    cb = col0 // width
    hb = t // HALO
    nblk = s // t

    def body(dy_ref, dyn_ref, x_ref, halo_ref, w_ref, dx_ref, dw_ref, db_ref):
        i = pl.program_id(0)
        dyv = dy_ref[...]
        nxt = jnp.where(i == nblk - 1, 0.0, dyn_ref[...])
        dyp = jnp.concatenate([dyv, nxt], axis=0)
        wv = w_ref[...]
        dx = wv[3:4] * dyv
        for j in range(3):
            dx = dx + wv[j:j + 1] * dyp[3 - j:3 - j + t]
        dx_ref[...] = dx.astype(dx_ref.dtype)
        halo = jnp.where(i == 0, 0.0, halo_ref[...])
        xp = jnp.concatenate([halo, x_ref[...]], axis=0)
        dw = jnp.concatenate(
            [jnp.sum(dyv * xp[HALO - 3 + j:HALO - 3 + j + t], axis=0, keepdims=True) for j in range(4)], axis=0)

        @pl.when(i == 0)
        def _():
            dw_ref[...] = jnp.zeros_like(dw_ref)
            db_ref[...] = jnp.zeros_like(db_ref)

        dw_ref[...] += dw
        db_ref[...] += jnp.sum(dyv, axis=0, keepdims=True)

    return pl.pallas_call(
        body, grid=(nblk,),
        in_specs=[_rowspec(t, width),
                  pl.BlockSpec((HALO, width), lambda i: (jnp.minimum((i + 1) * hb, s // HALO - 1), 0)),
                  pl.BlockSpec((t, width), lambda i: (i, cb)),
                  pl.BlockSpec((HALO, width), lambda i: (jnp.maximum(i * hb - 1, 0), cb)),
                  _whole(w.shape)],
        out_specs=[_rowspec(t, width), _whole((4, width)), _whole((1, width))],
        out_shape=[jax.ShapeDtypeStruct((s, width), BF16), jax.ShapeDtypeStruct((4, width), F32),
                   jax.ShapeDtypeStruct((1, width), F32)],
        name=name, compiler_params=_params(("arbitrary",)))(dy, dy, x, x, w)


def _hgrn_chunk(st, q, k, v, lf):
    cum = _running_sum(lf)
    tot = jnp.sum(lf, axis=0, keepdims=True)
    bm = _iota((A_W, A_W), 0) // A_HD == _iota((A_W, A_W), 1) // A_HD
    bo = bm.astype(F32)
    o_inter = _bdot(q * jnp.exp(cum), st, "nt")
    kd = k * jnp.exp(tot - cum)
    st_new = st * jnp.exp(tot) + jnp.where(bm, _bdot(v, kd, "tn"), 0.0)
    lane_h = _iota((SUB, A_W), 1) // A_HD
    n_h = A_W // A_HD
    t3 = _iota((SUB, SUB, A_W), 0)
    s3 = _iota((SUB, SUB, A_W), 1)
    outs = []
    for i in range(CHUNK // SUB):
        sl = slice(i * SUB, (i + 1) * SUB)
        qi, ki, vi, ci = q[sl], k[sl], v[sl], cum[sl]
        e = jnp.exp(jnp.minimum(ci[:, None, :] - ci[None, :, :], 0.0))
        d3 = jnp.where(s3 <= t3, qi[:, None, :] * ki[None, :, :] * e, 0.0)
        ds = _mdot(d3.reshape(SUB * SUB, A_W), bo, "nn", 1, 1, (True, False)).reshape(SUB, SUB, A_W)
        oi = jnp.sum(ds * vi[None, :, :], axis=1)
        if i > 0:
            n = i * SUB
            cb = cum[n - 1:n]
            qt = qi * jnp.exp(ci - cb)
            kt = k[:n] * jnp.exp(jnp.minimum(cb - cum[:n], 0.0))
            qs = jnp.concatenate([jnp.where(lane_h == h, qt, 0.0) for h in range(n_h)], axis=0)
            p = _bdot(_bdot(qs, kt, "nt"), v[:n])
            for h in range(n_h):
                oi = oi + jnp.where(lane_h == h, p[h * SUB:(h + 1) * SUB], 0.0)
        outs.append(oi)
    return st_new, o_inter + jnp.concatenate(outs, axis=0)


HGRN_CHUNKS = 2


def _hgrn_fwd(name, q, k, v, lf):
    s = q.shape[0]
    n = s // CHUNK
    g = HGRN_CHUNKS

    def body(q_ref, k_ref, v_ref, lf_ref, o_ref, sts_ref, st_ref):
        @pl.when(pl.program_id(0) == 0)
        def _():
            st_ref[...] = jnp.zeros_like(st_ref)

        st = st_ref[...]
        for c in range(g):
            rows = pl.ds(c * CHUNK, CHUNK)
            sts_ref[c] = st
            st, o = _hgrn_chunk(st, q_ref[rows], k_ref[rows], v_ref[rows], lf_ref[rows])
            o_ref[rows] = o
        st_ref[...] = st

    cs = _rowspec(g * CHUNK, A_W)
    return pl.pallas_call(
        body, grid=(n // g,), in_specs=[cs] * 4,
        out_specs=[cs, pl.BlockSpec((g, A_W, A_W), lambda i: (i, 0, 0))],
        out_shape=[jax.ShapeDtypeStruct((s, A_W), F32), jax.ShapeDtypeStruct((n, A_W, A_W), F32)],
        scratch_shapes=[pltpu.VMEM((A_W, A_W), F32)],
        name=name, compiler_params=_params(("arbitrary",)))(q, k, v, lf)


def _hgrn_bwd(name, q, k, v, lf, sts, do):
    s = q.shape[0]
    n = s // CHUNK
    g = HGRN_CHUNKS
    nsteps = n // g

    def body(q_ref, k_ref, v_ref, lf_ref, sts_ref, do_ref, dq_ref, dk_ref, dv_ref, dlf_ref, dst_ref):
        @pl.when(pl.program_id(0) == 0)
        def _():
            dst_ref[...] = jnp.zeros_like(dst_ref)

        dst = dst_ref[...]
        for c in reversed(range(g)):
            rows = pl.ds(c * CHUNK, CHUNK)
            _, vjp = jax.vjp(_hgrn_chunk, sts_ref[c], q_ref[rows], k_ref[rows], v_ref[rows], lf_ref[rows])
            ct = vjp((dst, do_ref[rows]))
            dst = ct[0]
            for r, gi in zip((dq_ref, dk_ref, dv_ref, dlf_ref), ct[1:]):
                r[rows] = gi
        dst_ref[...] = dst

    rs = pl.BlockSpec((g * CHUNK, A_W), lambda i: (nsteps - 1 - i, 0))
    return pl.pallas_call(
        body, grid=(nsteps,),
        in_specs=[rs] * 4 + [pl.BlockSpec((g, A_W, A_W), lambda i: (nsteps - 1 - i, 0, 0)), rs],
        out_specs=[rs] * 4, out_shape=[jax.ShapeDtypeStruct((s, A_W), F32)] * 4,
        scratch_shapes=[pltpu.VMEM((A_W, A_W), F32)],
        name=name, compiler_params=_params(("arbitrary",)))(q, k, v, lf, sts, do)


PREP_CHUNKS = 4
REC_CHUNKS = 4
DEC_ROWS = 8


def _to_heads(x, rows=CHUNK, width=B_HD):
    n = x.shape[0] // rows
    x3 = x.reshape(n, rows, x.shape[1])
    return jnp.concatenate([x3[:, :, h * B_HD:h * B_HD + width] for h in range(B_NH)], axis=0)


def _from_heads(y):
    n = y.shape[0] // B_NH
    x3 = jnp.concatenate([y[h * n:(h + 1) * n] for h in range(B_NH)], axis=2)
    return x3.reshape(n * y.shape[1], B_NH * B_HD)


@jax.custom_vjp
def _unit_lower_inverse(a_mat):
    nil = -a_mat
    eye = (_iota(a_mat.shape, 1) == _iota(a_mat.shape, 2)).astype(F32)
    inv = eye + nil
    sq = 2
    while sq < CHUNK:
        nil = _split_dot(nil, nil, "nn", 2, 2)
        inv = inv + _split_dot(inv, nil, "nn", 2, 2)
        sq *= 2
    return inv


def _unit_lower_inverse_fwd(a_mat):
    inv = _unit_lower_inverse(a_mat)
    return inv, inv


def _unit_lower_inverse_bwd(inv, ct):
    return (-_split_dot(_split_dot(inv, ct, "tn", 2, 2), inv, "nt", 2, 2),)


_unit_lower_inverse.defvjp(_unit_lower_inverse_fwd, _unit_lower_inverse_bwd)


@jax.custom_vjp
def _known_inverse(a_mat, inv):
    return inv


def _known_inverse_fwd(a_mat, inv):
    return inv, inv


def _known_inverse_bwd(inv, ct):
    return _unit_lower_inverse_bwd(inv, ct)[0], jnp.zeros_like(inv)


_known_inverse.defvjp(_known_inverse_fwd, _known_inverse_bwd)


def _gdn_prep(q, k, v, beta, la, inv_saved=None):
    q, k, v, beta, la = [_to_heads(z) for z in (q, k, v, beta, la)]
    nb = q.shape[0]
    r = _iota((nb, CHUNK, CHUNK), 1)
    c = _iota((nb, CHUNK, CHUNK), 2)
    causal = c <= r
    g = _running_sum(la)
    tot = jnp.sum(la, axis=1, keepdims=True)
    gcol = g[:, :, :CHUNK]
    decay = jnp.where(causal, jnp.exp(jnp.where(causal, gcol - jnp.swapaxes(gcol, 1, 2), 0.0)), 0.0)
    kb = k * beta
    a_mat = jnp.where(c < r, _bdot(kb, k, "nt") * decay, 0.0)
    if inv_saved is None:
        inv = _unit_lower_inverse(a_mat)
    else:
        inv = _known_inverse(a_mat, _to_heads(inv_saved, width=CHUNK))
    eg = jnp.exp(g)
    u = _bdot(inv, v * beta)
    w = _bdot(inv, kb * eg)
    qk = jnp.where(causal, _bdot(q, k, "nt") * decay, 0.0)
    widen = lambda z: jnp.concatenate([z, jnp.zeros((nb, CHUNK, B_HD - CHUNK), F32)], axis=2)
    dec = jnp.broadcast_to(jnp.exp(tot), (nb, DEC_ROWS, B_HD))
    return tuple(_from_heads(z) for z in (u, w, widen(qk), q * eg, k * jnp.exp(tot - g), dec, widen(inv)))


def _gdn_rec(st, u, w, qk, qd, kd, dec):
    u, w, qd, kd = [_to_heads(z) for z in (u, w, qd, kd)]
    qk = _to_heads(qk, width=CHUNK)
    v_new = u - _bdot(w, st)
    o = _bdot(qd, st) + _bdot(qk, v_new)
    st_new = st * _to_heads(dec, rows=DEC_ROWS)[:, :1, :1] + _bdot(kd, v_new, "tn")
    return st_new, _from_heads(o)


def _gdn_prep_fwd(name, q, k, v, beta, la):
    s = q.shape[0]
    t = PREP_CHUNKS * CHUNK
    td = PREP_CHUNKS * DEC_ROWS

    def body(*refs):
        outs = _gdn_prep(*[r[...] for r in refs[:5]])
        for o_ref, o in zip(refs[5:], outs):
            o_ref[...] = o

    rows, dec_rows = jax.ShapeDtypeStruct((s, B_W), F32), jax.ShapeDtypeStruct((s // CHUNK * DEC_ROWS, B_W), F32)
    return pl.pallas_call(
        body, grid=(s // t,), in_specs=[_rowspec(t, B_W)] * 5,
        out_specs=[_rowspec(t, B_W)] * 5 + [_rowspec(td, B_W), _rowspec(t, B_W)],
        out_shape=[rows] * 5 + [dec_rows, rows],
        name=name, compiler_params=_params(("arbitrary",)))(q, k, v, beta, la)


def _gdn_prep_bwd(name, q, k, v, beta, la, inv, cts):
    s = q.shape[0]
    t = PREP_CHUNKS * CHUNK
    td = PREP_CHUNKS * DEC_ROWS

    def body(*refs):
        inv_saved = refs[5][...]
        _, vjp = jax.vjp(lambda *a: _gdn_prep(*a, inv_saved=inv_saved)[:6], *[r[...] for r in refs[:5]])
        g = vjp(tuple(r[...] for r in refs[6:12]))
        for o_ref, o in zip(refs[12:], g):
            o_ref[...] = o

    return pl.pallas_call(
        body, grid=(s // t,), in_specs=[_rowspec(t, B_W)] * 11 + [_rowspec(td, B_W)],
        out_specs=[_rowspec(t, B_W)] * 5, out_shape=[jax.ShapeDtypeStruct((s, B_W), F32)] * 5,
        name=name, compiler_params=_params(("arbitrary",)))(q, k, v, beta, la, inv, *cts)


def _gdn_rec_fwd(name, u, w, qk, qd, kd, dec):
    s = u.shape[0]
    n = s // CHUNK
    g = REC_CHUNKS

    def body(u_ref, w_ref, qk_ref, qd_ref, kd_ref, dec_ref, o_ref, sts_ref, st_ref):
        @pl.when(pl.program_id(0) == 0)
        def _():
            st_ref[...] = jnp.zeros_like(st_ref)

        st = st_ref[...]
        for c in range(g):
            rows, drows = pl.ds(c * CHUNK, CHUNK), pl.ds(c * DEC_ROWS, DEC_ROWS)
            sts_ref[c] = st
            st, o = _gdn_rec(st, u_ref[rows], w_ref[rows], qk_ref[rows], qd_ref[rows], kd_ref[rows], dec_ref[drows])
            o_ref[rows] = o
        st_ref[...] = st

    cs = _rowspec(g * CHUNK, B_W)
    return pl.pallas_call(
        body, grid=(n // g,), in_specs=[cs] * 5 + [_rowspec(g * DEC_ROWS, B_W)],
        out_specs=[cs, pl.BlockSpec((g, B_NH, B_HD, B_HD), lambda i: (i, 0, 0, 0))],
        out_shape=[jax.ShapeDtypeStruct((s, B_W), F32), jax.ShapeDtypeStruct((n, B_NH, B_HD, B_HD), F32)],
        scratch_shapes=[pltpu.VMEM((B_NH, B_HD, B_HD), F32)],
        name=name, compiler_params=_params(("arbitrary",)))(u, w, qk, qd, kd, dec)


def _gdn_rec_bwd(name, u, w, qk, qd, kd, dec, sts, do):
    s = u.shape[0]
    n = s // CHUNK
    g = REC_CHUNKS
    nsteps = n // g

    def body(u_ref, w_ref, qk_ref, qd_ref, kd_ref, dec_ref, sts_ref, do_ref,
             du_ref, dw_ref, dqk_ref, dqd_ref, dkd_ref, ddec_ref, dst_ref):
        @pl.when(pl.program_id(0) == 0)
        def _():
            dst_ref[...] = jnp.zeros_like(dst_ref)

        dst = dst_ref[...]
        for c in reversed(range(g)):
            rows, drows = pl.ds(c * CHUNK, CHUNK), pl.ds(c * DEC_ROWS, DEC_ROWS)
            _, vjp = jax.vjp(_gdn_rec, sts_ref[c], u_ref[rows], w_ref[rows], qk_ref[rows], qd_ref[rows], kd_ref[rows],
                             dec_ref[drows])
            ct = vjp((dst, do_ref[rows]))
            dst = ct[0]
            for r, gi in zip((du_ref, dw_ref, dqk_ref, dqd_ref, dkd_ref), ct[1:6]):
                r[rows] = gi
            ddec_ref[drows] = ct[6]
        dst_ref[...] = dst

    rs = pl.BlockSpec((g * CHUNK, B_W), lambda i: (nsteps - 1 - i, 0))
    ds = pl.BlockSpec((g * DEC_ROWS, B_W), lambda i: (nsteps - 1 - i, 0))
    return pl.pallas_call(
        body, grid=(nsteps,),
        in_specs=[rs] * 5 + [ds, pl.BlockSpec((g, B_NH, B_HD, B_HD), lambda i: (nsteps - 1 - i, 0, 0, 0)), rs],
        out_specs=[rs] * 5 + [ds],
        out_shape=[jax.ShapeDtypeStruct((s, B_W), F32)] * 5 + [jax.ShapeDtypeStruct((n * DEC_ROWS, B_W), F32)],
        scratch_shapes=[pltpu.VMEM((B_NH, B_HD, B_HD), F32)],
        name=name, compiler_params=_params(("arbitrary",)))(u, w, qk, qd, kd, dec, sts, do)


SCAN_T = 512


def _block_scan(a, b, reverse):
    t = a.shape[0]
    rows = _iota(a.shape, 0)
    sft = 1
    while sft < t:
        if reverse:
            a_s, b_s = pltpu.roll(a, t - sft, 0), pltpu.roll(b, t - sft, 0)
            edge = rows >= t - sft
        else:
            a_s, b_s = pltpu.roll(a, sft, 0), pltpu.roll(b, sft, 0)
            edge = rows < sft
        a_s = jnp.where(edge, 1.0, a_s)
        b_s = jnp.where(edge, 0.0, b_s)
        b = a * b_s + b
        a = a * a_s
        sft *= 2
    return a, b


def _lru_scan_fwd(name, a, b):
    s, w = a.shape
    t = SCAN_T

    def body(a_ref, b_ref, h_ref, hp_ref, carry):
        @pl.when(pl.program_id(0) == 0)
        def _():
            carry[...] = jnp.zeros_like(carry)

        h_in = carry[0:1]
        ca, cb = _block_scan(a_ref[...], b_ref[...], False)
        h = ca * h_in + cb
        h_ref[...] = h
        hp_ref[...] = jnp.where(_iota(h.shape, 0) == 0, h_in, pltpu.roll(h, 1, 0))
        carry[...] = jnp.broadcast_to(h[t - 1:t], carry.shape)

    return pl.pallas_call(
        body, grid=(s // t,), in_specs=[_rowspec(t, w)] * 2, out_specs=[_rowspec(t, w)] * 2,
        out_shape=[jax.ShapeDtypeStruct((s, w), F32)] * 2, scratch_shapes=[pltpu.VMEM((8, w), F32)],
        name=name, compiler_params=_params(("arbitrary",)))(a, b)


def _lru_scan_bwd(name, a, dh):
    s, w = a.shape
    t = SCAN_T
    n = s // t

    def body(a_ref, dh_ref, g_ref, a_next, g_next):
        @pl.when(pl.program_id(0) == 0)
        def _():
            a_next[...] = jnp.zeros_like(a_next)
            g_next[...] = jnp.zeros_like(g_next)

        av = a_ref[...]
        a_up = jnp.where(_iota(av.shape, 0) == t - 1, a_next[0:1], pltpu.roll(av, t - 1, 0))
        ca, cb = _block_scan(a_up, dh_ref[...], True)
        g = ca * g_next[0:1] + cb
        g_ref[...] = g
        a_next[...] = jnp.broadcast_to(av[0:1], a_next.shape)
        g_next[...] = jnp.broadcast_to(g[0:1], g_next.shape)

    rs = pl.BlockSpec((t, w), lambda i: (n - 1 - i, 0))
    return pl.pallas_call(
        body, grid=(n,), in_specs=[rs] * 2, out_specs=rs, out_shape=jax.ShapeDtypeStruct((s, w), F32),
        scratch_shapes=[pltpu.VMEM((8, w), F32), pltpu.VMEM((8, w), F32)],
        name=name, compiler_params=_params(("arbitrary",)))(a, dh)


def _loss_fn(g, x, tgt):
    y = x * lax.rsqrt(jnp.mean(x * x, axis=-1, keepdims=True) + EPS) * g
    err = y - tgt
    return 0.5 * jnp.sum(jnp.mean(err * err, axis=-1, keepdims=True), axis=0, keepdims=True)


def _loss_and_grad(name, g, x, tgt, t=256):
    s, d = x.shape

    def body(g_ref, x_ref, t_ref, loss_ref, dx_ref, dg_ref, dx16_ref):
        val, vjp = jax.vjp(_loss_fn, g_ref[...], x_ref[...], t_ref[...])
        dg, dx, _ = vjp(jnp.ones((1, 1), F32))

        @pl.when(pl.program_id(0) == 0)
        def _():
            loss_ref[...] = jnp.zeros_like(loss_ref)
            dg_ref[...] = jnp.zeros_like(dg_ref)

        loss_ref[...] += jnp.broadcast_to(val, loss_ref.shape)
        dg_ref[...] += dg
        dx_ref[...] = dx
        dx16_ref[...] = dx.astype(BF16)

    return pl.pallas_call(
        body, grid=(s // t,), in_specs=[_whole(g.shape), _rowspec(t, d), _rowspec(t, d)],
        out_specs=[_whole((1, 128)), _rowspec(t, d), _whole(g.shape), _rowspec(t, d)],
        out_shape=[jax.ShapeDtypeStruct((1, 128), F32), jax.ShapeDtypeStruct((s, d), F32),
                   jax.ShapeDtypeStruct(g.shape, F32), jax.ShapeDtypeStruct((s, d), BF16)],
        name=name, compiler_params=_params(("arbitrary",)))(g, x, tgt)


def _ew(name, fn, ins, n_out, t=None):
    r, c = ins[0].shape
    t = _tile(r, (512, 256, 128, 64, 32, 16, 8)) if t is None else t

    def body(*refs):
        outs = fn(*[x[...] for x in refs[:len(ins)]])
        for o_ref, o in zip(refs[len(ins):], outs):
            o_ref[...] = o

    return pl.pallas_call(
        body, grid=(r // t,), in_specs=[_rowspec(t, c)] * len(ins), out_specs=[_rowspec(t, c)] * n_out,
        out_shape=[jax.ShapeDtypeStruct((r, c), F32)] * n_out,
        name=name, compiler_params=_params(("arbitrary",)))(*ins)


def _adam_fn(w, g, m, v):
    m = B1 * m + (1.0 - B1) * g
    v = B2 * v + (1.0 - B2) * (g * g)
    m_hat = m / (1.0 - B1 ** STEP)
    v_hat = v / (1.0 - B2 ** STEP)
    delta = -LR * (m_hat / (jnp.sqrt(v_hat) + AEPS) + WD * w)
    return delta, m, v


def _add2_fn(a, b):
    return (a + b,)


def _adam(name, w, g, m, v):
    shp = w.shape
    two = lambda z: z.reshape(-1, shp[-1])
    outs = _ew(name, _adam_fn, [two(w), two(g), two(m), two(v)], 3)
    return [o.reshape(shp) for o in outs]


def _swap(name, bufs, flips):
    n = len(bufs)
    hbm = pl.BlockSpec(memory_space=pltpu.HBM)

    def body(*refs):
        srcs, dsts = refs[:n], refs[n:2 * n]
        send_sems, recv_sems = refs[2 * n], refs[2 * n + 1]
        me = (lax.axis_index("x"), lax.axis_index("y"), lax.axis_index("c"))
        copies = []
        for i in range(n):
            peer = tuple(1 - m if f else m for m, f in zip(me, flips[i]))
            cp = pltpu.make_async_remote_copy(src_ref=srcs[i], dst_ref=dsts[i], send_sem=send_sems.at[i],
                                              recv_sem=recv_sems.at[i], device_id=peer, device_id_type=MESH)
            cp.start()
            copies.append(cp)
        for cp in copies:
            cp.wait()

    return pl.pallas_call(
        body, in_specs=[hbm] * n, out_specs=[hbm] * n,
        out_shape=[jax.ShapeDtypeStruct(b.shape, b.dtype) for b in bufs],
        scratch_shapes=[pltpu.SemaphoreType.DMA((n,)), pltpu.SemaphoreType.DMA((n,))],
        name=name, compiler_params=pltpu.CompilerParams(has_side_effects=True))(*bufs)


FLIP_C = (False, False, True)
FLIP_Y = (False, True, False)
FLIP_X = (True, False, False)


LAYER_ROWS = PACK_ROWS // DEPTH
HALF_ROWS = LAYER_ROWS // 2
SUM_ROWS = 128
GATHER_IDS = (1, 2, 3, 4)
SCATTER_IDS = (5, 6, 7, 8)


def _mesh_pos():
    ix, iy, core = lax.axis_index("x"), lax.axis_index("y"), lax.axis_index("c")
    others = [(ix, 1 - iy), (1 - ix, iy), (1 - ix, 1 - iy)]
    return ix, iy, core, others


def _pack_big(w_in, w_out, w_up, w_down):
    pad = jnp.pad(w_in, ((0, 0), (0, 0), (0, 1024 - w_in.shape[-1])))
    return jnp.concatenate([pad, w_up, w_down, w_out], axis=1)


def _unpack_big(p):
    w_in = p[..., 0:1024, :D_IN // 4]
    return w_in, p[..., 3072:LAYER_ROWS, :], p[..., 1024:2048, :], p[..., 2048:3072, :]


def _gather_layer(packed, collective_id):
    src = jax.new_ref(packed, memory_space=pltpu.MemorySpace.HBM)
    out = jax.empty_ref(jax.ShapeDtypeStruct((4,) + packed.shape, packed.dtype), memory_space=pltpu.MemorySpace.HBM)
    dma = pltpu.SemaphoreType.DMA

    @pl.kernel(mesh=plsc.ScalarSubcoreMesh(axis_name="seq", num_cores=1), name="gather_%d" % collective_id,
               scratch_types=(dma,) * 12, compiler_params=pltpu.CompilerParams(collective_id=collective_id))
    def launch(*sems):
        send_sems, recv_sems = sems[:6], sems[6:]
        ix, iy, core, others = _mesh_pos()
        barrier = pltpu.get_barrier_semaphore()
        for peer in [(ox, oy, core) for ox, oy in others] + [(ix, iy, 1 - core)]:
            pl.semaphore_signal(barrier, inc=1, device_id=peer, device_id_type=MESH)
        pl.semaphore_wait(barrier, 4)
        mine = pl.ds(core * HALF_ROWS, HALF_ROWS)
        theirs = pl.ds((1 - core) * HALF_ROWS, HALF_ROWS)

        def copy(k, src_ref, owner, rows, to):
            return pltpu.make_async_remote_copy(
                src_ref=src_ref, dst_ref=out.at[owner, rows], send_sem=send_sems[k], recv_sem=recv_sems[k],
                device_id=to, device_id_type=MESH)

        chip = 2 * ix + iy
        first = [copy(j, src.at[mine], chip, mine, (*others[j], core)) for j in range(3)]
        for cp in first:
            cp.start()
        passed = []
        for j, (ox, oy) in enumerate(others):
            owner = 2 * ox + oy
            copy(j, src.at[mine], owner, mine, (ix, iy, core)).wait_recv()
            fwd = copy(3 + j, out.at[owner, mine], owner, mine, (ix, iy, 1 - core))
            fwd.start()
            passed.append(fwd)
        for j, (ox, oy) in enumerate(others):
            copy(3 + j, src.at[theirs], 2 * ox + oy, theirs, (ix, iy, core)).wait_recv()
        for cp in first + passed:
            cp.wait_send()

    launch()
    return out[...]


def _rs_sibling(per_chip):
    hbm = pl.BlockSpec(memory_space=pltpu.HBM)

    def body(src, out, send_sem, recv_sem):
        ix, iy, core, _ = _mesh_pos()
        give = pl.ds((1 - core) * HALF_ROWS, HALF_ROWS)
        cp = pltpu.make_async_remote_copy(src_ref=src.at[pl.ds(0, 4), give], dst_ref=out, send_sem=send_sem,
                                          recv_sem=recv_sem, device_id=(ix, iy, 1 - core), device_id_type=MESH)
        cp.start()
        cp.wait()

    return pl.pallas_call(
        body, in_specs=[hbm], out_specs=hbm, out_shape=jax.ShapeDtypeStruct((4, HALF_ROWS, 1024), F32),
        scratch_shapes=[pltpu.SemaphoreType.DMA, pltpu.SemaphoreType.DMA],
        name="rs_sibling", compiler_params=pltpu.CompilerParams(has_side_effects=True))(per_chip)


def _rs_add_pair(per_chip, sib, chip, core):
    nb = HALF_ROWS // SUM_ROWS
    blk = (1, SUM_ROWS, 1024)

    def body(chip_ref, core_ref, a_ref, b_ref, own_ref, send_ref):
        total = a_ref[0] + b_ref[0]
        send_ref[0] = total.astype(BF16)

        @pl.when(pl.program_id(1) == chip_ref[0])
        def _():
            own_ref[...] = total

    return pl.pallas_call(
        body,
        grid_spec=pltpu.PrefetchScalarGridSpec(
            num_scalar_prefetch=2, grid=(nb, 4),
            in_specs=[pl.BlockSpec(blk, lambda i, k, ch, co: (k, co[0] * nb + i, 0)),
                      pl.BlockSpec(blk, lambda i, k, ch, co: (k, i, 0))],
            out_specs=[pl.BlockSpec((SUM_ROWS, 1024), lambda i, k, ch, co: (i, 0)),
                       pl.BlockSpec(blk, lambda i, k, ch, co: (k, i, 0))]),
        out_shape=[jax.ShapeDtypeStruct((HALF_ROWS, 1024), F32), jax.ShapeDtypeStruct((4, HALF_ROWS, 1024), BF16)],
        name="rs_add_pair", compiler_params=_params(("arbitrary", "arbitrary")))(
            chip.reshape(1), core.reshape(1), per_chip, sib)


def _rs_chips(pair, collective_id):
    src_ref = jax.new_ref(pair, memory_space=pltpu.MemorySpace.HBM)
    got_ref = jax.empty_ref(jax.ShapeDtypeStruct((3,) + pair.shape[1:], pair.dtype), memory_space=pltpu.MemorySpace.HBM)
    dma = pltpu.SemaphoreType.DMA

    @pl.kernel(mesh=plsc.ScalarSubcoreMesh(axis_name="seq", num_cores=1), name="rs_chips_%d" % collective_id,
               scratch_types=(dma,) * 6, compiler_params=pltpu.CompilerParams(collective_id=collective_id))
    def launch(s0, s1, s2, r0, r1, r2):
        ix, iy, core, others = _mesh_pos()
        barrier = pltpu.get_barrier_semaphore()
        for ox, oy in others:
            pl.semaphore_signal(barrier, inc=1, device_id=(ox, oy, core), device_id_type=MESH)
        pl.semaphore_wait(barrier, 3)
        copies = []
        for j, ((ox, oy), send_sem, recv_sem) in enumerate(zip(others, (s0, s1, s2), (r0, r1, r2))):
            cp = pltpu.make_async_remote_copy(src_ref=src_ref.at[2 * ox + oy], dst_ref=got_ref.at[j], send_sem=send_sem,
                                              recv_sem=recv_sem, device_id=(ox, oy, core), device_id_type=MESH)
            cp.start()
            copies.append(cp)
        for cp in copies:
            cp.wait()

    launch()
    return got_ref[...]


def _rs_add_chips(own, got, core, layer, buf):
    nb = HALF_ROWS // SUM_ROWS
    blk = (1, SUM_ROWS, 1024)

    def body(core_ref, a_ref, g0_ref, g1_ref, g2_ref, *rest):
        o_ref = rest[-1]
        o_ref[0, 0] = ((a_ref[...] + g0_ref[0].astype(F32)) + g1_ref[0].astype(F32)) + g2_ref[0].astype(F32)

    aliased = [] if buf is None else [buf]
    return pl.pallas_call(
        body,
        grid_spec=pltpu.PrefetchScalarGridSpec(
            num_scalar_prefetch=1, grid=(nb,),
            in_specs=[pl.BlockSpec((SUM_ROWS, 1024), lambda i, c: (i, 0))]
            + [pl.BlockSpec(blk, functools.partial(lambda j, i, c: (j, i, 0), j)) for j in range(3)]
            + [pl.BlockSpec(memory_space=pl.ANY)] * len(aliased),
            out_specs=pl.BlockSpec((1, 1, SUM_ROWS, 1024), lambda i, c: (layer, c[0], i, 0))),
        out_shape=jax.ShapeDtypeStruct((DEPTH, 2, HALF_ROWS, 1024), F32),
        input_output_aliases={} if buf is None else {5: 0},
        name="rs_add_chips", compiler_params=_params(("arbitrary",)))(core.reshape(1), own, got, got, got, *aliased)


def _rs_share(buf):
    hbm = pl.BlockSpec(memory_space=pltpu.HBM)

    def body(src, out, send_sem, recv_sem):
        ix, iy, core, _ = _mesh_pos()
        layers = pl.ds(0, DEPTH)
        cp = pltpu.make_async_remote_copy(src_ref=src.at[layers, core], dst_ref=out.at[layers, core], send_sem=send_sem,
                                          recv_sem=recv_sem, device_id=(ix, iy, 1 - core), device_id_type=MESH)
        cp.start()
        pltpu.make_async_remote_copy(src_ref=src.at[layers, 1 - core], dst_ref=out.at[layers, 1 - core],
                                     send_sem=send_sem, recv_sem=recv_sem, device_id=(ix, iy, core),
                                     device_id_type=MESH).wait_recv()
        cp.wait_send()

    return pl.pallas_call(
        body, in_specs=[hbm], out_specs=hbm, out_shape=jax.ShapeDtypeStruct(buf.shape, buf.dtype),
        input_output_aliases={0: 0}, scratch_shapes=[pltpu.SemaphoreType.DMA, pltpu.SemaphoreType.DMA],
        name="rs_share", compiler_params=pltpu.CompilerParams(has_side_effects=True))(buf)


def _scatter_start(per_chip, chip, core, layer):
    own, pair = _rs_add_pair(per_chip, _rs_sibling(per_chip), chip, core)
    return own, _rs_chips(pair, SCATTER_IDS[layer])


def _allreduce_small(tag, buf):
    for name, flip in (("c", FLIP_C), ("y", FLIP_Y), ("x", FLIP_X)):
        (other,) = _swap(tag + "_swap_" + name, [buf], [flip])
        (buf,) = _ew(tag + "_add_" + name, _add2_fn, [buf, other], 1)
    return buf


def _permute_in_cols(w):
    z = jnp.zeros(w.shape[:-1] + (D_IN_PAD - D_IN,), w.dtype)
    return jnp.concatenate([w[..., 1024:3072], w[..., :1024], w[..., 3080:D_IN], w[..., 3072:3080], z], axis=-1)


IN_COL_SEGMENTS = ((0, 1024, 2048), (1024, 3072, -1024), (3072, 3080, GATE0 - 3072), (3080, D_IN, COL_CX - 3080))


def _orig_cols(w_pad, lo, hi):
    parts = []
    for first, last, shift in IN_COL_SEGMENTS:
        a, b = max(lo, first), min(hi, last)
        if a < b:
            parts.append(w_pad[..., a + shift:b + shift])
    return parts


def _block_diag(w):
    n, d = w.shape[0], w.shape[1]
    on_diag = jnp.arange(n)[:, None] == jnp.arange(n)[None, :]
    return jnp.where(on_diag[:, None, :, None], w[:, :, None, :], 0.0).reshape(n * d, n * d)


def _diag_blocks(w):
    return jnp.stack([w[64 * n:64 * (n + 1), 64 * n:64 * (n + 1)] for n in range(4)], axis=0)


def _gate_row(v):
    return jnp.concatenate([jnp.zeros((4,), F32), v, jnp.zeros((120,), F32)]).reshape(1, 128)


SMALL = ("norm1_g", "hgrn_lb_logits", "hgrn_norm_g", "gdn_conv_w", "gdn_a_log", "gdn_dt_bias", "gdn_norm_g",
         "lru_conv_w", "lru_conv_b", "lru_w_a", "lru_b_a", "lru_w_x", "lru_b_x", "lru_lambda", "norm2_g",
         "final_norm_g")


def _flatten_small(tree):
    flat = jnp.concatenate([tree[k].reshape(-1) for k in sorted(tree)])
    rows = -(-flat.shape[0] // (512 * 128)) * 512
    return jnp.pad(flat, (0, rows * 128 - flat.shape[0])).reshape(rows, 128)


def _unflatten_small(buf, shapes):
    flat = buf.reshape(-1)
    out, off = {}, 0
    for k in sorted(shapes):
        size = 1
        for d in shapes[k]:
            size *= d
        out[k] = flat[off:off + size].reshape(shapes[k])
        off += size
    return out


def _layer_fwd(l, x, p, w_in, w_out, w_up, w_down):
    sv = {"x0": x}
    (h,) = _rowwise_fwd("norm1", _rmsnorm_fn, [p["norm1_g"]], [x], [D_MODEL], [BF16], 512)
    proj = _matmul("proj_in", h, w_in, "nn")
    sv["h"], sv["proj"] = h, proj
    aq, af, ag = _cols(proj, COL_AQ, A_W), _cols(proj, COL_AF, A_W), _cols(proj, COL_AG, A_W)
    ai = lax.slice_in_dim(proj, COL_AI, COL_AI + A_W, axis=1)
    bz, gate, cy = _cols(proj, COL_BZ, B_W), _cols(proj, GATE0, 128), _cols(proj, COL_CY, C_W)
    q, k, lf = _rowwise_fwd("hgrn_pre", functools.partial(_hgrn_pre_fn, l), [p["lb_logits"]], [aq, af],
                            [A_W] * 3, [F32] * 3, 512)
    o_a, sts_a = _hgrn_fwd("hgrn_chunk", q, k, ai, lf)
    (y_a,) = _rowwise_fwd("hgrn_post", _hgrn_post_fn, [p["hgrn_norm_g"]], [o_a, ag], [A_W], [BF16], 512)
    sv.update(aq=aq, af=af, ai=ai, ag=ag, hq=q, hk=k, hlf=lf, o_a=o_a, sts_a=sts_a)
    qkvc = _conv_fwd("gdn_conv", proj, 0, 3 * B_W, p["gdn_conv_w"], jnp.zeros((1, 3 * B_W), F32), 256)
    gq, gk, gv, gb, gla = _rowwise_fwd("gdn_pre", _gdn_pre_fn, [p["gdn_a_log"], p["gdn_dt_bias"]], [qkvc, gate],
                                       [B_W] * 5, [F32] * 5, 256)
    *wy, sv["gdn_inv"] = _gdn_prep_fwd("gdn_prep", gq, gk, gv, gb, gla)
    o_b, sts_b = _gdn_rec_fwd("gdn_rec", *wy)
    sv["wy"] = wy
    (y_b,) = _rowwise_fwd("gdn_post", _gdn_post_fn, [p["gdn_norm_g"]], [o_b, bz], [B_W], [BF16], 512)
    sv.update(qkvc=qkvc, gate=gate, bz=bz, gq=gq, gk=gk, gv=gv, gb=gb, gla=gla, o_b=o_b, sts_b=sts_b)
    xc = _conv_fwd("lru_conv", proj, COL_CX, C_W, p["lru_conv_w"], p["lru_conv_b"], 512)
    a, b = _rowwise_fwd("lru_pre", _lru_pre_fn, [p["lru_w_a"], p["lru_b_a"], p["lru_w_x"], p["lru_b_x"], p["lru_lambda"]],
                        [xc], [C_W] * 2, [F32] * 2, 512)
    hs, h_prev = _lru_scan_fwd("lru_scan", a, b)
    (y_c,) = _rowwise_fwd("lru_post", _lru_post_fn, [], [hs, cy], [C_W], [BF16], 512)
    sv.update(xc=xc, la=a, hs=hs, h_prev=h_prev, cy=cy)
    mixed = jnp.concatenate([y_b, y_a, y_c], axis=1)
    x1 = _matmul("proj_out", mixed, w_out, "nn", res=x)
    (h2,) = _rowwise_fwd("norm2", _rmsnorm_fn, [p["norm2_g"]], [x1], [D_MODEL], [BF16], 512)
    act = _matmul("mlp_up", h2, w_up, "nn", epi=_relu2, out_dtype=BF16)
    x2 = _matmul("mlp_down", act, w_down, "nn", res=x1)
    sv.update(mixed=mixed, x1=x1, h2=h2, act=act)
    return x2, sv


def _layer_bwd(l, dx, dx16, p, sv, w_in, w_out, w_up, w_down):
    gs = {}
    dup = _matmul("mlp_down_dx", dx16, w_down, "nt", epi=_drelu2, epi_in=sv["act"], out_dtype=BF16)
    d_w_down = _matmul("mlp_down_dw", sv["act"], dx16, "tn")
    dh2 = _matmul("mlp_up_dx", dup, w_up, "nt")
    d_w_up = _matmul("mlp_up_dw", sv["h2"], dup, "tn")
    (gs["norm2_g"],), (dx1, dx1_16) = _rowwise_bwd("norm2_bwd", _rmsnorm_fn, [p["norm2_g"]], [sv["x1"]], [dh2], 512,
                                                   res=dx, copy16=True)
    dmixed = _matmul("proj_out_dx", dx1_16, w_out, "nt")
    d_w_out = _matmul("proj_out_dw", sv["mixed"], dx1_16, "tn")
    dy_b, dy_a, dy_c = _cols(dmixed, 0, B_W), _cols(dmixed, B_W, A_W), _cols(dmixed, B_W + A_W, C_W)
    (gs["hgrn_norm_g"],), (do_a, dag) = _rowwise_bwd("hgrn_post_bwd", _hgrn_post_fn, [p["hgrn_norm_g"]],
                                                     [sv["o_a"], sv["ag"]], [dy_a], 512, dtypes=[F32, BF16])
    dq, dk, dai, dlf = _hgrn_bwd("hgrn_chunk_bwd", sv["hq"], sv["hk"], sv["ai"], sv["hlf"], sv["sts_a"], do_a)
    (gs["lb_logits"],), (daq, daf) = _rowwise_bwd("hgrn_pre_bwd", functools.partial(_hgrn_pre_fn, l), [p["lb_logits"]],
                                                  [sv["aq"], sv["af"]], [dq, dk, dlf], 512, dtypes=[BF16, BF16])
    (gs["gdn_norm_g"],), (do_b, dbz) = _rowwise_bwd("gdn_post_bwd", _gdn_post_fn, [p["gdn_norm_g"]],
                                                    [sv["o_b"], sv["bz"]], [dy_b], 512, dtypes=[F32, BF16])
    d_wy = _gdn_rec_bwd("gdn_rec_bwd", *sv["wy"], sv["sts_b"], do_b)
    dgq, dgk, dgv, dgb, dgla = _gdn_prep_bwd("gdn_prep_bwd", sv["gq"], sv["gk"], sv["gv"], sv["gb"], sv["gla"],
                                             sv["gdn_inv"], d_wy)
    (gs["gdn_a_log"], gs["gdn_dt_bias"]), (dqkvc, dgate) = _rowwise_bwd(
        "gdn_pre_bwd", _gdn_pre_fn, [p["gdn_a_log"], p["gdn_dt_bias"]], [sv["qkvc"], sv["gate"]],
        [dgq, dgk, dgv, dgb, dgla], 256, dtypes=[F32, BF16])
    dqkv, gs["gdn_conv_w"], _ = _conv_bwd("gdn_conv_bwd", dqkvc, sv["proj"], 0, 3 * B_W, p["gdn_conv_w"], 256)
    _, (dhs, dcy) = _rowwise_bwd("lru_post_bwd", _lru_post_fn, [], [sv["hs"], sv["cy"]], [dy_c], 512, dtypes=[F32, BF16])
    g = _lru_scan_bwd("lru_scan_bwd", sv["la"], dhs)
    (da,) = _rowwise_fwd("lru_da", _lru_da_fn, [], [g, sv["h_prev"]], [C_W], [F32], 512)
    lru_params = [p["lru_w_a"], p["lru_b_a"], p["lru_w_x"], p["lru_b_x"], p["lru_lambda"]]
    dps, (dxc,) = _rowwise_bwd("lru_pre_bwd", _lru_pre_fn, lru_params, [sv["xc"]], [da, g], 512)
    gs["lru_w_a"], gs["lru_b_a"], gs["lru_w_x"], gs["lru_b_x"], gs["lru_lambda"] = dps
    dcx, gs["lru_conv_w"], gs["lru_conv_b"] = _conv_bwd("lru_conv_bwd", dxc, sv["proj"], COL_CX, C_W, p["lru_conv_w"], 512)
    s = dx.shape[0]
    dproj = jnp.concatenate([dqkv, dbz, daq, daf, dai.astype(BF16), dag, dcx, dcy, dgate,
                             jnp.zeros((s, D_IN_PAD - GATE0 - 128), BF16)], axis=1)
    dh = _matmul("proj_in_dx", dproj, w_in, "nt")
    d_w_in = _matmul("proj_in_dw", sv["h"], dproj, "tn")
    (gs["norm1_g"],), (dx0, dx0_16) = _rowwise_bwd("norm1_bwd", _rmsnorm_fn, [p["norm1_g"]], [sv["x0"]], [dh], 512,
                                                   res=dx1, copy16=True)
    return dx0, dx0_16, dict(w_in=d_w_in, w_out=d_w_out, w_up=d_w_up, w_down=d_w_down), gs


def _layer_params(l, lb_logits, hgrn_norm_g, gdn_conv_w, gdn_a_log, gdn_dt_bias, gdn_norm_g, lru_conv_w, lru_conv_b,
                  lru_w_a, lru_b_a, lru_w_x, lru_b_x, lru_lambda, norm1_g, norm2_g):
    row = lambda v: v.reshape(1, -1)
    return dict(
        norm1_g=row(norm1_g[l]), norm2_g=row(norm2_g[l]), lb_logits=lb_logits,
        hgrn_norm_g=row(jnp.tile(hgrn_norm_g[l], A_W // A_HD)),
        gdn_conv_w=gdn_conv_w[l], gdn_a_log=_gate_row(gdn_a_log[l]), gdn_dt_bias=_gate_row(gdn_dt_bias[l]),
        gdn_norm_g=row(jnp.tile(gdn_norm_g[l], B_NH)),
        lru_conv_w=lru_conv_w[l], lru_conv_b=row(lru_conv_b[l]), lru_w_a=_block_diag(lru_w_a[l]), lru_b_a=row(lru_b_a[l]),
        lru_w_x=_block_diag(lru_w_x[l]), lru_b_x=row(lru_b_x[l]), lru_lambda=row(lru_lambda[l]))


def _small_grads_to_reference(gs_layers, d_final_g, d_logits):
    st = lambda k, f=lambda z: z: jnp.stack([f(g[k]) for g in gs_layers], axis=0)
    vec = lambda z: z.reshape(-1)
    return dict(
        norm1_g=st("norm1_g", vec), norm2_g=st("norm2_g", vec), hgrn_lb_logits=d_logits,
        hgrn_norm_g=st("hgrn_norm_g", lambda z: z.reshape(A_W // A_HD, A_HD).sum(0)),
        gdn_conv_w=st("gdn_conv_w"), gdn_a_log=st("gdn_a_log", lambda z: z[0, 4:8]),
        gdn_dt_bias=st("gdn_dt_bias", lambda z: z[0, 4:8]),
        gdn_norm_g=st("gdn_norm_g", lambda z: z.reshape(B_NH, B_HD).sum(0)),
        lru_conv_w=st("lru_conv_w"), lru_conv_b=st("lru_conv_b", vec), lru_w_a=st("lru_w_a", _diag_blocks),
        lru_b_a=st("lru_b_a", vec), lru_w_x=st("lru_w_x", _diag_blocks), lru_b_x=st("lru_b_x", vec),
        lru_lambda=st("lru_lambda", vec), final_norm_g=d_final_g.reshape(-1))


def _local_step(x, tgt, small, layer_weights, on_layer_grads=None):
    layer_p = [_layer_params(l, small["hgrn_lb_logits"], small["hgrn_norm_g"], small["gdn_conv_w"], small["gdn_a_log"],
                             small["gdn_dt_bias"], small["gdn_norm_g"], small["lru_conv_w"], small["lru_conv_b"],
                             small["lru_w_a"], small["lru_b_a"], small["lru_w_x"], small["lru_b_x"], small["lru_lambda"],
                             small["norm1_g"], small["norm2_g"]) for l in range(DEPTH)]
    saved, weights = [], []
    for l in range(DEPTH):
        x, w = layer_weights(l, x)
        x, sv = _layer_fwd(l, x, layer_p[l], *w)
        saved.append(sv)
        weights.append(w)
    loss, dx, d_final_g, dx16 = _loss_and_grad("final_loss", small["final_norm_g"].reshape(1, -1), x, tgt)
    big = [None] * DEPTH
    gs_layers = [None] * DEPTH
    d_logits = jnp.zeros_like(small["hgrn_lb_logits"])
    for l in reversed(range(DEPTH)):
        dx, dx16, big[l], gs_layers[l] = _layer_bwd(l, dx, dx16, layer_p[l], saved[l], *weights[l])
        d_logits = d_logits + gs_layers[l]["lb_logits"]
        if on_layer_grads is not None:
            big[l] = on_layer_grads(l, big[l])
    return loss, dx, big, _small_grads_to_reference(gs_layers, d_final_g, d_logits)


def kernel(x, norm1_g, w_in, hgrn_lb_logits, hgrn_norm_g, gdn_conv_w, gdn_a_log, gdn_dt_bias, gdn_norm_g, lru_conv_w, lru_conv_b, lru_w_a, lru_b_a, lru_w_x, lru_b_x, lru_lambda, w_out, norm2_g, w_up, w_down, final_norm_g, loss_target, m_norm1_g, m_w_in, m_hgrn_lb_logits, m_hgrn_norm_g, m_gdn_conv_w, m_gdn_a_log, m_gdn_dt_bias, m_gdn_norm_g, m_lru_conv_w, m_lru_conv_b, m_lru_w_a, m_lru_b_a, m_lru_w_x, m_lru_b_x, m_lru_lambda, m_w_out, m_norm2_g, m_w_up, m_w_down, m_final_norm_g, v_norm1_g, v_w_in, v_hgrn_lb_logits, v_hgrn_norm_g, v_gdn_conv_w, v_gdn_a_log, v_gdn_dt_bias, v_gdn_norm_g, v_lru_conv_w, v_lru_conv_b, v_lru_w_a, v_lru_b_a, v_lru_w_x, v_lru_b_x, v_lru_lambda, v_w_out, v_norm2_g, v_w_up, v_w_down, v_final_norm_g):
    args = dict(locals())
    ix, iy, core = lax.axis_index("x"), lax.axis_index("y"), lax.axis_index("c")
    chip = 2 * ix + iy

    packed = _pack_big(w_in, w_out, w_up, w_down).astype(BF16)
    gathered = [_gather_layer(packed[l], GATHER_IDS[l]) for l in range(DEPTH)]

    def layer_weights(l, x):
        got, own = gathered[l], packed[l]
        if l:
            got, own, x = lax.optimization_barrier((got, own, x))
        shares = lax.dynamic_update_slice(got, own[None], (chip, 0, 0))
        parts = [_unpack_big(shares[k]) for k in range(4)]
        return x, (_permute_in_cols(jnp.concatenate([pt[0] for pt in parts], axis=-1)),
                   jnp.concatenate([parts[k][1] for k in W_OUT_CHIP_ORDER], axis=0),
                   jnp.concatenate([pt[2] for pt in parts], axis=-1),
                   jnp.concatenate([pt[3] for pt in parts], axis=0))

    n_gc, n_lc = gdn_conv_w.shape[-1], lru_conv_w.shape[-1]
    conv_full = dict(
        gdn_conv_w=lax.dynamic_update_slice(jnp.zeros((DEPTH, 4, 4 * n_gc), F32), gdn_conv_w * 0.5, (0, 0, chip * n_gc)),
        lru_conv_w=lax.dynamic_update_slice(jnp.zeros((DEPTH, 4, 4 * n_lc), F32), lru_conv_w * 0.5, (0, 0, chip * n_lc)))
    conv_shapes = {k: v.shape for k, v in conv_full.items()}
    conv_full = _unflatten_small(_allreduce_small("convw", _flatten_small(conv_full)), conv_shapes)

    small = {k: args[k] for k in SMALL}
    small.update(conv_full)

    n_in = D_IN // 4

    def start_scatter(l, b):
        pieces = []
        for k in range(4):
            cols = jnp.concatenate(_orig_cols(b["w_in"], k * n_in, (k + 1) * n_in), axis=1)
            r0 = 256 * W_OUT_CHIP_ORDER.index(k)
            pieces += [jnp.pad(cols, ((0, 0), (0, 1024 - n_in))), b["w_up"][:, k * 1024:(k + 1) * 1024],
                       b["w_down"][k * 1024:(k + 1) * 1024], b["w_out"][r0:r0 + 256]]
        per_chip = jnp.concatenate(pieces, axis=0).reshape(4, LAYER_ROWS, 1024)
        return _scatter_start(per_chip, chip, core, l)

    loss, grad_x, started, sg = _local_step(x[0], loss_target[0], small, layer_weights, start_scatter)

    sg["loss"] = loss[0, :1]
    shapes = {k: v.shape for k, v in sg.items()}
    sg = _unflatten_small(_allreduce_small("smallgrad", _flatten_small(sg)), shapes)
    loss_out = sg.pop("loss")[0]

    buf = None
    for l in reversed(range(DEPTH)):
        own, got = started[l]
        buf = _rs_add_chips(own, got, core, l, buf)
    g_in, g_out, g_up, g_down = _unpack_big(_rs_share(buf).reshape(DEPTH, LAYER_ROWS, 1024))

    grads = dict(sg)
    grads["gdn_conv_w"] = lax.dynamic_slice_in_dim(sg["gdn_conv_w"], chip * n_gc, n_gc, axis=2)
    grads["lru_conv_w"] = lax.dynamic_slice_in_dim(sg["lru_conv_w"], chip * n_lc, n_lc, axis=2)
    grads.update(w_in=g_in, w_out=g_out, w_up=g_up, w_down=g_down)

    names = ['norm1_g', 'w_in', 'hgrn_lb_logits', 'hgrn_norm_g', 'gdn_conv_w', 'gdn_a_log', 'gdn_dt_bias', 'gdn_norm_g',
             'lru_conv_w', 'lru_conv_b', 'lru_w_a', 'lru_b_a', 'lru_w_x', 'lru_b_x', 'lru_lambda', 'w_out', 'norm2_g',
             'w_up', 'w_down', 'final_norm_g']
    big_names = ("w_in", "w_out", "w_up", "w_down")
    delta, new_m, new_v = {}, {}, {}
    for k in big_names:
        delta[k], new_m[k], new_v[k] = _adam("adam_" + k, args[k], grads[k], args["m_" + k], args["v_" + k])
    small_names = [k for k in names if k not in big_names]
    shapes = {k: args[k].shape for k in small_names}
    flat = [_flatten_small({k: src(k) for k in small_names})
            for src in (lambda k: args[k], lambda k: grads[k], lambda k: args["m_" + k], lambda k: args["v_" + k])]
    outs = _ew("adam_small", _adam_fn, flat, 3)
    for dst, o in zip((delta, new_m, new_v), outs):
        dst.update(_unflatten_small(o, shapes))
    return (loss_out, grad_x[None], *[grads[k] for k in names], *[delta[k] for k in names],
            *[new_m[k] for k in names], *[new_v[k] for k in names])
```

```python
import functools

import jax
import jax.numpy as jnp
from jax import lax
from jax.experimental import pallas as pl
from jax.experimental.pallas import tpu as pltpu
from jax.experimental.pallas import tpu_sc as plsc

F32 = jnp.float32
BF16 = jnp.bfloat16
MESH = pl.DeviceIdType.MESH

DEPTH = 4
D_MODEL = 1024
A_W, B_W, C_W = 256, 512, 256
A_HD, B_HD = 64, 128
B_NH = B_W // B_HD
D_IN = 3592
D_IN_PAD = 3840
COL_BZ, COL_AQ, COL_AF, COL_AI, COL_AG, COL_CX, COL_CY = 1536, 2048, 2304, 2560, 2816, 3072, 3328
W_OUT_CHIP_ORDER = (1, 2, 0, 3)
GATE0 = 3584
D_FF = 4096
CHUNK = 64
SUB = 16
EPS = 1e-6
TINY = 1e-30
MASKED_EXPONENT = -1e30
RG_C = 8.0
LR, B1, B2, AEPS, WD, STEP = 0.001, 0.9, 0.999, 1e-08, 0.01, 10
VMEM_LIMIT = 56 * 1024 * 1024
PACK_ROWS = 13 * 1024

NN = ((1,), (0,))
NT = ((1,), (1,))
TN = ((0,), (0,))


DIMS = {"nn": NN, "nt": NT, "tn": TN}


def _split(a, n):
    parts, r = [], a
    for i in range(n):
        p = r.astype(BF16)
        parts.append(p)
        if i + 1 < n:
            r = r - p.astype(F32)
    return parts


def _split_dot(a, b, mode, na, nb):
    (ca,), (cb,) = DIMS[mode]
    dims = ((DIMS[mode]), ((), ())) if a.ndim == 2 else (((ca + 1,), (cb + 1,)), ((0,), (0,)))
    pa, pb = _split(a, na), _split(b, nb)
    acc = None
    for i in range(na):
        for j in range(nb):
            if i + j < max(na, nb):
                t = lax.dot_general(pa[i], pb[j], dims, preferred_element_type=F32)
                acc = t if acc is None else acc + t
    return acc


@functools.partial(jax.custom_vjp, nondiff_argnums=(2, 3, 4, 5))
def _mdot(a, b, mode, na, nb, need):
    return _split_dot(a, b, mode, na, nb)


def _mdot_fwd(a, b, mode, na, nb, need):
    return _split_dot(a, b, mode, na, nb), (a, b)


def _mdot_bwd(mode, na, nb, need, res, ct):
    a, b = res
    n = max(na, nb)
    da, db = jnp.zeros_like(a), jnp.zeros_like(b)
    if mode == "nn":
        if need[0]:
            da = _split_dot(ct, b, "nt", n, nb)
        if need[1]:
            db = _split_dot(a, ct, "tn", na, n)
    elif mode == "nt":
        if need[0]:
            da = _split_dot(ct, b, "nn", n, nb)
        if need[1]:
            db = _split_dot(ct, a, "tn", n, na)
    else:
        if need[0]:
            da = _split_dot(b, ct, "nt", nb, n)
        if need[1]:
            db = _split_dot(a, ct, "nn", na, n)
    return da, db


_mdot.defvjp(_mdot_fwd, _mdot_bwd)
BOTH = (True, True)


def _bdot(a, b, mode="nn"):
    return _mdot(a, b, mode, 1, 1, BOTH)


def _group_sum(x, ones):
    return _mdot(x, ones, "nn", 2, 1, (True, False))


def _running_sum(x):
    shape = x.shape[:-2] + (CHUNK, CHUNK)
    tri = (_iota(shape, len(shape) - 1) <= _iota(shape, len(shape) - 2)).astype(F32)
    return _mdot(tri, x, "nn", 1, 3, (False, True))


def _iota(shape, d):
    return lax.broadcasted_iota(jnp.int32, shape, d)


def _block_ones(width, group):
    return (_iota((width, width), 0) // group == _iota((width, width), 1) // group).astype(F32)


def _params(sem):
    return pltpu.CompilerParams(dimension_semantics=sem, vmem_limit_bytes=VMEM_LIMIT)


def _tile(dim, prefs):
    for p in prefs:
        if dim % p == 0:
            return p
    return dim


def _whole(shape):
    return pl.BlockSpec(shape, lambda i: (0,) * len(shape))


def _rowspec(t, w, col0=0):
    cb = col0 // w
    return pl.BlockSpec((t, w), lambda i: (i, cb))


def _cols(arr, col0, width):
    assert col0 % width == 0
    return (arr, col0, width)


def _row_operands(rows, t):
    arrs, specs = [], []
    for r in rows:
        arr, col0, w = r if isinstance(r, tuple) else (r, 0, r.shape[1])
        arrs.append(arr)
        specs.append(_rowspec(t, w, col0))
    return arrs, specs


def _rowwise_fwd(name, fn, params, rows, out_widths, out_dtypes, t):
    rows, row_specs = _row_operands(rows, t)
    s = rows[0].shape[0]
    n_p, n_r = len(params), len(rows)

    def body(*refs):
        p = [r[...] for r in refs[:n_p]]
        xs = [r[...].astype(F32) for r in refs[n_p:n_p + n_r]]
        outs = fn(pl.program_id(0) * t, *p, *xs)
        for o_ref, o in zip(refs[n_p + n_r:], outs):
            o_ref[...] = o.astype(o_ref.dtype)

    return pl.pallas_call(
        body, grid=(s // t,),
        in_specs=[_whole(p.shape) for p in params] + row_specs,
        out_specs=[_rowspec(t, w) for w in out_widths],
        out_shape=[jax.ShapeDtypeStruct((s, w), d) for w, d in zip(out_widths, out_dtypes)],
        name=name, compiler_params=_params(("arbitrary",)))(*params, *rows)


def _rowwise_bwd(name, fn, params, rows, douts, t, need=None, res=None, dtypes=None, copy16=False):
    rows, row_specs = _row_operands(rows, t)
    douts, dout_specs = _row_operands(douts, t)
    s = rows[0].shape[0]
    n_p, n_r, n_o = len(params), len(rows), len(douts)
    need = [True] * n_r if need is None else need
    n_res = 0 if res is None else 1

    def body(*refs):
        p = [r[...] for r in refs[:n_p]]
        xs = [r[...].astype(F32) for r in refs[n_p:n_p + n_r]]
        dys = [r[...].astype(F32) for r in refs[n_p + n_r:n_p + n_r + n_o]]
        k = n_p + n_r + n_o
        res_ref = refs[k] if n_res else None
        dp_refs = refs[k + n_res:k + n_res + n_p]
        dx_refs = refs[k + n_res + n_p:]
        row0 = pl.program_id(0) * t
        _, vjp = jax.vjp(lambda *a: tuple(fn(row0, *a)), *p, *xs)
        g = vjp(tuple(dys))

        @pl.when(pl.program_id(0) == 0)
        def _():
            for r in dp_refs:
                r[...] = jnp.zeros_like(r)

        for r, gp in zip(dp_refs, g[:n_p]):
            r[...] += gp
        gx = [gi for gi, nd in zip(g[n_p:], need) if nd]
        if n_res:
            gx[0] = gx[0] + res_ref[...]
        if copy16:
            gx.append(gx[0])
        for r, gi in zip(dx_refs, gx):
            r[...] = gi.astype(r.dtype)

    widths = [sp.block_shape[1] for sp, nd in zip(row_specs, need) if nd]
    dtypes = [F32] * len(widths) if dtypes is None else list(dtypes)
    if copy16:
        widths, dtypes = widths + widths[:1], dtypes + [BF16]
    outs = pl.pallas_call(
        body, grid=(s // t,),
        in_specs=([_whole(p.shape) for p in params] + row_specs + dout_specs
                  + ([_rowspec(t, res.shape[1])] if n_res else [])),
        out_specs=[_whole(p.shape) for p in params] + [_rowspec(t, w) for w in widths],
        out_shape=([jax.ShapeDtypeStruct(p.shape, F32) for p in params]
                   + [jax.ShapeDtypeStruct((s, w), d) for w, d in zip(widths, dtypes)]),
        name=name, compiler_params=_params(("arbitrary",)))(*params, *rows, *douts, *([res] if n_res else []))
    return list(outs[:n_p]), list(outs[n_p:])


def _rmsnorm_fn(row0, g, x):
    y = x * lax.rsqrt(jnp.mean(x * x, axis=-1, keepdims=True) + EPS)
    return (y * g,)


def _hgrn_pre_fn(layer, row0, logits, aq, af):
    m = jnp.max(logits, axis=0, keepdims=True)
    e = jnp.exp(logits - m)
    p = e / jnp.sum(e, axis=0, keepdims=True)
    if layer == 0:
        lb = jnp.zeros((1, A_W), F32)
    else:
        acc = p[1:2]
        for j in range(2, layer + 1):
            acc = acc + p[j:j + 1]
        lb = jnp.minimum(jnp.maximum(acc, 0.0), 1.0 - EPS)
    sig = jax.nn.sigmoid(af)
    f = lb + (1.0 - lb) * sig
    log_f = jnp.log(jnp.maximum(f, TINY))
    k = (1.0 - lb) * jax.nn.sigmoid(-af)
    return jax.nn.silu(aq), k, log_f


def _hgrn_post_fn(row0, g, o, ag):
    ms = _group_sum(o * o, _block_ones(A_W, A_HD)) * (1.0 / A_HD)
    return (o * lax.rsqrt(ms + EPS) * g * jax.nn.silu(ag),)


def _gdn_pre_fn(row0, a_log, dt_bias, qkvc, gate):
    act = jax.nn.silu(qkvc)
    q, k, v = act[:, :B_W], act[:, B_W:2 * B_W], act[:, 2 * B_W:]
    bo = _block_ones(B_W, B_HD)
    q = q * lax.rsqrt(_group_sum(q * q, bo) + EPS) * (B_HD ** -0.5)
    k = k * lax.rsqrt(_group_sum(k * k, bo) + EPS)
    beta = jax.nn.sigmoid(gate)
    log_alpha = -jnp.exp(a_log) * jax.nn.softplus(gate + dt_bias)
    t = gate.shape[0]
    spread = lambda z, lane0: jnp.concatenate(
        [jnp.broadcast_to(z[:, lane0 + h:lane0 + h + 1], (t, B_HD)) for h in range(B_NH)], axis=1)
    return q, k, v, spread(beta, 0), spread(log_alpha, B_NH)


def _gdn_post_fn(row0, g, o, bz):
    ms = _group_sum(o * o, _block_ones(B_W, B_HD)) * (1.0 / B_HD)
    return (o * lax.rsqrt(ms + EPS) * g * jax.nn.silu(bz),)


def _lru_pre_fn(row0, w_a, b_a, w_x, b_x, lam, xc):
    r = jax.nn.sigmoid(_mdot(xc, w_a, "nn", 2, 2, BOTH) + b_a)
    i = jax.nn.sigmoid(_mdot(xc, w_x, "nn", 2, 2, BOTH) + b_x)
    log_a = -RG_C * r * jax.nn.softplus(-lam)
    a = jnp.exp(log_a)
    one_minus = -jnp.tanh(log_a) * (a * a + 1.0)
    mult = jnp.sqrt(jnp.maximum(one_minus, EPS))
    pos = row0 + _iota(xc.shape, 0)
    mult = jnp.where(pos == 0, 1.0, mult)
    return a, mult * i * xc


def _lru_post_fn(row0, h, cy):
    return (jax.nn.gelu(cy, approximate=True) * h,)


def _lru_da_fn(row0, g, h_prev):
    return (g * h_prev,)


def _relu2(u):
    r = jnp.maximum(u, 0.0)
    return r * r


def _drelu2(acc, u):
    return acc * (2.0 * jnp.maximum(u, 0.0))


MM_VMEM_BUDGET = 44 * 1024 * 1024


def _mm_tiles(mode, m, n, k, a_bytes, b_bytes, n_extra):
    best = None
    for tk in {k, 4096, 2048, 1024, 512, 256}:
        for tn in {n, 1280, 1024, 768, 512, 256, 128}:
            for tm in {m, 1024, 512, 256, 128}:
                if m % tm or n % tn or k % tk or tm > 1024:
                    continue
                tiles = tm * tk * a_bytes + tk * tn * b_bytes + (1 + n_extra) * tm * tn * 4
                if 2 * tiles + (tm * tn * 4 if tk < k else 0) > MM_VMEM_BUDGET:
                    continue
                a_reads = 1 if tk == k else n // tn
                b_reads = 1 if (tk == k and tn == n) else m // tm
                cost = m * k * a_bytes * a_reads + k * n * b_bytes * b_reads + m * n * 4 * (k // tk - 1)
                if best is None or (cost, -tm) < best[0]:
                    best = ((cost, -tm), (tm, tn, tk))
    if best is None:
        raise ValueError("no matmul tiling fits VMEM")
    return best[1]


def _matmul(name, a, b, mode, a_fn=None, res=None, epi=None, epi_in=None, out_dtype=F32, also=None):
    if mode == "nn":
        (m, k), n = a.shape, b.shape[1]
    elif mode == "nt":
        (m, k), n = a.shape, b.shape[0]
    else:
        (k, m), n = a.shape, b.shape[1]
    n_extra = (res is not None) + (epi_in is not None) + (also is not None)
    tm, tn, tk = _mm_tiles(mode, m, n, k, a.dtype.itemsize, b.dtype.itemsize, n_extra)
    nk = k // tk
    a_spec = {"nn": pl.BlockSpec((tm, tk), lambda i, j, kk: (i, kk)),
              "nt": pl.BlockSpec((tm, tk), lambda i, j, kk: (i, kk)),
              "tn": pl.BlockSpec((tk, tm), lambda i, j, kk: (kk, i))}[mode]
    b_spec = {"nn": pl.BlockSpec((tk, tn), lambda i, j, kk: (kk, j)),
              "nt": pl.BlockSpec((tn, tk), lambda i, j, kk: (j, kk)),
              "tn": pl.BlockSpec((tk, tn), lambda i, j, kk: (kk, j))}[mode]
    o_spec = pl.BlockSpec((tm, tn), lambda i, j, kk: (i, j))
    dims = {"nn": NN, "nt": NT, "tn": TN}[mode]
    extra = [x for x in (res, epi_in) if x is not None]

    def body(*refs):
        a_ref, b_ref = refs[0], refs[1]
        rest = list(refs[2:])
        res_ref = rest.pop(0) if res is not None else None
        epi_ref = rest.pop(0) if epi_in is not None else None
        o_ref = rest.pop(0)
        also_ref = rest.pop(0) if also is not None else None
        acc = rest[0] if nk > 1 else None
        kk = pl.program_id(2)
        at = a_ref[...]
        if a_fn is not None:
            at = a_fn(at.astype(F32))
        part = lax.dot_general(at.astype(BF16), b_ref[...].astype(BF16), (dims, ((), ())),
                               preferred_element_type=F32)

        def finish(out):
            if res_ref is not None:
                out = out + res_ref[...]
            if epi is not None:
                out = epi(out, epi_ref[...])
            o_ref[...] = out.astype(o_ref.dtype)
            if also is not None:
                also_ref[...] = also(out).astype(also_ref.dtype)

        if nk == 1:
            finish(part)
        else:
            @pl.when(kk == 0)
            def _():
                acc[...] = part

            @pl.when(jnp.logical_and(kk > 0, kk < nk - 1))
            def _():
                acc[...] += part

            @pl.when(kk == nk - 1)
            def _():
                finish(acc[...] + part)

    out_shape = [jax.ShapeDtypeStruct((m, n), out_dtype)] + ([jax.ShapeDtypeStruct((m, n), BF16)] if also else [])
    outs = pl.pallas_call(
        body, grid=(m // tm, n // tn, nk),
        in_specs=[a_spec, b_spec] + [o_spec] * len(extra), out_specs=[o_spec] * len(out_shape), out_shape=out_shape,
        scratch_shapes=[pltpu.VMEM((tm, tn), F32)] if nk > 1 else [],
        name=name, compiler_params=_params(("arbitrary", "arbitrary", "arbitrary")))(a, b, *extra)
    return outs if also else outs[0]


HALO = 8


def _conv_fwd(name, x, col0, width, w, bias, t=512):
    s = x.shape[0]
    cb = col0 // width
    hb = t // HALO

    def body(x_ref, halo_ref, w_ref, b_ref, y_ref):
        i = pl.program_id(0)
        halo = jnp.where(i == 0, 0.0, halo_ref[...])
        xp = jnp.concatenate([halo, x_ref[...]], axis=0)
        wv = w_ref[...]
        y = b_ref[...] + wv[0:1] * xp[HALO - 3:HALO - 3 + t]
        for j in range(1, 4):
            y = y + wv[j:j + 1] * xp[HALO - 3 + j:HALO - 3 + j + t]
        y_ref[...] = y

    return pl.pallas_call(
        body, grid=(s // t,),
        in_specs=[pl.BlockSpec((t, width), lambda i: (i, cb)),
                  pl.BlockSpec((HALO, width), lambda i: (jnp.maximum(i * hb - 1, 0), cb)),
                  _whole(w.shape), _whole(bias.shape)],
        out_specs=_rowspec(t, width), out_shape=jax.ShapeDtypeStruct((s, width), F32),
        name=name, compiler_params=_params(("arbitrary",)))(x, x, w, bias)


def _conv_bwd(name, dy, x, col0, width, w, t=512):
    s = x.shape[0]
    cb = col0 // width
    hb = t // HALO
    nblk = s // t

    def body(dy_ref, dyn_ref, x_ref, halo_ref, w_ref, dx_ref, dw_ref, db_ref):
        i = pl.program_id(0)
        dyv = dy_ref[...]
        nxt = jnp.where(i == nblk - 1, 0.0, dyn_ref[...])
        dyp = jnp.concatenate([dyv, nxt], axis=0)
        wv = w_ref[...]
        dx = wv[3:4] * dyv
        for j in range(3):
            dx = dx + wv[j:j + 1] * dyp[3 - j:3 - j + t]
        dx_ref[...] = dx.astype(dx_ref.dtype)
        halo = jnp.where(i == 0, 0.0, halo_ref[...])
        xp = jnp.concatenate([halo, x_ref[...]], axis=0)
        dw = jnp.concatenate(
            [jnp.sum(dyv * xp[HALO - 3 + j:HALO - 3 + j + t], axis=0, keepdims=True) for j in range(4)], axis=0)

        @pl.when(i == 0)
        def _():
            dw_ref[...] = jnp.zeros_like(dw_ref)
            db_ref[...] = jnp.zeros_like(db_ref)

        dw_ref[...] += dw
        db_ref[...] += jnp.sum(dyv, axis=0, keepdims=True)

    return pl.pallas_call(
        body, grid=(nblk,),
        in_specs=[_rowspec(t, width),
                  pl.BlockSpec((HALO, width), lambda i: (jnp.minimum((i + 1) * hb, s // HALO - 1), 0)),
                  pl.BlockSpec((t, width), lambda i: (i, cb)),
                  pl.BlockSpec((HALO, width), lambda i: (jnp.maximum(i * hb - 1, 0), cb)),
                  _whole(w.shape)],
        out_specs=[_rowspec(t, width), _whole((4, width)), _whole((1, width))],
        out_shape=[jax.ShapeDtypeStruct((s, width), BF16), jax.ShapeDtypeStruct((4, width), F32),
                   jax.ShapeDtypeStruct((1, width), F32)],
        name=name, compiler_params=_params(("arbitrary",)))(dy, dy, x, x, w)


def _hgrn_chunk(st, q, k, v, lf):
    cum = _running_sum(lf)
    tot = jnp.sum(lf, axis=0, keepdims=True)
    bm = _iota((A_W, A_W), 0) // A_HD == _iota((A_W, A_W), 1) // A_HD
    bo = bm.astype(F32)
    o_inter = _bdot(q * jnp.exp(cum), st, "nt")
    kd = k * jnp.exp(tot - cum)
    st_new = st * jnp.exp(tot) + jnp.where(bm, _bdot(v, kd, "tn"), 0.0)
    lane_h = _iota((SUB, A_W), 1) // A_HD
    n_h = A_W // A_HD
    t3 = _iota((SUB, SUB, A_W), 0)
    s3 = _iota((SUB, SUB, A_W), 1)
    outs = []
    for i in range(CHUNK // SUB):
        sl = slice(i * SUB, (i + 1) * SUB)
        qi, ki, vi, ci = q[sl], k[sl], v[sl], cum[sl]
        e = jnp.exp(jnp.where(s3 <= t3, ci[:, None, :] - ci[None, :, :], MASKED_EXPONENT))
        d3 = qi[:, None, :] * ki[None, :, :] * e
        ds = _mdot(d3.reshape(SUB * SUB, A_W), bo, "nn", 1, 1, (True, False)).reshape(SUB, SUB, A_W)
        oi = jnp.sum(ds * vi[None, :, :], axis=1)
        if i > 0:
            n = i * SUB
            cb = cum[n - 1:n]
            qt = qi * jnp.exp(ci - cb)
            kt = k[:n] * jnp.exp(cb - cum[:n])
            qs = jnp.concatenate([jnp.where(lane_h == h, qt, 0.0) for h in range(n_h)], axis=0)
            p = _bdot(_bdot(qs, kt, "nt"), v[:n])
            for h in range(n_h):
                oi = oi + jnp.where(lane_h == h, p[h * SUB:(h + 1) * SUB], 0.0)
        outs.append(oi)
    return st_new, o_inter + jnp.concatenate(outs, axis=0)


HGRN_CHUNKS = 2


def _hgrn_fwd(name, q, k, v, lf):
    s = q.shape[0]
    n = s // CHUNK
    g = HGRN_CHUNKS

    def body(q_ref, k_ref, v_ref, lf_ref, o_ref, sts_ref, st_ref):
        @pl.when(pl.program_id(0) == 0)
        def _():
            st_ref[...] = jnp.zeros_like(st_ref)

        st = st_ref[...]
        for c in range(g):
            rows = pl.ds(c * CHUNK, CHUNK)
            sts_ref[c] = st
            st, o = _hgrn_chunk(st, q_ref[rows], k_ref[rows], v_ref[rows], lf_ref[rows])
            o_ref[rows] = o
        st_ref[...] = st

    cs = _rowspec(g * CHUNK, A_W)
    return pl.pallas_call(
        body, grid=(n // g,), in_specs=[cs] * 4,
        out_specs=[cs, pl.BlockSpec((g, A_W, A_W), lambda i: (i, 0, 0))],
        out_shape=[jax.ShapeDtypeStruct((s, A_W), F32), jax.ShapeDtypeStruct((n, A_W, A_W), F32)],
        scratch_shapes=[pltpu.VMEM((A_W, A_W), F32)],
        name=name, compiler_params=_params(("arbitrary",)))(q, k, v, lf)


def _hgrn_bwd(name, q, k, v, lf, sts, do):
    s = q.shape[0]
    n = s // CHUNK
    g = HGRN_CHUNKS
    nsteps = n // g

    def body(q_ref, k_ref, v_ref, lf_ref, sts_ref, do_ref, dq_ref, dk_ref, dv_ref, dlf_ref, dst_ref):
        @pl.when(pl.program_id(0) == 0)
        def _():
            dst_ref[...] = jnp.zeros_like(dst_ref)

        dst = dst_ref[...]
        for c in reversed(range(g)):
            rows = pl.ds(c * CHUNK, CHUNK)
            _, vjp = jax.vjp(_hgrn_chunk, sts_ref[c], q_ref[rows], k_ref[rows], v_ref[rows], lf_ref[rows])
            ct = vjp((dst, do_ref[rows]))
            dst = ct[0]
            for r, gi in zip((dq_ref, dk_ref, dv_ref, dlf_ref), ct[1:]):
                r[rows] = gi
        dst_ref[...] = dst

    rs = pl.BlockSpec((g * CHUNK, A_W), lambda i: (nsteps - 1 - i, 0))
    return pl.pallas_call(
        body, grid=(nsteps,),
        in_specs=[rs] * 4 + [pl.BlockSpec((g, A_W, A_W), lambda i: (nsteps - 1 - i, 0, 0)), rs],
        out_specs=[rs] * 4, out_shape=[jax.ShapeDtypeStruct((s, A_W), F32)] * 4,
        scratch_shapes=[pltpu.VMEM((A_W, A_W), F32)],
        name=name, compiler_params=_params(("arbitrary",)))(q, k, v, lf, sts, do)


PREP_CHUNKS = 4
REC_CHUNKS = 4
DEC_ROWS = 8


def _to_heads(x, rows=CHUNK, width=B_HD):
    n = x.shape[0] // rows
    x3 = x.reshape(n, rows, x.shape[1])
    return jnp.concatenate([x3[:, :, h * B_HD:h * B_HD + width] for h in range(B_NH)], axis=0)


def _from_heads(y):
    n = y.shape[0] // B_NH
    x3 = jnp.concatenate([y[h * n:(h + 1) * n] for h in range(B_NH)], axis=2)
    return x3.reshape(n * y.shape[1], B_NH * B_HD)


@jax.custom_vjp
def _unit_lower_inverse(a_mat):
    nil = -a_mat
    eye = (_iota(a_mat.shape, 1) == _iota(a_mat.shape, 2)).astype(F32)
    inv = eye + nil
    sq = 2
    while sq < CHUNK:
        nil = _split_dot(nil, nil, "nn", 2, 2)
        inv = inv + _split_dot(inv, nil, "nn", 2, 2)
        sq *= 2
    return inv


def _unit_lower_inverse_fwd(a_mat):
    inv = _unit_lower_inverse(a_mat)
    return inv, inv


def _unit_lower_inverse_bwd(inv, ct):
    return (-_split_dot(_split_dot(inv, ct, "tn", 2, 2), inv, "nt", 2, 2),)


_unit_lower_inverse.defvjp(_unit_lower_inverse_fwd, _unit_lower_inverse_bwd)


@jax.custom_vjp
def _known_inverse(a_mat, inv):
    return inv


def _known_inverse_fwd(a_mat, inv):
    return inv, inv


def _known_inverse_bwd(inv, ct):
    return _unit_lower_inverse_bwd(inv, ct)[0], jnp.zeros_like(inv)


_known_inverse.defvjp(_known_inverse_fwd, _known_inverse_bwd)


def _gdn_prep(q, k, v, beta, la, inv_saved=None):
    q, k, v, beta, la = [_to_heads(z) for z in (q, k, v, beta, la)]
    nb = q.shape[0]
    r = _iota((nb, CHUNK, CHUNK), 1)
    c = _iota((nb, CHUNK, CHUNK), 2)
    causal = c <= r
    g = _running_sum(la)
    tot = jnp.sum(la, axis=1, keepdims=True)
    gcol = g[:, :, :CHUNK]
    decay = jnp.exp(jnp.where(causal, gcol - jnp.swapaxes(gcol, 1, 2), MASKED_EXPONENT))
    kb = k * beta
    a_mat = jnp.where(c < r, _bdot(kb, k, "nt") * decay, 0.0)
    if inv_saved is None:
        inv = _unit_lower_inverse(a_mat)
    else:
        inv = _known_inverse(a_mat, _to_heads(inv_saved, width=CHUNK))
    eg = jnp.exp(g)
    u = _bdot(inv, v * beta)
    w = _bdot(inv, kb * eg)
    qk = jnp.where(causal, _bdot(q, k, "nt") * decay, 0.0)
    widen = lambda z: jnp.concatenate([z, jnp.zeros((nb, CHUNK, B_HD - CHUNK), F32)], axis=2)
    dec = jnp.broadcast_to(jnp.exp(tot), (nb, DEC_ROWS, B_HD))
    return tuple(_from_heads(z) for z in (u, w, widen(qk), q * eg, k * jnp.exp(tot - g), dec, widen(inv)))


def _gdn_rec(st, u, w, qk, qd, kd, dec):
    u, w, qd, kd = [_to_heads(z) for z in (u, w, qd, kd)]
    qk = _to_heads(qk, width=CHUNK)
    v_new = u - _bdot(w, st)
    o = _bdot(qd, st) + _bdot(qk, v_new)
    st_new = st * _to_heads(dec, rows=DEC_ROWS)[:, :1, :1] + _bdot(kd, v_new, "tn")
    return st_new, _from_heads(o)


def _gdn_prep_fwd(name, q, k, v, beta, la):
    s = q.shape[0]
    t = PREP_CHUNKS * CHUNK
    td = PREP_CHUNKS * DEC_ROWS

    def body(*refs):
        outs = _gdn_prep(*[r[...] for r in refs[:5]])
        for o_ref, o in zip(refs[5:], outs):
            o_ref[...] = o

    rows, dec_rows = jax.ShapeDtypeStruct((s, B_W), F32), jax.ShapeDtypeStruct((s // CHUNK * DEC_ROWS, B_W), F32)
    return pl.pallas_call(
        body, grid=(s // t,), in_specs=[_rowspec(t, B_W)] * 5,
        out_specs=[_rowspec(t, B_W)] * 5 + [_rowspec(td, B_W), _rowspec(t, B_W)],
        out_shape=[rows] * 5 + [dec_rows, rows],
        name=name, compiler_params=_params(("arbitrary",)))(q, k, v, beta, la)


def _gdn_prep_bwd(name, q, k, v, beta, la, inv, cts):
    s = q.shape[0]
    t = PREP_CHUNKS * CHUNK
    td = PREP_CHUNKS * DEC_ROWS

    def body(*refs):
        inv_saved = refs[5][...]
        _, vjp = jax.vjp(lambda *a: _gdn_prep(*a, inv_saved=inv_saved)[:6], *[r[...] for r in refs[:5]])
        g = vjp(tuple(r[...] for r in refs[6:12]))
        for o_ref, o in zip(refs[12:], g):
            o_ref[...] = o

    return pl.pallas_call(
        body, grid=(s // t,), in_specs=[_rowspec(t, B_W)] * 11 + [_rowspec(td, B_W)],
        out_specs=[_rowspec(t, B_W)] * 5, out_shape=[jax.ShapeDtypeStruct((s, B_W), F32)] * 5,
        name=name, compiler_params=_params(("arbitrary",)))(q, k, v, beta, la, inv, *cts)


def _gdn_rec_fwd(name, u, w, qk, qd, kd, dec):
    s = u.shape[0]
    n = s // CHUNK
    g = REC_CHUNKS

    def body(u_ref, w_ref, qk_ref, qd_ref, kd_ref, dec_ref, o_ref, sts_ref, st_ref):
        @pl.when(pl.program_id(0) == 0)
        def _():
            st_ref[...] = jnp.zeros_like(st_ref)

        st = st_ref[...]
        for c in range(g):
            rows, drows = pl.ds(c * CHUNK, CHUNK), pl.ds(c * DEC_ROWS, DEC_ROWS)
            sts_ref[c] = st
            st, o = _gdn_rec(st, u_ref[rows], w_ref[rows], qk_ref[rows], qd_ref[rows], kd_ref[rows], dec_ref[drows])
            o_ref[rows] = o
        st_ref[...] = st

    cs = _rowspec(g * CHUNK, B_W)
    return pl.pallas_call(
        body, grid=(n // g,), in_specs=[cs] * 5 + [_rowspec(g * DEC_ROWS, B_W)],
        out_specs=[cs, pl.BlockSpec((g, B_NH, B_HD, B_HD), lambda i: (i, 0, 0, 0))],
        out_shape=[jax.ShapeDtypeStruct((s, B_W), F32), jax.ShapeDtypeStruct((n, B_NH, B_HD, B_HD), F32)],
        scratch_shapes=[pltpu.VMEM((B_NH, B_HD, B_HD), F32)],
        name=name, compiler_params=_params(("arbitrary",)))(u, w, qk, qd, kd, dec)


def _gdn_rec_bwd(name, u, w, qk, qd, kd, dec, sts, do):
    s = u.shape[0]
    n = s // CHUNK
    g = REC_CHUNKS
    nsteps = n // g

    def body(u_ref, w_ref, qk_ref, qd_ref, kd_ref, dec_ref, sts_ref, do_ref,
             du_ref, dw_ref, dqk_ref, dqd_ref, dkd_ref, ddec_ref, dst_ref):
        @pl.when(pl.program_id(0) == 0)
        def _():
            dst_ref[...] = jnp.zeros_like(dst_ref)

        dst = dst_ref[...]
        for c in reversed(range(g)):
            rows, drows = pl.ds(c * CHUNK, CHUNK), pl.ds(c * DEC_ROWS, DEC_ROWS)
            _, vjp = jax.vjp(_gdn_rec, sts_ref[c], u_ref[rows], w_ref[rows], qk_ref[rows], qd_ref[rows], kd_ref[rows],
                             dec_ref[drows])
            ct = vjp((dst, do_ref[rows]))
            dst = ct[0]
            for r, gi in zip((du_ref, dw_ref, dqk_ref, dqd_ref, dkd_ref), ct[1:6]):
                r[rows] = gi
            ddec_ref[drows] = ct[6]
        dst_ref[...] = dst

    rs = pl.BlockSpec((g * CHUNK, B_W), lambda i: (nsteps - 1 - i, 0))
    ds = pl.BlockSpec((g * DEC_ROWS, B_W), lambda i: (nsteps - 1 - i, 0))
    return pl.pallas_call(
        body, grid=(nsteps,),
        in_specs=[rs] * 5 + [ds, pl.BlockSpec((g, B_NH, B_HD, B_HD), lambda i: (nsteps - 1 - i, 0, 0, 0)), rs],
        out_specs=[rs] * 5 + [ds],
        out_shape=[jax.ShapeDtypeStruct((s, B_W), F32)] * 5 + [jax.ShapeDtypeStruct((n * DEC_ROWS, B_W), F32)],
        scratch_shapes=[pltpu.VMEM((B_NH, B_HD, B_HD), F32)],
        name=name, compiler_params=_params(("arbitrary",)))(u, w, qk, qd, kd, dec, sts, do)


SCAN_T = 512


def _block_scan(a, b, reverse):
    t = a.shape[0]
    rows = _iota(a.shape, 0)
    sft = 1
    while sft < t:
        if reverse:
            a_s, b_s = pltpu.roll(a, t - sft, 0), pltpu.roll(b, t - sft, 0)
            edge = rows >= t - sft
        else:
            a_s, b_s = pltpu.roll(a, sft, 0), pltpu.roll(b, sft, 0)
            edge = rows < sft
        a_s = jnp.where(edge, 1.0, a_s)
        b_s = jnp.where(edge, 0.0, b_s)
        b = a * b_s + b
        a = a * a_s
        sft *= 2
    return a, b


def _lru_scan_fwd(name, a, b):
    s, w = a.shape
    t = SCAN_T

    def body(a_ref, b_ref, h_ref, hp_ref, carry):
        @pl.when(pl.program_id(0) == 0)
        def _():
            carry[...] = jnp.zeros_like(carry)

        h_in = carry[0:1]
        ca, cb = _block_scan(a_ref[...], b_ref[...], False)
        h = ca * h_in + cb
        h_ref[...] = h
        hp_ref[...] = jnp.where(_iota(h.shape, 0) == 0, h_in, pltpu.roll(h, 1, 0))
        carry[...] = jnp.broadcast_to(h[t - 1:t], carry.shape)

    return pl.pallas_call(
        body, grid=(s // t,), in_specs=[_rowspec(t, w)] * 2, out_specs=[_rowspec(t, w)] * 2,
        out_shape=[jax.ShapeDtypeStruct((s, w), F32)] * 2, scratch_shapes=[pltpu.VMEM((8, w), F32)],
        name=name, compiler_params=_params(("arbitrary",)))(a, b)


def _lru_scan_bwd(name, a, dh):
    s, w = a.shape
    t = SCAN_T
    n = s // t

    def body(a_ref, dh_ref, g_ref, a_next, g_next):
        @pl.when(pl.program_id(0) == 0)
        def _():
            a_next[...] = jnp.zeros_like(a_next)
            g_next[...] = jnp.zeros_like(g_next)

        av = a_ref[...]
        a_up = jnp.where(_iota(av.shape, 0) == t - 1, a_next[0:1], pltpu.roll(av, t - 1, 0))
        ca, cb = _block_scan(a_up, dh_ref[...], True)
        g = ca * g_next[0:1] + cb
        g_ref[...] = g
        a_next[...] = jnp.broadcast_to(av[0:1], a_next.shape)
        g_next[...] = jnp.broadcast_to(g[0:1], g_next.shape)

    rs = pl.BlockSpec((t, w), lambda i: (n - 1 - i, 0))
    return pl.pallas_call(
        body, grid=(n,), in_specs=[rs] * 2, out_specs=rs, out_shape=jax.ShapeDtypeStruct((s, w), F32),
        scratch_shapes=[pltpu.VMEM((8, w), F32), pltpu.VMEM((8, w), F32)],
        name=name, compiler_params=_params(("arbitrary",)))(a, dh)


def _loss_fn(g, x, tgt):
    y = x * lax.rsqrt(jnp.mean(x * x, axis=-1, keepdims=True) + EPS) * g
    err = y - tgt
    return 0.5 * jnp.sum(jnp.mean(err * err, axis=-1, keepdims=True), axis=0, keepdims=True)


def _loss_and_grad(name, g, x, tgt, t=256):
    s, d = x.shape

    def body(g_ref, x_ref, t_ref, loss_ref, dx_ref, dg_ref, dx16_ref):
        val, vjp = jax.vjp(_loss_fn, g_ref[...], x_ref[...], t_ref[...])
        dg, dx, _ = vjp(jnp.ones((1, 1), F32))

        @pl.when(pl.program_id(0) == 0)
        def _():
            loss_ref[...] = jnp.zeros_like(loss_ref)
            dg_ref[...] = jnp.zeros_like(dg_ref)

        loss_ref[...] += jnp.broadcast_to(val, loss_ref.shape)
        dg_ref[...] += dg
        dx_ref[...] = dx
        dx16_ref[...] = dx.astype(BF16)

    return pl.pallas_call(
        body, grid=(s // t,), in_specs=[_whole(g.shape), _rowspec(t, d), _rowspec(t, d)],
        out_specs=[_whole((1, 128)), _rowspec(t, d), _whole(g.shape), _rowspec(t, d)],
        out_shape=[jax.ShapeDtypeStruct((1, 128), F32), jax.ShapeDtypeStruct((s, d), F32),
                   jax.ShapeDtypeStruct(g.shape, F32), jax.ShapeDtypeStruct((s, d), BF16)],
        name=name, compiler_params=_params(("arbitrary",)))(g, x, tgt)


def _ew(name, fn, ins, n_out, t=None):
    r, c = ins[0].shape
    t = _tile(r, (512, 256, 128, 64, 32, 16, 8)) if t is None else t

    def body(*refs):
        outs = fn(*[x[...] for x in refs[:len(ins)]])
        for o_ref, o in zip(refs[len(ins):], outs):
            o_ref[...] = o

    return pl.pallas_call(
        body, grid=(r // t,), in_specs=[_rowspec(t, c)] * len(ins), out_specs=[_rowspec(t, c)] * n_out,
        out_shape=[jax.ShapeDtypeStruct((r, c), F32)] * n_out,
        name=name, compiler_params=_params(("arbitrary",)))(*ins)


def _adam_fn(w, g, m, v):
    m = B1 * m + (1.0 - B1) * g
    v = B2 * v + (1.0 - B2) * (g * g)
    m_hat = m / (1.0 - B1 ** STEP)
    v_hat = v / (1.0 - B2 ** STEP)
    delta = -LR * (m_hat / (jnp.sqrt(v_hat) + AEPS) + WD * w)
    return delta, m, v


def _add2_fn(a, b):
    return (a + b,)


def _adam(name, w, g, m, v):
    shp = w.shape
    two = lambda z: z.reshape(-1, shp[-1])
    outs = _ew(name, _adam_fn, [two(w), two(g), two(m), two(v)], 3)
    return [o.reshape(shp) for o in outs]


def _swap(name, bufs, flips):
    n = len(bufs)
    hbm = pl.BlockSpec(memory_space=pltpu.HBM)

    def body(*refs):
        srcs, dsts = refs[:n], refs[n:2 * n]
        send_sems, recv_sems = refs[2 * n], refs[2 * n + 1]
        me = (lax.axis_index("x"), lax.axis_index("y"), lax.axis_index("c"))
        copies = []
        for i in range(n):
            peer = tuple(1 - m if f else m for m, f in zip(me, flips[i]))
            cp = pltpu.make_async_remote_copy(src_ref=srcs[i], dst_ref=dsts[i], send_sem=send_sems.at[i],
                                              recv_sem=recv_sems.at[i], device_id=peer, device_id_type=MESH)
            cp.start()
            copies.append(cp)
        for cp in copies:
            cp.wait()

    return pl.pallas_call(
        body, in_specs=[hbm] * n, out_specs=[hbm] * n,
        out_shape=[jax.ShapeDtypeStruct(b.shape, b.dtype) for b in bufs],
        scratch_shapes=[pltpu.SemaphoreType.DMA((n,)), pltpu.SemaphoreType.DMA((n,))],
        name=name, compiler_params=pltpu.CompilerParams(has_side_effects=True))(*bufs)


FLIP_C = (False, False, True)
FLIP_Y = (False, True, False)
FLIP_X = (True, False, False)


LAYER_ROWS = PACK_ROWS // DEPTH
HALF_ROWS = LAYER_ROWS // 2
SUM_ROWS = 128
GATHER_IDS = (1, 2, 3, 4)
SCATTER_IDS = (5, 6, 7, 8)
FIRST_GATHER_ID = 9


def _mesh_pos():
    ix, iy, core = lax.axis_index("x"), lax.axis_index("y"), lax.axis_index("c")
    others = [(ix, 1 - iy), (1 - ix, iy), (1 - ix, 1 - iy)]
    return ix, iy, core, others


def _pack_big(w_in, w_out, w_up, w_down):
    pad = jnp.pad(w_in, ((0, 0), (0, 0), (0, 1024 - w_in.shape[-1])))
    return jnp.concatenate([pad, w_up, w_down, w_out], axis=1)


def _unpack_big(p):
    w_in = p[..., 0:1024, :D_IN // 4]
    return w_in, p[..., 3072:LAYER_ROWS, :], p[..., 1024:2048, :], p[..., 2048:3072, :]


def _gather_layer(packed, collective_id):
    src = jax.new_ref(packed, memory_space=pltpu.MemorySpace.HBM)
    out = jax.empty_ref(jax.ShapeDtypeStruct((4,) + packed.shape, packed.dtype), memory_space=pltpu.MemorySpace.HBM)
    dma = pltpu.SemaphoreType.DMA

    @pl.kernel(mesh=plsc.ScalarSubcoreMesh(axis_name="seq", num_cores=1), name="gather_%d" % collective_id,
               scratch_types=(dma,) * 12, compiler_params=pltpu.CompilerParams(collective_id=collective_id))
    def launch(*sems):
        send_sems, recv_sems = sems[:6], sems[6:]
        ix, iy, core, others = _mesh_pos()
        barrier = pltpu.get_barrier_semaphore()
        for peer in [(ox, oy, core) for ox, oy in others] + [(ix, iy, 1 - core)]:
            pl.semaphore_signal(barrier, inc=1, device_id=peer, device_id_type=MESH)
        pl.semaphore_wait(barrier, 4)
        half = packed.shape[0] // 2
        mine = pl.ds(core * half, half)
        theirs = pl.ds((1 - core) * half, half)

        def copy(k, src_ref, owner, rows, to):
            return pltpu.make_async_remote_copy(
                src_ref=src_ref, dst_ref=out.at[owner, rows], send_sem=send_sems[k], recv_sem=recv_sems[k],
                device_id=to, device_id_type=MESH)

        chip = 2 * ix + iy
        first = [copy(j, src.at[mine], chip, mine, (*others[j], core)) for j in range(3)]
        for cp in first:
            cp.start()
        passed = []
        for j, (ox, oy) in enumerate(others):
            owner = 2 * ox + oy
            copy(j, src.at[mine], owner, mine, (ix, iy, core)).wait_recv()
            fwd = copy(3 + j, out.at[owner, mine], owner, mine, (ix, iy, 1 - core))
            fwd.start()
            passed.append(fwd)
        for j, (ox, oy) in enumerate(others):
            copy(3 + j, src.at[theirs], 2 * ox + oy, theirs, (ix, iy, core)).wait_recv()
        for cp in first + passed:
            cp.wait_send()

    launch()
    return out[...]


def _rs_sibling(per_chip):
    hbm = pl.BlockSpec(memory_space=pltpu.HBM)

    def body(src, out, send_sem, recv_sem):
        ix, iy, core, _ = _mesh_pos()
        give = pl.ds((1 - core) * HALF_ROWS, HALF_ROWS)
        cp = pltpu.make_async_remote_copy(src_ref=src.at[pl.ds(0, 4), give], dst_ref=out, send_sem=send_sem,
                                          recv_sem=recv_sem, device_id=(ix, iy, 1 - core), device_id_type=MESH)
        cp.start()
        cp.wait()

    return pl.pallas_call(
        body, in_specs=[hbm], out_specs=hbm, out_shape=jax.ShapeDtypeStruct((4, HALF_ROWS, 1024), F32),
        scratch_shapes=[pltpu.SemaphoreType.DMA, pltpu.SemaphoreType.DMA],
        name="rs_sibling", compiler_params=pltpu.CompilerParams(has_side_effects=True))(per_chip)


def _rs_add_pair(per_chip, sib, chip, core):
    nb = HALF_ROWS // SUM_ROWS
    blk = (1, SUM_ROWS, 1024)

    def body(chip_ref, core_ref, a_ref, b_ref, own_ref, send_ref):
        total = a_ref[0] + b_ref[0]
        send_ref[0] = total.astype(BF16)

        @pl.when(pl.program_id(1) == chip_ref[0])
        def _():
            own_ref[...] = total

    return pl.pallas_call(
        body,
        grid_spec=pltpu.PrefetchScalarGridSpec(
            num_scalar_prefetch=2, grid=(nb, 4),
            in_specs=[pl.BlockSpec(blk, lambda i, k, ch, co: (k, co[0] * nb + i, 0)),
                      pl.BlockSpec(blk, lambda i, k, ch, co: (k, i, 0))],
            out_specs=[pl.BlockSpec((SUM_ROWS, 1024), lambda i, k, ch, co: (i, 0)),
                       pl.BlockSpec(blk, lambda i, k, ch, co: (k, i, 0))]),
        out_shape=[jax.ShapeDtypeStruct((HALF_ROWS, 1024), F32), jax.ShapeDtypeStruct((4, HALF_ROWS, 1024), BF16)],
        name="rs_add_pair", compiler_params=_params(("arbitrary", "arbitrary")))(
            chip.reshape(1), core.reshape(1), per_chip, sib)


def _rs_chips(pair, collective_id):
    src_ref = jax.new_ref(pair, memory_space=pltpu.MemorySpace.HBM)
    got_ref = jax.empty_ref(jax.ShapeDtypeStruct((3,) + pair.shape[1:], pair.dtype), memory_space=pltpu.MemorySpace.HBM)
    dma = pltpu.SemaphoreType.DMA

    @pl.kernel(mesh=plsc.ScalarSubcoreMesh(axis_name="seq", num_cores=1), name="rs_chips_%d" % collective_id,
               scratch_types=(dma,) * 6, compiler_params=pltpu.CompilerParams(collective_id=collective_id))
    def launch(s0, s1, s2, r0, r1, r2):
        ix, iy, core, others = _mesh_pos()
        barrier = pltpu.get_barrier_semaphore()
        for ox, oy in others:
            pl.semaphore_signal(barrier, inc=1, device_id=(ox, oy, core), device_id_type=MESH)
        pl.semaphore_wait(barrier, 3)
        copies = []
        for j, ((ox, oy), send_sem, recv_sem) in enumerate(zip(others, (s0, s1, s2), (r0, r1, r2))):
            cp = pltpu.make_async_remote_copy(src_ref=src_ref.at[2 * ox + oy], dst_ref=got_ref.at[j], send_sem=send_sem,
                                              recv_sem=recv_sem, device_id=(ox, oy, core), device_id_type=MESH)
            cp.start()
            copies.append(cp)
        for cp in copies:
            cp.wait()

    launch()
    return got_ref[...]


def _rs_add_chips(own, got, core, layer, buf):
    nb = HALF_ROWS // SUM_ROWS
    blk = (1, SUM_ROWS, 1024)

    def body(core_ref, a_ref, g0_ref, g1_ref, g2_ref, *rest):
        o_ref = rest[-1]
        o_ref[0, 0] = ((a_ref[...] + g0_ref[0].astype(F32)) + g1_ref[0].astype(F32)) + g2_ref[0].astype(F32)

    aliased = [] if buf is None else [buf]
    return pl.pallas_call(
        body,
        grid_spec=pltpu.PrefetchScalarGridSpec(
            num_scalar_prefetch=1, grid=(nb,),
            in_specs=[pl.BlockSpec((SUM_ROWS, 1024), lambda i, c: (i, 0))]
            + [pl.BlockSpec(blk, functools.partial(lambda j, i, c: (j, i, 0), j)) for j in range(3)]
            + [pl.BlockSpec(memory_space=pl.ANY)] * len(aliased),
            out_specs=pl.BlockSpec((1, 1, SUM_ROWS, 1024), lambda i, c: (layer, c[0], i, 0))),
        out_shape=jax.ShapeDtypeStruct((DEPTH, 2, HALF_ROWS, 1024), F32),
        input_output_aliases={} if buf is None else {5: 0},
        name="rs_add_chips", compiler_params=_params(("arbitrary",)))(core.reshape(1), own, got, got, got, *aliased)


def _rs_share(buf):
    hbm = pl.BlockSpec(memory_space=pltpu.HBM)

    def body(src, out, send_sem, recv_sem):
        ix, iy, core, _ = _mesh_pos()
        layers = pl.ds(0, DEPTH)
        cp = pltpu.make_async_remote_copy(src_ref=src.at[layers, core], dst_ref=out.at[layers, core], send_sem=send_sem,
                                          recv_sem=recv_sem, device_id=(ix, iy, 1 - core), device_id_type=MESH)
        cp.start()
        pltpu.make_async_remote_copy(src_ref=src.at[layers, 1 - core], dst_ref=out.at[layers, 1 - core],
                                     send_sem=send_sem, recv_sem=recv_sem, device_id=(ix, iy, core),
                                     device_id_type=MESH).wait_recv()
        cp.wait_send()

    return pl.pallas_call(
        body, in_specs=[hbm], out_specs=hbm, out_shape=jax.ShapeDtypeStruct(buf.shape, buf.dtype),
        input_output_aliases={0: 0}, scratch_shapes=[pltpu.SemaphoreType.DMA, pltpu.SemaphoreType.DMA],
        name="rs_share", compiler_params=pltpu.CompilerParams(has_side_effects=True))(buf)


def _scatter_start(per_chip, chip, core, layer):
    own, pair = _rs_add_pair(per_chip, _rs_sibling(per_chip), chip, core)
    return own, _rs_chips(pair, SCATTER_IDS[layer])


def _allreduce_small(tag, buf):
    for name, flip in (("c", FLIP_C), ("y", FLIP_Y), ("x", FLIP_X)):
        (other,) = _swap(tag + "_swap_" + name, [buf], [flip])
        (buf,) = _ew(tag + "_add_" + name, _add2_fn, [buf, other], 1)
    return buf


def _permute_in_cols(w):
    z = jnp.zeros(w.shape[:-1] + (D_IN_PAD - D_IN,), w.dtype)
    return jnp.concatenate([w[..., 1024:3072], w[..., :1024], w[..., 3080:D_IN], w[..., 3072:3080], z], axis=-1)


IN_COL_SEGMENTS = ((0, 1024, 2048), (1024, 3072, -1024), (3072, 3080, GATE0 - 3072), (3080, D_IN, COL_CX - 3080))


def _orig_cols(w_pad, lo, hi):
    parts = []
    for first, last, shift in IN_COL_SEGMENTS:
        a, b = max(lo, first), min(hi, last)
        if a < b:
            parts.append(w_pad[..., a + shift:b + shift])
    return parts


def _block_diag(w):
    n, d = w.shape[0], w.shape[1]
    on_diag = jnp.arange(n)[:, None] == jnp.arange(n)[None, :]
    return jnp.where(on_diag[:, None, :, None], w[:, :, None, :], 0.0).reshape(n * d, n * d)


def _diag_blocks(w):
    return jnp.stack([w[64 * n:64 * (n + 1), 64 * n:64 * (n + 1)] for n in range(4)], axis=0)


def _gate_row(v):
    return jnp.concatenate([jnp.zeros((4,), F32), v, jnp.zeros((120,), F32)]).reshape(1, 128)


SMALL = ("norm1_g", "hgrn_lb_logits", "hgrn_norm_g", "gdn_conv_w", "gdn_a_log", "gdn_dt_bias", "gdn_norm_g",
         "lru_conv_w", "lru_conv_b", "lru_w_a", "lru_b_a", "lru_w_x", "lru_b_x", "lru_lambda", "norm2_g",
         "final_norm_g")


def _flatten_small(tree):
    flat = jnp.concatenate([tree[k].reshape(-1) for k in sorted(tree)])
    rows = -(-flat.shape[0] // (512 * 128)) * 512
    return jnp.pad(flat, (0, rows * 128 - flat.shape[0])).reshape(rows, 128)


def _unflatten_small(buf, shapes):
    flat = buf.reshape(-1)
    out, off = {}, 0
    for k in sorted(shapes):
        size = 1
        for d in shapes[k]:
            size *= d
        out[k] = flat[off:off + size].reshape(shapes[k])
        off += size
    return out


def _layer_fwd(l, x, p, w_in, late):
    sv = {"x0": x}
    (h,) = _rowwise_fwd("norm1", _rmsnorm_fn, [p["norm1_g"]], [x], [D_MODEL], [BF16], 512)
    proj, w_out, w_up, w_down = late(_matmul("proj_in", h, w_in, "nn"))
    sv["h"], sv["proj"] = h, proj
    aq, af, ag = _cols(proj, COL_AQ, A_W), _cols(proj, COL_AF, A_W), _cols(proj, COL_AG, A_W)
    ai = lax.slice_in_dim(proj, COL_AI, COL_AI + A_W, axis=1)
    bz, gate, cy = _cols(proj, COL_BZ, B_W), _cols(proj, GATE0, 128), _cols(proj, COL_CY, C_W)
    q, k, lf = _rowwise_fwd("hgrn_pre", functools.partial(_hgrn_pre_fn, l), [p["lb_logits"]], [aq, af],
                            [A_W] * 3, [F32] * 3, 512)
    o_a, sts_a = _hgrn_fwd("hgrn_chunk", q, k, ai, lf)
    (y_a,) = _rowwise_fwd("hgrn_post", _hgrn_post_fn, [p["hgrn_norm_g"]], [o_a, ag], [A_W], [BF16], 512)
    sv.update(aq=aq, af=af, ai=ai, ag=ag, hq=q, hk=k, hlf=lf, o_a=o_a, sts_a=sts_a)
    qkvc = _conv_fwd("gdn_conv", proj, 0, 3 * B_W, p["gdn_conv_w"], jnp.zeros((1, 3 * B_W), F32), 256)
    gq, gk, gv, gb, gla = _rowwise_fwd("gdn_pre", _gdn_pre_fn, [p["gdn_a_log"], p["gdn_dt_bias"]], [qkvc, gate],
                                       [B_W] * 5, [F32] * 5, 256)
    *wy, sv["gdn_inv"] = _gdn_prep_fwd("gdn_prep", gq, gk, gv, gb, gla)
    o_b, sts_b = _gdn_rec_fwd("gdn_rec", *wy)
    sv["wy"] = wy
    (y_b,) = _rowwise_fwd("gdn_post", _gdn_post_fn, [p["gdn_norm_g"]], [o_b, bz], [B_W], [BF16], 512)
    sv.update(qkvc=qkvc, gate=gate, bz=bz, gq=gq, gk=gk, gv=gv, gb=gb, gla=gla, o_b=o_b, sts_b=sts_b)
    xc = _conv_fwd("lru_conv", proj, COL_CX, C_W, p["lru_conv_w"], p["lru_conv_b"], 512)
    a, b = _rowwise_fwd("lru_pre", _lru_pre_fn, [p["lru_w_a"], p["lru_b_a"], p["lru_w_x"], p["lru_b_x"], p["lru_lambda"]],
                        [xc], [C_W] * 2, [F32] * 2, 512)
    hs, h_prev = _lru_scan_fwd("lru_scan", a, b)
    (y_c,) = _rowwise_fwd("lru_post", _lru_post_fn, [], [hs, cy], [C_W], [BF16], 512)
    sv.update(xc=xc, la=a, hs=hs, h_prev=h_prev, cy=cy)
    mixed = jnp.concatenate([y_b, y_a, y_c], axis=1)
    x1 = _matmul("proj_out", mixed, w_out, "nn", res=x)
    (h2,) = _rowwise_fwd("norm2", _rmsnorm_fn, [p["norm2_g"]], [x1], [D_MODEL], [BF16], 512)
    up, act = _matmul("mlp_up", h2, w_up, "nn", also=_relu2)
    x2 = _matmul("mlp_down", act, w_down, "nn", res=x1)
    sv.update(mixed=mixed, x1=x1, h2=h2, up=up, act=act)
    return x2, sv, (w_in, w_out, w_up, w_down)


def _layer_bwd(l, dx, dx16, p, sv, w_in, w_out, w_up, w_down):
    gs = {}
    dup = _matmul("mlp_down_dx", dx16, w_down, "nt", epi=_drelu2, epi_in=sv["up"], out_dtype=BF16)
    d_w_down = _matmul("mlp_down_dw", sv["act"], dx16, "tn")
    dh2 = _matmul("mlp_up_dx", dup, w_up, "nt")
    d_w_up = _matmul("mlp_up_dw", sv["h2"], dup, "tn")
    (gs["norm2_g"],), (dx1, dx1_16) = _rowwise_bwd("norm2_bwd", _rmsnorm_fn, [p["norm2_g"]], [sv["x1"]], [dh2], 512,
                                                   res=dx, copy16=True)
    dmixed = _matmul("proj_out_dx", dx1_16, w_out, "nt")
    d_w_out = _matmul("proj_out_dw", sv["mixed"], dx1_16, "tn")
    dy_b, dy_a, dy_c = _cols(dmixed, 0, B_W), _cols(dmixed, B_W, A_W), _cols(dmixed, B_W + A_W, C_W)
    (gs["hgrn_norm_g"],), (do_a, dag) = _rowwise_bwd("hgrn_post_bwd", _hgrn_post_fn, [p["hgrn_norm_g"]],
                                                     [sv["o_a"], sv["ag"]], [dy_a], 512, dtypes=[F32, BF16])
    dq, dk, dai, dlf = _hgrn_bwd("hgrn_chunk_bwd", sv["hq"], sv["hk"], sv["ai"], sv["hlf"], sv["sts_a"], do_a)
    (gs["lb_logits"],), (daq, daf) = _rowwise_bwd("hgrn_pre_bwd", functools.partial(_hgrn_pre_fn, l), [p["lb_logits"]],
                                                  [sv["aq"], sv["af"]], [dq, dk, dlf], 512, dtypes=[BF16, BF16])
    (gs["gdn_norm_g"],), (do_b, dbz) = _rowwise_bwd("gdn_post_bwd", _gdn_post_fn, [p["gdn_norm_g"]],
                                                    [sv["o_b"], sv["bz"]], [dy_b], 512, dtypes=[F32, BF16])
    d_wy = _gdn_rec_bwd("gdn_rec_bwd", *sv["wy"], sv["sts_b"], do_b)
    dgq, dgk, dgv, dgb, dgla = _gdn_prep_bwd("gdn_prep_bwd", sv["gq"], sv["gk"], sv["gv"], sv["gb"], sv["gla"],
                                             sv["gdn_inv"], d_wy)
    (gs["gdn_a_log"], gs["gdn_dt_bias"]), (dqkvc, dgate) = _rowwise_bwd(
        "gdn_pre_bwd", _gdn_pre_fn, [p["gdn_a_log"], p["gdn_dt_bias"]], [sv["qkvc"], sv["gate"]],
        [dgq, dgk, dgv, dgb, dgla], 256, dtypes=[F32, BF16])
    dqkv, gs["gdn_conv_w"], _ = _conv_bwd("gdn_conv_bwd", dqkvc, sv["proj"], 0, 3 * B_W, p["gdn_conv_w"], 256)
    _, (dhs, dcy) = _rowwise_bwd("lru_post_bwd", _lru_post_fn, [], [sv["hs"], sv["cy"]], [dy_c], 512, dtypes=[F32, BF16])
    g = _lru_scan_bwd("lru_scan_bwd", sv["la"], dhs)
    (da,) = _rowwise_fwd("lru_da", _lru_da_fn, [], [g, sv["h_prev"]], [C_W], [F32], 512)
    lru_params = [p["lru_w_a"], p["lru_b_a"], p["lru_w_x"], p["lru_b_x"], p["lru_lambda"]]
    dps, (dxc,) = _rowwise_bwd("lru_pre_bwd", _lru_pre_fn, lru_params, [sv["xc"]], [da, g], 512)
    gs["lru_w_a"], gs["lru_b_a"], gs["lru_w_x"], gs["lru_b_x"], gs["lru_lambda"] = dps
    dcx, gs["lru_conv_w"], gs["lru_conv_b"] = _conv_bwd("lru_conv_bwd", dxc, sv["proj"], COL_CX, C_W, p["lru_conv_w"], 512)
    s = dx.shape[0]
    dproj = jnp.concatenate([dqkv, dbz, daq, daf, dai.astype(BF16), dag, dcx, dcy, dgate,
                             jnp.zeros((s, D_IN_PAD - GATE0 - 128), BF16)], axis=1)
    dh = _matmul("proj_in_dx", dproj, w_in, "nt")
    d_w_in = _matmul("proj_in_dw", sv["h"], dproj, "tn")
    (gs["norm1_g"],), (dx0, dx0_16) = _rowwise_bwd("norm1_bwd", _rmsnorm_fn, [p["norm1_g"]], [sv["x0"]], [dh], 512,
                                                   res=dx1, copy16=True)
    return dx0, dx0_16, dict(w_in=d_w_in, w_out=d_w_out, w_up=d_w_up, w_down=d_w_down), gs


def _layer_params(l, lb_logits, hgrn_norm_g, gdn_conv_w, gdn_a_log, gdn_dt_bias, gdn_norm_g, lru_conv_w, lru_conv_b,
                  lru_w_a, lru_b_a, lru_w_x, lru_b_x, lru_lambda, norm1_g, norm2_g):
    row = lambda v: v.reshape(1, -1)
    return dict(
        norm1_g=row(norm1_g[l]), norm2_g=row(norm2_g[l]), lb_logits=lb_logits,
        hgrn_norm_g=row(jnp.tile(hgrn_norm_g[l], A_W // A_HD)),
        gdn_conv_w=gdn_conv_w[l], gdn_a_log=_gate_row(gdn_a_log[l]), gdn_dt_bias=_gate_row(gdn_dt_bias[l]),
        gdn_norm_g=row(jnp.tile(gdn_norm_g[l], B_NH)),
        lru_conv_w=lru_conv_w[l], lru_conv_b=row(lru_conv_b[l]), lru_w_a=_block_diag(lru_w_a[l]), lru_b_a=row(lru_b_a[l]),
        lru_w_x=_block_diag(lru_w_x[l]), lru_b_x=row(lru_b_x[l]), lru_lambda=row(lru_lambda[l]))


def _small_grads_to_reference(gs_layers, d_final_g, d_logits):
    st = lambda k, f=lambda z: z: jnp.stack([f(g[k]) for g in gs_layers], axis=0)
    vec = lambda z: z.reshape(-1)
    return dict(
        norm1_g=st("norm1_g", vec), norm2_g=st("norm2_g", vec), hgrn_lb_logits=d_logits,
        hgrn_norm_g=st("hgrn_norm_g", lambda z: z.reshape(A_W // A_HD, A_HD).sum(0)),
        gdn_conv_w=st("gdn_conv_w"), gdn_a_log=st("gdn_a_log", lambda z: z[0, 4:8]),
        gdn_dt_bias=st("gdn_dt_bias", lambda z: z[0, 4:8]),
        gdn_norm_g=st("gdn_norm_g", lambda z: z.reshape(B_NH, B_HD).sum(0)),
        lru_conv_w=st("lru_conv_w"), lru_conv_b=st("lru_conv_b", vec), lru_w_a=st("lru_w_a", _diag_blocks),
        lru_b_a=st("lru_b_a", vec), lru_w_x=st("lru_w_x", _diag_blocks), lru_b_x=st("lru_b_x", vec),
        lru_lambda=st("lru_lambda", vec), final_norm_g=d_final_g.reshape(-1))


def _local_step(x, tgt, small, layer_weights, on_layer_grads=None):
    layer_p = [_layer_params(l, small["hgrn_lb_logits"], small["hgrn_norm_g"], small["gdn_conv_w"], small["gdn_a_log"],
                             small["gdn_dt_bias"], small["gdn_norm_g"], small["lru_conv_w"], small["lru_conv_b"],
                             small["lru_w_a"], small["lru_b_a"], small["lru_w_x"], small["lru_b_x"], small["lru_lambda"],
                             small["norm1_g"], small["norm2_g"]) for l in range(DEPTH)]
    saved, weights = [], []
    for l in range(DEPTH):
        x, w_in, late = layer_weights(l, x)
        x, sv, w = _layer_fwd(l, x, layer_p[l], w_in, late)
        saved.append(sv)
        weights.append(w)
    loss, dx, d_final_g, dx16 = _loss_and_grad("final_loss", small["final_norm_g"].reshape(1, -1), x, tgt)
    big = [None] * DEPTH
    gs_layers = [None] * DEPTH
    d_logits = jnp.zeros_like(small["hgrn_lb_logits"])
    for l in reversed(range(DEPTH)):
        dx, dx16, big[l], gs_layers[l] = _layer_bwd(l, dx, dx16, layer_p[l], saved[l], *weights[l])
        d_logits = d_logits + gs_layers[l]["lb_logits"]
        if on_layer_grads is not None:
            big[l] = on_layer_grads(l, big[l])
    return loss, dx, big, _small_grads_to_reference(gs_layers, d_final_g, d_logits)


def kernel(x, norm1_g, w_in, hgrn_lb_logits, hgrn_norm_g, gdn_conv_w, gdn_a_log, gdn_dt_bias, gdn_norm_g, lru_conv_w, lru_conv_b, lru_w_a, lru_b_a, lru_w_x, lru_b_x, lru_lambda, w_out, norm2_g, w_up, w_down, final_norm_g, loss_target, m_norm1_g, m_w_in, m_hgrn_lb_logits, m_hgrn_norm_g, m_gdn_conv_w, m_gdn_a_log, m_gdn_dt_bias, m_gdn_norm_g, m_lru_conv_w, m_lru_conv_b, m_lru_w_a, m_lru_b_a, m_lru_w_x, m_lru_b_x, m_lru_lambda, m_w_out, m_norm2_g, m_w_up, m_w_down, m_final_norm_g, v_norm1_g, v_w_in, v_hgrn_lb_logits, v_hgrn_norm_g, v_gdn_conv_w, v_gdn_a_log, v_gdn_dt_bias, v_gdn_norm_g, v_lru_conv_w, v_lru_conv_b, v_lru_w_a, v_lru_b_a, v_lru_w_x, v_lru_b_x, v_lru_lambda, v_w_out, v_norm2_g, v_w_up, v_w_down, v_final_norm_g):
    args = dict(locals())
    ix, iy, core = lax.axis_index("x"), lax.axis_index("y"), lax.axis_index("c")
    chip = 2 * ix + iy

    packed = _pack_big(w_in, w_out, w_up, w_down).astype(BF16)
    first_w_in = _gather_layer(packed[0, :1024], FIRST_GATHER_ID)
    gathered = [_gather_layer(packed[l, 1024 * (l == 0):], GATHER_IDS[l]) for l in range(DEPTH)]

    def with_own(got, own):
        return lax.dynamic_update_slice(got, own[None], (chip, 0, 0))

    def full_w_in(shares):
        return _permute_in_cols(jnp.concatenate([shares[k, :1024, :D_IN // 4] for k in range(4)], axis=-1))

    def full_rest(shares):
        return (jnp.concatenate([shares[k, 2048:] for k in W_OUT_CHIP_ORDER], axis=0),
                jnp.concatenate([shares[k, :1024] for k in range(4)], axis=-1),
                jnp.concatenate([shares[k, 1024:2048] for k in range(4)], axis=0))

    def layer_weights(l, x):
        got, own = gathered[l], packed[l]
        if l:
            got, own, x = lax.optimization_barrier((got, own, x))
            shares = with_own(got, own)
            rest = full_rest(shares[:, 1024:])
            return x, full_w_in(shares), lambda v: (v,) + rest

        def late(v):
            got_v, own_v, v = lax.optimization_barrier((got, own, v))
            return (v,) + full_rest(with_own(got_v, own_v[1024:]))

        return x, full_w_in(with_own(first_w_in, own[:1024])), late

    n_gc, n_lc = gdn_conv_w.shape[-1], lru_conv_w.shape[-1]
    conv_full = dict(
        gdn_conv_w=lax.dynamic_update_slice(jnp.zeros((DEPTH, 4, 4 * n_gc), F32), gdn_conv_w * 0.5, (0, 0, chip * n_gc)),
        lru_conv_w=lax.dynamic_update_slice(jnp.zeros((DEPTH, 4, 4 * n_lc), F32), lru_conv_w * 0.5, (0, 0, chip * n_lc)))
    conv_shapes = {k: v.shape for k, v in conv_full.items()}
    conv_full = _unflatten_small(_allreduce_small("convw", _flatten_small(conv_full)), conv_shapes)

    small = {k: args[k] for k in SMALL}
    small.update(conv_full)

    n_in = D_IN // 4

    def start_scatter(l, b):
        pieces = []
        for k in range(4):
            cols = jnp.concatenate(_orig_cols(b["w_in"], k * n_in, (k + 1) * n_in), axis=1)
            r0 = 256 * W_OUT_CHIP_ORDER.index(k)
            pieces += [jnp.pad(cols, ((0, 0), (0, 1024 - n_in))), b["w_up"][:, k * 1024:(k + 1) * 1024],
                       b["w_down"][k * 1024:(k + 1) * 1024], b["w_out"][r0:r0 + 256]]
        per_chip = jnp.concatenate(pieces, axis=0).reshape(4, LAYER_ROWS, 1024)
        return _scatter_start(per_chip, chip, core, l)

    loss, grad_x, started, sg = _local_step(x[0], loss_target[0], small, layer_weights, start_scatter)

    sg["loss"] = loss[0, :1]
    shapes = {k: v.shape for k, v in sg.items()}
    sg = _unflatten_small(_allreduce_small("smallgrad", _flatten_small(sg)), shapes)
    loss_out = sg.pop("loss")[0]

    buf = None
    for l in reversed(range(DEPTH)):
        own, got = started[l]
        buf = _rs_add_chips(own, got, core, l, buf)
    g_in, g_out, g_up, g_down = _unpack_big(_rs_share(buf).reshape(DEPTH, LAYER_ROWS, 1024))

    grads = dict(sg)
    grads["gdn_conv_w"] = lax.dynamic_slice_in_dim(sg["gdn_conv_w"], chip * n_gc, n_gc, axis=2)
    grads["lru_conv_w"] = lax.dynamic_slice_in_dim(sg["lru_conv_w"], chip * n_lc, n_lc, axis=2)
    grads.update(w_in=g_in, w_out=g_out, w_up=g_up, w_down=g_down)

    names = ['norm1_g', 'w_in', 'hgrn_lb_logits', 'hgrn_norm_g', 'gdn_conv_w', 'gdn_a_log', 'gdn_dt_bias', 'gdn_norm_g',
             'lru_conv_w', 'lru_conv_b', 'lru_w_a', 'lru_b_a', 'lru_w_x', 'lru_b_x', 'lru_lambda', 'w_out', 'norm2_g',
             'w_up', 'w_down', 'final_norm_g']
    big_names = ("w_in", "w_out", "w_up", "w_down")
    delta, new_m, new_v = {}, {}, {}
    for k in big_names:
        delta[k], new_m[k], new_v[k] = _adam("adam_" + k, args[k], grads[k], args["m_" + k], args["v_" + k])
    small_names = [k for k in names if k not in big_names]
    shapes = {k: args[k].shape for k in small_names}
    flat = [_flatten_small({k: src(k) for k in small_names})
            for src in (lambda k: args[k], lambda k: grads[k], lambda k: args["m_" + k], lambda k: args["v_" + k])]
    outs = _ew("adam_small", _adam_fn, flat, 3)
    for dst, o in zip((delta, new_m, new_v), outs):
        dst.update(_unflatten_small(o, shapes))
    return (loss_out, grad_x[None], *[grads[k] for k in names], *[delta[k] for k in names],
            *[new_m[k] for k in names], *[new_v[k] for k in names])
```

```python
import functools

import jax
import jax.numpy as jnp
from jax import lax
from jax.experimental import pallas as pl
from jax.experimental.pallas import tpu as pltpu
from jax.experimental.pallas import tpu_sc as plsc

F32 = jnp.float32
BF16 = jnp.bfloat16
MESH = pl.DeviceIdType.MESH

DEPTH = 4
D_MODEL = 1024
A_W, B_W, C_W = 256, 512, 256
A_HD, B_HD = 64, 128
B_NH = B_W // B_HD
D_IN = 3592
D_IN_PAD = 3840
COL_BZ, COL_AQ, COL_AF, COL_AI, COL_AG, COL_CX, COL_CY = 1536, 2048, 2304, 2560, 2816, 3072, 3328
W_OUT_CHIP_ORDER = (1, 2, 0, 3)
GATE0 = 3584
D_FF = 4096
CHUNK = 64
SUB = 16
EPS = 1e-6
TINY = 1e-30
MASKED_EXPONENT = -1e30
RG_C = 8.0
LR, B1, B2, AEPS, WD, STEP = 0.001, 0.9, 0.999, 1e-08, 0.01, 10
VMEM_LIMIT = 56 * 1024 * 1024
PACK_ROWS = 13 * 1024

NN = ((1,), (0,))
NT = ((1,), (1,))
TN = ((0,), (0,))


DIMS = {"nn": NN, "nt": NT, "tn": TN}


def _split(a, n):
    parts, r = [], a
    for i in range(n):
        p = r.astype(BF16)
        parts.append(p)
        if i + 1 < n:
            r = r - p.astype(F32)
    return parts


def _split_dot(a, b, mode, na, nb):
    (ca,), (cb,) = DIMS[mode]
    dims = ((DIMS[mode]), ((), ())) if a.ndim == 2 else (((ca + 1,), (cb + 1,)), ((0,), (0,)))
    pa, pb = _split(a, na), _split(b, nb)
    acc = None
    for i in range(na):
        for j in range(nb):
            if i + j < max(na, nb):
                t = lax.dot_general(pa[i], pb[j], dims, preferred_element_type=F32)
                acc = t if acc is None else acc + t
    return acc


@functools.partial(jax.custom_vjp, nondiff_argnums=(2, 3, 4, 5))
def _mdot(a, b, mode, na, nb, need):
    return _split_dot(a, b, mode, na, nb)


def _mdot_fwd(a, b, mode, na, nb, need):
    return _split_dot(a, b, mode, na, nb), (a, b)


def _mdot_bwd(mode, na, nb, need, res, ct):
    a, b = res
    n = max(na, nb)
    da, db = jnp.zeros_like(a), jnp.zeros_like(b)
    if mode == "nn":
        if need[0]:
            da = _split_dot(ct, b, "nt", n, nb)
        if need[1]:
            db = _split_dot(a, ct, "tn", na, n)
    elif mode == "nt":
        if need[0]:
            da = _split_dot(ct, b, "nn", n, nb)
        if need[1]:
            db = _split_dot(ct, a, "tn", n, na)
    else:
        if need[0]:
            da = _split_dot(b, ct, "nt", nb, n)
        if need[1]:
            db = _split_dot(a, ct, "nn", na, n)
    return da, db


_mdot.defvjp(_mdot_fwd, _mdot_bwd)
BOTH = (True, True)


def _bdot(a, b, mode="nn"):
    return _mdot(a, b, mode, 1, 1, BOTH)


def _group_sum(x, ones):
    return _mdot(x, ones, "nn", 2, 1, (True, False))


def _running_sum(x):
    shape = x.shape[:-2] + (CHUNK, CHUNK)
    tri = (_iota(shape, len(shape) - 1) <= _iota(shape, len(shape) - 2)).astype(F32)
    return _mdot(tri, x, "nn", 1, 3, (False, True))


def _iota(shape, d):
    return lax.broadcasted_iota(jnp.int32, shape, d)


def _block_ones(width, group):
    return (_iota((width, width), 0) // group == _iota((width, width), 1) // group).astype(F32)


def _params(sem):
    return pltpu.CompilerParams(dimension_semantics=sem, vmem_limit_bytes=VMEM_LIMIT)


def _tile(dim, prefs):
    for p in prefs:
        if dim % p == 0:
            return p
    return dim


def _whole(shape):
    return pl.BlockSpec(shape, lambda i: (0,) * len(shape))


def _rowspec(t, w, col0=0):
    cb = col0 // w
    return pl.BlockSpec((t, w), lambda i: (i, cb))


def _cols(arr, col0, width):
    assert col0 % width == 0
    return (arr, col0, width)


def _row_operands(rows, t):
    arrs, specs = [], []
    for r in rows:
        arr, col0, w = r if isinstance(r, tuple) else (r, 0, r.shape[1])
        arrs.append(arr)
        specs.append(_rowspec(t, w, col0))
    return arrs, specs


def _rowwise_fwd(name, fn, params, rows, out_widths, out_dtypes, t):
    rows, row_specs = _row_operands(rows, t)
    s = rows[0].shape[0]
    n_p, n_r = len(params), len(rows)

    def body(*refs):
        p = [r[...] for r in refs[:n_p]]
        xs = [r[...].astype(F32) for r in refs[n_p:n_p + n_r]]
        outs = fn(pl.program_id(0) * t, *p, *xs)
        for o_ref, o in zip(refs[n_p + n_r:], outs):
            o_ref[...] = o.astype(o_ref.dtype)

    return pl.pallas_call(
        body, grid=(s // t,),
        in_specs=[_whole(p.shape) for p in params] + row_specs,
        out_specs=[_rowspec(t, w) for w in out_widths],
        out_shape=[jax.ShapeDtypeStruct((s, w), d) for w, d in zip(out_widths, out_dtypes)],
        name=name, compiler_params=_params(("arbitrary",)))(*params, *rows)


def _into_spec(t, width, col0, rev_steps=None):
    cb = col0 // width
    assert col0 % width == 0
    if rev_steps is None:
        return pl.BlockSpec((t, width), lambda i: (i, cb))
    return pl.BlockSpec((t, width), lambda i: (rev_steps - 1 - i, cb))


def _rowwise_bwd(name, fn, params, rows, douts, t, need=None, res=None, dtypes=None, copy16=False, pack=None):
    rows, row_specs = _row_operands(rows, t)
    douts, dout_specs = _row_operands(douts, t)
    s = rows[0].shape[0]
    n_p, n_r, n_o = len(params), len(rows), len(douts)
    need = [True] * n_r if need is None else need
    n_res = 0 if res is None else 1
    n_buf = 0 if pack is None or pack[3] is None else 1

    def body(*refs):
        p = [r[...] for r in refs[:n_p]]
        xs = [r[...].astype(F32) for r in refs[n_p:n_p + n_r]]
        dys = [r[...].astype(F32) for r in refs[n_p + n_r:n_p + n_r + n_o]]
        k = n_p + n_r + n_o
        res_ref = refs[k] if n_res else None
        k += n_res + n_buf
        dp_refs = refs[k:k + n_p]
        dx_refs = list(refs[k + n_p:])
        row0 = pl.program_id(0) * t
        _, vjp = jax.vjp(lambda *a: tuple(fn(row0, *a)), *p, *xs)
        g = vjp(tuple(dys))

        @pl.when(pl.program_id(0) == 0)
        def _():
            for r in dp_refs:
                r[...] = jnp.zeros_like(r)

        for r, gp in zip(dp_refs, g[:n_p]):
            r[...] += gp
        gx = [gi for gi, nd in zip(g[n_p:], need) if nd]
        if n_res:
            gx[0] = gx[0] + res_ref[...]
        if copy16:
            gx.append(gx[0])
        if pack is not None:
            pieces = [gx[j].astype(BF16) for j in pack[0]]
            missing = pack[2] - sum(pc.shape[1] for pc in pieces)
            if missing:
                pieces.append(jnp.zeros((t, missing), BF16))
            dx_refs.pop()[...] = pieces[0] if len(pieces) == 1 else jnp.concatenate(pieces, axis=1)
            gx = [gi for j, gi in enumerate(gx) if j not in pack[0]]
        for r, gi in zip(dx_refs, gx):
            r[...] = gi.astype(r.dtype)

    widths = [sp.block_shape[1] for sp, nd in zip(row_specs, need) if nd]
    dtypes = [F32] * len(widths) if dtypes is None else list(dtypes)
    if copy16:
        widths, dtypes = widths + widths[:1], dtypes + [BF16]
    out_specs = [_rowspec(t, w) for w in widths]
    out_shape = [jax.ShapeDtypeStruct((s, w), d) for w, d in zip(widths, dtypes)]
    extra, aliases = [], {}
    if pack is not None:
        keep = [j for j in range(len(widths)) if j not in pack[0]]
        out_specs = [out_specs[j] for j in keep] + [_into_spec(t, pack[2], pack[1])]
        out_shape = [out_shape[j] for j in keep] + [jax.ShapeDtypeStruct((s, D_IN_PAD), BF16)]
        if n_buf:
            extra = [pack[3]]
            aliases = {n_p + n_r + n_o + n_res: n_p + len(out_shape) - 1}
    outs = pl.pallas_call(
        body, grid=(s // t,),
        in_specs=([_whole(p.shape) for p in params] + row_specs + dout_specs
                  + ([_rowspec(t, res.shape[1])] if n_res else []) + [pl.BlockSpec(memory_space=pl.ANY)] * n_buf),
        out_specs=[_whole(p.shape) for p in params] + out_specs,
        out_shape=[jax.ShapeDtypeStruct(p.shape, F32) for p in params] + out_shape, input_output_aliases=aliases,
        name=name, compiler_params=_params(("arbitrary",)))(*params, *rows, *douts, *([res] if n_res else []), *extra)
    return list(outs[:n_p]), list(outs[n_p:])


def _rmsnorm_fn(row0, g, x):
    y = x * lax.rsqrt(jnp.mean(x * x, axis=-1, keepdims=True) + EPS)
    return (y * g,)


def _hgrn_pre_fn(layer, row0, logits, aq, af):
    m = jnp.max(logits, axis=0, keepdims=True)
    e = jnp.exp(logits - m)
    p = e / jnp.sum(e, axis=0, keepdims=True)
    if layer == 0:
        lb = jnp.zeros((1, A_W), F32)
    else:
        acc = p[1:2]
        for j in range(2, layer + 1):
            acc = acc + p[j:j + 1]
        lb = jnp.minimum(jnp.maximum(acc, 0.0), 1.0 - EPS)
    sig = jax.nn.sigmoid(af)
    f = lb + (1.0 - lb) * sig
    log_f = jnp.log(jnp.maximum(f, TINY))
    k = (1.0 - lb) * jax.nn.sigmoid(-af)
    return jax.nn.silu(aq), k, log_f


def _hgrn_post_fn(row0, g, o, ag):
    ms = _group_sum(o * o, _block_ones(A_W, A_HD)) * (1.0 / A_HD)
    return (o * lax.rsqrt(ms + EPS) * g * jax.nn.silu(ag),)


def _gdn_pre_fn(row0, a_log, dt_bias, qkvc, gate):
    act = jax.nn.silu(qkvc)
    q, k, v = act[:, :B_W], act[:, B_W:2 * B_W], act[:, 2 * B_W:]
    bo = _block_ones(B_W, B_HD)
    q = q * lax.rsqrt(_group_sum(q * q, bo) + EPS) * (B_HD ** -0.5)
    k = k * lax.rsqrt(_group_sum(k * k, bo) + EPS)
    beta = jax.nn.sigmoid(gate)
    log_alpha = -jnp.exp(a_log) * jax.nn.softplus(gate + dt_bias)
    t = gate.shape[0]
    spread = lambda z, lane0: jnp.concatenate(
        [jnp.broadcast_to(z[:, lane0 + h:lane0 + h + 1], (t, B_HD)) for h in range(B_NH)], axis=1)
    return q, k, v, spread(beta, 0), spread(log_alpha, B_NH)


def _gdn_post_fn(row0, g, o, bz):
    ms = _group_sum(o * o, _block_ones(B_W, B_HD)) * (1.0 / B_HD)
    return (o * lax.rsqrt(ms + EPS) * g * jax.nn.silu(bz),)


def _lru_pre_fn(row0, w_a, b_a, w_x, b_x, lam, xc):
    r = jax.nn.sigmoid(_mdot(xc, w_a, "nn", 2, 2, BOTH) + b_a)
    i = jax.nn.sigmoid(_mdot(xc, w_x, "nn", 2, 2, BOTH) + b_x)
    log_a = -RG_C * r * jax.nn.softplus(-lam)
    a = jnp.exp(log_a)
    one_minus = -jnp.tanh(log_a) * (a * a + 1.0)
    mult = jnp.sqrt(jnp.maximum(one_minus, EPS))
    pos = row0 + _iota(xc.shape, 0)
    mult = jnp.where(pos == 0, 1.0, mult)
    return a, mult * i * xc


def _lru_post_fn(row0, h, cy):
    return (jax.nn.gelu(cy, approximate=True) * h,)


def _lru_da_fn(row0, g, h_prev):
    return (g * h_prev,)


def _relu2(u):
    r = jnp.maximum(u, 0.0)
    return r * r


def _drelu2(acc, u):
    return acc * (2.0 * jnp.maximum(u, 0.0))


MM_VMEM_BUDGET = 44 * 1024 * 1024


def _mm_tiles(mode, m, n, k, a_bytes, b_bytes, n_extra):
    best = None
    for tk in {k, 4096, 2048, 1024, 512, 256}:
        for tn in {n, 1280, 1024, 768, 512, 256, 128}:
            for tm in {m, 1024, 512, 256, 128}:
                if m % tm or n % tn or k % tk or tm > 1024:
                    continue
                tiles = tm * tk * a_bytes + tk * tn * b_bytes + (1 + n_extra) * tm * tn * 4
                if 2 * tiles + (tm * tn * 4 if tk < k else 0) > MM_VMEM_BUDGET:
                    continue
                a_reads = 1 if tk == k else n // tn
                b_reads = 1 if (tk == k and tn == n) else m // tm
                cost = m * k * a_bytes * a_reads + k * n * b_bytes * b_reads + m * n * 4 * (k // tk - 1)
                if best is None or (cost, -tm) < best[0]:
                    best = ((cost, -tm), (tm, tn, tk))
    if best is None:
        raise ValueError("no matmul tiling fits VMEM")
    return best[1]


def _matmul(name, a, b, mode, a_fn=None, res=None, epi=None, epi_in=None, out_dtype=F32, also=None):
    if mode == "nn":
        (m, k), n = a.shape, b.shape[1]
    elif mode == "nt":
        (m, k), n = a.shape, b.shape[0]
    else:
        (k, m), n = a.shape, b.shape[1]
    n_extra = (res is not None) + (epi_in is not None) + (also is not None)
    tm, tn, tk = _mm_tiles(mode, m, n, k, a.dtype.itemsize, b.dtype.itemsize, n_extra)
    nk = k // tk
    a_spec = {"nn": pl.BlockSpec((tm, tk), lambda i, j, kk: (i, kk)),
              "nt": pl.BlockSpec((tm, tk), lambda i, j, kk: (i, kk)),
              "tn": pl.BlockSpec((tk, tm), lambda i, j, kk: (kk, i))}[mode]
    b_spec = {"nn": pl.BlockSpec((tk, tn), lambda i, j, kk: (kk, j)),
              "nt": pl.BlockSpec((tn, tk), lambda i, j, kk: (j, kk)),
              "tn": pl.BlockSpec((tk, tn), lambda i, j, kk: (kk, j))}[mode]
    o_spec = pl.BlockSpec((tm, tn), lambda i, j, kk: (i, j))
    dims = {"nn": NN, "nt": NT, "tn": TN}[mode]
    extra = [x for x in (res, epi_in) if x is not None]

    def body(*refs):
        a_ref, b_ref = refs[0], refs[1]
        rest = list(refs[2:])
        res_ref = rest.pop(0) if res is not None else None
        epi_ref = rest.pop(0) if epi_in is not None else None
        o_ref = rest.pop(0)
        also_ref = rest.pop(0) if also is not None else None
        acc = rest[0] if nk > 1 else None
        kk = pl.program_id(2)
        at = a_ref[...]
        if a_fn is not None:
            at = a_fn(at.astype(F32))
        part = lax.dot_general(at.astype(BF16), b_ref[...].astype(BF16), (dims, ((), ())),
                               preferred_element_type=F32)

        def finish(out):
            if res_ref is not None:
                out = out + res_ref[...]
            if epi is not None:
                out = epi(out, epi_ref[...])
            o_ref[...] = out.astype(o_ref.dtype)
            if also is not None:
                also_ref[...] = also(out).astype(also_ref.dtype)

        if nk == 1:
            finish(part)
        else:
            @pl.when(kk == 0)
            def _():
                acc[...] = part

            @pl.when(jnp.logical_and(kk > 0, kk < nk - 1))
            def _():
                acc[...] += part

            @pl.when(kk == nk - 1)
            def _():
                finish(acc[...] + part)

    out_shape = [jax.ShapeDtypeStruct((m, n), out_dtype)] + ([jax.ShapeDtypeStruct((m, n), BF16)] if also else [])
    outs = pl.pallas_call(
        body, grid=(m // tm, n // tn, nk),
        in_specs=[a_spec, b_spec] + [o_spec] * len(extra), out_specs=[o_spec] * len(out_shape), out_shape=out_shape,
        scratch_shapes=[pltpu.VMEM((tm, tn), F32)] if nk > 1 else [],
        name=name, compiler_params=_params(("arbitrary", "arbitrary", "arbitrary")))(a, b, *extra)
    return outs if also else outs[0]


HALO = 8


def _conv_fwd(name, x, col0, width, w, bias, t=512):
    s = x.shape[0]
    cb = col0 // width
    hb = t // HALO

    def body(x_ref, halo_ref, w_ref, b_ref, y_ref):
        i = pl.program_id(0)
        halo = jnp.where(i == 0, 0.0, halo_ref[...])
        xp = jnp.concatenate([halo, x_ref[...]], axis=0)
        wv = w_ref[...]
        y = b_ref[...] + wv[0:1] * xp[HALO - 3:HALO - 3 + t]
        for j in range(1, 4):
            y = y + wv[j:j + 1] * xp[HALO - 3 + j:HALO - 3 + j + t]
        y_ref[...] = y

    return pl.pallas_call(
        body, grid=(s // t,),
        in_specs=[pl.BlockSpec((t, width), lambda i: (i, cb)),
                  pl.BlockSpec((HALO, width), lambda i: (jnp.maximum(i * hb - 1, 0), cb)),
                  _whole(w.shape), _whole(bias.shape)],
        out_specs=_rowspec(t, width), out_shape=jax.ShapeDtypeStruct((s, width), F32),
        name=name, compiler_params=_params(("arbitrary",)))(x, x, w, bias)


def _conv_bwd(name, dy, x, col0, width, w, buf, t=512):
    s = x.shape[0]
    cb = col0 // width
    hb = t // HALO
    nblk = s // t

    def body(dy_ref, dyn_ref, x_ref, halo_ref, w_ref, buf_ref, dx_ref, dw_ref, db_ref):
        i = pl.program_id(0)
        dyv = dy_ref[...]
        nxt = jnp.where(i == nblk - 1, 0.0, dyn_ref[...])
        dyp = jnp.concatenate([dyv, nxt], axis=0)
        wv = w_ref[...]
        dx = wv[3:4] * dyv
        for j in range(3):
            dx = dx + wv[j:j + 1] * dyp[3 - j:3 - j + t]
        dx_ref[...] = dx.astype(dx_ref.dtype)
        halo = jnp.where(i == 0, 0.0, halo_ref[...])
        xp = jnp.concatenate([halo, x_ref[...]], axis=0)
        dw = jnp.concatenate(
            [jnp.sum(dyv * xp[HALO - 3 + j:HALO - 3 + j + t], axis=0, keepdims=True) for j in range(4)], axis=0)

        @pl.when(i == 0)
        def _():
            dw_ref[...] = jnp.zeros_like(dw_ref)
            db_ref[...] = jnp.zeros_like(db_ref)

        dw_ref[...] += dw
        db_ref[...] += jnp.sum(dyv, axis=0, keepdims=True)

    return pl.pallas_call(
        body, grid=(nblk,),
        in_specs=[_rowspec(t, width),
                  pl.BlockSpec((HALO, width), lambda i: (jnp.minimum((i + 1) * hb, s // HALO - 1), 0)),
                  pl.BlockSpec((t, width), lambda i: (i, cb)),
                  pl.BlockSpec((HALO, width), lambda i: (jnp.maximum(i * hb - 1, 0), cb)),
                  _whole(w.shape), pl.BlockSpec(memory_space=pl.ANY)],
        out_specs=[_into_spec(t, width, col0), _whole((4, width)), _whole((1, width))],
        out_shape=[jax.ShapeDtypeStruct(buf.shape, buf.dtype), jax.ShapeDtypeStruct((4, width), F32),
                   jax.ShapeDtypeStruct((1, width), F32)],
        input_output_aliases={5: 0},
        name=name, compiler_params=_params(("arbitrary",)))(dy, dy, x, x, w, buf)


def _hgrn_chunk(st, q, k, v, lf):
    cum = _running_sum(lf)
    tot = jnp.sum(lf, axis=0, keepdims=True)
    bm = _iota((A_W, A_W), 0) // A_HD == _iota((A_W, A_W), 1) // A_HD
    bo = bm.astype(F32)
    o_inter = _bdot(q * jnp.exp(cum), st, "nt")
    kd = k * jnp.exp(tot - cum)
    st_new = st * jnp.exp(tot) + jnp.where(bm, _bdot(v, kd, "tn"), 0.0)
    lane_h = _iota((SUB, A_W), 1) // A_HD
    n_h = A_W // A_HD
    t3 = _iota((SUB, SUB, A_W), 0)
    s3 = _iota((SUB, SUB, A_W), 1)
    outs = []
    for i in range(CHUNK // SUB):
        sl = slice(i * SUB, (i + 1) * SUB)
        qi, ki, vi, ci = q[sl], k[sl], v[sl], cum[sl]
        e = jnp.exp(jnp.where(s3 <= t3, ci[:, None, :] - ci[None, :, :], MASKED_EXPONENT))
        d3 = qi[:, None, :] * ki[None, :, :] * e
        ds = _mdot(d3.reshape(SUB * SUB, A_W), bo, "nn", 1, 1, (True, False)).reshape(SUB, SUB, A_W)
        oi = jnp.sum(ds * vi[None, :, :], axis=1)
        if i > 0:
            n = i * SUB
            cb = cum[n - 1:n]
            qt = qi * jnp.exp(ci - cb)
            kt = k[:n] * jnp.exp(cb - cum[:n])
            qs = jnp.concatenate([jnp.where(lane_h == h, qt, 0.0) for h in range(n_h)], axis=0)
            p = _bdot(_bdot(qs, kt, "nt"), v[:n])
            for h in range(n_h):
                oi = oi + jnp.where(lane_h == h, p[h * SUB:(h + 1) * SUB], 0.0)
        outs.append(oi)
    return st_new, o_inter + jnp.concatenate(outs, axis=0)


HGRN_CHUNKS = 2


def _hgrn_fwd(name, q, k, v, lf):
    s = q.shape[0]
    n = s // CHUNK
    g = HGRN_CHUNKS

    def body(q_ref, k_ref, v_ref, lf_ref, o_ref, sts_ref, st_ref):
        @pl.when(pl.program_id(0) == 0)
        def _():
            st_ref[...] = jnp.zeros_like(st_ref)

        st = st_ref[...]
        for c in range(g):
            rows = pl.ds(c * CHUNK, CHUNK)
            sts_ref[c] = st
            st, o = _hgrn_chunk(st, q_ref[rows], k_ref[rows], v_ref[rows], lf_ref[rows])
            o_ref[rows] = o
        st_ref[...] = st

    cs = _rowspec(g * CHUNK, A_W)
    return pl.pallas_call(
        body, grid=(n // g,), in_specs=[cs] * 4,
        out_specs=[cs, pl.BlockSpec((g, A_W, A_W), lambda i: (i, 0, 0))],
        out_shape=[jax.ShapeDtypeStruct((s, A_W), F32), jax.ShapeDtypeStruct((n, A_W, A_W), F32)],
        scratch_shapes=[pltpu.VMEM((A_W, A_W), F32)],
        name=name, compiler_params=_params(("arbitrary",)))(q, k, v, lf)


def _hgrn_bwd(name, q, k, v, lf, sts, do, buf, col0):
    s = q.shape[0]
    n = s // CHUNK
    g = HGRN_CHUNKS
    nsteps = n // g

    def body(q_ref, k_ref, v_ref, lf_ref, sts_ref, do_ref, buf_ref, dq_ref, dk_ref, dv_ref, dlf_ref, dst_ref):
        @pl.when(pl.program_id(0) == 0)
        def _():
            dst_ref[...] = jnp.zeros_like(dst_ref)

        dst = dst_ref[...]
        for c in reversed(range(g)):
            rows = pl.ds(c * CHUNK, CHUNK)
            _, vjp = jax.vjp(_hgrn_chunk, sts_ref[c], q_ref[rows], k_ref[rows], v_ref[rows], lf_ref[rows])
            ct = vjp((dst, do_ref[rows]))
            dst = ct[0]
            for r, gi in zip((dq_ref, dk_ref, dv_ref, dlf_ref), ct[1:]):
                r[rows] = gi.astype(r.dtype)
        dst_ref[...] = dst

    rs = pl.BlockSpec((g * CHUNK, A_W), lambda i: (nsteps - 1 - i, 0))
    rows = jax.ShapeDtypeStruct((s, A_W), F32)
    return pl.pallas_call(
        body, grid=(nsteps,),
        in_specs=[rs] * 4 + [pl.BlockSpec((g, A_W, A_W), lambda i: (nsteps - 1 - i, 0, 0)), rs,
                             pl.BlockSpec(memory_space=pl.ANY)],
        out_specs=[rs, rs, _into_spec(g * CHUNK, A_W, col0, nsteps), rs],
        out_shape=[rows, rows, jax.ShapeDtypeStruct(buf.shape, buf.dtype), rows],
        input_output_aliases={6: 2}, scratch_shapes=[pltpu.VMEM((A_W, A_W), F32)],
        name=name, compiler_params=_params(("arbitrary",)))(q, k, v, lf, sts, do, buf)


PREP_CHUNKS = 4
REC_CHUNKS = 8
DEC_ROWS = 8


def _to_heads(x, rows=CHUNK, width=B_HD):
    n = x.shape[0] // rows
    x3 = x.reshape(n, rows, x.shape[1])
    return jnp.concatenate([x3[:, :, h * B_HD:h * B_HD + width] for h in range(B_NH)], axis=0)


def _from_heads(y):
    n = y.shape[0] // B_NH
    x3 = jnp.concatenate([y[h * n:(h + 1) * n] for h in range(B_NH)], axis=2)
    return x3.reshape(n * y.shape[1], B_NH * B_HD)


@jax.custom_vjp
def _unit_lower_inverse(a_mat):
    nil = -a_mat
    eye = (_iota(a_mat.shape, 1) == _iota(a_mat.shape, 2)).astype(F32)
    inv = eye + nil
    sq = 2
    while sq < CHUNK:
        nil = _split_dot(nil, nil, "nn", 2, 2)
        inv = inv + _split_dot(inv, nil, "nn", 2, 2)
        sq *= 2
    return inv


def _unit_lower_inverse_fwd(a_mat):
    inv = _unit_lower_inverse(a_mat)
    return inv, inv


def _unit_lower_inverse_bwd(inv, ct):
    return (-_split_dot(_split_dot(inv, ct, "tn", 2, 2), inv, "nt", 2, 2),)


_unit_lower_inverse.defvjp(_unit_lower_inverse_fwd, _unit_lower_inverse_bwd)


@jax.custom_vjp
def _known_inverse(a_mat, inv):
    return inv


def _known_inverse_fwd(a_mat, inv):
    return inv, inv


def _known_inverse_bwd(inv, ct):
    return _unit_lower_inverse_bwd(inv, ct)[0], jnp.zeros_like(inv)


_known_inverse.defvjp(_known_inverse_fwd, _known_inverse_bwd)


def _gdn_prep(q, k, v, beta, la, inv_saved=None):
    q, k, v, beta, la = [_to_heads(z) for z in (q, k, v, beta, la)]
    nb = q.shape[0]
    r = _iota((nb, CHUNK, CHUNK), 1)
    c = _iota((nb, CHUNK, CHUNK), 2)
    causal = c <= r
    g = _running_sum(la)
    tot = jnp.sum(la, axis=1, keepdims=True)
    gcol = g[:, :, :CHUNK]
    decay = jnp.exp(jnp.where(causal, gcol - jnp.swapaxes(gcol, 1, 2), MASKED_EXPONENT))
    kb = k * beta
    a_mat = jnp.where(c < r, _bdot(kb, k, "nt") * decay, 0.0)
    if inv_saved is None:
        inv = _unit_lower_inverse(a_mat)
    else:
        inv = _known_inverse(a_mat, _to_heads(inv_saved, width=CHUNK))
    eg = jnp.exp(g)
    u = _bdot(inv, v * beta)
    w = _bdot(inv, kb * eg)
    qk = jnp.where(causal, _bdot(q, k, "nt") * decay, 0.0)
    widen = lambda z: jnp.concatenate([z, jnp.zeros((nb, CHUNK, B_HD - CHUNK), F32)], axis=2)
    dec = jnp.broadcast_to(jnp.exp(tot), (nb, DEC_ROWS, B_HD))
    return tuple(_from_heads(z) for z in (u, w, widen(qk), q * eg, k * jnp.exp(tot - g), dec, widen(inv)))


def _gdn_rec(st, u, w, qk, qd, kd, dec):
    u, w, qd, kd = [_to_heads(z) for z in (u, w, qd, kd)]
    qk = _to_heads(qk, width=CHUNK)
    v_new = u - _bdot(w, st)
    o = _bdot(qd, st) + _bdot(qk, v_new)
    st_new = st * _to_heads(dec, rows=DEC_ROWS)[:, :1, :1] + _bdot(kd, v_new, "tn")
    return st_new, _from_heads(o)


def _gdn_prep_fwd(name, q, k, v, beta, la):
    s = q.shape[0]
    t = PREP_CHUNKS * CHUNK
    td = PREP_CHUNKS * DEC_ROWS

    def body(*refs):
        outs = _gdn_prep(*[r[...] for r in refs[:5]])
        for o_ref, o in zip(refs[5:], outs):
            o_ref[...] = o

    rows, dec_rows = jax.ShapeDtypeStruct((s, B_W), F32), jax.ShapeDtypeStruct((s // CHUNK * DEC_ROWS, B_W), F32)
    return pl.pallas_call(
        body, grid=(s // t,), in_specs=[_rowspec(t, B_W)] * 5,
        out_specs=[_rowspec(t, B_W)] * 5 + [_rowspec(td, B_W), _rowspec(t, B_W)],
        out_shape=[rows] * 5 + [dec_rows, rows],
        name=name, compiler_params=_params(("arbitrary",)))(q, k, v, beta, la)


def _gdn_prep_bwd(name, q, k, v, beta, la, inv, cts):
    s = q.shape[0]
    t = PREP_CHUNKS * CHUNK
    td = PREP_CHUNKS * DEC_ROWS

    def body(*refs):
        inv_saved = refs[5][...]
        _, vjp = jax.vjp(lambda *a: _gdn_prep(*a, inv_saved=inv_saved)[:6], *[r[...] for r in refs[:5]])
        g = vjp(tuple(r[...] for r in refs[6:12]))
        for o_ref, o in zip(refs[12:], g):
            o_ref[...] = o

    return pl.pallas_call(
        body, grid=(s // t,), in_specs=[_rowspec(t, B_W)] * 11 + [_rowspec(td, B_W)],
        out_specs=[_rowspec(t, B_W)] * 5, out_shape=[jax.ShapeDtypeStruct((s, B_W), F32)] * 5,
        name=name, compiler_params=_params(("arbitrary",)))(q, k, v, beta, la, inv, *cts)


def _gdn_rec_fwd(name, u, w, qk, qd, kd, dec):
    s = u.shape[0]
    n = s // CHUNK
    g = REC_CHUNKS

    def body(u_ref, w_ref, qk_ref, qd_ref, kd_ref, dec_ref, o_ref, sts_ref, st_ref):
        @pl.when(pl.program_id(0) == 0)
        def _():
            st_ref[...] = jnp.zeros_like(st_ref)

        st = st_ref[...]
        for c in range(g):
            rows, drows = pl.ds(c * CHUNK, CHUNK), pl.ds(c * DEC_ROWS, DEC_ROWS)
            sts_ref[c] = st
            st, o = _gdn_rec(st, u_ref[rows], w_ref[rows], qk_ref[rows], qd_ref[rows], kd_ref[rows], dec_ref[drows])
            o_ref[rows] = o
        st_ref[...] = st

    cs = _rowspec(g * CHUNK, B_W)
    return pl.pallas_call(
        body, grid=(n // g,), in_specs=[cs] * 5 + [_rowspec(g * DEC_ROWS, B_W)],
        out_specs=[cs, pl.BlockSpec((g, B_NH, B_HD, B_HD), lambda i: (i, 0, 0, 0))],
        out_shape=[jax.ShapeDtypeStruct((s, B_W), F32), jax.ShapeDtypeStruct((n, B_NH, B_HD, B_HD), F32)],
        scratch_shapes=[pltpu.VMEM((B_NH, B_HD, B_HD), F32)],
        name=name, compiler_params=_params(("arbitrary",)))(u, w, qk, qd, kd, dec)


def _gdn_rec_bwd(name, u, w, qk, qd, kd, dec, sts, do):
    s = u.shape[0]
    n = s // CHUNK
    g = REC_CHUNKS
    nsteps = n // g

    def body(u_ref, w_ref, qk_ref, qd_ref, kd_ref, dec_ref, sts_ref, do_ref,
             du_ref, dw_ref, dqk_ref, dqd_ref, dkd_ref, ddec_ref, dst_ref):
        @pl.when(pl.program_id(0) == 0)
        def _():
            dst_ref[...] = jnp.zeros_like(dst_ref)

        dst = dst_ref[...]
        for c in reversed(range(g)):
            rows, drows = pl.ds(c * CHUNK, CHUNK), pl.ds(c * DEC_ROWS, DEC_ROWS)
            _, vjp = jax.vjp(_gdn_rec, sts_ref[c], u_ref[rows], w_ref[rows], qk_ref[rows], qd_ref[rows], kd_ref[rows],
                             dec_ref[drows])
            ct = vjp((dst, do_ref[rows]))
            dst = ct[0]
            for r, gi in zip((du_ref, dw_ref, dqk_ref, dqd_ref, dkd_ref), ct[1:6]):
                r[rows] = gi
            ddec_ref[drows] = ct[6]
        dst_ref[...] = dst

    rs = pl.BlockSpec((g * CHUNK, B_W), lambda i: (nsteps - 1 - i, 0))
    ds = pl.BlockSpec((g * DEC_ROWS, B_W), lambda i: (nsteps - 1 - i, 0))
    return pl.pallas_call(
        body, grid=(nsteps,),
        in_specs=[rs] * 5 + [ds, pl.BlockSpec((g, B_NH, B_HD, B_HD), lambda i: (nsteps - 1 - i, 0, 0, 0)), rs],
        out_specs=[rs] * 5 + [ds],
        out_shape=[jax.ShapeDtypeStruct((s, B_W), F32)] * 5 + [jax.ShapeDtypeStruct((n * DEC_ROWS, B_W), F32)],
        scratch_shapes=[pltpu.VMEM((B_NH, B_HD, B_HD), F32)],
        name=name, compiler_params=_params(("arbitrary",)))(u, w, qk, qd, kd, dec, sts, do)


SCAN_T = 512


def _block_scan(a, b, reverse):
    t = a.shape[0]
    rows = _iota(a.shape, 0)
    sft = 1
    while sft < t:
        if reverse:
            a_s, b_s = pltpu.roll(a, t - sft, 0), pltpu.roll(b, t - sft, 0)
            edge = rows >= t - sft
        else:
            a_s, b_s = pltpu.roll(a, sft, 0), pltpu.roll(b, sft, 0)
            edge = rows < sft
        a_s = jnp.where(edge, 1.0, a_s)
        b_s = jnp.where(edge, 0.0, b_s)
        b = a * b_s + b
        a = a * a_s
        sft *= 2
    return a, b


def _lru_scan_fwd(name, a, b):
    s, w = a.shape
    t = SCAN_T

    def body(a_ref, b_ref, h_ref, hp_ref, carry):
        @pl.when(pl.program_id(0) == 0)
        def _():
            carry[...] = jnp.zeros_like(carry)

        h_in = carry[0:1]
        ca, cb = _block_scan(a_ref[...], b_ref[...], False)
        h = ca * h_in + cb
        h_ref[...] = h
        hp_ref[...] = jnp.where(_iota(h.shape, 0) == 0, h_in, pltpu.roll(h, 1, 0))
        carry[...] = jnp.broadcast_to(h[t - 1:t], carry.shape)

    return pl.pallas_call(
        body, grid=(s // t,), in_specs=[_rowspec(t, w)] * 2, out_specs=[_rowspec(t, w)] * 2,
        out_shape=[jax.ShapeDtypeStruct((s, w), F32)] * 2, scratch_shapes=[pltpu.VMEM((8, w), F32)],
        name=name, compiler_params=_params(("arbitrary",)))(a, b)


def _lru_scan_bwd(name, a, dh):
    s, w = a.shape
    t = SCAN_T
    n = s // t

    def body(a_ref, dh_ref, g_ref, a_next, g_next):
        @pl.when(pl.program_id(0) == 0)
        def _():
            a_next[...] = jnp.zeros_like(a_next)
            g_next[...] = jnp.zeros_like(g_next)

        av = a_ref[...]
        a_up = jnp.where(_iota(av.shape, 0) == t - 1, a_next[0:1], pltpu.roll(av, t - 1, 0))
        ca, cb = _block_scan(a_up, dh_ref[...], True)
        g = ca * g_next[0:1] + cb
        g_ref[...] = g
        a_next[...] = jnp.broadcast_to(av[0:1], a_next.shape)
        g_next[...] = jnp.broadcast_to(g[0:1], g_next.shape)

    rs = pl.BlockSpec((t, w), lambda i: (n - 1 - i, 0))
    return pl.pallas_call(
        body, grid=(n,), in_specs=[rs] * 2, out_specs=rs, out_shape=jax.ShapeDtypeStruct((s, w), F32),
        scratch_shapes=[pltpu.VMEM((8, w), F32), pltpu.VMEM((8, w), F32)],
        name=name, compiler_params=_params(("arbitrary",)))(a, dh)


def _loss_fn(g, x, tgt):
    y = x * lax.rsqrt(jnp.mean(x * x, axis=-1, keepdims=True) + EPS) * g
    err = y - tgt
    return 0.5 * jnp.sum(jnp.mean(err * err, axis=-1, keepdims=True), axis=0, keepdims=True)


def _loss_and_grad(name, g, x, tgt, t=256):
    s, d = x.shape

    def body(g_ref, x_ref, t_ref, loss_ref, dx_ref, dg_ref, dx16_ref):
        val, vjp = jax.vjp(_loss_fn, g_ref[...], x_ref[...], t_ref[...])
        dg, dx, _ = vjp(jnp.ones((1, 1), F32))

        @pl.when(pl.program_id(0) == 0)
        def _():
            loss_ref[...] = jnp.zeros_like(loss_ref)
            dg_ref[...] = jnp.zeros_like(dg_ref)

        loss_ref[...] += jnp.broadcast_to(val, loss_ref.shape)
        dg_ref[...] += dg
        dx_ref[...] = dx
        dx16_ref[...] = dx.astype(BF16)

    return pl.pallas_call(
        body, grid=(s // t,), in_specs=[_whole(g.shape), _rowspec(t, d), _rowspec(t, d)],
        out_specs=[_whole((1, 128)), _rowspec(t, d), _whole(g.shape), _rowspec(t, d)],
        out_shape=[jax.ShapeDtypeStruct((1, 128), F32), jax.ShapeDtypeStruct((s, d), F32),
                   jax.ShapeDtypeStruct(g.shape, F32), jax.ShapeDtypeStruct((s, d), BF16)],
        name=name, compiler_params=_params(("arbitrary",)))(g, x, tgt)


def _ew(name, fn, ins, n_out, t=None):
    r, c = ins[0].shape
    t = _tile(r, (512, 256, 128, 64, 32, 16, 8)) if t is None else t

    def body(*refs):
        outs = fn(*[x[...] for x in refs[:len(ins)]])
        for o_ref, o in zip(refs[len(ins):], outs):
            o_ref[...] = o

    return pl.pallas_call(
        body, grid=(r // t,), in_specs=[_rowspec(t, c)] * len(ins), out_specs=[_rowspec(t, c)] * n_out,
        out_shape=[jax.ShapeDtypeStruct((r, c), F32)] * n_out,
        name=name, compiler_params=_params(("arbitrary",)))(*ins)


def _adam_fn(w, g, m, v):
    m = B1 * m + (1.0 - B1) * g
    v = B2 * v + (1.0 - B2) * (g * g)
    m_hat = m / (1.0 - B1 ** STEP)
    v_hat = v / (1.0 - B2 ** STEP)
    delta = -LR * (m_hat / (jnp.sqrt(v_hat) + AEPS) + WD * w)
    return delta, m, v


def _add2_fn(a, b):
    return (a + b,)


def _adam(name, w, g, m, v):
    shp = w.shape
    two = lambda z: z.reshape(-1, shp[-1])
    outs = _ew(name, _adam_fn, [two(w), two(g), two(m), two(v)], 3)
    return [o.reshape(shp) for o in outs]


def _swap(name, bufs, flips):
    n = len(bufs)
    hbm = pl.BlockSpec(memory_space=pltpu.HBM)

    def body(*refs):
        srcs, dsts = refs[:n], refs[n:2 * n]
        send_sems, recv_sems = refs[2 * n], refs[2 * n + 1]
        me = (lax.axis_index("x"), lax.axis_index("y"), lax.axis_index("c"))
        copies = []
        for i in range(n):
            peer = tuple(1 - m if f else m for m, f in zip(me, flips[i]))
            cp = pltpu.make_async_remote_copy(src_ref=srcs[i], dst_ref=dsts[i], send_sem=send_sems.at[i],
                                              recv_sem=recv_sems.at[i], device_id=peer, device_id_type=MESH)
            cp.start()
            copies.append(cp)
        for cp in copies:
            cp.wait()

    return pl.pallas_call(
        body, in_specs=[hbm] * n, out_specs=[hbm] * n,
        out_shape=[jax.ShapeDtypeStruct(b.shape, b.dtype) for b in bufs],
        scratch_shapes=[pltpu.SemaphoreType.DMA((n,)), pltpu.SemaphoreType.DMA((n,))],
        name=name, compiler_params=pltpu.CompilerParams(has_side_effects=True))(*bufs)


FLIP_C = (False, False, True)
FLIP_Y = (False, True, False)
FLIP_X = (True, False, False)


LAYER_ROWS = PACK_ROWS // DEPTH
HALF_ROWS = LAYER_ROWS // 2
SUM_ROWS = 128
GATHER_IDS = (1, 2, 3, 4)
SCATTER_IDS = (5, 6, 7, 8)
FIRST_GATHER_ID = 9


def _mesh_pos():
    ix, iy, core = lax.axis_index("x"), lax.axis_index("y"), lax.axis_index("c")
    others = [(ix, 1 - iy), (1 - ix, iy), (1 - ix, 1 - iy)]
    return ix, iy, core, others


def _pack_big(w_in, w_out, w_up, w_down):
    pad = jnp.pad(w_in, ((0, 0), (0, 0), (0, 1024 - w_in.shape[-1])))
    return jnp.concatenate([pad, w_up, w_down, w_out], axis=1)


def _unpack_big(p):
    w_in = p[..., 0:1024, :D_IN // 4]
    return w_in, p[..., 3072:LAYER_ROWS, :], p[..., 1024:2048, :], p[..., 2048:3072, :]


def _gather_layer(packed, collective_id):
    src = jax.new_ref(packed, memory_space=pltpu.MemorySpace.HBM)
    out = jax.empty_ref(jax.ShapeDtypeStruct((4,) + packed.shape, packed.dtype), memory_space=pltpu.MemorySpace.HBM)
    dma = pltpu.SemaphoreType.DMA

    @pl.kernel(mesh=plsc.ScalarSubcoreMesh(axis_name="seq", num_cores=1), name="gather_%d" % collective_id,
               scratch_types=(dma,) * 12, compiler_params=pltpu.CompilerParams(collective_id=collective_id))
    def launch(*sems):
        send_sems, recv_sems = sems[:6], sems[6:]
        ix, iy, core, others = _mesh_pos()
        barrier = pltpu.get_barrier_semaphore()
        for peer in [(ox, oy, core) for ox, oy in others] + [(ix, iy, 1 - core)]:
            pl.semaphore_signal(barrier, inc=1, device_id=peer, device_id_type=MESH)
        pl.semaphore_wait(barrier, 4)
        half = packed.shape[0] // 2
        mine = pl.ds(core * half, half)
        theirs = pl.ds((1 - core) * half, half)

        def copy(k, src_ref, owner, rows, to):
            return pltpu.make_async_remote_copy(
                src_ref=src_ref, dst_ref=out.at[owner, rows], send_sem=send_sems[k], recv_sem=recv_sems[k],
                device_id=to, device_id_type=MESH)

        chip = 2 * ix + iy
        first = [copy(j, src.at[mine], chip, mine, (*others[j], core)) for j in range(3)]
        for cp in first:
            cp.start()
        passed = []
        for j, (ox, oy) in enumerate(others):
            owner = 2 * ox + oy
            copy(j, src.at[mine], owner, mine, (ix, iy, core)).wait_recv()
            fwd = copy(3 + j, out.at[owner, mine], owner, mine, (ix, iy, 1 - core))
            fwd.start()
            passed.append(fwd)
        for j, (ox, oy) in enumerate(others):
            copy(3 + j, src.at[theirs], 2 * ox + oy, theirs, (ix, iy, core)).wait_recv()
        for cp in first + passed:
            cp.wait_send()

    launch()
    return out[...]


def _rs_sibling(per_chip):
    hbm = pl.BlockSpec(memory_space=pltpu.HBM)

    def body(src, out, send_sem, recv_sem):
        ix, iy, core, _ = _mesh_pos()
        give = pl.ds((1 - core) * HALF_ROWS, HALF_ROWS)
        cp = pltpu.make_async_remote_copy(src_ref=src.at[pl.ds(0, 4), give], dst_ref=out, send_sem=send_sem,
                                          recv_sem=recv_sem, device_id=(ix, iy, 1 - core), device_id_type=MESH)
        cp.start()
        cp.wait()

    return pl.pallas_call(
        body, in_specs=[hbm], out_specs=hbm, out_shape=jax.ShapeDtypeStruct((4, HALF_ROWS, 1024), F32),
        scratch_shapes=[pltpu.SemaphoreType.DMA, pltpu.SemaphoreType.DMA],
        name="rs_sibling", compiler_params=pltpu.CompilerParams(has_side_effects=True))(per_chip)


def _rs_add_pair(per_chip, sib, chip, core):
    nb = HALF_ROWS // SUM_ROWS
    blk = (1, SUM_ROWS, 1024)

    def body(chip_ref, core_ref, a_ref, b_ref, own_ref, send_ref):
        total = a_ref[0] + b_ref[0]
        send_ref[0] = total.astype(BF16)

        @pl.when(pl.program_id(1) == chip_ref[0])
        def _():
            own_ref[...] = total

    return pl.pallas_call(
        body,
        grid_spec=pltpu.PrefetchScalarGridSpec(
            num_scalar_prefetch=2, grid=(nb, 4),
            in_specs=[pl.BlockSpec(blk, lambda i, k, ch, co: (k, co[0] * nb + i, 0)),
                      pl.BlockSpec(blk, lambda i, k, ch, co: (k, i, 0))],
            out_specs=[pl.BlockSpec((SUM_ROWS, 1024), lambda i, k, ch, co: (i, 0)),
                       pl.BlockSpec(blk, lambda i, k, ch, co: (k, i, 0))]),
        out_shape=[jax.ShapeDtypeStruct((HALF_ROWS, 1024), F32), jax.ShapeDtypeStruct((4, HALF_ROWS, 1024), BF16)],
        name="rs_add_pair", compiler_params=_params(("arbitrary", "arbitrary")))(
            chip.reshape(1), core.reshape(1), per_chip, sib)


def _rs_chips(pair, collective_id):
    src_ref = jax.new_ref(pair, memory_space=pltpu.MemorySpace.HBM)
    got_ref = jax.empty_ref(jax.ShapeDtypeStruct((3,) + pair.shape[1:], pair.dtype), memory_space=pltpu.MemorySpace.HBM)
    dma = pltpu.SemaphoreType.DMA

    @pl.kernel(mesh=plsc.ScalarSubcoreMesh(axis_name="seq", num_cores=1), name="rs_chips_%d" % collective_id,
               scratch_types=(dma,) * 6, compiler_params=pltpu.CompilerParams(collective_id=collective_id))
    def launch(s0, s1, s2, r0, r1, r2):
        ix, iy, core, others = _mesh_pos()
        barrier = pltpu.get_barrier_semaphore()
        for ox, oy in others:
            pl.semaphore_signal(barrier, inc=1, device_id=(ox, oy, core), device_id_type=MESH)
        pl.semaphore_wait(barrier, 3)
        copies = []
        for j, ((ox, oy), send_sem, recv_sem) in enumerate(zip(others, (s0, s1, s2), (r0, r1, r2))):
            cp = pltpu.make_async_remote_copy(src_ref=src_ref.at[2 * ox + oy], dst_ref=got_ref.at[j], send_sem=send_sem,
                                              recv_sem=recv_sem, device_id=(ox, oy, core), device_id_type=MESH)
            cp.start()
            copies.append(cp)
        for cp in copies:
            cp.wait()

    launch()
    return got_ref[...]


def _rs_add_chips(own, got, core, layer, buf):
    nb = HALF_ROWS // SUM_ROWS
    blk = (1, SUM_ROWS, 1024)

    def body(core_ref, a_ref, g0_ref, g1_ref, g2_ref, *rest):
        o_ref = rest[-1]
        o_ref[0, 0] = ((a_ref[...] + g0_ref[0].astype(F32)) + g1_ref[0].astype(F32)) + g2_ref[0].astype(F32)

    aliased = [] if buf is None else [buf]
    return pl.pallas_call(
        body,
        grid_spec=pltpu.PrefetchScalarGridSpec(
            num_scalar_prefetch=1, grid=(nb,),
            in_specs=[pl.BlockSpec((SUM_ROWS, 1024), lambda i, c: (i, 0))]
            + [pl.BlockSpec(blk, functools.partial(lambda j, i, c: (j, i, 0), j)) for j in range(3)]
            + [pl.BlockSpec(memory_space=pl.ANY)] * len(aliased),
            out_specs=pl.BlockSpec((1, 1, SUM_ROWS, 1024), lambda i, c: (layer, c[0], i, 0))),
        out_shape=jax.ShapeDtypeStruct((DEPTH, 2, HALF_ROWS, 1024), F32),
        input_output_aliases={} if buf is None else {5: 0},
        name="rs_add_chips", compiler_params=_params(("arbitrary",)))(core.reshape(1), own, got, got, got, *aliased)


def _rs_share(buf):
    hbm = pl.BlockSpec(memory_space=pltpu.HBM)

    def body(src, out, send_sem, recv_sem):
        ix, iy, core, _ = _mesh_pos()
        layers = pl.ds(0, DEPTH)
        cp = pltpu.make_async_remote_copy(src_ref=src.at[layers, core], dst_ref=out.at[layers, core], send_sem=send_sem,
                                          recv_sem=recv_sem, device_id=(ix, iy, 1 - core), device_id_type=MESH)
        cp.start()
        pltpu.make_async_remote_copy(src_ref=src.at[layers, 1 - core], dst_ref=out.at[layers, 1 - core],
                                     send_sem=send_sem, recv_sem=recv_sem, device_id=(ix, iy, core),
                                     device_id_type=MESH).wait_recv()
        cp.wait_send()

    return pl.pallas_call(
        body, in_specs=[hbm], out_specs=hbm, out_shape=jax.ShapeDtypeStruct(buf.shape, buf.dtype),
        input_output_aliases={0: 0}, scratch_shapes=[pltpu.SemaphoreType.DMA, pltpu.SemaphoreType.DMA],
        name="rs_share", compiler_params=pltpu.CompilerParams(has_side_effects=True))(buf)


def _scatter_start(per_chip, chip, core, layer):
    own, pair = _rs_add_pair(per_chip, _rs_sibling(per_chip), chip, core)
    return own, _rs_chips(pair, SCATTER_IDS[layer])


def _allreduce_small(tag, buf):
    for name, flip in (("c", FLIP_C), ("y", FLIP_Y), ("x", FLIP_X)):
        (other,) = _swap(tag + "_swap_" + name, [buf], [flip])
        (buf,) = _ew(tag + "_add_" + name, _add2_fn, [buf, other], 1)
    return buf


def _permute_in_cols(w):
    z = jnp.zeros(w.shape[:-1] + (D_IN_PAD - D_IN,), w.dtype)
    return jnp.concatenate([w[..., 1024:3072], w[..., :1024], w[..., 3080:D_IN], w[..., 3072:3080], z], axis=-1)


IN_COL_SEGMENTS = ((0, 1024, 2048), (1024, 3072, -1024), (3072, 3080, GATE0 - 3072), (3080, D_IN, COL_CX - 3080))


def _orig_cols(w_pad, lo, hi):
    parts = []
    for first, last, shift in IN_COL_SEGMENTS:
        a, b = max(lo, first), min(hi, last)
        if a < b:
            parts.append(w_pad[..., a + shift:b + shift])
    return parts


def _block_diag(w):
    n, d = w.shape[0], w.shape[1]
    on_diag = jnp.arange(n)[:, None] == jnp.arange(n)[None, :]
    return jnp.where(on_diag[:, None, :, None], w[:, :, None, :], 0.0).reshape(n * d, n * d)


def _diag_blocks(w):
    return jnp.stack([w[64 * n:64 * (n + 1), 64 * n:64 * (n + 1)] for n in range(4)], axis=0)


def _gate_row(v):
    return jnp.concatenate([jnp.zeros((4,), F32), v, jnp.zeros((120,), F32)]).reshape(1, 128)


SMALL = ("norm1_g", "hgrn_lb_logits", "hgrn_norm_g", "gdn_conv_w", "gdn_a_log", "gdn_dt_bias", "gdn_norm_g",
         "lru_conv_w", "lru_conv_b", "lru_w_a", "lru_b_a", "lru_w_x", "lru_b_x", "lru_lambda", "norm2_g",
         "final_norm_g")


def _flatten_small(tree):
    flat = jnp.concatenate([tree[k].reshape(-1) for k in sorted(tree)])
    rows = -(-flat.shape[0] // (512 * 128)) * 512
    return jnp.pad(flat, (0, rows * 128 - flat.shape[0])).reshape(rows, 128)


def _unflatten_small(buf, shapes):
    flat = buf.reshape(-1)
    out, off = {}, 0
    for k in sorted(shapes):
        size = 1
        for d in shapes[k]:
            size *= d
        out[k] = flat[off:off + size].reshape(shapes[k])
        off += size
    return out


def _layer_fwd(l, x, p, w_in, late):
    sv = {"x0": x}
    (h,) = _rowwise_fwd("norm1", _rmsnorm_fn, [p["norm1_g"]], [x], [D_MODEL], [BF16], 512)
    proj, w_out, w_up, w_down = late(_matmul("proj_in", h, w_in, "nn"))
    sv["h"], sv["proj"] = h, proj
    aq, af, ag = _cols(proj, COL_AQ, A_W), _cols(proj, COL_AF, A_W), _cols(proj, COL_AG, A_W)
    ai = lax.slice_in_dim(proj, COL_AI, COL_AI + A_W, axis=1)
    bz, gate, cy = _cols(proj, COL_BZ, B_W), _cols(proj, GATE0, 128), _cols(proj, COL_CY, C_W)
    q, k, lf = _rowwise_fwd("hgrn_pre", functools.partial(_hgrn_pre_fn, l), [p["lb_logits"]], [aq, af],
                            [A_W] * 3, [F32] * 3, 512)
    o_a, sts_a = _hgrn_fwd("hgrn_chunk", q, k, ai, lf)
    (y_a,) = _rowwise_fwd("hgrn_post", _hgrn_post_fn, [p["hgrn_norm_g"]], [o_a, ag], [A_W], [BF16], 512)
    sv.update(aq=aq, af=af, ai=ai, ag=ag, hq=q, hk=k, hlf=lf, o_a=o_a, sts_a=sts_a)
    qkvc = _conv_fwd("gdn_conv", proj, 0, 3 * B_W, p["gdn_conv_w"], jnp.zeros((1, 3 * B_W), F32), 256)
    gq, gk, gv, gb, gla = _rowwise_fwd("gdn_pre", _gdn_pre_fn, [p["gdn_a_log"], p["gdn_dt_bias"]], [qkvc, gate],
                                       [B_W] * 5, [F32] * 5, 256)
    *wy, sv["gdn_inv"] = _gdn_prep_fwd("gdn_prep", gq, gk, gv, gb, gla)
    o_b, sts_b = _gdn_rec_fwd("gdn_rec", *wy)
    sv["wy"] = wy
    (y_b,) = _rowwise_fwd("gdn_post", _gdn_post_fn, [p["gdn_norm_g"]], [o_b, bz], [B_W], [BF16], 512)
    sv.update(qkvc=qkvc, gate=gate, bz=bz, gq=gq, gk=gk, gv=gv, gb=gb, gla=gla, o_b=o_b, sts_b=sts_b)
    xc = _conv_fwd("lru_conv", proj, COL_CX, C_W, p["lru_conv_w"], p["lru_conv_b"], 512)
    a, b = _rowwise_fwd("lru_pre", _lru_pre_fn, [p["lru_w_a"], p["lru_b_a"], p["lru_w_x"], p["lru_b_x"], p["lru_lambda"]],
                        [xc], [C_W] * 2, [F32] * 2, 512)
    hs, h_prev = _lru_scan_fwd("lru_scan", a, b)
    (y_c,) = _rowwise_fwd("lru_post", _lru_post_fn, [], [hs, cy], [C_W], [BF16], 512)
    sv.update(xc=xc, la=a, hs=hs, h_prev=h_prev, cy=cy)
    mixed = jnp.concatenate([y_b, y_a, y_c], axis=1)
    x1 = _matmul("proj_out", mixed, w_out, "nn", res=x)
    (h2,) = _rowwise_fwd("norm2", _rmsnorm_fn, [p["norm2_g"]], [x1], [D_MODEL], [BF16], 512)
    up, act = _matmul("mlp_up", h2, w_up, "nn", also=_relu2)
    x2 = _matmul("mlp_down", act, w_down, "nn", res=x1)
    sv.update(mixed=mixed, x1=x1, h2=h2, up=up, act=act)
    return x2, sv, (w_in, w_out, w_up, w_down)


def _layer_bwd(l, dx, dx16, p, sv, w_in, w_out, w_up, w_down):
    gs = {}
    dup = _matmul("mlp_down_dx", dx16, w_down, "nt", epi=_drelu2, epi_in=sv["up"], out_dtype=BF16)
    d_w_down = _matmul("mlp_down_dw", sv["act"], dx16, "tn")
    dh2 = _matmul("mlp_up_dx", dup, w_up, "nt")
    d_w_up = _matmul("mlp_up_dw", sv["h2"], dup, "tn")
    (gs["norm2_g"],), (dx1, dx1_16) = _rowwise_bwd("norm2_bwd", _rmsnorm_fn, [p["norm2_g"]], [sv["x1"]], [dh2], 512,
                                                   res=dx, copy16=True)
    dmixed = _matmul("proj_out_dx", dx1_16, w_out, "nt")
    d_w_out = _matmul("proj_out_dw", sv["mixed"], dx1_16, "tn")
    dy_b, dy_a, dy_c = _cols(dmixed, 0, B_W), _cols(dmixed, B_W, A_W), _cols(dmixed, B_W + A_W, C_W)
    (gs["hgrn_norm_g"],), (do_a, dproj) = _rowwise_bwd("hgrn_post_bwd", _hgrn_post_fn, [p["hgrn_norm_g"]],
                                                       [sv["o_a"], sv["ag"]], [dy_a], 512, pack=([1], COL_AG, A_W, None))
    dq, dk, dproj, dlf = _hgrn_bwd("hgrn_chunk_bwd", sv["hq"], sv["hk"], sv["ai"], sv["hlf"], sv["sts_a"], do_a,
                                   dproj, COL_AI)
    (gs["lb_logits"],), (dproj,) = _rowwise_bwd("hgrn_pre_bwd", functools.partial(_hgrn_pre_fn, l), [p["lb_logits"]],
                                                [sv["aq"], sv["af"]], [dq, dk, dlf], 512,
                                                pack=([0, 1], COL_AQ, 2 * A_W, dproj))
    (gs["gdn_norm_g"],), (do_b, dproj) = _rowwise_bwd("gdn_post_bwd", _gdn_post_fn, [p["gdn_norm_g"]],
                                                      [sv["o_b"], sv["bz"]], [dy_b], 512, pack=([1], COL_BZ, B_W, dproj))
    d_wy = _gdn_rec_bwd("gdn_rec_bwd", *sv["wy"], sv["sts_b"], do_b)
    dgq, dgk, dgv, dgb, dgla = _gdn_prep_bwd("gdn_prep_bwd", sv["gq"], sv["gk"], sv["gv"], sv["gb"], sv["gla"],
                                             sv["gdn_inv"], d_wy)
    (gs["gdn_a_log"], gs["gdn_dt_bias"]), (dqkvc, dproj) = _rowwise_bwd(
        "gdn_pre_bwd", _gdn_pre_fn, [p["gdn_a_log"], p["gdn_dt_bias"]], [sv["qkvc"], sv["gate"]],
        [dgq, dgk, dgv, dgb, dgla], 256, pack=([1], GATE0, D_IN_PAD - GATE0, dproj))
    dproj, gs["gdn_conv_w"], _ = _conv_bwd("gdn_conv_bwd", dqkvc, sv["proj"], 0, 3 * B_W, p["gdn_conv_w"], dproj, 256)
    _, (dhs, dproj) = _rowwise_bwd("lru_post_bwd", _lru_post_fn, [], [sv["hs"], sv["cy"]], [dy_c], 512,
                                   pack=([1], COL_CY, C_W, dproj))
    g = _lru_scan_bwd("lru_scan_bwd", sv["la"], dhs)
    (da,) = _rowwise_fwd("lru_da", _lru_da_fn, [], [g, sv["h_prev"]], [C_W], [F32], 512)
    lru_params = [p["lru_w_a"], p["lru_b_a"], p["lru_w_x"], p["lru_b_x"], p["lru_lambda"]]
    dps, (dxc,) = _rowwise_bwd("lru_pre_bwd", _lru_pre_fn, lru_params, [sv["xc"]], [da, g], 512)
    gs["lru_w_a"], gs["lru_b_a"], gs["lru_w_x"], gs["lru_b_x"], gs["lru_lambda"] = dps
    dproj, gs["lru_conv_w"], gs["lru_conv_b"] = _conv_bwd("lru_conv_bwd", dxc, sv["proj"], COL_CX, C_W, p["lru_conv_w"],
                                                          dproj, 512)
    dh = _matmul("proj_in_dx", dproj, w_in, "nt")
    d_w_in = _matmul("proj_in_dw", sv["h"], dproj, "tn")
    (gs["norm1_g"],), (dx0, dx0_16) = _rowwise_bwd("norm1_bwd", _rmsnorm_fn, [p["norm1_g"]], [sv["x0"]], [dh], 512,
                                                   res=dx1, copy16=True)
    return dx0, dx0_16, dict(w_in=d_w_in, w_out=d_w_out, w_up=d_w_up, w_down=d_w_down), gs


def _layer_params(l, lb_logits, hgrn_norm_g, gdn_conv_w, gdn_a_log, gdn_dt_bias, gdn_norm_g, lru_conv_w, lru_conv_b,
                  lru_w_a, lru_b_a, lru_w_x, lru_b_x, lru_lambda, norm1_g, norm2_g):
    row = lambda v: v.reshape(1, -1)
    return dict(
        norm1_g=row(norm1_g[l]), norm2_g=row(norm2_g[l]), lb_logits=lb_logits,
        hgrn_norm_g=row(jnp.tile(hgrn_norm_g[l], A_W // A_HD)),
        gdn_conv_w=gdn_conv_w[l], gdn_a_log=_gate_row(gdn_a_log[l]), gdn_dt_bias=_gate_row(gdn_dt_bias[l]),
        gdn_norm_g=row(jnp.tile(gdn_norm_g[l], B_NH)),
        lru_conv_w=lru_conv_w[l], lru_conv_b=row(lru_conv_b[l]), lru_w_a=_block_diag(lru_w_a[l]), lru_b_a=row(lru_b_a[l]),
        lru_w_x=_block_diag(lru_w_x[l]), lru_b_x=row(lru_b_x[l]), lru_lambda=row(lru_lambda[l]))


def _small_grads_to_reference(gs_layers, d_final_g, d_logits):
    st = lambda k, f=lambda z: z: jnp.stack([f(g[k]) for g in gs_layers], axis=0)
    vec = lambda z: z.reshape(-1)
    return dict(
        norm1_g=st("norm1_g", vec), norm2_g=st("norm2_g", vec), hgrn_lb_logits=d_logits,
        hgrn_norm_g=st("hgrn_norm_g", lambda z: z.reshape(A_W // A_HD, A_HD).sum(0)),
        gdn_conv_w=st("gdn_conv_w"), gdn_a_log=st("gdn_a_log", lambda z: z[0, 4:8]),
        gdn_dt_bias=st("gdn_dt_bias", lambda z: z[0, 4:8]),
        gdn_norm_g=st("gdn_norm_g", lambda z: z.reshape(B_NH, B_HD).sum(0)),
        lru_conv_w=st("lru_conv_w"), lru_conv_b=st("lru_conv_b", vec), lru_w_a=st("lru_w_a", _diag_blocks),
        lru_b_a=st("lru_b_a", vec), lru_w_x=st("lru_w_x", _diag_blocks), lru_b_x=st("lru_b_x", vec),
        lru_lambda=st("lru_lambda", vec), final_norm_g=d_final_g.reshape(-1))


def _local_step(x, tgt, small, layer_weights, on_layer_grads=None):
    layer_p = [_layer_params(l, small["hgrn_lb_logits"], small["hgrn_norm_g"], small["gdn_conv_w"], small["gdn_a_log"],
                             small["gdn_dt_bias"], small["gdn_norm_g"], small["lru_conv_w"], small["lru_conv_b"],
                             small["lru_w_a"], small["lru_b_a"], small["lru_w_x"], small["lru_b_x"], small["lru_lambda"],
                             small["norm1_g"], small["norm2_g"]) for l in range(DEPTH)]
    saved, weights = [], []
    for l in range(DEPTH):
        x, w_in, late = layer_weights(l, x)
        x, sv, w = _layer_fwd(l, x, layer_p[l], w_in, late)
        saved.append(sv)
        weights.append(w)
    loss, dx, d_final_g, dx16 = _loss_and_grad("final_loss", small["final_norm_g"].reshape(1, -1), x, tgt)
    big = [None] * DEPTH
    gs_layers = [None] * DEPTH
    d_logits = jnp.zeros_like(small["hgrn_lb_logits"])
    for l in reversed(range(DEPTH)):
        dx, dx16, big[l], gs_layers[l] = _layer_bwd(l, dx, dx16, layer_p[l], saved[l], *weights[l])
        d_logits = d_logits + gs_layers[l]["lb_logits"]
        if on_layer_grads is not None:
            big[l] = on_layer_grads(l, big[l])
    return loss, dx, big, _small_grads_to_reference(gs_layers, d_final_g, d_logits)


def kernel(x, norm1_g, w_in, hgrn_lb_logits, hgrn_norm_g, gdn_conv_w, gdn_a_log, gdn_dt_bias, gdn_norm_g, lru_conv_w, lru_conv_b, lru_w_a, lru_b_a, lru_w_x, lru_b_x, lru_lambda, w_out, norm2_g, w_up, w_down, final_norm_g, loss_target, m_norm1_g, m_w_in, m_hgrn_lb_logits, m_hgrn_norm_g, m_gdn_conv_w, m_gdn_a_log, m_gdn_dt_bias, m_gdn_norm_g, m_lru_conv_w, m_lru_conv_b, m_lru_w_a, m_lru_b_a, m_lru_w_x, m_lru_b_x, m_lru_lambda, m_w_out, m_norm2_g, m_w_up, m_w_down, m_final_norm_g, v_norm1_g, v_w_in, v_hgrn_lb_logits, v_hgrn_norm_g, v_gdn_conv_w, v_gdn_a_log, v_gdn_dt_bias, v_gdn_norm_g, v_lru_conv_w, v_lru_conv_b, v_lru_w_a, v_lru_b_a, v_lru_w_x, v_lru_b_x, v_lru_lambda, v_w_out, v_norm2_g, v_w_up, v_w_down, v_final_norm_g):
    args = dict(locals())
    ix, iy, core = lax.axis_index("x"), lax.axis_index("y"), lax.axis_index("c")
    chip = 2 * ix + iy

    packed = _pack_big(w_in, w_out, w_up, w_down).astype(BF16)
    first_w_in = _gather_layer(packed[0, :1024], FIRST_GATHER_ID)
    gathered = [_gather_layer(packed[l, 1024 * (l == 0):], GATHER_IDS[l]) for l in range(DEPTH)]

    def with_own(got, own):
        return lax.dynamic_update_slice(got, own[None], (chip, 0, 0))

    def full_w_in(shares):
        return _permute_in_cols(jnp.concatenate([shares[k, :1024, :D_IN // 4] for k in range(4)], axis=-1))

    def full_rest(shares):
        return (jnp.concatenate([shares[k, 2048:] for k in W_OUT_CHIP_ORDER], axis=0),
                jnp.concatenate([shares[k, :1024] for k in range(4)], axis=-1),
                jnp.concatenate([shares[k, 1024:2048] for k in range(4)], axis=0))

    def layer_weights(l, x):
        got, own = gathered[l], packed[l]
        if l:
            got, own, x = lax.optimization_barrier((got, own, x))
            shares = with_own(got, own)
            rest = full_rest(shares[:, 1024:])
            return x, full_w_in(shares), lambda v: (v,) + rest

        def late(v):
            got_v, own_v, v = lax.optimization_barrier((got, own, v))
            return (v,) + full_rest(with_own(got_v, own_v[1024:]))

        return x, full_w_in(with_own(first_w_in, own[:1024])), late

    n_gc, n_lc = gdn_conv_w.shape[-1], lru_conv_w.shape[-1]
    conv_full = dict(
        gdn_conv_w=lax.dynamic_update_slice(jnp.zeros((DEPTH, 4, 4 * n_gc), F32), gdn_conv_w * 0.5, (0, 0, chip * n_gc)),
        lru_conv_w=lax.dynamic_update_slice(jnp.zeros((DEPTH, 4, 4 * n_lc), F32), lru_conv_w * 0.5, (0, 0, chip * n_lc)))
    conv_shapes = {k: v.shape for k, v in conv_full.items()}
    conv_full = _unflatten_small(_allreduce_small("convw", _flatten_small(conv_full)), conv_shapes)

    small = {k: args[k] for k in SMALL}
    small.update(conv_full)

    n_in = D_IN // 4

    def start_scatter(l, b):
        pieces = []
        for k in range(4):
            cols = jnp.concatenate(_orig_cols(b["w_in"], k * n_in, (k + 1) * n_in), axis=1)
            r0 = 256 * W_OUT_CHIP_ORDER.index(k)
            pieces += [jnp.pad(cols, ((0, 0), (0, 1024 - n_in))), b["w_up"][:, k * 1024:(k + 1) * 1024],
                       b["w_down"][k * 1024:(k + 1) * 1024], b["w_out"][r0:r0 + 256]]
        per_chip = jnp.concatenate(pieces, axis=0).reshape(4, LAYER_ROWS, 1024)
        return _scatter_start(per_chip, chip, core, l)

    loss, grad_x, started, sg = _local_step(x[0], loss_target[0], small, layer_weights, start_scatter)

    sg["loss"] = loss[0, :1]
    shapes = {k: v.shape for k, v in sg.items()}
    sg = _unflatten_small(_allreduce_small("smallgrad", _flatten_small(sg)), shapes)
    loss_out = sg.pop("loss")[0]

    buf = None
    for l in reversed(range(DEPTH)):
        own, got = started[l]
        buf = _rs_add_chips(own, got, core, l, buf)
    g_in, g_out, g_up, g_down = _unpack_big(_rs_share(buf).reshape(DEPTH, LAYER_ROWS, 1024))

    grads = dict(sg)
    grads["gdn_conv_w"] = lax.dynamic_slice_in_dim(sg["gdn_conv_w"], chip * n_gc, n_gc, axis=2)
    grads["lru_conv_w"] = lax.dynamic_slice_in_dim(sg["lru_conv_w"], chip * n_lc, n_lc, axis=2)
    grads.update(w_in=g_in, w_out=g_out, w_up=g_up, w_down=g_down)

    names = ['norm1_g', 'w_in', 'hgrn_lb_logits', 'hgrn_norm_g', 'gdn_conv_w', 'gdn_a_log', 'gdn_dt_bias', 'gdn_norm_g',
             'lru_conv_w', 'lru_conv_b', 'lru_w_a', 'lru_b_a', 'lru_w_x', 'lru_b_x', 'lru_lambda', 'w_out', 'norm2_g',
             'w_up', 'w_down', 'final_norm_g']
    big_names = ("w_in", "w_out", "w_up", "w_down")
    delta, new_m, new_v = {}, {}, {}
    for k in big_names:
        delta[k], new_m[k], new_v[k] = _adam("adam_" + k, args[k], grads[k], args["m_" + k], args["v_" + k])
    small_names = [k for k in names if k not in big_names]
    shapes = {k: args[k].shape for k in small_names}
    flat = [_flatten_small({k: src(k) for k in small_names})
            for src in (lambda k: args[k], lambda k: grads[k], lambda k: args["m_" + k], lambda k: args["v_" + k])]
    outs = _ew("adam_small", _adam_fn, flat, 3)
    for dst, o in zip((delta, new_m, new_v), outs):
        dst.update(_unflatten_small(o, shapes))
    return (loss_out, grad_x[None], *[grads[k] for k in names], *[delta[k] for k in names],
            *[new_m[k] for k in names], *[new_v[k] for k in names])
```

```python
import functools

import jax
import jax.numpy as jnp
from jax import lax
from jax.experimental import pallas as pl
from jax.experimental.pallas import tpu as pltpu
from jax.experimental.pallas import tpu_sc as plsc

F32 = jnp.float32
BF16 = jnp.bfloat16
MESH = pl.DeviceIdType.MESH

DEPTH = 4
D_MODEL = 1024
A_W, B_W, C_W = 256, 512, 256
A_HD, B_HD = 64, 128
B_NH = B_W // B_HD
D_IN = 3592
D_IN_PAD = 3840
COL_BZ, COL_AQ, COL_AF, COL_AI, COL_AG, COL_CX, COL_CY = 1536, 2048, 2304, 2560, 2816, 3072, 3328
W_OUT_CHIP_ORDER = (1, 2, 0, 3)
GATE0 = 3584
D_FF = 4096
CHUNK = 64
SUB = 16
EPS = 1e-6
TINY = 1e-30
MASKED_EXPONENT = -1e30
RG_C = 8.0
LR, B1, B2, AEPS, WD, STEP = 0.001, 0.9, 0.999, 1e-08, 0.01, 10
VMEM_LIMIT = 56 * 1024 * 1024
PACK_ROWS = 13 * 1024

NN = ((1,), (0,))
NT = ((1,), (1,))
TN = ((0,), (0,))


DIMS = {"nn": NN, "nt": NT, "tn": TN}


def _split(a, n):
    parts, r = [], a
    for i in range(n):
        p = r.astype(BF16)
        parts.append(p)
        if i + 1 < n:
            r = r - p.astype(F32)
    return parts


def _split_dot(a, b, mode, na, nb):
    (ca,), (cb,) = DIMS[mode]
    dims = ((DIMS[mode]), ((), ())) if a.ndim == 2 else (((ca + 1,), (cb + 1,)), ((0,), (0,)))
    pa, pb = _split(a, na), _split(b, nb)
    acc = None
    for i in range(na):
        for j in range(nb):
            if i + j < max(na, nb):
                t = lax.dot_general(pa[i], pb[j], dims, preferred_element_type=F32)
                acc = t if acc is None else acc + t
    return acc


@functools.partial(jax.custom_vjp, nondiff_argnums=(2, 3, 4, 5))
def _mdot(a, b, mode, na, nb, need):
    return _split_dot(a, b, mode, na, nb)


def _mdot_fwd(a, b, mode, na, nb, need):
    return _split_dot(a, b, mode, na, nb), (a, b)


def _mdot_bwd(mode, na, nb, need, res, ct):
    a, b = res
    n = max(na, nb)
    da, db = jnp.zeros_like(a), jnp.zeros_like(b)
    if mode == "nn":
        if need[0]:
            da = _split_dot(ct, b, "nt", n, nb)
        if need[1]:
            db = _split_dot(a, ct, "tn", na, n)
    elif mode == "nt":
        if need[0]:
            da = _split_dot(ct, b, "nn", n, nb)
        if need[1]:
            db = _split_dot(ct, a, "tn", n, na)
    else:
        if need[0]:
            da = _split_dot(b, ct, "nt", nb, n)
        if need[1]:
            db = _split_dot(a, ct, "nn", na, n)
    return da, db


_mdot.defvjp(_mdot_fwd, _mdot_bwd)
BOTH = (True, True)


def _bdot(a, b, mode="nn"):
    return _mdot(a, b, mode, 1, 1, BOTH)


def _group_sum(x, ones):
    return _mdot(x, ones, "nn", 2, 1, (True, False))


def _running_sum(x):
    shape = x.shape[:-2] + (CHUNK, CHUNK)
    tri = (_iota(shape, len(shape) - 1) <= _iota(shape, len(shape) - 2)).astype(F32)
    return _mdot(tri, x, "nn", 1, 3, (False, True))


def _iota(shape, d):
    return lax.broadcasted_iota(jnp.int32, shape, d)


def _block_ones(width, group):
    return (_iota((width, width), 0) // group == _iota((width, width), 1) // group).astype(F32)


def _params(sem):
    return pltpu.CompilerParams(dimension_semantics=sem, vmem_limit_bytes=VMEM_LIMIT)


def _tile(dim, prefs):
    for p in prefs:
        if dim % p == 0:
            return p
    return dim


def _whole(shape):
    return pl.BlockSpec(shape, lambda i: (0,) * len(shape))


def _rowspec(t, w, col0=0):
    cb = col0 // w
    return pl.BlockSpec((t, w), lambda i: (i, cb))


def _cols(arr, col0, width):
    assert col0 % width == 0
    return (arr, col0, width)


def _row_operands(rows, t):
    arrs, specs = [], []
    for r in rows:
        arr, col0, w = r if isinstance(r, tuple) else (r, 0, r.shape[1])
        arrs.append(arr)
        specs.append(_rowspec(t, w, col0))
    return arrs, specs


def _rowwise_fwd(name, fn, params, rows, out_widths, out_dtypes, t):
    rows, row_specs = _row_operands(rows, t)
    s = rows[0].shape[0]
    n_p, n_r = len(params), len(rows)

    def body(*refs):
        p = [r[...] for r in refs[:n_p]]
        xs = [r[...].astype(F32) for r in refs[n_p:n_p + n_r]]
        outs = fn(pl.program_id(0) * t, *p, *xs)
        for o_ref, o in zip(refs[n_p + n_r:], outs):
            o_ref[...] = o.astype(o_ref.dtype)

    return pl.pallas_call(
        body, grid=(s // t,),
        in_specs=[_whole(p.shape) for p in params] + row_specs,
        out_specs=[_rowspec(t, w) for w in out_widths],
        out_shape=[jax.ShapeDtypeStruct((s, w), d) for w, d in zip(out_widths, out_dtypes)],
        name=name, compiler_params=_params(("arbitrary",)))(*params, *rows)


def _into_spec(t, width, col0, rev_steps=None):
    cb = col0 // width
    assert col0 % width == 0
    if rev_steps is None:
        return pl.BlockSpec((t, width), lambda i: (i, cb))
    return pl.BlockSpec((t, width), lambda i: (rev_steps - 1 - i, cb))


def _rowwise_bwd(name, fn, params, rows, douts, t, need=None, res=None, dtypes=None, copy16=False, pack=None):
    rows, row_specs = _row_operands(rows, t)
    douts, dout_specs = _row_operands(douts, t)
    s = rows[0].shape[0]
    n_p, n_r, n_o = len(params), len(rows), len(douts)
    need = [True] * n_r if need is None else need
    n_res = 0 if res is None else 1
    n_buf = 0 if pack is None or pack[3] is None else 1

    def body(*refs):
        p = [r[...] for r in refs[:n_p]]
        xs = [r[...].astype(F32) for r in refs[n_p:n_p + n_r]]
        dys = [r[...].astype(F32) for r in refs[n_p + n_r:n_p + n_r + n_o]]
        k = n_p + n_r + n_o
        res_ref = refs[k] if n_res else None
        k += n_res + n_buf
        dp_refs = refs[k:k + n_p]
        dx_refs = list(refs[k + n_p:])
        row0 = pl.program_id(0) * t
        _, vjp = jax.vjp(lambda *a: tuple(fn(row0, *a)), *p, *xs)
        g = vjp(tuple(dys))

        @pl.when(pl.program_id(0) == 0)
        def _():
            for r in dp_refs:
                r[...] = jnp.zeros_like(r)

        for r, gp in zip(dp_refs, g[:n_p]):
            r[...] += gp
        gx = [gi for gi, nd in zip(g[n_p:], need) if nd]
        if n_res:
            gx[0] = gx[0] + res_ref[...]
        if copy16:
            gx.append(gx[0])
        if pack is not None:
            pieces = [gx[j].astype(BF16) for j in pack[0]]
            missing = pack[2] - sum(pc.shape[1] for pc in pieces)
            if missing:
                pieces.append(jnp.zeros((t, missing), BF16))
            dx_refs.pop()[...] = pieces[0] if len(pieces) == 1 else jnp.concatenate(pieces, axis=1)
            gx = [gi for j, gi in enumerate(gx) if j not in pack[0]]
        for r, gi in zip(dx_refs, gx):
            r[...] = gi.astype(r.dtype)

    widths = [sp.block_shape[1] for sp, nd in zip(row_specs, need) if nd]
    dtypes = [F32] * len(widths) if dtypes is None else list(dtypes)
    if copy16:
        widths, dtypes = widths + widths[:1], dtypes + [BF16]
    out_specs = [_rowspec(t, w) for w in widths]
    out_shape = [jax.ShapeDtypeStruct((s, w), d) for w, d in zip(widths, dtypes)]
    extra, aliases = [], {}
    if pack is not None:
        keep = [j for j in range(len(widths)) if j not in pack[0]]
        out_specs = [out_specs[j] for j in keep] + [_into_spec(t, pack[2], pack[1])]
        out_shape = [out_shape[j] for j in keep] + [jax.ShapeDtypeStruct((s, D_IN_PAD), BF16)]
        if n_buf:
            extra = [pack[3]]
            aliases = {n_p + n_r + n_o + n_res: n_p + len(out_shape) - 1}
    outs = pl.pallas_call(
        body, grid=(s // t,),
        in_specs=([_whole(p.shape) for p in params] + row_specs + dout_specs
                  + ([_rowspec(t, res.shape[1])] if n_res else []) + [pl.BlockSpec(memory_space=pl.ANY)] * n_buf),
        out_specs=[_whole(p.shape) for p in params] + out_specs,
        out_shape=[jax.ShapeDtypeStruct(p.shape, F32) for p in params] + out_shape, input_output_aliases=aliases,
        name=name, compiler_params=_params(("arbitrary",)))(*params, *rows, *douts, *([res] if n_res else []), *extra)
    return list(outs[:n_p]), list(outs[n_p:])


def _rmsnorm_fn(row0, g, x):
    y = x * lax.rsqrt(jnp.mean(x * x, axis=-1, keepdims=True) + EPS)
    return (y * g,)


def _hgrn_pre_fn(layer, row0, logits, aq, af):
    m = jnp.max(logits, axis=0, keepdims=True)
    e = jnp.exp(logits - m)
    p = e / jnp.sum(e, axis=0, keepdims=True)
    if layer == 0:
        lb = jnp.zeros((1, A_W), F32)
    else:
        acc = p[1:2]
        for j in range(2, layer + 1):
            acc = acc + p[j:j + 1]
        lb = jnp.minimum(jnp.maximum(acc, 0.0), 1.0 - EPS)
    sig = jax.nn.sigmoid(af)
    f = lb + (1.0 - lb) * sig
    log_f = jnp.log(jnp.maximum(f, TINY))
    k = (1.0 - lb) * jax.nn.sigmoid(-af)
    return jax.nn.silu(aq), k, log_f


def _hgrn_post_fn(row0, g, o, ag):
    ms = _group_sum(o * o, _block_ones(A_W, A_HD)) * (1.0 / A_HD)
    return (o * lax.rsqrt(ms + EPS) * g * jax.nn.silu(ag),)


def _gdn_pre_fn(row0, a_log, dt_bias, qkvc, gate):
    act = jax.nn.silu(qkvc)
    q, k, v = act[:, :B_W], act[:, B_W:2 * B_W], act[:, 2 * B_W:]
    bo = _block_ones(B_W, B_HD)
    q = q * lax.rsqrt(_group_sum(q * q, bo) + EPS) * (B_HD ** -0.5)
    k = k * lax.rsqrt(_group_sum(k * k, bo) + EPS)
    beta = jax.nn.sigmoid(gate)
    log_alpha = -jnp.exp(a_log) * jax.nn.softplus(gate + dt_bias)
    t = gate.shape[0]
    spread = lambda z, lane0: jnp.concatenate(
        [jnp.broadcast_to(z[:, lane0 + h:lane0 + h + 1], (t, B_HD)) for h in range(B_NH)], axis=1)
    return q, k, v, spread(beta, 0), spread(log_alpha, B_NH)


def _gdn_post_fn(row0, g, o, bz):
    ms = _group_sum(o * o, _block_ones(B_W, B_HD)) * (1.0 / B_HD)
    return (o * lax.rsqrt(ms + EPS) * g * jax.nn.silu(bz),)


def _lru_pre_fn(row0, w_a, b_a, w_x, b_x, lam, xc):
    r = jax.nn.sigmoid(_mdot(xc, w_a, "nn", 2, 2, BOTH) + b_a)
    i = jax.nn.sigmoid(_mdot(xc, w_x, "nn", 2, 2, BOTH) + b_x)
    log_a = -RG_C * r * jax.nn.softplus(-lam)
    a = jnp.exp(log_a)
    one_minus = -jnp.tanh(log_a) * (a * a + 1.0)
    mult = jnp.sqrt(jnp.maximum(one_minus, EPS))
    pos = row0 + _iota(xc.shape, 0)
    mult = jnp.where(pos == 0, 1.0, mult)
    return a, mult * i * xc


def _lru_post_fn(row0, h, cy):
    return (jax.nn.gelu(cy, approximate=True) * h,)


def _lru_da_fn(row0, g, h_prev):
    return (g * h_prev,)


def _relu2(u):
    r = jnp.maximum(u, 0.0)
    return r * r


def _drelu2(acc, u):
    return acc * (2.0 * jnp.maximum(u, 0.0))


MM_VMEM_BUDGET = 44 * 1024 * 1024


def _mm_tiles(mode, m, n, k, a_bytes, b_bytes, n_extra):
    best = None
    for tk in {k, 4096, 2048, 1024, 512, 256}:
        for tn in {n, 1280, 1024, 768, 512, 256, 128}:
            for tm in {m, 1024, 512, 256, 128}:
                if m % tm or n % tn or k % tk or tm > 1024:
                    continue
                tiles = tm * tk * a_bytes + tk * tn * b_bytes + (1 + n_extra) * tm * tn * 4
                if 2 * tiles + (tm * tn * 4 if tk < k else 0) > MM_VMEM_BUDGET:
                    continue
                a_reads = 1 if tk == k else n // tn
                b_reads = 1 if (tk == k and tn == n) else m // tm
                cost = m * k * a_bytes * a_reads + k * n * b_bytes * b_reads + m * n * 4 * (k // tk - 1)
                if best is None or (cost, -tm) < best[0]:
                    best = ((cost, -tm), (tm, tn, tk))
    if best is None:
        raise ValueError("no matmul tiling fits VMEM")
    return best[1]


def _matmul(name, a, b, mode, a_fn=None, res=None, epi=None, epi_in=None, out_dtype=F32, also=None):
    if mode == "nn":
        (m, k), n = a.shape, b.shape[1]
    elif mode == "nt":
        (m, k), n = a.shape, b.shape[0]
    else:
        (k, m), n = a.shape, b.shape[1]
    n_extra = (res is not None) + (epi_in is not None) + (also is not None)
    tm, tn, tk = _mm_tiles(mode, m, n, k, a.dtype.itemsize, b.dtype.itemsize, n_extra)
    nk = k // tk
    a_spec = {"nn": pl.BlockSpec((tm, tk), lambda i, j, kk: (i, kk)),
              "nt": pl.BlockSpec((tm, tk), lambda i, j, kk: (i, kk)),
              "tn": pl.BlockSpec((tk, tm), lambda i, j, kk: (kk, i))}[mode]
    b_spec = {"nn": pl.BlockSpec((tk, tn), lambda i, j, kk: (kk, j)),
              "nt": pl.BlockSpec((tn, tk), lambda i, j, kk: (j, kk)),
              "tn": pl.BlockSpec((tk, tn), lambda i, j, kk: (kk, j))}[mode]
    o_spec = pl.BlockSpec((tm, tn), lambda i, j, kk: (i, j))
    dims = {"nn": NN, "nt": NT, "tn": TN}[mode]
    extra = [x for x in (res, epi_in) if x is not None]

    def body(*refs):
        a_ref, b_ref = refs[0], refs[1]
        rest = list(refs[2:])
        res_ref = rest.pop(0) if res is not None else None
        epi_ref = rest.pop(0) if epi_in is not None else None
        o_ref = rest.pop(0)
        also_ref = rest.pop(0) if also is not None else None
        acc = rest[0] if nk > 1 else None
        kk = pl.program_id(2)
        at = a_ref[...]
        if a_fn is not None:
            at = a_fn(at.astype(F32))
        part = lax.dot_general(at.astype(BF16), b_ref[...].astype(BF16), (dims, ((), ())),
                               preferred_element_type=F32)

        def finish(out):
            if res_ref is not None:
                out = out + res_ref[...]
            if epi is not None:
                out = epi(out, epi_ref[...])
            o_ref[...] = out.astype(o_ref.dtype)
            if also is not None:
                also_ref[...] = also(out).astype(also_ref.dtype)

        if nk == 1:
            finish(part)
        else:
            @pl.when(kk == 0)
            def _():
                acc[...] = part

            @pl.when(jnp.logical_and(kk > 0, kk < nk - 1))
            def _():
                acc[...] += part

            @pl.when(kk == nk - 1)
            def _():
                finish(acc[...] + part)

    out_shape = [jax.ShapeDtypeStruct((m, n), out_dtype)] + ([jax.ShapeDtypeStruct((m, n), BF16)] if also else [])
    outs = pl.pallas_call(
        body, grid=(m // tm, n // tn, nk),
        in_specs=[a_spec, b_spec] + [o_spec] * len(extra), out_specs=[o_spec] * len(out_shape), out_shape=out_shape,
        scratch_shapes=[pltpu.VMEM((tm, tn), F32)] if nk > 1 else [],
        name=name, compiler_params=_params(("arbitrary", "arbitrary", "arbitrary")))(a, b, *extra)
    return outs if also else outs[0]


HALO = 8


def _conv_fwd(name, x, col0, width, w, bias, t=512):
    s = x.shape[0]
    cb = col0 // width
    hb = t // HALO

    def body(x_ref, halo_ref, w_ref, b_ref, y_ref):
        i = pl.program_id(0)
        halo = jnp.where(i == 0, 0.0, halo_ref[...])
        xp = jnp.concatenate([halo, x_ref[...]], axis=0)
        wv = w_ref[...]
        y = b_ref[...] + wv[0:1] * xp[HALO - 3:HALO - 3 + t]
        for j in range(1, 4):
            y = y + wv[j:j + 1] * xp[HALO - 3 + j:HALO - 3 + j + t]
        y_ref[...] = y

    return pl.pallas_call(
        body, grid=(s // t,),
        in_specs=[pl.BlockSpec((t, width), lambda i: (i, cb)),
                  pl.BlockSpec((HALO, width), lambda i: (jnp.maximum(i * hb - 1, 0), cb)),
                  _whole(w.shape), _whole(bias.shape)],
        out_specs=_rowspec(t, width), out_shape=jax.ShapeDtypeStruct((s, width), F32),
        name=name, compiler_params=_params(("arbitrary",)))(x, x, w, bias)


def _conv_bwd(name, dy, x, col0, width, w, buf, t=512):
    s = x.shape[0]
    cb = col0 // width
    hb = t // HALO
    nblk = s // t

    def body(dy_ref, dyn_ref, x_ref, halo_ref, w_ref, buf_ref, dx_ref, dw_ref, db_ref):
        i = pl.program_id(0)
        dyv = dy_ref[...]
        nxt = jnp.where(i == nblk - 1, 0.0, dyn_ref[...])
        dyp = jnp.concatenate([dyv, nxt], axis=0)
        wv = w_ref[...]
        dx = wv[3:4] * dyv
        for j in range(3):
            dx = dx + wv[j:j + 1] * dyp[3 - j:3 - j + t]
        dx_ref[...] = dx.astype(dx_ref.dtype)
        halo = jnp.where(i == 0, 0.0, halo_ref[...])
        xp = jnp.concatenate([halo, x_ref[...]], axis=0)
        dw = jnp.concatenate(
            [jnp.sum(dyv * xp[HALO - 3 + j:HALO - 3 + j + t], axis=0, keepdims=True) for j in range(4)], axis=0)

        @pl.when(i == 0)
        def _():
            dw_ref[...] = jnp.zeros_like(dw_ref)
            db_ref[...] = jnp.zeros_like(db_ref)

        dw_ref[...] += dw
        db_ref[...] += jnp.sum(dyv, axis=0, keepdims=True)

    return pl.pallas_call(
        body, grid=(nblk,),
        in_specs=[_rowspec(t, width),
                  pl.BlockSpec((HALO, width), lambda i: (jnp.minimum((i + 1) * hb, s // HALO - 1), 0)),
                  pl.BlockSpec((t, width), lambda i: (i, cb)),
                  pl.BlockSpec((HALO, width), lambda i: (jnp.maximum(i * hb - 1, 0), cb)),
                  _whole(w.shape), pl.BlockSpec(memory_space=pl.ANY)],
        out_specs=[_into_spec(t, width, col0), _whole((4, width)), _whole((1, width))],
        out_shape=[jax.ShapeDtypeStruct(buf.shape, buf.dtype), jax.ShapeDtypeStruct((4, width), F32),
                   jax.ShapeDtypeStruct((1, width), F32)],
        input_output_aliases={5: 0},
        name=name, compiler_params=_params(("arbitrary",)))(dy, dy, x, x, w, buf)


def _hgrn_chunk(st, q, k, v, lf):
    cum = _running_sum(lf)
    tot = jnp.sum(lf, axis=0, keepdims=True)
    bm = _iota((A_W, A_W), 0) // A_HD == _iota((A_W, A_W), 1) // A_HD
    bo = bm.astype(F32)
    o_inter = _bdot(q * jnp.exp(cum), st, "nt")
    kd = k * jnp.exp(tot - cum)
    st_new = st * jnp.exp(tot) + jnp.where(bm, _bdot(v, kd, "tn"), 0.0)
    lane_h = _iota((SUB, A_W), 1) // A_HD
    n_h = A_W // A_HD
    t3 = _iota((SUB, SUB, A_W), 0)
    s3 = _iota((SUB, SUB, A_W), 1)
    outs = []
    for i in range(CHUNK // SUB):
        sl = slice(i * SUB, (i + 1) * SUB)
        qi, ki, vi, ci = q[sl], k[sl], v[sl], cum[sl]
        e = jnp.exp(jnp.where(s3 <= t3, ci[:, None, :] - ci[None, :, :], MASKED_EXPONENT))
        d3 = qi[:, None, :] * ki[None, :, :] * e
        ds = _mdot(d3.reshape(SUB * SUB, A_W), bo, "nn", 1, 1, (True, False)).reshape(SUB, SUB, A_W)
        oi = jnp.sum(ds * vi[None, :, :], axis=1)
        if i > 0:
            n = i * SUB
            cb = cum[n - 1:n]
            qt = qi * jnp.exp(ci - cb)
            kt = k[:n] * jnp.exp(cb - cum[:n])
            qs = jnp.concatenate([jnp.where(lane_h == h, qt, 0.0) for h in range(n_h)], axis=0)
            p = _bdot(_bdot(qs, kt, "nt"), v[:n])
            for h in range(n_h):
                oi = oi + jnp.where(lane_h == h, p[h * SUB:(h + 1) * SUB], 0.0)
        outs.append(oi)
    return st_new, o_inter + jnp.concatenate(outs, axis=0)


HGRN_CHUNKS = 2


def _hgrn_fwd(name, q, k, v, lf):
    s = q.shape[0]
    n = s // CHUNK
    g = HGRN_CHUNKS

    def body(q_ref, k_ref, v_ref, lf_ref, o_ref, sts_ref, st_ref):
        @pl.when(pl.program_id(0) == 0)
        def _():
            st_ref[...] = jnp.zeros_like(st_ref)

        st = st_ref[...]
        for c in range(g):
            rows = pl.ds(c * CHUNK, CHUNK)
            sts_ref[c] = st
            st, o = _hgrn_chunk(st, q_ref[rows], k_ref[rows], v_ref[rows], lf_ref[rows])
            o_ref[rows] = o
        st_ref[...] = st

    cs = _rowspec(g * CHUNK, A_W)
    return pl.pallas_call(
        body, grid=(n // g,), in_specs=[cs] * 4,
        out_specs=[cs, pl.BlockSpec((g, A_W, A_W), lambda i: (i, 0, 0))],
        out_shape=[jax.ShapeDtypeStruct((s, A_W), F32), jax.ShapeDtypeStruct((n, A_W, A_W), F32)],
        scratch_shapes=[pltpu.VMEM((A_W, A_W), F32)],
        name=name, compiler_params=_params(("arbitrary",)))(q, k, v, lf)


def _hgrn_bwd(name, q, k, v, lf, sts, do, buf, col0):
    s = q.shape[0]
    n = s // CHUNK
    g = HGRN_CHUNKS
    nsteps = n // g

    def body(q_ref, k_ref, v_ref, lf_ref, sts_ref, do_ref, buf_ref, dq_ref, dk_ref, dv_ref, dlf_ref, dst_ref):
        @pl.when(pl.program_id(0) == 0)
        def _():
            dst_ref[...] = jnp.zeros_like(dst_ref)

        dst = dst_ref[...]
        for c in reversed(range(g)):
            rows = pl.ds(c * CHUNK, CHUNK)
            _, vjp = jax.vjp(_hgrn_chunk, sts_ref[c], q_ref[rows], k_ref[rows], v_ref[rows], lf_ref[rows])
            ct = vjp((dst, do_ref[rows]))
            dst = ct[0]
            for r, gi in zip((dq_ref, dk_ref, dv_ref, dlf_ref), ct[1:]):
                r[rows] = gi.astype(r.dtype)
        dst_ref[...] = dst

    rs = pl.BlockSpec((g * CHUNK, A_W), lambda i: (nsteps - 1 - i, 0))
    rows = jax.ShapeDtypeStruct((s, A_W), F32)
    return pl.pallas_call(
        body, grid=(nsteps,),
        in_specs=[rs] * 4 + [pl.BlockSpec((g, A_W, A_W), lambda i: (nsteps - 1 - i, 0, 0)), rs,
                             pl.BlockSpec(memory_space=pl.ANY)],
        out_specs=[rs, rs, _into_spec(g * CHUNK, A_W, col0, nsteps), rs],
        out_shape=[rows, rows, jax.ShapeDtypeStruct(buf.shape, buf.dtype), rows],
        input_output_aliases={6: 2}, scratch_shapes=[pltpu.VMEM((A_W, A_W), F32)],
        name=name, compiler_params=_params(("arbitrary",)))(q, k, v, lf, sts, do, buf)


PREP_CHUNKS = 4
REC_CHUNKS = 8
DEC_ROWS = 8


def _to_heads(x, rows=CHUNK, width=B_HD):
    n = x.shape[0] // rows
    x3 = x.reshape(n, rows, x.shape[1])
    return jnp.concatenate([x3[:, :, h * B_HD:h * B_HD + width] for h in range(B_NH)], axis=0)


def _from_heads(y):
    n = y.shape[0] // B_NH
    x3 = jnp.concatenate([y[h * n:(h + 1) * n] for h in range(B_NH)], axis=2)
    return x3.reshape(n * y.shape[1], B_NH * B_HD)


@jax.custom_vjp
def _unit_lower_inverse(a_mat):
    nil = -a_mat
    eye = (_iota(a_mat.shape, 1) == _iota(a_mat.shape, 2)).astype(F32)
    inv = eye + nil
    sq = 2
    while sq < CHUNK:
        nil = _split_dot(nil, nil, "nn", 2, 2)
        inv = inv + _split_dot(inv, nil, "nn", 2, 2)
        sq *= 2
    return inv


def _unit_lower_inverse_fwd(a_mat):
    inv = _unit_lower_inverse(a_mat)
    return inv, inv


def _unit_lower_inverse_bwd(inv, ct):
    return (-_split_dot(_split_dot(inv, ct, "tn", 2, 2), inv, "nt", 2, 2),)


_unit_lower_inverse.defvjp(_unit_lower_inverse_fwd, _unit_lower_inverse_bwd)


@jax.custom_vjp
def _known_inverse(a_mat, inv):
    return inv


def _known_inverse_fwd(a_mat, inv):
    return inv, inv


def _known_inverse_bwd(inv, ct):
    return _unit_lower_inverse_bwd(inv, ct)[0], jnp.zeros_like(inv)


_known_inverse.defvjp(_known_inverse_fwd, _known_inverse_bwd)


def _gdn_prep(q, k, v, beta, la, inv_saved=None):
    q, k, v, beta, la = [_to_heads(z) for z in (q, k, v, beta, la)]
    nb = q.shape[0]
    r = _iota((nb, CHUNK, CHUNK), 1)
    c = _iota((nb, CHUNK, CHUNK), 2)
    causal = c <= r
    g = _running_sum(la)
    tot = jnp.sum(la, axis=1, keepdims=True)
    gcol = g[:, :, :CHUNK]
    decay = jnp.exp(jnp.where(causal, gcol - jnp.swapaxes(gcol, 1, 2), MASKED_EXPONENT))
    kb = k * beta
    a_mat = jnp.where(c < r, _bdot(kb, k, "nt") * decay, 0.0)
    if inv_saved is None:
        inv = _unit_lower_inverse(a_mat)
    else:
        inv = _known_inverse(a_mat, _to_heads(inv_saved, width=CHUNK))
    eg = jnp.exp(g)
    u = _bdot(inv, v * beta)
    w = _bdot(inv, kb * eg)
    qk = jnp.where(causal, _bdot(q, k, "nt") * decay, 0.0)
    widen = lambda z: jnp.concatenate([z, jnp.zeros((nb, CHUNK, B_HD - CHUNK), F32)], axis=2)
    dec = jnp.broadcast_to(jnp.exp(tot), (nb, DEC_ROWS, B_HD))
    return tuple(_from_heads(z) for z in (u, w, widen(qk), q * eg, k * jnp.exp(tot - g), dec, widen(inv)))


def _gdn_rec(st, u, w, qk, qd, kd, dec):
    u, w, qd, kd = [_to_heads(z) for z in (u, w, qd, kd)]
    qk = _to_heads(qk, width=CHUNK)
    v_new = u - _bdot(w, st)
    o = _bdot(qd, st) + _bdot(qk, v_new)
    st_new = st * _to_heads(dec, rows=DEC_ROWS)[:, :1, :1] + _bdot(kd, v_new, "tn")
    return st_new, _from_heads(o)


def _gdn_prep_fwd(name, q, k, v, beta, la):
    s = q.shape[0]
    t = PREP_CHUNKS * CHUNK
    td = PREP_CHUNKS * DEC_ROWS

    def body(*refs):
        outs = _gdn_prep(*[r[...] for r in refs[:5]])
        for o_ref, o in zip(refs[5:], outs):
            o_ref[...] = o

    rows, dec_rows = jax.ShapeDtypeStruct((s, B_W), F32), jax.ShapeDtypeStruct((s // CHUNK * DEC_ROWS, B_W), F32)
    return pl.pallas_call(
        body, grid=(s // t,), in_specs=[_rowspec(t, B_W)] * 5,
        out_specs=[_rowspec(t, B_W)] * 5 + [_rowspec(td, B_W), _rowspec(t, B_W)],
        out_shape=[rows] * 5 + [dec_rows, rows],
        name=name, compiler_params=_params(("arbitrary",)))(q, k, v, beta, la)


def _gdn_prep_bwd(name, q, k, v, beta, la, inv, cts):
    s = q.shape[0]
    t = PREP_CHUNKS * CHUNK
    td = PREP_CHUNKS * DEC_ROWS

    def body(*refs):
        inv_saved = refs[5][...]
        _, vjp = jax.vjp(lambda *a: _gdn_prep(*a, inv_saved=inv_saved)[:6], *[r[...] for r in refs[:5]])
        g = vjp(tuple(r[...] for r in refs[6:12]))
        for o_ref, o in zip(refs[12:], g):
            o_ref[...] = o

    return pl.pallas_call(
        body, grid=(s // t,), in_specs=[_rowspec(t, B_W)] * 11 + [_rowspec(td, B_W)],
        out_specs=[_rowspec(t, B_W)] * 5, out_shape=[jax.ShapeDtypeStruct((s, B_W), F32)] * 5,
        name=name, compiler_params=_params(("arbitrary",)))(q, k, v, beta, la, inv, *cts)


def _gdn_rec_fwd(name, u, w, qk, qd, kd, dec):
    s = u.shape[0]
    n = s // CHUNK
    g = REC_CHUNKS

    def body(u_ref, w_ref, qk_ref, qd_ref, kd_ref, dec_ref, o_ref, sts_ref, st_ref):
        @pl.when(pl.program_id(0) == 0)
        def _():
            st_ref[...] = jnp.zeros_like(st_ref)

        st = st_ref[...]
        for c in range(g):
            rows, drows = pl.ds(c * CHUNK, CHUNK), pl.ds(c * DEC_ROWS, DEC_ROWS)
            sts_ref[c] = st
            st, o = _gdn_rec(st, u_ref[rows], w_ref[rows], qk_ref[rows], qd_ref[rows], kd_ref[rows], dec_ref[drows])
            o_ref[rows] = o
        st_ref[...] = st

    cs = _rowspec(g * CHUNK, B_W)
    return pl.pallas_call(
        body, grid=(n // g,), in_specs=[cs] * 5 + [_rowspec(g * DEC_ROWS, B_W)],
        out_specs=[cs, pl.BlockSpec((g, B_NH, B_HD, B_HD), lambda i: (i, 0, 0, 0))],
        out_shape=[jax.ShapeDtypeStruct((s, B_W), F32), jax.ShapeDtypeStruct((n, B_NH, B_HD, B_HD), F32)],
        scratch_shapes=[pltpu.VMEM((B_NH, B_HD, B_HD), F32)],
        name=name, compiler_params=_params(("arbitrary",)))(u, w, qk, qd, kd, dec)


def _gdn_rec_bwd(name, u, w, qk, qd, kd, dec, sts, do):
    s = u.shape[0]
    n = s // CHUNK
    g = REC_CHUNKS
    nsteps = n // g

    def body(u_ref, w_ref, qk_ref, qd_ref, kd_ref, dec_ref, sts_ref, do_ref,
             du_ref, dw_ref, dqk_ref, dqd_ref, dkd_ref, ddec_ref, dst_ref):
        @pl.when(pl.program_id(0) == 0)
        def _():
            dst_ref[...] = jnp.zeros_like(dst_ref)

        dst = dst_ref[...]
        for c in reversed(range(g)):
            rows, drows = pl.ds(c * CHUNK, CHUNK), pl.ds(c * DEC_ROWS, DEC_ROWS)
            _, vjp = jax.vjp(_gdn_rec, sts_ref[c], u_ref[rows], w_ref[rows], qk_ref[rows], qd_ref[rows], kd_ref[rows],
                             dec_ref[drows])
            ct = vjp((dst, do_ref[rows]))
            dst = ct[0]
            for r, gi in zip((du_ref, dw_ref, dqk_ref, dqd_ref, dkd_ref), ct[1:6]):
                r[rows] = gi
            ddec_ref[drows] = ct[6]
        dst_ref[...] = dst

    rs = pl.BlockSpec((g * CHUNK, B_W), lambda i: (nsteps - 1 - i, 0))
    ds = pl.BlockSpec((g * DEC_ROWS, B_W), lambda i: (nsteps - 1 - i, 0))
    return pl.pallas_call(
        body, grid=(nsteps,),
        in_specs=[rs] * 5 + [ds, pl.BlockSpec((g, B_NH, B_HD, B_HD), lambda i: (nsteps - 1 - i, 0, 0, 0)), rs],
        out_specs=[rs] * 5 + [ds],
        out_shape=[jax.ShapeDtypeStruct((s, B_W), F32)] * 5 + [jax.ShapeDtypeStruct((n * DEC_ROWS, B_W), F32)],
        scratch_shapes=[pltpu.VMEM((B_NH, B_HD, B_HD), F32)],
        name=name, compiler_params=_params(("arbitrary",)))(u, w, qk, qd, kd, dec, sts, do)


SCAN_T = 512


def _block_scan(a, b, reverse):
    t = a.shape[0]
    rows = _iota(a.shape, 0)
    sft = 1
    while sft < t:
        if reverse:
            a_s, b_s = pltpu.roll(a, t - sft, 0), pltpu.roll(b, t - sft, 0)
            edge = rows >= t - sft
        else:
            a_s, b_s = pltpu.roll(a, sft, 0), pltpu.roll(b, sft, 0)
            edge = rows < sft
        a_s = jnp.where(edge, 1.0, a_s)
        b_s = jnp.where(edge, 0.0, b_s)
        b = a * b_s + b
        a = a * a_s
        sft *= 2
    return a, b


def _lru_scan_fwd(name, a, b):
    s, w = a.shape
    t = SCAN_T

    def body(a_ref, b_ref, h_ref, hp_ref, carry):
        @pl.when(pl.program_id(0) == 0)
        def _():
            carry[...] = jnp.zeros_like(carry)

        h_in = carry[0:1]
        ca, cb = _block_scan(a_ref[...], b_ref[...], False)
        h = ca * h_in + cb
        h_ref[...] = h
        hp_ref[...] = jnp.where(_iota(h.shape, 0) == 0, h_in, pltpu.roll(h, 1, 0))
        carry[...] = jnp.broadcast_to(h[t - 1:t], carry.shape)

    return pl.pallas_call(
        body, grid=(s // t,), in_specs=[_rowspec(t, w)] * 2, out_specs=[_rowspec(t, w)] * 2,
        out_shape=[jax.ShapeDtypeStruct((s, w), F32)] * 2, scratch_shapes=[pltpu.VMEM((8, w), F32)],
        name=name, compiler_params=_params(("arbitrary",)))(a, b)


def _lru_scan_bwd(name, a, dh):
    s, w = a.shape
    t = SCAN_T
    n = s // t

    def body(a_ref, dh_ref, g_ref, a_next, g_next):
        @pl.when(pl.program_id(0) == 0)
        def _():
            a_next[...] = jnp.zeros_like(a_next)
            g_next[...] = jnp.zeros_like(g_next)

        av = a_ref[...]
        a_up = jnp.where(_iota(av.shape, 0) == t - 1, a_next[0:1], pltpu.roll(av, t - 1, 0))
        ca, cb = _block_scan(a_up, dh_ref[...], True)
        g = ca * g_next[0:1] + cb
        g_ref[...] = g
        a_next[...] = jnp.broadcast_to(av[0:1], a_next.shape)
        g_next[...] = jnp.broadcast_to(g[0:1], g_next.shape)

    rs = pl.BlockSpec((t, w), lambda i: (n - 1 - i, 0))
    return pl.pallas_call(
        body, grid=(n,), in_specs=[rs] * 2, out_specs=rs, out_shape=jax.ShapeDtypeStruct((s, w), F32),
        scratch_shapes=[pltpu.VMEM((8, w), F32), pltpu.VMEM((8, w), F32)],
        name=name, compiler_params=_params(("arbitrary",)))(a, dh)


def _loss_fn(g, x, tgt):
    y = x * lax.rsqrt(jnp.mean(x * x, axis=-1, keepdims=True) + EPS) * g
    err = y - tgt
    return 0.5 * jnp.sum(jnp.mean(err * err, axis=-1, keepdims=True), axis=0, keepdims=True)


def _loss_and_grad(name, g, x, tgt, t=256):
    s, d = x.shape

    def body(g_ref, x_ref, t_ref, loss_ref, dx_ref, dg_ref, dx16_ref):
        val, vjp = jax.vjp(_loss_fn, g_ref[...], x_ref[...], t_ref[...])
        dg, dx, _ = vjp(jnp.ones((1, 1), F32))

        @pl.when(pl.program_id(0) == 0)
        def _():
            loss_ref[...] = jnp.zeros_like(loss_ref)
            dg_ref[...] = jnp.zeros_like(dg_ref)

        loss_ref[...] += jnp.broadcast_to(val, loss_ref.shape)
        dg_ref[...] += dg
        dx_ref[...] = dx
        dx16_ref[...] = dx.astype(BF16)

    return pl.pallas_call(
        body, grid=(s // t,), in_specs=[_whole(g.shape), _rowspec(t, d), _rowspec(t, d)],
        out_specs=[_whole((1, 128)), _rowspec(t, d), _whole(g.shape), _rowspec(t, d)],
        out_shape=[jax.ShapeDtypeStruct((1, 128), F32), jax.ShapeDtypeStruct((s, d), F32),
                   jax.ShapeDtypeStruct(g.shape, F32), jax.ShapeDtypeStruct((s, d), BF16)],
        name=name, compiler_params=_params(("arbitrary",)))(g, x, tgt)


def _ew(name, fn, ins, n_out, t=None):
    r, c = ins[0].shape
    t = _tile(r, (512, 256, 128, 64, 32, 16, 8)) if t is None else t

    def body(*refs):
        outs = fn(*[x[...] for x in refs[:len(ins)]])
        for o_ref, o in zip(refs[len(ins):], outs):
            o_ref[...] = o

    return pl.pallas_call(
        body, grid=(r // t,), in_specs=[_rowspec(t, c)] * len(ins), out_specs=[_rowspec(t, c)] * n_out,
        out_shape=[jax.ShapeDtypeStruct((r, c), F32)] * n_out,
        name=name, compiler_params=_params(("arbitrary",)))(*ins)


def _adam_fn(w, g, m, v):
    m = B1 * m + (1.0 - B1) * g
    v = B2 * v + (1.0 - B2) * (g * g)
    m_hat = m / (1.0 - B1 ** STEP)
    v_hat = v / (1.0 - B2 ** STEP)
    delta = -LR * (m_hat / (jnp.sqrt(v_hat) + AEPS) + WD * w)
    return delta, m, v


def _add2_fn(a, b):
    return (a + b,)


def _adam(name, w, g, m, v):
    shp = w.shape
    two = lambda z: z.reshape(-1, shp[-1])
    outs = _ew(name, _adam_fn, [two(w), two(g), two(m), two(v)], 3)
    return [o.reshape(shp) for o in outs]


def _swap(name, bufs, flips):
    n = len(bufs)
    hbm = pl.BlockSpec(memory_space=pltpu.HBM)

    def body(*refs):
        srcs, dsts = refs[:n], refs[n:2 * n]
        send_sems, recv_sems = refs[2 * n], refs[2 * n + 1]
        me = (lax.axis_index("x"), lax.axis_index("y"), lax.axis_index("c"))
        copies = []
        for i in range(n):
            peer = tuple(1 - m if f else m for m, f in zip(me, flips[i]))
            cp = pltpu.make_async_remote_copy(src_ref=srcs[i], dst_ref=dsts[i], send_sem=send_sems.at[i],
                                              recv_sem=recv_sems.at[i], device_id=peer, device_id_type=MESH)
            cp.start()
            copies.append(cp)
        for cp in copies:
            cp.wait()

    return pl.pallas_call(
        body, in_specs=[hbm] * n, out_specs=[hbm] * n,
        out_shape=[jax.ShapeDtypeStruct(b.shape, b.dtype) for b in bufs],
        scratch_shapes=[pltpu.SemaphoreType.DMA((n,)), pltpu.SemaphoreType.DMA((n,))],
        name=name, compiler_params=pltpu.CompilerParams(has_side_effects=True))(*bufs)


FLIP_C = (False, False, True)
FLIP_Y = (False, True, False)
FLIP_X = (True, False, False)


LAYER_ROWS = PACK_ROWS // DEPTH
HALF_ROWS = LAYER_ROWS // 2
SUM_ROWS = HALF_ROWS // 2
GATHER_IDS = (1, 2, 3, 4)
SCATTER_IDS = (5, 6, 7, 8)
FIRST_GATHER_ID = 9


def _mesh_pos():
    ix, iy, core = lax.axis_index("x"), lax.axis_index("y"), lax.axis_index("c")
    others = [(ix, 1 - iy), (1 - ix, iy), (1 - ix, 1 - iy)]
    return ix, iy, core, others


def _pack_big(w_in, w_out, w_up, w_down):
    pad = jnp.pad(w_in, ((0, 0), (0, 0), (0, 1024 - w_in.shape[-1])))
    return jnp.concatenate([pad, w_up, w_down, w_out], axis=1)


def _unpack_big(p):
    w_in = p[..., 0:1024, :D_IN // 4]
    return w_in, p[..., 3072:LAYER_ROWS, :], p[..., 1024:2048, :], p[..., 2048:3072, :]


def _gather_layer(packed, collective_id):
    src = jax.new_ref(packed, memory_space=pltpu.MemorySpace.HBM)
    out = jax.empty_ref(jax.ShapeDtypeStruct((4,) + packed.shape, packed.dtype), memory_space=pltpu.MemorySpace.HBM)
    dma = pltpu.SemaphoreType.DMA

    @pl.kernel(mesh=plsc.ScalarSubcoreMesh(axis_name="seq", num_cores=1), name="gather_%d" % collective_id,
               scratch_types=(dma,) * 12, compiler_params=pltpu.CompilerParams(collective_id=collective_id))
    def launch(*sems):
        send_sems, recv_sems = sems[:6], sems[6:]
        ix, iy, core, others = _mesh_pos()
        barrier = pltpu.get_barrier_semaphore()
        for peer in [(ox, oy, core) for ox, oy in others] + [(ix, iy, 1 - core)]:
            pl.semaphore_signal(barrier, inc=1, device_id=peer, device_id_type=MESH)
        pl.semaphore_wait(barrier, 4)
        half = packed.shape[0] // 2
        mine = pl.ds(core * half, half)
        theirs = pl.ds((1 - core) * half, half)

        def copy(k, src_ref, owner, rows, to):
            return pltpu.make_async_remote_copy(
                src_ref=src_ref, dst_ref=out.at[owner, rows], send_sem=send_sems[k], recv_sem=recv_sems[k],
                device_id=to, device_id_type=MESH)

        chip = 2 * ix + iy
        first = [copy(j, src.at[mine], chip, mine, (*others[j], core)) for j in range(3)]
        for cp in first:
            cp.start()
        passed = []
        for j, (ox, oy) in enumerate(others):
            owner = 2 * ox + oy
            copy(j, src.at[mine], owner, mine, (ix, iy, core)).wait_recv()
            fwd = copy(3 + j, out.at[owner, mine], owner, mine, (ix, iy, 1 - core))
            fwd.start()
            passed.append(fwd)
        for j, (ox, oy) in enumerate(others):
            copy(3 + j, src.at[theirs], 2 * ox + oy, theirs, (ix, iy, core)).wait_recv()
        for cp in first + passed:
            cp.wait_send()

    launch()
    return out[...]


def _rs_sibling(per_chip):
    hbm = pl.BlockSpec(memory_space=pltpu.HBM)

    def body(src, out, send_sem, recv_sem):
        ix, iy, core, _ = _mesh_pos()
        give = pl.ds((1 - core) * HALF_ROWS, HALF_ROWS)
        cp = pltpu.make_async_remote_copy(src_ref=src.at[pl.ds(0, 4), give], dst_ref=out, send_sem=send_sem,
                                          recv_sem=recv_sem, device_id=(ix, iy, 1 - core), device_id_type=MESH)
        cp.start()
        cp.wait()

    return pl.pallas_call(
        body, in_specs=[hbm], out_specs=hbm, out_shape=jax.ShapeDtypeStruct((4, HALF_ROWS, 1024), F32),
        scratch_shapes=[pltpu.SemaphoreType.DMA, pltpu.SemaphoreType.DMA],
        name="rs_sibling", compiler_params=pltpu.CompilerParams(has_side_effects=True))(per_chip)


def _rs_add_pair(per_chip, sib, chip, core):
    nb = HALF_ROWS // SUM_ROWS
    blk = (1, SUM_ROWS, 1024)

    def body(chip_ref, core_ref, a_ref, b_ref, own_ref, send_ref):
        total = a_ref[0] + b_ref[0]
        send_ref[0] = total.astype(BF16)

        @pl.when(pl.program_id(1) == chip_ref[0])
        def _():
            own_ref[...] = total

    return pl.pallas_call(
        body,
        grid_spec=pltpu.PrefetchScalarGridSpec(
            num_scalar_prefetch=2, grid=(nb, 4),
            in_specs=[pl.BlockSpec(blk, lambda i, k, ch, co: (k, co[0] * nb + i, 0)),
                      pl.BlockSpec(blk, lambda i, k, ch, co: (k, i, 0))],
            out_specs=[pl.BlockSpec((SUM_ROWS, 1024), lambda i, k, ch, co: (i, 0)),
                       pl.BlockSpec(blk, lambda i, k, ch, co: (k, i, 0))]),
        out_shape=[jax.ShapeDtypeStruct((HALF_ROWS, 1024), F32), jax.ShapeDtypeStruct((4, HALF_ROWS, 1024), BF16)],
        name="rs_add_pair", compiler_params=_params(("arbitrary", "arbitrary")))(
            chip.reshape(1), core.reshape(1), per_chip, sib)


def _rs_chips(pair, collective_id):
    src_ref = jax.new_ref(pair, memory_space=pltpu.MemorySpace.HBM)
    got_ref = jax.empty_ref(jax.ShapeDtypeStruct((3,) + pair.shape[1:], pair.dtype), memory_space=pltpu.MemorySpace.HBM)
    dma = pltpu.SemaphoreType.DMA

    @pl.kernel(mesh=plsc.ScalarSubcoreMesh(axis_name="seq", num_cores=1), name="rs_chips_%d" % collective_id,
               scratch_types=(dma,) * 6, compiler_params=pltpu.CompilerParams(collective_id=collective_id))
    def launch(s0, s1, s2, r0, r1, r2):
        ix, iy, core, others = _mesh_pos()
        barrier = pltpu.get_barrier_semaphore()
        for ox, oy in others:
            pl.semaphore_signal(barrier, inc=1, device_id=(ox, oy, core), device_id_type=MESH)
        pl.semaphore_wait(barrier, 3)
        copies = []
        for j, ((ox, oy), send_sem, recv_sem) in enumerate(zip(others, (s0, s1, s2), (r0, r1, r2))):
            cp = pltpu.make_async_remote_copy(src_ref=src_ref.at[2 * ox + oy], dst_ref=got_ref.at[j], send_sem=send_sem,
                                              recv_sem=recv_sem, device_id=(ox, oy, core), device_id_type=MESH)
            cp.start()
            copies.append(cp)
        for cp in copies:
            cp.wait()

    launch()
    return got_ref[...]


def _rs_add_chips(own, got, core, layer, buf):
    nb = HALF_ROWS // SUM_ROWS
    blk = (1, SUM_ROWS, 1024)

    def body(core_ref, a_ref, g0_ref, g1_ref, g2_ref, *rest):
        o_ref = rest[-1]
        o_ref[0, 0] = ((a_ref[...] + g0_ref[0].astype(F32)) + g1_ref[0].astype(F32)) + g2_ref[0].astype(F32)

    aliased = [] if buf is None else [buf]
    return pl.pallas_call(
        body,
        grid_spec=pltpu.PrefetchScalarGridSpec(
            num_scalar_prefetch=1, grid=(nb,),
            in_specs=[pl.BlockSpec((SUM_ROWS, 1024), lambda i, c: (i, 0))]
            + [pl.BlockSpec(blk, functools.partial(lambda j, i, c: (j, i, 0), j)) for j in range(3)]
            + [pl.BlockSpec(memory_space=pl.ANY)] * len(aliased),
            out_specs=pl.BlockSpec((1, 1, SUM_ROWS, 1024), lambda i, c: (layer, c[0], i, 0))),
        out_shape=jax.ShapeDtypeStruct((DEPTH, 2, HALF_ROWS, 1024), F32),
        input_output_aliases={} if buf is None else {5: 0},
        name="rs_add_chips", compiler_params=_params(("arbitrary",)))(core.reshape(1), own, got, got, got, *aliased)


def _rs_share(buf):
    hbm = pl.BlockSpec(memory_space=pltpu.HBM)

    def body(src, out, send_sem, recv_sem):
        ix, iy, core, _ = _mesh_pos()
        layers = pl.ds(0, DEPTH)
        cp = pltpu.make_async_remote_copy(src_ref=src.at[layers, core], dst_ref=out.at[layers, core], send_sem=send_sem,
                                          recv_sem=recv_sem, device_id=(ix, iy, 1 - core), device_id_type=MESH)
        cp.start()
        pltpu.make_async_remote_copy(src_ref=src.at[layers, 1 - core], dst_ref=out.at[layers, 1 - core],
                                     send_sem=send_sem, recv_sem=recv_sem, device_id=(ix, iy, core),
                                     device_id_type=MESH).wait_recv()
        cp.wait_send()

    return pl.pallas_call(
        body, in_specs=[hbm], out_specs=hbm, out_shape=jax.ShapeDtypeStruct(buf.shape, buf.dtype),
        input_output_aliases={0: 0}, scratch_shapes=[pltpu.SemaphoreType.DMA, pltpu.SemaphoreType.DMA],
        name="rs_share", compiler_params=pltpu.CompilerParams(has_side_effects=True))(buf)


def _scatter_start(per_chip, chip, core, layer):
    own, pair = _rs_add_pair(per_chip, _rs_sibling(per_chip), chip, core)
    return own, _rs_chips(pair, SCATTER_IDS[layer])


def _allreduce_small(tag, buf):
    for name, flip in (("c", FLIP_C), ("y", FLIP_Y), ("x", FLIP_X)):
        (other,) = _swap(tag + "_swap_" + name, [buf], [flip])
        (buf,) = _ew(tag + "_add_" + name, _add2_fn, [buf, other], 1)
    return buf


IN_COL_SEGMENTS = ((0, 1024, 2048), (1024, 3072, -1024), (3072, 3080, GATE0 - 3072), (3080, D_IN, COL_CX - 3080))


def _orig_cols(w_pad, lo, hi):
    parts = []
    for first, last, shift in IN_COL_SEGMENTS:
        a, b = max(lo, first), min(hi, last)
        if a < b:
            parts.append(w_pad[..., a + shift:b + shift])
    return parts


def _block_diag(w):
    n, d = w.shape[0], w.shape[1]
    on_diag = jnp.arange(n)[:, None] == jnp.arange(n)[None, :]
    return jnp.where(on_diag[:, None, :, None], w[:, :, None, :], 0.0).reshape(n * d, n * d)


def _diag_blocks(w):
    return jnp.stack([w[64 * n:64 * (n + 1), 64 * n:64 * (n + 1)] for n in range(4)], axis=0)


def _gate_row(v):
    return jnp.concatenate([jnp.zeros((4,), F32), v, jnp.zeros((120,), F32)]).reshape(1, 128)


SMALL = ("norm1_g", "hgrn_lb_logits", "hgrn_norm_g", "gdn_conv_w", "gdn_a_log", "gdn_dt_bias", "gdn_norm_g",
         "lru_conv_w", "lru_conv_b", "lru_w_a", "lru_b_a", "lru_w_x", "lru_b_x", "lru_lambda", "norm2_g",
         "final_norm_g")


def _rows_of(shape):
    size = 1
    for d in shape:
        size *= d
    return size, -(-size // 1024) * 8


def _flatten_small(tree):
    parts = []
    for k in sorted(tree):
        size, nr = _rows_of(tree[k].shape)
        parts.append(jnp.pad(tree[k].reshape(-1), (0, nr * 128 - size)).reshape(nr, 128))
    used = sum(pt.shape[0] for pt in parts)
    parts.append(jnp.zeros((-(-used // 512) * 512 - used, 128), F32))
    return jnp.concatenate(parts, axis=0)


def _unflatten_small(buf, shapes):
    out, r0 = {}, 0
    for k in sorted(shapes):
        size, nr = _rows_of(shapes[k])
        out[k] = buf[r0:r0 + nr].reshape(-1)[:size].reshape(shapes[k])
        r0 += nr
    return out


def _layer_fwd(l, x, p, w_in, late):
    sv = {"x0": x}
    (h,) = _rowwise_fwd("norm1", _rmsnorm_fn, [p["norm1_g"]], [x], [D_MODEL], [BF16], 512)
    proj, w_out, w_up, w_down = late(_matmul("proj_in", h, w_in, "nn"))
    sv["h"], sv["proj"] = h, proj
    aq, af, ag = _cols(proj, COL_AQ, A_W), _cols(proj, COL_AF, A_W), _cols(proj, COL_AG, A_W)
    ai = lax.slice_in_dim(proj, COL_AI, COL_AI + A_W, axis=1)
    bz, gate, cy = _cols(proj, COL_BZ, B_W), _cols(proj, GATE0, 128), _cols(proj, COL_CY, C_W)
    q, k, lf = _rowwise_fwd("hgrn_pre", functools.partial(_hgrn_pre_fn, l), [p["lb_logits"]], [aq, af],
                            [A_W] * 3, [F32] * 3, 512)
    o_a, sts_a = _hgrn_fwd("hgrn_chunk", q, k, ai, lf)
    (y_a,) = _rowwise_fwd("hgrn_post", _hgrn_post_fn, [p["hgrn_norm_g"]], [o_a, ag], [A_W], [BF16], 512)
    sv.update(aq=aq, af=af, ai=ai, ag=ag, hq=q, hk=k, hlf=lf, o_a=o_a, sts_a=sts_a)
    qkvc = _conv_fwd("gdn_conv", proj, 0, 3 * B_W, p["gdn_conv_w"], jnp.zeros((1, 3 * B_W), F32), 256)
    gq, gk, gv, gb, gla = _rowwise_fwd("gdn_pre", _gdn_pre_fn, [p["gdn_a_log"], p["gdn_dt_bias"]], [qkvc, gate],
                                       [B_W] * 5, [F32] * 5, 256)
    *wy, sv["gdn_inv"] = _gdn_prep_fwd("gdn_prep", gq, gk, gv, gb, gla)
    o_b, sts_b = _gdn_rec_fwd("gdn_rec", *wy)
    sv["wy"] = wy
    (y_b,) = _rowwise_fwd("gdn_post", _gdn_post_fn, [p["gdn_norm_g"]], [o_b, bz], [B_W], [BF16], 512)
    sv.update(qkvc=qkvc, gate=gate, bz=bz, gq=gq, gk=gk, gv=gv, gb=gb, gla=gla, o_b=o_b, sts_b=sts_b)
    xc = _conv_fwd("lru_conv", proj, COL_CX, C_W, p["lru_conv_w"], p["lru_conv_b"], 512)
    a, b = _rowwise_fwd("lru_pre", _lru_pre_fn, [p["lru_w_a"], p["lru_b_a"], p["lru_w_x"], p["lru_b_x"], p["lru_lambda"]],
                        [xc], [C_W] * 2, [F32] * 2, 512)
    hs, h_prev = _lru_scan_fwd("lru_scan", a, b)
    (y_c,) = _rowwise_fwd("lru_post", _lru_post_fn, [], [hs, cy], [C_W], [BF16], 512)
    sv.update(xc=xc, la=a, hs=hs, h_prev=h_prev, cy=cy)
    mixed = jnp.concatenate([y_b, y_a, y_c], axis=1)
    x1 = _matmul("proj_out", mixed, w_out, "nn", res=x)
    (h2,) = _rowwise_fwd("norm2", _rmsnorm_fn, [p["norm2_g"]], [x1], [D_MODEL], [BF16], 512)
    up, act = _matmul("mlp_up", h2, w_up, "nn", also=_relu2)
    x2 = _matmul("mlp_down", act, w_down, "nn", res=x1)
    sv.update(mixed=mixed, x1=x1, h2=h2, up=up, act=act)
    return x2, sv, (w_in, w_out, w_up, w_down)


def _layer_bwd(l, dx, dx16, p, sv, w_in, w_out, w_up, w_down):
    gs = {}
    dup = _matmul("mlp_down_dx", dx16, w_down, "nt", epi=_drelu2, epi_in=sv["up"], out_dtype=BF16)
    d_w_down = _matmul("mlp_down_dw", sv["act"], dx16, "tn")
    dh2 = _matmul("mlp_up_dx", dup, w_up, "nt")
    d_w_up = _matmul("mlp_up_dw", sv["h2"], dup, "tn")
    (gs["norm2_g"],), (dx1, dx1_16) = _rowwise_bwd("norm2_bwd", _rmsnorm_fn, [p["norm2_g"]], [sv["x1"]], [dh2], 512,
                                                   res=dx, copy16=True)
    dmixed = _matmul("proj_out_dx", dx1_16, w_out, "nt")
    d_w_out = _matmul("proj_out_dw", sv["mixed"], dx1_16, "tn")
    dy_b, dy_a, dy_c = _cols(dmixed, 0, B_W), _cols(dmixed, B_W, A_W), _cols(dmixed, B_W + A_W, C_W)
    (gs["hgrn_norm_g"],), (do_a, dproj) = _rowwise_bwd("hgrn_post_bwd", _hgrn_post_fn, [p["hgrn_norm_g"]],
                                                       [sv["o_a"], sv["ag"]], [dy_a], 512, pack=([1], COL_AG, A_W, None))
    dq, dk, dproj, dlf = _hgrn_bwd("hgrn_chunk_bwd", sv["hq"], sv["hk"], sv["ai"], sv["hlf"], sv["sts_a"], do_a,
                                   dproj, COL_AI)
    (gs["lb_logits"],), (dproj,) = _rowwise_bwd("hgrn_pre_bwd", functools.partial(_hgrn_pre_fn, l), [p["lb_logits"]],
                                                [sv["aq"], sv["af"]], [dq, dk, dlf], 512,
                                                pack=([0, 1], COL_AQ, 2 * A_W, dproj))
    (gs["gdn_norm_g"],), (do_b, dproj) = _rowwise_bwd("gdn_post_bwd", _gdn_post_fn, [p["gdn_norm_g"]],
                                                      [sv["o_b"], sv["bz"]], [dy_b], 512, pack=([1], COL_BZ, B_W, dproj))
    d_wy = _gdn_rec_bwd("gdn_rec_bwd", *sv["wy"], sv["sts_b"], do_b)
    dgq, dgk, dgv, dgb, dgla = _gdn_prep_bwd("gdn_prep_bwd", sv["gq"], sv["gk"], sv["gv"], sv["gb"], sv["gla"],
                                             sv["gdn_inv"], d_wy)
    (gs["gdn_a_log"], gs["gdn_dt_bias"]), (dqkvc, dproj) = _rowwise_bwd(
        "gdn_pre_bwd", _gdn_pre_fn, [p["gdn_a_log"], p["gdn_dt_bias"]], [sv["qkvc"], sv["gate"]],
        [dgq, dgk, dgv, dgb, dgla], 256, pack=([1], GATE0, D_IN_PAD - GATE0, dproj))
    dproj, gs["gdn_conv_w"], _ = _conv_bwd("gdn_conv_bwd", dqkvc, sv["proj"], 0, 3 * B_W, p["gdn_conv_w"], dproj, 256)
    _, (dhs, dproj) = _rowwise_bwd("lru_post_bwd", _lru_post_fn, [], [sv["hs"], sv["cy"]], [dy_c], 512,
                                   pack=([1], COL_CY, C_W, dproj))
    g = _lru_scan_bwd("lru_scan_bwd", sv["la"], dhs)
    (da,) = _rowwise_fwd("lru_da", _lru_da_fn, [], [g, sv["h_prev"]], [C_W], [F32], 512)
    lru_params = [p["lru_w_a"], p["lru_b_a"], p["lru_w_x"], p["lru_b_x"], p["lru_lambda"]]
    dps, (dxc,) = _rowwise_bwd("lru_pre_bwd", _lru_pre_fn, lru_params, [sv["xc"]], [da, g], 512)
    gs["lru_w_a"], gs["lru_b_a"], gs["lru_w_x"], gs["lru_b_x"], gs["lru_lambda"] = dps
    dproj, gs["lru_conv_w"], gs["lru_conv_b"] = _conv_bwd("lru_conv_bwd", dxc, sv["proj"], COL_CX, C_W, p["lru_conv_w"],
                                                          dproj, 512)
    dh = _matmul("proj_in_dx", dproj, w_in, "nt")
    d_w_in = _matmul("proj_in_dw", sv["h"], dproj, "tn")
    (gs["norm1_g"],), (dx0, dx0_16) = _rowwise_bwd("norm1_bwd", _rmsnorm_fn, [p["norm1_g"]], [sv["x0"]], [dh], 512,
                                                   res=dx1, copy16=True)
    return dx0, dx0_16, dict(w_in=d_w_in, w_out=d_w_out, w_up=d_w_up, w_down=d_w_down), gs


def _layer_params(l, lb_logits, hgrn_norm_g, gdn_conv_w, gdn_a_log, gdn_dt_bias, gdn_norm_g, lru_conv_w, lru_conv_b,
                  lru_w_a, lru_b_a, lru_w_x, lru_b_x, lru_lambda, norm1_g, norm2_g):
    row = lambda v: v.reshape(1, -1)
    return dict(
        norm1_g=row(norm1_g[l]), norm2_g=row(norm2_g[l]), lb_logits=lb_logits,
        hgrn_norm_g=row(jnp.tile(hgrn_norm_g[l], A_W // A_HD)),
        gdn_conv_w=gdn_conv_w[l], gdn_a_log=_gate_row(gdn_a_log[l]), gdn_dt_bias=_gate_row(gdn_dt_bias[l]),
        gdn_norm_g=row(jnp.tile(gdn_norm_g[l], B_NH)),
        lru_conv_w=lru_conv_w[l], lru_conv_b=row(lru_conv_b[l]), lru_w_a=_block_diag(lru_w_a[l]), lru_b_a=row(lru_b_a[l]),
        lru_w_x=_block_diag(lru_w_x[l]), lru_b_x=row(lru_b_x[l]), lru_lambda=row(lru_lambda[l]))


def _small_grads_to_reference(gs_layers, d_final_g, d_logits):
    st = lambda k, f=lambda z: z: jnp.stack([f(g[k]) for g in gs_layers], axis=0)
    vec = lambda z: z.reshape(-1)
    return dict(
        norm1_g=st("norm1_g", vec), norm2_g=st("norm2_g", vec), hgrn_lb_logits=d_logits,
        hgrn_norm_g=st("hgrn_norm_g", lambda z: z.reshape(A_W // A_HD, A_HD).sum(0)),
        gdn_conv_w=st("gdn_conv_w"), gdn_a_log=st("gdn_a_log", lambda z: z[0, 4:8]),
        gdn_dt_bias=st("gdn_dt_bias", lambda z: z[0, 4:8]),
        gdn_norm_g=st("gdn_norm_g", lambda z: z.reshape(B_NH, B_HD).sum(0)),
        lru_conv_w=st("lru_conv_w"), lru_conv_b=st("lru_conv_b", vec), lru_w_a=st("lru_w_a", _diag_blocks),
        lru_b_a=st("lru_b_a", vec), lru_w_x=st("lru_w_x", _diag_blocks), lru_b_x=st("lru_b_x", vec),
        lru_lambda=st("lru_lambda", vec), final_norm_g=d_final_g.reshape(-1))


def _local_step(x, tgt, small, layer_weights, on_layer_grads=None):
    layer_p = [_layer_params(l, small["hgrn_lb_logits"], small["hgrn_norm_g"], small["gdn_conv_w"], small["gdn_a_log"],
                             small["gdn_dt_bias"], small["gdn_norm_g"], small["lru_conv_w"], small["lru_conv_b"],
                             small["lru_w_a"], small["lru_b_a"], small["lru_w_x"], small["lru_b_x"], small["lru_lambda"],
                             small["norm1_g"], small["norm2_g"]) for l in range(DEPTH)]
    saved, weights = [], []
    for l in range(DEPTH):
        x, w_in, late = layer_weights(l, x)
        x, sv, w = _layer_fwd(l, x, layer_p[l], w_in, late)
        saved.append(sv)
        weights.append(w)
    loss, dx, d_final_g, dx16 = _loss_and_grad("final_loss", small["final_norm_g"].reshape(1, -1), x, tgt)
    big = [None] * DEPTH
    gs_layers = [None] * DEPTH
    d_logits = jnp.zeros_like(small["hgrn_lb_logits"])
    for l in reversed(range(DEPTH)):
        dx, dx16, big[l], gs_layers[l] = _layer_bwd(l, dx, dx16, layer_p[l], saved[l], *weights[l])
        d_logits = d_logits + gs_layers[l]["lb_logits"]
        if on_layer_grads is not None:
            big[l] = on_layer_grads(l, big[l])
    return loss, dx, big, _small_grads_to_reference(gs_layers, d_final_g, d_logits)


def kernel(x, norm1_g, w_in, hgrn_lb_logits, hgrn_norm_g, gdn_conv_w, gdn_a_log, gdn_dt_bias, gdn_norm_g, lru_conv_w, lru_conv_b, lru_w_a, lru_b_a, lru_w_x, lru_b_x, lru_lambda, w_out, norm2_g, w_up, w_down, final_norm_g, loss_target, m_norm1_g, m_w_in, m_hgrn_lb_logits, m_hgrn_norm_g, m_gdn_conv_w, m_gdn_a_log, m_gdn_dt_bias, m_gdn_norm_g, m_lru_conv_w, m_lru_conv_b, m_lru_w_a, m_lru_b_a, m_lru_w_x, m_lru_b_x, m_lru_lambda, m_w_out, m_norm2_g, m_w_up, m_w_down, m_final_norm_g, v_norm1_g, v_w_in, v_hgrn_lb_logits, v_hgrn_norm_g, v_gdn_conv_w, v_gdn_a_log, v_gdn_dt_bias, v_gdn_norm_g, v_lru_conv_w, v_lru_conv_b, v_lru_w_a, v_lru_b_a, v_lru_w_x, v_lru_b_x, v_lru_lambda, v_w_out, v_norm2_g, v_w_up, v_w_down, v_final_norm_g):
    args = dict(locals())
    ix, iy, core = lax.axis_index("x"), lax.axis_index("y"), lax.axis_index("c")
    chip = 2 * ix + iy

    packed = _pack_big(w_in, w_out, w_up, w_down).astype(BF16)
    first_w_in = _gather_layer(packed[0, :1024], FIRST_GATHER_ID)
    gathered = [_gather_layer(packed[l, 1024 * (l == 0):], GATHER_IDS[l]) for l in range(DEPTH)]

    def with_own(got, own):
        return lax.dynamic_update_slice(got, own[None], (chip, 0, 0))

    def full_w_in(shares):
        def cols(lo, hi):
            out = []
            for k in range(4):
                a, b = max(lo, k * n_in), min(hi, (k + 1) * n_in)
                if a < b:
                    out.append(shares[k, :1024, a - k * n_in:b - k * n_in])
            return out

        pieces = [pc for first, last, _ in sorted(IN_COL_SEGMENTS, key=lambda seg: seg[0] + seg[2])
                  for pc in cols(first, last)]
        return jnp.concatenate(pieces + [jnp.zeros((1024, D_IN_PAD - D_IN), shares.dtype)], axis=-1)

    def full_rest(shares):
        return (jnp.concatenate([shares[k, 2048:] for k in W_OUT_CHIP_ORDER], axis=0),
                jnp.concatenate([shares[k, :1024] for k in range(4)], axis=-1),
                jnp.concatenate([shares[k, 1024:2048] for k in range(4)], axis=0))

    def layer_weights(l, x):
        got, own = gathered[l], packed[l]
        if l:
            got, own, x = lax.optimization_barrier((got, own, x))
            shares = with_own(got, own)
            rest = full_rest(shares[:, 1024:])
            return x, full_w_in(shares), lambda v: (v,) + rest

        def late(v):
            got_v, own_v, v = lax.optimization_barrier((got, own, v))
            return (v,) + full_rest(with_own(got_v, own_v[1024:]))

        return x, full_w_in(with_own(first_w_in, own[:1024])), late

    n_gc, n_lc = gdn_conv_w.shape[-1], lru_conv_w.shape[-1]
    conv_full = dict(
        gdn_conv_w=lax.dynamic_update_slice(jnp.zeros((DEPTH, 4, 4 * n_gc), F32), gdn_conv_w * 0.5, (0, 0, chip * n_gc)),
        lru_conv_w=lax.dynamic_update_slice(jnp.zeros((DEPTH, 4, 4 * n_lc), F32), lru_conv_w * 0.5, (0, 0, chip * n_lc)))
    conv_shapes = {k: v.shape for k, v in conv_full.items()}
    conv_full = _unflatten_small(_allreduce_small("convw", _flatten_small(conv_full)), conv_shapes)

    small = {k: args[k] for k in SMALL}
    small.update(conv_full)

    n_in = D_IN // 4

    def start_scatter(l, b):
        pieces = []
        for k in range(4):
            cols = jnp.concatenate(_orig_cols(b["w_in"], k * n_in, (k + 1) * n_in), axis=1)
            r0 = 256 * W_OUT_CHIP_ORDER.index(k)
            pieces += [jnp.pad(cols, ((0, 0), (0, 1024 - n_in))), b["w_up"][:, k * 1024:(k + 1) * 1024],
                       b["w_down"][k * 1024:(k + 1) * 1024], b["w_out"][r0:r0 + 256]]
        per_chip = jnp.concatenate(pieces, axis=0).reshape(4, LAYER_ROWS, 1024)
        return _scatter_start(per_chip, chip, core, l)

    loss, grad_x, started, sg = _local_step(x[0], loss_target[0], small, layer_weights, start_scatter)

    sg["loss"] = loss[0, :1]
    shapes = {k: v.shape for k, v in sg.items()}
    sg = _unflatten_small(_allreduce_small("smallgrad", _flatten_small(sg)), shapes)
    loss_out = sg.pop("loss")[0]

    buf = None
    for l in reversed(range(DEPTH)):
        own, got = started[l]
        buf = _rs_add_chips(own, got, core, l, buf)
    g_in, g_out, g_up, g_down = _unpack_big(_rs_share(buf).reshape(DEPTH, LAYER_ROWS, 1024))

    grads = dict(sg)
    grads["gdn_conv_w"] = lax.dynamic_slice_in_dim(sg["gdn_conv_w"], chip * n_gc, n_gc, axis=2)
    grads["lru_conv_w"] = lax.dynamic_slice_in_dim(sg["lru_conv_w"], chip * n_lc, n_lc, axis=2)
    grads.update(w_in=g_in, w_out=g_out, w_up=g_up, w_down=g_down)

    names = ['norm1_g', 'w_in', 'hgrn_lb_logits', 'hgrn_norm_g', 'gdn_conv_w', 'gdn_a_log', 'gdn_dt_bias', 'gdn_norm_g',
             'lru_conv_w', 'lru_conv_b', 'lru_w_a', 'lru_b_a', 'lru_w_x', 'lru_b_x', 'lru_lambda', 'w_out', 'norm2_g',
             'w_up', 'w_down', 'final_norm_g']
    big_names = ("w_in", "w_out", "w_up", "w_down")
    delta, new_m, new_v = {}, {}, {}
    for k in big_names:
        delta[k], new_m[k], new_v[k] = _adam("adam_" + k, args[k], grads[k], args["m_" + k], args["v_" + k])
    small_names = [k for k in names if k not in big_names]
    shapes = {k: args[k].shape for k in small_names}
    flat = [_flatten_small({k: src(k) for k in small_names})
            for src in (lambda k: args[k], lambda k: grads[k], lambda k: args["m_" + k], lambda k: args["v_" + k])]
    outs = _ew("adam_small", _adam_fn, flat, 3)
    for dst, o in zip((delta, new_m, new_v), outs):
        dst.update(_unflatten_small(o, shapes))
    return (loss_out, grad_x[None], *[grads[k] for k in names], *[delta[k] for k in names],
            *[new_m[k] for k in names], *[new_v[k] for k in names])
```

```python
import functools

import jax
import jax.numpy as jnp
from jax import lax
from jax.experimental import pallas as pl
from jax.experimental.pallas import tpu as pltpu
from jax.experimental.pallas import tpu_sc as plsc

F32 = jnp.float32
BF16 = jnp.bfloat16
MESH = pl.DeviceIdType.MESH

DEPTH = 4
D_MODEL = 1024
A_W, B_W, C_W = 256, 512, 256
A_HD, B_HD = 64, 128
B_NH = B_W // B_HD
D_IN = 3592
D_IN_PAD = 3840
COL_BZ, COL_AQ, COL_AF, COL_AI, COL_AG, COL_CX, COL_CY = 1536, 2048, 2304, 2560, 2816, 3072, 3328
W_OUT_CHIP_ORDER = (1, 2, 0, 3)
GATE0 = 3584
D_FF = 4096
CHUNK = 64
SUB = 16
EPS = 1e-6
TINY = 1e-30
MASKED_EXPONENT = -1e30
RG_C = 8.0
LR, B1, B2, AEPS, WD, STEP = 0.001, 0.9, 0.999, 1e-08, 0.01, 10
VMEM_LIMIT = 56 * 1024 * 1024
PACK_ROWS = 13 * 1024

NN = ((1,), (0,))
NT = ((1,), (1,))
TN = ((0,), (0,))


DIMS = {"nn": NN, "nt": NT, "tn": TN}


def _split(a, n):
    parts, r = [], a
    for i in range(n):
        p = r.astype(BF16)
        parts.append(p)
        if i + 1 < n:
            r = r - p.astype(F32)
    return parts


def _split_dot(a, b, mode, na, nb):
    (ca,), (cb,) = DIMS[mode]
    dims = ((DIMS[mode]), ((), ())) if a.ndim == 2 else (((ca + 1,), (cb + 1,)), ((0,), (0,)))
    pa, pb = _split(a, na), _split(b, nb)
    acc = None
    for i in range(na):
        for j in range(nb):
            if i + j < max(na, nb):
                t = lax.dot_general(pa[i], pb[j], dims, preferred_element_type=F32)
                acc = t if acc is None else acc + t
    return acc


@functools.partial(jax.custom_vjp, nondiff_argnums=(2, 3, 4, 5))
def _mdot(a, b, mode, na, nb, need):
    return _split_dot(a, b, mode, na, nb)


def _mdot_fwd(a, b, mode, na, nb, need):
    return _split_dot(a, b, mode, na, nb), (a, b)


def _mdot_bwd(mode, na, nb, need, res, ct):
    a, b = res
    n = max(na, nb)
    da, db = jnp.zeros_like(a), jnp.zeros_like(b)
    if mode == "nn":
        if need[0]:
            da = _split_dot(ct, b, "nt", n, nb)
        if need[1]:
            db = _split_dot(a, ct, "tn", na, n)
    elif mode == "nt":
        if need[0]:
            da = _split_dot(ct, b, "nn", n, nb)
        if need[1]:
            db = _split_dot(ct, a, "tn", n, na)
    else:
        if need[0]:
            da = _split_dot(b, ct, "nt", nb, n)
        if need[1]:
            db = _split_dot(a, ct, "nn", na, n)
    return da, db


_mdot.defvjp(_mdot_fwd, _mdot_bwd)
BOTH = (True, True)


def _bdot(a, b, mode="nn"):
    return _mdot(a, b, mode, 1, 1, BOTH)


def _group_sum(x, ones):
    return _mdot(x, ones, "nn", 2, 1, (True, False))


def _running_sum(x):
    shape = x.shape[:-2] + (CHUNK, CHUNK)
    tri = (_iota(shape, len(shape) - 1) <= _iota(shape, len(shape) - 2)).astype(F32)
    return _mdot(tri, x, "nn", 1, 3, (False, True))


def _iota(shape, d):
    return lax.broadcasted_iota(jnp.int32, shape, d)


def _block_ones(width, group):
    return (_iota((width, width), 0) // group == _iota((width, width), 1) // group).astype(F32)


def _params(sem):
    return pltpu.CompilerParams(dimension_semantics=sem, vmem_limit_bytes=VMEM_LIMIT)


def _tile(dim, prefs):
    for p in prefs:
        if dim % p == 0:
            return p
    return dim


def _whole(shape):
    return pl.BlockSpec(shape, lambda i: (0,) * len(shape))


def _rowspec(t, w, col0=0):
    cb = col0 // w
    return pl.BlockSpec((t, w), lambda i: (i, cb))


def _cols(arr, col0, width):
    assert col0 % width == 0
    return (arr, col0, width)


def _row_operands(rows, t):
    arrs, specs = [], []
    for r in rows:
        arr, col0, w = r if isinstance(r, tuple) else (r, 0, r.shape[1])
        arrs.append(arr)
        specs.append(_rowspec(t, w, col0))
    return arrs, specs


def _rowwise_fwd(name, fn, params, rows, out_widths, out_dtypes, t, into=None):
    rows, row_specs = _row_operands(rows, t)
    s = rows[0].shape[0]
    n_p, n_r = len(params), len(rows)
    n_buf = 0 if into is None or into[2] is None else 1

    def body(*refs):
        p = [r[...] for r in refs[:n_p]]
        xs = [r[...].astype(F32) for r in refs[n_p:n_p + n_r]]
        outs = fn(pl.program_id(0) * t, *p, *xs)
        for o_ref, o in zip(refs[n_p + n_r + n_buf:], outs):
            o_ref[...] = o.astype(o_ref.dtype)

    out_specs = [_rowspec(t, w) for w in out_widths]
    out_shape = [jax.ShapeDtypeStruct((s, w), d) for w, d in zip(out_widths, out_dtypes)]
    extra = []
    if into is not None:
        out_specs = [_rowspec(t, out_widths[0], into[0])]
        out_shape = [jax.ShapeDtypeStruct((s, into[1]), out_dtypes[0])]
        extra = [into[2]] * n_buf
    return pl.pallas_call(
        body, grid=(s // t,),
        in_specs=[_whole(p.shape) for p in params] + row_specs + [pl.BlockSpec(memory_space=pl.ANY)] * n_buf,
        out_specs=out_specs, out_shape=out_shape, input_output_aliases={n_p + n_r: 0} if n_buf else {},
        name=name, compiler_params=_params(("arbitrary",)))(*params, *rows, *extra)


def _into_spec(t, width, col0, rev_steps=None):
    cb = col0 // width
    assert col0 % width == 0
    if rev_steps is None:
        return pl.BlockSpec((t, width), lambda i: (i, cb))
    return pl.BlockSpec((t, width), lambda i: (rev_steps - 1 - i, cb))


def _rowwise_bwd(name, fn, params, rows, douts, t, need=None, res=None, dtypes=None, copy16=False, pack=None,
                 dout_fn=None):
    rows, row_specs = _row_operands(rows, t)
    douts, dout_specs = _row_operands(douts, t)
    s = rows[0].shape[0]
    n_p, n_r, n_o = len(params), len(rows), len(douts)
    need = [True] * n_r if need is None else need
    n_res = 0 if res is None else 1
    n_buf = 0 if pack is None or pack[3] is None else 1

    def body(*refs):
        p = [r[...] for r in refs[:n_p]]
        xs = [r[...].astype(F32) for r in refs[n_p:n_p + n_r]]
        dys = [r[...].astype(F32) for r in refs[n_p + n_r:n_p + n_r + n_o]]
        k = n_p + n_r + n_o
        res_ref = refs[k] if n_res else None
        k += n_res + n_buf
        dp_refs = refs[k:k + n_p]
        dx_refs = list(refs[k + n_p:])
        row0 = pl.program_id(0) * t
        _, vjp = jax.vjp(lambda *a: tuple(fn(row0, *a)), *p, *xs)
        g = vjp(tuple(dys) if dout_fn is None else tuple(dout_fn(*dys)))

        @pl.when(pl.program_id(0) == 0)
        def _():
            for r in dp_refs:
                r[...] = jnp.zeros_like(r)

        for r, gp in zip(dp_refs, g[:n_p]):
            r[...] += gp
        gx = [gi for gi, nd in zip(g[n_p:], need) if nd]
        if n_res:
            gx[0] = gx[0] + res_ref[...]
        if copy16:
            gx.append(gx[0])
        if pack is not None:
            pieces = [gx[j].astype(BF16) for j in pack[0]]
            missing = pack[2] - sum(pc.shape[1] for pc in pieces)
            if missing:
                pieces.append(jnp.zeros((t, missing), BF16))
            dx_refs.pop()[...] = pieces[0] if len(pieces) == 1 else jnp.concatenate(pieces, axis=1)
            gx = [gi for j, gi in enumerate(gx) if j not in pack[0]]
        for r, gi in zip(dx_refs, gx):
            r[...] = gi.astype(r.dtype)

    widths = [sp.block_shape[1] for sp, nd in zip(row_specs, need) if nd]
    dtypes = [F32] * len(widths) if dtypes is None else list(dtypes)
    if copy16:
        widths, dtypes = widths + widths[:1], dtypes + [BF16]
    out_specs = [_rowspec(t, w) for w in widths]
    out_shape = [jax.ShapeDtypeStruct((s, w), d) for w, d in zip(widths, dtypes)]
    extra, aliases = [], {}
    if pack is not None:
        keep = [j for j in range(len(widths)) if j not in pack[0]]
        out_specs = [out_specs[j] for j in keep] + [_into_spec(t, pack[2], pack[1])]
        out_shape = [out_shape[j] for j in keep] + [jax.ShapeDtypeStruct((s, D_IN_PAD), BF16)]
        if n_buf:
            extra = [pack[3]]
            aliases = {n_p + n_r + n_o + n_res: n_p + len(out_shape) - 1}
    outs = pl.pallas_call(
        body, grid=(s // t,),
        in_specs=([_whole(p.shape) for p in params] + row_specs + dout_specs
                  + ([_rowspec(t, res.shape[1])] if n_res else []) + [pl.BlockSpec(memory_space=pl.ANY)] * n_buf),
        out_specs=[_whole(p.shape) for p in params] + out_specs,
        out_shape=[jax.ShapeDtypeStruct(p.shape, F32) for p in params] + out_shape, input_output_aliases=aliases,
        name=name, compiler_params=_params(("arbitrary",)))(*params, *rows, *douts, *([res] if n_res else []), *extra)
    return list(outs[:n_p]), list(outs[n_p:])


def _rmsnorm_fn(row0, g, x):
    y = x * lax.rsqrt(jnp.mean(x * x, axis=-1, keepdims=True) + EPS)
    return (y * g,)


def _hgrn_pre_fn(layer, row0, logits, aq, af):
    m = jnp.max(logits, axis=0, keepdims=True)
    e = jnp.exp(logits - m)
    p = e / jnp.sum(e, axis=0, keepdims=True)
    if layer == 0:
        lb = jnp.zeros((1, A_W), F32)
    else:
        acc = p[1:2]
        for j in range(2, layer + 1):
            acc = acc + p[j:j + 1]
        lb = jnp.minimum(jnp.maximum(acc, 0.0), 1.0 - EPS)
    sig = jax.nn.sigmoid(af)
    f = lb + (1.0 - lb) * sig
    log_f = jnp.log(jnp.maximum(f, TINY))
    k = (1.0 - lb) * jax.nn.sigmoid(-af)
    return jax.nn.silu(aq), k, log_f


def _hgrn_post_fn(row0, g, o, ag):
    ms = _group_sum(o * o, _block_ones(A_W, A_HD)) * (1.0 / A_HD)
    return (o * lax.rsqrt(ms + EPS) * g * jax.nn.silu(ag),)


def _gdn_pre_fn(row0, a_log, dt_bias, qkvc, gate):
    act = jax.nn.silu(qkvc)
    q, k, v = act[:, :B_W], act[:, B_W:2 * B_W], act[:, 2 * B_W:]
    bo = _block_ones(B_W, B_HD)
    q = q * lax.rsqrt(_group_sum(q * q, bo) + EPS) * (B_HD ** -0.5)
    k = k * lax.rsqrt(_group_sum(k * k, bo) + EPS)
    beta = jax.nn.sigmoid(gate)
    log_alpha = -jnp.exp(a_log) * jax.nn.softplus(gate + dt_bias)
    t = gate.shape[0]
    spread = lambda z, lane0: jnp.concatenate(
        [jnp.broadcast_to(z[:, lane0 + h:lane0 + h + 1], (t, B_HD)) for h in range(B_NH)], axis=1)
    return q, k, v, spread(beta, 0), spread(log_alpha, B_NH)


def _gdn_post_fn(row0, g, o, bz):
    ms = _group_sum(o * o, _block_ones(B_W, B_HD)) * (1.0 / B_HD)
    return (o * lax.rsqrt(ms + EPS) * g * jax.nn.silu(bz),)


def _lru_pre_fn(row0, w_a, b_a, w_x, b_x, lam, xc):
    r = jax.nn.sigmoid(_mdot(xc, w_a, "nn", 2, 2, BOTH) + b_a)
    i = jax.nn.sigmoid(_mdot(xc, w_x, "nn", 2, 2, BOTH) + b_x)
    log_a = -RG_C * r * jax.nn.softplus(-lam)
    a = jnp.exp(log_a)
    one_minus = -jnp.tanh(log_a) * (a * a + 1.0)
    mult = jnp.sqrt(jnp.maximum(one_minus, EPS))
    pos = row0 + _iota(xc.shape, 0)
    mult = jnp.where(pos == 0, 1.0, mult)
    return a, mult * i * xc


def _lru_post_fn(row0, h, cy):
    return (jax.nn.gelu(cy, approximate=True) * h,)


def _relu2(u):
    r = jnp.maximum(u, 0.0)
    return r * r


def _drelu2(acc, u):
    return acc * (2.0 * jnp.maximum(u, 0.0))


MM_VMEM_BUDGET = 44 * 1024 * 1024


def _mm_tiles(mode, m, n, k, a_bytes, b_bytes, n_extra):
    best = None
    for tk in {k, 4096, 2048, 1024, 512, 256}:
        for tn in {n, 1280, 1024, 768, 512, 256, 128}:
            for tm in {m, 1024, 512, 256, 128}:
                if m % tm or n % tn or k % tk or tm > 1024:
                    continue
                tiles = tm * tk * a_bytes + tk * tn * b_bytes + (1 + n_extra) * tm * tn * 4
                if 2 * tiles + (tm * tn * 4 if tk < k else 0) > MM_VMEM_BUDGET:
                    continue
                a_reads = 1 if tk == k else n // tn
                b_reads = 1 if (tk == k and tn == n) else m // tm
                cost = m * k * a_bytes * a_reads + k * n * b_bytes * b_reads + m * n * 4 * (k // tk - 1)
                if best is None or (cost, -tm) < best[0]:
                    best = ((cost, -tm), (tm, tn, tk))
    if best is None:
        raise ValueError("no matmul tiling fits VMEM")
    return best[1]


def _matmul(name, a, b, mode, a_fn=None, res=None, epi=None, epi_in=None, out_dtype=F32, also=None):
    if mode == "nn":
        (m, k), n = a.shape, b.shape[1]
    elif mode == "nt":
        (m, k), n = a.shape, b.shape[0]
    else:
        (k, m), n = a.shape, b.shape[1]
    n_extra = (res is not None) + (epi_in is not None) + (also is not None)
    tm, tn, tk = _mm_tiles(mode, m, n, k, a.dtype.itemsize, b.dtype.itemsize, n_extra)
    nk = k // tk
    a_spec = {"nn": pl.BlockSpec((tm, tk), lambda i, j, kk: (i, kk)),
              "nt": pl.BlockSpec((tm, tk), lambda i, j, kk: (i, kk)),
              "tn": pl.BlockSpec((tk, tm), lambda i, j, kk: (kk, i))}[mode]
    b_spec = {"nn": pl.BlockSpec((tk, tn), lambda i, j, kk: (kk, j)),
              "nt": pl.BlockSpec((tn, tk), lambda i, j, kk: (j, kk)),
              "tn": pl.BlockSpec((tk, tn), lambda i, j, kk: (kk, j))}[mode]
    o_spec = pl.BlockSpec((tm, tn), lambda i, j, kk: (i, j))
    dims = {"nn": NN, "nt": NT, "tn": TN}[mode]
    extra = [x for x in (res, epi_in) if x is not None]

    def body(*refs):
        a_ref, b_ref = refs[0], refs[1]
        rest = list(refs[2:])
        res_ref = rest.pop(0) if res is not None else None
        epi_ref = rest.pop(0) if epi_in is not None else None
        o_ref = rest.pop(0)
        also_ref = rest.pop(0) if also is not None else None
        acc = rest[0] if nk > 1 else None
        kk = pl.program_id(2)
        at = a_ref[...]
        if a_fn is not None:
            at = a_fn(at.astype(F32))
        part = lax.dot_general(at.astype(BF16), b_ref[...].astype(BF16), (dims, ((), ())),
                               preferred_element_type=F32)

        def finish(out):
            if res_ref is not None:
                out = out + res_ref[...]
            if epi is not None:
                out = epi(out, epi_ref[...])
            o_ref[...] = out.astype(o_ref.dtype)
            if also is not None:
                also_ref[...] = also(out).astype(also_ref.dtype)

        if nk == 1:
            finish(part)
        else:
            @pl.when(kk == 0)
            def _():
                acc[...] = part

            @pl.when(jnp.logical_and(kk > 0, kk < nk - 1))
            def _():
                acc[...] += part

            @pl.when(kk == nk - 1)
            def _():
                finish(acc[...] + part)

    out_shape = [jax.ShapeDtypeStruct((m, n), out_dtype)] + ([jax.ShapeDtypeStruct((m, n), BF16)] if also else [])
    outs = pl.pallas_call(
        body, grid=(m // tm, n // tn, nk),
        in_specs=[a_spec, b_spec] + [o_spec] * len(extra), out_specs=[o_spec] * len(out_shape), out_shape=out_shape,
        scratch_shapes=[pltpu.VMEM((tm, tn), F32)] if nk > 1 else [],
        name=name, compiler_params=_params(("arbitrary", "arbitrary", "arbitrary")))(a, b, *extra)
    return outs if also else outs[0]


HALO = 8


def _conv_fwd(name, x, col0, width, w, bias, t=512):
    s = x.shape[0]
    cb = col0 // width
    hb = t // HALO

    def body(x_ref, halo_ref, w_ref, b_ref, y_ref):
        i = pl.program_id(0)
        halo = jnp.where(i == 0, 0.0, halo_ref[...])
        xp = jnp.concatenate([halo, x_ref[...]], axis=0)
        wv = w_ref[...]
        y = b_ref[...] + wv[0:1] * xp[HALO - 3:HALO - 3 + t]
        for j in range(1, 4):
            y = y + wv[j:j + 1] * xp[HALO - 3 + j:HALO - 3 + j + t]
        y_ref[...] = y

    return pl.pallas_call(
        body, grid=(s // t,),
        in_specs=[pl.BlockSpec((t, width), lambda i: (i, cb)),
                  pl.BlockSpec((HALO, width), lambda i: (jnp.maximum(i * hb - 1, 0), cb)),
                  _whole(w.shape), _whole(bias.shape)],
        out_specs=_rowspec(t, width), out_shape=jax.ShapeDtypeStruct((s, width), F32),
        name=name, compiler_params=_params(("arbitrary",)))(x, x, w, bias)


def _conv_bwd(name, dy, x, col0, width, w, buf, t=512):
    s = x.shape[0]
    cb = col0 // width
    hb = t // HALO
    nblk = s // t

    def body(dy_ref, dyn_ref, x_ref, halo_ref, w_ref, buf_ref, dx_ref, dw_ref, db_ref):
        i = pl.program_id(0)
        dyv = dy_ref[...]
        nxt = jnp.where(i == nblk - 1, 0.0, dyn_ref[...])
        dyp = jnp.concatenate([dyv, nxt], axis=0)
        wv = w_ref[...]
        dx = wv[3:4] * dyv
        for j in range(3):
            dx = dx + wv[j:j + 1] * dyp[3 - j:3 - j + t]
        dx_ref[...] = dx.astype(dx_ref.dtype)
        halo = jnp.where(i == 0, 0.0, halo_ref[...])
        xp = jnp.concatenate([halo, x_ref[...]], axis=0)
        dw = jnp.concatenate(
            [jnp.sum(dyv * xp[HALO - 3 + j:HALO - 3 + j + t], axis=0, keepdims=True) for j in range(4)], axis=0)

        @pl.when(i == 0)
        def _():
            dw_ref[...] = jnp.zeros_like(dw_ref)
            db_ref[...] = jnp.zeros_like(db_ref)

        dw_ref[...] += dw
        db_ref[...] += jnp.sum(dyv, axis=0, keepdims=True)

    return pl.pallas_call(
        body, grid=(nblk,),
        in_specs=[_rowspec(t, width),
                  pl.BlockSpec((HALO, width), lambda i: (jnp.minimum((i + 1) * hb, s // HALO - 1), 0)),
                  pl.BlockSpec((t, width), lambda i: (i, cb)),
                  pl.BlockSpec((HALO, width), lambda i: (jnp.maximum(i * hb - 1, 0), cb)),
                  _whole(w.shape), pl.BlockSpec(memory_space=pl.ANY)],
        out_specs=[_into_spec(t, width, col0), _whole((4, width)), _whole((1, width))],
        out_shape=[jax.ShapeDtypeStruct(buf.shape, buf.dtype), jax.ShapeDtypeStruct((4, width), F32),
                   jax.ShapeDtypeStruct((1, width), F32)],
        input_output_aliases={5: 0},
        name=name, compiler_params=_params(("arbitrary",)))(dy, dy, x, x, w, buf)


def _hgrn_chunk(st, q, k, v, lf):
    cum = _running_sum(lf)
    tot = jnp.sum(lf, axis=0, keepdims=True)
    bm = _iota((A_W, A_W), 0) // A_HD == _iota((A_W, A_W), 1) // A_HD
    bo = bm.astype(F32)
    o_inter = _bdot(q * jnp.exp(cum), st, "nt")
    kd = k * jnp.exp(tot - cum)
    st_new = st * jnp.exp(tot) + jnp.where(bm, _bdot(v, kd, "tn"), 0.0)
    lane_h = _iota((SUB, A_W), 1) // A_HD
    n_h = A_W // A_HD
    t3 = _iota((SUB, SUB, A_W), 0)
    s3 = _iota((SUB, SUB, A_W), 1)
    outs = []
    for i in range(CHUNK // SUB):
        sl = slice(i * SUB, (i + 1) * SUB)
        qi, ki, vi, ci = q[sl], k[sl], v[sl], cum[sl]
        e = jnp.exp(jnp.where(s3 <= t3, ci[:, None, :] - ci[None, :, :], MASKED_EXPONENT))
        d3 = qi[:, None, :] * ki[None, :, :] * e
        ds = _mdot(d3.reshape(SUB * SUB, A_W), bo, "nn", 1, 1, (True, False)).reshape(SUB, SUB, A_W)
        oi = jnp.sum(ds * vi[None, :, :], axis=1)
        if i > 0:
            n = i * SUB
            cb = cum[n - 1:n]
            qt = qi * jnp.exp(ci - cb)
            kt = k[:n] * jnp.exp(cb - cum[:n])
            qs = jnp.concatenate([jnp.where(lane_h == h, qt, 0.0) for h in range(n_h)], axis=0)
            p = _bdot(_bdot(qs, kt, "nt"), v[:n])
            for h in range(n_h):
                oi = oi + jnp.where(lane_h == h, p[h * SUB:(h + 1) * SUB], 0.0)
        outs.append(oi)
    return st_new, o_inter + jnp.concatenate(outs, axis=0)


HGRN_CHUNKS = 2


def _hgrn_fwd(name, q, k, v, lf):
    s = q.shape[0]
    n = s // CHUNK
    g = HGRN_CHUNKS

    def body(q_ref, k_ref, v_ref, lf_ref, o_ref, sts_ref, st_ref):
        @pl.when(pl.program_id(0) == 0)
        def _():
            st_ref[...] = jnp.zeros_like(st_ref)

        st = st_ref[...]
        for c in range(g):
            rows = pl.ds(c * CHUNK, CHUNK)
            sts_ref[c] = st
            st, o = _hgrn_chunk(st, q_ref[rows], k_ref[rows], v_ref[rows], lf_ref[rows])
            o_ref[rows] = o
        st_ref[...] = st

    cs = _rowspec(g * CHUNK, A_W)
    return pl.pallas_call(
        body, grid=(n // g,), in_specs=[cs] * 4,
        out_specs=[cs, pl.BlockSpec((g, A_W, A_W), lambda i: (i, 0, 0))],
        out_shape=[jax.ShapeDtypeStruct((s, A_W), F32), jax.ShapeDtypeStruct((n, A_W, A_W), F32)],
        scratch_shapes=[pltpu.VMEM((A_W, A_W), F32)],
        name=name, compiler_params=_params(("arbitrary",)))(q, k, v, lf)


def _hgrn_bwd(name, q, k, v, lf, sts, do, buf, col0):
    s = q.shape[0]
    n = s // CHUNK
    g = HGRN_CHUNKS
    nsteps = n // g

    def body(q_ref, k_ref, v_ref, lf_ref, sts_ref, do_ref, buf_ref, dq_ref, dk_ref, dv_ref, dlf_ref, dst_ref):
        @pl.when(pl.program_id(0) == 0)
        def _():
            dst_ref[...] = jnp.zeros_like(dst_ref)

        dst = dst_ref[...]
        for c in reversed(range(g)):
            rows = pl.ds(c * CHUNK, CHUNK)
            _, vjp = jax.vjp(_hgrn_chunk, sts_ref[c], q_ref[rows], k_ref[rows], v_ref[rows], lf_ref[rows])
            ct = vjp((dst, do_ref[rows]))
            dst = ct[0]
            for r, gi in zip((dq_ref, dk_ref, dv_ref, dlf_ref), ct[1:]):
                r[rows] = gi.astype(r.dtype)
        dst_ref[...] = dst

    rs = pl.BlockSpec((g * CHUNK, A_W), lambda i: (nsteps - 1 - i, 0))
    rows = jax.ShapeDtypeStruct((s, A_W), F32)
    return pl.pallas_call(
        body, grid=(nsteps,),
        in_specs=[rs] * 4 + [pl.BlockSpec((g, A_W, A_W), lambda i: (nsteps - 1 - i, 0, 0)), rs,
                             pl.BlockSpec(memory_space=pl.ANY)],
        out_specs=[rs, rs, _into_spec(g * CHUNK, A_W, col0, nsteps), rs],
        out_shape=[rows, rows, jax.ShapeDtypeStruct(buf.shape, buf.dtype), rows],
        input_output_aliases={6: 2}, scratch_shapes=[pltpu.VMEM((A_W, A_W), F32)],
        name=name, compiler_params=_params(("arbitrary",)))(q, k, v, lf, sts, do, buf)


PREP_CHUNKS = 4
REC_CHUNKS = 8
DEC_ROWS = 8


def _to_heads(x, rows=CHUNK, width=B_HD):
    n = x.shape[0] // rows
    x3 = x.reshape(n, rows, x.shape[1])
    return jnp.concatenate([x3[:, :, h * B_HD:h * B_HD + width] for h in range(B_NH)], axis=0)


def _from_heads(y):
    n = y.shape[0] // B_NH
    x3 = jnp.concatenate([y[h * n:(h + 1) * n] for h in range(B_NH)], axis=2)
    return x3.reshape(n * y.shape[1], B_NH * B_HD)


@jax.custom_vjp
def _unit_lower_inverse(a_mat):
    nil = -a_mat
    eye = (_iota(a_mat.shape, 1) == _iota(a_mat.shape, 2)).astype(F32)
    inv = eye + nil
    sq = 2
    while sq < CHUNK:
        nil = _split_dot(nil, nil, "nn", 2, 2)
        inv = inv + _split_dot(inv, nil, "nn", 2, 2)
        sq *= 2
    return inv


def _unit_lower_inverse_fwd(a_mat):
    inv = _unit_lower_inverse(a_mat)
    return inv, inv


def _unit_lower_inverse_bwd(inv, ct):
    return (-_split_dot(_split_dot(inv, ct, "tn", 2, 2), inv, "nt", 2, 2),)


_unit_lower_inverse.defvjp(_unit_lower_inverse_fwd, _unit_lower_inverse_bwd)


@jax.custom_vjp
def _known_inverse(a_mat, inv):
    return inv


def _known_inverse_fwd(a_mat, inv):
    return inv, inv


def _known_inverse_bwd(inv, ct):
    return _unit_lower_inverse_bwd(inv, ct)[0], jnp.zeros_like(inv)


_known_inverse.defvjp(_known_inverse_fwd, _known_inverse_bwd)


def _gdn_prep(q, k, v, beta, la, inv_saved=None):
    q, k, v, beta, la = [_to_heads(z) for z in (q, k, v, beta, la)]
    nb = q.shape[0]
    r = _iota((nb, CHUNK, CHUNK), 1)
    c = _iota((nb, CHUNK, CHUNK), 2)
    causal = c <= r
    g = _running_sum(la)
    tot = jnp.sum(la, axis=1, keepdims=True)
    gcol = g[:, :, :CHUNK]
    decay = jnp.exp(jnp.where(causal, gcol - jnp.swapaxes(gcol, 1, 2), MASKED_EXPONENT))
    kb = k * beta
    a_mat = jnp.where(c < r, _bdot(kb, k, "nt") * decay, 0.0)
    if inv_saved is None:
        inv = _unit_lower_inverse(a_mat)
    else:
        inv = _known_inverse(a_mat, _to_heads(inv_saved, width=CHUNK))
    eg = jnp.exp(g)
    u = _bdot(inv, v * beta)
    w = _bdot(inv, kb * eg)
    qk = jnp.where(causal, _bdot(q, k, "nt") * decay, 0.0)
    widen = lambda z: jnp.concatenate([z, jnp.zeros((nb, CHUNK, B_HD - CHUNK), F32)], axis=2)
    dec = jnp.broadcast_to(jnp.exp(tot), (nb, DEC_ROWS, B_HD))
    return tuple(_from_heads(z) for z in (u, w, widen(qk), q * eg, k * jnp.exp(tot - g), dec, widen(inv)))


def _gdn_rec(st, u, w, qk, qd, kd, dec):
    u, w, qd, kd = [_to_heads(z) for z in (u, w, qd, kd)]
    qk = _to_heads(qk, width=CHUNK)
    v_new = u - _bdot(w, st)
    o = _bdot(qd, st) + _bdot(qk, v_new)
    st_new = st * _to_heads(dec, rows=DEC_ROWS)[:, :1, :1] + _bdot(kd, v_new, "tn")
    return st_new, _from_heads(o)


def _gdn_prep_fwd(name, q, k, v, beta, la):
    s = q.shape[0]
    t = PREP_CHUNKS * CHUNK
    td = PREP_CHUNKS * DEC_ROWS

    def body(*refs):
        outs = _gdn_prep(*[r[...] for r in refs[:5]])
        for o_ref, o in zip(refs[5:], outs):
            o_ref[...] = o

    rows, dec_rows = jax.ShapeDtypeStruct((s, B_W), F32), jax.ShapeDtypeStruct((s // CHUNK * DEC_ROWS, B_W), F32)
    return pl.pallas_call(
        body, grid=(s // t,), in_specs=[_rowspec(t, B_W)] * 5,
        out_specs=[_rowspec(t, B_W)] * 5 + [_rowspec(td, B_W), _rowspec(t, B_W)],
        out_shape=[rows] * 5 + [dec_rows, rows],
        name=name, compiler_params=_params(("arbitrary",)))(q, k, v, beta, la)


def _gdn_prep_bwd(name, q, k, v, beta, la, inv, cts):
    s = q.shape[0]
    t = PREP_CHUNKS * CHUNK
    td = PREP_CHUNKS * DEC_ROWS

    def body(*refs):
        inv_saved = refs[5][...]
        _, vjp = jax.vjp(lambda *a: _gdn_prep(*a, inv_saved=inv_saved)[:6], *[r[...] for r in refs[:5]])
        g = vjp(tuple(r[...] for r in refs[6:12]))
        for o_ref, o in zip(refs[12:], g):
            o_ref[...] = o

    return pl.pallas_call(
        body, grid=(s // t,), in_specs=[_rowspec(t, B_W)] * 11 + [_rowspec(td, B_W)],
        out_specs=[_rowspec(t, B_W)] * 5, out_shape=[jax.ShapeDtypeStruct((s, B_W), F32)] * 5,
        name=name, compiler_params=_params(("arbitrary",)))(q, k, v, beta, la, inv, *cts)


def _gdn_rec_fwd(name, u, w, qk, qd, kd, dec):
    s = u.shape[0]
    n = s // CHUNK
    g = REC_CHUNKS

    def body(u_ref, w_ref, qk_ref, qd_ref, kd_ref, dec_ref, o_ref, sts_ref, st_ref):
        @pl.when(pl.program_id(0) == 0)
        def _():
            st_ref[...] = jnp.zeros_like(st_ref)

        st = st_ref[...]
        for c in range(g):
            rows, drows = pl.ds(c * CHUNK, CHUNK), pl.ds(c * DEC_ROWS, DEC_ROWS)
            sts_ref[c] = st
            st, o = _gdn_rec(st, u_ref[rows], w_ref[rows], qk_ref[rows], qd_ref[rows], kd_ref[rows], dec_ref[drows])
            o_ref[rows] = o
        st_ref[...] = st

    cs = _rowspec(g * CHUNK, B_W)
    return pl.pallas_call(
        body, grid=(n // g,), in_specs=[cs] * 5 + [_rowspec(g * DEC_ROWS, B_W)],
        out_specs=[cs, pl.BlockSpec((g, B_NH, B_HD, B_HD), lambda i: (i, 0, 0, 0))],
        out_shape=[jax.ShapeDtypeStruct((s, B_W), F32), jax.ShapeDtypeStruct((n, B_NH, B_HD, B_HD), F32)],
        scratch_shapes=[pltpu.VMEM((B_NH, B_HD, B_HD), F32)],
        name=name, compiler_params=_params(("arbitrary",)))(u, w, qk, qd, kd, dec)


def _gdn_rec_bwd(name, u, w, qk, qd, kd, dec, sts, do):
    s = u.shape[0]
    n = s // CHUNK
    g = REC_CHUNKS
    nsteps = n // g

    def body(u_ref, w_ref, qk_ref, qd_ref, kd_ref, dec_ref, sts_ref, do_ref,
             du_ref, dw_ref, dqk_ref, dqd_ref, dkd_ref, ddec_ref, dst_ref):
        @pl.when(pl.program_id(0) == 0)
        def _():
            dst_ref[...] = jnp.zeros_like(dst_ref)

        dst = dst_ref[...]
        for c in reversed(range(g)):
            rows, drows = pl.ds(c * CHUNK, CHUNK), pl.ds(c * DEC_ROWS, DEC_ROWS)
            _, vjp = jax.vjp(_gdn_rec, sts_ref[c], u_ref[rows], w_ref[rows], qk_ref[rows], qd_ref[rows], kd_ref[rows],
                             dec_ref[drows])
            ct = vjp((dst, do_ref[rows]))
            dst = ct[0]
            for r, gi in zip((du_ref, dw_ref, dqk_ref, dqd_ref, dkd_ref), ct[1:6]):
                r[rows] = gi
            ddec_ref[drows] = ct[6]
        dst_ref[...] = dst

    rs = pl.BlockSpec((g * CHUNK, B_W), lambda i: (nsteps - 1 - i, 0))
    ds = pl.BlockSpec((g * DEC_ROWS, B_W), lambda i: (nsteps - 1 - i, 0))
    return pl.pallas_call(
        body, grid=(nsteps,),
        in_specs=[rs] * 5 + [ds, pl.BlockSpec((g, B_NH, B_HD, B_HD), lambda i: (nsteps - 1 - i, 0, 0, 0)), rs],
        out_specs=[rs] * 5 + [ds],
        out_shape=[jax.ShapeDtypeStruct((s, B_W), F32)] * 5 + [jax.ShapeDtypeStruct((n * DEC_ROWS, B_W), F32)],
        scratch_shapes=[pltpu.VMEM((B_NH, B_HD, B_HD), F32)],
        name=name, compiler_params=_params(("arbitrary",)))(u, w, qk, qd, kd, dec, sts, do)


SCAN_T = 512


def _block_scan(a, b, reverse):
    t = a.shape[0]
    rows = _iota(a.shape, 0)
    sft = 1
    while sft < t:
        if reverse:
            a_s, b_s = pltpu.roll(a, t - sft, 0), pltpu.roll(b, t - sft, 0)
            edge = rows >= t - sft
        else:
            a_s, b_s = pltpu.roll(a, sft, 0), pltpu.roll(b, sft, 0)
            edge = rows < sft
        a_s = jnp.where(edge, 1.0, a_s)
        b_s = jnp.where(edge, 0.0, b_s)
        b = a * b_s + b
        a = a * a_s
        sft *= 2
    return a, b


def _lru_scan_fwd(name, a, b):
    s, w = a.shape
    t = SCAN_T

    def body(a_ref, b_ref, h_ref, hp_ref, carry):
        @pl.when(pl.program_id(0) == 0)
        def _():
            carry[...] = jnp.zeros_like(carry)

        h_in = carry[0:1]
        ca, cb = _block_scan(a_ref[...], b_ref[...], False)
        h = ca * h_in + cb
        h_ref[...] = h
        hp_ref[...] = jnp.where(_iota(h.shape, 0) == 0, h_in, pltpu.roll(h, 1, 0))
        carry[...] = jnp.broadcast_to(h[t - 1:t], carry.shape)

    return pl.pallas_call(
        body, grid=(s // t,), in_specs=[_rowspec(t, w)] * 2, out_specs=[_rowspec(t, w)] * 2,
        out_shape=[jax.ShapeDtypeStruct((s, w), F32)] * 2, scratch_shapes=[pltpu.VMEM((8, w), F32)],
        name=name, compiler_params=_params(("arbitrary",)))(a, b)


def _lru_scan_bwd(name, a, dh):
    s, w = a.shape
    t = SCAN_T
    n = s // t

    def body(a_ref, dh_ref, g_ref, a_next, g_next):
        @pl.when(pl.program_id(0) == 0)
        def _():
            a_next[...] = jnp.zeros_like(a_next)
            g_next[...] = jnp.zeros_like(g_next)

        av = a_ref[...]
        a_up = jnp.where(_iota(av.shape, 0) == t - 1, a_next[0:1], pltpu.roll(av, t - 1, 0))
        ca, cb = _block_scan(a_up, dh_ref[...], True)
        g = ca * g_next[0:1] + cb
        g_ref[...] = g
        a_next[...] = jnp.broadcast_to(av[0:1], a_next.shape)
        g_next[...] = jnp.broadcast_to(g[0:1], g_next.shape)

    rs = pl.BlockSpec((t, w), lambda i: (n - 1 - i, 0))
    return pl.pallas_call(
        body, grid=(n,), in_specs=[rs] * 2, out_specs=rs, out_shape=jax.ShapeDtypeStruct((s, w), F32),
        scratch_shapes=[pltpu.VMEM((8, w), F32), pltpu.VMEM((8, w), F32)],
        name=name, compiler_params=_params(("arbitrary",)))(a, dh)


def _loss_fn(g, x, tgt):
    y = x * lax.rsqrt(jnp.mean(x * x, axis=-1, keepdims=True) + EPS) * g
    err = y - tgt
    return 0.5 * jnp.sum(jnp.mean(err * err, axis=-1, keepdims=True), axis=0, keepdims=True)


def _loss_and_grad(name, g, x, tgt, t=256):
    s, d = x.shape

    def body(g_ref, x_ref, t_ref, loss_ref, dx_ref, dg_ref, dx16_ref):
        val, vjp = jax.vjp(_loss_fn, g_ref[...], x_ref[...], t_ref[...])
        dg, dx, _ = vjp(jnp.ones((1, 1), F32))

        @pl.when(pl.program_id(0) == 0)
        def _():
            loss_ref[...] = jnp.zeros_like(loss_ref)
            dg_ref[...] = jnp.zeros_like(dg_ref)

        loss_ref[...] += jnp.broadcast_to(val, loss_ref.shape)
        dg_ref[...] += dg
        dx_ref[...] = dx
        dx16_ref[...] = dx.astype(BF16)

    return pl.pallas_call(
        body, grid=(s // t,), in_specs=[_whole(g.shape), _rowspec(t, d), _rowspec(t, d)],
        out_specs=[_whole((1, 128)), _rowspec(t, d), _whole(g.shape), _rowspec(t, d)],
        out_shape=[jax.ShapeDtypeStruct((1, 128), F32), jax.ShapeDtypeStruct((s, d), F32),
                   jax.ShapeDtypeStruct(g.shape, F32), jax.ShapeDtypeStruct((s, d), BF16)],
        name=name, compiler_params=_params(("arbitrary",)))(g, x, tgt)


def _ew(name, fn, ins, n_out, t=None):
    r, c = ins[0].shape
    t = _tile(r, (512, 256, 128, 64, 32, 16, 8)) if t is None else t

    def body(*refs):
        outs = fn(*[x[...] for x in refs[:len(ins)]])
        for o_ref, o in zip(refs[len(ins):], outs):
            o_ref[...] = o

    return pl.pallas_call(
        body, grid=(r // t,), in_specs=[_rowspec(t, c)] * len(ins), out_specs=[_rowspec(t, c)] * n_out,
        out_shape=[jax.ShapeDtypeStruct((r, c), F32)] * n_out,
        name=name, compiler_params=_params(("arbitrary",)))(*ins)


def _adam_fn(w, g, m, v):
    m = B1 * m + (1.0 - B1) * g
    v = B2 * v + (1.0 - B2) * (g * g)
    m_hat = m / (1.0 - B1 ** STEP)
    v_hat = v / (1.0 - B2 ** STEP)
    delta = -LR * (m_hat / (jnp.sqrt(v_hat) + AEPS) + WD * w)
    return delta, m, v


def _add2_fn(a, b):
    return (a + b,)


def _adam(name, w, g, m, v):
    shp = w.shape
    two = lambda z: z.reshape(-1, shp[-1])
    outs = _ew(name, _adam_fn, [two(w), two(g), two(m), two(v)], 3)
    return [o.reshape(shp) for o in outs]


def _swap(name, bufs, flips):
    n = len(bufs)
    hbm = pl.BlockSpec(memory_space=pltpu.HBM)

    def body(*refs):
        srcs, dsts = refs[:n], refs[n:2 * n]
        send_sems, recv_sems = refs[2 * n], refs[2 * n + 1]
        me = (lax.axis_index("x"), lax.axis_index("y"), lax.axis_index("c"))
        copies = []
        for i in range(n):
            peer = tuple(1 - m if f else m for m, f in zip(me, flips[i]))
            cp = pltpu.make_async_remote_copy(src_ref=srcs[i], dst_ref=dsts[i], send_sem=send_sems.at[i],
                                              recv_sem=recv_sems.at[i], device_id=peer, device_id_type=MESH)
            cp.start()
            copies.append(cp)
        for cp in copies:
            cp.wait()

    return pl.pallas_call(
        body, in_specs=[hbm] * n, out_specs=[hbm] * n,
        out_shape=[jax.ShapeDtypeStruct(b.shape, b.dtype) for b in bufs],
        scratch_shapes=[pltpu.SemaphoreType.DMA((n,)), pltpu.SemaphoreType.DMA((n,))],
        name=name, compiler_params=pltpu.CompilerParams(has_side_effects=True))(*bufs)


FLIP_C = (False, False, True)
FLIP_Y = (False, True, False)
FLIP_X = (True, False, False)


LAYER_ROWS = PACK_ROWS // DEPTH
HALF_ROWS = LAYER_ROWS // 2
SUM_ROWS = HALF_ROWS // 2
GATHER_IDS = (1, 2, 3, 4)
SCATTER_IDS = (5, 6, 7, 8)
FIRST_GATHER_ID = 9


def _mesh_pos():
    ix, iy, core = lax.axis_index("x"), lax.axis_index("y"), lax.axis_index("c")
    others = [(ix, 1 - iy), (1 - ix, iy), (1 - ix, 1 - iy)]
    return ix, iy, core, others


def _pack_big(w_in, w_out, w_up, w_down):
    pad = jnp.pad(w_in, ((0, 0), (0, 0), (0, 1024 - w_in.shape[-1])))
    return jnp.concatenate([pad, w_up, w_down, w_out], axis=1)


def _unpack_big(p):
    w_in = p[..., 0:1024, :D_IN // 4]
    return w_in, p[..., 3072:LAYER_ROWS, :], p[..., 1024:2048, :], p[..., 2048:3072, :]


def _gather_layer(packed, collective_id):
    src = jax.new_ref(packed, memory_space=pltpu.MemorySpace.HBM)
    out = jax.empty_ref(jax.ShapeDtypeStruct((4,) + packed.shape, packed.dtype), memory_space=pltpu.MemorySpace.HBM)
    dma = pltpu.SemaphoreType.DMA

    @pl.kernel(mesh=plsc.ScalarSubcoreMesh(axis_name="seq", num_cores=1), name="gather_%d" % collective_id,
               scratch_types=(dma,) * 12, compiler_params=pltpu.CompilerParams(collective_id=collective_id))
    def launch(*sems):
        send_sems, recv_sems = sems[:6], sems[6:]
        ix, iy, core, others = _mesh_pos()
        barrier = pltpu.get_barrier_semaphore()
        for peer in [(ox, oy, core) for ox, oy in others] + [(ix, iy, 1 - core)]:
            pl.semaphore_signal(barrier, inc=1, device_id=peer, device_id_type=MESH)
        pl.semaphore_wait(barrier, 4)
        half = packed.shape[0] // 2
        mine = pl.ds(core * half, half)
        theirs = pl.ds((1 - core) * half, half)

        def copy(k, src_ref, owner, rows, to):
            return pltpu.make_async_remote_copy(
                src_ref=src_ref, dst_ref=out.at[owner, rows], send_sem=send_sems[k], recv_sem=recv_sems[k],
                device_id=to, device_id_type=MESH)

        chip = 2 * ix + iy
        first = [copy(j, src.at[mine], chip, mine, (*others[j], core)) for j in range(3)]
        for cp in first:
            cp.start()
        passed = []
        for j, (ox, oy) in enumerate(others):
            owner = 2 * ox + oy
            copy(j, src.at[mine], owner, mine, (ix, iy, core)).wait_recv()
            fwd = copy(3 + j, out.at[owner, mine], owner, mine, (ix, iy, 1 - core))
            fwd.start()
            passed.append(fwd)
        for j, (ox, oy) in enumerate(others):
            copy(3 + j, src.at[theirs], 2 * ox + oy, theirs, (ix, iy, core)).wait_recv()
        for cp in first + passed:
            cp.wait_send()

    launch()
    return out[...]


def _rs_sibling(per_chip):
    hbm = pl.BlockSpec(memory_space=pltpu.HBM)

    def body(src, out, send_sem, recv_sem):
        ix, iy, core, _ = _mesh_pos()
        give = pl.ds((1 - core) * HALF_ROWS, HALF_ROWS)
        cp = pltpu.make_async_remote_copy(src_ref=src.at[pl.ds(0, 4), give], dst_ref=out, send_sem=send_sem,
                                          recv_sem=recv_sem, device_id=(ix, iy, 1 - core), device_id_type=MESH)
        cp.start()
        cp.wait()

    return pl.pallas_call(
        body, in_specs=[hbm], out_specs=hbm, out_shape=jax.ShapeDtypeStruct((4, HALF_ROWS, 1024), F32),
        scratch_shapes=[pltpu.SemaphoreType.DMA, pltpu.SemaphoreType.DMA],
        name="rs_sibling", compiler_params=pltpu.CompilerParams(has_side_effects=True))(per_chip)


def _rs_add_pair(per_chip, sib, chip, core):
    nb = HALF_ROWS // SUM_ROWS
    blk = (1, SUM_ROWS, 1024)

    def body(chip_ref, core_ref, a_ref, b_ref, own_ref, send_ref):
        total = a_ref[0] + b_ref[0]
        send_ref[0] = total.astype(BF16)

        @pl.when(pl.program_id(1) == chip_ref[0])
        def _():
            own_ref[...] = total

    return pl.pallas_call(
        body,
        grid_spec=pltpu.PrefetchScalarGridSpec(
            num_scalar_prefetch=2, grid=(nb, 4),
            in_specs=[pl.BlockSpec(blk, lambda i, k, ch, co: (k, co[0] * nb + i, 0)),
                      pl.BlockSpec(blk, lambda i, k, ch, co: (k, i, 0))],
            out_specs=[pl.BlockSpec((SUM_ROWS, 1024), lambda i, k, ch, co: (i, 0)),
                       pl.BlockSpec(blk, lambda i, k, ch, co: (k, i, 0))]),
        out_shape=[jax.ShapeDtypeStruct((HALF_ROWS, 1024), F32), jax.ShapeDtypeStruct((4, HALF_ROWS, 1024), BF16)],
        name="rs_add_pair", compiler_params=_params(("arbitrary", "arbitrary")))(
            chip.reshape(1), core.reshape(1), per_chip, sib)


def _rs_chips(pair, collective_id):
    src_ref = jax.new_ref(pair, memory_space=pltpu.MemorySpace.HBM)
    got_ref = jax.empty_ref(jax.ShapeDtypeStruct((3,) + pair.shape[1:], pair.dtype), memory_space=pltpu.MemorySpace.HBM)
    dma = pltpu.SemaphoreType.DMA

    @pl.kernel(mesh=plsc.ScalarSubcoreMesh(axis_name="seq", num_cores=1), name="rs_chips_%d" % collective_id,
               scratch_types=(dma,) * 6, compiler_params=pltpu.CompilerParams(collective_id=collective_id))
    def launch(s0, s1, s2, r0, r1, r2):
        ix, iy, core, others = _mesh_pos()
        barrier = pltpu.get_barrier_semaphore()
        for ox, oy in others:
            pl.semaphore_signal(barrier, inc=1, device_id=(ox, oy, core), device_id_type=MESH)
        pl.semaphore_wait(barrier, 3)
        copies = []
        for j, ((ox, oy), send_sem, recv_sem) in enumerate(zip(others, (s0, s1, s2), (r0, r1, r2))):
            cp = pltpu.make_async_remote_copy(src_ref=src_ref.at[2 * ox + oy], dst_ref=got_ref.at[j], send_sem=send_sem,
                                              recv_sem=recv_sem, device_id=(ox, oy, core), device_id_type=MESH)
            cp.start()
            copies.append(cp)
        for cp in copies:
            cp.wait()

    launch()
    return got_ref[...]


def _rs_add_chips(own, got, core, layer, buf):
    nb = HALF_ROWS // SUM_ROWS
    blk = (1, SUM_ROWS, 1024)

    def body(core_ref, a_ref, g0_ref, g1_ref, g2_ref, *rest):
        o_ref = rest[-1]
        o_ref[0, 0] = ((a_ref[...] + g0_ref[0].astype(F32)) + g1_ref[0].astype(F32)) + g2_ref[0].astype(F32)

    aliased = [] if buf is None else [buf]
    return pl.pallas_call(
        body,
        grid_spec=pltpu.PrefetchScalarGridSpec(
            num_scalar_prefetch=1, grid=(nb,),
            in_specs=[pl.BlockSpec((SUM_ROWS, 1024), lambda i, c: (i, 0))]
            + [pl.BlockSpec(blk, functools.partial(lambda j, i, c: (j, i, 0), j)) for j in range(3)]
            + [pl.BlockSpec(memory_space=pl.ANY)] * len(aliased),
            out_specs=pl.BlockSpec((1, 1, SUM_ROWS, 1024), lambda i, c: (layer, c[0], i, 0))),
        out_shape=jax.ShapeDtypeStruct((DEPTH, 2, HALF_ROWS, 1024), F32),
        input_output_aliases={} if buf is None else {5: 0},
        name="rs_add_chips", compiler_params=_params(("arbitrary",)))(core.reshape(1), own, got, got, got, *aliased)


def _rs_share(buf):
    hbm = pl.BlockSpec(memory_space=pltpu.HBM)

    def body(src, out, send_sem, recv_sem):
        ix, iy, core, _ = _mesh_pos()
        layers = pl.ds(0, DEPTH)
        cp = pltpu.make_async_remote_copy(src_ref=src.at[layers, core], dst_ref=out.at[layers, core], send_sem=send_sem,
                                          recv_sem=recv_sem, device_id=(ix, iy, 1 - core), device_id_type=MESH)
        cp.start()
        pltpu.make_async_remote_copy(src_ref=src.at[layers, 1 - core], dst_ref=out.at[layers, 1 - core],
                                     send_sem=send_sem, recv_sem=recv_sem, device_id=(ix, iy, core),
                                     device_id_type=MESH).wait_recv()
        cp.wait_send()

    return pl.pallas_call(
        body, in_specs=[hbm], out_specs=hbm, out_shape=jax.ShapeDtypeStruct(buf.shape, buf.dtype),
        input_output_aliases={0: 0}, scratch_shapes=[pltpu.SemaphoreType.DMA, pltpu.SemaphoreType.DMA],
        name="rs_share", compiler_params=pltpu.CompilerParams(has_side_effects=True))(buf)


def _scatter_start(per_chip, chip, core, layer):
    own, pair = _rs_add_pair(per_chip, _rs_sibling(per_chip), chip, core)
    return own, _rs_chips(pair, SCATTER_IDS[layer])


def _allreduce_small(tag, buf):
    for name, flip in (("c", FLIP_C), ("y", FLIP_Y), ("x", FLIP_X)):
        (other,) = _swap(tag + "_swap_" + name, [buf], [flip])
        (buf,) = _ew(tag + "_add_" + name, _add2_fn, [buf, other], 1)
    return buf


IN_COL_SEGMENTS = ((0, 1024, 2048), (1024, 3072, -1024), (3072, 3080, GATE0 - 3072), (3080, D_IN, COL_CX - 3080))


def _orig_cols(w_pad, lo, hi):
    parts = []
    for first, last, shift in IN_COL_SEGMENTS:
        a, b = max(lo, first), min(hi, last)
        if a < b:
            parts.append(w_pad[..., a + shift:b + shift])
    return parts


def _block_diag(w):
    n, d = w.shape[0], w.shape[1]
    on_diag = jnp.arange(n)[:, None] == jnp.arange(n)[None, :]
    return jnp.where(on_diag[:, None, :, None], w[:, :, None, :], 0.0).reshape(n * d, n * d)


def _diag_blocks(w):
    return jnp.stack([w[64 * n:64 * (n + 1), 64 * n:64 * (n + 1)] for n in range(4)], axis=0)


def _gate_row(v):
    return jnp.concatenate([jnp.zeros((4,), F32), v, jnp.zeros((120,), F32)]).reshape(1, 128)


SMALL = ("norm1_g", "hgrn_lb_logits", "hgrn_norm_g", "gdn_conv_w", "gdn_a_log", "gdn_dt_bias", "gdn_norm_g",
         "lru_conv_w", "lru_conv_b", "lru_w_a", "lru_b_a", "lru_w_x", "lru_b_x", "lru_lambda", "norm2_g",
         "final_norm_g")


def _rows_of(shape):
    size = 1
    for d in shape:
        size *= d
    return size, -(-size // 1024) * 8


def _flatten_small(tree):
    parts = []
    for k in sorted(tree):
        size, nr = _rows_of(tree[k].shape)
        parts.append(jnp.pad(tree[k].reshape(-1), (0, nr * 128 - size)).reshape(nr, 128))
    used = sum(pt.shape[0] for pt in parts)
    parts.append(jnp.zeros((-(-used // 512) * 512 - used, 128), F32))
    return jnp.concatenate(parts, axis=0)


def _unflatten_small(buf, shapes):
    out, r0 = {}, 0
    for k in sorted(shapes):
        size, nr = _rows_of(shapes[k])
        out[k] = buf[r0:r0 + nr].reshape(-1)[:size].reshape(shapes[k])
        r0 += nr
    return out


def _layer_fwd(l, x, p, w_in, late):
    sv = {"x0": x}
    (h,) = _rowwise_fwd("norm1", _rmsnorm_fn, [p["norm1_g"]], [x], [D_MODEL], [BF16], 512)
    proj, w_out, w_up, w_down = late(_matmul("proj_in", h, w_in, "nn"))
    sv["h"], sv["proj"] = h, proj
    aq, af, ag = _cols(proj, COL_AQ, A_W), _cols(proj, COL_AF, A_W), _cols(proj, COL_AG, A_W)
    ai = lax.slice_in_dim(proj, COL_AI, COL_AI + A_W, axis=1)
    bz, gate, cy = _cols(proj, COL_BZ, B_W), _cols(proj, GATE0, 128), _cols(proj, COL_CY, C_W)
    q, k, lf = _rowwise_fwd("hgrn_pre", functools.partial(_hgrn_pre_fn, l), [p["lb_logits"]], [aq, af],
                            [A_W] * 3, [F32] * 3, 512)
    o_a, sts_a = _hgrn_fwd("hgrn_chunk", q, k, ai, lf)
    (mixed,) = _rowwise_fwd("hgrn_post", _hgrn_post_fn, [p["hgrn_norm_g"]], [o_a, ag], [A_W], [BF16], 512,
                            into=(B_W, D_MODEL, None))
    sv.update(aq=aq, af=af, ai=ai, ag=ag, hq=q, hk=k, hlf=lf, o_a=o_a, sts_a=sts_a)
    qkvc = _conv_fwd("gdn_conv", proj, 0, 3 * B_W, p["gdn_conv_w"], jnp.zeros((1, 3 * B_W), F32), 256)
    gq, gk, gv, gb, gla = _rowwise_fwd("gdn_pre", _gdn_pre_fn, [p["gdn_a_log"], p["gdn_dt_bias"]], [qkvc, gate],
                                       [B_W] * 5, [F32] * 5, 256)
    *wy, sv["gdn_inv"] = _gdn_prep_fwd("gdn_prep", gq, gk, gv, gb, gla)
    o_b, sts_b = _gdn_rec_fwd("gdn_rec", *wy)
    sv["wy"] = wy
    (mixed,) = _rowwise_fwd("gdn_post", _gdn_post_fn, [p["gdn_norm_g"]], [o_b, bz], [B_W], [BF16], 512,
                            into=(0, D_MODEL, mixed))
    sv.update(qkvc=qkvc, gate=gate, bz=bz, gq=gq, gk=gk, gv=gv, gb=gb, gla=gla, o_b=o_b, sts_b=sts_b)
    xc = _conv_fwd("lru_conv", proj, COL_CX, C_W, p["lru_conv_w"], p["lru_conv_b"], 512)
    a, b = _rowwise_fwd("lru_pre", _lru_pre_fn, [p["lru_w_a"], p["lru_b_a"], p["lru_w_x"], p["lru_b_x"], p["lru_lambda"]],
                        [xc], [C_W] * 2, [F32] * 2, 512)
    hs, h_prev = _lru_scan_fwd("lru_scan", a, b)
    (mixed,) = _rowwise_fwd("lru_post", _lru_post_fn, [], [hs, cy], [C_W], [BF16], 512,
                            into=(B_W + A_W, D_MODEL, mixed))
    sv.update(xc=xc, la=a, hs=hs, h_prev=h_prev, cy=cy)
    x1 = _matmul("proj_out", mixed, w_out, "nn", res=x)
    (h2,) = _rowwise_fwd("norm2", _rmsnorm_fn, [p["norm2_g"]], [x1], [D_MODEL], [BF16], 512)
    up, act = _matmul("mlp_up", h2, w_up, "nn", also=_relu2)
    x2 = _matmul("mlp_down", act, w_down, "nn", res=x1)
    sv.update(mixed=mixed, x1=x1, h2=h2, up=up, act=act)
    return x2, sv, (w_in, w_out, w_up, w_down)


def _layer_bwd(l, dx, dx16, p, sv, w_in, w_out, w_up, w_down):
    gs = {}
    dup = _matmul("mlp_down_dx", dx16, w_down, "nt", epi=_drelu2, epi_in=sv["up"], out_dtype=BF16)
    d_w_down = _matmul("mlp_down_dw", sv["act"], dx16, "tn")
    dh2 = _matmul("mlp_up_dx", dup, w_up, "nt")
    d_w_up = _matmul("mlp_up_dw", sv["h2"], dup, "tn")
    (gs["norm2_g"],), (dx1, dx1_16) = _rowwise_bwd("norm2_bwd", _rmsnorm_fn, [p["norm2_g"]], [sv["x1"]], [dh2], 512,
                                                   res=dx, copy16=True)
    dmixed = _matmul("proj_out_dx", dx1_16, w_out, "nt")
    d_w_out = _matmul("proj_out_dw", sv["mixed"], dx1_16, "tn")
    dy_b, dy_a, dy_c = _cols(dmixed, 0, B_W), _cols(dmixed, B_W, A_W), _cols(dmixed, B_W + A_W, C_W)
    (gs["hgrn_norm_g"],), (do_a, dproj) = _rowwise_bwd("hgrn_post_bwd", _hgrn_post_fn, [p["hgrn_norm_g"]],
                                                       [sv["o_a"], sv["ag"]], [dy_a], 512, pack=([1], COL_AG, A_W, None))
    dq, dk, dproj, dlf = _hgrn_bwd("hgrn_chunk_bwd", sv["hq"], sv["hk"], sv["ai"], sv["hlf"], sv["sts_a"], do_a,
                                   dproj, COL_AI)
    (gs["lb_logits"],), (dproj,) = _rowwise_bwd("hgrn_pre_bwd", functools.partial(_hgrn_pre_fn, l), [p["lb_logits"]],
                                                [sv["aq"], sv["af"]], [dq, dk, dlf], 512,
                                                pack=([0, 1], COL_AQ, 2 * A_W, dproj))
    (gs["gdn_norm_g"],), (do_b, dproj) = _rowwise_bwd("gdn_post_bwd", _gdn_post_fn, [p["gdn_norm_g"]],
                                                      [sv["o_b"], sv["bz"]], [dy_b], 512, pack=([1], COL_BZ, B_W, dproj))
    d_wy = _gdn_rec_bwd("gdn_rec_bwd", *sv["wy"], sv["sts_b"], do_b)
    dgq, dgk, dgv, dgb, dgla = _gdn_prep_bwd("gdn_prep_bwd", sv["gq"], sv["gk"], sv["gv"], sv["gb"], sv["gla"],
                                             sv["gdn_inv"], d_wy)
    (gs["gdn_a_log"], gs["gdn_dt_bias"]), (dqkvc, dproj) = _rowwise_bwd(
        "gdn_pre_bwd", _gdn_pre_fn, [p["gdn_a_log"], p["gdn_dt_bias"]], [sv["qkvc"], sv["gate"]],
        [dgq, dgk, dgv, dgb, dgla], 256, pack=([1], GATE0, D_IN_PAD - GATE0, dproj))
    dproj, gs["gdn_conv_w"], _ = _conv_bwd("gdn_conv_bwd", dqkvc, sv["proj"], 0, 3 * B_W, p["gdn_conv_w"], dproj, 256)
    _, (dhs, dproj) = _rowwise_bwd("lru_post_bwd", _lru_post_fn, [], [sv["hs"], sv["cy"]], [dy_c], 512,
                                   pack=([1], COL_CY, C_W, dproj))
    g = _lru_scan_bwd("lru_scan_bwd", sv["la"], dhs)
    lru_params = [p["lru_w_a"], p["lru_b_a"], p["lru_w_x"], p["lru_b_x"], p["lru_lambda"]]
    dps, (dxc,) = _rowwise_bwd("lru_pre_bwd", _lru_pre_fn, lru_params, [sv["xc"]], [g, sv["h_prev"]], 512,
                               dout_fn=lambda gt, h_prev: (gt * h_prev, gt))
    gs["lru_w_a"], gs["lru_b_a"], gs["lru_w_x"], gs["lru_b_x"], gs["lru_lambda"] = dps
    dproj, gs["lru_conv_w"], gs["lru_conv_b"] = _conv_bwd("lru_conv_bwd", dxc, sv["proj"], COL_CX, C_W, p["lru_conv_w"],
                                                          dproj, 512)
    dh = _matmul("proj_in_dx", dproj, w_in, "nt")
    d_w_in = _matmul("proj_in_dw", sv["h"], dproj, "tn")
    (gs["norm1_g"],), (dx0, dx0_16) = _rowwise_bwd("norm1_bwd", _rmsnorm_fn, [p["norm1_g"]], [sv["x0"]], [dh], 512,
                                                   res=dx1, copy16=True)
    return dx0, dx0_16, dict(w_in=d_w_in, w_out=d_w_out, w_up=d_w_up, w_down=d_w_down), gs


def _layer_params(l, lb_logits, hgrn_norm_g, gdn_conv_w, gdn_a_log, gdn_dt_bias, gdn_norm_g, lru_conv_w, lru_conv_b,
                  lru_w_a, lru_b_a, lru_w_x, lru_b_x, lru_lambda, norm1_g, norm2_g):
    row = lambda v: v.reshape(1, -1)
    return dict(
        norm1_g=row(norm1_g[l]), norm2_g=row(norm2_g[l]), lb_logits=lb_logits,
        hgrn_norm_g=row(jnp.tile(hgrn_norm_g[l], A_W // A_HD)),
        gdn_conv_w=gdn_conv_w[l], gdn_a_log=_gate_row(gdn_a_log[l]), gdn_dt_bias=_gate_row(gdn_dt_bias[l]),
        gdn_norm_g=row(jnp.tile(gdn_norm_g[l], B_NH)),
        lru_conv_w=lru_conv_w[l], lru_conv_b=row(lru_conv_b[l]), lru_w_a=_block_diag(lru_w_a[l]), lru_b_a=row(lru_b_a[l]),
        lru_w_x=_block_diag(lru_w_x[l]), lru_b_x=row(lru_b_x[l]), lru_lambda=row(lru_lambda[l]))


def _small_grads_to_reference(gs_layers, d_final_g, d_logits):
    st = lambda k, f=lambda z: z: jnp.stack([f(g[k]) for g in gs_layers], axis=0)
    vec = lambda z: z.reshape(-1)
    return dict(
        norm1_g=st("norm1_g", vec), norm2_g=st("norm2_g", vec), hgrn_lb_logits=d_logits,
        hgrn_norm_g=st("hgrn_norm_g", lambda z: z.reshape(A_W // A_HD, A_HD).sum(0)),
        gdn_conv_w=st("gdn_conv_w"), gdn_a_log=st("gdn_a_log", lambda z: z[0, 4:8]),
        gdn_dt_bias=st("gdn_dt_bias", lambda z: z[0, 4:8]),
        gdn_norm_g=st("gdn_norm_g", lambda z: z.reshape(B_NH, B_HD).sum(0)),
        lru_conv_w=st("lru_conv_w"), lru_conv_b=st("lru_conv_b", vec), lru_w_a=st("lru_w_a", _diag_blocks),
        lru_b_a=st("lru_b_a", vec), lru_w_x=st("lru_w_x", _diag_blocks), lru_b_x=st("lru_b_x", vec),
        lru_lambda=st("lru_lambda", vec), final_norm_g=d_final_g.reshape(-1))


def _local_step(x, tgt, small, layer_weights, on_layer_grads=None):
    layer_p = [_layer_params(l, small["hgrn_lb_logits"], small["hgrn_norm_g"], small["gdn_conv_w"], small["gdn_a_log"],
                             small["gdn_dt_bias"], small["gdn_norm_g"], small["lru_conv_w"], small["lru_conv_b"],
                             small["lru_w_a"], small["lru_b_a"], small["lru_w_x"], small["lru_b_x"], small["lru_lambda"],
                             small["norm1_g"], small["norm2_g"]) for l in range(DEPTH)]
    saved, weights = [], []
    for l in range(DEPTH):
        x, w_in, late = layer_weights(l, x)
        x, sv, w = _layer_fwd(l, x, layer_p[l], w_in, late)
        saved.append(sv)
        weights.append(w)
    loss, dx, d_final_g, dx16 = _loss_and_grad("final_loss", small["final_norm_g"].reshape(1, -1), x, tgt)
    big = [None] * DEPTH
    gs_layers = [None] * DEPTH
    d_logits = jnp.zeros_like(small["hgrn_lb_logits"])
    for l in reversed(range(DEPTH)):
        dx, dx16, big[l], gs_layers[l] = _layer_bwd(l, dx, dx16, layer_p[l], saved[l], *weights[l])
        d_logits = d_logits + gs_layers[l]["lb_logits"]
        if on_layer_grads is not None:
            big[l] = on_layer_grads(l, big[l])
    return loss, dx, big, _small_grads_to_reference(gs_layers, d_final_g, d_logits)


def kernel(x, norm1_g, w_in, hgrn_lb_logits, hgrn_norm_g, gdn_conv_w, gdn_a_log, gdn_dt_bias, gdn_norm_g, lru_conv_w, lru_conv_b, lru_w_a, lru_b_a, lru_w_x, lru_b_x, lru_lambda, w_out, norm2_g, w_up, w_down, final_norm_g, loss_target, m_norm1_g, m_w_in, m_hgrn_lb_logits, m_hgrn_norm_g, m_gdn_conv_w, m_gdn_a_log, m_gdn_dt_bias, m_gdn_norm_g, m_lru_conv_w, m_lru_conv_b, m_lru_w_a, m_lru_b_a, m_lru_w_x, m_lru_b_x, m_lru_lambda, m_w_out, m_norm2_g, m_w_up, m_w_down, m_final_norm_g, v_norm1_g, v_w_in, v_hgrn_lb_logits, v_hgrn_norm_g, v_gdn_conv_w, v_gdn_a_log, v_gdn_dt_bias, v_gdn_norm_g, v_lru_conv_w, v_lru_conv_b, v_lru_w_a, v_lru_b_a, v_lru_w_x, v_lru_b_x, v_lru_lambda, v_w_out, v_norm2_g, v_w_up, v_w_down, v_final_norm_g):
    args = dict(locals())
    ix, iy, core = lax.axis_index("x"), lax.axis_index("y"), lax.axis_index("c")
    chip = 2 * ix + iy

    packed = _pack_big(w_in, w_out, w_up, w_down).astype(BF16)
    first_w_in = _gather_layer(packed[0, :1024], FIRST_GATHER_ID)
    gathered = [_gather_layer(packed[l, 1024 * (l == 0):], GATHER_IDS[l]) for l in range(DEPTH)]

    def with_own(got, own):
        return lax.dynamic_update_slice(got, own[None], (chip, 0, 0))

    def full_w_in(shares):
        def cols(lo, hi):
            out = []
            for k in range(4):
                a, b = max(lo, k * n_in), min(hi, (k + 1) * n_in)
                if a < b:
                    out.append(shares[k, :1024, a - k * n_in:b - k * n_in])
            return out

        pieces = [pc for first, last, _ in sorted(IN_COL_SEGMENTS, key=lambda seg: seg[0] + seg[2])
                  for pc in cols(first, last)]
        return jnp.concatenate(pieces + [jnp.zeros((1024, D_IN_PAD - D_IN), shares.dtype)], axis=-1)

    def full_rest(shares):
        return (jnp.concatenate([shares[k, 2048:] for k in W_OUT_CHIP_ORDER], axis=0),
                jnp.concatenate([shares[k, :1024] for k in range(4)], axis=-1),
                jnp.concatenate([shares[k, 1024:2048] for k in range(4)], axis=0))

    def layer_weights(l, x):
        got, own = gathered[l], packed[l]
        if l:
            got, own, x = lax.optimization_barrier((got, own, x))
            shares = with_own(got, own)
            rest = full_rest(shares[:, 1024:])
            return x, full_w_in(shares), lambda v: (v,) + rest

        def late(v):
            got_v, own_v, v = lax.optimization_barrier((got, own, v))
            return (v,) + full_rest(with_own(got_v, own_v[1024:]))

        return x, full_w_in(with_own(first_w_in, own[:1024])), late

    n_gc, n_lc = gdn_conv_w.shape[-1], lru_conv_w.shape[-1]
    conv_full = dict(
        gdn_conv_w=lax.dynamic_update_slice(jnp.zeros((DEPTH, 4, 4 * n_gc), F32), gdn_conv_w * 0.5, (0, 0, chip * n_gc)),
        lru_conv_w=lax.dynamic_update_slice(jnp.zeros((DEPTH, 4, 4 * n_lc), F32), lru_conv_w * 0.5, (0, 0, chip * n_lc)))
    conv_shapes = {k: v.shape for k, v in conv_full.items()}
    conv_full = _unflatten_small(_allreduce_small("convw", _flatten_small(conv_full)), conv_shapes)

    small = {k: args[k] for k in SMALL}
    small.update(conv_full)

    n_in = D_IN // 4

    def start_scatter(l, b):
        pieces = []
        for k in range(4):
            cols = jnp.concatenate(_orig_cols(b["w_in"], k * n_in, (k + 1) * n_in), axis=1)
            r0 = 256 * W_OUT_CHIP_ORDER.index(k)
            pieces += [jnp.pad(cols, ((0, 0), (0, 1024 - n_in))), b["w_up"][:, k * 1024:(k + 1) * 1024],
                       b["w_down"][k * 1024:(k + 1) * 1024], b["w_out"][r0:r0 + 256]]
        per_chip = jnp.concatenate(pieces, axis=0).reshape(4, LAYER_ROWS, 1024)
        return _scatter_start(per_chip, chip, core, l)

    loss, grad_x, started, sg = _local_step(x[0], loss_target[0], small, layer_weights, start_scatter)

    sg["loss"] = loss[0, :1]
    shapes = {k: v.shape for k, v in sg.items()}
    sg = _unflatten_small(_allreduce_small("smallgrad", _flatten_small(sg)), shapes)
    loss_out = sg.pop("loss")[0]

    buf = None
    for l in reversed(range(DEPTH)):
        own, got = started[l]
        buf = _rs_add_chips(own, got, core, l, buf)
    g_in, g_out, g_up, g_down = _unpack_big(_rs_share(buf).reshape(DEPTH, LAYER_ROWS, 1024))

    grads = dict(sg)
    grads["gdn_conv_w"] = lax.dynamic_slice_in_dim(sg["gdn_conv_w"], chip * n_gc, n_gc, axis=2)
    grads["lru_conv_w"] = lax.dynamic_slice_in_dim(sg["lru_conv_w"], chip * n_lc, n_lc, axis=2)
    grads.update(w_in=g_in, w_out=g_out, w_up=g_up, w_down=g_down)

    names = ['norm1_g', 'w_in', 'hgrn_lb_logits', 'hgrn_norm_g', 'gdn_conv_w', 'gdn_a_log', 'gdn_dt_bias', 'gdn_norm_g',
             'lru_conv_w', 'lru_conv_b', 'lru_w_a', 'lru_b_a', 'lru_w_x', 'lru_b_x', 'lru_lambda', 'w_out', 'norm2_g',
             'w_up', 'w_down', 'final_norm_g']
    big_names = ("w_in", "w_out", "w_up", "w_down")
    delta, new_m, new_v = {}, {}, {}
    for k in big_names:
        delta[k], new_m[k], new_v[k] = _adam("adam_" + k, args[k], grads[k], args["m_" + k], args["v_" + k])
    small_names = [k for k in names if k not in big_names]
    shapes = {k: args[k].shape for k in small_names}
    flat = [_flatten_small({k: src(k) for k in small_names})
            for src in (lambda k: args[k], lambda k: grads[k], lambda k: args["m_" + k], lambda k: args["v_" + k])]
    outs = _ew("adam_small", _adam_fn, flat, 3)
    for dst, o in zip((delta, new_m, new_v), outs):
        dst.update(_unflatten_small(o, shapes))
    return (loss_out, grad_x[None], *[grads[k] for k in names], *[delta[k] for k in names],
            *[new_m[k] for k in names], *[new_v[k] for k in names])
```

```python
import functools

import jax
import jax.numpy as jnp
from jax import lax
from jax.experimental import pallas as pl
from jax.experimental.pallas import tpu as pltpu
from jax.experimental.pallas import tpu_sc as plsc

F32 = jnp.float32
BF16 = jnp.bfloat16
MESH = pl.DeviceIdType.MESH

DEPTH = 4
D_MODEL = 1024
A_W, B_W, C_W = 256, 512, 256
A_HD, B_HD = 64, 128
B_NH = B_W // B_HD
D_IN = 3592
D_IN_PAD = 3840
COL_BZ, COL_AQ, COL_AF, COL_AI, COL_AG, COL_CX, COL_CY = 1536, 2048, 2304, 2560, 2816, 3072, 3328
W_OUT_CHIP_ORDER = (1, 2, 0, 3)
GATE0 = 3584
D_FF = 4096
CHUNK = 64
SUB = 16
EPS = 1e-6
TINY = 1e-30
MASKED_EXPONENT = -1e30
RG_C = 8.0
LR, B1, B2, AEPS, WD, STEP = 0.001, 0.9, 0.999, 1e-08, 0.01, 10
VMEM_LIMIT = 56 * 1024 * 1024
PACK_ROWS = 13 * 1024

NN = ((1,), (0,))
NT = ((1,), (1,))
TN = ((0,), (0,))


DIMS = {"nn": NN, "nt": NT, "tn": TN}


def _split(a, n):
    parts, r = [], a
    for i in range(n):
        p = r.astype(BF16)
        parts.append(p)
        if i + 1 < n:
            r = r - p.astype(F32)
    return parts


def _split_dot(a, b, mode, na, nb):
    (ca,), (cb,) = DIMS[mode]
    dims = ((DIMS[mode]), ((), ())) if a.ndim == 2 else (((ca + 1,), (cb + 1,)), ((0,), (0,)))
    pa, pb = _split(a, na), _split(b, nb)
    acc = None
    for i in range(na):
        for j in range(nb):
            if i + j < max(na, nb):
                t = lax.dot_general(pa[i], pb[j], dims, preferred_element_type=F32)
                acc = t if acc is None else acc + t
    return acc


@functools.partial(jax.custom_vjp, nondiff_argnums=(2, 3, 4, 5))
def _mdot(a, b, mode, na, nb, need):
    return _split_dot(a, b, mode, na, nb)


def _mdot_fwd(a, b, mode, na, nb, need):
    return _split_dot(a, b, mode, na, nb), (a, b)


def _mdot_bwd(mode, na, nb, need, res, ct):
    a, b = res
    n = max(na, nb)
    da, db = jnp.zeros_like(a), jnp.zeros_like(b)
    if mode == "nn":
        if need[0]:
            da = _split_dot(ct, b, "nt", n, nb)
        if need[1]:
            db = _split_dot(a, ct, "tn", na, n)
    elif mode == "nt":
        if need[0]:
            da = _split_dot(ct, b, "nn", n, nb)
        if need[1]:
            db = _split_dot(ct, a, "tn", n, na)
    else:
        if need[0]:
            da = _split_dot(b, ct, "nt", nb, n)
        if need[1]:
            db = _split_dot(a, ct, "nn", na, n)
    return da, db


_mdot.defvjp(_mdot_fwd, _mdot_bwd)
BOTH = (True, True)


def _bdot(a, b, mode="nn"):
    return _mdot(a, b, mode, 1, 1, BOTH)


def _group_sum(x, ones):
    return _mdot(x, ones, "nn", 2, 1, (True, False))


def _running_sum(x):
    shape = x.shape[:-2] + (CHUNK, CHUNK)
    tri = (_iota(shape, len(shape) - 1) <= _iota(shape, len(shape) - 2)).astype(F32)
    return _mdot(tri, x, "nn", 1, 3, (False, True))


def _iota(shape, d):
    return lax.broadcasted_iota(jnp.int32, shape, d)


def _block_ones(width, group):
    return (_iota((width, width), 0) // group == _iota((width, width), 1) // group).astype(F32)


def _params(sem):
    return pltpu.CompilerParams(dimension_semantics=sem, vmem_limit_bytes=VMEM_LIMIT)


def _tile(dim, prefs):
    for p in prefs:
        if dim % p == 0:
            return p
    return dim


def _whole(shape):
    return pl.BlockSpec(shape, lambda i: (0,) * len(shape))


def _rowspec(t, w, col0=0):
    cb = col0 // w
    return pl.BlockSpec((t, w), lambda i: (i, cb))


def _cols(arr, col0, width):
    assert col0 % width == 0
    return (arr, col0, width)


def _row_operands(rows, t):
    arrs, specs = [], []
    for r in rows:
        arr, col0, w = r if isinstance(r, tuple) else (r, 0, r.shape[1])
        arrs.append(arr)
        specs.append(_rowspec(t, w, col0))
    return arrs, specs


def _rowwise_fwd(name, fn, params, rows, out_widths, out_dtypes, t, into=None):
    rows, row_specs = _row_operands(rows, t)
    s = rows[0].shape[0]
    n_p, n_r = len(params), len(rows)
    n_buf = 0 if into is None or into[2] is None else 1

    def body(*refs):
        p = [r[...] for r in refs[:n_p]]
        xs = [r[...].astype(F32) for r in refs[n_p:n_p + n_r]]
        outs = fn(pl.program_id(0) * t, *p, *xs)
        for o_ref, o in zip(refs[n_p + n_r + n_buf:], outs):
            o_ref[...] = o.astype(o_ref.dtype)

    out_specs = [_rowspec(t, w) for w in out_widths]
    out_shape = [jax.ShapeDtypeStruct((s, w), d) for w, d in zip(out_widths, out_dtypes)]
    extra = []
    if into is not None:
        out_specs = [_rowspec(t, out_widths[0], into[0])]
        out_shape = [jax.ShapeDtypeStruct((s, into[1]), out_dtypes[0])]
        extra = [into[2]] * n_buf
    return pl.pallas_call(
        body, grid=(s // t,),
        in_specs=[_whole(p.shape) for p in params] + row_specs + [pl.BlockSpec(memory_space=pl.ANY)] * n_buf,
        out_specs=out_specs, out_shape=out_shape, input_output_aliases={n_p + n_r: 0} if n_buf else {},
        name=name, compiler_params=_params(("arbitrary",)))(*params, *rows, *extra)


def _into_spec(t, width, col0, rev_steps=None):
    cb = col0 // width
    assert col0 % width == 0
    if rev_steps is None:
        return pl.BlockSpec((t, width), lambda i: (i, cb))
    return pl.BlockSpec((t, width), lambda i: (rev_steps - 1 - i, cb))


def _rowwise_bwd(name, fn, params, rows, douts, t, need=None, res=None, dtypes=None, copy16=False, pack=None,
                 dout_fn=None):
    rows, row_specs = _row_operands(rows, t)
    douts, dout_specs = _row_operands(douts, t)
    s = rows[0].shape[0]
    n_p, n_r, n_o = len(params), len(rows), len(douts)
    need = [True] * n_r if need is None else need
    n_res = 0 if res is None else 1
    n_buf = 0 if pack is None or pack[3] is None else 1

    def body(*refs):
        p = [r[...] for r in refs[:n_p]]
        xs = [r[...].astype(F32) for r in refs[n_p:n_p + n_r]]
        dys = [r[...].astype(F32) for r in refs[n_p + n_r:n_p + n_r + n_o]]
        k = n_p + n_r + n_o
        res_ref = refs[k] if n_res else None
        k += n_res + n_buf
        dp_refs = refs[k:k + n_p]
        dx_refs = list(refs[k + n_p:])
        row0 = pl.program_id(0) * t
        _, vjp = jax.vjp(lambda *a: tuple(fn(row0, *a)), *p, *xs)
        g = vjp(tuple(dys) if dout_fn is None else tuple(dout_fn(*dys)))

        @pl.when(pl.program_id(0) == 0)
        def _():
            for r in dp_refs:
                r[...] = jnp.zeros_like(r)

        for r, gp in zip(dp_refs, g[:n_p]):
            r[...] += gp
        gx = [gi for gi, nd in zip(g[n_p:], need) if nd]
        if n_res:
            gx[0] = gx[0] + res_ref[...]
        if copy16:
            gx.append(gx[0])
        if pack is not None:
            pieces = [gx[j].astype(BF16) for j in pack[0]]
            missing = pack[2] - sum(pc.shape[1] for pc in pieces)
            if missing:
                pieces.append(jnp.zeros((t, missing), BF16))
            dx_refs.pop()[...] = pieces[0] if len(pieces) == 1 else jnp.concatenate(pieces, axis=1)
            gx = [gi for j, gi in enumerate(gx) if j not in pack[0]]
        for r, gi in zip(dx_refs, gx):
            r[...] = gi.astype(r.dtype)

    widths = [sp.block_shape[1] for sp, nd in zip(row_specs, need) if nd]
    dtypes = [F32] * len(widths) if dtypes is None else list(dtypes)
    if copy16:
        widths, dtypes = widths + widths[:1], dtypes + [BF16]
    out_specs = [_rowspec(t, w) for w in widths]
    out_shape = [jax.ShapeDtypeStruct((s, w), d) for w, d in zip(widths, dtypes)]
    extra, aliases = [], {}
    if pack is not None:
        keep = [j for j in range(len(widths)) if j not in pack[0]]
        out_specs = [out_specs[j] for j in keep] + [_into_spec(t, pack[2], pack[1])]
        out_shape = [out_shape[j] for j in keep] + [jax.ShapeDtypeStruct((s, D_IN_PAD), BF16)]
        if n_buf:
            extra = [pack[3]]
            aliases = {n_p + n_r + n_o + n_res: n_p + len(out_shape) - 1}
    outs = pl.pallas_call(
        body, grid=(s // t,),
        in_specs=([_whole(p.shape) for p in params] + row_specs + dout_specs
                  + ([_rowspec(t, res.shape[1])] if n_res else []) + [pl.BlockSpec(memory_space=pl.ANY)] * n_buf),
        out_specs=[_whole(p.shape) for p in params] + out_specs,
        out_shape=[jax.ShapeDtypeStruct(p.shape, F32) for p in params] + out_shape, input_output_aliases=aliases,
        name=name, compiler_params=_params(("arbitrary",)))(*params, *rows, *douts, *([res] if n_res else []), *extra)
    return list(outs[:n_p]), list(outs[n_p:])


def _rmsnorm_fn(row0, g, x):
    y = x * lax.rsqrt(jnp.mean(x * x, axis=-1, keepdims=True) + EPS)
    return (y * g,)


def _hgrn_pre_fn(layer, row0, logits, aq, af):
    m = jnp.max(logits, axis=0, keepdims=True)
    e = jnp.exp(logits - m)
    p = e / jnp.sum(e, axis=0, keepdims=True)
    if layer == 0:
        lb = jnp.zeros((1, A_W), F32)
    else:
        acc = p[1:2]
        for j in range(2, layer + 1):
            acc = acc + p[j:j + 1]
        lb = jnp.minimum(jnp.maximum(acc, 0.0), 1.0 - EPS)
    sig = jax.nn.sigmoid(af)
    f = lb + (1.0 - lb) * sig
    log_f = jnp.log(jnp.maximum(f, TINY))
    k = (1.0 - lb) * jax.nn.sigmoid(-af)
    return jax.nn.silu(aq), k, log_f


def _hgrn_post_fn(row0, g, o, ag):
    ms = _group_sum(o * o, _block_ones(A_W, A_HD)) * (1.0 / A_HD)
    return (o * lax.rsqrt(ms + EPS) * g * jax.nn.silu(ag),)


def _gdn_pre_fn(row0, a_log, dt_bias, qkvc, gate):
    act = jax.nn.silu(qkvc)
    q, k, v = act[:, :B_W], act[:, B_W:2 * B_W], act[:, 2 * B_W:]
    bo = _block_ones(B_W, B_HD)
    q = q * lax.rsqrt(_group_sum(q * q, bo) + EPS) * (B_HD ** -0.5)
    k = k * lax.rsqrt(_group_sum(k * k, bo) + EPS)
    beta = jax.nn.sigmoid(gate)
    log_alpha = -jnp.exp(a_log) * jax.nn.softplus(gate + dt_bias)
    t = gate.shape[0]
    spread = lambda z, lane0: jnp.concatenate(
        [jnp.broadcast_to(z[:, lane0 + h:lane0 + h + 1], (t, B_HD)) for h in range(B_NH)], axis=1)
    return q, k, v, spread(beta, 0), spread(log_alpha, B_NH)


def _gdn_post_fn(row0, g, o, bz):
    ms = _group_sum(o * o, _block_ones(B_W, B_HD)) * (1.0 / B_HD)
    return (o * lax.rsqrt(ms + EPS) * g * jax.nn.silu(bz),)


def _lru_pre_fn(row0, w_a, b_a, w_x, b_x, lam, xc):
    r = jax.nn.sigmoid(_mdot(xc, w_a, "nn", 2, 2, BOTH) + b_a)
    i = jax.nn.sigmoid(_mdot(xc, w_x, "nn", 2, 2, BOTH) + b_x)
    log_a = -RG_C * r * jax.nn.softplus(-lam)
    a = jnp.exp(log_a)
    one_minus = -jnp.tanh(log_a) * (a * a + 1.0)
    mult = jnp.sqrt(jnp.maximum(one_minus, EPS))
    pos = row0 + _iota(xc.shape, 0)
    mult = jnp.where(pos == 0, 1.0, mult)
    return a, mult * i * xc


def _lru_post_fn(row0, h, cy):
    return (jax.nn.gelu(cy, approximate=True) * h,)


def _relu2(u):
    r = jnp.maximum(u, 0.0)
    return r * r


def _drelu2(acc, u):
    return acc * (2.0 * jnp.maximum(u, 0.0))


MM_VMEM_BUDGET = 44 * 1024 * 1024


def _mm_tiles(mode, m, n, k, a_bytes, b_bytes, n_extra):
    best = None
    for tk in {k, 4096, 2048, 1024, 512, 256}:
        for tn in {n, 1280, 1024, 768, 512, 256, 128}:
            for tm in {m, 1024, 512, 256, 128}:
                if m % tm or n % tn or k % tk or tm > 1024:
                    continue
                tiles = tm * tk * a_bytes + tk * tn * b_bytes + (1 + n_extra) * tm * tn * 4
                if 2 * tiles + (tm * tn * 4 if tk < k else 0) > MM_VMEM_BUDGET:
                    continue
                a_reads = 1 if tk == k else n // tn
                b_reads = 1 if (tk == k and tn == n) else m // tm
                cost = m * k * a_bytes * a_reads + k * n * b_bytes * b_reads + m * n * 4 * (k // tk - 1)
                if best is None or (cost, -tm) < best[0]:
                    best = ((cost, -tm), (tm, tn, tk))
    if best is None:
        raise ValueError("no matmul tiling fits VMEM")
    return best[1]


def _matmul(name, a, b, mode, a_fn=None, res=None, epi=None, epi_in=None, out_dtype=F32, also=None):
    if mode == "nn":
        (m, k), n = a.shape, b.shape[1]
    elif mode == "nt":
        (m, k), n = a.shape, b.shape[0]
    else:
        (k, m), n = a.shape, b.shape[1]
    n_extra = (res is not None) + (epi_in is not None) + (also is not None)
    tm, tn, tk = _mm_tiles(mode, m, n, k, a.dtype.itemsize, b.dtype.itemsize, n_extra)
    nk = k // tk
    a_spec = {"nn": pl.BlockSpec((tm, tk), lambda i, j, kk: (i, kk)),
              "nt": pl.BlockSpec((tm, tk), lambda i, j, kk: (i, kk)),
              "tn": pl.BlockSpec((tk, tm), lambda i, j, kk: (kk, i))}[mode]
    b_spec = {"nn": pl.BlockSpec((tk, tn), lambda i, j, kk: (kk, j)),
              "nt": pl.BlockSpec((tn, tk), lambda i, j, kk: (j, kk)),
              "tn": pl.BlockSpec((tk, tn), lambda i, j, kk: (kk, j))}[mode]
    o_spec = pl.BlockSpec((tm, tn), lambda i, j, kk: (i, j))
    dims = {"nn": NN, "nt": NT, "tn": TN}[mode]
    extra = [x for x in (res, epi_in) if x is not None]

    def body(*refs):
        a_ref, b_ref = refs[0], refs[1]
        rest = list(refs[2:])
        res_ref = rest.pop(0) if res is not None else None
        epi_ref = rest.pop(0) if epi_in is not None else None
        o_ref = rest.pop(0)
        also_ref = rest.pop(0) if also is not None else None
        acc = rest[0] if nk > 1 else None
        kk = pl.program_id(2)
        at = a_ref[...]
        if a_fn is not None:
            at = a_fn(at.astype(F32))
        part = lax.dot_general(at.astype(BF16), b_ref[...].astype(BF16), (dims, ((), ())),
                               preferred_element_type=F32)

        def finish(out):
            if res_ref is not None:
                out = out + res_ref[...]
            if epi is not None:
                out = epi(out, epi_ref[...])
            o_ref[...] = out.astype(o_ref.dtype)
            if also is not None:
                also_ref[...] = also(out).astype(also_ref.dtype)

        if nk == 1:
            finish(part)
        else:
            @pl.when(kk == 0)
            def _():
                acc[...] = part

            @pl.when(jnp.logical_and(kk > 0, kk < nk - 1))
            def _():
                acc[...] += part

            @pl.when(kk == nk - 1)
            def _():
                finish(acc[...] + part)

    out_shape = [jax.ShapeDtypeStruct((m, n), out_dtype)] + ([jax.ShapeDtypeStruct((m, n), BF16)] if also else [])
    outs = pl.pallas_call(
        body, grid=(m // tm, n // tn, nk),
        in_specs=[a_spec, b_spec] + [o_spec] * len(extra), out_specs=[o_spec] * len(out_shape), out_shape=out_shape,
        scratch_shapes=[pltpu.VMEM((tm, tn), F32)] if nk > 1 else [],
        name=name, compiler_params=_params(("arbitrary", "arbitrary", "arbitrary")))(a, b, *extra)
    return outs if also else outs[0]


HALO = 8


def _conv_fwd(name, x, col0, width, w, bias, t=512):
    s = x.shape[0]
    cb = col0 // width
    hb = t // HALO

    def body(x_ref, halo_ref, w_ref, b_ref, y_ref):
        i = pl.program_id(0)
        halo = jnp.where(i == 0, 0.0, halo_ref[...])
        xp = jnp.concatenate([halo, x_ref[...]], axis=0)
        wv = w_ref[...]
        y = b_ref[...] + wv[0:1] * xp[HALO - 3:HALO - 3 + t]
        for j in range(1, 4):
            y = y + wv[j:j + 1] * xp[HALO - 3 + j:HALO - 3 + j + t]
        y_ref[...] = y

    return pl.pallas_call(
        body, grid=(s // t,),
        in_specs=[pl.BlockSpec((t, width), lambda i: (i, cb)),
                  pl.BlockSpec((HALO, width), lambda i: (jnp.maximum(i * hb - 1, 0), cb)),
                  _whole(w.shape), _whole(bias.shape)],
        out_specs=_rowspec(t, width), out_shape=jax.ShapeDtypeStruct((s, width), F32),
        name=name, compiler_params=_params(("arbitrary",)))(x, x, w, bias)


def _conv_bwd(name, dy, x, col0, width, w, buf, t=512):
    s = x.shape[0]
    cb = col0 // width
    hb = t // HALO
    nblk = s // t

    def body(dy_ref, dyn_ref, x_ref, halo_ref, w_ref, buf_ref, dx_ref, dw_ref, db_ref):
        i = pl.program_id(0)
        dyv = dy_ref[...]
        nxt = jnp.where(i == nblk - 1, 0.0, dyn_ref[...])
        dyp = jnp.concatenate([dyv, nxt], axis=0)
        wv = w_ref[...]
        dx = wv[3:4] * dyv
        for j in range(3):
            dx = dx + wv[j:j + 1] * dyp[3 - j:3 - j + t]
        dx_ref[...] = dx.astype(dx_ref.dtype)
        halo = jnp.where(i == 0, 0.0, halo_ref[...])
        xp = jnp.concatenate([halo, x_ref[...]], axis=0)
        dw = jnp.concatenate(
            [jnp.sum(dyv * xp[HALO - 3 + j:HALO - 3 + j + t], axis=0, keepdims=True) for j in range(4)], axis=0)

        @pl.when(i == 0)
        def _():
            dw_ref[...] = jnp.zeros_like(dw_ref)
            db_ref[...] = jnp.zeros_like(db_ref)

        dw_ref[...] += dw
        db_ref[...] += jnp.sum(dyv, axis=0, keepdims=True)

    return pl.pallas_call(
        body, grid=(nblk,),
        in_specs=[_rowspec(t, width),
                  pl.BlockSpec((HALO, width), lambda i: (jnp.minimum((i + 1) * hb, s // HALO - 1), 0)),
                  pl.BlockSpec((t, width), lambda i: (i, cb)),
                  pl.BlockSpec((HALO, width), lambda i: (jnp.maximum(i * hb - 1, 0), cb)),
                  _whole(w.shape), pl.BlockSpec(memory_space=pl.ANY)],
        out_specs=[_into_spec(t, width, col0), _whole((4, width)), _whole((1, width))],
        out_shape=[jax.ShapeDtypeStruct(buf.shape, buf.dtype), jax.ShapeDtypeStruct((4, width), F32),
                   jax.ShapeDtypeStruct((1, width), F32)],
        input_output_aliases={5: 0},
        name=name, compiler_params=_params(("arbitrary",)))(dy, dy, x, x, w, buf)


def _hgrn_chunk(st, q, k, v, lf):
    cum = _running_sum(lf)
    tot = jnp.sum(lf, axis=0, keepdims=True)
    bm = _iota((A_W, A_W), 0) // A_HD == _iota((A_W, A_W), 1) // A_HD
    bo = bm.astype(F32)
    o_inter = _bdot(q * jnp.exp(cum), st, "nt")
    kd = k * jnp.exp(tot - cum)
    st_new = st * jnp.exp(tot) + jnp.where(bm, _bdot(v, kd, "tn"), 0.0)
    lane_h = _iota((SUB, A_W), 1) // A_HD
    n_h = A_W // A_HD
    t3 = _iota((SUB, SUB, A_W), 0)
    s3 = _iota((SUB, SUB, A_W), 1)
    outs = []
    for i in range(CHUNK // SUB):
        sl = slice(i * SUB, (i + 1) * SUB)
        qi, ki, vi, ci = q[sl], k[sl], v[sl], cum[sl]
        e = jnp.exp(jnp.where(s3 <= t3, ci[:, None, :] - ci[None, :, :], MASKED_EXPONENT))
        d3 = qi[:, None, :] * ki[None, :, :] * e
        ds = _mdot(d3.reshape(SUB * SUB, A_W), bo, "nn", 1, 1, (True, False)).reshape(SUB, SUB, A_W)
        oi = jnp.sum(ds * vi[None, :, :], axis=1)
        if i > 0:
            n = i * SUB
            cb = cum[n - 1:n]
            qt = qi * jnp.exp(ci - cb)
            kt = k[:n] * jnp.exp(cb - cum[:n])
            qs = jnp.concatenate([jnp.where(lane_h == h, qt, 0.0) for h in range(n_h)], axis=0)
            p = _bdot(_bdot(qs, kt, "nt"), v[:n])
            for h in range(n_h):
                oi = oi + jnp.where(lane_h == h, p[h * SUB:(h + 1) * SUB], 0.0)
        outs.append(oi)
    return st_new, o_inter + jnp.concatenate(outs, axis=0)


HGRN_CHUNKS = 2


def _hgrn_fwd(name, q, k, v, lf):
    s = q.shape[0]
    n = s // CHUNK
    g = HGRN_CHUNKS

    def body(q_ref, k_ref, v_ref, lf_ref, o_ref, sts_ref, st_ref):
        @pl.when(pl.program_id(0) == 0)
        def _():
            st_ref[...] = jnp.zeros_like(st_ref)

        st = st_ref[...]
        for c in range(g):
            rows = pl.ds(c * CHUNK, CHUNK)
            sts_ref[c] = st
            st, o = _hgrn_chunk(st, q_ref[rows], k_ref[rows], v_ref[rows], lf_ref[rows])
            o_ref[rows] = o
        st_ref[...] = st

    cs = _rowspec(g * CHUNK, A_W)
    return pl.pallas_call(
        body, grid=(n // g,), in_specs=[cs] * 4,
        out_specs=[cs, pl.BlockSpec((g, A_W, A_W), lambda i: (i, 0, 0))],
        out_shape=[jax.ShapeDtypeStruct((s, A_W), F32), jax.ShapeDtypeStruct((n, A_W, A_W), F32)],
        scratch_shapes=[pltpu.VMEM((A_W, A_W), F32)],
        name=name, compiler_params=_params(("arbitrary",)))(q, k, v, lf)


def _hgrn_bwd(name, q, k, v, lf, sts, do, buf, col0):
    s = q.shape[0]
    n = s // CHUNK
    g = HGRN_CHUNKS
    nsteps = n // g

    def body(q_ref, k_ref, v_ref, lf_ref, sts_ref, do_ref, buf_ref, dq_ref, dk_ref, dv_ref, dlf_ref, dst_ref):
        @pl.when(pl.program_id(0) == 0)
        def _():
            dst_ref[...] = jnp.zeros_like(dst_ref)

        dst = dst_ref[...]
        for c in reversed(range(g)):
            rows = pl.ds(c * CHUNK, CHUNK)
            _, vjp = jax.vjp(_hgrn_chunk, sts_ref[c], q_ref[rows], k_ref[rows], v_ref[rows], lf_ref[rows])
            ct = vjp((dst, do_ref[rows]))
            dst = ct[0]
            for r, gi in zip((dq_ref, dk_ref, dv_ref, dlf_ref), ct[1:]):
                r[rows] = gi.astype(r.dtype)
        dst_ref[...] = dst

    rs = pl.BlockSpec((g * CHUNK, A_W), lambda i: (nsteps - 1 - i, 0))
    rows = jax.ShapeDtypeStruct((s, A_W), F32)
    return pl.pallas_call(
        body, grid=(nsteps,),
        in_specs=[rs] * 4 + [pl.BlockSpec((g, A_W, A_W), lambda i: (nsteps - 1 - i, 0, 0)), rs,
                             pl.BlockSpec(memory_space=pl.ANY)],
        out_specs=[rs, rs, _into_spec(g * CHUNK, A_W, col0, nsteps), rs],
        out_shape=[rows, rows, jax.ShapeDtypeStruct(buf.shape, buf.dtype), rows],
        input_output_aliases={6: 2}, scratch_shapes=[pltpu.VMEM((A_W, A_W), F32)],
        name=name, compiler_params=_params(("arbitrary",)))(q, k, v, lf, sts, do, buf)


PREP_CHUNKS = 4
REC_CHUNKS = 8
DEC_ROWS = 8


def _to_heads(x, rows=CHUNK, width=B_HD):
    n = x.shape[0] // rows
    x3 = x.reshape(n, rows, x.shape[1])
    return jnp.concatenate([x3[:, :, h * B_HD:h * B_HD + width] for h in range(B_NH)], axis=0)


def _from_heads(y):
    n = y.shape[0] // B_NH
    x3 = jnp.concatenate([y[h * n:(h + 1) * n] for h in range(B_NH)], axis=2)
    return x3.reshape(n * y.shape[1], B_NH * B_HD)


@jax.custom_vjp
def _unit_lower_inverse(a_mat):
    nil = -a_mat
    eye = (_iota(a_mat.shape, 1) == _iota(a_mat.shape, 2)).astype(F32)
    inv = eye + nil
    sq = 2
    while sq < CHUNK:
        nil = _split_dot(nil, nil, "nn", 2, 2)
        inv = inv + _split_dot(inv, nil, "nn", 2, 2)
        sq *= 2
    return inv


def _unit_lower_inverse_fwd(a_mat):
    inv = _unit_lower_inverse(a_mat)
    return inv, inv


def _unit_lower_inverse_bwd(inv, ct):
    return (-_split_dot(_split_dot(inv, ct, "tn", 2, 2), inv, "nt", 2, 2),)


_unit_lower_inverse.defvjp(_unit_lower_inverse_fwd, _unit_lower_inverse_bwd)


@jax.custom_vjp
def _known_inverse(a_mat, inv):
    return inv


def _known_inverse_fwd(a_mat, inv):
    return inv, inv


def _known_inverse_bwd(inv, ct):
    return _unit_lower_inverse_bwd(inv, ct)[0], jnp.zeros_like(inv)


_known_inverse.defvjp(_known_inverse_fwd, _known_inverse_bwd)


def _gdn_prep(q, k, v, beta, la, inv_saved=None):
    q, k, v, beta, la = [_to_heads(z) for z in (q, k, v, beta, la)]
    nb = q.shape[0]
    r = _iota((nb, CHUNK, CHUNK), 1)
    c = _iota((nb, CHUNK, CHUNK), 2)
    causal = c <= r
    g = _running_sum(la)
    tot = jnp.sum(la, axis=1, keepdims=True)
    gcol = g[:, :, :CHUNK]
    decay = jnp.exp(jnp.where(causal, gcol - jnp.swapaxes(gcol, 1, 2), MASKED_EXPONENT))
    kb = k * beta
    a_mat = jnp.where(c < r, _bdot(kb, k, "nt") * decay, 0.0)
    if inv_saved is None:
        inv = _unit_lower_inverse(a_mat)
    else:
        inv = _known_inverse(a_mat, _to_heads(inv_saved, width=CHUNK))
    eg = jnp.exp(g)
    u = _bdot(inv, v * beta)
    w = _bdot(inv, kb * eg)
    qk = jnp.where(causal, _bdot(q, k, "nt") * decay, 0.0)
    widen = lambda z: jnp.concatenate([z, jnp.zeros((nb, CHUNK, B_HD - CHUNK), F32)], axis=2)
    dec = jnp.broadcast_to(jnp.exp(tot), (nb, DEC_ROWS, B_HD))
    return tuple(_from_heads(z) for z in (u, w, widen(qk), q * eg, k * jnp.exp(tot - g), dec, widen(inv)))


def _gdn_rec(st, u, w, qk, qd, kd, dec):
    u, w, qd, kd = [_to_heads(z) for z in (u, w, qd, kd)]
    qk = _to_heads(qk, width=CHUNK)
    v_new = u - _bdot(w, st)
    o = _bdot(qd, st) + _bdot(qk, v_new)
    st_new = st * _to_heads(dec, rows=DEC_ROWS)[:, :1, :1] + _bdot(kd, v_new, "tn")
    return st_new, _from_heads(o)


def _gdn_prep_fwd(name, q, k, v, beta, la):
    s = q.shape[0]
    t = PREP_CHUNKS * CHUNK
    td = PREP_CHUNKS * DEC_ROWS

    def body(*refs):
        outs = _gdn_prep(*[r[...] for r in refs[:5]])
        for o_ref, o in zip(refs[5:], outs):
            o_ref[...] = o

    rows, dec_rows = jax.ShapeDtypeStruct((s, B_W), F32), jax.ShapeDtypeStruct((s // CHUNK * DEC_ROWS, B_W), F32)
    return pl.pallas_call(
        body, grid=(s // t,), in_specs=[_rowspec(t, B_W)] * 5,
        out_specs=[_rowspec(t, B_W)] * 5 + [_rowspec(td, B_W), _rowspec(t, B_W)],
        out_shape=[rows] * 5 + [dec_rows, rows],
        name=name, compiler_params=_params(("arbitrary",)))(q, k, v, beta, la)


def _gdn_prep_bwd(name, q, k, v, beta, la, inv, cts):
    s = q.shape[0]
    t = PREP_CHUNKS * CHUNK
    td = PREP_CHUNKS * DEC_ROWS

    def body(*refs):
        inv_saved = refs[5][...]
        _, vjp = jax.vjp(lambda *a: _gdn_prep(*a, inv_saved=inv_saved)[:6], *[r[...] for r in refs[:5]])
        g = vjp(tuple(r[...] for r in refs[6:12]))
        for o_ref, o in zip(refs[12:], g):
            o_ref[...] = o

    return pl.pallas_call(
        body, grid=(s // t,), in_specs=[_rowspec(t, B_W)] * 11 + [_rowspec(td, B_W)],
        out_specs=[_rowspec(t, B_W)] * 5, out_shape=[jax.ShapeDtypeStruct((s, B_W), F32)] * 5,
        name=name, compiler_params=_params(("arbitrary",)))(q, k, v, beta, la, inv, *cts)


def _gdn_rec_fwd(name, u, w, qk, qd, kd, dec):
    s = u.shape[0]
    n = s // CHUNK
    g = REC_CHUNKS

    def body(u_ref, w_ref, qk_ref, qd_ref, kd_ref, dec_ref, o_ref, sts_ref, st_ref):
        @pl.when(pl.program_id(0) == 0)
        def _():
            st_ref[...] = jnp.zeros_like(st_ref)

        st = st_ref[...]
        for c in range(g):
            rows, drows = pl.ds(c * CHUNK, CHUNK), pl.ds(c * DEC_ROWS, DEC_ROWS)
            sts_ref[c] = st
            st, o = _gdn_rec(st, u_ref[rows], w_ref[rows], qk_ref[rows], qd_ref[rows], kd_ref[rows], dec_ref[drows])
            o_ref[rows] = o
        st_ref[...] = st

    cs = _rowspec(g * CHUNK, B_W)
    return pl.pallas_call(
        body, grid=(n // g,), in_specs=[cs] * 5 + [_rowspec(g * DEC_ROWS, B_W)],
        out_specs=[cs, pl.BlockSpec((g, B_NH, B_HD, B_HD), lambda i: (i, 0, 0, 0))],
        out_shape=[jax.ShapeDtypeStruct((s, B_W), F32), jax.ShapeDtypeStruct((n, B_NH, B_HD, B_HD), F32)],
        scratch_shapes=[pltpu.VMEM((B_NH, B_HD, B_HD), F32)],
        name=name, compiler_params=_params(("arbitrary",)))(u, w, qk, qd, kd, dec)


def _gdn_rec_bwd(name, u, w, qk, qd, kd, dec, sts, do):
    s = u.shape[0]
    n = s // CHUNK
    g = REC_CHUNKS
    nsteps = n // g

    def body(u_ref, w_ref, qk_ref, qd_ref, kd_ref, dec_ref, sts_ref, do_ref,
             du_ref, dw_ref, dqk_ref, dqd_ref, dkd_ref, ddec_ref, dst_ref):
        @pl.when(pl.program_id(0) == 0)
        def _():
            dst_ref[...] = jnp.zeros_like(dst_ref)

        dst = dst_ref[...]
        for c in reversed(range(g)):
            rows, drows = pl.ds(c * CHUNK, CHUNK), pl.ds(c * DEC_ROWS, DEC_ROWS)
            _, vjp = jax.vjp(_gdn_rec, sts_ref[c], u_ref[rows], w_ref[rows], qk_ref[rows], qd_ref[rows], kd_ref[rows],
                             dec_ref[drows])
            ct = vjp((dst, do_ref[rows]))
            dst = ct[0]
            for r, gi in zip((du_ref, dw_ref, dqk_ref, dqd_ref, dkd_ref), ct[1:6]):
                r[rows] = gi
            ddec_ref[drows] = ct[6]
        dst_ref[...] = dst

    rs = pl.BlockSpec((g * CHUNK, B_W), lambda i: (nsteps - 1 - i, 0))
    ds = pl.BlockSpec((g * DEC_ROWS, B_W), lambda i: (nsteps - 1 - i, 0))
    return pl.pallas_call(
        body, grid=(nsteps,),
        in_specs=[rs] * 5 + [ds, pl.BlockSpec((g, B_NH, B_HD, B_HD), lambda i: (nsteps - 1 - i, 0, 0, 0)), rs],
        out_specs=[rs] * 5 + [ds],
        out_shape=[jax.ShapeDtypeStruct((s, B_W), F32)] * 5 + [jax.ShapeDtypeStruct((n * DEC_ROWS, B_W), F32)],
        scratch_shapes=[pltpu.VMEM((B_NH, B_HD, B_HD), F32)],
        name=name, compiler_params=_params(("arbitrary",)))(u, w, qk, qd, kd, dec, sts, do)


SCAN_T = 512


def _block_scan(a, b, reverse):
    t = a.shape[0]
    rows = _iota(a.shape, 0)
    sft = 1
    while sft < t:
        if reverse:
            a_s, b_s = pltpu.roll(a, t - sft, 0), pltpu.roll(b, t - sft, 0)
            edge = rows >= t - sft
        else:
            a_s, b_s = pltpu.roll(a, sft, 0), pltpu.roll(b, sft, 0)
            edge = rows < sft
        a_s = jnp.where(edge, 1.0, a_s)
        b_s = jnp.where(edge, 0.0, b_s)
        b = a * b_s + b
        a = a * a_s
        sft *= 2
    return a, b


def _lru_scan_fwd(name, a, b):
    s, w = a.shape
    t = SCAN_T

    def body(a_ref, b_ref, h_ref, hp_ref, carry):
        @pl.when(pl.program_id(0) == 0)
        def _():
            carry[...] = jnp.zeros_like(carry)

        h_in = carry[0:1]
        ca, cb = _block_scan(a_ref[...], b_ref[...], False)
        h = ca * h_in + cb
        h_ref[...] = h
        hp_ref[...] = jnp.where(_iota(h.shape, 0) == 0, h_in, pltpu.roll(h, 1, 0))
        carry[...] = jnp.broadcast_to(h[t - 1:t], carry.shape)

    return pl.pallas_call(
        body, grid=(s // t,), in_specs=[_rowspec(t, w)] * 2, out_specs=[_rowspec(t, w)] * 2,
        out_shape=[jax.ShapeDtypeStruct((s, w), F32)] * 2, scratch_shapes=[pltpu.VMEM((8, w), F32)],
        name=name, compiler_params=_params(("arbitrary",)))(a, b)


def _lru_scan_bwd(name, a, dh):
    s, w = a.shape
    t = SCAN_T
    n = s // t

    def body(a_ref, dh_ref, g_ref, a_next, g_next):
        @pl.when(pl.program_id(0) == 0)
        def _():
            a_next[...] = jnp.zeros_like(a_next)
            g_next[...] = jnp.zeros_like(g_next)

        av = a_ref[...]
        a_up = jnp.where(_iota(av.shape, 0) == t - 1, a_next[0:1], pltpu.roll(av, t - 1, 0))
        ca, cb = _block_scan(a_up, dh_ref[...], True)
        g = ca * g_next[0:1] + cb
        g_ref[...] = g
        a_next[...] = jnp.broadcast_to(av[0:1], a_next.shape)
        g_next[...] = jnp.broadcast_to(g[0:1], g_next.shape)

    rs = pl.BlockSpec((t, w), lambda i: (n - 1 - i, 0))
    return pl.pallas_call(
        body, grid=(n,), in_specs=[rs] * 2, out_specs=rs, out_shape=jax.ShapeDtypeStruct((s, w), F32),
        scratch_shapes=[pltpu.VMEM((8, w), F32), pltpu.VMEM((8, w), F32)],
        name=name, compiler_params=_params(("arbitrary",)))(a, dh)


def _loss_fn(g, x, tgt):
    y = x * lax.rsqrt(jnp.mean(x * x, axis=-1, keepdims=True) + EPS) * g
    err = y - tgt
    return 0.5 * jnp.sum(jnp.mean(err * err, axis=-1, keepdims=True), axis=0, keepdims=True)


def _loss_and_grad(name, g, x, tgt, t=256):
    s, d = x.shape

    def body(g_ref, x_ref, t_ref, loss_ref, dx_ref, dg_ref, dx16_ref):
        val, vjp = jax.vjp(_loss_fn, g_ref[...], x_ref[...], t_ref[...])
        dg, dx, _ = vjp(jnp.ones((1, 1), F32))

        @pl.when(pl.program_id(0) == 0)
        def _():
            loss_ref[...] = jnp.zeros_like(loss_ref)
            dg_ref[...] = jnp.zeros_like(dg_ref)

        loss_ref[...] += jnp.broadcast_to(val, loss_ref.shape)
        dg_ref[...] += dg
        dx_ref[...] = dx
        dx16_ref[...] = dx.astype(BF16)

    return pl.pallas_call(
        body, grid=(s // t,), in_specs=[_whole(g.shape), _rowspec(t, d), _rowspec(t, d)],
        out_specs=[_whole((1, 128)), _rowspec(t, d), _whole(g.shape), _rowspec(t, d)],
        out_shape=[jax.ShapeDtypeStruct((1, 128), F32), jax.ShapeDtypeStruct((s, d), F32),
                   jax.ShapeDtypeStruct(g.shape, F32), jax.ShapeDtypeStruct((s, d), BF16)],
        name=name, compiler_params=_params(("arbitrary",)))(g, x, tgt)


def _ew(name, fn, ins, n_out, t=None):
    r, c = ins[0].shape
    t = _tile(r, (512, 256, 128, 64, 32, 16, 8)) if t is None else t

    def body(*refs):
        outs = fn(*[x[...] for x in refs[:len(ins)]])
        for o_ref, o in zip(refs[len(ins):], outs):
            o_ref[...] = o

    return pl.pallas_call(
        body, grid=(r // t,), in_specs=[_rowspec(t, c)] * len(ins), out_specs=[_rowspec(t, c)] * n_out,
        out_shape=[jax.ShapeDtypeStruct((r, c), F32)] * n_out,
        name=name, compiler_params=_params(("arbitrary",)))(*ins)


def _adam_fn(w, g, m, v):
    m = B1 * m + (1.0 - B1) * g
    v = B2 * v + (1.0 - B2) * (g * g)
    m_hat = m / (1.0 - B1 ** STEP)
    v_hat = v / (1.0 - B2 ** STEP)
    delta = -LR * (m_hat / (jnp.sqrt(v_hat) + AEPS) + WD * w)
    return delta, m, v


def _adam(name, w, g, m, v):
    shp = w.shape
    two = lambda z: z.reshape(-1, shp[-1])
    outs = _ew(name, _adam_fn, [two(w), two(g), two(m), two(v)], 3)
    return [o.reshape(shp) for o in outs]


LAYER_ROWS = PACK_ROWS // DEPTH
HALF_ROWS = LAYER_ROWS // 2
SUM_ROWS = HALF_ROWS // 2
GATHER_IDS = (1, 2, 3, 4)
SCATTER_IDS = (5, 6, 7, 8)
FIRST_GATHER_ID = 9
CONV_SUM_ID, GRAD_SUM_ID = 10, 11


def _mesh_pos():
    ix, iy, core = lax.axis_index("x"), lax.axis_index("y"), lax.axis_index("c")
    others = [(ix, 1 - iy), (1 - ix, iy), (1 - ix, 1 - iy)]
    return ix, iy, core, others


def _pack_big(w_in, w_out, w_up, w_down):
    pad = jnp.pad(w_in, ((0, 0), (0, 0), (0, 1024 - w_in.shape[-1])))
    return jnp.concatenate([pad, w_up, w_down, w_out], axis=1)


def _unpack_big(p):
    w_in = p[..., 0:1024, :D_IN // 4]
    return w_in, p[..., 3072:LAYER_ROWS, :], p[..., 1024:2048, :], p[..., 2048:3072, :]


def _gather_layer(packed, collective_id):
    src = jax.new_ref(packed, memory_space=pltpu.MemorySpace.HBM)
    out = jax.empty_ref(jax.ShapeDtypeStruct((4,) + packed.shape, packed.dtype), memory_space=pltpu.MemorySpace.HBM)
    dma = pltpu.SemaphoreType.DMA

    @pl.kernel(mesh=plsc.ScalarSubcoreMesh(axis_name="seq", num_cores=1), name="gather_%d" % collective_id,
               scratch_types=(dma,) * 12, compiler_params=pltpu.CompilerParams(collective_id=collective_id))
    def launch(*sems):
        send_sems, recv_sems = sems[:6], sems[6:]
        ix, iy, core, others = _mesh_pos()
        barrier = pltpu.get_barrier_semaphore()
        for peer in [(ox, oy, core) for ox, oy in others] + [(ix, iy, 1 - core)]:
            pl.semaphore_signal(barrier, inc=1, device_id=peer, device_id_type=MESH)
        pl.semaphore_wait(barrier, 4)
        half = packed.shape[0] // 2
        mine = pl.ds(core * half, half)
        theirs = pl.ds((1 - core) * half, half)

        def copy(k, src_ref, owner, rows, to):
            return pltpu.make_async_remote_copy(
                src_ref=src_ref, dst_ref=out.at[owner, rows], send_sem=send_sems[k], recv_sem=recv_sems[k],
                device_id=to, device_id_type=MESH)

        chip = 2 * ix + iy
        first = [copy(j, src.at[mine], chip, mine, (*others[j], core)) for j in range(3)]
        for cp in first:
            cp.start()
        passed = []
        for j, (ox, oy) in enumerate(others):
            owner = 2 * ox + oy
            copy(j, src.at[mine], owner, mine, (ix, iy, core)).wait_recv()
            fwd = copy(3 + j, out.at[owner, mine], owner, mine, (ix, iy, 1 - core))
            fwd.start()
            passed.append(fwd)
        for j, (ox, oy) in enumerate(others):
            copy(3 + j, src.at[theirs], 2 * ox + oy, theirs, (ix, iy, core)).wait_recv()
        for cp in first + passed:
            cp.wait_send()

    launch()
    return out[...]


def _rs_sibling(per_chip):
    hbm = pl.BlockSpec(memory_space=pltpu.HBM)

    def body(src, out, send_sem, recv_sem):
        ix, iy, core, _ = _mesh_pos()
        give = pl.ds((1 - core) * HALF_ROWS, HALF_ROWS)
        cp = pltpu.make_async_remote_copy(src_ref=src.at[pl.ds(0, 4), give], dst_ref=out, send_sem=send_sem,
                                          recv_sem=recv_sem, device_id=(ix, iy, 1 - core), device_id_type=MESH)
        cp.start()
        cp.wait()

    return pl.pallas_call(
        body, in_specs=[hbm], out_specs=hbm, out_shape=jax.ShapeDtypeStruct((4, HALF_ROWS, 1024), F32),
        scratch_shapes=[pltpu.SemaphoreType.DMA, pltpu.SemaphoreType.DMA],
        name="rs_sibling", compiler_params=pltpu.CompilerParams(has_side_effects=True))(per_chip)


def _rs_add_pair(per_chip, sib, chip, core):
    nb = HALF_ROWS // SUM_ROWS
    blk = (1, SUM_ROWS, 1024)

    def body(chip_ref, core_ref, a_ref, b_ref, own_ref, send_ref):
        total = a_ref[0] + b_ref[0]
        send_ref[0] = total.astype(BF16)

        @pl.when(pl.program_id(1) == chip_ref[0])
        def _():
            own_ref[...] = total

    return pl.pallas_call(
        body,
        grid_spec=pltpu.PrefetchScalarGridSpec(
            num_scalar_prefetch=2, grid=(nb, 4),
            in_specs=[pl.BlockSpec(blk, lambda i, k, ch, co: (k, co[0] * nb + i, 0)),
                      pl.BlockSpec(blk, lambda i, k, ch, co: (k, i, 0))],
            out_specs=[pl.BlockSpec((SUM_ROWS, 1024), lambda i, k, ch, co: (i, 0)),
                       pl.BlockSpec(blk, lambda i, k, ch, co: (k, i, 0))]),
        out_shape=[jax.ShapeDtypeStruct((HALF_ROWS, 1024), F32), jax.ShapeDtypeStruct((4, HALF_ROWS, 1024), BF16)],
        name="rs_add_pair", compiler_params=_params(("arbitrary", "arbitrary")))(
            chip.reshape(1), core.reshape(1), per_chip, sib)


def _rs_chips(pair, collective_id):
    src_ref = jax.new_ref(pair, memory_space=pltpu.MemorySpace.HBM)
    got_ref = jax.empty_ref(jax.ShapeDtypeStruct((3,) + pair.shape[1:], pair.dtype), memory_space=pltpu.MemorySpace.HBM)
    dma = pltpu.SemaphoreType.DMA

    @pl.kernel(mesh=plsc.ScalarSubcoreMesh(axis_name="seq", num_cores=1), name="rs_chips_%d" % collective_id,
               scratch_types=(dma,) * 6, compiler_params=pltpu.CompilerParams(collective_id=collective_id))
    def launch(s0, s1, s2, r0, r1, r2):
        ix, iy, core, others = _mesh_pos()
        barrier = pltpu.get_barrier_semaphore()
        for ox, oy in others:
            pl.semaphore_signal(barrier, inc=1, device_id=(ox, oy, core), device_id_type=MESH)
        pl.semaphore_wait(barrier, 3)
        copies = []
        for j, ((ox, oy), send_sem, recv_sem) in enumerate(zip(others, (s0, s1, s2), (r0, r1, r2))):
            cp = pltpu.make_async_remote_copy(src_ref=src_ref.at[2 * ox + oy], dst_ref=got_ref.at[j], send_sem=send_sem,
                                              recv_sem=recv_sem, device_id=(ox, oy, core), device_id_type=MESH)
            cp.start()
            copies.append(cp)
        for cp in copies:
            cp.wait()

    launch()
    return got_ref[...]


def _rs_add_chips(own, got, core, layer, buf):
    nb = HALF_ROWS // SUM_ROWS
    blk = (1, SUM_ROWS, 1024)

    def body(core_ref, a_ref, g0_ref, g1_ref, g2_ref, *rest):
        o_ref = rest[-1]
        o_ref[0, 0] = ((a_ref[...] + g0_ref[0].astype(F32)) + g1_ref[0].astype(F32)) + g2_ref[0].astype(F32)

    aliased = [] if buf is None else [buf]
    return pl.pallas_call(
        body,
        grid_spec=pltpu.PrefetchScalarGridSpec(
            num_scalar_prefetch=1, grid=(nb,),
            in_specs=[pl.BlockSpec((SUM_ROWS, 1024), lambda i, c: (i, 0))]
            + [pl.BlockSpec(blk, functools.partial(lambda j, i, c: (j, i, 0), j)) for j in range(3)]
            + [pl.BlockSpec(memory_space=pl.ANY)] * len(aliased),
            out_specs=pl.BlockSpec((1, 1, SUM_ROWS, 1024), lambda i, c: (layer, c[0], i, 0))),
        out_shape=jax.ShapeDtypeStruct((DEPTH, 2, HALF_ROWS, 1024), F32),
        input_output_aliases={} if buf is None else {5: 0},
        name="rs_add_chips", compiler_params=_params(("arbitrary",)))(core.reshape(1), own, got, got, got, *aliased)


def _rs_share(buf):
    hbm = pl.BlockSpec(memory_space=pltpu.HBM)

    def body(src, out, send_sem, recv_sem):
        ix, iy, core, _ = _mesh_pos()
        layers = pl.ds(0, DEPTH)
        cp = pltpu.make_async_remote_copy(src_ref=src.at[layers, core], dst_ref=out.at[layers, core], send_sem=send_sem,
                                          recv_sem=recv_sem, device_id=(ix, iy, 1 - core), device_id_type=MESH)
        cp.start()
        pltpu.make_async_remote_copy(src_ref=src.at[layers, 1 - core], dst_ref=out.at[layers, 1 - core],
                                     send_sem=send_sem, recv_sem=recv_sem, device_id=(ix, iy, core),
                                     device_id_type=MESH).wait_recv()
        cp.wait_send()

    return pl.pallas_call(
        body, in_specs=[hbm], out_specs=hbm, out_shape=jax.ShapeDtypeStruct(buf.shape, buf.dtype),
        input_output_aliases={0: 0}, scratch_shapes=[pltpu.SemaphoreType.DMA, pltpu.SemaphoreType.DMA],
        name="rs_share", compiler_params=pltpu.CompilerParams(has_side_effects=True))(buf)


def _scatter_start(per_chip, chip, core, layer):
    own, pair = _rs_add_pair(per_chip, _rs_sibling(per_chip), chip, core)
    return own, _rs_chips(pair, SCATTER_IDS[layer])


def _allreduce_small(tag, buf, collective_id):
    rows = buf.shape[0]
    src = jax.new_ref(buf, memory_space=pltpu.MemorySpace.HBM)
    out = jax.empty_ref(jax.ShapeDtypeStruct((8,) + buf.shape, buf.dtype), memory_space=pltpu.MemorySpace.HBM)
    dma = pltpu.SemaphoreType.DMA
    flips = [(a, b, c) for a in (0, 1) for b in (0, 1) for c in (0, 1)][1:]

    @pl.kernel(mesh=plsc.ScalarSubcoreMesh(axis_name="seq", num_cores=1), name="%s_gather_%d" % (tag, collective_id),
               scratch_types=(dma,) * 14, compiler_params=pltpu.CompilerParams(collective_id=collective_id))
    def launch(*sems):
        ix, iy, core, _ = _mesh_pos()
        peers = [(1 - ix if a else ix, 1 - iy if b else iy, 1 - core if c else core) for a, b, c in flips]
        barrier = pltpu.get_barrier_semaphore()
        for peer in peers:
            pl.semaphore_signal(barrier, inc=1, device_id=peer, device_id_type=MESH)
        pl.semaphore_wait(barrier, 7)
        copies = []
        for j, peer in enumerate(peers):
            cp = pltpu.make_async_remote_copy(src_ref=src, dst_ref=out.at[4 * ix + 2 * iy + core], send_sem=sems[j],
                                              recv_sem=sems[7 + j], device_id=peer, device_id_type=MESH)
            cp.start()
            copies.append(cp)
        for j, (px, py, pc) in enumerate(peers):
            pltpu.make_async_remote_copy(src_ref=src, dst_ref=out.at[4 * px + 2 * py + pc], send_sem=sems[j],
                                         recv_sem=sems[7 + j], device_id=peers[j], device_id_type=MESH).wait_recv()
        for cp in copies:
            cp.wait_send()

    launch()
    ix, iy, core = lax.axis_index("x"), lax.axis_index("y"), lax.axis_index("c")
    eight = lax.dynamic_update_slice(out[...], buf[None], (4 * ix + 2 * iy + core, 0, 0))

    def body(x_ref, o_ref):
        acc = x_ref[0]
        for d in range(1, 8):
            acc = acc + x_ref[d]
        o_ref[...] = acc

    t = 512
    return pl.pallas_call(
        body, grid=(rows // t,), in_specs=[pl.BlockSpec((8, t, 128), lambda i: (0, i, 0))],
        out_specs=_rowspec(t, 128), out_shape=jax.ShapeDtypeStruct(buf.shape, buf.dtype),
        name=tag + "_sum", compiler_params=_params(("arbitrary",)))(eight)


IN_COL_SEGMENTS = ((0, 1024, 2048), (1024, 3072, -1024), (3072, 3080, GATE0 - 3072), (3080, D_IN, COL_CX - 3080))


def _orig_cols(w_pad, lo, hi):
    parts = []
    for first, last, shift in IN_COL_SEGMENTS:
        a, b = max(lo, first), min(hi, last)
        if a < b:
            parts.append(w_pad[..., a + shift:b + shift])
    return parts


def _block_diag(w):
    n, d = w.shape[0], w.shape[1]
    on_diag = jnp.arange(n)[:, None] == jnp.arange(n)[None, :]
    return jnp.where(on_diag[:, None, :, None], w[:, :, None, :], 0.0).reshape(n * d, n * d)


def _diag_blocks(w):
    return jnp.stack([w[64 * n:64 * (n + 1), 64 * n:64 * (n + 1)] for n in range(4)], axis=0)


def _gate_row(v):
    return jnp.concatenate([jnp.zeros((4,), F32), v, jnp.zeros((120,), F32)]).reshape(1, 128)


SMALL = ("norm1_g", "hgrn_lb_logits", "hgrn_norm_g", "gdn_conv_w", "gdn_a_log", "gdn_dt_bias", "gdn_norm_g",
         "lru_conv_w", "lru_conv_b", "lru_w_a", "lru_b_a", "lru_w_x", "lru_b_x", "lru_lambda", "norm2_g",
         "final_norm_g")


def _rows_of(shape):
    size = 1
    for d in shape:
        size *= d
    return size, -(-size // 1024) * 8


def _flatten_small(tree):
    parts = []
    for k in sorted(tree):
        size, nr = _rows_of(tree[k].shape)
        parts.append(jnp.pad(tree[k].reshape(-1), (0, nr * 128 - size)).reshape(nr, 128))
    used = sum(pt.shape[0] for pt in parts)
    parts.append(jnp.zeros((-(-used // 512) * 512 - used, 128), F32))
    return jnp.concatenate(parts, axis=0)


def _unflatten_small(buf, shapes):
    out, r0 = {}, 0
    for k in sorted(shapes):
        size, nr = _rows_of(shapes[k])
        out[k] = buf[r0:r0 + nr].reshape(-1)[:size].reshape(shapes[k])
        r0 += nr
    return out


def _layer_fwd(l, x, p, w_in, late):
    sv = {"x0": x}
    (h,) = _rowwise_fwd("norm1", _rmsnorm_fn, [p["norm1_g"]], [x], [D_MODEL], [BF16], 512)
    proj, w_out, w_up, w_down = late(_matmul("proj_in", h, w_in, "nn"))
    sv["h"], sv["proj"] = h, proj
    aq, af, ag = _cols(proj, COL_AQ, A_W), _cols(proj, COL_AF, A_W), _cols(proj, COL_AG, A_W)
    ai = lax.slice_in_dim(proj, COL_AI, COL_AI + A_W, axis=1)
    bz, gate, cy = _cols(proj, COL_BZ, B_W), _cols(proj, GATE0, 128), _cols(proj, COL_CY, C_W)
    q, k, lf = _rowwise_fwd("hgrn_pre", functools.partial(_hgrn_pre_fn, l), [p["lb_logits"]], [aq, af],
                            [A_W] * 3, [F32] * 3, 512)
    o_a, sts_a = _hgrn_fwd("hgrn_chunk", q, k, ai, lf)
    (mixed,) = _rowwise_fwd("hgrn_post", _hgrn_post_fn, [p["hgrn_norm_g"]], [o_a, ag], [A_W], [BF16], 512,
                            into=(B_W, D_MODEL, None))
    sv.update(aq=aq, af=af, ai=ai, ag=ag, hq=q, hk=k, hlf=lf, o_a=o_a, sts_a=sts_a)
    qkvc = _conv_fwd("gdn_conv", proj, 0, 3 * B_W, p["gdn_conv_w"], jnp.zeros((1, 3 * B_W), F32), 256)
    gq, gk, gv, gb, gla = _rowwise_fwd("gdn_pre", _gdn_pre_fn, [p["gdn_a_log"], p["gdn_dt_bias"]], [qkvc, gate],
                                       [B_W] * 5, [F32] * 5, 256)
    *wy, sv["gdn_inv"] = _gdn_prep_fwd("gdn_prep", gq, gk, gv, gb, gla)
    o_b, sts_b = _gdn_rec_fwd("gdn_rec", *wy)
    sv["wy"] = wy
    (mixed,) = _rowwise_fwd("gdn_post", _gdn_post_fn, [p["gdn_norm_g"]], [o_b, bz], [B_W], [BF16], 512,
                            into=(0, D_MODEL, mixed))
    sv.update(qkvc=qkvc, gate=gate, bz=bz, gq=gq, gk=gk, gv=gv, gb=gb, gla=gla, o_b=o_b, sts_b=sts_b)
    xc = _conv_fwd("lru_conv", proj, COL_CX, C_W, p["lru_conv_w"], p["lru_conv_b"], 512)
    a, b = _rowwise_fwd("lru_pre", _lru_pre_fn, [p["lru_w_a"], p["lru_b_a"], p["lru_w_x"], p["lru_b_x"], p["lru_lambda"]],
                        [xc], [C_W] * 2, [F32] * 2, 512)
    hs, h_prev = _lru_scan_fwd("lru_scan", a, b)
    (mixed,) = _rowwise_fwd("lru_post", _lru_post_fn, [], [hs, cy], [C_W], [BF16], 512,
                            into=(B_W + A_W, D_MODEL, mixed))
    sv.update(xc=xc, la=a, hs=hs, h_prev=h_prev, cy=cy)
    x1 = _matmul("proj_out", mixed, w_out, "nn", res=x)
    (h2,) = _rowwise_fwd("norm2", _rmsnorm_fn, [p["norm2_g"]], [x1], [D_MODEL], [BF16], 512)
    up, act = _matmul("mlp_up", h2, w_up, "nn", also=_relu2)
    x2 = _matmul("mlp_down", act, w_down, "nn", res=x1)
    sv.update(mixed=mixed, x1=x1, h2=h2, up=up, act=act)
    return x2, sv, (w_in, w_out, w_up, w_down)


def _layer_bwd(l, dx, dx16, p, sv, w_in, w_out, w_up, w_down):
    gs = {}
    dup = _matmul("mlp_down_dx", dx16, w_down, "nt", epi=_drelu2, epi_in=sv["up"], out_dtype=BF16)
    d_w_down = _matmul("mlp_down_dw", sv["act"], dx16, "tn")
    dh2 = _matmul("mlp_up_dx", dup, w_up, "nt")
    d_w_up = _matmul("mlp_up_dw", sv["h2"], dup, "tn")
    (gs["norm2_g"],), (dx1, dx1_16) = _rowwise_bwd("norm2_bwd", _rmsnorm_fn, [p["norm2_g"]], [sv["x1"]], [dh2], 512,
                                                   res=dx, copy16=True)
    dmixed = _matmul("proj_out_dx", dx1_16, w_out, "nt")
    d_w_out = _matmul("proj_out_dw", sv["mixed"], dx1_16, "tn")
    dy_b, dy_a, dy_c = _cols(dmixed, 0, B_W), _cols(dmixed, B_W, A_W), _cols(dmixed, B_W + A_W, C_W)
    (gs["hgrn_norm_g"],), (do_a, dproj) = _rowwise_bwd("hgrn_post_bwd", _hgrn_post_fn, [p["hgrn_norm_g"]],
                                                       [sv["o_a"], sv["ag"]], [dy_a], 512, pack=([1], COL_AG, A_W, None))
    dq, dk, dproj, dlf = _hgrn_bwd("hgrn_chunk_bwd", sv["hq"], sv["hk"], sv["ai"], sv["hlf"], sv["sts_a"], do_a,
                                   dproj, COL_AI)
    (gs["lb_logits"],), (dproj,) = _rowwise_bwd("hgrn_pre_bwd", functools.partial(_hgrn_pre_fn, l), [p["lb_logits"]],
                                                [sv["aq"], sv["af"]], [dq, dk, dlf], 512,
                                                pack=([0, 1], COL_AQ, 2 * A_W, dproj))
    (gs["gdn_norm_g"],), (do_b, dproj) = _rowwise_bwd("gdn_post_bwd", _gdn_post_fn, [p["gdn_norm_g"]],
                                                      [sv["o_b"], sv["bz"]], [dy_b], 512, pack=([1], COL_BZ, B_W, dproj))
    d_wy = _gdn_rec_bwd("gdn_rec_bwd", *sv["wy"], sv["sts_b"], do_b)
    dgq, dgk, dgv, dgb, dgla = _gdn_prep_bwd("gdn_prep_bwd", sv["gq"], sv["gk"], sv["gv"], sv["gb"], sv["gla"],
                                             sv["gdn_inv"], d_wy)
    (gs["gdn_a_log"], gs["gdn_dt_bias"]), (dqkvc, dproj) = _rowwise_bwd(
        "gdn_pre_bwd", _gdn_pre_fn, [p["gdn_a_log"], p["gdn_dt_bias"]], [sv["qkvc"], sv["gate"]],
        [dgq, dgk, dgv, dgb, dgla], 256, pack=([1], GATE0, D_IN_PAD - GATE0, dproj))
    dproj, gs["gdn_conv_w"], _ = _conv_bwd("gdn_conv_bwd", dqkvc, sv["proj"], 0, 3 * B_W, p["gdn_conv_w"], dproj, 256)
    _, (dhs, dproj) = _rowwise_bwd("lru_post_bwd", _lru_post_fn, [], [sv["hs"], sv["cy"]], [dy_c], 512,
                                   pack=([1], COL_CY, C_W, dproj))
    g = _lru_scan_bwd("lru_scan_bwd", sv["la"], dhs)
    lru_params = [p["lru_w_a"], p["lru_b_a"], p["lru_w_x"], p["lru_b_x"], p["lru_lambda"]]
    dps, (dxc,) = _rowwise_bwd("lru_pre_bwd", _lru_pre_fn, lru_params, [sv["xc"]], [g, sv["h_prev"]], 512,
                               dout_fn=lambda gt, h_prev: (gt * h_prev, gt))
    gs["lru_w_a"], gs["lru_b_a"], gs["lru_w_x"], gs["lru_b_x"], gs["lru_lambda"] = dps
    dproj, gs["lru_conv_w"], gs["lru_conv_b"] = _conv_bwd("lru_conv_bwd", dxc, sv["proj"], COL_CX, C_W, p["lru_conv_w"],
                                                          dproj, 512)
    dh = _matmul("proj_in_dx", dproj, w_in, "nt")
    d_w_in = _matmul("proj_in_dw", sv["h"], dproj, "tn")
    (gs["norm1_g"],), (dx0, dx0_16) = _rowwise_bwd("norm1_bwd", _rmsnorm_fn, [p["norm1_g"]], [sv["x0"]], [dh], 512,
                                                   res=dx1, copy16=True)
    return dx0, dx0_16, dict(w_in=d_w_in, w_out=d_w_out, w_up=d_w_up, w_down=d_w_down), gs


def _layer_params(l, lb_logits, hgrn_norm_g, gdn_conv_w, gdn_a_log, gdn_dt_bias, gdn_norm_g, lru_conv_w, lru_conv_b,
                  lru_w_a, lru_b_a, lru_w_x, lru_b_x, lru_lambda, norm1_g, norm2_g):
    row = lambda v: v.reshape(1, -1)
    return dict(
        norm1_g=row(norm1_g[l]), norm2_g=row(norm2_g[l]), lb_logits=lb_logits,
        hgrn_norm_g=row(jnp.tile(hgrn_norm_g[l], A_W // A_HD)),
        gdn_conv_w=gdn_conv_w[l], gdn_a_log=_gate_row(gdn_a_log[l]), gdn_dt_bias=_gate_row(gdn_dt_bias[l]),
        gdn_norm_g=row(jnp.tile(gdn_norm_g[l], B_NH)),
        lru_conv_w=lru_conv_w[l], lru_conv_b=row(lru_conv_b[l]), lru_w_a=_block_diag(lru_w_a[l]), lru_b_a=row(lru_b_a[l]),
        lru_w_x=_block_diag(lru_w_x[l]), lru_b_x=row(lru_b_x[l]), lru_lambda=row(lru_lambda[l]))


def _small_grads_to_reference(gs_layers, d_final_g, d_logits):
    st = lambda k, f=lambda z: z: jnp.stack([f(g[k]) for g in gs_layers], axis=0)
    vec = lambda z: z.reshape(-1)
    return dict(
        norm1_g=st("norm1_g", vec), norm2_g=st("norm2_g", vec), hgrn_lb_logits=d_logits,
        hgrn_norm_g=st("hgrn_norm_g", lambda z: z.reshape(A_W // A_HD, A_HD).sum(0)),
        gdn_conv_w=st("gdn_conv_w"), gdn_a_log=st("gdn_a_log", lambda z: z[0, 4:8]),
        gdn_dt_bias=st("gdn_dt_bias", lambda z: z[0, 4:8]),
        gdn_norm_g=st("gdn_norm_g", lambda z: z.reshape(B_NH, B_HD).sum(0)),
        lru_conv_w=st("lru_conv_w"), lru_conv_b=st("lru_conv_b", vec), lru_w_a=st("lru_w_a", _diag_blocks),
        lru_b_a=st("lru_b_a", vec), lru_w_x=st("lru_w_x", _diag_blocks), lru_b_x=st("lru_b_x", vec),
        lru_lambda=st("lru_lambda", vec), final_norm_g=d_final_g.reshape(-1))


def _local_step(x, tgt, small, layer_weights, on_layer_grads=None):
    layer_p = [_layer_params(l, small["hgrn_lb_logits"], small["hgrn_norm_g"], small["gdn_conv_w"], small["gdn_a_log"],
                             small["gdn_dt_bias"], small["gdn_norm_g"], small["lru_conv_w"], small["lru_conv_b"],
                             small["lru_w_a"], small["lru_b_a"], small["lru_w_x"], small["lru_b_x"], small["lru_lambda"],
                             small["norm1_g"], small["norm2_g"]) for l in range(DEPTH)]
    saved, weights = [], []
    for l in range(DEPTH):
        x, w_in, late = layer_weights(l, x)
        x, sv, w = _layer_fwd(l, x, layer_p[l], w_in, late)
        saved.append(sv)
        weights.append(w)
    loss, dx, d_final_g, dx16 = _loss_and_grad("final_loss", small["final_norm_g"].reshape(1, -1), x, tgt)
    big = [None] * DEPTH
    gs_layers = [None] * DEPTH
    d_logits = jnp.zeros_like(small["hgrn_lb_logits"])
    for l in reversed(range(DEPTH)):
        dx, dx16, big[l], gs_layers[l] = _layer_bwd(l, dx, dx16, layer_p[l], saved[l], *weights[l])
        d_logits = d_logits + gs_layers[l]["lb_logits"]
        if on_layer_grads is not None:
            big[l] = on_layer_grads(l, big[l])
    return loss, dx, big, _small_grads_to_reference(gs_layers, d_final_g, d_logits)


def kernel(x, norm1_g, w_in, hgrn_lb_logits, hgrn_norm_g, gdn_conv_w, gdn_a_log, gdn_dt_bias, gdn_norm_g, lru_conv_w, lru_conv_b, lru_w_a, lru_b_a, lru_w_x, lru_b_x, lru_lambda, w_out, norm2_g, w_up, w_down, final_norm_g, loss_target, m_norm1_g, m_w_in, m_hgrn_lb_logits, m_hgrn_norm_g, m_gdn_conv_w, m_gdn_a_log, m_gdn_dt_bias, m_gdn_norm_g, m_lru_conv_w, m_lru_conv_b, m_lru_w_a, m_lru_b_a, m_lru_w_x, m_lru_b_x, m_lru_lambda, m_w_out, m_norm2_g, m_w_up, m_w_down, m_final_norm_g, v_norm1_g, v_w_in, v_hgrn_lb_logits, v_hgrn_norm_g, v_gdn_conv_w, v_gdn_a_log, v_gdn_dt_bias, v_gdn_norm_g, v_lru_conv_w, v_lru_conv_b, v_lru_w_a, v_lru_b_a, v_lru_w_x, v_lru_b_x, v_lru_lambda, v_w_out, v_norm2_g, v_w_up, v_w_down, v_final_norm_g):
    args = dict(locals())
    ix, iy, core = lax.axis_index("x"), lax.axis_index("y"), lax.axis_index("c")
    chip = 2 * ix + iy

    packed = _pack_big(w_in, w_out, w_up, w_down).astype(BF16)
    first_w_in = _gather_layer(packed[0, :1024], FIRST_GATHER_ID)
    gathered = [_gather_layer(packed[l, 1024 * (l == 0):], GATHER_IDS[l]) for l in range(DEPTH)]

    def with_own(got, own):
        return lax.dynamic_update_slice(got, own[None], (chip, 0, 0))

    def full_w_in(shares):
        def cols(lo, hi):
            out = []
            for k in range(4):
                a, b = max(lo, k * n_in), min(hi, (k + 1) * n_in)
                if a < b:
                    out.append(shares[k, :1024, a - k * n_in:b - k * n_in])
            return out

        pieces = [pc for first, last, _ in sorted(IN_COL_SEGMENTS, key=lambda seg: seg[0] + seg[2])
                  for pc in cols(first, last)]
        return jnp.concatenate(pieces + [jnp.zeros((1024, D_IN_PAD - D_IN), shares.dtype)], axis=-1)

    def full_rest(shares):
        return (jnp.concatenate([shares[k, 2048:] for k in W_OUT_CHIP_ORDER], axis=0),
                jnp.concatenate([shares[k, :1024] for k in range(4)], axis=-1),
                jnp.concatenate([shares[k, 1024:2048] for k in range(4)], axis=0))

    def layer_weights(l, x):
        got, own = gathered[l], packed[l]
        if l:
            got, own, x = lax.optimization_barrier((got, own, x))
            shares = with_own(got, own)
            rest = full_rest(shares[:, 1024:])
            return x, full_w_in(shares), lambda v: (v,) + rest

        def late(v):
            got_v, own_v, v = lax.optimization_barrier((got, own, v))
            return (v,) + full_rest(with_own(got_v, own_v[1024:]))

        return x, full_w_in(with_own(first_w_in, own[:1024])), late

    n_gc, n_lc = gdn_conv_w.shape[-1], lru_conv_w.shape[-1]
    conv_full = dict(
        gdn_conv_w=lax.dynamic_update_slice(jnp.zeros((DEPTH, 4, 4 * n_gc), F32), gdn_conv_w * 0.5, (0, 0, chip * n_gc)),
        lru_conv_w=lax.dynamic_update_slice(jnp.zeros((DEPTH, 4, 4 * n_lc), F32), lru_conv_w * 0.5, (0, 0, chip * n_lc)))
    conv_shapes = {k: v.shape for k, v in conv_full.items()}
    conv_full = _unflatten_small(_allreduce_small("convw", _flatten_small(conv_full), CONV_SUM_ID), conv_shapes)

    small = {k: args[k] for k in SMALL}
    small.update(conv_full)

    n_in = D_IN // 4

    def start_scatter(l, b):
        pieces = []
        for k in range(4):
            cols = jnp.concatenate(_orig_cols(b["w_in"], k * n_in, (k + 1) * n_in), axis=1)
            r0 = 256 * W_OUT_CHIP_ORDER.index(k)
            pieces += [jnp.pad(cols, ((0, 0), (0, 1024 - n_in))), b["w_up"][:, k * 1024:(k + 1) * 1024],
                       b["w_down"][k * 1024:(k + 1) * 1024], b["w_out"][r0:r0 + 256]]
        per_chip = jnp.concatenate(pieces, axis=0).reshape(4, LAYER_ROWS, 1024)
        return _scatter_start(per_chip, chip, core, l)

    loss, grad_x, started, sg = _local_step(x[0], loss_target[0], small, layer_weights, start_scatter)

    sg["loss"] = loss[0, :1]
    shapes = {k: v.shape for k, v in sg.items()}
    sg = _unflatten_small(_allreduce_small("smallgrad", _flatten_small(sg), GRAD_SUM_ID), shapes)
    loss_out = sg.pop("loss")[0]

    buf = None
    for l in reversed(range(DEPTH)):
        own, got = started[l]
        buf = _rs_add_chips(own, got, core, l, buf)
    g_in, g_out, g_up, g_down = _unpack_big(_rs_share(buf).reshape(DEPTH, LAYER_ROWS, 1024))

    grads = dict(sg)
    grads["gdn_conv_w"] = lax.dynamic_slice_in_dim(sg["gdn_conv_w"], chip * n_gc, n_gc, axis=2)
    grads["lru_conv_w"] = lax.dynamic_slice_in_dim(sg["lru_conv_w"], chip * n_lc, n_lc, axis=2)
    grads.update(w_in=g_in, w_out=g_out, w_up=g_up, w_down=g_down)

    names = ['norm1_g', 'w_in', 'hgrn_lb_logits', 'hgrn_norm_g', 'gdn_conv_w', 'gdn_a_log', 'gdn_dt_bias', 'gdn_norm_g',
             'lru_conv_w', 'lru_conv_b', 'lru_w_a', 'lru_b_a', 'lru_w_x', 'lru_b_x', 'lru_lambda', 'w_out', 'norm2_g',
             'w_up', 'w_down', 'final_norm_g']
    big_names = ("w_in", "w_out", "w_up", "w_down")
    delta, new_m, new_v = {}, {}, {}
    for k in big_names:
        delta[k], new_m[k], new_v[k] = _adam("adam_" + k, args[k], grads[k], args["m_" + k], args["v_" + k])
    small_names = [k for k in names if k not in big_names]
    shapes = {k: args[k].shape for k in small_names}
    flat = [_flatten_small({k: src(k) for k in small_names})
            for src in (lambda k: args[k], lambda k: grads[k], lambda k: args["m_" + k], lambda k: args["v_" + k])]
    outs = _ew("adam_small", _adam_fn, flat, 3)
    for dst, o in zip((delta, new_m, new_v), outs):
        dst.update(_unflatten_small(o, shapes))
    return (loss_out, grad_x[None], *[grads[k] for k in names], *[delta[k] for k in names],
            *[new_m[k] for k in names], *[new_v[k] for k in names])
```

```python
import functools

import jax
import jax.numpy as jnp
from jax import lax
from jax.experimental import pallas as pl
from jax.experimental.pallas import tpu as pltpu
from jax.experimental.pallas import tpu_sc as plsc

F32 = jnp.float32
BF16 = jnp.bfloat16
MESH = pl.DeviceIdType.MESH

DEPTH = 4
D_MODEL = 1024
A_W, B_W, C_W = 256, 512, 256
A_HD, B_HD = 64, 128
B_NH = B_W // B_HD
D_IN = 3592
D_IN_PAD = 3840
COL_BZ, COL_AQ, COL_AF, COL_AI, COL_AG, COL_CX, COL_CY = 1536, 2048, 2304, 2560, 2816, 3072, 3328
W_OUT_CHIP_ORDER = (1, 2, 0, 3)
GATE0 = 3584
ROWS_NARROW = 1024
CHUNK = 64
SUB = 16
EPS = 1e-6
TINY = 1e-30
MASKED_EXPONENT = -1e30
RG_C = 8.0
LR, B1, B2, AEPS, WD, STEP = 0.001, 0.9, 0.999, 1e-08, 0.01, 10
VMEM_LIMIT = 56 * 1024 * 1024
PACK_ROWS = 13 * 1024

NN = ((1,), (0,))
NT = ((1,), (1,))
TN = ((0,), (0,))


DIMS = {"nn": NN, "nt": NT, "tn": TN}


def _split(a, n):
    parts, r = [], a
    for i in range(n):
        p = r.astype(BF16)
        parts.append(p)
        if i + 1 < n:
            r = r - p.astype(F32)
    return parts


def _split_dot(a, b, mode, na, nb):
    (ca,), (cb,) = DIMS[mode]
    dims = ((DIMS[mode]), ((), ())) if a.ndim == 2 else (((ca + 1,), (cb + 1,)), ((0,), (0,)))
    pa, pb = _split(a, na), _split(b, nb)
    acc = None
    for i in range(na):
        for j in range(nb):
            if i + j < max(na, nb):
                t = lax.dot_general(pa[i], pb[j], dims, preferred_element_type=F32)
                acc = t if acc is None else acc + t
    return acc


@functools.partial(jax.custom_vjp, nondiff_argnums=(2, 3, 4, 5))
def _mdot(a, b, mode, na, nb, need):
    return _split_dot(a, b, mode, na, nb)


def _mdot_fwd(a, b, mode, na, nb, need):
    return _split_dot(a, b, mode, na, nb), (a, b)


def _mdot_bwd(mode, na, nb, need, res, ct):
    a, b = res
    n = max(na, nb)
    da, db = jnp.zeros_like(a), jnp.zeros_like(b)
    if mode == "nn":
        if need[0]:
            da = _split_dot(ct, b, "nt", n, nb)
        if need[1]:
            db = _split_dot(a, ct, "tn", na, n)
    elif mode == "nt":
        if need[0]:
            da = _split_dot(ct, b, "nn", n, nb)
        if need[1]:
            db = _split_dot(ct, a, "tn", n, na)
    else:
        if need[0]:
            da = _split_dot(b, ct, "nt", nb, n)
        if need[1]:
            db = _split_dot(a, ct, "nn", na, n)
    return da, db


_mdot.defvjp(_mdot_fwd, _mdot_bwd)
BOTH = (True, True)


def _bdot(a, b, mode="nn"):
    return _mdot(a, b, mode, 1, 1, BOTH)


def _group_sum(x, ones):
    return _mdot(x, ones, "nn", 2, 1, (True, False))


def _running_sum(x):
    shape = x.shape[:-2] + (CHUNK, CHUNK)
    tri = (_iota(shape, len(shape) - 1) <= _iota(shape, len(shape) - 2)).astype(F32)
    return _mdot(tri, x, "nn", 1, 3, (False, True))


def _iota(shape, d):
    return lax.broadcasted_iota(jnp.int32, shape, d)


def _block_ones(width, group):
    return (_iota((width, width), 0) // group == _iota((width, width), 1) // group).astype(F32)


def _params(sem):
    return pltpu.CompilerParams(dimension_semantics=sem, vmem_limit_bytes=VMEM_LIMIT)


def _tile(dim, prefs):
    for p in prefs:
        if dim % p == 0:
            return p
    return dim


def _whole(shape):
    return pl.BlockSpec(shape, lambda i: (0,) * len(shape))


def _rowspec(t, w, col0=0):
    cb = col0 // w
    return pl.BlockSpec((t, w), lambda i: (i, cb))


def _cols(arr, col0, width):
    assert col0 % width == 0
    return (arr, col0, width)


def _row_operands(rows, t):
    arrs, specs = [], []
    for r in rows:
        arr, col0, w = r if isinstance(r, tuple) else (r, 0, r.shape[1])
        arrs.append(arr)
        specs.append(_rowspec(t, w, col0))
    return arrs, specs


def _rowwise_fwd(name, fn, params, rows, out_widths, out_dtypes, t, into=None):
    rows, row_specs = _row_operands(rows, t)
    s = rows[0].shape[0]
    n_p, n_r = len(params), len(rows)
    n_buf = 0 if into is None or into[2] is None else 1

    def body(*refs):
        p = [r[...] for r in refs[:n_p]]
        xs = [r[...].astype(F32) for r in refs[n_p:n_p + n_r]]
        outs = fn(pl.program_id(0) * t, *p, *xs)
        for o_ref, o in zip(refs[n_p + n_r + n_buf:], outs):
            o_ref[...] = o.astype(o_ref.dtype)

    out_specs = [_rowspec(t, w) for w in out_widths]
    out_shape = [jax.ShapeDtypeStruct((s, w), d) for w, d in zip(out_widths, out_dtypes)]
    extra = []
    if into is not None:
        out_specs = [_rowspec(t, out_widths[0], into[0])]
        out_shape = [jax.ShapeDtypeStruct((s, into[1]), out_dtypes[0])]
        extra = [into[2]] * n_buf
    return pl.pallas_call(
        body, grid=(s // t,),
        in_specs=[_whole(p.shape) for p in params] + row_specs + [pl.BlockSpec(memory_space=pl.ANY)] * n_buf,
        out_specs=out_specs, out_shape=out_shape, input_output_aliases={n_p + n_r: 0} if n_buf else {},
        name=name, compiler_params=_params(("arbitrary",)))(*params, *rows, *extra)


def _into_spec(t, width, col0, rev_steps=None):
    cb = col0 // width
    assert col0 % width == 0
    if rev_steps is None:
        return pl.BlockSpec((t, width), lambda i: (i, cb))
    return pl.BlockSpec((t, width), lambda i: (rev_steps - 1 - i, cb))


def _rowwise_bwd(name, fn, params, rows, douts, t, need=None, res=None, dtypes=None, copy16=False, pack=None,
                 dout_fn=None):
    rows, row_specs = _row_operands(rows, t)
    douts, dout_specs = _row_operands(douts, t)
    s = rows[0].shape[0]
    n_p, n_r, n_o = len(params), len(rows), len(douts)
    need = [True] * n_r if need is None else need
    n_res = 0 if res is None else 1
    n_buf = 0 if pack is None or pack[3] is None else 1

    def body(*refs):
        p = [r[...] for r in refs[:n_p]]
        xs = [r[...].astype(F32) for r in refs[n_p:n_p + n_r]]
        dys = [r[...].astype(F32) for r in refs[n_p + n_r:n_p + n_r + n_o]]
        k = n_p + n_r + n_o
        res_ref = refs[k] if n_res else None
        k += n_res + n_buf
        dp_refs = refs[k:k + n_p]
        dx_refs = list(refs[k + n_p:])
        row0 = pl.program_id(0) * t
        _, vjp = jax.vjp(lambda *a: tuple(fn(row0, *a)), *p, *xs)
        g = vjp(tuple(dys) if dout_fn is None else tuple(dout_fn(*dys)))

        @pl.when(pl.program_id(0) == 0)
        def _():
            for r in dp_refs:
                r[...] = jnp.zeros_like(r)

        for r, gp in zip(dp_refs, g[:n_p]):
            r[...] += gp
        gx = [gi for gi, nd in zip(g[n_p:], need) if nd]
        if n_res:
            gx[0] = gx[0] + res_ref[...]
        if copy16:
            gx.append(gx[0])
        if pack is not None:
            pieces = [gx[j].astype(BF16) for j in pack[0]]
            missing = pack[2] - sum(pc.shape[1] for pc in pieces)
            if missing:
                pieces.append(jnp.zeros((t, missing), BF16))
            dx_refs.pop()[...] = pieces[0] if len(pieces) == 1 else jnp.concatenate(pieces, axis=1)
            gx = [gi for j, gi in enumerate(gx) if j not in pack[0]]
        for r, gi in zip(dx_refs, gx):
            r[...] = gi.astype(r.dtype)

    widths = [sp.block_shape[1] for sp, nd in zip(row_specs, need) if nd]
    dtypes = [F32] * len(widths) if dtypes is None else list(dtypes)
    if copy16:
        widths, dtypes = widths + widths[:1], dtypes + [BF16]
    out_specs = [_rowspec(t, w) for w in widths]
    out_shape = [jax.ShapeDtypeStruct((s, w), d) for w, d in zip(widths, dtypes)]
    extra, aliases = [], {}
    if pack is not None:
        keep = [j for j in range(len(widths)) if j not in pack[0]]
        out_specs = [out_specs[j] for j in keep] + [_into_spec(t, pack[2], pack[1])]
        out_shape = [out_shape[j] for j in keep] + [jax.ShapeDtypeStruct((s, D_IN_PAD), BF16)]
        if n_buf:
            extra = [pack[3]]
            aliases = {n_p + n_r + n_o + n_res: n_p + len(out_shape) - 1}
    outs = pl.pallas_call(
        body, grid=(s // t,),
        in_specs=([_whole(p.shape) for p in params] + row_specs + dout_specs
                  + ([_rowspec(t, res.shape[1])] if n_res else []) + [pl.BlockSpec(memory_space=pl.ANY)] * n_buf),
        out_specs=[_whole(p.shape) for p in params] + out_specs,
        out_shape=[jax.ShapeDtypeStruct(p.shape, F32) for p in params] + out_shape, input_output_aliases=aliases,
        name=name, compiler_params=_params(("arbitrary",)))(*params, *rows, *douts, *([res] if n_res else []), *extra)
    return list(outs[:n_p]), list(outs[n_p:])


def _rmsnorm_fn(row0, g, x):
    y = x * lax.rsqrt(jnp.mean(x * x, axis=-1, keepdims=True) + EPS)
    return (y * g,)


def _hgrn_pre_fn(layer, row0, logits, aq, af):
    m = jnp.max(logits, axis=0, keepdims=True)
    e = jnp.exp(logits - m)
    p = e / jnp.sum(e, axis=0, keepdims=True)
    if layer == 0:
        lb = jnp.zeros((1, A_W), F32)
    else:
        acc = p[1:2]
        for j in range(2, layer + 1):
            acc = acc + p[j:j + 1]
        lb = jnp.minimum(jnp.maximum(acc, 0.0), 1.0 - EPS)
    sig = jax.nn.sigmoid(af)
    f = lb + (1.0 - lb) * sig
    log_f = jnp.log(jnp.maximum(f, TINY))
    k = (1.0 - lb) * jax.nn.sigmoid(-af)
    return jax.nn.silu(aq), k, log_f


def _hgrn_post_fn(row0, g, o, ag):
    ms = _group_sum(o * o, _block_ones(A_W, A_HD)) * (1.0 / A_HD)
    return (o * lax.rsqrt(ms + EPS) * g * jax.nn.silu(ag),)


def _gdn_pre_fn(row0, a_log, dt_bias, qkvc, gate):
    act = jax.nn.silu(qkvc)
    q, k, v = act[:, :B_W], act[:, B_W:2 * B_W], act[:, 2 * B_W:]
    bo = _block_ones(B_W, B_HD)
    q = q * lax.rsqrt(_group_sum(q * q, bo) + EPS) * (B_HD ** -0.5)
    k = k * lax.rsqrt(_group_sum(k * k, bo) + EPS)
    beta = jax.nn.sigmoid(gate)
    log_alpha = -jnp.exp(a_log) * jax.nn.softplus(gate + dt_bias)
    t = gate.shape[0]
    spread = lambda z, lane0: jnp.concatenate(
        [jnp.broadcast_to(z[:, lane0 + h:lane0 + h + 1], (t, B_HD)) for h in range(B_NH)], axis=1)
    return q, k, v, spread(beta, 0), spread(log_alpha, B_NH)


def _gdn_post_fn(row0, g, o, bz):
    ms = _group_sum(o * o, _block_ones(B_W, B_HD)) * (1.0 / B_HD)
    return (o * lax.rsqrt(ms + EPS) * g * jax.nn.silu(bz),)


def _lru_pre_fn(row0, w_a, b_a, w_x, b_x, lam, xc):
    r = jax.nn.sigmoid(_mdot(xc, w_a, "nn", 2, 2, BOTH) + b_a)
    i = jax.nn.sigmoid(_mdot(xc, w_x, "nn", 2, 2, BOTH) + b_x)
    log_a = -RG_C * r * jax.nn.softplus(-lam)
    a = jnp.exp(log_a)
    one_minus = -jnp.tanh(log_a) * (a * a + 1.0)
    mult = jnp.sqrt(jnp.maximum(one_minus, EPS))
    pos = row0 + _iota(xc.shape, 0)
    mult = jnp.where(pos == 0, 1.0, mult)
    return a, mult * i * xc


def _lru_post_fn(row0, h, cy):
    return (jax.nn.gelu(cy, approximate=True) * h,)


def _relu2(u):
    r = jnp.maximum(u, 0.0)
    return r * r


def _drelu2(acc, u):
    return acc * (2.0 * jnp.maximum(u, 0.0))


MM_VMEM_BUDGET = 44 * 1024 * 1024


def _mm_tiles(mode, m, n, k, a_bytes, b_bytes, n_extra):
    best = None
    for tk in {k, 4096, 2048, 1024, 512, 256}:
        for tn in {n, 1280, 1024, 768, 512, 256, 128}:
            for tm in {m, 1024, 512, 256, 128}:
                if m % tm or n % tn or k % tk or tm > 1024:
                    continue
                tiles = tm * tk * a_bytes + tk * tn * b_bytes + (1 + n_extra) * tm * tn * 4
                if 2 * tiles + (tm * tn * 4 if tk < k else 0) > MM_VMEM_BUDGET:
                    continue
                a_reads = 1 if tk == k else n // tn
                b_reads = 1 if (tk == k and tn == n) else m // tm
                cost = m * k * a_bytes * a_reads + k * n * b_bytes * b_reads + m * n * 4 * (k // tk - 1)
                if best is None or (cost, -tm) < best[0]:
                    best = ((cost, -tm), (tm, tn, tk))
    if best is None:
        raise ValueError("no matmul tiling fits VMEM")
    return best[1]


def _matmul(name, a, b, mode, a_fn=None, res=None, epi=None, epi_in=None, out_dtype=F32, also=None):
    if mode == "nn":
        (m, k), n = a.shape, b.shape[1]
    elif mode == "nt":
        (m, k), n = a.shape, b.shape[0]
    else:
        (k, m), n = a.shape, b.shape[1]
    n_extra = (res is not None) + (epi_in is not None) + (also is not None)
    tm, tn, tk = _mm_tiles(mode, m, n, k, a.dtype.itemsize, b.dtype.itemsize, n_extra)
    nk = k // tk
    a_spec = {"nn": pl.BlockSpec((tm, tk), lambda i, j, kk: (i, kk)),
              "nt": pl.BlockSpec((tm, tk), lambda i, j, kk: (i, kk)),
              "tn": pl.BlockSpec((tk, tm), lambda i, j, kk: (kk, i))}[mode]
    b_spec = {"nn": pl.BlockSpec((tk, tn), lambda i, j, kk: (kk, j)),
              "nt": pl.BlockSpec((tn, tk), lambda i, j, kk: (j, kk)),
              "tn": pl.BlockSpec((tk, tn), lambda i, j, kk: (kk, j))}[mode]
    o_spec = pl.BlockSpec((tm, tn), lambda i, j, kk: (i, j))
    dims = {"nn": NN, "nt": NT, "tn": TN}[mode]
    extra = [x for x in (res, epi_in) if x is not None]

    def body(*refs):
        a_ref, b_ref = refs[0], refs[1]
        rest = list(refs[2:])
        res_ref = rest.pop(0) if res is not None else None
        epi_ref = rest.pop(0) if epi_in is not None else None
        o_ref = rest.pop(0)
        also_ref = rest.pop(0) if also is not None else None
        acc = rest[0] if nk > 1 else None
        kk = pl.program_id(2)
        at = a_ref[...]
        if a_fn is not None:
            at = a_fn(at.astype(F32))
        part = lax.dot_general(at.astype(BF16), b_ref[...].astype(BF16), (dims, ((), ())),
                               preferred_element_type=F32)

        def finish(out):
            if res_ref is not None:
                out = out + res_ref[...]
            if epi is not None:
                out = epi(out, epi_ref[...])
            o_ref[...] = out.astype(o_ref.dtype)
            if also is not None:
                also_ref[...] = also(out).astype(also_ref.dtype)

        if nk == 1:
            finish(part)
        else:
            @pl.when(kk == 0)
            def _():
                acc[...] = part

            @pl.when(jnp.logical_and(kk > 0, kk < nk - 1))
            def _():
                acc[...] += part

            @pl.when(kk == nk - 1)
            def _():
                finish(acc[...] + part)

    out_shape = [jax.ShapeDtypeStruct((m, n), out_dtype)] + ([jax.ShapeDtypeStruct((m, n), BF16)] if also else [])
    outs = pl.pallas_call(
        body, grid=(m // tm, n // tn, nk),
        in_specs=[a_spec, b_spec] + [o_spec] * len(extra), out_specs=[o_spec] * len(out_shape), out_shape=out_shape,
        scratch_shapes=[pltpu.VMEM((tm, tn), F32)] if nk > 1 else [],
        name=name, compiler_params=_params(("arbitrary", "arbitrary", "arbitrary")))(a, b, *extra)
    return outs if also else outs[0]


HALO = 8


def _conv_fwd(name, x, col0, width, w, bias, t=512):
    s = x.shape[0]
    cb = col0 // width
    hb = t // HALO

    def body(x_ref, halo_ref, w_ref, b_ref, y_ref):
        i = pl.program_id(0)
        halo = jnp.where(i == 0, 0.0, halo_ref[...])
        xp = jnp.concatenate([halo, x_ref[...]], axis=0)
        wv = w_ref[...]
        y = b_ref[...] + wv[0:1] * xp[HALO - 3:HALO - 3 + t]
        for j in range(1, 4):
            y = y + wv[j:j + 1] * xp[HALO - 3 + j:HALO - 3 + j + t]
        y_ref[...] = y

    return pl.pallas_call(
        body, grid=(s // t,),
        in_specs=[pl.BlockSpec((t, width), lambda i: (i, cb)),
                  pl.BlockSpec((HALO, width), lambda i: (jnp.maximum(i * hb - 1, 0), cb)),
                  _whole(w.shape), _whole(bias.shape)],
        out_specs=_rowspec(t, width), out_shape=jax.ShapeDtypeStruct((s, width), F32),
        name=name, compiler_params=_params(("arbitrary",)))(x, x, w, bias)


def _conv_bwd(name, dy, x, col0, width, w, buf, t=512):
    s = x.shape[0]
    cb = col0 // width
    hb = t // HALO
    nblk = s // t

    def body(dy_ref, dyn_ref, x_ref, halo_ref, w_ref, buf_ref, dx_ref, dw_ref, db_ref):
        i = pl.program_id(0)
        dyv = dy_ref[...]
        nxt = jnp.where(i == nblk - 1, 0.0, dyn_ref[...])
        dyp = jnp.concatenate([dyv, nxt], axis=0)
        wv = w_ref[...]
        dx = wv[3:4] * dyv
        for j in range(3):
            dx = dx + wv[j:j + 1] * dyp[3 - j:3 - j + t]
        dx_ref[...] = dx.astype(dx_ref.dtype)
        halo = jnp.where(i == 0, 0.0, halo_ref[...])
        xp = jnp.concatenate([halo, x_ref[...]], axis=0)
        dw = jnp.concatenate(
            [jnp.sum(dyv * xp[HALO - 3 + j:HALO - 3 + j + t], axis=0, keepdims=True) for j in range(4)], axis=0)

        @pl.when(i == 0)
        def _():
            dw_ref[...] = jnp.zeros_like(dw_ref)
            db_ref[...] = jnp.zeros_like(db_ref)

        dw_ref[...] += dw
        db_ref[...] += jnp.sum(dyv, axis=0, keepdims=True)

    return pl.pallas_call(
        body, grid=(nblk,),
        in_specs=[_rowspec(t, width),
                  pl.BlockSpec((HALO, width), lambda i: (jnp.minimum((i + 1) * hb, s // HALO - 1), 0)),
                  pl.BlockSpec((t, width), lambda i: (i, cb)),
                  pl.BlockSpec((HALO, width), lambda i: (jnp.maximum(i * hb - 1, 0), cb)),
                  _whole(w.shape), pl.BlockSpec(memory_space=pl.ANY)],
        out_specs=[_into_spec(t, width, col0), _whole((4, width)), _whole((1, width))],
        out_shape=[jax.ShapeDtypeStruct(buf.shape, buf.dtype), jax.ShapeDtypeStruct((4, width), F32),
                   jax.ShapeDtypeStruct((1, width), F32)],
        input_output_aliases={5: 0},
        name=name, compiler_params=_params(("arbitrary",)))(dy, dy, x, x, w, buf)


def _hgrn_chunk(st, q, k, v, lf):
    cum = _running_sum(lf)
    tot = jnp.sum(lf, axis=0, keepdims=True)
    bm = _iota((A_W, A_W), 0) // A_HD == _iota((A_W, A_W), 1) // A_HD
    bo = bm.astype(F32)
    o_inter = _bdot(q * jnp.exp(cum), st, "nt")
    kd = k * jnp.exp(tot - cum)
    st_new = st * jnp.exp(tot) + jnp.where(bm, _bdot(v, kd, "tn"), 0.0)
    lane_h = _iota((SUB, A_W), 1) // A_HD
    n_h = A_W // A_HD
    t3 = _iota((SUB, SUB, A_W), 0)
    s3 = _iota((SUB, SUB, A_W), 1)
    outs = []
    for i in range(CHUNK // SUB):
        sl = slice(i * SUB, (i + 1) * SUB)
        qi, ki, vi, ci = q[sl], k[sl], v[sl], cum[sl]
        e = jnp.exp(jnp.where(s3 <= t3, ci[:, None, :] - ci[None, :, :], MASKED_EXPONENT))
        d3 = qi[:, None, :] * ki[None, :, :] * e
        ds = _mdot(d3.reshape(SUB * SUB, A_W), bo, "nn", 1, 1, (True, False)).reshape(SUB, SUB, A_W)
        oi = jnp.sum(ds * vi[None, :, :], axis=1)
        if i > 0:
            n = i * SUB
            cb = cum[n - 1:n]
            qt = qi * jnp.exp(ci - cb)
            kt = k[:n] * jnp.exp(cb - cum[:n])
            qs = jnp.concatenate([jnp.where(lane_h == h, qt, 0.0) for h in range(n_h)], axis=0)
            p = _bdot(_bdot(qs, kt, "nt"), v[:n])
            for h in range(n_h):
                oi = oi + jnp.where(lane_h == h, p[h * SUB:(h + 1) * SUB], 0.0)
        outs.append(oi)
    return st_new, o_inter + jnp.concatenate(outs, axis=0)


HGRN_CHUNKS = 2


def _hgrn_fwd(name, q, k, v, lf):
    s = q.shape[0]
    n = s // CHUNK
    g = HGRN_CHUNKS

    def body(q_ref, k_ref, v_ref, lf_ref, o_ref, sts_ref, st_ref):
        @pl.when(pl.program_id(0) == 0)
        def _():
            st_ref[...] = jnp.zeros_like(st_ref)

        st = st_ref[...]
        for c in range(g):
            rows = pl.ds(c * CHUNK, CHUNK)
            sts_ref[c] = st
            st, o = _hgrn_chunk(st, q_ref[rows], k_ref[rows], v_ref[rows], lf_ref[rows])
            o_ref[rows] = o
        st_ref[...] = st

    cs = _rowspec(g * CHUNK, A_W)
    return pl.pallas_call(
        body, grid=(n // g,), in_specs=[cs] * 4,
        out_specs=[cs, pl.BlockSpec((g, A_W, A_W), lambda i: (i, 0, 0))],
        out_shape=[jax.ShapeDtypeStruct((s, A_W), F32), jax.ShapeDtypeStruct((n, A_W, A_W), F32)],
        scratch_shapes=[pltpu.VMEM((A_W, A_W), F32)],
        name=name, compiler_params=_params(("arbitrary",)))(q, k, v, lf)


def _hgrn_bwd(name, q, k, v, lf, sts, do, buf, col0):
    s = q.shape[0]
    n = s // CHUNK
    g = HGRN_CHUNKS
    nsteps = n // g

    def body(q_ref, k_ref, v_ref, lf_ref, sts_ref, do_ref, buf_ref, dq_ref, dk_ref, dv_ref, dlf_ref, dst_ref):
        @pl.when(pl.program_id(0) == 0)
        def _():
            dst_ref[...] = jnp.zeros_like(dst_ref)

        dst = dst_ref[...]
        for c in reversed(range(g)):
            rows = pl.ds(c * CHUNK, CHUNK)
            _, vjp = jax.vjp(_hgrn_chunk, sts_ref[c], q_ref[rows], k_ref[rows], v_ref[rows], lf_ref[rows])
            ct = vjp((dst, do_ref[rows]))
            dst = ct[0]
            for r, gi in zip((dq_ref, dk_ref, dv_ref, dlf_ref), ct[1:]):
                r[rows] = gi.astype(r.dtype)
        dst_ref[...] = dst

    rs = pl.BlockSpec((g * CHUNK, A_W), lambda i: (nsteps - 1 - i, 0))
    rows = jax.ShapeDtypeStruct((s, A_W), F32)
    return pl.pallas_call(
        body, grid=(nsteps,),
        in_specs=[rs] * 4 + [pl.BlockSpec((g, A_W, A_W), lambda i: (nsteps - 1 - i, 0, 0)), rs,
                             pl.BlockSpec(memory_space=pl.ANY)],
        out_specs=[rs, rs, _into_spec(g * CHUNK, A_W, col0, nsteps), rs],
        out_shape=[rows, rows, jax.ShapeDtypeStruct(buf.shape, buf.dtype), rows],
        input_output_aliases={6: 2}, scratch_shapes=[pltpu.VMEM((A_W, A_W), F32)],
        name=name, compiler_params=_params(("arbitrary",)))(q, k, v, lf, sts, do, buf)


PREP_CHUNKS = 4
REC_CHUNKS = 8
DEC_ROWS = 8


def _to_heads(x, rows=CHUNK, width=B_HD):
    n = x.shape[0] // rows
    x3 = x.reshape(n, rows, x.shape[1])
    return jnp.concatenate([x3[:, :, h * B_HD:h * B_HD + width] for h in range(B_NH)], axis=0)


def _from_heads(y):
    n = y.shape[0] // B_NH
    x3 = jnp.concatenate([y[h * n:(h + 1) * n] for h in range(B_NH)], axis=2)
    return x3.reshape(n * y.shape[1], B_NH * B_HD)


@jax.custom_vjp
def _unit_lower_inverse(a_mat):
    nil = -a_mat
    eye = (_iota(a_mat.shape, 1) == _iota(a_mat.shape, 2)).astype(F32)
    inv = eye + nil
    sq = 2
    while sq < CHUNK:
        nil = _split_dot(nil, nil, "nn", 2, 2)
        inv = inv + _split_dot(inv, nil, "nn", 2, 2)
        sq *= 2
    return inv


def _unit_lower_inverse_fwd(a_mat):
    inv = _unit_lower_inverse(a_mat)
    return inv, inv


def _unit_lower_inverse_bwd(inv, ct):
    return (-_split_dot(_split_dot(inv, ct, "tn", 2, 2), inv, "nt", 2, 2),)


_unit_lower_inverse.defvjp(_unit_lower_inverse_fwd, _unit_lower_inverse_bwd)


@jax.custom_vjp
def _known_inverse(a_mat, inv):
    return inv


def _known_inverse_fwd(a_mat, inv):
    return inv, inv


def _known_inverse_bwd(inv, ct):
    return _unit_lower_inverse_bwd(inv, ct)[0], jnp.zeros_like(inv)


_known_inverse.defvjp(_known_inverse_fwd, _known_inverse_bwd)


def _gdn_prep(q, k, v, beta, la, inv_saved=None):
    q, k, v, beta, la = [_to_heads(z) for z in (q, k, v, beta, la)]
    nb = q.shape[0]
    r = _iota((nb, CHUNK, CHUNK), 1)
    c = _iota((nb, CHUNK, CHUNK), 2)
    causal = c <= r
    g = _running_sum(la)
    tot = jnp.sum(la, axis=1, keepdims=True)
    gcol = g[:, :, :CHUNK]
    decay = jnp.exp(jnp.where(causal, gcol - jnp.swapaxes(gcol, 1, 2), MASKED_EXPONENT))
    kb = k * beta
    a_mat = jnp.where(c < r, _bdot(kb, k, "nt") * decay, 0.0)
    if inv_saved is None:
        inv = _unit_lower_inverse(a_mat)
    else:
        inv = _known_inverse(a_mat, _to_heads(inv_saved, width=CHUNK))
    eg = jnp.exp(g)
    u = _bdot(inv, v * beta)
    w = _bdot(inv, kb * eg)
    qk = jnp.where(causal, _bdot(q, k, "nt") * decay, 0.0)
    widen = lambda z: jnp.concatenate([z, jnp.zeros((nb, CHUNK, B_HD - CHUNK), F32)], axis=2)
    dec = jnp.broadcast_to(jnp.exp(tot), (nb, DEC_ROWS, B_HD))
    return tuple(_from_heads(z) for z in (u, w, widen(qk), q * eg, k * jnp.exp(tot - g), dec, widen(inv)))


def _gdn_rec(st, u, w, qk, qd, kd, dec):
    u, w, qd, kd = [_to_heads(z) for z in (u, w, qd, kd)]
    qk = _to_heads(qk, width=CHUNK)
    v_new = u - _bdot(w, st)
    o = _bdot(qd, st) + _bdot(qk, v_new)
    st_new = st * _to_heads(dec, rows=DEC_ROWS)[:, :1, :1] + _bdot(kd, v_new, "tn")
    return st_new, _from_heads(o)


def _gdn_prep_fwd(name, q, k, v, beta, la):
    s = q.shape[0]
    t = PREP_CHUNKS * CHUNK
    td = PREP_CHUNKS * DEC_ROWS

    def body(*refs):
        outs = _gdn_prep(*[r[...] for r in refs[:5]])
        for o_ref, o in zip(refs[5:], outs):
            o_ref[...] = o

    rows, dec_rows = jax.ShapeDtypeStruct((s, B_W), F32), jax.ShapeDtypeStruct((s // CHUNK * DEC_ROWS, B_W), F32)
    return pl.pallas_call(
        body, grid=(s // t,), in_specs=[_rowspec(t, B_W)] * 5,
        out_specs=[_rowspec(t, B_W)] * 5 + [_rowspec(td, B_W), _rowspec(t, B_W)],
        out_shape=[rows] * 5 + [dec_rows, rows],
        name=name, compiler_params=_params(("arbitrary",)))(q, k, v, beta, la)


def _gdn_prep_bwd(name, q, k, v, beta, la, inv, cts):
    s = q.shape[0]
    t = PREP_CHUNKS * CHUNK
    td = PREP_CHUNKS * DEC_ROWS

    def body(*refs):
        inv_saved = refs[5][...]
        _, vjp = jax.vjp(lambda *a: _gdn_prep(*a, inv_saved=inv_saved)[:6], *[r[...] for r in refs[:5]])
        g = vjp(tuple(r[...] for r in refs[6:12]))
        for o_ref, o in zip(refs[12:], g):
            o_ref[...] = o

    return pl.pallas_call(
        body, grid=(s // t,), in_specs=[_rowspec(t, B_W)] * 11 + [_rowspec(td, B_W)],
        out_specs=[_rowspec(t, B_W)] * 5, out_shape=[jax.ShapeDtypeStruct((s, B_W), F32)] * 5,
        name=name, compiler_params=_params(("arbitrary",)))(q, k, v, beta, la, inv, *cts)


def _gdn_rec_fwd(name, u, w, qk, qd, kd, dec):
    s = u.shape[0]
    n = s // CHUNK
    g = REC_CHUNKS

    def body(u_ref, w_ref, qk_ref, qd_ref, kd_ref, dec_ref, o_ref, sts_ref, st_ref):
        @pl.when(pl.program_id(0) == 0)
        def _():
            st_ref[...] = jnp.zeros_like(st_ref)

        st = st_ref[...]
        for c in range(g):
            rows, drows = pl.ds(c * CHUNK, CHUNK), pl.ds(c * DEC_ROWS, DEC_ROWS)
            sts_ref[c] = st
            st, o = _gdn_rec(st, u_ref[rows], w_ref[rows], qk_ref[rows], qd_ref[rows], kd_ref[rows], dec_ref[drows])
            o_ref[rows] = o
        st_ref[...] = st

    cs = _rowspec(g * CHUNK, B_W)
    return pl.pallas_call(
        body, grid=(n // g,), in_specs=[cs] * 5 + [_rowspec(g * DEC_ROWS, B_W)],
        out_specs=[cs, pl.BlockSpec((g, B_NH, B_HD, B_HD), lambda i: (i, 0, 0, 0))],
        out_shape=[jax.ShapeDtypeStruct((s, B_W), F32), jax.ShapeDtypeStruct((n, B_NH, B_HD, B_HD), F32)],
        scratch_shapes=[pltpu.VMEM((B_NH, B_HD, B_HD), F32)],
        name=name, compiler_params=_params(("arbitrary",)))(u, w, qk, qd, kd, dec)


def _gdn_rec_bwd(name, u, w, qk, qd, kd, dec, sts, do):
    s = u.shape[0]
    n = s // CHUNK
    g = REC_CHUNKS
    nsteps = n // g

    def body(u_ref, w_ref, qk_ref, qd_ref, kd_ref, dec_ref, sts_ref, do_ref,
             du_ref, dw_ref, dqk_ref, dqd_ref, dkd_ref, ddec_ref, dst_ref):
        @pl.when(pl.program_id(0) == 0)
        def _():
            dst_ref[...] = jnp.zeros_like(dst_ref)

        dst = dst_ref[...]
        for c in reversed(range(g)):
            rows, drows = pl.ds(c * CHUNK, CHUNK), pl.ds(c * DEC_ROWS, DEC_ROWS)
            _, vjp = jax.vjp(_gdn_rec, sts_ref[c], u_ref[rows], w_ref[rows], qk_ref[rows], qd_ref[rows], kd_ref[rows],
                             dec_ref[drows])
            ct = vjp((dst, do_ref[rows]))
            dst = ct[0]
            for r, gi in zip((du_ref, dw_ref, dqk_ref, dqd_ref, dkd_ref), ct[1:6]):
                r[rows] = gi
            ddec_ref[drows] = ct[6]
        dst_ref[...] = dst

    rs = pl.BlockSpec((g * CHUNK, B_W), lambda i: (nsteps - 1 - i, 0))
    ds = pl.BlockSpec((g * DEC_ROWS, B_W), lambda i: (nsteps - 1 - i, 0))
    return pl.pallas_call(
        body, grid=(nsteps,),
        in_specs=[rs] * 5 + [ds, pl.BlockSpec((g, B_NH, B_HD, B_HD), lambda i: (nsteps - 1 - i, 0, 0, 0)), rs],
        out_specs=[rs] * 5 + [ds],
        out_shape=[jax.ShapeDtypeStruct((s, B_W), F32)] * 5 + [jax.ShapeDtypeStruct((n * DEC_ROWS, B_W), F32)],
        scratch_shapes=[pltpu.VMEM((B_NH, B_HD, B_HD), F32)],
        name=name, compiler_params=_params(("arbitrary",)))(u, w, qk, qd, kd, dec, sts, do)


SCAN_T = 512


def _block_scan(a, b, reverse):
    t = a.shape[0]
    rows = _iota(a.shape, 0)
    sft = 1
    while sft < t:
        if reverse:
            a_s, b_s = pltpu.roll(a, t - sft, 0), pltpu.roll(b, t - sft, 0)
            edge = rows >= t - sft
        else:
            a_s, b_s = pltpu.roll(a, sft, 0), pltpu.roll(b, sft, 0)
            edge = rows < sft
        a_s = jnp.where(edge, 1.0, a_s)
        b_s = jnp.where(edge, 0.0, b_s)
        b = a * b_s + b
        a = a * a_s
        sft *= 2
    return a, b


def _lru_scan_fwd(name, a, b):
    s, w = a.shape
    t = SCAN_T

    def body(a_ref, b_ref, h_ref, hp_ref, carry):
        @pl.when(pl.program_id(0) == 0)
        def _():
            carry[...] = jnp.zeros_like(carry)

        h_in = carry[0:1]
        ca, cb = _block_scan(a_ref[...], b_ref[...], False)
        h = ca * h_in + cb
        h_ref[...] = h
        hp_ref[...] = jnp.where(_iota(h.shape, 0) == 0, h_in, pltpu.roll(h, 1, 0))
        carry[...] = jnp.broadcast_to(h[t - 1:t], carry.shape)

    return pl.pallas_call(
        body, grid=(s // t,), in_specs=[_rowspec(t, w)] * 2, out_specs=[_rowspec(t, w)] * 2,
        out_shape=[jax.ShapeDtypeStruct((s, w), F32)] * 2, scratch_shapes=[pltpu.VMEM((8, w), F32)],
        name=name, compiler_params=_params(("arbitrary",)))(a, b)


def _lru_scan_bwd(name, a, dh):
    s, w = a.shape
    t = SCAN_T
    n = s // t

    def body(a_ref, dh_ref, g_ref, a_next, g_next):
        @pl.when(pl.program_id(0) == 0)
        def _():
            a_next[...] = jnp.zeros_like(a_next)
            g_next[...] = jnp.zeros_like(g_next)

        av = a_ref[...]
        a_up = jnp.where(_iota(av.shape, 0) == t - 1, a_next[0:1], pltpu.roll(av, t - 1, 0))
        ca, cb = _block_scan(a_up, dh_ref[...], True)
        g = ca * g_next[0:1] + cb
        g_ref[...] = g
        a_next[...] = jnp.broadcast_to(av[0:1], a_next.shape)
        g_next[...] = jnp.broadcast_to(g[0:1], g_next.shape)

    rs = pl.BlockSpec((t, w), lambda i: (n - 1 - i, 0))
    return pl.pallas_call(
        body, grid=(n,), in_specs=[rs] * 2, out_specs=rs, out_shape=jax.ShapeDtypeStruct((s, w), F32),
        scratch_shapes=[pltpu.VMEM((8, w), F32), pltpu.VMEM((8, w), F32)],
        name=name, compiler_params=_params(("arbitrary",)))(a, dh)


def _loss_fn(g, x, tgt):
    y = x * lax.rsqrt(jnp.mean(x * x, axis=-1, keepdims=True) + EPS) * g
    err = y - tgt
    return 0.5 * jnp.sum(jnp.mean(err * err, axis=-1, keepdims=True), axis=0, keepdims=True)


def _loss_and_grad(name, g, x, tgt, t=256):
    s, d = x.shape

    def body(g_ref, x_ref, t_ref, loss_ref, dx_ref, dg_ref, dx16_ref):
        val, vjp = jax.vjp(_loss_fn, g_ref[...], x_ref[...], t_ref[...])
        dg, dx, _ = vjp(jnp.ones((1, 1), F32))

        @pl.when(pl.program_id(0) == 0)
        def _():
            loss_ref[...] = jnp.zeros_like(loss_ref)
            dg_ref[...] = jnp.zeros_like(dg_ref)

        loss_ref[...] += jnp.broadcast_to(val, loss_ref.shape)
        dg_ref[...] += dg
        dx_ref[...] = dx
        dx16_ref[...] = dx.astype(BF16)

    return pl.pallas_call(
        body, grid=(s // t,), in_specs=[_whole(g.shape), _rowspec(t, d), _rowspec(t, d)],
        out_specs=[_whole((1, 128)), _rowspec(t, d), _whole(g.shape), _rowspec(t, d)],
        out_shape=[jax.ShapeDtypeStruct((1, 128), F32), jax.ShapeDtypeStruct((s, d), F32),
                   jax.ShapeDtypeStruct(g.shape, F32), jax.ShapeDtypeStruct((s, d), BF16)],
        name=name, compiler_params=_params(("arbitrary",)))(g, x, tgt)


def _ew(name, fn, ins, n_out, t=None):
    r, c = ins[0].shape
    t = _tile(r, (512, 256, 128, 64, 32, 16, 8)) if t is None else t

    def body(*refs):
        outs = fn(*[x[...] for x in refs[:len(ins)]])
        for o_ref, o in zip(refs[len(ins):], outs):
            o_ref[...] = o

    return pl.pallas_call(
        body, grid=(r // t,), in_specs=[_rowspec(t, c)] * len(ins), out_specs=[_rowspec(t, c)] * n_out,
        out_shape=[jax.ShapeDtypeStruct((r, c), F32)] * n_out,
        name=name, compiler_params=_params(("arbitrary",)))(*ins)


def _adam_fn(w, g, m, v):
    m = B1 * m + (1.0 - B1) * g
    v = B2 * v + (1.0 - B2) * (g * g)
    m_hat = m / (1.0 - B1 ** STEP)
    v_hat = v / (1.0 - B2 ** STEP)
    delta = -LR * (m_hat / (jnp.sqrt(v_hat) + AEPS) + WD * w)
    return delta, m, v


def _adam(name, w, g, m, v):
    shp = w.shape
    two = lambda z: z.reshape(-1, shp[-1])
    outs = _ew(name, _adam_fn, [two(w), two(g), two(m), two(v)], 3)
    return [o.reshape(shp) for o in outs]


LAYER_ROWS = PACK_ROWS // DEPTH
HALF_ROWS = LAYER_ROWS // 2
SUM_ROWS = HALF_ROWS // 2
GATHER_IDS = (1, 2, 3, 4)
SCATTER_IDS = (5, 6, 7, 8)
FIRST_GATHER_ID = 9
CONV_SUM_ID, GRAD_SUM_ID = 10, 11


def _mesh_pos():
    ix, iy, core = lax.axis_index("x"), lax.axis_index("y"), lax.axis_index("c")
    others = [(ix, 1 - iy), (1 - ix, iy), (1 - ix, 1 - iy)]
    return ix, iy, core, others


def _pack_big(w_in, w_out, w_up, w_down):
    pad = jnp.pad(w_in, ((0, 0), (0, 0), (0, 1024 - w_in.shape[-1])))
    return jnp.concatenate([pad, w_up, w_down, w_out], axis=1)


def _unpack_big(p):
    w_in = p[..., 0:1024, :D_IN // 4]
    return w_in, p[..., 3072:LAYER_ROWS, :], p[..., 1024:2048, :], p[..., 2048:3072, :]


def _gather_layer(packed, collective_id):
    src = jax.new_ref(packed, memory_space=pltpu.MemorySpace.HBM)
    out = jax.empty_ref(jax.ShapeDtypeStruct((4,) + packed.shape, packed.dtype), memory_space=pltpu.MemorySpace.HBM)
    dma = pltpu.SemaphoreType.DMA

    @pl.kernel(mesh=plsc.ScalarSubcoreMesh(axis_name="seq", num_cores=1), name="gather_%d" % collective_id,
               scratch_types=(dma,) * 12, compiler_params=pltpu.CompilerParams(collective_id=collective_id))
    def launch(*sems):
        send_sems, recv_sems = sems[:6], sems[6:]
        ix, iy, core, others = _mesh_pos()
        barrier = pltpu.get_barrier_semaphore()
        for peer in [(ox, oy, core) for ox, oy in others] + [(ix, iy, 1 - core)]:
            pl.semaphore_signal(barrier, inc=1, device_id=peer, device_id_type=MESH)
        pl.semaphore_wait(barrier, 4)
        half = packed.shape[0] // 2
        mine = pl.ds(core * half, half)
        theirs = pl.ds((1 - core) * half, half)

        def copy(k, src_ref, owner, rows, to):
            return pltpu.make_async_remote_copy(
                src_ref=src_ref, dst_ref=out.at[owner, rows], send_sem=send_sems[k], recv_sem=recv_sems[k],
                device_id=to, device_id_type=MESH)

        chip = 2 * ix + iy
        first = [copy(j, src.at[mine], chip, mine, (*others[j], core)) for j in range(3)]
        for cp in first:
            cp.start()
        passed = []
        for j, (ox, oy) in enumerate(others):
            owner = 2 * ox + oy
            copy(j, src.at[mine], owner, mine, (ix, iy, core)).wait_recv()
            fwd = copy(3 + j, out.at[owner, mine], owner, mine, (ix, iy, 1 - core))
            fwd.start()
            passed.append(fwd)
        for j, (ox, oy) in enumerate(others):
            copy(3 + j, src.at[theirs], 2 * ox + oy, theirs, (ix, iy, core)).wait_recv()
        for cp in first + passed:
            cp.wait_send()

    launch()
    return out[...]


def _rs_sibling(per_chip):
    hbm = pl.BlockSpec(memory_space=pltpu.HBM)

    def body(src, out, send_sem, recv_sem):
        ix, iy, core, _ = _mesh_pos()
        give = pl.ds((1 - core) * HALF_ROWS, HALF_ROWS)
        cp = pltpu.make_async_remote_copy(src_ref=src.at[pl.ds(0, 4), give], dst_ref=out, send_sem=send_sem,
                                          recv_sem=recv_sem, device_id=(ix, iy, 1 - core), device_id_type=MESH)
        cp.start()
        cp.wait()

    return pl.pallas_call(
        body, in_specs=[hbm], out_specs=hbm, out_shape=jax.ShapeDtypeStruct((4, HALF_ROWS, 1024), F32),
        scratch_shapes=[pltpu.SemaphoreType.DMA, pltpu.SemaphoreType.DMA],
        name="rs_sibling", compiler_params=pltpu.CompilerParams(has_side_effects=True))(per_chip)


def _rs_add_pair(per_chip, sib, chip, core):
    nb = HALF_ROWS // SUM_ROWS
    blk = (1, SUM_ROWS, 1024)

    def body(chip_ref, core_ref, a_ref, b_ref, own_ref, send_ref):
        total = a_ref[0] + b_ref[0]
        send_ref[0] = total.astype(BF16)

        @pl.when(pl.program_id(1) == chip_ref[0])
        def _():
            own_ref[...] = total

    return pl.pallas_call(
        body,
        grid_spec=pltpu.PrefetchScalarGridSpec(
            num_scalar_prefetch=2, grid=(nb, 4),
            in_specs=[pl.BlockSpec(blk, lambda i, k, ch, co: (k, co[0] * nb + i, 0)),
                      pl.BlockSpec(blk, lambda i, k, ch, co: (k, i, 0))],
            out_specs=[pl.BlockSpec((SUM_ROWS, 1024), lambda i, k, ch, co: (i, 0)),
                       pl.BlockSpec(blk, lambda i, k, ch, co: (k, i, 0))]),
        out_shape=[jax.ShapeDtypeStruct((HALF_ROWS, 1024), F32), jax.ShapeDtypeStruct((4, HALF_ROWS, 1024), BF16)],
        name="rs_add_pair", compiler_params=_params(("arbitrary", "arbitrary")))(
            chip.reshape(1), core.reshape(1), per_chip, sib)


def _rs_chips(pair, collective_id):
    src_ref = jax.new_ref(pair, memory_space=pltpu.MemorySpace.HBM)
    got_ref = jax.empty_ref(jax.ShapeDtypeStruct((3,) + pair.shape[1:], pair.dtype), memory_space=pltpu.MemorySpace.HBM)
    dma = pltpu.SemaphoreType.DMA

    @pl.kernel(mesh=plsc.ScalarSubcoreMesh(axis_name="seq", num_cores=1), name="rs_chips_%d" % collective_id,
               scratch_types=(dma,) * 6, compiler_params=pltpu.CompilerParams(collective_id=collective_id))
    def launch(s0, s1, s2, r0, r1, r2):
        ix, iy, core, others = _mesh_pos()
        barrier = pltpu.get_barrier_semaphore()
        for ox, oy in others:
            pl.semaphore_signal(barrier, inc=1, device_id=(ox, oy, core), device_id_type=MESH)
        pl.semaphore_wait(barrier, 3)
        copies = []
        for j, ((ox, oy), send_sem, recv_sem) in enumerate(zip(others, (s0, s1, s2), (r0, r1, r2))):
            cp = pltpu.make_async_remote_copy(src_ref=src_ref.at[2 * ox + oy], dst_ref=got_ref.at[j], send_sem=send_sem,
                                              recv_sem=recv_sem, device_id=(ox, oy, core), device_id_type=MESH)
            cp.start()
            copies.append(cp)
        for cp in copies:
            cp.wait()

    launch()
    return got_ref[...]


def _rs_add_chips(own, got, core, layer, buf):
    nb = HALF_ROWS // SUM_ROWS
    blk = (1, SUM_ROWS, 1024)

    def body(core_ref, a_ref, g0_ref, g1_ref, g2_ref, *rest):
        o_ref = rest[-1]
        o_ref[0, 0] = ((a_ref[...] + g0_ref[0].astype(F32)) + g1_ref[0].astype(F32)) + g2_ref[0].astype(F32)

    aliased = [] if buf is None else [buf]
    return pl.pallas_call(
        body,
        grid_spec=pltpu.PrefetchScalarGridSpec(
            num_scalar_prefetch=1, grid=(nb,),
            in_specs=[pl.BlockSpec((SUM_ROWS, 1024), lambda i, c: (i, 0))]
            + [pl.BlockSpec(blk, functools.partial(lambda j, i, c: (j, i, 0), j)) for j in range(3)]
            + [pl.BlockSpec(memory_space=pl.ANY)] * len(aliased),
            out_specs=pl.BlockSpec((1, 1, SUM_ROWS, 1024), lambda i, c: (layer, c[0], i, 0))),
        out_shape=jax.ShapeDtypeStruct((DEPTH, 2, HALF_ROWS, 1024), F32),
        input_output_aliases={} if buf is None else {5: 0},
        name="rs_add_chips", compiler_params=_params(("arbitrary",)))(core.reshape(1), own, got, got, got, *aliased)


def _rs_share(buf):
    hbm = pl.BlockSpec(memory_space=pltpu.HBM)

    def body(src, out, send_sem, recv_sem):
        ix, iy, core, _ = _mesh_pos()
        layers = pl.ds(0, DEPTH)
        cp = pltpu.make_async_remote_copy(src_ref=src.at[layers, core], dst_ref=out.at[layers, core], send_sem=send_sem,
                                          recv_sem=recv_sem, device_id=(ix, iy, 1 - core), device_id_type=MESH)
        cp.start()
        pltpu.make_async_remote_copy(src_ref=src.at[layers, 1 - core], dst_ref=out.at[layers, 1 - core],
                                     send_sem=send_sem, recv_sem=recv_sem, device_id=(ix, iy, core),
                                     device_id_type=MESH).wait_recv()
        cp.wait_send()

    return pl.pallas_call(
        body, in_specs=[hbm], out_specs=hbm, out_shape=jax.ShapeDtypeStruct(buf.shape, buf.dtype),
        input_output_aliases={0: 0}, scratch_shapes=[pltpu.SemaphoreType.DMA, pltpu.SemaphoreType.DMA],
        name="rs_share", compiler_params=pltpu.CompilerParams(has_side_effects=True))(buf)


def _scatter_start(per_chip, chip, core, layer):
    own, pair = _rs_add_pair(per_chip, _rs_sibling(per_chip), chip, core)
    return own, _rs_chips(pair, SCATTER_IDS[layer])


def _allreduce_small(tag, buf, collective_id):
    rows = buf.shape[0]
    src = jax.new_ref(buf, memory_space=pltpu.MemorySpace.HBM)
    out = jax.empty_ref(jax.ShapeDtypeStruct((8,) + buf.shape, buf.dtype), memory_space=pltpu.MemorySpace.HBM)
    dma = pltpu.SemaphoreType.DMA
    flips = [(a, b, c) for a in (0, 1) for b in (0, 1) for c in (0, 1)][1:]

    @pl.kernel(mesh=plsc.ScalarSubcoreMesh(axis_name="seq", num_cores=1), name="%s_gather_%d" % (tag, collective_id),
               scratch_types=(dma,) * 14, compiler_params=pltpu.CompilerParams(collective_id=collective_id))
    def launch(*sems):
        ix, iy, core, _ = _mesh_pos()
        peers = [(1 - ix if a else ix, 1 - iy if b else iy, 1 - core if c else core) for a, b, c in flips]
        barrier = pltpu.get_barrier_semaphore()
        for peer in peers:
            pl.semaphore_signal(barrier, inc=1, device_id=peer, device_id_type=MESH)
        pl.semaphore_wait(barrier, 7)
        copies = []
        for j, peer in enumerate(peers):
            cp = pltpu.make_async_remote_copy(src_ref=src, dst_ref=out.at[4 * ix + 2 * iy + core], send_sem=sems[j],
                                              recv_sem=sems[7 + j], device_id=peer, device_id_type=MESH)
            cp.start()
            copies.append(cp)
        for j, (px, py, pc) in enumerate(peers):
            pltpu.make_async_remote_copy(src_ref=src, dst_ref=out.at[4 * px + 2 * py + pc], send_sem=sems[j],
                                         recv_sem=sems[7 + j], device_id=peers[j], device_id_type=MESH).wait_recv()
        for cp in copies:
            cp.wait_send()

    launch()
    ix, iy, core = lax.axis_index("x"), lax.axis_index("y"), lax.axis_index("c")
    eight = lax.dynamic_update_slice(out[...], buf[None], (4 * ix + 2 * iy + core, 0, 0))

    def body(x_ref, o_ref):
        acc = x_ref[0]
        for d in range(1, 8):
            acc = acc + x_ref[d]
        o_ref[...] = acc

    t = 512
    return pl.pallas_call(
        body, grid=(rows // t,), in_specs=[pl.BlockSpec((8, t, 128), lambda i: (0, i, 0))],
        out_specs=_rowspec(t, 128), out_shape=jax.ShapeDtypeStruct(buf.shape, buf.dtype),
        name=tag + "_sum", compiler_params=_params(("arbitrary",)))(eight)


IN_COL_SEGMENTS = ((0, 1024, 2048), (1024, 3072, -1024), (3072, 3080, GATE0 - 3072), (3080, D_IN, COL_CX - 3080))


def _orig_cols(w_pad, lo, hi):
    parts = []
    for first, last, shift in IN_COL_SEGMENTS:
        a, b = max(lo, first), min(hi, last)
        if a < b:
            parts.append(w_pad[..., a + shift:b + shift])
    return parts


def _block_diag(w):
    n, d = w.shape[0], w.shape[1]
    on_diag = jnp.arange(n)[:, None] == jnp.arange(n)[None, :]
    return jnp.where(on_diag[:, None, :, None], w[:, :, None, :], 0.0).reshape(n * d, n * d)


def _diag_blocks(w):
    return jnp.stack([w[64 * n:64 * (n + 1), 64 * n:64 * (n + 1)] for n in range(4)], axis=0)


def _gate_row(v):
    return jnp.concatenate([jnp.zeros((4,), F32), v, jnp.zeros((120,), F32)]).reshape(1, 128)


SMALL = ("norm1_g", "hgrn_lb_logits", "hgrn_norm_g", "gdn_conv_w", "gdn_a_log", "gdn_dt_bias", "gdn_norm_g",
         "lru_conv_w", "lru_conv_b", "lru_w_a", "lru_b_a", "lru_w_x", "lru_b_x", "lru_lambda", "norm2_g",
         "final_norm_g")


def _rows_of(shape):
    size = 1
    for d in shape:
        size *= d
    return size, -(-size // 1024) * 8


def _flatten_small(tree):
    parts = []
    for k in sorted(tree):
        size, nr = _rows_of(tree[k].shape)
        parts.append(jnp.pad(tree[k].reshape(-1), (0, nr * 128 - size)).reshape(nr, 128))
    used = sum(pt.shape[0] for pt in parts)
    parts.append(jnp.zeros((-(-used // 512) * 512 - used, 128), F32))
    return jnp.concatenate(parts, axis=0)


def _unflatten_small(buf, shapes):
    out, r0 = {}, 0
    for k in sorted(shapes):
        size, nr = _rows_of(shapes[k])
        out[k] = buf[r0:r0 + nr].reshape(-1)[:size].reshape(shapes[k])
        r0 += nr
    return out


def _layer_fwd(l, x, p, w_in, late):
    sv = {"x0": x}
    (h,) = _rowwise_fwd("norm1", _rmsnorm_fn, [p["norm1_g"]], [x], [D_MODEL], [BF16], 512)
    proj, w_out, w_up, w_down = late(_matmul("proj_in", h, w_in, "nn"))
    sv["h"], sv["proj"] = h, proj
    aq, af, ag = _cols(proj, COL_AQ, A_W), _cols(proj, COL_AF, A_W), _cols(proj, COL_AG, A_W)
    ai = lax.slice_in_dim(proj, COL_AI, COL_AI + A_W, axis=1)
    bz, gate, cy = _cols(proj, COL_BZ, B_W), _cols(proj, GATE0, 128), _cols(proj, COL_CY, C_W)
    q, k, lf = _rowwise_fwd("hgrn_pre", functools.partial(_hgrn_pre_fn, l), [p["lb_logits"]], [aq, af],
                            [A_W] * 3, [F32] * 3, ROWS_NARROW)
    o_a, sts_a = _hgrn_fwd("hgrn_chunk", q, k, ai, lf)
    (mixed,) = _rowwise_fwd("hgrn_post", _hgrn_post_fn, [p["hgrn_norm_g"]], [o_a, ag], [A_W], [BF16], ROWS_NARROW,
                            into=(B_W, D_MODEL, None))
    sv.update(aq=aq, af=af, ai=ai, ag=ag, hq=q, hk=k, hlf=lf, o_a=o_a, sts_a=sts_a)
    qkvc = _conv_fwd("gdn_conv", proj, 0, 3 * B_W, p["gdn_conv_w"], jnp.zeros((1, 3 * B_W), F32), 256)
    gq, gk, gv, gb, gla = _rowwise_fwd("gdn_pre", _gdn_pre_fn, [p["gdn_a_log"], p["gdn_dt_bias"]], [qkvc, gate],
                                       [B_W] * 5, [F32] * 5, 256)
    *wy, sv["gdn_inv"] = _gdn_prep_fwd("gdn_prep", gq, gk, gv, gb, gla)
    o_b, sts_b = _gdn_rec_fwd("gdn_rec", *wy)
    sv["wy"] = wy
    (mixed,) = _rowwise_fwd("gdn_post", _gdn_post_fn, [p["gdn_norm_g"]], [o_b, bz], [B_W], [BF16], ROWS_NARROW,
                            into=(0, D_MODEL, mixed))
    sv.update(qkvc=qkvc, gate=gate, bz=bz, gq=gq, gk=gk, gv=gv, gb=gb, gla=gla, o_b=o_b, sts_b=sts_b)
    xc = _conv_fwd("lru_conv", proj, COL_CX, C_W, p["lru_conv_w"], p["lru_conv_b"], ROWS_NARROW)
    a, b = _rowwise_fwd("lru_pre", _lru_pre_fn, [p["lru_w_a"], p["lru_b_a"], p["lru_w_x"], p["lru_b_x"], p["lru_lambda"]],
                        [xc], [C_W] * 2, [F32] * 2, ROWS_NARROW)
    hs, h_prev = _lru_scan_fwd("lru_scan", a, b)
    (mixed,) = _rowwise_fwd("lru_post", _lru_post_fn, [], [hs, cy], [C_W], [BF16], ROWS_NARROW,
                            into=(B_W + A_W, D_MODEL, mixed))
    sv.update(xc=xc, la=a, hs=hs, h_prev=h_prev, cy=cy)
    x1 = _matmul("proj_out", mixed, w_out, "nn", res=x)
    (h2,) = _rowwise_fwd("norm2", _rmsnorm_fn, [p["norm2_g"]], [x1], [D_MODEL], [BF16], 512)
    up, act = _matmul("mlp_up", h2, w_up, "nn", also=_relu2)
    x2 = _matmul("mlp_down", act, w_down, "nn", res=x1)
    sv.update(mixed=mixed, x1=x1, h2=h2, up=up, act=act)
    return x2, sv, (w_in, w_out, w_up, w_down)


def _layer_bwd(l, dx, dx16, p, sv, w_in, w_out, w_up, w_down):
    gs = {}
    dup = _matmul("mlp_down_dx", dx16, w_down, "nt", epi=_drelu2, epi_in=sv["up"], out_dtype=BF16)
    d_w_down = _matmul("mlp_down_dw", sv["act"], dx16, "tn")
    dh2 = _matmul("mlp_up_dx", dup, w_up, "nt")
    d_w_up = _matmul("mlp_up_dw", sv["h2"], dup, "tn")
    (gs["norm2_g"],), (dx1, dx1_16) = _rowwise_bwd("norm2_bwd", _rmsnorm_fn, [p["norm2_g"]], [sv["x1"]], [dh2], 512,
                                                   res=dx, copy16=True)
    dmixed = _matmul("proj_out_dx", dx1_16, w_out, "nt")
    d_w_out = _matmul("proj_out_dw", sv["mixed"], dx1_16, "tn")
    dy_b, dy_a, dy_c = _cols(dmixed, 0, B_W), _cols(dmixed, B_W, A_W), _cols(dmixed, B_W + A_W, C_W)
    (gs["hgrn_norm_g"],), (do_a, dproj) = _rowwise_bwd("hgrn_post_bwd", _hgrn_post_fn, [p["hgrn_norm_g"]],
                                                       [sv["o_a"], sv["ag"]], [dy_a], ROWS_NARROW, pack=([1], COL_AG, A_W, None))
    dq, dk, dproj, dlf = _hgrn_bwd("hgrn_chunk_bwd", sv["hq"], sv["hk"], sv["ai"], sv["hlf"], sv["sts_a"], do_a,
                                   dproj, COL_AI)
    (gs["lb_logits"],), (dproj,) = _rowwise_bwd("hgrn_pre_bwd", functools.partial(_hgrn_pre_fn, l), [p["lb_logits"]],
                                                [sv["aq"], sv["af"]], [dq, dk, dlf], ROWS_NARROW,
                                                pack=([0, 1], COL_AQ, 2 * A_W, dproj))
    (gs["gdn_norm_g"],), (do_b, dproj) = _rowwise_bwd("gdn_post_bwd", _gdn_post_fn, [p["gdn_norm_g"]],
                                                      [sv["o_b"], sv["bz"]], [dy_b], ROWS_NARROW, pack=([1], COL_BZ, B_W, dproj))
    d_wy = _gdn_rec_bwd("gdn_rec_bwd", *sv["wy"], sv["sts_b"], do_b)
    dgq, dgk, dgv, dgb, dgla = _gdn_prep_bwd("gdn_prep_bwd", sv["gq"], sv["gk"], sv["gv"], sv["gb"], sv["gla"],
                                             sv["gdn_inv"], d_wy)
    (gs["gdn_a_log"], gs["gdn_dt_bias"]), (dqkvc, dproj) = _rowwise_bwd(
        "gdn_pre_bwd", _gdn_pre_fn, [p["gdn_a_log"], p["gdn_dt_bias"]], [sv["qkvc"], sv["gate"]],
        [dgq, dgk, dgv, dgb, dgla], 256, pack=([1], GATE0, D_IN_PAD - GATE0, dproj))
    dproj, gs["gdn_conv_w"], _ = _conv_bwd("gdn_conv_bwd", dqkvc, sv["proj"], 0, 3 * B_W, p["gdn_conv_w"], dproj, 256)
    _, (dhs, dproj) = _rowwise_bwd("lru_post_bwd", _lru_post_fn, [], [sv["hs"], sv["cy"]], [dy_c], ROWS_NARROW,
                                   pack=([1], COL_CY, C_W, dproj))
    g = _lru_scan_bwd("lru_scan_bwd", sv["la"], dhs)
    lru_params = [p["lru_w_a"], p["lru_b_a"], p["lru_w_x"], p["lru_b_x"], p["lru_lambda"]]
    dps, (dxc,) = _rowwise_bwd("lru_pre_bwd", _lru_pre_fn, lru_params, [sv["xc"]], [g, sv["h_prev"]], ROWS_NARROW,
                               dout_fn=lambda gt, h_prev: (gt * h_prev, gt))
    gs["lru_w_a"], gs["lru_b_a"], gs["lru_w_x"], gs["lru_b_x"], gs["lru_lambda"] = dps
    dproj, gs["lru_conv_w"], gs["lru_conv_b"] = _conv_bwd("lru_conv_bwd", dxc, sv["proj"], COL_CX, C_W, p["lru_conv_w"],
                                                          dproj, ROWS_NARROW)
    dh = _matmul("proj_in_dx", dproj, w_in, "nt")
    d_w_in = _matmul("proj_in_dw", sv["h"], dproj, "tn")
    (gs["norm1_g"],), (dx0, dx0_16) = _rowwise_bwd("norm1_bwd", _rmsnorm_fn, [p["norm1_g"]], [sv["x0"]], [dh], 512,
                                                   res=dx1, copy16=True)
    return dx0, dx0_16, dict(w_in=d_w_in, w_out=d_w_out, w_up=d_w_up, w_down=d_w_down), gs


def _layer_params(l, lb_logits, hgrn_norm_g, gdn_conv_w, gdn_a_log, gdn_dt_bias, gdn_norm_g, lru_conv_w, lru_conv_b,
                  lru_w_a, lru_b_a, lru_w_x, lru_b_x, lru_lambda, norm1_g, norm2_g):
    row = lambda v: v.reshape(1, -1)
    return dict(
        norm1_g=row(norm1_g[l]), norm2_g=row(norm2_g[l]), lb_logits=lb_logits,
        hgrn_norm_g=row(jnp.tile(hgrn_norm_g[l], A_W // A_HD)),
        gdn_conv_w=gdn_conv_w[l], gdn_a_log=_gate_row(gdn_a_log[l]), gdn_dt_bias=_gate_row(gdn_dt_bias[l]),
        gdn_norm_g=row(jnp.tile(gdn_norm_g[l], B_NH)),
        lru_conv_w=lru_conv_w[l], lru_conv_b=row(lru_conv_b[l]), lru_w_a=_block_diag(lru_w_a[l]), lru_b_a=row(lru_b_a[l]),
        lru_w_x=_block_diag(lru_w_x[l]), lru_b_x=row(lru_b_x[l]), lru_lambda=row(lru_lambda[l]))


def _small_grads_to_reference(gs_layers, d_final_g, d_logits):
    st = lambda k, f=lambda z: z: jnp.stack([f(g[k]) for g in gs_layers], axis=0)
    vec = lambda z: z.reshape(-1)
    return dict(
        norm1_g=st("norm1_g", vec), norm2_g=st("norm2_g", vec), hgrn_lb_logits=d_logits,
        hgrn_norm_g=st("hgrn_norm_g", lambda z: z.reshape(A_W // A_HD, A_HD).sum(0)),
        gdn_conv_w=st("gdn_conv_w"), gdn_a_log=st("gdn_a_log", lambda z: z[0, 4:8]),
        gdn_dt_bias=st("gdn_dt_bias", lambda z: z[0, 4:8]),
        gdn_norm_g=st("gdn_norm_g", lambda z: z.reshape(B_NH, B_HD).sum(0)),
        lru_conv_w=st("lru_conv_w"), lru_conv_b=st("lru_conv_b", vec), lru_w_a=st("lru_w_a", _diag_blocks),
        lru_b_a=st("lru_b_a", vec), lru_w_x=st("lru_w_x", _diag_blocks), lru_b_x=st("lru_b_x", vec),
        lru_lambda=st("lru_lambda", vec), final_norm_g=d_final_g.reshape(-1))


def _local_step(x, tgt, small, layer_weights, on_layer_grads=None):
    layer_p = [_layer_params(l, small["hgrn_lb_logits"], small["hgrn_norm_g"], small["gdn_conv_w"], small["gdn_a_log"],
                             small["gdn_dt_bias"], small["gdn_norm_g"], small["lru_conv_w"], small["lru_conv_b"],
                             small["lru_w_a"], small["lru_b_a"], small["lru_w_x"], small["lru_b_x"], small["lru_lambda"],
                             small["norm1_g"], small["norm2_g"]) for l in range(DEPTH)]
    saved, weights = [], []
    for l in range(DEPTH):
        x, w_in, late = layer_weights(l, x)
        x, sv, w = _layer_fwd(l, x, layer_p[l], w_in, late)
        saved.append(sv)
        weights.append(w)
    loss, dx, d_final_g, dx16 = _loss_and_grad("final_loss", small["final_norm_g"].reshape(1, -1), x, tgt)
    big = [None] * DEPTH
    gs_layers = [None] * DEPTH
    d_logits = jnp.zeros_like(small["hgrn_lb_logits"])
    for l in reversed(range(DEPTH)):
        dx, dx16, big[l], gs_layers[l] = _layer_bwd(l, dx, dx16, layer_p[l], saved[l], *weights[l])
        d_logits = d_logits + gs_layers[l]["lb_logits"]
        if on_layer_grads is not None:
            big[l] = on_layer_grads(l, big[l])
    return loss, dx, big, _small_grads_to_reference(gs_layers, d_final_g, d_logits)


def kernel(x, norm1_g, w_in, hgrn_lb_logits, hgrn_norm_g, gdn_conv_w, gdn_a_log, gdn_dt_bias, gdn_norm_g, lru_conv_w, lru_conv_b, lru_w_a, lru_b_a, lru_w_x, lru_b_x, lru_lambda, w_out, norm2_g, w_up, w_down, final_norm_g, loss_target, m_norm1_g, m_w_in, m_hgrn_lb_logits, m_hgrn_norm_g, m_gdn_conv_w, m_gdn_a_log, m_gdn_dt_bias, m_gdn_norm_g, m_lru_conv_w, m_lru_conv_b, m_lru_w_a, m_lru_b_a, m_lru_w_x, m_lru_b_x, m_lru_lambda, m_w_out, m_norm2_g, m_w_up, m_w_down, m_final_norm_g, v_norm1_g, v_w_in, v_hgrn_lb_logits, v_hgrn_norm_g, v_gdn_conv_w, v_gdn_a_log, v_gdn_dt_bias, v_gdn_norm_g, v_lru_conv_w, v_lru_conv_b, v_lru_w_a, v_lru_b_a, v_lru_w_x, v_lru_b_x, v_lru_lambda, v_w_out, v_norm2_g, v_w_up, v_w_down, v_final_norm_g):
    args = dict(locals())
    ix, iy, core = lax.axis_index("x"), lax.axis_index("y"), lax.axis_index("c")
    chip = 2 * ix + iy

    packed = _pack_big(w_in, w_out, w_up, w_down).astype(BF16)
    first_w_in = _gather_layer(packed[0, :1024], FIRST_GATHER_ID)
    gathered = [_gather_layer(packed[l, 1024 * (l == 0):], GATHER_IDS[l]) for l in range(DEPTH)]

    def with_own(got, own):
        return lax.dynamic_update_slice(got, own[None], (chip, 0, 0))

    def full_w_in(shares):
        def cols(lo, hi):
            out = []
            for k in range(4):
                a, b = max(lo, k * n_in), min(hi, (k + 1) * n_in)
                if a < b:
                    out.append(shares[k, :1024, a - k * n_in:b - k * n_in])
            return out

        pieces = [pc for first, last, _ in sorted(IN_COL_SEGMENTS, key=lambda seg: seg[0] + seg[2])
                  for pc in cols(first, last)]
        return jnp.concatenate(pieces + [jnp.zeros((1024, D_IN_PAD - D_IN), shares.dtype)], axis=-1)

    def full_rest(shares):
        return (jnp.concatenate([shares[k, 2048:] for k in W_OUT_CHIP_ORDER], axis=0),
                jnp.concatenate([shares[k, :1024] for k in range(4)], axis=-1),
                jnp.concatenate([shares[k, 1024:2048] for k in range(4)], axis=0))

    def layer_weights(l, x):
        got, own = gathered[l], packed[l]
        if l:
            got, own, x = lax.optimization_barrier((got, own, x))
            shares = with_own(got, own)
            rest = full_rest(shares[:, 1024:])
            return x, full_w_in(shares), lambda v: (v,) + rest

        def late(v):
            got_v, own_v, v = lax.optimization_barrier((got, own, v))
            return (v,) + full_rest(with_own(got_v, own_v[1024:]))

        return x, full_w_in(with_own(first_w_in, own[:1024])), late

    n_gc, n_lc = gdn_conv_w.shape[-1], lru_conv_w.shape[-1]
    conv_full = dict(
        gdn_conv_w=lax.dynamic_update_slice(jnp.zeros((DEPTH, 4, 4 * n_gc), F32), gdn_conv_w * 0.5, (0, 0, chip * n_gc)),
        lru_conv_w=lax.dynamic_update_slice(jnp.zeros((DEPTH, 4, 4 * n_lc), F32), lru_conv_w * 0.5, (0, 0, chip * n_lc)))
    conv_shapes = {k: v.shape for k, v in conv_full.items()}
    conv_full = _unflatten_small(_allreduce_small("convw", _flatten_small(conv_full), CONV_SUM_ID), conv_shapes)

    small = {k: args[k] for k in SMALL}
    small.update(conv_full)

    n_in = D_IN // 4

    def start_scatter(l, b):
        pieces = []
        for k in range(4):
            cols = jnp.concatenate(_orig_cols(b["w_in"], k * n_in, (k + 1) * n_in), axis=1)
            r0 = 256 * W_OUT_CHIP_ORDER.index(k)
            pieces += [jnp.pad(cols, ((0, 0), (0, 1024 - n_in))), b["w_up"][:, k * 1024:(k + 1) * 1024],
                       b["w_down"][k * 1024:(k + 1) * 1024], b["w_out"][r0:r0 + 256]]
        per_chip = jnp.concatenate(pieces, axis=0).reshape(4, LAYER_ROWS, 1024)
        return _scatter_start(per_chip, chip, core, l)

    loss, grad_x, started, sg = _local_step(x[0], loss_target[0], small, layer_weights, start_scatter)

    sg["loss"] = loss[0, :1]
    shapes = {k: v.shape for k, v in sg.items()}
    sg = _unflatten_small(_allreduce_small("smallgrad", _flatten_small(sg), GRAD_SUM_ID), shapes)
    loss_out = sg.pop("loss")[0]

    buf = None
    for l in reversed(range(DEPTH)):
        own, got = started[l]
        buf = _rs_add_chips(own, got, core, l, buf)
    g_in, g_out, g_up, g_down = _unpack_big(_rs_share(buf).reshape(DEPTH, LAYER_ROWS, 1024))

    grads = dict(sg)
    grads["gdn_conv_w"] = lax.dynamic_slice_in_dim(sg["gdn_conv_w"], chip * n_gc, n_gc, axis=2)
    grads["lru_conv_w"] = lax.dynamic_slice_in_dim(sg["lru_conv_w"], chip * n_lc, n_lc, axis=2)
    grads.update(w_in=g_in, w_out=g_out, w_up=g_up, w_down=g_down)

    names = ['norm1_g', 'w_in', 'hgrn_lb_logits', 'hgrn_norm_g', 'gdn_conv_w', 'gdn_a_log', 'gdn_dt_bias', 'gdn_norm_g',
             'lru_conv_w', 'lru_conv_b', 'lru_w_a', 'lru_b_a', 'lru_w_x', 'lru_b_x', 'lru_lambda', 'w_out', 'norm2_g',
             'w_up', 'w_down', 'final_norm_g']
    big_names = ("w_in", "w_out", "w_up", "w_down")
    delta, new_m, new_v = {}, {}, {}
    for k in big_names:
        delta[k], new_m[k], new_v[k] = _adam("adam_" + k, args[k], grads[k], args["m_" + k], args["v_" + k])
    small_names = [k for k in names if k not in big_names]
    shapes = {k: args[k].shape for k in small_names}
    flat = [_flatten_small({k: src(k) for k in small_names})
            for src in (lambda k: args[k], lambda k: grads[k], lambda k: args["m_" + k], lambda k: args["v_" + k])]
    outs = _ew("adam_small", _adam_fn, flat, 3)
    for dst, o in zip((delta, new_m, new_v), outs):
        dst.update(_unflatten_small(o, shapes))
    return (loss_out, grad_x[None], *[grads[k] for k in names], *[delta[k] for k in names],
            *[new_m[k] for k in names], *[new_v[k] for k in names])
```

```python
import functools

import jax
import jax.numpy as jnp
from jax import lax
from jax.experimental import pallas as pl
from jax.experimental.pallas import tpu as pltpu
from jax.experimental.pallas import tpu_sc as plsc

F32 = jnp.float32
BF16 = jnp.bfloat16
MESH = pl.DeviceIdType.MESH

DEPTH = 4
D_MODEL = 1024
A_W, B_W, C_W = 256, 512, 256
A_HD, B_HD = 64, 128
B_NH = B_W // B_HD
D_IN = 3592
D_IN_PAD = 3840
COL_BZ, COL_AQ, COL_AF, COL_AI, COL_AG, COL_CX, COL_CY = 1536, 2048, 2304, 2560, 2816, 3072, 3328
W_OUT_CHIP_ORDER = (1, 2, 0, 3)
GATE0 = 3584
ROWS_NARROW = 1024
ROWS_WIDE = 512
CHUNK = 64
SUB = 16
EPS = 1e-6
TINY = 1e-30
MASKED_EXPONENT = -1e30
RG_C = 8.0
LR, B1, B2, AEPS, WD, STEP = 0.001, 0.9, 0.999, 1e-08, 0.01, 10
VMEM_LIMIT = 56 * 1024 * 1024
PACK_ROWS = 13 * 1024

NN = ((1,), (0,))
NT = ((1,), (1,))
TN = ((0,), (0,))


DIMS = {"nn": NN, "nt": NT, "tn": TN}


def _split(a, n):
    parts, r = [], a
    for i in range(n):
        p = r.astype(BF16)
        parts.append(p)
        if i + 1 < n:
            r = r - p.astype(F32)
    return parts


def _split_dot(a, b, mode, na, nb):
    (ca,), (cb,) = DIMS[mode]
    dims = ((DIMS[mode]), ((), ())) if a.ndim == 2 else (((ca + 1,), (cb + 1,)), ((0,), (0,)))
    pa, pb = _split(a, na), _split(b, nb)
    acc = None
    for i in range(na):
        for j in range(nb):
            if i + j < max(na, nb):
                t = lax.dot_general(pa[i], pb[j], dims, preferred_element_type=F32)
                acc = t if acc is None else acc + t
    return acc


@functools.partial(jax.custom_vjp, nondiff_argnums=(2, 3, 4, 5))
def _mdot(a, b, mode, na, nb, need):
    return _split_dot(a, b, mode, na, nb)


def _mdot_fwd(a, b, mode, na, nb, need):
    return _split_dot(a, b, mode, na, nb), (a, b)


def _mdot_bwd(mode, na, nb, need, res, ct):
    a, b = res
    n = max(na, nb)
    da, db = jnp.zeros_like(a), jnp.zeros_like(b)
    if mode == "nn":
        if need[0]:
            da = _split_dot(ct, b, "nt", n, nb)
        if need[1]:
            db = _split_dot(a, ct, "tn", na, n)
    elif mode == "nt":
        if need[0]:
            da = _split_dot(ct, b, "nn", n, nb)
        if need[1]:
            db = _split_dot(ct, a, "tn", n, na)
    else:
        if need[0]:
            da = _split_dot(b, ct, "nt", nb, n)
        if need[1]:
            db = _split_dot(a, ct, "nn", na, n)
    return da, db


_mdot.defvjp(_mdot_fwd, _mdot_bwd)
BOTH = (True, True)


def _bdot(a, b, mode="nn"):
    return _mdot(a, b, mode, 1, 1, BOTH)


def _group_sum(x, ones):
    return _mdot(x, ones, "nn", 2, 1, (True, False))


def _running_sum(x):
    shape = x.shape[:-2] + (CHUNK, CHUNK)
    tri = (_iota(shape, len(shape) - 1) <= _iota(shape, len(shape) - 2)).astype(F32)
    return _mdot(tri, x, "nn", 1, 3, (False, True))


def _iota(shape, d):
    return lax.broadcasted_iota(jnp.int32, shape, d)


def _block_ones(width, group):
    return (_iota((width, width), 0) // group == _iota((width, width), 1) // group).astype(F32)


def _params(sem):
    return pltpu.CompilerParams(dimension_semantics=sem, vmem_limit_bytes=VMEM_LIMIT)


def _tile(dim, prefs):
    for p in prefs:
        if dim % p == 0:
            return p
    return dim


def _whole(shape):
    return pl.BlockSpec(shape, lambda i: (0,) * len(shape))


def _rowspec(t, w, col0=0):
    cb = col0 // w
    return pl.BlockSpec((t, w), lambda i: (i, cb))


def _cols(arr, col0, width):
    assert col0 % width == 0
    return (arr, col0, width)


def _row_operands(rows, t):
    arrs, specs = [], []
    for r in rows:
        arr, col0, w = r if isinstance(r, tuple) else (r, 0, r.shape[1])
        arrs.append(arr)
        specs.append(_rowspec(t, w, col0))
    return arrs, specs


def _rowwise_fwd(name, fn, params, rows, out_widths, out_dtypes, t, into=None):
    rows, row_specs = _row_operands(rows, t)
    s = rows[0].shape[0]
    n_p, n_r = len(params), len(rows)
    n_buf = 0 if into is None or into[2] is None else 1

    def body(*refs):
        p = [r[...] for r in refs[:n_p]]
        xs = [r[...].astype(F32) for r in refs[n_p:n_p + n_r]]
        outs = fn(pl.program_id(0) * t, *p, *xs)
        for o_ref, o in zip(refs[n_p + n_r + n_buf:], outs):
            o_ref[...] = o.astype(o_ref.dtype)

    out_specs = [_rowspec(t, w) for w in out_widths]
    out_shape = [jax.ShapeDtypeStruct((s, w), d) for w, d in zip(out_widths, out_dtypes)]
    extra = []
    if into is not None:
        out_specs = [_rowspec(t, out_widths[0], into[0])]
        out_shape = [jax.ShapeDtypeStruct((s, into[1]), out_dtypes[0])]
        extra = [into[2]] * n_buf
    return pl.pallas_call(
        body, grid=(s // t,),
        in_specs=[_whole(p.shape) for p in params] + row_specs + [pl.BlockSpec(memory_space=pl.ANY)] * n_buf,
        out_specs=out_specs, out_shape=out_shape, input_output_aliases={n_p + n_r: 0} if n_buf else {},
        name=name, compiler_params=_params(("arbitrary",)))(*params, *rows, *extra)


def _into_spec(t, width, col0, rev_steps=None):
    cb = col0 // width
    assert col0 % width == 0
    if rev_steps is None:
        return pl.BlockSpec((t, width), lambda i: (i, cb))
    return pl.BlockSpec((t, width), lambda i: (rev_steps - 1 - i, cb))


def _rowwise_bwd(name, fn, params, rows, douts, t, need=None, res=None, dtypes=None, copy16=False, pack=None,
                 dout_fn=None):
    rows, row_specs = _row_operands(rows, t)
    douts, dout_specs = _row_operands(douts, t)
    s = rows[0].shape[0]
    n_p, n_r, n_o = len(params), len(rows), len(douts)
    need = [True] * n_r if need is None else need
    n_res = 0 if res is None else 1
    n_buf = 0 if pack is None or pack[3] is None else 1

    def body(*refs):
        p = [r[...] for r in refs[:n_p]]
        xs = [r[...].astype(F32) for r in refs[n_p:n_p + n_r]]
        dys = [r[...].astype(F32) for r in refs[n_p + n_r:n_p + n_r + n_o]]
        k = n_p + n_r + n_o
        res_ref = refs[k] if n_res else None
        k += n_res + n_buf
        dp_refs = refs[k:k + n_p]
        dx_refs = list(refs[k + n_p:])
        row0 = pl.program_id(0) * t
        _, vjp = jax.vjp(lambda *a: tuple(fn(row0, *a)), *p, *xs)
        g = vjp(tuple(dys) if dout_fn is None else tuple(dout_fn(*dys)))

        @pl.when(pl.program_id(0) == 0)
        def _():
            for r in dp_refs:
                r[...] = jnp.zeros_like(r)

        for r, gp in zip(dp_refs, g[:n_p]):
            r[...] += gp
        gx = [gi for gi, nd in zip(g[n_p:], need) if nd]
        if n_res:
            gx[0] = gx[0] + res_ref[...]
        if copy16:
            gx.append(gx[0])
        if pack is not None:
            pieces = [gx[j].astype(BF16) for j in pack[0]]
            missing = pack[2] - sum(pc.shape[1] for pc in pieces)
            if missing:
                pieces.append(jnp.zeros((t, missing), BF16))
            dx_refs.pop()[...] = pieces[0] if len(pieces) == 1 else jnp.concatenate(pieces, axis=1)
            gx = [gi for j, gi in enumerate(gx) if j not in pack[0]]
        for r, gi in zip(dx_refs, gx):
            r[...] = gi.astype(r.dtype)

    widths = [sp.block_shape[1] for sp, nd in zip(row_specs, need) if nd]
    dtypes = [F32] * len(widths) if dtypes is None else list(dtypes)
    if copy16:
        widths, dtypes = widths + widths[:1], dtypes + [BF16]
    out_specs = [_rowspec(t, w) for w in widths]
    out_shape = [jax.ShapeDtypeStruct((s, w), d) for w, d in zip(widths, dtypes)]
    extra, aliases = [], {}
    if pack is not None:
        keep = [j for j in range(len(widths)) if j not in pack[0]]
        out_specs = [out_specs[j] for j in keep] + [_into_spec(t, pack[2], pack[1])]
        out_shape = [out_shape[j] for j in keep] + [jax.ShapeDtypeStruct((s, D_IN_PAD), BF16)]
        if n_buf:
            extra = [pack[3]]
            aliases = {n_p + n_r + n_o + n_res: n_p + len(out_shape) - 1}
    outs = pl.pallas_call(
        body, grid=(s // t,),
        in_specs=([_whole(p.shape) for p in params] + row_specs + dout_specs
                  + ([_rowspec(t, res.shape[1])] if n_res else []) + [pl.BlockSpec(memory_space=pl.ANY)] * n_buf),
        out_specs=[_whole(p.shape) for p in params] + out_specs,
        out_shape=[jax.ShapeDtypeStruct(p.shape, F32) for p in params] + out_shape, input_output_aliases=aliases,
        name=name, compiler_params=_params(("arbitrary",)))(*params, *rows, *douts, *([res] if n_res else []), *extra)
    return list(outs[:n_p]), list(outs[n_p:])


def _rmsnorm_fn(row0, g, x):
    y = x * lax.rsqrt(jnp.mean(x * x, axis=-1, keepdims=True) + EPS)
    return (y * g,)


def _hgrn_pre_fn(layer, row0, logits, aq, af):
    m = jnp.max(logits, axis=0, keepdims=True)
    e = jnp.exp(logits - m)
    p = e / jnp.sum(e, axis=0, keepdims=True)
    if layer == 0:
        lb = jnp.zeros((1, A_W), F32)
    else:
        acc = p[1:2]
        for j in range(2, layer + 1):
            acc = acc + p[j:j + 1]
        lb = jnp.minimum(jnp.maximum(acc, 0.0), 1.0 - EPS)
    sig = jax.nn.sigmoid(af)
    f = lb + (1.0 - lb) * sig
    log_f = jnp.log(jnp.maximum(f, TINY))
    k = (1.0 - lb) * jax.nn.sigmoid(-af)
    return jax.nn.silu(aq), k, log_f


def _hgrn_post_fn(row0, g, o, ag):
    ms = _group_sum(o * o, _block_ones(A_W, A_HD)) * (1.0 / A_HD)
    return (o * lax.rsqrt(ms + EPS) * g * jax.nn.silu(ag),)


def _gdn_pre_fn(row0, a_log, dt_bias, qkvc, gate):
    act = jax.nn.silu(qkvc)
    q, k, v = act[:, :B_W], act[:, B_W:2 * B_W], act[:, 2 * B_W:]
    bo = _block_ones(B_W, B_HD)
    q = q * lax.rsqrt(_group_sum(q * q, bo) + EPS) * (B_HD ** -0.5)
    k = k * lax.rsqrt(_group_sum(k * k, bo) + EPS)
    beta = jax.nn.sigmoid(gate)
    log_alpha = -jnp.exp(a_log) * jax.nn.softplus(gate + dt_bias)
    t = gate.shape[0]
    spread = lambda z, lane0: jnp.concatenate(
        [jnp.broadcast_to(z[:, lane0 + h:lane0 + h + 1], (t, B_HD)) for h in range(B_NH)], axis=1)
    return q, k, v, spread(beta, 0), spread(log_alpha, B_NH)


def _gdn_post_fn(row0, g, o, bz):
    ms = _group_sum(o * o, _block_ones(B_W, B_HD)) * (1.0 / B_HD)
    return (o * lax.rsqrt(ms + EPS) * g * jax.nn.silu(bz),)


def _lru_pre_fn(row0, w_a, b_a, w_x, b_x, lam, xc):
    r = jax.nn.sigmoid(_mdot(xc, w_a, "nn", 2, 2, BOTH) + b_a)
    i = jax.nn.sigmoid(_mdot(xc, w_x, "nn", 2, 2, BOTH) + b_x)
    log_a = -RG_C * r * jax.nn.softplus(-lam)
    a = jnp.exp(log_a)
    one_minus = -jnp.tanh(log_a) * (a * a + 1.0)
    mult = jnp.sqrt(jnp.maximum(one_minus, EPS))
    pos = row0 + _iota(xc.shape, 0)
    mult = jnp.where(pos == 0, 1.0, mult)
    return a, mult * i * xc


def _lru_post_fn(row0, h, cy):
    return (jax.nn.gelu(cy, approximate=True) * h,)


def _relu2(u):
    r = jnp.maximum(u, 0.0)
    return r * r


def _drelu2(acc, u):
    return acc * (2.0 * jnp.maximum(u, 0.0))


MM_VMEM_BUDGET = 44 * 1024 * 1024


def _mm_tiles(mode, m, n, k, a_bytes, b_bytes, n_extra):
    best = None
    for tk in {k, 4096, 2048, 1024, 512, 256}:
        for tn in {n, 1280, 1024, 768, 512, 256, 128}:
            for tm in {m, 1024, 512, 256, 128}:
                if m % tm or n % tn or k % tk or tm > 1024:
                    continue
                tiles = tm * tk * a_bytes + tk * tn * b_bytes + (1 + n_extra) * tm * tn * 4
                if 2 * tiles + (tm * tn * 4 if tk < k else 0) > MM_VMEM_BUDGET:
                    continue
                a_reads = 1 if tk == k else n // tn
                b_reads = 1 if (tk == k and tn == n) else m // tm
                cost = m * k * a_bytes * a_reads + k * n * b_bytes * b_reads + m * n * 4 * (k // tk - 1)
                if best is None or (cost, -tm) < best[0]:
                    best = ((cost, -tm), (tm, tn, tk))
    if best is None:
        raise ValueError("no matmul tiling fits VMEM")
    return best[1]


def _matmul(name, a, b, mode, a_fn=None, res=None, epi=None, epi_in=None, out_dtype=F32, also=None):
    if mode == "nn":
        (m, k), n = a.shape, b.shape[1]
    elif mode == "nt":
        (m, k), n = a.shape, b.shape[0]
    else:
        (k, m), n = a.shape, b.shape[1]
    n_extra = (res is not None) + (epi_in is not None) + (also is not None)
    tm, tn, tk = _mm_tiles(mode, m, n, k, a.dtype.itemsize, b.dtype.itemsize, n_extra)
    nk = k // tk
    a_spec = {"nn": pl.BlockSpec((tm, tk), lambda i, j, kk: (i, kk)),
              "nt": pl.BlockSpec((tm, tk), lambda i, j, kk: (i, kk)),
              "tn": pl.BlockSpec((tk, tm), lambda i, j, kk: (kk, i))}[mode]
    b_spec = {"nn": pl.BlockSpec((tk, tn), lambda i, j, kk: (kk, j)),
              "nt": pl.BlockSpec((tn, tk), lambda i, j, kk: (j, kk)),
              "tn": pl.BlockSpec((tk, tn), lambda i, j, kk: (kk, j))}[mode]
    o_spec = pl.BlockSpec((tm, tn), lambda i, j, kk: (i, j))
    dims = {"nn": NN, "nt": NT, "tn": TN}[mode]
    extra = [x for x in (res, epi_in) if x is not None]

    def body(*refs):
        a_ref, b_ref = refs[0], refs[1]
        rest = list(refs[2:])
        res_ref = rest.pop(0) if res is not None else None
        epi_ref = rest.pop(0) if epi_in is not None else None
        o_ref = rest.pop(0)
        also_ref = rest.pop(0) if also is not None else None
        acc = rest[0] if nk > 1 else None
        kk = pl.program_id(2)
        at = a_ref[...]
        if a_fn is not None:
            at = a_fn(at.astype(F32))
        part = lax.dot_general(at.astype(BF16), b_ref[...].astype(BF16), (dims, ((), ())),
                               preferred_element_type=F32)

        def finish(out):
            if res_ref is not None:
                out = out + res_ref[...]
            if epi is not None:
                out = epi(out, epi_ref[...])
            o_ref[...] = out.astype(o_ref.dtype)
            if also is not None:
                also_ref[...] = also(out).astype(also_ref.dtype)

        if nk == 1:
            finish(part)
        else:
            @pl.when(kk == 0)
            def _():
                acc[...] = part

            @pl.when(jnp.logical_and(kk > 0, kk < nk - 1))
            def _():
                acc[...] += part

            @pl.when(kk == nk - 1)
            def _():
                finish(acc[...] + part)

    out_shape = [jax.ShapeDtypeStruct((m, n), out_dtype)] + ([jax.ShapeDtypeStruct((m, n), BF16)] if also else [])
    outs = pl.pallas_call(
        body, grid=(m // tm, n // tn, nk),
        in_specs=[a_spec, b_spec] + [o_spec] * len(extra), out_specs=[o_spec] * len(out_shape), out_shape=out_shape,
        scratch_shapes=[pltpu.VMEM((tm, tn), F32)] if nk > 1 else [],
        name=name, compiler_params=_params(("arbitrary", "arbitrary", "arbitrary")))(a, b, *extra)
    return outs if also else outs[0]


HALO = 8


def _conv_fwd(name, x, col0, width, w, bias, t=512):
    s = x.shape[0]
    cb = col0 // width
    hb = t // HALO

    def body(x_ref, halo_ref, w_ref, b_ref, y_ref):
        i = pl.program_id(0)
        halo = jnp.where(i == 0, 0.0, halo_ref[...])
        xp = jnp.concatenate([halo, x_ref[...]], axis=0)
        wv = w_ref[...]
        y = b_ref[...] + wv[0:1] * xp[HALO - 3:HALO - 3 + t]
        for j in range(1, 4):
            y = y + wv[j:j + 1] * xp[HALO - 3 + j:HALO - 3 + j + t]
        y_ref[...] = y

    return pl.pallas_call(
        body, grid=(s // t,),
        in_specs=[pl.BlockSpec((t, width), lambda i: (i, cb)),
                  pl.BlockSpec((HALO, width), lambda i: (jnp.maximum(i * hb - 1, 0), cb)),
                  _whole(w.shape), _whole(bias.shape)],
        out_specs=_rowspec(t, width), out_shape=jax.ShapeDtypeStruct((s, width), F32),
        name=name, compiler_params=_params(("arbitrary",)))(x, x, w, bias)


def _conv_bwd(name, dy, x, col0, width, w, buf, t=512):
    s = x.shape[0]
    cb = col0 // width
    hb = t // HALO
    nblk = s // t

    def body(dy_ref, dyn_ref, x_ref, halo_ref, w_ref, buf_ref, dx_ref, dw_ref, db_ref):
        i = pl.program_id(0)
        dyv = dy_ref[...]
        nxt = jnp.where(i == nblk - 1, 0.0, dyn_ref[...])
        dyp = jnp.concatenate([dyv, nxt], axis=0)
        wv = w_ref[...]
        dx = wv[3:4] * dyv
        for j in range(3):
            dx = dx + wv[j:j + 1] * dyp[3 - j:3 - j + t]
        dx_ref[...] = dx.astype(dx_ref.dtype)
        halo = jnp.where(i == 0, 0.0, halo_ref[...])
        xp = jnp.concatenate([halo, x_ref[...]], axis=0)
        dw = jnp.concatenate(
            [jnp.sum(dyv * xp[HALO - 3 + j:HALO - 3 + j + t], axis=0, keepdims=True) for j in range(4)], axis=0)

        @pl.when(i == 0)
        def _():
            dw_ref[...] = jnp.zeros_like(dw_ref)
            db_ref[...] = jnp.zeros_like(db_ref)

        dw_ref[...] += dw
        db_ref[...] += jnp.sum(dyv, axis=0, keepdims=True)

    return pl.pallas_call(
        body, grid=(nblk,),
        in_specs=[_rowspec(t, width),
                  pl.BlockSpec((HALO, width), lambda i: (jnp.minimum((i + 1) * hb, s // HALO - 1), 0)),
                  pl.BlockSpec((t, width), lambda i: (i, cb)),
                  pl.BlockSpec((HALO, width), lambda i: (jnp.maximum(i * hb - 1, 0), cb)),
                  _whole(w.shape), pl.BlockSpec(memory_space=pl.ANY)],
        out_specs=[_into_spec(t, width, col0), _whole((4, width)), _whole((1, width))],
        out_shape=[jax.ShapeDtypeStruct(buf.shape, buf.dtype), jax.ShapeDtypeStruct((4, width), F32),
                   jax.ShapeDtypeStruct((1, width), F32)],
        input_output_aliases={5: 0},
        name=name, compiler_params=_params(("arbitrary",)))(dy, dy, x, x, w, buf)


def _hgrn_chunk(st, q, k, v, lf):
    cum = _running_sum(lf)
    tot = jnp.sum(lf, axis=0, keepdims=True)
    bm = _iota((A_W, A_W), 0) // A_HD == _iota((A_W, A_W), 1) // A_HD
    bo = bm.astype(F32)
    o_inter = _bdot(q * jnp.exp(cum), st, "nt")
    kd = k * jnp.exp(tot - cum)
    st_new = st * jnp.exp(tot) + jnp.where(bm, _bdot(v, kd, "tn"), 0.0)
    lane_h = _iota((SUB, A_W), 1) // A_HD
    n_h = A_W // A_HD
    t3 = _iota((SUB, SUB, A_W), 0)
    s3 = _iota((SUB, SUB, A_W), 1)
    outs = []
    for i in range(CHUNK // SUB):
        sl = slice(i * SUB, (i + 1) * SUB)
        qi, ki, vi, ci = q[sl], k[sl], v[sl], cum[sl]
        e = jnp.exp(jnp.where(s3 <= t3, ci[:, None, :] - ci[None, :, :], MASKED_EXPONENT))
        d3 = qi[:, None, :] * ki[None, :, :] * e
        ds = _mdot(d3.reshape(SUB * SUB, A_W), bo, "nn", 1, 1, (True, False)).reshape(SUB, SUB, A_W)
        oi = jnp.sum(ds * vi[None, :, :], axis=1)
        if i > 0:
            n = i * SUB
            cb = cum[n - 1:n]
            qt = qi * jnp.exp(ci - cb)
            kt = k[:n] * jnp.exp(cb - cum[:n])
            qs = jnp.concatenate([jnp.where(lane_h == h, qt, 0.0) for h in range(n_h)], axis=0)
            p = _bdot(_bdot(qs, kt, "nt"), v[:n])
            for h in range(n_h):
                oi = oi + jnp.where(lane_h == h, p[h * SUB:(h + 1) * SUB], 0.0)
        outs.append(oi)
    return st_new, o_inter + jnp.concatenate(outs, axis=0)


HGRN_CHUNKS = 2


def _hgrn_fwd(name, q, k, v, lf):
    s = q.shape[0]
    n = s // CHUNK
    g = HGRN_CHUNKS

    def body(q_ref, k_ref, v_ref, lf_ref, o_ref, sts_ref, st_ref):
        @pl.when(pl.program_id(0) == 0)
        def _():
            st_ref[...] = jnp.zeros_like(st_ref)

        st = st_ref[...]
        for c in range(g):
            rows = pl.ds(c * CHUNK, CHUNK)
            sts_ref[c] = st
            st, o = _hgrn_chunk(st, q_ref[rows], k_ref[rows], v_ref[rows], lf_ref[rows])
            o_ref[rows] = o
        st_ref[...] = st

    cs = _rowspec(g * CHUNK, A_W)
    return pl.pallas_call(
        body, grid=(n // g,), in_specs=[cs] * 4,
        out_specs=[cs, pl.BlockSpec((g, A_W, A_W), lambda i: (i, 0, 0))],
        out_shape=[jax.ShapeDtypeStruct((s, A_W), F32), jax.ShapeDtypeStruct((n, A_W, A_W), F32)],
        scratch_shapes=[pltpu.VMEM((A_W, A_W), F32)],
        name=name, compiler_params=_params(("arbitrary",)))(q, k, v, lf)


def _hgrn_bwd(name, q, k, v, lf, sts, do, buf, col0):
    s = q.shape[0]
    n = s // CHUNK
    g = HGRN_CHUNKS
    nsteps = n // g

    def body(q_ref, k_ref, v_ref, lf_ref, sts_ref, do_ref, buf_ref, dq_ref, dk_ref, dv_ref, dlf_ref, dst_ref):
        @pl.when(pl.program_id(0) == 0)
        def _():
            dst_ref[...] = jnp.zeros_like(dst_ref)

        dst = dst_ref[...]
        for c in reversed(range(g)):
            rows = pl.ds(c * CHUNK, CHUNK)
            _, vjp = jax.vjp(_hgrn_chunk, sts_ref[c], q_ref[rows], k_ref[rows], v_ref[rows], lf_ref[rows])
            ct = vjp((dst, do_ref[rows]))
            dst = ct[0]
            for r, gi in zip((dq_ref, dk_ref, dv_ref, dlf_ref), ct[1:]):
                r[rows] = gi.astype(r.dtype)
        dst_ref[...] = dst

    rs = pl.BlockSpec((g * CHUNK, A_W), lambda i: (nsteps - 1 - i, 0))
    rows = jax.ShapeDtypeStruct((s, A_W), F32)
    return pl.pallas_call(
        body, grid=(nsteps,),
        in_specs=[rs] * 4 + [pl.BlockSpec((g, A_W, A_W), lambda i: (nsteps - 1 - i, 0, 0)), rs,
                             pl.BlockSpec(memory_space=pl.ANY)],
        out_specs=[rs, rs, _into_spec(g * CHUNK, A_W, col0, nsteps), rs],
        out_shape=[rows, rows, jax.ShapeDtypeStruct(buf.shape, buf.dtype), rows],
        input_output_aliases={6: 2}, scratch_shapes=[pltpu.VMEM((A_W, A_W), F32)],
        name=name, compiler_params=_params(("arbitrary",)))(q, k, v, lf, sts, do, buf)


PREP_CHUNKS = 4
REC_CHUNKS = 8
DEC_ROWS = 8


def _to_heads(x, rows=CHUNK, width=B_HD):
    n = x.shape[0] // rows
    x3 = x.reshape(n, rows, x.shape[1])
    return jnp.concatenate([x3[:, :, h * B_HD:h * B_HD + width] for h in range(B_NH)], axis=0)


def _from_heads(y):
    n = y.shape[0] // B_NH
    x3 = jnp.concatenate([y[h * n:(h + 1) * n] for h in range(B_NH)], axis=2)
    return x3.reshape(n * y.shape[1], B_NH * B_HD)


@jax.custom_vjp
def _unit_lower_inverse(a_mat):
    nil = -a_mat
    eye = (_iota(a_mat.shape, 1) == _iota(a_mat.shape, 2)).astype(F32)
    inv = eye + nil
    sq = 2
    while sq < CHUNK:
        nil = _split_dot(nil, nil, "nn", 2, 2)
        inv = inv + _split_dot(inv, nil, "nn", 2, 2)
        sq *= 2
    return inv


def _unit_lower_inverse_fwd(a_mat):
    inv = _unit_lower_inverse(a_mat)
    return inv, inv


def _unit_lower_inverse_bwd(inv, ct):
    return (-_split_dot(_split_dot(inv, ct, "tn", 2, 2), inv, "nt", 2, 2),)


_unit_lower_inverse.defvjp(_unit_lower_inverse_fwd, _unit_lower_inverse_bwd)


@jax.custom_vjp
def _known_inverse(a_mat, inv):
    return inv


def _known_inverse_fwd(a_mat, inv):
    return inv, inv


def _known_inverse_bwd(inv, ct):
    return _unit_lower_inverse_bwd(inv, ct)[0], jnp.zeros_like(inv)


_known_inverse.defvjp(_known_inverse_fwd, _known_inverse_bwd)


def _gdn_prep(q, k, v, beta, la, inv_saved=None):
    q, k, v, beta, la = [_to_heads(z) for z in (q, k, v, beta, la)]
    nb = q.shape[0]
    r = _iota((nb, CHUNK, CHUNK), 1)
    c = _iota((nb, CHUNK, CHUNK), 2)
    causal = c <= r
    g = _running_sum(la)
    tot = jnp.sum(la, axis=1, keepdims=True)
    gcol = g[:, :, :CHUNK]
    decay = jnp.exp(jnp.where(causal, gcol - jnp.swapaxes(gcol, 1, 2), MASKED_EXPONENT))
    kb = k * beta
    a_mat = jnp.where(c < r, _bdot(kb, k, "nt") * decay, 0.0)
    if inv_saved is None:
        inv = _unit_lower_inverse(a_mat)
    else:
        inv = _known_inverse(a_mat, _to_heads(inv_saved, width=CHUNK))
    eg = jnp.exp(g)
    u = _bdot(inv, v * beta)
    w = _bdot(inv, kb * eg)
    qk = jnp.where(causal, _bdot(q, k, "nt") * decay, 0.0)
    widen = lambda z: jnp.concatenate([z, jnp.zeros((nb, CHUNK, B_HD - CHUNK), F32)], axis=2)
    dec = jnp.broadcast_to(jnp.exp(tot), (nb, DEC_ROWS, B_HD))
    return tuple(_from_heads(z) for z in (u, w, widen(qk), q * eg, k * jnp.exp(tot - g), dec, widen(inv)))


def _gdn_rec(st, u, w, qk, qd, kd, dec):
    u, w, qd, kd = [_to_heads(z) for z in (u, w, qd, kd)]
    qk = _to_heads(qk, width=CHUNK)
    v_new = u - _bdot(w, st)
    o = _bdot(qd, st) + _bdot(qk, v_new)
    st_new = st * _to_heads(dec, rows=DEC_ROWS)[:, :1, :1] + _bdot(kd, v_new, "tn")
    return st_new, _from_heads(o)


def _gdn_prep_fwd(name, q, k, v, beta, la):
    s = q.shape[0]
    t = PREP_CHUNKS * CHUNK
    td = PREP_CHUNKS * DEC_ROWS

    def body(*refs):
        outs = _gdn_prep(*[r[...] for r in refs[:5]])
        for o_ref, o in zip(refs[5:], outs):
            o_ref[...] = o

    rows, dec_rows = jax.ShapeDtypeStruct((s, B_W), F32), jax.ShapeDtypeStruct((s // CHUNK * DEC_ROWS, B_W), F32)
    return pl.pallas_call(
        body, grid=(s // t,), in_specs=[_rowspec(t, B_W)] * 5,
        out_specs=[_rowspec(t, B_W)] * 5 + [_rowspec(td, B_W), _rowspec(t, B_W)],
        out_shape=[rows] * 5 + [dec_rows, rows],
        name=name, compiler_params=_params(("arbitrary",)))(q, k, v, beta, la)


def _gdn_prep_bwd(name, q, k, v, beta, la, inv, cts):
    s = q.shape[0]
    t = PREP_CHUNKS * CHUNK
    td = PREP_CHUNKS * DEC_ROWS

    def body(*refs):
        inv_saved = refs[5][...]
        _, vjp = jax.vjp(lambda *a: _gdn_prep(*a, inv_saved=inv_saved)[:6], *[r[...] for r in refs[:5]])
        g = vjp(tuple(r[...] for r in refs[6:12]))
        for o_ref, o in zip(refs[12:], g):
            o_ref[...] = o

    return pl.pallas_call(
        body, grid=(s // t,), in_specs=[_rowspec(t, B_W)] * 11 + [_rowspec(td, B_W)],
        out_specs=[_rowspec(t, B_W)] * 5, out_shape=[jax.ShapeDtypeStruct((s, B_W), F32)] * 5,
        name=name, compiler_params=_params(("arbitrary",)))(q, k, v, beta, la, inv, *cts)


def _gdn_rec_fwd(name, u, w, qk, qd, kd, dec):
    s = u.shape[0]
    n = s // CHUNK
    g = REC_CHUNKS

    def body(u_ref, w_ref, qk_ref, qd_ref, kd_ref, dec_ref, o_ref, sts_ref, st_ref):
        @pl.when(pl.program_id(0) == 0)
        def _():
            st_ref[...] = jnp.zeros_like(st_ref)

        st = st_ref[...]
        for c in range(g):
            rows, drows = pl.ds(c * CHUNK, CHUNK), pl.ds(c * DEC_ROWS, DEC_ROWS)
            sts_ref[c] = st
            st, o = _gdn_rec(st, u_ref[rows], w_ref[rows], qk_ref[rows], qd_ref[rows], kd_ref[rows], dec_ref[drows])
            o_ref[rows] = o
        st_ref[...] = st

    cs = _rowspec(g * CHUNK, B_W)
    return pl.pallas_call(
        body, grid=(n // g,), in_specs=[cs] * 5 + [_rowspec(g * DEC_ROWS, B_W)],
        out_specs=[cs, pl.BlockSpec((g, B_NH, B_HD, B_HD), lambda i: (i, 0, 0, 0))],
        out_shape=[jax.ShapeDtypeStruct((s, B_W), F32), jax.ShapeDtypeStruct((n, B_NH, B_HD, B_HD), F32)],
        scratch_shapes=[pltpu.VMEM((B_NH, B_HD, B_HD), F32)],
        name=name, compiler_params=_params(("arbitrary",)))(u, w, qk, qd, kd, dec)


def _gdn_rec_bwd(name, u, w, qk, qd, kd, dec, sts, do):
    s = u.shape[0]
    n = s // CHUNK
    g = REC_CHUNKS
    nsteps = n // g

    def body(u_ref, w_ref, qk_ref, qd_ref, kd_ref, dec_ref, sts_ref, do_ref,
             du_ref, dw_ref, dqk_ref, dqd_ref, dkd_ref, ddec_ref, dst_ref):
        @pl.when(pl.program_id(0) == 0)
        def _():
            dst_ref[...] = jnp.zeros_like(dst_ref)

        dst = dst_ref[...]
        for c in reversed(range(g)):
            rows, drows = pl.ds(c * CHUNK, CHUNK), pl.ds(c * DEC_ROWS, DEC_ROWS)
            _, vjp = jax.vjp(_gdn_rec, sts_ref[c], u_ref[rows], w_ref[rows], qk_ref[rows], qd_ref[rows], kd_ref[rows],
                             dec_ref[drows])
            ct = vjp((dst, do_ref[rows]))
            dst = ct[0]
            for r, gi in zip((du_ref, dw_ref, dqk_ref, dqd_ref, dkd_ref), ct[1:6]):
                r[rows] = gi
            ddec_ref[drows] = ct[6]
        dst_ref[...] = dst

    rs = pl.BlockSpec((g * CHUNK, B_W), lambda i: (nsteps - 1 - i, 0))
    ds = pl.BlockSpec((g * DEC_ROWS, B_W), lambda i: (nsteps - 1 - i, 0))
    return pl.pallas_call(
        body, grid=(nsteps,),
        in_specs=[rs] * 5 + [ds, pl.BlockSpec((g, B_NH, B_HD, B_HD), lambda i: (nsteps - 1 - i, 0, 0, 0)), rs],
        out_specs=[rs] * 5 + [ds],
        out_shape=[jax.ShapeDtypeStruct((s, B_W), F32)] * 5 + [jax.ShapeDtypeStruct((n * DEC_ROWS, B_W), F32)],
        scratch_shapes=[pltpu.VMEM((B_NH, B_HD, B_HD), F32)],
        name=name, compiler_params=_params(("arbitrary",)))(u, w, qk, qd, kd, dec, sts, do)


SCAN_T = 512


def _block_scan(a, b, reverse):
    t = a.shape[0]
    rows = _iota(a.shape, 0)
    sft = 1
    while sft < t:
        if reverse:
            a_s, b_s = pltpu.roll(a, t - sft, 0), pltpu.roll(b, t - sft, 0)
            edge = rows >= t - sft
        else:
            a_s, b_s = pltpu.roll(a, sft, 0), pltpu.roll(b, sft, 0)
            edge = rows < sft
        a_s = jnp.where(edge, 1.0, a_s)
        b_s = jnp.where(edge, 0.0, b_s)
        b = a * b_s + b
        a = a * a_s
        sft *= 2
    return a, b


def _lru_scan_fwd(name, a, b):
    s, w = a.shape
    t = SCAN_T

    def body(a_ref, b_ref, h_ref, hp_ref, carry):
        @pl.when(pl.program_id(0) == 0)
        def _():
            carry[...] = jnp.zeros_like(carry)

        h_in = carry[0:1]
        ca, cb = _block_scan(a_ref[...], b_ref[...], False)
        h = ca * h_in + cb
        h_ref[...] = h
        hp_ref[...] = jnp.where(_iota(h.shape, 0) == 0, h_in, pltpu.roll(h, 1, 0))
        carry[...] = jnp.broadcast_to(h[t - 1:t], carry.shape)

    return pl.pallas_call(
        body, grid=(s // t,), in_specs=[_rowspec(t, w)] * 2, out_specs=[_rowspec(t, w)] * 2,
        out_shape=[jax.ShapeDtypeStruct((s, w), F32)] * 2, scratch_shapes=[pltpu.VMEM((8, w), F32)],
        name=name, compiler_params=_params(("arbitrary",)))(a, b)


def _lru_scan_bwd(name, a, dh):
    s, w = a.shape
    t = SCAN_T
    n = s // t

    def body(a_ref, dh_ref, g_ref, a_next, g_next):
        @pl.when(pl.program_id(0) == 0)
        def _():
            a_next[...] = jnp.zeros_like(a_next)
            g_next[...] = jnp.zeros_like(g_next)

        av = a_ref[...]
        a_up = jnp.where(_iota(av.shape, 0) == t - 1, a_next[0:1], pltpu.roll(av, t - 1, 0))
        ca, cb = _block_scan(a_up, dh_ref[...], True)
        g = ca * g_next[0:1] + cb
        g_ref[...] = g
        a_next[...] = jnp.broadcast_to(av[0:1], a_next.shape)
        g_next[...] = jnp.broadcast_to(g[0:1], g_next.shape)

    rs = pl.BlockSpec((t, w), lambda i: (n - 1 - i, 0))
    return pl.pallas_call(
        body, grid=(n,), in_specs=[rs] * 2, out_specs=rs, out_shape=jax.ShapeDtypeStruct((s, w), F32),
        scratch_shapes=[pltpu.VMEM((8, w), F32), pltpu.VMEM((8, w), F32)],
        name=name, compiler_params=_params(("arbitrary",)))(a, dh)


def _loss_fn(g, x, tgt):
    y = x * lax.rsqrt(jnp.mean(x * x, axis=-1, keepdims=True) + EPS) * g
    err = y - tgt
    return 0.5 * jnp.sum(jnp.mean(err * err, axis=-1, keepdims=True), axis=0, keepdims=True)


def _loss_and_grad(name, g, x, tgt, t=256):
    s, d = x.shape

    def body(g_ref, x_ref, t_ref, loss_ref, dx_ref, dg_ref, dx16_ref):
        val, vjp = jax.vjp(_loss_fn, g_ref[...], x_ref[...], t_ref[...])
        dg, dx, _ = vjp(jnp.ones((1, 1), F32))

        @pl.when(pl.program_id(0) == 0)
        def _():
            loss_ref[...] = jnp.zeros_like(loss_ref)
            dg_ref[...] = jnp.zeros_like(dg_ref)

        loss_ref[...] += jnp.broadcast_to(val, loss_ref.shape)
        dg_ref[...] += dg
        dx_ref[...] = dx
        dx16_ref[...] = dx.astype(BF16)

    return pl.pallas_call(
        body, grid=(s // t,), in_specs=[_whole(g.shape), _rowspec(t, d), _rowspec(t, d)],
        out_specs=[_whole((1, 128)), _rowspec(t, d), _whole(g.shape), _rowspec(t, d)],
        out_shape=[jax.ShapeDtypeStruct((1, 128), F32), jax.ShapeDtypeStruct((s, d), F32),
                   jax.ShapeDtypeStruct(g.shape, F32), jax.ShapeDtypeStruct((s, d), BF16)],
        name=name, compiler_params=_params(("arbitrary",)))(g, x, tgt)


def _ew(name, fn, ins, n_out, t=None):
    r, c = ins[0].shape
    t = _tile(r, (512, 256, 128, 64, 32, 16, 8)) if t is None else t

    def body(*refs):
        outs = fn(*[x[...] for x in refs[:len(ins)]])
        for o_ref, o in zip(refs[len(ins):], outs):
            o_ref[...] = o

    return pl.pallas_call(
        body, grid=(r // t,), in_specs=[_rowspec(t, c)] * len(ins), out_specs=[_rowspec(t, c)] * n_out,
        out_shape=[jax.ShapeDtypeStruct((r, c), F32)] * n_out,
        name=name, compiler_params=_params(("arbitrary",)))(*ins)


def _adam_fn(w, g, m, v):
    m = B1 * m + (1.0 - B1) * g
    v = B2 * v + (1.0 - B2) * (g * g)
    m_hat = m / (1.0 - B1 ** STEP)
    v_hat = v / (1.0 - B2 ** STEP)
    delta = -LR * (m_hat / (jnp.sqrt(v_hat) + AEPS) + WD * w)
    return delta, m, v


def _adam(name, w, g, m, v):
    shp = w.shape
    two = lambda z: z.reshape(-1, shp[-1])
    outs = _ew(name, _adam_fn, [two(w), two(g), two(m), two(v)], 3)
    return [o.reshape(shp) for o in outs]


LAYER_ROWS = PACK_ROWS // DEPTH
HALF_ROWS = LAYER_ROWS // 2
SUM_ROWS = HALF_ROWS // 2
GATHER_IDS = (1, 2, 3, 4)
SCATTER_IDS = (5, 6, 7, 8)
FIRST_GATHER_ID = 9
CONV_SUM_ID, GRAD_SUM_ID = 10, 11


def _mesh_pos():
    ix, iy, core = lax.axis_index("x"), lax.axis_index("y"), lax.axis_index("c")
    others = [(ix, 1 - iy), (1 - ix, iy), (1 - ix, 1 - iy)]
    return ix, iy, core, others


def _pack_big(w_in, w_out, w_up, w_down):
    pad = jnp.pad(w_in, ((0, 0), (0, 0), (0, 1024 - w_in.shape[-1])))
    return jnp.concatenate([pad, w_up, w_down, w_out], axis=1)


def _unpack_big(p):
    w_in = p[..., 0:1024, :D_IN // 4]
    return w_in, p[..., 3072:LAYER_ROWS, :], p[..., 1024:2048, :], p[..., 2048:3072, :]


def _gather_layer(packed, collective_id):
    src = jax.new_ref(packed, memory_space=pltpu.MemorySpace.HBM)
    out = jax.empty_ref(jax.ShapeDtypeStruct((4,) + packed.shape, packed.dtype), memory_space=pltpu.MemorySpace.HBM)
    dma = pltpu.SemaphoreType.DMA

    @pl.kernel(mesh=plsc.ScalarSubcoreMesh(axis_name="seq", num_cores=1), name="gather_%d" % collective_id,
               scratch_types=(dma,) * 12, compiler_params=pltpu.CompilerParams(collective_id=collective_id))
    def launch(*sems):
        send_sems, recv_sems = sems[:6], sems[6:]
        ix, iy, core, others = _mesh_pos()
        barrier = pltpu.get_barrier_semaphore()
        for peer in [(ox, oy, core) for ox, oy in others] + [(ix, iy, 1 - core)]:
            pl.semaphore_signal(barrier, inc=1, device_id=peer, device_id_type=MESH)
        pl.semaphore_wait(barrier, 4)
        half = packed.shape[0] // 2
        mine = pl.ds(core * half, half)
        theirs = pl.ds((1 - core) * half, half)

        def copy(k, src_ref, owner, rows, to):
            return pltpu.make_async_remote_copy(
                src_ref=src_ref, dst_ref=out.at[owner, rows], send_sem=send_sems[k], recv_sem=recv_sems[k],
                device_id=to, device_id_type=MESH)

        chip = 2 * ix + iy
        first = [copy(j, src.at[mine], chip, mine, (*others[j], core)) for j in range(3)]
        for cp in first:
            cp.start()
        passed = []
        for j, (ox, oy) in enumerate(others):
            owner = 2 * ox + oy
            copy(j, src.at[mine], owner, mine, (ix, iy, core)).wait_recv()
            fwd = copy(3 + j, out.at[owner, mine], owner, mine, (ix, iy, 1 - core))
            fwd.start()
            passed.append(fwd)
        for j, (ox, oy) in enumerate(others):
            copy(3 + j, src.at[theirs], 2 * ox + oy, theirs, (ix, iy, core)).wait_recv()
        for cp in first + passed:
            cp.wait_send()

    launch()
    return out[...]


def _rs_sibling(per_chip):
    hbm = pl.BlockSpec(memory_space=pltpu.HBM)

    def body(src, out, send_sem, recv_sem):
        ix, iy, core, _ = _mesh_pos()
        give = pl.ds((1 - core) * HALF_ROWS, HALF_ROWS)
        cp = pltpu.make_async_remote_copy(src_ref=src.at[pl.ds(0, 4), give], dst_ref=out, send_sem=send_sem,
                                          recv_sem=recv_sem, device_id=(ix, iy, 1 - core), device_id_type=MESH)
        cp.start()
        cp.wait()

    return pl.pallas_call(
        body, in_specs=[hbm], out_specs=hbm, out_shape=jax.ShapeDtypeStruct((4, HALF_ROWS, 1024), F32),
        scratch_shapes=[pltpu.SemaphoreType.DMA, pltpu.SemaphoreType.DMA],
        name="rs_sibling", compiler_params=pltpu.CompilerParams(has_side_effects=True))(per_chip)


def _rs_add_pair(per_chip, sib, chip, core):
    nb = HALF_ROWS // SUM_ROWS
    blk = (1, SUM_ROWS, 1024)

    def body(chip_ref, core_ref, a_ref, b_ref, own_ref, send_ref):
        total = a_ref[0] + b_ref[0]
        send_ref[0] = total.astype(BF16)

        @pl.when(pl.program_id(1) == chip_ref[0])
        def _():
            own_ref[...] = total

    return pl.pallas_call(
        body,
        grid_spec=pltpu.PrefetchScalarGridSpec(
            num_scalar_prefetch=2, grid=(nb, 4),
            in_specs=[pl.BlockSpec(blk, lambda i, k, ch, co: (k, co[0] * nb + i, 0)),
                      pl.BlockSpec(blk, lambda i, k, ch, co: (k, i, 0))],
            out_specs=[pl.BlockSpec((SUM_ROWS, 1024), lambda i, k, ch, co: (i, 0)),
                       pl.BlockSpec(blk, lambda i, k, ch, co: (k, i, 0))]),
        out_shape=[jax.ShapeDtypeStruct((HALF_ROWS, 1024), F32), jax.ShapeDtypeStruct((4, HALF_ROWS, 1024), BF16)],
        name="rs_add_pair", compiler_params=_params(("arbitrary", "arbitrary")))(
            chip.reshape(1), core.reshape(1), per_chip, sib)


def _rs_chips(pair, collective_id):
    src_ref = jax.new_ref(pair, memory_space=pltpu.MemorySpace.HBM)
    got_ref = jax.empty_ref(jax.ShapeDtypeStruct((3,) + pair.shape[1:], pair.dtype), memory_space=pltpu.MemorySpace.HBM)
    dma = pltpu.SemaphoreType.DMA

    @pl.kernel(mesh=plsc.ScalarSubcoreMesh(axis_name="seq", num_cores=1), name="rs_chips_%d" % collective_id,
               scratch_types=(dma,) * 6, compiler_params=pltpu.CompilerParams(collective_id=collective_id))
    def launch(s0, s1, s2, r0, r1, r2):
        ix, iy, core, others = _mesh_pos()
        barrier = pltpu.get_barrier_semaphore()
        for ox, oy in others:
            pl.semaphore_signal(barrier, inc=1, device_id=(ox, oy, core), device_id_type=MESH)
        pl.semaphore_wait(barrier, 3)
        copies = []
        for j, ((ox, oy), send_sem, recv_sem) in enumerate(zip(others, (s0, s1, s2), (r0, r1, r2))):
            cp = pltpu.make_async_remote_copy(src_ref=src_ref.at[2 * ox + oy], dst_ref=got_ref.at[j], send_sem=send_sem,
                                              recv_sem=recv_sem, device_id=(ox, oy, core), device_id_type=MESH)
            cp.start()
            copies.append(cp)
        for cp in copies:
            cp.wait()

    launch()
    return got_ref[...]


def _rs_add_chips(own, got, core, layer, buf):
    nb = HALF_ROWS // SUM_ROWS
    blk = (1, SUM_ROWS, 1024)

    def body(core_ref, a_ref, g0_ref, g1_ref, g2_ref, *rest):
        o_ref = rest[-1]
        o_ref[0, 0] = ((a_ref[...] + g0_ref[0].astype(F32)) + g1_ref[0].astype(F32)) + g2_ref[0].astype(F32)

    aliased = [] if buf is None else [buf]
    return pl.pallas_call(
        body,
        grid_spec=pltpu.PrefetchScalarGridSpec(
            num_scalar_prefetch=1, grid=(nb,),
            in_specs=[pl.BlockSpec((SUM_ROWS, 1024), lambda i, c: (i, 0))]
            + [pl.BlockSpec(blk, functools.partial(lambda j, i, c: (j, i, 0), j)) for j in range(3)]
            + [pl.BlockSpec(memory_space=pl.ANY)] * len(aliased),
            out_specs=pl.BlockSpec((1, 1, SUM_ROWS, 1024), lambda i, c: (layer, c[0], i, 0))),
        out_shape=jax.ShapeDtypeStruct((DEPTH, 2, HALF_ROWS, 1024), F32),
        input_output_aliases={} if buf is None else {5: 0},
        name="rs_add_chips", compiler_params=_params(("arbitrary",)))(core.reshape(1), own, got, got, got, *aliased)


def _rs_share(buf):
    hbm = pl.BlockSpec(memory_space=pltpu.HBM)

    def body(src, out, send_sem, recv_sem):
        ix, iy, core, _ = _mesh_pos()
        layers = pl.ds(0, DEPTH)
        cp = pltpu.make_async_remote_copy(src_ref=src.at[layers, core], dst_ref=out.at[layers, core], send_sem=send_sem,
                                          recv_sem=recv_sem, device_id=(ix, iy, 1 - core), device_id_type=MESH)
        cp.start()
        pltpu.make_async_remote_copy(src_ref=src.at[layers, 1 - core], dst_ref=out.at[layers, 1 - core],
                                     send_sem=send_sem, recv_sem=recv_sem, device_id=(ix, iy, core),
                                     device_id_type=MESH).wait_recv()
        cp.wait_send()

    return pl.pallas_call(
        body, in_specs=[hbm], out_specs=hbm, out_shape=jax.ShapeDtypeStruct(buf.shape, buf.dtype),
        input_output_aliases={0: 0}, scratch_shapes=[pltpu.SemaphoreType.DMA, pltpu.SemaphoreType.DMA],
        name="rs_share", compiler_params=pltpu.CompilerParams(has_side_effects=True))(buf)


def _scatter_start(per_chip, chip, core, layer):
    own, pair = _rs_add_pair(per_chip, _rs_sibling(per_chip), chip, core)
    return own, _rs_chips(pair, SCATTER_IDS[layer])


def _allreduce_small(tag, buf, collective_id):
    rows = buf.shape[0]
    src = jax.new_ref(buf, memory_space=pltpu.MemorySpace.HBM)
    out = jax.empty_ref(jax.ShapeDtypeStruct((8,) + buf.shape, buf.dtype), memory_space=pltpu.MemorySpace.HBM)
    dma = pltpu.SemaphoreType.DMA
    flips = [(a, b, c) for a in (0, 1) for b in (0, 1) for c in (0, 1)][1:]

    @pl.kernel(mesh=plsc.ScalarSubcoreMesh(axis_name="seq", num_cores=1), name="%s_gather_%d" % (tag, collective_id),
               scratch_types=(dma,) * 14, compiler_params=pltpu.CompilerParams(collective_id=collective_id))
    def launch(*sems):
        ix, iy, core, _ = _mesh_pos()
        peers = [(1 - ix if a else ix, 1 - iy if b else iy, 1 - core if c else core) for a, b, c in flips]
        barrier = pltpu.get_barrier_semaphore()
        for peer in peers:
            pl.semaphore_signal(barrier, inc=1, device_id=peer, device_id_type=MESH)
        pl.semaphore_wait(barrier, 7)
        copies = []
        for j, peer in enumerate(peers):
            cp = pltpu.make_async_remote_copy(src_ref=src, dst_ref=out.at[4 * ix + 2 * iy + core], send_sem=sems[j],
                                              recv_sem=sems[7 + j], device_id=peer, device_id_type=MESH)
            cp.start()
            copies.append(cp)
        for j, (px, py, pc) in enumerate(peers):
            pltpu.make_async_remote_copy(src_ref=src, dst_ref=out.at[4 * px + 2 * py + pc], send_sem=sems[j],
                                         recv_sem=sems[7 + j], device_id=peers[j], device_id_type=MESH).wait_recv()
        for cp in copies:
            cp.wait_send()

    launch()
    ix, iy, core = lax.axis_index("x"), lax.axis_index("y"), lax.axis_index("c")
    eight = lax.dynamic_update_slice(out[...], buf[None], (4 * ix + 2 * iy + core, 0, 0))

    def body(x_ref, o_ref):
        acc = x_ref[0]
        for d in range(1, 8):
            acc = acc + x_ref[d]
        o_ref[...] = acc

    t = 512
    return pl.pallas_call(
        body, grid=(rows // t,), in_specs=[pl.BlockSpec((8, t, 128), lambda i: (0, i, 0))],
        out_specs=_rowspec(t, 128), out_shape=jax.ShapeDtypeStruct(buf.shape, buf.dtype),
        name=tag + "_sum", compiler_params=_params(("arbitrary",)))(eight)


IN_COL_SEGMENTS = ((0, 1024, 2048), (1024, 3072, -1024), (3072, 3080, GATE0 - 3072), (3080, D_IN, COL_CX - 3080))


def _orig_cols(w_pad, lo, hi):
    parts = []
    for first, last, shift in IN_COL_SEGMENTS:
        a, b = max(lo, first), min(hi, last)
        if a < b:
            parts.append(w_pad[..., a + shift:b + shift])
    return parts


def _block_diag(w):
    n, d = w.shape[0], w.shape[1]
    on_diag = jnp.arange(n)[:, None] == jnp.arange(n)[None, :]
    return jnp.where(on_diag[:, None, :, None], w[:, :, None, :], 0.0).reshape(n * d, n * d)


def _diag_blocks(w):
    return jnp.stack([w[64 * n:64 * (n + 1), 64 * n:64 * (n + 1)] for n in range(4)], axis=0)


def _gate_row(v):
    return jnp.concatenate([jnp.zeros((4,), F32), v, jnp.zeros((120,), F32)]).reshape(1, 128)


SMALL = ("norm1_g", "hgrn_lb_logits", "hgrn_norm_g", "gdn_conv_w", "gdn_a_log", "gdn_dt_bias", "gdn_norm_g",
         "lru_conv_w", "lru_conv_b", "lru_w_a", "lru_b_a", "lru_w_x", "lru_b_x", "lru_lambda", "norm2_g",
         "final_norm_g")


def _rows_of(shape):
    size = 1
    for d in shape:
        size *= d
    return size, -(-size // 1024) * 8


def _flatten_small(tree):
    parts = []
    for k in sorted(tree):
        size, nr = _rows_of(tree[k].shape)
        parts.append(jnp.pad(tree[k].reshape(-1), (0, nr * 128 - size)).reshape(nr, 128))
    used = sum(pt.shape[0] for pt in parts)
    parts.append(jnp.zeros((-(-used // 512) * 512 - used, 128), F32))
    return jnp.concatenate(parts, axis=0)


def _unflatten_small(buf, shapes):
    out, r0 = {}, 0
    for k in sorted(shapes):
        size, nr = _rows_of(shapes[k])
        out[k] = buf[r0:r0 + nr].reshape(-1)[:size].reshape(shapes[k])
        r0 += nr
    return out


def _layer_fwd(l, x, p, w_in, late):
    sv = {"x0": x}
    (h,) = _rowwise_fwd("norm1", _rmsnorm_fn, [p["norm1_g"]], [x], [D_MODEL], [BF16], 512)
    proj, w_out, w_up, w_down = late(_matmul("proj_in", h, w_in, "nn"))
    sv["h"], sv["proj"] = h, proj
    aq, af, ag = _cols(proj, COL_AQ, A_W), _cols(proj, COL_AF, A_W), _cols(proj, COL_AG, A_W)
    ai = lax.slice_in_dim(proj, COL_AI, COL_AI + A_W, axis=1)
    bz, gate, cy = _cols(proj, COL_BZ, B_W), _cols(proj, GATE0, 128), _cols(proj, COL_CY, C_W)
    q, k, lf = _rowwise_fwd("hgrn_pre", functools.partial(_hgrn_pre_fn, l), [p["lb_logits"]], [aq, af],
                            [A_W] * 3, [F32] * 3, ROWS_NARROW)
    o_a, sts_a = _hgrn_fwd("hgrn_chunk", q, k, ai, lf)
    (mixed,) = _rowwise_fwd("hgrn_post", _hgrn_post_fn, [p["hgrn_norm_g"]], [o_a, ag], [A_W], [BF16], ROWS_NARROW,
                            into=(B_W, D_MODEL, None))
    sv.update(aq=aq, af=af, ai=ai, ag=ag, hq=q, hk=k, hlf=lf, o_a=o_a, sts_a=sts_a)
    qkvc = _conv_fwd("gdn_conv", proj, 0, 3 * B_W, p["gdn_conv_w"], jnp.zeros((1, 3 * B_W), F32), ROWS_WIDE)
    gq, gk, gv, gb, gla = _rowwise_fwd("gdn_pre", _gdn_pre_fn, [p["gdn_a_log"], p["gdn_dt_bias"]], [qkvc, gate],
                                       [B_W] * 5, [F32] * 5, ROWS_WIDE)
    *wy, sv["gdn_inv"] = _gdn_prep_fwd("gdn_prep", gq, gk, gv, gb, gla)
    o_b, sts_b = _gdn_rec_fwd("gdn_rec", *wy)
    sv["wy"] = wy
    (mixed,) = _rowwise_fwd("gdn_post", _gdn_post_fn, [p["gdn_norm_g"]], [o_b, bz], [B_W], [BF16], ROWS_NARROW,
                            into=(0, D_MODEL, mixed))
    sv.update(qkvc=qkvc, gate=gate, bz=bz, gq=gq, gk=gk, gv=gv, gb=gb, gla=gla, o_b=o_b, sts_b=sts_b)
    xc = _conv_fwd("lru_conv", proj, COL_CX, C_W, p["lru_conv_w"], p["lru_conv_b"], ROWS_NARROW)
    a, b = _rowwise_fwd("lru_pre", _lru_pre_fn, [p["lru_w_a"], p["lru_b_a"], p["lru_w_x"], p["lru_b_x"], p["lru_lambda"]],
                        [xc], [C_W] * 2, [F32] * 2, ROWS_NARROW)
    hs, h_prev = _lru_scan_fwd("lru_scan", a, b)
    (mixed,) = _rowwise_fwd("lru_post", _lru_post_fn, [], [hs, cy], [C_W], [BF16], ROWS_NARROW,
                            into=(B_W + A_W, D_MODEL, mixed))
    sv.update(xc=xc, la=a, hs=hs, h_prev=h_prev, cy=cy)
    x1 = _matmul("proj_out", mixed, w_out, "nn", res=x)
    (h2,) = _rowwise_fwd("norm2", _rmsnorm_fn, [p["norm2_g"]], [x1], [D_MODEL], [BF16], 512)
    up, act = _matmul("mlp_up", h2, w_up, "nn", also=_relu2)
    x2 = _matmul("mlp_down", act, w_down, "nn", res=x1)
    sv.update(mixed=mixed, x1=x1, h2=h2, up=up, act=act)
    return x2, sv, (w_in, w_out, w_up, w_down)


def _layer_bwd(l, dx, dx16, p, sv, w_in, w_out, w_up, w_down):
    gs = {}
    dup = _matmul("mlp_down_dx", dx16, w_down, "nt", epi=_drelu2, epi_in=sv["up"], out_dtype=BF16)
    d_w_down = _matmul("mlp_down_dw", sv["act"], dx16, "tn")
    dh2 = _matmul("mlp_up_dx", dup, w_up, "nt")
    d_w_up = _matmul("mlp_up_dw", sv["h2"], dup, "tn")
    (gs["norm2_g"],), (dx1, dx1_16) = _rowwise_bwd("norm2_bwd", _rmsnorm_fn, [p["norm2_g"]], [sv["x1"]], [dh2], 512,
                                                   res=dx, copy16=True)
    dmixed = _matmul("proj_out_dx", dx1_16, w_out, "nt")
    d_w_out = _matmul("proj_out_dw", sv["mixed"], dx1_16, "tn")
    dy_b, dy_a, dy_c = _cols(dmixed, 0, B_W), _cols(dmixed, B_W, A_W), _cols(dmixed, B_W + A_W, C_W)
    (gs["hgrn_norm_g"],), (do_a, dproj) = _rowwise_bwd("hgrn_post_bwd", _hgrn_post_fn, [p["hgrn_norm_g"]],
                                                       [sv["o_a"], sv["ag"]], [dy_a], ROWS_NARROW, pack=([1], COL_AG, A_W, None))
    dq, dk, dproj, dlf = _hgrn_bwd("hgrn_chunk_bwd", sv["hq"], sv["hk"], sv["ai"], sv["hlf"], sv["sts_a"], do_a,
                                   dproj, COL_AI)
    (gs["lb_logits"],), (dproj,) = _rowwise_bwd("hgrn_pre_bwd", functools.partial(_hgrn_pre_fn, l), [p["lb_logits"]],
                                                [sv["aq"], sv["af"]], [dq, dk, dlf], ROWS_NARROW,
                                                pack=([0, 1], COL_AQ, 2 * A_W, dproj))
    (gs["gdn_norm_g"],), (do_b, dproj) = _rowwise_bwd("gdn_post_bwd", _gdn_post_fn, [p["gdn_norm_g"]],
                                                      [sv["o_b"], sv["bz"]], [dy_b], ROWS_NARROW, pack=([1], COL_BZ, B_W, dproj))
    d_wy = _gdn_rec_bwd("gdn_rec_bwd", *sv["wy"], sv["sts_b"], do_b)
    dgq, dgk, dgv, dgb, dgla = _gdn_prep_bwd("gdn_prep_bwd", sv["gq"], sv["gk"], sv["gv"], sv["gb"], sv["gla"],
                                             sv["gdn_inv"], d_wy)
    (gs["gdn_a_log"], gs["gdn_dt_bias"]), (dqkvc, dproj) = _rowwise_bwd(
        "gdn_pre_bwd", _gdn_pre_fn, [p["gdn_a_log"], p["gdn_dt_bias"]], [sv["qkvc"], sv["gate"]],
        [dgq, dgk, dgv, dgb, dgla], ROWS_WIDE, pack=([1], GATE0, D_IN_PAD - GATE0, dproj))
    dproj, gs["gdn_conv_w"], _ = _conv_bwd("gdn_conv_bwd", dqkvc, sv["proj"], 0, 3 * B_W, p["gdn_conv_w"], dproj, ROWS_WIDE)
    _, (dhs, dproj) = _rowwise_bwd("lru_post_bwd", _lru_post_fn, [], [sv["hs"], sv["cy"]], [dy_c], ROWS_NARROW,
                                   pack=([1], COL_CY, C_W, dproj))
    g = _lru_scan_bwd("lru_scan_bwd", sv["la"], dhs)
    lru_params = [p["lru_w_a"], p["lru_b_a"], p["lru_w_x"], p["lru_b_x"], p["lru_lambda"]]
    dps, (dxc,) = _rowwise_bwd("lru_pre_bwd", _lru_pre_fn, lru_params, [sv["xc"]], [g, sv["h_prev"]], ROWS_NARROW,
                               dout_fn=lambda gt, h_prev: (gt * h_prev, gt))
    gs["lru_w_a"], gs["lru_b_a"], gs["lru_w_x"], gs["lru_b_x"], gs["lru_lambda"] = dps
    dproj, gs["lru_conv_w"], gs["lru_conv_b"] = _conv_bwd("lru_conv_bwd", dxc, sv["proj"], COL_CX, C_W, p["lru_conv_w"],
                                                          dproj, ROWS_NARROW)
    dh = _matmul("proj_in_dx", dproj, w_in, "nt")
    d_w_in = _matmul("proj_in_dw", sv["h"], dproj, "tn")
    (gs["norm1_g"],), (dx0, dx0_16) = _rowwise_bwd("norm1_bwd", _rmsnorm_fn, [p["norm1_g"]], [sv["x0"]], [dh], 512,
                                                   res=dx1, copy16=True)
    return dx0, dx0_16, dict(w_in=d_w_in, w_out=d_w_out, w_up=d_w_up, w_down=d_w_down), gs


def _layer_params(l, lb_logits, hgrn_norm_g, gdn_conv_w, gdn_a_log, gdn_dt_bias, gdn_norm_g, lru_conv_w, lru_conv_b,
                  lru_w_a, lru_b_a, lru_w_x, lru_b_x, lru_lambda, norm1_g, norm2_g):
    row = lambda v: v.reshape(1, -1)
    return dict(
        norm1_g=row(norm1_g[l]), norm2_g=row(norm2_g[l]), lb_logits=lb_logits,
        hgrn_norm_g=row(jnp.tile(hgrn_norm_g[l], A_W // A_HD)),
        gdn_conv_w=gdn_conv_w[l], gdn_a_log=_gate_row(gdn_a_log[l]), gdn_dt_bias=_gate_row(gdn_dt_bias[l]),
        gdn_norm_g=row(jnp.tile(gdn_norm_g[l], B_NH)),
        lru_conv_w=lru_conv_w[l], lru_conv_b=row(lru_conv_b[l]), lru_w_a=_block_diag(lru_w_a[l]), lru_b_a=row(lru_b_a[l]),
        lru_w_x=_block_diag(lru_w_x[l]), lru_b_x=row(lru_b_x[l]), lru_lambda=row(lru_lambda[l]))


def _small_grads_to_reference(gs_layers, d_final_g, d_logits):
    st = lambda k, f=lambda z: z: jnp.stack([f(g[k]) for g in gs_layers], axis=0)
    vec = lambda z: z.reshape(-1)
    return dict(
        norm1_g=st("norm1_g", vec), norm2_g=st("norm2_g", vec), hgrn_lb_logits=d_logits,
        hgrn_norm_g=st("hgrn_norm_g", lambda z: z.reshape(A_W // A_HD, A_HD).sum(0)),
        gdn_conv_w=st("gdn_conv_w"), gdn_a_log=st("gdn_a_log", lambda z: z[0, 4:8]),
        gdn_dt_bias=st("gdn_dt_bias", lambda z: z[0, 4:8]),
        gdn_norm_g=st("gdn_norm_g", lambda z: z.reshape(B_NH, B_HD).sum(0)),
        lru_conv_w=st("lru_conv_w"), lru_conv_b=st("lru_conv_b", vec), lru_w_a=st("lru_w_a", _diag_blocks),
        lru_b_a=st("lru_b_a", vec), lru_w_x=st("lru_w_x", _diag_blocks), lru_b_x=st("lru_b_x", vec),
        lru_lambda=st("lru_lambda", vec), final_norm_g=d_final_g.reshape(-1))


def _local_step(x, tgt, small, layer_weights, on_layer_grads=None):
    layer_p = [_layer_params(l, small["hgrn_lb_logits"], small["hgrn_norm_g"], small["gdn_conv_w"], small["gdn_a_log"],
                             small["gdn_dt_bias"], small["gdn_norm_g"], small["lru_conv_w"], small["lru_conv_b"],
                             small["lru_w_a"], small["lru_b_a"], small["lru_w_x"], small["lru_b_x"], small["lru_lambda"],
                             small["norm1_g"], small["norm2_g"]) for l in range(DEPTH)]
    saved, weights = [], []
    for l in range(DEPTH):
        x, w_in, late = layer_weights(l, x)
        x, sv, w = _layer_fwd(l, x, layer_p[l], w_in, late)
        saved.append(sv)
        weights.append(w)
    loss, dx, d_final_g, dx16 = _loss_and_grad("final_loss", small["final_norm_g"].reshape(1, -1), x, tgt)
    big = [None] * DEPTH
    gs_layers = [None] * DEPTH
    d_logits = jnp.zeros_like(small["hgrn_lb_logits"])
    for l in reversed(range(DEPTH)):
        dx, dx16, big[l], gs_layers[l] = _layer_bwd(l, dx, dx16, layer_p[l], saved[l], *weights[l])
        d_logits = d_logits + gs_layers[l]["lb_logits"]
        if on_layer_grads is not None:
            big[l] = on_layer_grads(l, big[l])
    return loss, dx, big, _small_grads_to_reference(gs_layers, d_final_g, d_logits)


def kernel(x, norm1_g, w_in, hgrn_lb_logits, hgrn_norm_g, gdn_conv_w, gdn_a_log, gdn_dt_bias, gdn_norm_g, lru_conv_w, lru_conv_b, lru_w_a, lru_b_a, lru_w_x, lru_b_x, lru_lambda, w_out, norm2_g, w_up, w_down, final_norm_g, loss_target, m_norm1_g, m_w_in, m_hgrn_lb_logits, m_hgrn_norm_g, m_gdn_conv_w, m_gdn_a_log, m_gdn_dt_bias, m_gdn_norm_g, m_lru_conv_w, m_lru_conv_b, m_lru_w_a, m_lru_b_a, m_lru_w_x, m_lru_b_x, m_lru_lambda, m_w_out, m_norm2_g, m_w_up, m_w_down, m_final_norm_g, v_norm1_g, v_w_in, v_hgrn_lb_logits, v_hgrn_norm_g, v_gdn_conv_w, v_gdn_a_log, v_gdn_dt_bias, v_gdn_norm_g, v_lru_conv_w, v_lru_conv_b, v_lru_w_a, v_lru_b_a, v_lru_w_x, v_lru_b_x, v_lru_lambda, v_w_out, v_norm2_g, v_w_up, v_w_down, v_final_norm_g):
    args = dict(locals())
    ix, iy, core = lax.axis_index("x"), lax.axis_index("y"), lax.axis_index("c")
    chip = 2 * ix + iy

    packed = _pack_big(w_in, w_out, w_up, w_down).astype(BF16)
    first_w_in = _gather_layer(packed[0, :1024], FIRST_GATHER_ID)
    gathered = [_gather_layer(packed[l, 1024 * (l == 0):], GATHER_IDS[l]) for l in range(DEPTH)]

    def with_own(got, own):
        return lax.dynamic_update_slice(got, own[None], (chip, 0, 0))

    def full_w_in(shares):
        def cols(lo, hi):
            out = []
            for k in range(4):
                a, b = max(lo, k * n_in), min(hi, (k + 1) * n_in)
                if a < b:
                    out.append(shares[k, :1024, a - k * n_in:b - k * n_in])
            return out

        pieces = [pc for first, last, _ in sorted(IN_COL_SEGMENTS, key=lambda seg: seg[0] + seg[2])
                  for pc in cols(first, last)]
        return jnp.concatenate(pieces + [jnp.zeros((1024, D_IN_PAD - D_IN), shares.dtype)], axis=-1)

    def full_rest(shares):
        return (jnp.concatenate([shares[k, 2048:] for k in W_OUT_CHIP_ORDER], axis=0),
                jnp.concatenate([shares[k, :1024] for k in range(4)], axis=-1),
                jnp.concatenate([shares[k, 1024:2048] for k in range(4)], axis=0))

    def layer_weights(l, x):
        got, own = gathered[l], packed[l]
        if l:
            got, own, x = lax.optimization_barrier((got, own, x))
            shares = with_own(got, own)
            rest = full_rest(shares[:, 1024:])
            return x, full_w_in(shares), lambda v: (v,) + rest

        def late(v):
            got_v, own_v, v = lax.optimization_barrier((got, own, v))
            return (v,) + full_rest(with_own(got_v, own_v[1024:]))

        return x, full_w_in(with_own(first_w_in, own[:1024])), late

    n_gc, n_lc = gdn_conv_w.shape[-1], lru_conv_w.shape[-1]
    conv_full = dict(
        gdn_conv_w=lax.dynamic_update_slice(jnp.zeros((DEPTH, 4, 4 * n_gc), F32), gdn_conv_w * 0.5, (0, 0, chip * n_gc)),
        lru_conv_w=lax.dynamic_update_slice(jnp.zeros((DEPTH, 4, 4 * n_lc), F32), lru_conv_w * 0.5, (0, 0, chip * n_lc)))
    conv_shapes = {k: v.shape for k, v in conv_full.items()}
    conv_full = _unflatten_small(_allreduce_small("convw", _flatten_small(conv_full), CONV_SUM_ID), conv_shapes)

    small = {k: args[k] for k in SMALL}
    small.update(conv_full)

    n_in = D_IN // 4

    def start_scatter(l, b):
        pieces = []
        for k in range(4):
            cols = jnp.concatenate(_orig_cols(b["w_in"], k * n_in, (k + 1) * n_in), axis=1)
            r0 = 256 * W_OUT_CHIP_ORDER.index(k)
            pieces += [jnp.pad(cols, ((0, 0), (0, 1024 - n_in))), b["w_up"][:, k * 1024:(k + 1) * 1024],
                       b["w_down"][k * 1024:(k + 1) * 1024], b["w_out"][r0:r0 + 256]]
        per_chip = jnp.concatenate(pieces, axis=0).reshape(4, LAYER_ROWS, 1024)
        return _scatter_start(per_chip, chip, core, l)

    loss, grad_x, started, sg = _local_step(x[0], loss_target[0], small, layer_weights, start_scatter)

    sg["loss"] = loss[0, :1]
    shapes = {k: v.shape for k, v in sg.items()}
    sg = _unflatten_small(_allreduce_small("smallgrad", _flatten_small(sg), GRAD_SUM_ID), shapes)
    loss_out = sg.pop("loss")[0]

    buf = None
    for l in reversed(range(DEPTH)):
        own, got = started[l]
        buf = _rs_add_chips(own, got, core, l, buf)
    g_in, g_out, g_up, g_down = _unpack_big(_rs_share(buf).reshape(DEPTH, LAYER_ROWS, 1024))

    grads = dict(sg)
    grads["gdn_conv_w"] = lax.dynamic_slice_in_dim(sg["gdn_conv_w"], chip * n_gc, n_gc, axis=2)
    grads["lru_conv_w"] = lax.dynamic_slice_in_dim(sg["lru_conv_w"], chip * n_lc, n_lc, axis=2)
    grads.update(w_in=g_in, w_out=g_out, w_up=g_up, w_down=g_down)

    names = ['norm1_g', 'w_in', 'hgrn_lb_logits', 'hgrn_norm_g', 'gdn_conv_w', 'gdn_a_log', 'gdn_dt_bias', 'gdn_norm_g',
             'lru_conv_w', 'lru_conv_b', 'lru_w_a', 'lru_b_a', 'lru_w_x', 'lru_b_x', 'lru_lambda', 'w_out', 'norm2_g',
             'w_up', 'w_down', 'final_norm_g']
    big_names = ("w_in", "w_out", "w_up", "w_down")
    delta, new_m, new_v = {}, {}, {}
    for k in big_names:
        delta[k], new_m[k], new_v[k] = _adam("adam_" + k, args[k], grads[k], args["m_" + k], args["v_" + k])
    small_names = [k for k in names if k not in big_names]
    shapes = {k: args[k].shape for k in small_names}
    flat = [_flatten_small({k: src(k) for k in small_names})
            for src in (lambda k: args[k], lambda k: grads[k], lambda k: args["m_" + k], lambda k: args["v_" + k])]
    outs = _ew("adam_small", _adam_fn, flat, 3)
    for dst, o in zip((delta, new_m, new_v), outs):
        dst.update(_unflatten_small(o, shapes))
    return (loss_out, grad_x[None], *[grads[k] for k in names], *[delta[k] for k in names],
            *[new_m[k] for k in names], *[new_v[k] for k in names])
```

```python
import functools

import jax
import jax.numpy as jnp
from jax import lax
from jax.experimental import pallas as pl
from jax.experimental.pallas import tpu as pltpu
from jax.experimental.pallas import tpu_sc as plsc

F32 = jnp.float32
BF16 = jnp.bfloat16
MESH = pl.DeviceIdType.MESH

DEPTH = 4
D_MODEL = 1024
A_W, B_W, C_W = 256, 512, 256
A_HD, B_HD = 64, 128
B_NH = B_W // B_HD
D_IN = 3592
D_IN_PAD = 3840
COL_BZ, COL_AQ, COL_AF, COL_AI, COL_AG, COL_CX, COL_CY = 1536, 2048, 2304, 2560, 2816, 3072, 3328
W_OUT_CHIP_ORDER = (1, 2, 0, 3)
GATE0 = 3584
ROWS_NARROW = 1024
ROWS_WIDE = 512
CHUNK = 64
SUB = 16
EPS = 1e-6
TINY = 1e-30
MASKED_EXPONENT = -1e30
RG_C = 8.0
LR, B1, B2, AEPS, WD, STEP = 0.001, 0.9, 0.999, 1e-08, 0.01, 10
VMEM_LIMIT = 56 * 1024 * 1024
PACK_ROWS = 13 * 1024

NN = ((1,), (0,))
NT = ((1,), (1,))
TN = ((0,), (0,))


DIMS = {"nn": NN, "nt": NT, "tn": TN}


def _split(a, n):
    parts, r = [], a
    for i in range(n):
        p = r.astype(BF16)
        parts.append(p)
        if i + 1 < n:
            r = r - p.astype(F32)
    return parts


def _split_dot(a, b, mode, na, nb):
    (ca,), (cb,) = DIMS[mode]
    dims = ((DIMS[mode]), ((), ())) if a.ndim == 2 else (((ca + 1,), (cb + 1,)), ((0,), (0,)))
    pa, pb = _split(a, na), _split(b, nb)
    acc = None
    for i in range(na):
        for j in range(nb):
            if i + j < max(na, nb):
                t = lax.dot_general(pa[i], pb[j], dims, preferred_element_type=F32)
                acc = t if acc is None else acc + t
    return acc


@functools.partial(jax.custom_vjp, nondiff_argnums=(2, 3, 4, 5))
def _mdot(a, b, mode, na, nb, need):
    return _split_dot(a, b, mode, na, nb)


def _mdot_fwd(a, b, mode, na, nb, need):
    return _split_dot(a, b, mode, na, nb), (a, b)


def _mdot_bwd(mode, na, nb, need, res, ct):
    a, b = res
    n = max(na, nb)
    da, db = jnp.zeros_like(a), jnp.zeros_like(b)
    if mode == "nn":
        if need[0]:
            da = _split_dot(ct, b, "nt", n, nb)
        if need[1]:
            db = _split_dot(a, ct, "tn", na, n)
    elif mode == "nt":
        if need[0]:
            da = _split_dot(ct, b, "nn", n, nb)
        if need[1]:
            db = _split_dot(ct, a, "tn", n, na)
    else:
        if need[0]:
            da = _split_dot(b, ct, "nt", nb, n)
        if need[1]:
            db = _split_dot(a, ct, "nn", na, n)
    return da, db


_mdot.defvjp(_mdot_fwd, _mdot_bwd)
BOTH = (True, True)


def _bdot(a, b, mode="nn"):
    return _mdot(a, b, mode, 1, 1, BOTH)


def _group_sum(x, ones):
    return _mdot(x, ones, "nn", 2, 1, (True, False))


def _running_sum(x):
    shape = x.shape[:-2] + (CHUNK, CHUNK)
    tri = (_iota(shape, len(shape) - 1) <= _iota(shape, len(shape) - 2)).astype(F32)
    return _mdot(tri, x, "nn", 1, 3, (False, True))


def _iota(shape, d):
    return lax.broadcasted_iota(jnp.int32, shape, d)


def _block_ones(width, group):
    return (_iota((width, width), 0) // group == _iota((width, width), 1) // group).astype(F32)


def _params(sem):
    return pltpu.CompilerParams(dimension_semantics=sem, vmem_limit_bytes=VMEM_LIMIT)


def _tile(dim, prefs):
    for p in prefs:
        if dim % p == 0:
            return p
    return dim


def _whole(shape):
    return pl.BlockSpec(shape, lambda i: (0,) * len(shape))


def _rowspec(t, w, col0=0):
    cb = col0 // w
    return pl.BlockSpec((t, w), lambda i: (i, cb))


def _cols(arr, col0, width):
    assert col0 % width == 0
    return (arr, col0, width)


def _row_operands(rows, t):
    arrs, specs = [], []
    for r in rows:
        arr, col0, w = r if isinstance(r, tuple) else (r, 0, r.shape[1])
        arrs.append(arr)
        specs.append(_rowspec(t, w, col0))
    return arrs, specs


def _rowwise_fwd(name, fn, params, rows, out_widths, out_dtypes, t, into=None):
    rows, row_specs = _row_operands(rows, t)
    s = rows[0].shape[0]
    n_p, n_r = len(params), len(rows)
    n_buf = 0 if into is None or into[2] is None else 1

    def body(*refs):
        p = [r[...] for r in refs[:n_p]]
        xs = [r[...].astype(F32) for r in refs[n_p:n_p + n_r]]
        outs = fn(pl.program_id(0) * t, *p, *xs)
        for o_ref, o in zip(refs[n_p + n_r + n_buf:], outs):
            o_ref[...] = o.astype(o_ref.dtype)

    out_specs = [_rowspec(t, w) for w in out_widths]
    out_shape = [jax.ShapeDtypeStruct((s, w), d) for w, d in zip(out_widths, out_dtypes)]
    extra = []
    if into is not None:
        out_specs = [_rowspec(t, out_widths[0], into[0])]
        out_shape = [jax.ShapeDtypeStruct((s, into[1]), out_dtypes[0])]
        extra = [into[2]] * n_buf
    return pl.pallas_call(
        body, grid=(s // t,),
        in_specs=[_whole(p.shape) for p in params] + row_specs + [pl.BlockSpec(memory_space=pl.ANY)] * n_buf,
        out_specs=out_specs, out_shape=out_shape, input_output_aliases={n_p + n_r: 0} if n_buf else {},
        name=name, compiler_params=_params(("arbitrary",)))(*params, *rows, *extra)


def _into_spec(t, width, col0, rev_steps=None):
    cb = col0 // width
    assert col0 % width == 0
    if rev_steps is None:
        return pl.BlockSpec((t, width), lambda i: (i, cb))
    return pl.BlockSpec((t, width), lambda i: (rev_steps - 1 - i, cb))


def _rowwise_bwd(name, fn, params, rows, douts, t, need=None, res=None, dtypes=None, copy16=False, pack=None,
                 dout_fn=None):
    rows, row_specs = _row_operands(rows, t)
    douts, dout_specs = _row_operands(douts, t)
    s = rows[0].shape[0]
    n_p, n_r, n_o = len(params), len(rows), len(douts)
    need = [True] * n_r if need is None else need
    n_res = 0 if res is None else 1
    n_buf = 0 if pack is None or pack[3] is None else 1

    def body(*refs):
        p = [r[...] for r in refs[:n_p]]
        xs = [r[...].astype(F32) for r in refs[n_p:n_p + n_r]]
        dys = [r[...].astype(F32) for r in refs[n_p + n_r:n_p + n_r + n_o]]
        k = n_p + n_r + n_o
        res_ref = refs[k] if n_res else None
        k += n_res + n_buf
        dp_refs = refs[k:k + n_p]
        dx_refs = list(refs[k + n_p:])
        row0 = pl.program_id(0) * t
        _, vjp = jax.vjp(lambda *a: tuple(fn(row0, *a)), *p, *xs)
        g = vjp(tuple(dys) if dout_fn is None else tuple(dout_fn(*dys)))

        @pl.when(pl.program_id(0) == 0)
        def _():
            for r in dp_refs:
                r[...] = jnp.zeros_like(r)

        for r, gp in zip(dp_refs, g[:n_p]):
            r[...] += gp
        gx = [gi for gi, nd in zip(g[n_p:], need) if nd]
        if n_res:
            gx[0] = gx[0] + res_ref[...]
        if copy16:
            gx.append(gx[0])
        if pack is not None:
            pieces = [gx[j].astype(BF16) for j in pack[0]]
            missing = pack[2] - sum(pc.shape[1] for pc in pieces)
            if missing:
                pieces.append(jnp.zeros((t, missing), BF16))
            dx_refs.pop()[...] = pieces[0] if len(pieces) == 1 else jnp.concatenate(pieces, axis=1)
            gx = [gi for j, gi in enumerate(gx) if j not in pack[0]]
        for r, gi in zip(dx_refs, gx):
            r[...] = gi.astype(r.dtype)

    widths = [sp.block_shape[1] for sp, nd in zip(row_specs, need) if nd]
    dtypes = [F32] * len(widths) if dtypes is None else list(dtypes)
    if copy16:
        widths, dtypes = widths + widths[:1], dtypes + [BF16]
    out_specs = [_rowspec(t, w) for w in widths]
    out_shape = [jax.ShapeDtypeStruct((s, w), d) for w, d in zip(widths, dtypes)]
    extra, aliases = [], {}
    if pack is not None:
        keep = [j for j in range(len(widths)) if j not in pack[0]]
        out_specs = [out_specs[j] for j in keep] + [_into_spec(t, pack[2], pack[1])]
        out_shape = [out_shape[j] for j in keep] + [jax.ShapeDtypeStruct((s, D_IN_PAD), BF16)]
        if n_buf:
            extra = [pack[3]]
            aliases = {n_p + n_r + n_o + n_res: n_p + len(out_shape) - 1}
    outs = pl.pallas_call(
        body, grid=(s // t,),
        in_specs=([_whole(p.shape) for p in params] + row_specs + dout_specs
                  + ([_rowspec(t, res.shape[1])] if n_res else []) + [pl.BlockSpec(memory_space=pl.ANY)] * n_buf),
        out_specs=[_whole(p.shape) for p in params] + out_specs,
        out_shape=[jax.ShapeDtypeStruct(p.shape, F32) for p in params] + out_shape, input_output_aliases=aliases,
        name=name, compiler_params=_params(("arbitrary",)))(*params, *rows, *douts, *([res] if n_res else []), *extra)
    return list(outs[:n_p]), list(outs[n_p:])


def _rmsnorm_fn(row0, g, x):
    y = x * lax.rsqrt(jnp.mean(x * x, axis=-1, keepdims=True) + EPS)
    return (y * g,)


def _hgrn_pre_fn(layer, row0, logits, aq, af):
    m = jnp.max(logits, axis=0, keepdims=True)
    e = jnp.exp(logits - m)
    p = e / jnp.sum(e, axis=0, keepdims=True)
    if layer == 0:
        lb = jnp.zeros((1, A_W), F32)
    else:
        acc = p[1:2]
        for j in range(2, layer + 1):
            acc = acc + p[j:j + 1]
        lb = jnp.minimum(jnp.maximum(acc, 0.0), 1.0 - EPS)
    sig = jax.nn.sigmoid(af)
    f = lb + (1.0 - lb) * sig
    log_f = jnp.log(jnp.maximum(f, TINY))
    k = (1.0 - lb) * jax.nn.sigmoid(-af)
    return jax.nn.silu(aq), k, log_f


def _hgrn_post_fn(row0, g, o, ag):
    ms = _group_sum(o * o, _block_ones(A_W, A_HD)) * (1.0 / A_HD)
    return (o * lax.rsqrt(ms + EPS) * g * jax.nn.silu(ag),)


def _gdn_pre_fn(row0, a_log, dt_bias, qkvc, gate):
    act = jax.nn.silu(qkvc)
    q, k, v = act[:, :B_W], act[:, B_W:2 * B_W], act[:, 2 * B_W:]
    bo = _block_ones(B_W, B_HD)
    q = q * lax.rsqrt(_group_sum(q * q, bo) + EPS) * (B_HD ** -0.5)
    k = k * lax.rsqrt(_group_sum(k * k, bo) + EPS)
    beta = jax.nn.sigmoid(gate)
    log_alpha = -jnp.exp(a_log) * jax.nn.softplus(gate + dt_bias)
    t = gate.shape[0]
    spread = lambda z, lane0: jnp.concatenate(
        [jnp.broadcast_to(z[:, lane0 + h:lane0 + h + 1], (t, B_HD)) for h in range(B_NH)], axis=1)
    return q, k, v, spread(beta, 0), spread(log_alpha, B_NH)


def _gdn_post_fn(row0, g, o, bz):
    ms = _group_sum(o * o, _block_ones(B_W, B_HD)) * (1.0 / B_HD)
    return (o * lax.rsqrt(ms + EPS) * g * jax.nn.silu(bz),)


def _lru_pre_fn(row0, w_a, b_a, w_x, b_x, lam, xc):
    r = jax.nn.sigmoid(_mdot(xc, w_a, "nn", 2, 2, BOTH) + b_a)
    i = jax.nn.sigmoid(_mdot(xc, w_x, "nn", 2, 2, BOTH) + b_x)
    log_a = -RG_C * r * jax.nn.softplus(-lam)
    a = jnp.exp(log_a)
    one_minus = -jnp.tanh(log_a) * (a * a + 1.0)
    mult = jnp.sqrt(jnp.maximum(one_minus, EPS))
    pos = row0 + _iota(xc.shape, 0)
    mult = jnp.where(pos == 0, 1.0, mult)
    return a, mult * i * xc


def _lru_post_fn(row0, h, cy):
    return (jax.nn.gelu(cy, approximate=True) * h,)


def _relu2(u):
    r = jnp.maximum(u, 0.0)
    return r * r


def _drelu2(acc, u):
    return acc * (2.0 * jnp.maximum(u, 0.0))


MM_VMEM_BUDGET = 44 * 1024 * 1024


def _mm_tiles(mode, m, n, k, a_bytes, b_bytes, n_extra):
    best = None
    for tk in {k, 4096, 2048, 1024, 512, 256}:
        for tn in {n, 1280, 1024, 768, 512, 256, 128}:
            for tm in {m, 1024, 512, 256, 128}:
                if m % tm or n % tn or k % tk or tm > 1024:
                    continue
                tiles = tm * tk * a_bytes + tk * tn * b_bytes + (1 + n_extra) * tm * tn * 4
                if 2 * tiles + (tm * tn * 4 if tk < k else 0) > MM_VMEM_BUDGET:
                    continue
                a_reads = 1 if tk == k else n // tn
                b_reads = 1 if (tk == k and tn == n) else m // tm
                cost = m * k * a_bytes * a_reads + k * n * b_bytes * b_reads + m * n * 4 * (k // tk - 1)
                if best is None or (cost, -tm) < best[0]:
                    best = ((cost, -tm), (tm, tn, tk))
    if best is None:
        raise ValueError("no matmul tiling fits VMEM")
    return best[1]


def _matmul(name, a, b, mode, a_fn=None, res=None, epi=None, epi_in=None, out_dtype=F32, also=None):
    if mode == "nn":
        (m, k), n = a.shape, b.shape[1]
    elif mode == "nt":
        (m, k), n = a.shape, b.shape[0]
    else:
        (k, m), n = a.shape, b.shape[1]
    n_extra = (res is not None) + (epi_in is not None) + (also is not None)
    tm, tn, tk = _mm_tiles(mode, m, n, k, a.dtype.itemsize, b.dtype.itemsize, n_extra)
    nk = k // tk
    a_spec = {"nn": pl.BlockSpec((tm, tk), lambda i, j, kk: (i, kk)),
              "nt": pl.BlockSpec((tm, tk), lambda i, j, kk: (i, kk)),
              "tn": pl.BlockSpec((tk, tm), lambda i, j, kk: (kk, i))}[mode]
    b_spec = {"nn": pl.BlockSpec((tk, tn), lambda i, j, kk: (kk, j)),
              "nt": pl.BlockSpec((tn, tk), lambda i, j, kk: (j, kk)),
              "tn": pl.BlockSpec((tk, tn), lambda i, j, kk: (kk, j))}[mode]
    o_spec = pl.BlockSpec((tm, tn), lambda i, j, kk: (i, j))
    dims = {"nn": NN, "nt": NT, "tn": TN}[mode]
    extra = [x for x in (res, epi_in) if x is not None]

    def body(*refs):
        a_ref, b_ref = refs[0], refs[1]
        rest = list(refs[2:])
        res_ref = rest.pop(0) if res is not None else None
        epi_ref = rest.pop(0) if epi_in is not None else None
        o_ref = rest.pop(0)
        also_ref = rest.pop(0) if also is not None else None
        acc = rest[0] if nk > 1 else None
        kk = pl.program_id(2)
        at = a_ref[...]
        if a_fn is not None:
            at = a_fn(at.astype(F32))
        part = lax.dot_general(at.astype(BF16), b_ref[...].astype(BF16), (dims, ((), ())),
                               preferred_element_type=F32)

        def finish(out):
            if res_ref is not None:
                out = out + res_ref[...]
            if epi is not None:
                out = epi(out, epi_ref[...])
            o_ref[...] = out.astype(o_ref.dtype)
            if also is not None:
                also_ref[...] = also(out).astype(also_ref.dtype)

        if nk == 1:
            finish(part)
        else:
            @pl.when(kk == 0)
            def _():
                acc[...] = part

            @pl.when(jnp.logical_and(kk > 0, kk < nk - 1))
            def _():
                acc[...] += part

            @pl.when(kk == nk - 1)
            def _():
                finish(acc[...] + part)

    out_shape = [jax.ShapeDtypeStruct((m, n), out_dtype)] + ([jax.ShapeDtypeStruct((m, n), BF16)] if also else [])
    outs = pl.pallas_call(
        body, grid=(m // tm, n // tn, nk),
        in_specs=[a_spec, b_spec] + [o_spec] * len(extra), out_specs=[o_spec] * len(out_shape), out_shape=out_shape,
        scratch_shapes=[pltpu.VMEM((tm, tn), F32)] if nk > 1 else [],
        name=name, compiler_params=_params(("arbitrary", "arbitrary", "arbitrary")))(a, b, *extra)
    return outs if also else outs[0]


HALO = 8


def _conv_fwd(name, x, col0, width, w, bias, t=512):
    s = x.shape[0]
    cb = col0 // width
    hb = t // HALO

    def body(x_ref, halo_ref, w_ref, b_ref, y_ref):
        i = pl.program_id(0)
        halo = jnp.where(i == 0, 0.0, halo_ref[...])
        xp = jnp.concatenate([halo, x_ref[...]], axis=0)
        wv = w_ref[...]
        y = b_ref[...] + wv[0:1] * xp[HALO - 3:HALO - 3 + t]
        for j in range(1, 4):
            y = y + wv[j:j + 1] * xp[HALO - 3 + j:HALO - 3 + j + t]
        y_ref[...] = y

    return pl.pallas_call(
        body, grid=(s // t,),
        in_specs=[pl.BlockSpec((t, width), lambda i: (i, cb)),
                  pl.BlockSpec((HALO, width), lambda i: (jnp.maximum(i * hb - 1, 0), cb)),
                  _whole(w.shape), _whole(bias.shape)],
        out_specs=_rowspec(t, width), out_shape=jax.ShapeDtypeStruct((s, width), F32),
        name=name, compiler_params=_params(("arbitrary",)))(x, x, w, bias)


def _conv_bwd(name, dy, x, col0, width, w, buf, t=512):
    s = x.shape[0]
    cb = col0 // width
    hb = t // HALO
    nblk = s // t

    def body(dy_ref, dyn_ref, x_ref, halo_ref, w_ref, buf_ref, dx_ref, dw_ref, db_ref):
        i = pl.program_id(0)
        dyv = dy_ref[...]
        nxt = jnp.where(i == nblk - 1, 0.0, dyn_ref[...])
        dyp = jnp.concatenate([dyv, nxt], axis=0)
        wv = w_ref[...]
        dx = wv[3:4] * dyv
        for j in range(3):
            dx = dx + wv[j:j + 1] * dyp[3 - j:3 - j + t]
        dx_ref[...] = dx.astype(dx_ref.dtype)
        halo = jnp.where(i == 0, 0.0, halo_ref[...])
        xp = jnp.concatenate([halo, x_ref[...]], axis=0)
        dw = jnp.concatenate(
            [jnp.sum(dyv * xp[HALO - 3 + j:HALO - 3 + j + t], axis=0, keepdims=True) for j in range(4)], axis=0)

        @pl.when(i == 0)
        def _():
            dw_ref[...] = jnp.zeros_like(dw_ref)
            db_ref[...] = jnp.zeros_like(db_ref)

        dw_ref[...] += dw
        db_ref[...] += jnp.sum(dyv, axis=0, keepdims=True)

    return pl.pallas_call(
        body, grid=(nblk,),
        in_specs=[_rowspec(t, width),
                  pl.BlockSpec((HALO, width), lambda i: (jnp.minimum((i + 1) * hb, s // HALO - 1), 0)),
                  pl.BlockSpec((t, width), lambda i: (i, cb)),
                  pl.BlockSpec((HALO, width), lambda i: (jnp.maximum(i * hb - 1, 0), cb)),
                  _whole(w.shape), pl.BlockSpec(memory_space=pl.ANY)],
        out_specs=[_into_spec(t, width, col0), _whole((4, width)), _whole((1, width))],
        out_shape=[jax.ShapeDtypeStruct(buf.shape, buf.dtype), jax.ShapeDtypeStruct((4, width), F32),
                   jax.ShapeDtypeStruct((1, width), F32)],
        input_output_aliases={5: 0},
        name=name, compiler_params=_params(("arbitrary",)))(dy, dy, x, x, w, buf)


def _hgrn_chunk(st, q, k, v, lf):
    cum = _running_sum(lf)
    tot = jnp.sum(lf, axis=0, keepdims=True)
    bm = _iota((A_W, A_W), 0) // A_HD == _iota((A_W, A_W), 1) // A_HD
    bo = bm.astype(F32)
    o_inter = _bdot(q * jnp.exp(cum), st, "nt")
    kd = k * jnp.exp(tot - cum)
    st_new = st * jnp.exp(tot) + jnp.where(bm, _bdot(v, kd, "tn"), 0.0)
    lane_h = _iota((SUB, A_W), 1) // A_HD
    n_h = A_W // A_HD
    t3 = _iota((SUB, SUB, A_W), 0)
    s3 = _iota((SUB, SUB, A_W), 1)
    outs = []
    for i in range(CHUNK // SUB):
        sl = slice(i * SUB, (i + 1) * SUB)
        qi, ki, vi, ci = q[sl], k[sl], v[sl], cum[sl]
        e = jnp.exp(jnp.where(s3 <= t3, ci[:, None, :] - ci[None, :, :], MASKED_EXPONENT))
        d3 = qi[:, None, :] * ki[None, :, :] * e
        ds = _mdot(d3.reshape(SUB * SUB, A_W), bo, "nn", 1, 1, (True, False)).reshape(SUB, SUB, A_W)
        oi = jnp.sum(ds * vi[None, :, :], axis=1)
        if i > 0:
            n = i * SUB
            cb = cum[n - 1:n]
            qt = qi * jnp.exp(ci - cb)
            kt = k[:n] * jnp.exp(cb - cum[:n])
            qs = jnp.concatenate([jnp.where(lane_h == h, qt, 0.0) for h in range(n_h)], axis=0)
            p = _bdot(_bdot(qs, kt, "nt"), v[:n])
            for h in range(n_h):
                oi = oi + jnp.where(lane_h == h, p[h * SUB:(h + 1) * SUB], 0.0)
        outs.append(oi)
    return st_new, o_inter + jnp.concatenate(outs, axis=0)


HGRN_CHUNKS = 2


def _hgrn_fwd(name, q, k, v, lf):
    s = q.shape[0]
    n = s // CHUNK
    g = HGRN_CHUNKS

    def body(q_ref, k_ref, v_ref, lf_ref, o_ref, sts_ref, st_ref):
        @pl.when(pl.program_id(0) == 0)
        def _():
            st_ref[...] = jnp.zeros_like(st_ref)

        st = st_ref[...]
        for c in range(g):
            rows = pl.ds(c * CHUNK, CHUNK)
            sts_ref[c] = st
            st, o = _hgrn_chunk(st, q_ref[rows], k_ref[rows], v_ref[rows], lf_ref[rows])
            o_ref[rows] = o
        st_ref[...] = st

    cs = _rowspec(g * CHUNK, A_W)
    return pl.pallas_call(
        body, grid=(n // g,), in_specs=[cs] * 4,
        out_specs=[cs, pl.BlockSpec((g, A_W, A_W), lambda i: (i, 0, 0))],
        out_shape=[jax.ShapeDtypeStruct((s, A_W), F32), jax.ShapeDtypeStruct((n, A_W, A_W), F32)],
        scratch_shapes=[pltpu.VMEM((A_W, A_W), F32)],
        name=name, compiler_params=_params(("arbitrary",)))(q, k, v, lf)


def _hgrn_bwd(name, q, k, v, lf, sts, do, buf, col0):
    s = q.shape[0]
    n = s // CHUNK
    g = HGRN_CHUNKS
    nsteps = n // g

    def body(q_ref, k_ref, v_ref, lf_ref, sts_ref, do_ref, buf_ref, dq_ref, dk_ref, dv_ref, dlf_ref, dst_ref):
        @pl.when(pl.program_id(0) == 0)
        def _():
            dst_ref[...] = jnp.zeros_like(dst_ref)

        dst = dst_ref[...]
        for c in reversed(range(g)):
            rows = pl.ds(c * CHUNK, CHUNK)
            _, vjp = jax.vjp(_hgrn_chunk, sts_ref[c], q_ref[rows], k_ref[rows], v_ref[rows], lf_ref[rows])
            ct = vjp((dst, do_ref[rows]))
            dst = ct[0]
            for r, gi in zip((dq_ref, dk_ref, dv_ref, dlf_ref), ct[1:]):
                r[rows] = gi.astype(r.dtype)
        dst_ref[...] = dst

    rs = pl.BlockSpec((g * CHUNK, A_W), lambda i: (nsteps - 1 - i, 0))
    rows = jax.ShapeDtypeStruct((s, A_W), F32)
    return pl.pallas_call(
        body, grid=(nsteps,),
        in_specs=[rs] * 4 + [pl.BlockSpec((g, A_W, A_W), lambda i: (nsteps - 1 - i, 0, 0)), rs,
                             pl.BlockSpec(memory_space=pl.ANY)],
        out_specs=[rs, rs, _into_spec(g * CHUNK, A_W, col0, nsteps), rs],
        out_shape=[rows, rows, jax.ShapeDtypeStruct(buf.shape, buf.dtype), rows],
        input_output_aliases={6: 2}, scratch_shapes=[pltpu.VMEM((A_W, A_W), F32)],
        name=name, compiler_params=_params(("arbitrary",)))(q, k, v, lf, sts, do, buf)


PREP_CHUNKS = 4
REC_CHUNKS = 8
DEC_ROWS = 8


def _to_heads(x, rows=CHUNK, width=B_HD):
    n = x.shape[0] // rows
    x3 = x.reshape(n, rows, x.shape[1])
    return jnp.concatenate([x3[:, :, h * B_HD:h * B_HD + width] for h in range(B_NH)], axis=0)


def _from_heads(y):
    n = y.shape[0] // B_NH
    x3 = jnp.concatenate([y[h * n:(h + 1) * n] for h in range(B_NH)], axis=2)
    return x3.reshape(n * y.shape[1], B_NH * B_HD)


@jax.custom_vjp
def _unit_lower_inverse(a_mat):
    nil = -a_mat
    eye = (_iota(a_mat.shape, 1) == _iota(a_mat.shape, 2)).astype(F32)
    inv = eye + nil
    sq = 2
    while sq < CHUNK:
        nil = _split_dot(nil, nil, "nn", 2, 2)
        inv = inv + _split_dot(inv, nil, "nn", 2, 2)
        sq *= 2
    return inv


def _unit_lower_inverse_fwd(a_mat):
    inv = _unit_lower_inverse(a_mat)
    return inv, inv


def _unit_lower_inverse_bwd(inv, ct):
    return (-_split_dot(_split_dot(inv, ct, "tn", 2, 2), inv, "nt", 2, 2),)


_unit_lower_inverse.defvjp(_unit_lower_inverse_fwd, _unit_lower_inverse_bwd)


@jax.custom_vjp
def _known_inverse(a_mat, inv):
    return inv


def _known_inverse_fwd(a_mat, inv):
    return inv, inv


def _known_inverse_bwd(inv, ct):
    return _unit_lower_inverse_bwd(inv, ct)[0], jnp.zeros_like(inv)


_known_inverse.defvjp(_known_inverse_fwd, _known_inverse_bwd)


def _gdn_prep(q, k, v, beta, la, inv_saved=None):
    q, k, v, beta, la = [_to_heads(z) for z in (q, k, v, beta, la)]
    nb = q.shape[0]
    r = _iota((nb, CHUNK, CHUNK), 1)
    c = _iota((nb, CHUNK, CHUNK), 2)
    causal = c <= r
    g = _running_sum(la)
    tot = jnp.sum(la, axis=1, keepdims=True)
    gcol = g[:, :, :CHUNK]
    decay = jnp.exp(jnp.where(causal, gcol - jnp.swapaxes(gcol, 1, 2), MASKED_EXPONENT))
    kb = k * beta
    a_mat = jnp.where(c < r, _bdot(kb, k, "nt") * decay, 0.0)
    if inv_saved is None:
        inv = _unit_lower_inverse(a_mat)
    else:
        inv = _known_inverse(a_mat, _to_heads(inv_saved, width=CHUNK))
    eg = jnp.exp(g)
    u = _bdot(inv, v * beta)
    w = _bdot(inv, kb * eg)
    qk = jnp.where(causal, _bdot(q, k, "nt") * decay, 0.0)
    widen = lambda z: jnp.concatenate([z, jnp.zeros((nb, CHUNK, B_HD - CHUNK), F32)], axis=2)
    dec = jnp.broadcast_to(jnp.exp(tot), (nb, DEC_ROWS, B_HD))
    return tuple(_from_heads(z) for z in (u, w, widen(qk), q * eg, k * jnp.exp(tot - g), dec, widen(inv)))


def _gdn_rec(st, u, w, qk, qd, kd, dec):
    u, w, qd, kd = [_to_heads(z) for z in (u, w, qd, kd)]
    qk = _to_heads(qk, width=CHUNK)
    v_new = u - _bdot(w, st)
    o = _bdot(qd, st) + _bdot(qk, v_new)
    st_new = st * _to_heads(dec, rows=DEC_ROWS)[:, :1, :1] + _bdot(kd, v_new, "tn")
    return st_new, _from_heads(o)


def _gdn_prep_fwd(name, q, k, v, beta, la):
    s = q.shape[0]
    t = PREP_CHUNKS * CHUNK
    td = PREP_CHUNKS * DEC_ROWS

    def body(*refs):
        outs = _gdn_prep(*[r[...] for r in refs[:5]])
        for o_ref, o in zip(refs[5:], outs):
            o_ref[...] = o

    rows, dec_rows = jax.ShapeDtypeStruct((s, B_W), F32), jax.ShapeDtypeStruct((s // CHUNK * DEC_ROWS, B_W), F32)
    return pl.pallas_call(
        body, grid=(s // t,), in_specs=[_rowspec(t, B_W)] * 5,
        out_specs=[_rowspec(t, B_W)] * 5 + [_rowspec(td, B_W), _rowspec(t, B_W)],
        out_shape=[rows] * 5 + [dec_rows, rows],
        name=name, compiler_params=_params(("arbitrary",)))(q, k, v, beta, la)


def _gdn_prep_bwd(name, q, k, v, beta, la, inv, cts):
    s = q.shape[0]
    t = PREP_CHUNKS * CHUNK
    td = PREP_CHUNKS * DEC_ROWS

    def body(*refs):
        inv_saved = refs[5][...]
        _, vjp = jax.vjp(lambda *a: _gdn_prep(*a, inv_saved=inv_saved)[:6], *[r[...] for r in refs[:5]])
        g = vjp(tuple(r[...] for r in refs[6:12]))
        for o_ref, o in zip(refs[12:], g):
            o_ref[...] = o

    return pl.pallas_call(
        body, grid=(s // t,), in_specs=[_rowspec(t, B_W)] * 11 + [_rowspec(td, B_W)],
        out_specs=[_rowspec(t, B_W)] * 5, out_shape=[jax.ShapeDtypeStruct((s, B_W), F32)] * 5,
        name=name, compiler_params=_params(("arbitrary",)))(q, k, v, beta, la, inv, *cts)


def _gdn_rec_fwd(name, u, w, qk, qd, kd, dec):
    s = u.shape[0]
    n = s // CHUNK
    g = REC_CHUNKS

    def body(u_ref, w_ref, qk_ref, qd_ref, kd_ref, dec_ref, o_ref, sts_ref, st_ref):
        @pl.when(pl.program_id(0) == 0)
        def _():
            st_ref[...] = jnp.zeros_like(st_ref)

        st = st_ref[...]
        for c in range(g):
            rows, drows = pl.ds(c * CHUNK, CHUNK), pl.ds(c * DEC_ROWS, DEC_ROWS)
            sts_ref[c] = st
            st, o = _gdn_rec(st, u_ref[rows], w_ref[rows], qk_ref[rows], qd_ref[rows], kd_ref[rows], dec_ref[drows])
            o_ref[rows] = o
        st_ref[...] = st

    cs = _rowspec(g * CHUNK, B_W)
    return pl.pallas_call(
        body, grid=(n // g,), in_specs=[cs] * 5 + [_rowspec(g * DEC_ROWS, B_W)],
        out_specs=[cs, pl.BlockSpec((g, B_NH, B_HD, B_HD), lambda i: (i, 0, 0, 0))],
        out_shape=[jax.ShapeDtypeStruct((s, B_W), F32), jax.ShapeDtypeStruct((n, B_NH, B_HD, B_HD), F32)],
        scratch_shapes=[pltpu.VMEM((B_NH, B_HD, B_HD), F32)],
        name=name, compiler_params=_params(("arbitrary",)))(u, w, qk, qd, kd, dec)


def _gdn_rec_bwd(name, u, w, qk, qd, kd, dec, sts, do):
    s = u.shape[0]
    n = s // CHUNK
    g = REC_CHUNKS
    nsteps = n // g

    def body(u_ref, w_ref, qk_ref, qd_ref, kd_ref, dec_ref, sts_ref, do_ref,
             du_ref, dw_ref, dqk_ref, dqd_ref, dkd_ref, ddec_ref, dst_ref):
        @pl.when(pl.program_id(0) == 0)
        def _():
            dst_ref[...] = jnp.zeros_like(dst_ref)

        dst = dst_ref[...]
        for c in reversed(range(g)):
            rows, drows = pl.ds(c * CHUNK, CHUNK), pl.ds(c * DEC_ROWS, DEC_ROWS)
            _, vjp = jax.vjp(_gdn_rec, sts_ref[c], u_ref[rows], w_ref[rows], qk_ref[rows], qd_ref[rows], kd_ref[rows],
                             dec_ref[drows])
            ct = vjp((dst, do_ref[rows]))
            dst = ct[0]
            for r, gi in zip((du_ref, dw_ref, dqk_ref, dqd_ref, dkd_ref), ct[1:6]):
                r[rows] = gi
            ddec_ref[drows] = ct[6]
        dst_ref[...] = dst

    rs = pl.BlockSpec((g * CHUNK, B_W), lambda i: (nsteps - 1 - i, 0))
    ds = pl.BlockSpec((g * DEC_ROWS, B_W), lambda i: (nsteps - 1 - i, 0))
    return pl.pallas_call(
        body, grid=(nsteps,),
        in_specs=[rs] * 5 + [ds, pl.BlockSpec((g, B_NH, B_HD, B_HD), lambda i: (nsteps - 1 - i, 0, 0, 0)), rs],
        out_specs=[rs] * 5 + [ds],
        out_shape=[jax.ShapeDtypeStruct((s, B_W), F32)] * 5 + [jax.ShapeDtypeStruct((n * DEC_ROWS, B_W), F32)],
        scratch_shapes=[pltpu.VMEM((B_NH, B_HD, B_HD), F32)],
        name=name, compiler_params=_params(("arbitrary",)))(u, w, qk, qd, kd, dec, sts, do)


SCAN_T = 512


def _block_scan(a, b, reverse):
    t = a.shape[0]
    rows = _iota(a.shape, 0)
    sft = 1
    while sft < t:
        if reverse:
            a_s, b_s = pltpu.roll(a, t - sft, 0), pltpu.roll(b, t - sft, 0)
            edge = rows >= t - sft
        else:
            a_s, b_s = pltpu.roll(a, sft, 0), pltpu.roll(b, sft, 0)
            edge = rows < sft
        a_s = jnp.where(edge, 1.0, a_s)
        b_s = jnp.where(edge, 0.0, b_s)
        b = a * b_s + b
        a = a * a_s
        sft *= 2
    return a, b


def _lru_scan_fwd(name, a, b):
    s, w = a.shape
    t = SCAN_T

    def body(a_ref, b_ref, h_ref, hp_ref, carry):
        @pl.when(pl.program_id(0) == 0)
        def _():
            carry[...] = jnp.zeros_like(carry)

        h_in = carry[0:1]
        ca, cb = _block_scan(a_ref[...], b_ref[...], False)
        h = ca * h_in + cb
        h_ref[...] = h
        hp_ref[...] = jnp.where(_iota(h.shape, 0) == 0, h_in, pltpu.roll(h, 1, 0))
        carry[...] = jnp.broadcast_to(h[t - 1:t], carry.shape)

    return pl.pallas_call(
        body, grid=(s // t,), in_specs=[_rowspec(t, w)] * 2, out_specs=[_rowspec(t, w)] * 2,
        out_shape=[jax.ShapeDtypeStruct((s, w), F32)] * 2, scratch_shapes=[pltpu.VMEM((8, w), F32)],
        name=name, compiler_params=_params(("arbitrary",)))(a, b)


def _lru_scan_bwd(name, a, dh):
    s, w = a.shape
    t = SCAN_T
    n = s // t

    def body(a_ref, dh_ref, g_ref, a_next, g_next):
        @pl.when(pl.program_id(0) == 0)
        def _():
            a_next[...] = jnp.zeros_like(a_next)
            g_next[...] = jnp.zeros_like(g_next)

        av = a_ref[...]
        a_up = jnp.where(_iota(av.shape, 0) == t - 1, a_next[0:1], pltpu.roll(av, t - 1, 0))
        ca, cb = _block_scan(a_up, dh_ref[...], True)
        g = ca * g_next[0:1] + cb
        g_ref[...] = g
        a_next[...] = jnp.broadcast_to(av[0:1], a_next.shape)
        g_next[...] = jnp.broadcast_to(g[0:1], g_next.shape)

    rs = pl.BlockSpec((t, w), lambda i: (n - 1 - i, 0))
    return pl.pallas_call(
        body, grid=(n,), in_specs=[rs] * 2, out_specs=rs, out_shape=jax.ShapeDtypeStruct((s, w), F32),
        scratch_shapes=[pltpu.VMEM((8, w), F32), pltpu.VMEM((8, w), F32)],
        name=name, compiler_params=_params(("arbitrary",)))(a, dh)


def _loss_fn(g, x, tgt):
    y = x * lax.rsqrt(jnp.mean(x * x, axis=-1, keepdims=True) + EPS) * g
    err = y - tgt
    return 0.5 * jnp.sum(jnp.mean(err * err, axis=-1, keepdims=True), axis=0, keepdims=True)


def _loss_and_grad(name, g, x, tgt, t=256):
    s, d = x.shape

    def body(g_ref, x_ref, t_ref, loss_ref, dx_ref, dg_ref, dx16_ref):
        val, vjp = jax.vjp(_loss_fn, g_ref[...], x_ref[...], t_ref[...])
        dg, dx, _ = vjp(jnp.ones((1, 1), F32))

        @pl.when(pl.program_id(0) == 0)
        def _():
            loss_ref[...] = jnp.zeros_like(loss_ref)
            dg_ref[...] = jnp.zeros_like(dg_ref)

        loss_ref[...] += jnp.broadcast_to(val, loss_ref.shape)
        dg_ref[...] += dg
        dx_ref[...] = dx
        dx16_ref[...] = dx.astype(BF16)

    return pl.pallas_call(
        body, grid=(s // t,), in_specs=[_whole(g.shape), _rowspec(t, d), _rowspec(t, d)],
        out_specs=[_whole((1, 128)), _rowspec(t, d), _whole(g.shape), _rowspec(t, d)],
        out_shape=[jax.ShapeDtypeStruct((1, 128), F32), jax.ShapeDtypeStruct((s, d), F32),
                   jax.ShapeDtypeStruct(g.shape, F32), jax.ShapeDtypeStruct((s, d), BF16)],
        name=name, compiler_params=_params(("arbitrary",)))(g, x, tgt)


def _ew(name, fn, ins, n_out, t=None):
    r, c = ins[0].shape
    t = _tile(r, (512, 256, 128, 64, 32, 16, 8)) if t is None else t

    def body(*refs):
        outs = fn(*[x[...] for x in refs[:len(ins)]])
        for o_ref, o in zip(refs[len(ins):], outs):
            o_ref[...] = o

    return pl.pallas_call(
        body, grid=(r // t,), in_specs=[_rowspec(t, c)] * len(ins), out_specs=[_rowspec(t, c)] * n_out,
        out_shape=[jax.ShapeDtypeStruct((r, c), F32)] * n_out,
        name=name, compiler_params=_params(("arbitrary",)))(*ins)


def _adam_fn(w, g, m, v):
    m = B1 * m + (1.0 - B1) * g
    v = B2 * v + (1.0 - B2) * (g * g)
    m_hat = m / (1.0 - B1 ** STEP)
    v_hat = v / (1.0 - B2 ** STEP)
    delta = -LR * (m_hat / (jnp.sqrt(v_hat) + AEPS) + WD * w)
    return delta, m, v


def _adam(name, w, g, m, v):
    shp = w.shape
    two = lambda z: z.reshape(-1, shp[-1])
    outs = _ew(name, _adam_fn, [two(w), two(g), two(m), two(v)], 3)
    return [o.reshape(shp) for o in outs]


LAYER_ROWS = PACK_ROWS // DEPTH
HALF_ROWS = LAYER_ROWS // 2
SUM_ROWS = HALF_ROWS // 2
GATHER_IDS = (1, 2, 3, 4)
SCATTER_IDS = (5, 6, 7, 8)
FIRST_GATHER_ID = 9
CONV_SUM_ID, GRAD_SUM_ID = 10, 11


def _mesh_pos():
    ix, iy, core = lax.axis_index("x"), lax.axis_index("y"), lax.axis_index("c")
    others = [(ix, 1 - iy), (1 - ix, iy), (1 - ix, 1 - iy)]
    return ix, iy, core, others


def _pack_big(w_in, w_out, w_up, w_down):
    pad = jnp.pad(w_in, ((0, 0), (0, 0), (0, 1024 - w_in.shape[-1])))
    return jnp.concatenate([pad, w_up, w_down, w_out], axis=1)


def _unpack_big(p):
    w_in = p[..., 0:1024, :D_IN // 4]
    return w_in, p[..., 3072:LAYER_ROWS, :], p[..., 1024:2048, :], p[..., 2048:3072, :]


def _gather_layer(packed, collective_id):
    src = jax.new_ref(packed, memory_space=pltpu.MemorySpace.HBM)
    out = jax.empty_ref(jax.ShapeDtypeStruct((4,) + packed.shape, packed.dtype), memory_space=pltpu.MemorySpace.HBM)
    dma = pltpu.SemaphoreType.DMA

    @pl.kernel(mesh=plsc.ScalarSubcoreMesh(axis_name="seq", num_cores=1), name="gather_%d" % collective_id,
               scratch_types=(dma,) * 12, compiler_params=pltpu.CompilerParams(collective_id=collective_id))
    def launch(*sems):
        send_sems, recv_sems = sems[:6], sems[6:]
        ix, iy, core, others = _mesh_pos()
        barrier = pltpu.get_barrier_semaphore()
        for peer in [(ox, oy, core) for ox, oy in others] + [(ix, iy, 1 - core)]:
            pl.semaphore_signal(barrier, inc=1, device_id=peer, device_id_type=MESH)
        pl.semaphore_wait(barrier, 4)
        half = packed.shape[0] // 2
        mine = pl.ds(core * half, half)
        theirs = pl.ds((1 - core) * half, half)

        def copy(k, src_ref, owner, rows, to):
            return pltpu.make_async_remote_copy(
                src_ref=src_ref, dst_ref=out.at[owner, rows], send_sem=send_sems[k], recv_sem=recv_sems[k],
                device_id=to, device_id_type=MESH)

        chip = 2 * ix + iy
        first = [copy(j, src.at[mine], chip, mine, (*others[j], core)) for j in range(3)]
        for cp in first:
            cp.start()
        passed = []
        for j, (ox, oy) in enumerate(others):
            owner = 2 * ox + oy
            copy(j, src.at[mine], owner, mine, (ix, iy, core)).wait_recv()
            fwd = copy(3 + j, out.at[owner, mine], owner, mine, (ix, iy, 1 - core))
            fwd.start()
            passed.append(fwd)
        for j, (ox, oy) in enumerate(others):
            copy(3 + j, src.at[theirs], 2 * ox + oy, theirs, (ix, iy, core)).wait_recv()
        for cp in first + passed:
            cp.wait_send()

    launch()
    return out[...]


def _rs_sibling(per_chip):
    hbm = pl.BlockSpec(memory_space=pltpu.HBM)

    def body(src, out, send_sem, recv_sem):
        ix, iy, core, _ = _mesh_pos()
        give = pl.ds((1 - core) * HALF_ROWS, HALF_ROWS)
        cp = pltpu.make_async_remote_copy(src_ref=src.at[pl.ds(0, 4), give], dst_ref=out, send_sem=send_sem,
                                          recv_sem=recv_sem, device_id=(ix, iy, 1 - core), device_id_type=MESH)
        cp.start()
        cp.wait()

    return pl.pallas_call(
        body, in_specs=[hbm], out_specs=hbm, out_shape=jax.ShapeDtypeStruct((4, HALF_ROWS, 1024), F32),
        scratch_shapes=[pltpu.SemaphoreType.DMA, pltpu.SemaphoreType.DMA],
        name="rs_sibling", compiler_params=pltpu.CompilerParams(has_side_effects=True))(per_chip)


def _rs_add_pair(per_chip, sib, chip, core):
    nb = HALF_ROWS // SUM_ROWS
    blk = (1, SUM_ROWS, 1024)

    def body(chip_ref, core_ref, a_ref, b_ref, own_ref, send_ref):
        total = a_ref[0] + b_ref[0]
        send_ref[0] = total.astype(BF16)

        @pl.when(pl.program_id(1) == chip_ref[0])
        def _():
            own_ref[...] = total

    return pl.pallas_call(
        body,
        grid_spec=pltpu.PrefetchScalarGridSpec(
            num_scalar_prefetch=2, grid=(nb, 4),
            in_specs=[pl.BlockSpec(blk, lambda i, k, ch, co: (k, co[0] * nb + i, 0)),
                      pl.BlockSpec(blk, lambda i, k, ch, co: (k, i, 0))],
            out_specs=[pl.BlockSpec((SUM_ROWS, 1024), lambda i, k, ch, co: (i, 0)),
                       pl.BlockSpec(blk, lambda i, k, ch, co: (k, i, 0))]),
        out_shape=[jax.ShapeDtypeStruct((HALF_ROWS, 1024), F32), jax.ShapeDtypeStruct((4, HALF_ROWS, 1024), BF16)],
        name="rs_add_pair", compiler_params=_params(("arbitrary", "arbitrary")))(
            chip.reshape(1), core.reshape(1), per_chip, sib)


def _rs_chips(pair, collective_id):
    src_ref = jax.new_ref(pair, memory_space=pltpu.MemorySpace.HBM)
    got_ref = jax.empty_ref(jax.ShapeDtypeStruct((3,) + pair.shape[1:], pair.dtype), memory_space=pltpu.MemorySpace.HBM)
    dma = pltpu.SemaphoreType.DMA

    @pl.kernel(mesh=plsc.ScalarSubcoreMesh(axis_name="seq", num_cores=1), name="rs_chips_%d" % collective_id,
               scratch_types=(dma,) * 6, compiler_params=pltpu.CompilerParams(collective_id=collective_id))
    def launch(s0, s1, s2, r0, r1, r2):
        ix, iy, core, others = _mesh_pos()
        barrier = pltpu.get_barrier_semaphore()
        for ox, oy in others:
            pl.semaphore_signal(barrier, inc=1, device_id=(ox, oy, core), device_id_type=MESH)
        pl.semaphore_wait(barrier, 3)
        copies = []
        for j, ((ox, oy), send_sem, recv_sem) in enumerate(zip(others, (s0, s1, s2), (r0, r1, r2))):
            cp = pltpu.make_async_remote_copy(src_ref=src_ref.at[2 * ox + oy], dst_ref=got_ref.at[j], send_sem=send_sem,
                                              recv_sem=recv_sem, device_id=(ox, oy, core), device_id_type=MESH)
            cp.start()
            copies.append(cp)
        for cp in copies:
            cp.wait()

    launch()
    return got_ref[...]


def _rs_add_chips(own, got, core, layer, buf):
    nb = HALF_ROWS // SUM_ROWS
    blk = (1, SUM_ROWS, 1024)

    def body(core_ref, a_ref, g0_ref, g1_ref, g2_ref, *rest):
        o_ref = rest[-1]
        o_ref[0, 0] = ((a_ref[...] + g0_ref[0].astype(F32)) + g1_ref[0].astype(F32)) + g2_ref[0].astype(F32)

    aliased = [] if buf is None else [buf]
    return pl.pallas_call(
        body,
        grid_spec=pltpu.PrefetchScalarGridSpec(
            num_scalar_prefetch=1, grid=(nb,),
            in_specs=[pl.BlockSpec((SUM_ROWS, 1024), lambda i, c: (i, 0))]
            + [pl.BlockSpec(blk, functools.partial(lambda j, i, c: (j, i, 0), j)) for j in range(3)]
            + [pl.BlockSpec(memory_space=pl.ANY)] * len(aliased),
            out_specs=pl.BlockSpec((1, 1, SUM_ROWS, 1024), lambda i, c: (layer, c[0], i, 0))),
        out_shape=jax.ShapeDtypeStruct((DEPTH, 2, HALF_ROWS, 1024), F32),
        input_output_aliases={} if buf is None else {5: 0},
        name="rs_add_chips", compiler_params=_params(("arbitrary",)))(core.reshape(1), own, got, got, got, *aliased)


def _rs_share(buf):
    hbm = pl.BlockSpec(memory_space=pltpu.HBM)

    def body(src, out, send_sem, recv_sem):
        ix, iy, core, _ = _mesh_pos()
        layers = pl.ds(0, DEPTH)
        cp = pltpu.make_async_remote_copy(src_ref=src.at[layers, core], dst_ref=out.at[layers, core], send_sem=send_sem,
                                          recv_sem=recv_sem, device_id=(ix, iy, 1 - core), device_id_type=MESH)
        cp.start()
        pltpu.make_async_remote_copy(src_ref=src.at[layers, 1 - core], dst_ref=out.at[layers, 1 - core],
                                     send_sem=send_sem, recv_sem=recv_sem, device_id=(ix, iy, core),
                                     device_id_type=MESH).wait_recv()
        cp.wait_send()

    return pl.pallas_call(
        body, in_specs=[hbm], out_specs=hbm, out_shape=jax.ShapeDtypeStruct(buf.shape, buf.dtype),
        input_output_aliases={0: 0}, scratch_shapes=[pltpu.SemaphoreType.DMA, pltpu.SemaphoreType.DMA],
        name="rs_share", compiler_params=pltpu.CompilerParams(has_side_effects=True))(buf)


def _scatter_start(per_chip, chip, core, layer):
    own, pair = _rs_add_pair(per_chip, _rs_sibling(per_chip), chip, core)
    return own, _rs_chips(pair, SCATTER_IDS[layer])


def _allreduce_small(tag, buf, collective_id):
    rows = buf.shape[0]
    src = jax.new_ref(buf, memory_space=pltpu.MemorySpace.HBM)
    out = jax.empty_ref(jax.ShapeDtypeStruct((8,) + buf.shape, buf.dtype), memory_space=pltpu.MemorySpace.HBM)
    dma = pltpu.SemaphoreType.DMA
    flips = [(a, b, c) for a in (0, 1) for b in (0, 1) for c in (0, 1)][1:]

    @pl.kernel(mesh=plsc.ScalarSubcoreMesh(axis_name="seq", num_cores=1), name="%s_gather_%d" % (tag, collective_id),
               scratch_types=(dma,) * 14, compiler_params=pltpu.CompilerParams(collective_id=collective_id))
    def launch(*sems):
        ix, iy, core, _ = _mesh_pos()
        peers = [(1 - ix if a else ix, 1 - iy if b else iy, 1 - core if c else core) for a, b, c in flips]
        barrier = pltpu.get_barrier_semaphore()
        for peer in peers:
            pl.semaphore_signal(barrier, inc=1, device_id=peer, device_id_type=MESH)
        pl.semaphore_wait(barrier, 7)
        copies = []
        for j, peer in enumerate(peers):
            cp = pltpu.make_async_remote_copy(src_ref=src, dst_ref=out.at[4 * ix + 2 * iy + core], send_sem=sems[j],
                                              recv_sem=sems[7 + j], device_id=peer, device_id_type=MESH)
            cp.start()
            copies.append(cp)
        for j, (px, py, pc) in enumerate(peers):
            pltpu.make_async_remote_copy(src_ref=src, dst_ref=out.at[4 * px + 2 * py + pc], send_sem=sems[j],
                                         recv_sem=sems[7 + j], device_id=peers[j], device_id_type=MESH).wait_recv()
        for cp in copies:
            cp.wait_send()

    launch()
    ix, iy, core = lax.axis_index("x"), lax.axis_index("y"), lax.axis_index("c")
    eight = lax.dynamic_update_slice(out[...], buf[None], (4 * ix + 2 * iy + core, 0, 0))

    def body(x_ref, o_ref):
        acc = x_ref[0]
        for d in range(1, 8):
            acc = acc + x_ref[d]
        o_ref[...] = acc

    t = 512
    return pl.pallas_call(
        body, grid=(rows // t,), in_specs=[pl.BlockSpec((8, t, 128), lambda i: (0, i, 0))],
        out_specs=_rowspec(t, 128), out_shape=jax.ShapeDtypeStruct(buf.shape, buf.dtype),
        name=tag + "_sum", compiler_params=_params(("arbitrary",)))(eight)


IN_COL_SEGMENTS = ((0, 1024, 2048), (1024, 3072, -1024), (3072, 3080, GATE0 - 3072), (3080, D_IN, COL_CX - 3080))


def _orig_cols(w_pad, lo, hi):
    parts = []
    for first, last, shift in IN_COL_SEGMENTS:
        a, b = max(lo, first), min(hi, last)
        if a < b:
            parts.append(w_pad[..., a + shift:b + shift])
    return parts


def _block_diag(w):
    n, d = w.shape[0], w.shape[1]
    on_diag = jnp.arange(n)[:, None] == jnp.arange(n)[None, :]
    return jnp.where(on_diag[:, None, :, None], w[:, :, None, :], 0.0).reshape(n * d, n * d)


def _diag_blocks(w):
    return jnp.stack([w[64 * n:64 * (n + 1), 64 * n:64 * (n + 1)] for n in range(4)], axis=0)


def _gate_row(v):
    return jnp.concatenate([jnp.zeros((4,), F32), v, jnp.zeros((120,), F32)]).reshape(1, 128)


SMALL = ("norm1_g", "hgrn_lb_logits", "hgrn_norm_g", "gdn_conv_w", "gdn_a_log", "gdn_dt_bias", "gdn_norm_g",
         "lru_conv_w", "lru_conv_b", "lru_w_a", "lru_b_a", "lru_w_x", "lru_b_x", "lru_lambda", "norm2_g",
         "final_norm_g")


def _rows_of(shape):
    size = 1
    for d in shape:
        size *= d
    return size, -(-size // 1024) * 8


def _flatten_small(tree):
    parts = []
    for k in sorted(tree):
        size, nr = _rows_of(tree[k].shape)
        parts.append(jnp.pad(tree[k].reshape(-1), (0, nr * 128 - size)).reshape(nr, 128))
    used = sum(pt.shape[0] for pt in parts)
    parts.append(jnp.zeros((-(-used // 512) * 512 - used, 128), F32))
    return jnp.concatenate(parts, axis=0)


def _unflatten_small(buf, shapes):
    out, r0 = {}, 0
    for k in sorted(shapes):
        size, nr = _rows_of(shapes[k])
        out[k] = buf[r0:r0 + nr].reshape(-1)[:size].reshape(shapes[k])
        r0 += nr
    return out


def _layer_fwd(l, x, p, w_in, late):
    sv = {"x0": x}
    (h,) = _rowwise_fwd("norm1", _rmsnorm_fn, [p["norm1_g"]], [x], [D_MODEL], [BF16], ROWS_NARROW)
    proj, w_out, w_up, w_down = late(_matmul("proj_in", h, w_in, "nn"))
    sv["h"], sv["proj"] = h, proj
    aq, af, ag = _cols(proj, COL_AQ, A_W), _cols(proj, COL_AF, A_W), _cols(proj, COL_AG, A_W)
    ai = lax.slice_in_dim(proj, COL_AI, COL_AI + A_W, axis=1)
    bz, gate, cy = _cols(proj, COL_BZ, B_W), _cols(proj, GATE0, 128), _cols(proj, COL_CY, C_W)
    q, k, lf = _rowwise_fwd("hgrn_pre", functools.partial(_hgrn_pre_fn, l), [p["lb_logits"]], [aq, af],
                            [A_W] * 3, [F32] * 3, ROWS_NARROW)
    o_a, sts_a = _hgrn_fwd("hgrn_chunk", q, k, ai, lf)
    (mixed,) = _rowwise_fwd("hgrn_post", _hgrn_post_fn, [p["hgrn_norm_g"]], [o_a, ag], [A_W], [BF16], ROWS_NARROW,
                            into=(B_W, D_MODEL, None))
    sv.update(aq=aq, af=af, ai=ai, ag=ag, hq=q, hk=k, hlf=lf, o_a=o_a, sts_a=sts_a)
    qkvc = _conv_fwd("gdn_conv", proj, 0, 3 * B_W, p["gdn_conv_w"], jnp.zeros((1, 3 * B_W), F32), ROWS_WIDE)
    gq, gk, gv, gb, gla = _rowwise_fwd("gdn_pre", _gdn_pre_fn, [p["gdn_a_log"], p["gdn_dt_bias"]], [qkvc, gate],
                                       [B_W] * 5, [F32] * 5, ROWS_WIDE)
    *wy, sv["gdn_inv"] = _gdn_prep_fwd("gdn_prep", gq, gk, gv, gb, gla)
    o_b, sts_b = _gdn_rec_fwd("gdn_rec", *wy)
    sv["wy"] = wy
    (mixed,) = _rowwise_fwd("gdn_post", _gdn_post_fn, [p["gdn_norm_g"]], [o_b, bz], [B_W], [BF16], ROWS_NARROW,
                            into=(0, D_MODEL, mixed))
    sv.update(qkvc=qkvc, gate=gate, bz=bz, gq=gq, gk=gk, gv=gv, gb=gb, gla=gla, o_b=o_b, sts_b=sts_b)
    xc = _conv_fwd("lru_conv", proj, COL_CX, C_W, p["lru_conv_w"], p["lru_conv_b"], ROWS_NARROW)
    a, b = _rowwise_fwd("lru_pre", _lru_pre_fn, [p["lru_w_a"], p["lru_b_a"], p["lru_w_x"], p["lru_b_x"], p["lru_lambda"]],
                        [xc], [C_W] * 2, [F32] * 2, ROWS_NARROW)
    hs, h_prev = _lru_scan_fwd("lru_scan", a, b)
    (mixed,) = _rowwise_fwd("lru_post", _lru_post_fn, [], [hs, cy], [C_W], [BF16], ROWS_NARROW,
                            into=(B_W + A_W, D_MODEL, mixed))
    sv.update(xc=xc, la=a, hs=hs, h_prev=h_prev, cy=cy)
    x1 = _matmul("proj_out", mixed, w_out, "nn", res=x)
    (h2,) = _rowwise_fwd("norm2", _rmsnorm_fn, [p["norm2_g"]], [x1], [D_MODEL], [BF16], ROWS_NARROW)
    up, act = _matmul("mlp_up", h2, w_up, "nn", also=_relu2)
    x2 = _matmul("mlp_down", act, w_down, "nn", res=x1)
    sv.update(mixed=mixed, x1=x1, h2=h2, up=up, act=act)
    return x2, sv, (w_in, w_out, w_up, w_down)


def _layer_bwd(l, dx, dx16, p, sv, w_in, w_out, w_up, w_down):
    gs = {}
    dup = _matmul("mlp_down_dx", dx16, w_down, "nt", epi=_drelu2, epi_in=sv["up"], out_dtype=BF16)
    d_w_down = _matmul("mlp_down_dw", sv["act"], dx16, "tn")
    dh2 = _matmul("mlp_up_dx", dup, w_up, "nt")
    d_w_up = _matmul("mlp_up_dw", sv["h2"], dup, "tn")
    (gs["norm2_g"],), (dx1, dx1_16) = _rowwise_bwd("norm2_bwd", _rmsnorm_fn, [p["norm2_g"]], [sv["x1"]], [dh2], 512,
                                                   res=dx, copy16=True)
    dmixed = _matmul("proj_out_dx", dx1_16, w_out, "nt")
    d_w_out = _matmul("proj_out_dw", sv["mixed"], dx1_16, "tn")
    dy_b, dy_a, dy_c = _cols(dmixed, 0, B_W), _cols(dmixed, B_W, A_W), _cols(dmixed, B_W + A_W, C_W)
    (gs["hgrn_norm_g"],), (do_a, dproj) = _rowwise_bwd("hgrn_post_bwd", _hgrn_post_fn, [p["hgrn_norm_g"]],
                                                       [sv["o_a"], sv["ag"]], [dy_a], ROWS_NARROW, pack=([1], COL_AG, A_W, None))
    dq, dk, dproj, dlf = _hgrn_bwd("hgrn_chunk_bwd", sv["hq"], sv["hk"], sv["ai"], sv["hlf"], sv["sts_a"], do_a,
                                   dproj, COL_AI)
    (gs["lb_logits"],), (dproj,) = _rowwise_bwd("hgrn_pre_bwd", functools.partial(_hgrn_pre_fn, l), [p["lb_logits"]],
                                                [sv["aq"], sv["af"]], [dq, dk, dlf], ROWS_NARROW,
                                                pack=([0, 1], COL_AQ, 2 * A_W, dproj))
    (gs["gdn_norm_g"],), (do_b, dproj) = _rowwise_bwd("gdn_post_bwd", _gdn_post_fn, [p["gdn_norm_g"]],
                                                      [sv["o_b"], sv["bz"]], [dy_b], ROWS_NARROW, pack=([1], COL_BZ, B_W, dproj))
    d_wy = _gdn_rec_bwd("gdn_rec_bwd", *sv["wy"], sv["sts_b"], do_b)
    dgq, dgk, dgv, dgb, dgla = _gdn_prep_bwd("gdn_prep_bwd", sv["gq"], sv["gk"], sv["gv"], sv["gb"], sv["gla"],
                                             sv["gdn_inv"], d_wy)
    (gs["gdn_a_log"], gs["gdn_dt_bias"]), (dqkvc, dproj) = _rowwise_bwd(
        "gdn_pre_bwd", _gdn_pre_fn, [p["gdn_a_log"], p["gdn_dt_bias"]], [sv["qkvc"], sv["gate"]],
        [dgq, dgk, dgv, dgb, dgla], ROWS_WIDE, pack=([1], GATE0, D_IN_PAD - GATE0, dproj))
    dproj, gs["gdn_conv_w"], _ = _conv_bwd("gdn_conv_bwd", dqkvc, sv["proj"], 0, 3 * B_W, p["gdn_conv_w"], dproj, ROWS_WIDE)
    _, (dhs, dproj) = _rowwise_bwd("lru_post_bwd", _lru_post_fn, [], [sv["hs"], sv["cy"]], [dy_c], ROWS_NARROW,
                                   pack=([1], COL_CY, C_W, dproj))
    g = _lru_scan_bwd("lru_scan_bwd", sv["la"], dhs)
    lru_params = [p["lru_w_a"], p["lru_b_a"], p["lru_w_x"], p["lru_b_x"], p["lru_lambda"]]
    dps, (dxc,) = _rowwise_bwd("lru_pre_bwd", _lru_pre_fn, lru_params, [sv["xc"]], [g, sv["h_prev"]], ROWS_NARROW,
                               dout_fn=lambda gt, h_prev: (gt * h_prev, gt))
    gs["lru_w_a"], gs["lru_b_a"], gs["lru_w_x"], gs["lru_b_x"], gs["lru_lambda"] = dps
    dproj, gs["lru_conv_w"], gs["lru_conv_b"] = _conv_bwd("lru_conv_bwd", dxc, sv["proj"], COL_CX, C_W, p["lru_conv_w"],
                                                          dproj, ROWS_NARROW)
    dh = _matmul("proj_in_dx", dproj, w_in, "nt")
    d_w_in = _matmul("proj_in_dw", sv["h"], dproj, "tn")
    (gs["norm1_g"],), (dx0, dx0_16) = _rowwise_bwd("norm1_bwd", _rmsnorm_fn, [p["norm1_g"]], [sv["x0"]], [dh], 512,
                                                   res=dx1, copy16=True)
    return dx0, dx0_16, dict(w_in=d_w_in, w_out=d_w_out, w_up=d_w_up, w_down=d_w_down), gs


def _layer_params(l, lb_logits, hgrn_norm_g, gdn_conv_w, gdn_a_log, gdn_dt_bias, gdn_norm_g, lru_conv_w, lru_conv_b,
                  lru_w_a, lru_b_a, lru_w_x, lru_b_x, lru_lambda, norm1_g, norm2_g):
    row = lambda v: v.reshape(1, -1)
    return dict(
        norm1_g=row(norm1_g[l]), norm2_g=row(norm2_g[l]), lb_logits=lb_logits,
        hgrn_norm_g=row(jnp.tile(hgrn_norm_g[l], A_W // A_HD)),
        gdn_conv_w=gdn_conv_w[l], gdn_a_log=_gate_row(gdn_a_log[l]), gdn_dt_bias=_gate_row(gdn_dt_bias[l]),
        gdn_norm_g=row(jnp.tile(gdn_norm_g[l], B_NH)),
        lru_conv_w=lru_conv_w[l], lru_conv_b=row(lru_conv_b[l]), lru_w_a=_block_diag(lru_w_a[l]), lru_b_a=row(lru_b_a[l]),
        lru_w_x=_block_diag(lru_w_x[l]), lru_b_x=row(lru_b_x[l]), lru_lambda=row(lru_lambda[l]))


def _small_grads_to_reference(gs_layers, d_final_g, d_logits):
    st = lambda k, f=lambda z: z: jnp.stack([f(g[k]) for g in gs_layers], axis=0)
    vec = lambda z: z.reshape(-1)
    return dict(
        norm1_g=st("norm1_g", vec), norm2_g=st("norm2_g", vec), hgrn_lb_logits=d_logits,
        hgrn_norm_g=st("hgrn_norm_g", lambda z: z.reshape(A_W // A_HD, A_HD).sum(0)),
        gdn_conv_w=st("gdn_conv_w"), gdn_a_log=st("gdn_a_log", lambda z: z[0, 4:8]),
        gdn_dt_bias=st("gdn_dt_bias", lambda z: z[0, 4:8]),
        gdn_norm_g=st("gdn_norm_g", lambda z: z.reshape(B_NH, B_HD).sum(0)),
        lru_conv_w=st("lru_conv_w"), lru_conv_b=st("lru_conv_b", vec), lru_w_a=st("lru_w_a", _diag_blocks),
        lru_b_a=st("lru_b_a", vec), lru_w_x=st("lru_w_x", _diag_blocks), lru_b_x=st("lru_b_x", vec),
        lru_lambda=st("lru_lambda", vec), final_norm_g=d_final_g.reshape(-1))


def _local_step(x, tgt, small, layer_weights, on_layer_grads=None):
    layer_p = [_layer_params(l, small["hgrn_lb_logits"], small["hgrn_norm_g"], small["gdn_conv_w"], small["gdn_a_log"],
                             small["gdn_dt_bias"], small["gdn_norm_g"], small["lru_conv_w"], small["lru_conv_b"],
                             small["lru_w_a"], small["lru_b_a"], small["lru_w_x"], small["lru_b_x"], small["lru_lambda"],
                             small["norm1_g"], small["norm2_g"]) for l in range(DEPTH)]
    saved, weights = [], []
    for l in range(DEPTH):
        x, w_in, late = layer_weights(l, x)
        x, sv, w = _layer_fwd(l, x, layer_p[l], w_in, late)
        saved.append(sv)
        weights.append(w)
    loss, dx, d_final_g, dx16 = _loss_and_grad("final_loss", small["final_norm_g"].reshape(1, -1), x, tgt)
    big = [None] * DEPTH
    gs_layers = [None] * DEPTH
    d_logits = jnp.zeros_like(small["hgrn_lb_logits"])
    for l in reversed(range(DEPTH)):
        dx, dx16, big[l], gs_layers[l] = _layer_bwd(l, dx, dx16, layer_p[l], saved[l], *weights[l])
        d_logits = d_logits + gs_layers[l]["lb_logits"]
        if on_layer_grads is not None:
            big[l] = on_layer_grads(l, big[l])
    return loss, dx, big, _small_grads_to_reference(gs_layers, d_final_g, d_logits)


def kernel(x, norm1_g, w_in, hgrn_lb_logits, hgrn_norm_g, gdn_conv_w, gdn_a_log, gdn_dt_bias, gdn_norm_g, lru_conv_w, lru_conv_b, lru_w_a, lru_b_a, lru_w_x, lru_b_x, lru_lambda, w_out, norm2_g, w_up, w_down, final_norm_g, loss_target, m_norm1_g, m_w_in, m_hgrn_lb_logits, m_hgrn_norm_g, m_gdn_conv_w, m_gdn_a_log, m_gdn_dt_bias, m_gdn_norm_g, m_lru_conv_w, m_lru_conv_b, m_lru_w_a, m_lru_b_a, m_lru_w_x, m_lru_b_x, m_lru_lambda, m_w_out, m_norm2_g, m_w_up, m_w_down, m_final_norm_g, v_norm1_g, v_w_in, v_hgrn_lb_logits, v_hgrn_norm_g, v_gdn_conv_w, v_gdn_a_log, v_gdn_dt_bias, v_gdn_norm_g, v_lru_conv_w, v_lru_conv_b, v_lru_w_a, v_lru_b_a, v_lru_w_x, v_lru_b_x, v_lru_lambda, v_w_out, v_norm2_g, v_w_up, v_w_down, v_final_norm_g):
    args = dict(locals())
    ix, iy, core = lax.axis_index("x"), lax.axis_index("y"), lax.axis_index("c")
    chip = 2 * ix + iy

    packed = _pack_big(w_in, w_out, w_up, w_down).astype(BF16)
    first_w_in = _gather_layer(packed[0, :1024], FIRST_GATHER_ID)
    gathered = [_gather_layer(packed[l, 1024 * (l == 0):], GATHER_IDS[l]) for l in range(DEPTH)]

    def with_own(got, own):
        return lax.dynamic_update_slice(got, own[None], (chip, 0, 0))

    def full_w_in(shares):
        def cols(lo, hi):
            out = []
            for k in range(4):
                a, b = max(lo, k * n_in), min(hi, (k + 1) * n_in)
                if a < b:
                    out.append(shares[k, :1024, a - k * n_in:b - k * n_in])
            return out

        pieces = [pc for first, last, _ in sorted(IN_COL_SEGMENTS, key=lambda seg: seg[0] + seg[2])
                  for pc in cols(first, last)]
        return jnp.concatenate(pieces + [jnp.zeros((1024, D_IN_PAD - D_IN), shares.dtype)], axis=-1)

    def full_rest(shares):
        return (jnp.concatenate([shares[k, 2048:] for k in W_OUT_CHIP_ORDER], axis=0),
                jnp.concatenate([shares[k, :1024] for k in range(4)], axis=-1),
                jnp.concatenate([shares[k, 1024:2048] for k in range(4)], axis=0))

    def layer_weights(l, x):
        got, own = gathered[l], packed[l]
        if l:
            got, own, x = lax.optimization_barrier((got, own, x))
            shares = with_own(got, own)
            rest = full_rest(shares[:, 1024:])
            return x, full_w_in(shares), lambda v: (v,) + rest

        def late(v):
            got_v, own_v, v = lax.optimization_barrier((got, own, v))
            return (v,) + full_rest(with_own(got_v, own_v[1024:]))

        return x, full_w_in(with_own(first_w_in, own[:1024])), late

    n_gc, n_lc = gdn_conv_w.shape[-1], lru_conv_w.shape[-1]
    conv_full = dict(
        gdn_conv_w=lax.dynamic_update_slice(jnp.zeros((DEPTH, 4, 4 * n_gc), F32), gdn_conv_w * 0.5, (0, 0, chip * n_gc)),
        lru_conv_w=lax.dynamic_update_slice(jnp.zeros((DEPTH, 4, 4 * n_lc), F32), lru_conv_w * 0.5, (0, 0, chip * n_lc)))
    conv_shapes = {k: v.shape for k, v in conv_full.items()}
    conv_full = _unflatten_small(_allreduce_small("convw", _flatten_small(conv_full), CONV_SUM_ID), conv_shapes)

    small = {k: args[k] for k in SMALL}
    small.update(conv_full)

    n_in = D_IN // 4

    def start_scatter(l, b):
        pieces = []
        for k in range(4):
            cols = jnp.concatenate(_orig_cols(b["w_in"], k * n_in, (k + 1) * n_in), axis=1)
            r0 = 256 * W_OUT_CHIP_ORDER.index(k)
            pieces += [jnp.pad(cols, ((0, 0), (0, 1024 - n_in))), b["w_up"][:, k * 1024:(k + 1) * 1024],
                       b["w_down"][k * 1024:(k + 1) * 1024], b["w_out"][r0:r0 + 256]]
        per_chip = jnp.concatenate(pieces, axis=0).reshape(4, LAYER_ROWS, 1024)
        return _scatter_start(per_chip, chip, core, l)

    loss, grad_x, started, sg = _local_step(x[0], loss_target[0], small, layer_weights, start_scatter)

    sg["loss"] = loss[0, :1]
    shapes = {k: v.shape for k, v in sg.items()}
    sg = _unflatten_small(_allreduce_small("smallgrad", _flatten_small(sg), GRAD_SUM_ID), shapes)
    loss_out = sg.pop("loss")[0]

    buf = None
    for l in reversed(range(DEPTH)):
        own, got = started[l]
        buf = _rs_add_chips(own, got, core, l, buf)
    g_in, g_out, g_up, g_down = _unpack_big(_rs_share(buf).reshape(DEPTH, LAYER_ROWS, 1024))

    grads = dict(sg)
    grads["gdn_conv_w"] = lax.dynamic_slice_in_dim(sg["gdn_conv_w"], chip * n_gc, n_gc, axis=2)
    grads["lru_conv_w"] = lax.dynamic_slice_in_dim(sg["lru_conv_w"], chip * n_lc, n_lc, axis=2)
    grads.update(w_in=g_in, w_out=g_out, w_up=g_up, w_down=g_down)

    names = ['norm1_g', 'w_in', 'hgrn_lb_logits', 'hgrn_norm_g', 'gdn_conv_w', 'gdn_a_log', 'gdn_dt_bias', 'gdn_norm_g',
             'lru_conv_w', 'lru_conv_b', 'lru_w_a', 'lru_b_a', 'lru_w_x', 'lru_b_x', 'lru_lambda', 'w_out', 'norm2_g',
             'w_up', 'w_down', 'final_norm_g']
    big_names = ("w_in", "w_out", "w_up", "w_down")
    delta, new_m, new_v = {}, {}, {}
    for k in big_names:
        delta[k], new_m[k], new_v[k] = _adam("adam_" + k, args[k], grads[k], args["m_" + k], args["v_" + k])
    small_names = [k for k in names if k not in big_names]
    shapes = {k: args[k].shape for k in small_names}
    flat = [_flatten_small({k: src(k) for k in small_names})
            for src in (lambda k: args[k], lambda k: grads[k], lambda k: args["m_" + k], lambda k: args["v_" + k])]
    outs = _ew("adam_small", _adam_fn, flat, 3)
    for dst, o in zip((delta, new_m, new_v), outs):
        dst.update(_unflatten_small(o, shapes))
    return (loss_out, grad_x[None], *[grads[k] for k in names], *[delta[k] for k in names],
            *[new_m[k] for k in names], *[new_v[k] for k in names])
```
